```python
import jax, jax.numpy as jnp
from jax import lax
import numpy as np

D_MODEL = 1024
BATCH = 8
SEQ = 8192
DEPTH = 1

CHUNK = 64
A_HEADS = D_MODEL // 128
A_HEAD_DIM = 64
A_LEFT_CHUNKS = 8
A_BAND = A_LEFT_CHUNKS + 1
A_MAX_REL = 128
B_HEADS = D_MODEL // 128
B_Q_LORA = D_MODEL // 4
B_KV_LORA = D_MODEL // 8
B_NOPE = 64
B_ROPE = 32
B_V_DIM = 64
B_QK_DIM = B_NOPE + B_ROPE
ROPE_THETA = 10000.0
Q_BLOCK = 128
A_WIDTH = A_HEADS * A_HEAD_DIM
B_WIDTH = B_HEADS * B_V_DIM
D_MIX = A_WIDTH + B_WIDTH
IN_COLS = 3 * A_WIDTH + B_Q_LORA + B_KV_LORA + B_ROPE
D_FF = ((8 * D_MODEL // 3 + 127) // 128) * 128
EPS = 1e-6
NEG_INF = -1e30

kernel_name = "hymba_chunked_relbias_mla_macaron"


def rmsnorm(x, g):
    xf = x.astype(jnp.float32)
    y = xf * lax.rsqrt(jnp.mean(xf * xf, axis=-1, keepdims=True) + EPS)
    return (y * g.astype(jnp.float32)).astype(x.dtype)


def swiglu(h, w_gate, w_up, w_down):
    return (jax.nn.silu(h @ w_gate) * (h @ w_up)) @ w_down


def rope_tables(seq, dim):
    inv = 1.0 / (ROPE_THETA ** (jnp.arange(0, dim, 2, dtype=jnp.float32) / dim))
    ang = jnp.arange(seq, dtype=jnp.float32)[:, None] * inv[None, :]
    return jnp.cos(ang), jnp.sin(ang)


def apply_rope(x, cos, sin):
    x1, x2 = jnp.split(x, 2, axis=-1)
    c = cos[None, :, None, :].astype(x.dtype)
    s = sin[None, :, None, :].astype(x.dtype)
    return jnp.concatenate([x1 * c - x2 * s, x1 * s + x2 * c], axis=-1)


def chunked_relbias_attention(q, k, v, rel_bias):
    b, s, h, dh = q.shape
    nc = s // CHUNK
    band_len = A_BAND * CHUNK
    pad = ((0, 0), (A_LEFT_CHUNKS * CHUNK, 0), (0, 0), (0, 0))
    kp = jnp.pad(k, pad)
    vp = jnp.pad(v, pad)
    qc = q.reshape(b, nc, CHUNK, h, dh).transpose(1, 0, 2, 3, 4)
    qi = jnp.arange(CHUNK)
    kj = jnp.arange(band_len) - A_LEFT_CHUNKS * CHUNK
    rel = jnp.clip(qi[:, None] - kj[None, :], -A_MAX_REL, A_MAX_REL) + A_MAX_REL
    bias = rel_bias.astype(jnp.float32)[:, rel]
    scale = dh ** -0.5

    def one_chunk(args):
        q_blk, c = args
        kb = lax.dynamic_slice_in_dim(kp, c * CHUNK, band_len, axis=1)
        vb = lax.dynamic_slice_in_dim(vp, c * CHUNK, band_len, axis=1)
        sc = jnp.einsum('bihd,bjhd->bhij', q_blk, kb).astype(jnp.float32) * scale
        sc = sc + bias[None]
        valid = jnp.arange(band_len) >= (A_LEFT_CHUNKS - c) * CHUNK
        sc = jnp.where(valid[None, None, None, :], sc, NEG_INF)
        p = jax.nn.softmax(sc, axis=-1).astype(vb.dtype)
        return jnp.einsum('bhij,bjhd->bihd', p, vb)

    o = lax.map(one_chunk, (qc, jnp.arange(nc, dtype=jnp.int32)))
    return o.transpose(1, 0, 2, 3, 4).reshape(b, s, h * dh)


def block_causal_attention(q, k, v):
    b, s, h, dq = q.shape
    dv = v.shape[-1]
    nb = s // Q_BLOCK
    qb = q.reshape(b, nb, Q_BLOCK, h, dq).transpose(1, 0, 2, 3, 4)
    k_chunk = jnp.arange(s) // CHUNK
    scale = dq ** -0.5

    def one_block(args):
        q_blk, i = args
        q_chunk = (i * Q_BLOCK + jnp.arange(Q_BLOCK)) // CHUNK
        sc = jnp.einsum('bqhd,bkhd->bhqk', q_blk, k).astype(jnp.float32) * scale
        mask = k_chunk[None, :] <= q_chunk[:, None]
        sc = jnp.where(mask[None, None], sc, NEG_INF)
        p = jax.nn.softmax(sc, axis=-1).astype(v.dtype)
        return jnp.einsum('bhqk,bkhd->bqhd', p, v)

    o = lax.map(one_block, (qb, jnp.arange(nb, dtype=jnp.int32)))
    return o.transpose(1, 0, 2, 3, 4).reshape(b, s, h * dv)


def mla_attention(c_q, c_kv, k_rope, q_lat_norm, w_uq, kv_lat_norm, w_ukv,
                  q_nope_norm, q_rope_norm, k_nope_norm, k_rope_norm, cos, sin):
    b, s, _ = c_q.shape
    q = (rmsnorm(c_q, q_lat_norm) @ w_uq).reshape(b, s, B_HEADS, B_QK_DIM)
    kv = (rmsnorm(c_kv, kv_lat_norm) @ w_ukv).reshape(b, s, B_HEADS, B_NOPE + B_V_DIM)
    q_nope = rmsnorm(q[..., :B_NOPE], q_nope_norm)
    q_pe = apply_rope(rmsnorm(q[..., B_NOPE:], q_rope_norm), cos, sin)
    k_nope = rmsnorm(kv[..., :B_NOPE], k_nope_norm)
    v = kv[..., B_NOPE:]
    k_pe = apply_rope(rmsnorm(k_rope, k_rope_norm)[:, :, None, :], cos, sin)
    q_full = jnp.concatenate([q_nope, q_pe], axis=-1)
    k_full = jnp.concatenate([k_nope, jnp.broadcast_to(k_pe, (b, s, B_HEADS, B_ROPE))], axis=-1)
    return block_causal_attention(q_full, k_full, v)


def _fwd_setup_inputs(seed: int = 0) -> dict:
    key = jax.random.key(seed)
    ks = iter(jax.random.split(key, 32))
    f32 = jnp.float32

    def w(shape, fan_in):
        return jax.random.normal(next(ks), (DEPTH,) + shape, f32) * fan_in ** -0.5

    def gain(n):
        return 1.0 + 0.05 * jax.random.normal(next(ks), (DEPTH, n), f32)

    x = jax.random.normal(next(ks), (BATCH, SEQ, D_MODEL), f32)
    return {
        "x": x,
        "ffn1_norm": gain(D_MODEL),
        "ffn1_w_gate": w((D_MODEL, D_FF), D_MODEL),
        "ffn1_w_up": w((D_MODEL, D_FF), D_MODEL),
        "ffn1_w_down": w((D_FF, D_MODEL), D_FF),
        "mix_norm": gain(D_MODEL),
        "w_in": w((D_MODEL, IN_COLS), D_MODEL),
        "a_q_norm": gain(A_HEAD_DIM),
        "a_k_norm": gain(A_HEAD_DIM),
        "a_rel_bias": 0.5 * jax.random.normal(next(ks), (DEPTH, A_HEADS, 2 * A_MAX_REL + 1), f32),
        "b_q_lat_norm": gain(B_Q_LORA),
        "b_w_uq": w((B_Q_LORA, B_HEADS * B_QK_DIM), B_Q_LORA),
        "b_kv_lat_norm": gain(B_KV_LORA),
        "b_w_ukv": w((B_KV_LORA, B_HEADS * (B_NOPE + B_V_DIM)), B_KV_LORA),
        "b_q_nope_norm": gain(B_NOPE),
        "b_q_rope_norm": gain(B_ROPE),
        "b_k_nope_norm": gain(B_NOPE),
        "b_k_rope_norm": gain(B_ROPE),
        "w_out": w((D_MIX, D_MODEL), D_MIX),
        "ffn2_norm": gain(D_MODEL),
        "ffn2_w_gate": w((D_MODEL, D_FF), D_MODEL),
        "ffn2_w_up": w((D_MODEL, D_FF), D_MODEL),
        "ffn2_w_down": w((D_FF, D_MODEL), D_FF),
        "final_norm": gain(D_MODEL),
    }


def _fwd_reference(x, ffn1_norm, ffn1_w_gate, ffn1_w_up, ffn1_w_down, mix_norm, w_in,
              a_q_norm, a_k_norm, a_rel_bias, b_q_lat_norm, b_w_uq, b_kv_lat_norm,
              b_w_ukv, b_q_nope_norm, b_q_rope_norm, b_k_nope_norm, b_k_rope_norm,
              w_out, ffn2_norm, ffn2_w_gate, ffn2_w_up, ffn2_w_down, final_norm):
    b, s, _ = x.shape
    cos, sin = rope_tables(s, B_ROPE)
    o_qa, o_ka, o_va = 0, A_WIDTH, 2 * A_WIDTH
    o_cq = 3 * A_WIDTH
    o_ckv = o_cq + B_Q_LORA
    o_kr = o_ckv + B_KV_LORA
    for l in range(DEPTH):
        x = x + 0.5 * swiglu(rmsnorm(x, ffn1_norm[l]), ffn1_w_gate[l], ffn1_w_up[l], ffn1_w_down[l])
        proj = rmsnorm(x, mix_norm[l]) @ w_in[l]
        qa = rmsnorm(proj[..., o_qa:o_ka].reshape(b, s, A_HEADS, A_HEAD_DIM), a_q_norm[l])
        ka = rmsnorm(proj[..., o_ka:o_va].reshape(b, s, A_HEADS, A_HEAD_DIM), a_k_norm[l])
        va = proj[..., o_va:o_cq].reshape(b, s, A_HEADS, A_HEAD_DIM)
        out_a = chunked_relbias_attention(qa, ka, va, a_rel_bias[l])
        out_b = mla_attention(proj[..., o_cq:o_ckv], proj[..., o_ckv:o_kr], proj[..., o_kr:],
                              b_q_lat_norm[l], b_w_uq[l], b_kv_lat_norm[l], b_w_ukv[l],
                              b_q_nope_norm[l], b_q_rope_norm[l], b_k_nope_norm[l],
                              b_k_rope_norm[l], cos, sin)
        x = x + jnp.concatenate([out_a, out_b], axis=-1) @ w_out[l]
        x = x + 0.5 * swiglu(rmsnorm(x, ffn2_norm[l]), ffn2_w_gate[l], ffn2_w_up[l], ffn2_w_down[l])
        x = rmsnorm(x, final_norm[l])
    return x


import jax as _jax
import jax.numpy as _jnp

TWIN_FORMAT = 'train_step'
FWD_PARAMS = ['x', 'ffn1_norm', 'ffn1_w_gate', 'ffn1_w_up', 'ffn1_w_down', 'mix_norm', 'w_in', 'a_q_norm', 'a_k_norm', 'a_rel_bias', 'b_q_lat_norm', 'b_w_uq', 'b_kv_lat_norm', 'b_w_ukv', 'b_q_nope_norm', 'b_q_rope_norm', 'b_k_nope_norm', 'b_k_rope_norm', 'w_out', 'ffn2_norm', 'ffn2_w_gate', 'ffn2_w_up', 'ffn2_w_down', 'final_norm']
TWIN_WEIGHTS = ['ffn1_norm', 'ffn1_w_gate', 'ffn1_w_up', 'ffn1_w_down', 'mix_norm', 'w_in', 'a_q_norm', 'a_k_norm', 'a_rel_bias', 'b_q_lat_norm', 'b_w_uq', 'b_kv_lat_norm', 'b_w_ukv', 'b_q_nope_norm', 'b_q_rope_norm', 'b_k_nope_norm', 'b_k_rope_norm', 'w_out', 'ffn2_norm', 'ffn2_w_gate', 'ffn2_w_up', 'ffn2_w_down', 'final_norm']
TWIN_DIFF_INPUT = 'x'
TWIN_INPUTS = ['x', 'ffn1_norm', 'ffn1_w_gate', 'ffn1_w_up', 'ffn1_w_down', 'mix_norm', 'w_in', 'a_q_norm', 'a_k_norm', 'a_rel_bias', 'b_q_lat_norm', 'b_w_uq', 'b_kv_lat_norm', 'b_w_ukv', 'b_q_nope_norm', 'b_q_rope_norm', 'b_k_nope_norm', 'b_k_rope_norm', 'w_out', 'ffn2_norm', 'ffn2_w_gate', 'ffn2_w_up', 'ffn2_w_down', 'final_norm', 'loss_target', 'm_ffn1_norm', 'm_ffn1_w_gate', 'm_ffn1_w_up', 'm_ffn1_w_down', 'm_mix_norm', 'm_w_in', 'm_a_q_norm', 'm_a_k_norm', 'm_a_rel_bias', 'm_b_q_lat_norm', 'm_b_w_uq', 'm_b_kv_lat_norm', 'm_b_w_ukv', 'm_b_q_nope_norm', 'm_b_q_rope_norm', 'm_b_k_nope_norm', 'm_b_k_rope_norm', 'm_w_out', 'm_ffn2_norm', 'm_ffn2_w_gate', 'm_ffn2_w_up', 'm_ffn2_w_down', 'm_final_norm', 'v_ffn1_norm', 'v_ffn1_w_gate', 'v_ffn1_w_up', 'v_ffn1_w_down', 'v_mix_norm', 'v_w_in', 'v_a_q_norm', 'v_a_k_norm', 'v_a_rel_bias', 'v_b_q_lat_norm', 'v_b_w_uq', 'v_b_kv_lat_norm', 'v_b_w_ukv', 'v_b_q_nope_norm', 'v_b_q_rope_norm', 'v_b_k_nope_norm', 'v_b_k_rope_norm', 'v_w_out', 'v_ffn2_norm', 'v_ffn2_w_gate', 'v_ffn2_w_up', 'v_ffn2_w_down', 'v_final_norm']
TWIN_OUTPUTS = ['loss', 'grad_x', 'grad_ffn1_norm', 'grad_ffn1_w_gate', 'grad_ffn1_w_up', 'grad_ffn1_w_down', 'grad_mix_norm', 'grad_w_in', 'grad_a_q_norm', 'grad_a_k_norm', 'grad_a_rel_bias', 'grad_b_q_lat_norm', 'grad_b_w_uq', 'grad_b_kv_lat_norm', 'grad_b_w_ukv', 'grad_b_q_nope_norm', 'grad_b_q_rope_norm', 'grad_b_k_nope_norm', 'grad_b_k_rope_norm', 'grad_w_out', 'grad_ffn2_norm', 'grad_ffn2_w_gate', 'grad_ffn2_w_up', 'grad_ffn2_w_down', 'grad_final_norm', 'delta_ffn1_norm', 'delta_ffn1_w_gate', 'delta_ffn1_w_up', 'delta_ffn1_w_down', 'delta_mix_norm', 'delta_w_in', 'delta_a_q_norm', 'delta_a_k_norm', 'delta_a_rel_bias', 'delta_b_q_lat_norm', 'delta_b_w_uq', 'delta_b_kv_lat_norm', 'delta_b_w_ukv', 'delta_b_q_nope_norm', 'delta_b_q_rope_norm', 'delta_b_k_nope_norm', 'delta_b_k_rope_norm', 'delta_w_out', 'delta_ffn2_norm', 'delta_ffn2_w_gate', 'delta_ffn2_w_up', 'delta_ffn2_w_down', 'delta_final_norm', 'new_m_ffn1_norm', 'new_m_ffn1_w_gate', 'new_m_ffn1_w_up', 'new_m_ffn1_w_down', 'new_m_mix_norm', 'new_m_w_in', 'new_m_a_q_norm', 'new_m_a_k_norm', 'new_m_a_rel_bias', 'new_m_b_q_lat_norm', 'new_m_b_w_uq', 'new_m_b_kv_lat_norm', 'new_m_b_w_ukv', 'new_m_b_q_nope_norm', 'new_m_b_q_rope_norm', 'new_m_b_k_nope_norm', 'new_m_b_k_rope_norm', 'new_m_w_out', 'new_m_ffn2_norm', 'new_m_ffn2_w_gate', 'new_m_ffn2_w_up', 'new_m_ffn2_w_down', 'new_m_final_norm', 'new_v_ffn1_norm', 'new_v_ffn1_w_gate', 'new_v_ffn1_w_up', 'new_v_ffn1_w_down', 'new_v_mix_norm', 'new_v_w_in', 'new_v_a_q_norm', 'new_v_a_k_norm', 'new_v_a_rel_bias', 'new_v_b_q_lat_norm', 'new_v_b_w_uq', 'new_v_b_kv_lat_norm', 'new_v_b_w_ukv', 'new_v_b_q_nope_norm', 'new_v_b_q_rope_norm', 'new_v_b_k_nope_norm', 'new_v_b_k_rope_norm', 'new_v_w_out', 'new_v_ffn2_norm', 'new_v_ffn2_w_gate', 'new_v_ffn2_w_up', 'new_v_ffn2_w_down', 'new_v_final_norm']
TWIN_LEAF_KINDS = {'loss': 'loss', 'grad_x': 'grad_x', 'grad_ffn1_norm': 'grad_w', 'grad_ffn1_w_gate': 'grad_w', 'grad_ffn1_w_up': 'grad_w', 'grad_ffn1_w_down': 'grad_w', 'grad_mix_norm': 'grad_w', 'grad_w_in': 'grad_w', 'grad_a_q_norm': 'grad_w', 'grad_a_k_norm': 'grad_w', 'grad_a_rel_bias': 'grad_w', 'grad_b_q_lat_norm': 'grad_w', 'grad_b_w_uq': 'grad_w', 'grad_b_kv_lat_norm': 'grad_w', 'grad_b_w_ukv': 'grad_w', 'grad_b_q_nope_norm': 'grad_w', 'grad_b_q_rope_norm': 'grad_w', 'grad_b_k_nope_norm': 'grad_w', 'grad_b_k_rope_norm': 'grad_w', 'grad_w_out': 'grad_w', 'grad_ffn2_norm': 'grad_w', 'grad_ffn2_w_gate': 'grad_w', 'grad_ffn2_w_up': 'grad_w', 'grad_ffn2_w_down': 'grad_w', 'grad_final_norm': 'grad_w', 'delta_ffn1_norm': 'delta_w', 'delta_ffn1_w_gate': 'delta_w', 'delta_ffn1_w_up': 'delta_w', 'delta_ffn1_w_down': 'delta_w', 'delta_mix_norm': 'delta_w', 'delta_w_in': 'delta_w', 'delta_a_q_norm': 'delta_w', 'delta_a_k_norm': 'delta_w', 'delta_a_rel_bias': 'delta_w', 'delta_b_q_lat_norm': 'delta_w', 'delta_b_w_uq': 'delta_w', 'delta_b_kv_lat_norm': 'delta_w', 'delta_b_w_ukv': 'delta_w', 'delta_b_q_nope_norm': 'delta_w', 'delta_b_q_rope_norm': 'delta_w', 'delta_b_k_nope_norm': 'delta_w', 'delta_b_k_rope_norm': 'delta_w', 'delta_w_out': 'delta_w', 'delta_ffn2_norm': 'delta_w', 'delta_ffn2_w_gate': 'delta_w', 'delta_ffn2_w_up': 'delta_w', 'delta_ffn2_w_down': 'delta_w', 'delta_final_norm': 'delta_w', 'new_m_ffn1_norm': 'new_m', 'new_m_ffn1_w_gate': 'new_m', 'new_m_ffn1_w_up': 'new_m', 'new_m_ffn1_w_down': 'new_m', 'new_m_mix_norm': 'new_m', 'new_m_w_in': 'new_m', 'new_m_a_q_norm': 'new_m', 'new_m_a_k_norm': 'new_m', 'new_m_a_rel_bias': 'new_m', 'new_m_b_q_lat_norm': 'new_m', 'new_m_b_w_uq': 'new_m', 'new_m_b_kv_lat_norm': 'new_m', 'new_m_b_w_ukv': 'new_m', 'new_m_b_q_nope_norm': 'new_m', 'new_m_b_q_rope_norm': 'new_m', 'new_m_b_k_nope_norm': 'new_m', 'new_m_b_k_rope_norm': 'new_m', 'new_m_w_out': 'new_m', 'new_m_ffn2_norm': 'new_m', 'new_m_ffn2_w_gate': 'new_m', 'new_m_ffn2_w_up': 'new_m', 'new_m_ffn2_w_down': 'new_m', 'new_m_final_norm': 'new_m', 'new_v_ffn1_norm': 'new_v', 'new_v_ffn1_w_gate': 'new_v', 'new_v_ffn1_w_up': 'new_v', 'new_v_ffn1_w_down': 'new_v', 'new_v_mix_norm': 'new_v', 'new_v_w_in': 'new_v', 'new_v_a_q_norm': 'new_v', 'new_v_a_k_norm': 'new_v', 'new_v_a_rel_bias': 'new_v', 'new_v_b_q_lat_norm': 'new_v', 'new_v_b_w_uq': 'new_v', 'new_v_b_kv_lat_norm': 'new_v', 'new_v_b_w_ukv': 'new_v', 'new_v_b_q_nope_norm': 'new_v', 'new_v_b_q_rope_norm': 'new_v', 'new_v_b_k_nope_norm': 'new_v', 'new_v_b_k_rope_norm': 'new_v', 'new_v_w_out': 'new_v', 'new_v_ffn2_norm': 'new_v', 'new_v_ffn2_w_gate': 'new_v', 'new_v_ffn2_w_up': 'new_v', 'new_v_ffn2_w_down': 'new_v', 'new_v_final_norm': 'new_v'}


def _forward(args):
    return _fwd_reference(*[args[k] for k in FWD_PARAMS])


def _output_shape():
    def fwd():
        inp = _fwd_setup_inputs(0)
        return _fwd_reference(*[inp[k] for k in FWD_PARAMS])
    out = _jax.eval_shape(fwd)
    return out.shape, out.dtype

N_MICROBATCH = 1
ADAM_LR = 0.001
ADAM_B1 = 0.9
ADAM_B2 = 0.999
ADAM_EPS = 1e-08
ADAM_WD = 0.01
ADAM_STEP = 10
PER_EXAMPLE_BATCH_AXIS = {'x': 0, 'loss_target': 0}
SHARED_INPUTS = []
_WEIGHT_DTYPES = {'ffn1_norm': _jnp.float32, 'ffn1_w_gate': _jnp.float32, 'ffn1_w_up': _jnp.float32, 'ffn1_w_down': _jnp.float32, 'mix_norm': _jnp.float32, 'w_in': _jnp.float32, 'a_q_norm': _jnp.float32, 'a_k_norm': _jnp.float32, 'a_rel_bias': _jnp.float32, 'b_q_lat_norm': _jnp.float32, 'b_w_uq': _jnp.float32, 'b_kv_lat_norm': _jnp.float32, 'b_w_ukv': _jnp.float32, 'b_q_nope_norm': _jnp.float32, 'b_q_rope_norm': _jnp.float32, 'b_k_nope_norm': _jnp.float32, 'b_k_rope_norm': _jnp.float32, 'w_out': _jnp.float32, 'ffn2_norm': _jnp.float32, 'ffn2_w_gate': _jnp.float32, 'ffn2_w_up': _jnp.float32, 'ffn2_w_down': _jnp.float32, 'final_norm': _jnp.float32}
MOMENT_SCALE = {'ffn1_norm': 1.138293e-01, 'ffn1_w_gate': 4.734419e-02, 'ffn1_w_up': 4.586433e-02, 'ffn1_w_down': 7.605269e-02, 'mix_norm': 6.167643e-02, 'w_in': 4.652780e-02, 'a_q_norm': 9.756638e-02, 'a_k_norm': 9.707076e-02, 'a_rel_bias': 1.695632e-02, 'b_q_lat_norm': 5.495009e-02, 'b_w_uq': 3.170419e-02, 'b_kv_lat_norm': 1.090711e-01, 'b_w_ukv': 3.663577e-02, 'b_q_nope_norm': 8.992546e-02, 'b_q_rope_norm': 8.710442e-02, 'b_k_nope_norm': 9.197126e-02, 'b_k_rope_norm': 6.834253e-02, 'w_out': 3.953525e-02, 'ffn2_norm': 9.667103e-02, 'ffn2_w_gate': 4.310646e-02, 'ffn2_w_up': 4.221335e-02, 'ffn2_w_down': 7.030615e-02, 'final_norm': 6.394289e+01}


def _to_microbatches(a, axis):
    t = _jnp.moveaxis(a, axis, 0)
    t = t.reshape((N_MICROBATCH, t.shape[0] // N_MICROBATCH) + t.shape[1:])
    return _jnp.moveaxis(t, 1, axis + 1)


def setup_inputs(seed: int = 0) -> dict:
    inp = _fwd_setup_inputs(seed)
    key = _jax.random.fold_in(_jax.random.key(seed), 7919)
    shape, _ = _output_shape()
    out = dict(inp)
    out["loss_target"] = _jax.random.normal(_jax.random.fold_in(key, 0), shape, _jnp.float32)
    for i, name in enumerate(TWIN_WEIGHTS):
        w = inp[name].astype(_jnp.float32)
        if MOMENT_SCALE is None:
            s = _jnp.sqrt(_jnp.mean(_jnp.square(w)) + 1e-30)
        else:
            s = MOMENT_SCALE[name]
        km, kv = _jax.random.split(_jax.random.fold_in(key, i + 1))
        out[name] = w
        out["m_" + name] = s * _jax.random.normal(km, w.shape, _jnp.float32)
        out["v_" + name] = (s * s) * _jax.random.uniform(kv, w.shape, _jnp.float32, 0.5, 1.5)
    if N_MICROBATCH > 1:
        for name, axis in PER_EXAMPLE_BATCH_AXIS.items():
            out[name] = _to_microbatches(out[name], axis)
    return {'x': out['x'], 'ffn1_norm': out['ffn1_norm'], 'ffn1_w_gate': out['ffn1_w_gate'], 'ffn1_w_up': out['ffn1_w_up'], 'ffn1_w_down': out['ffn1_w_down'], 'mix_norm': out['mix_norm'], 'w_in': out['w_in'], 'a_q_norm': out['a_q_norm'], 'a_k_norm': out['a_k_norm'], 'a_rel_bias': out['a_rel_bias'], 'b_q_lat_norm': out['b_q_lat_norm'], 'b_w_uq': out['b_w_uq'], 'b_kv_lat_norm': out['b_kv_lat_norm'], 'b_w_ukv': out['b_w_ukv'], 'b_q_nope_norm': out['b_q_nope_norm'], 'b_q_rope_norm': out['b_q_rope_norm'], 'b_k_nope_norm': out['b_k_nope_norm'], 'b_k_rope_norm': out['b_k_rope_norm'], 'w_out': out['w_out'], 'ffn2_norm': out['ffn2_norm'], 'ffn2_w_gate': out['ffn2_w_gate'], 'ffn2_w_up': out['ffn2_w_up'], 'ffn2_w_down': out['ffn2_w_down'], 'final_norm': out['final_norm'], 'loss_target': out['loss_target'], 'm_ffn1_norm': out['m_ffn1_norm'], 'm_ffn1_w_gate': out['m_ffn1_w_gate'], 'm_ffn1_w_up': out['m_ffn1_w_up'], 'm_ffn1_w_down': out['m_ffn1_w_down'], 'm_mix_norm': out['m_mix_norm'], 'm_w_in': out['m_w_in'], 'm_a_q_norm': out['m_a_q_norm'], 'm_a_k_norm': out['m_a_k_norm'], 'm_a_rel_bias': out['m_a_rel_bias'], 'm_b_q_lat_norm': out['m_b_q_lat_norm'], 'm_b_w_uq': out['m_b_w_uq'], 'm_b_kv_lat_norm': out['m_b_kv_lat_norm'], 'm_b_w_ukv': out['m_b_w_ukv'], 'm_b_q_nope_norm': out['m_b_q_nope_norm'], 'm_b_q_rope_norm': out['m_b_q_rope_norm'], 'm_b_k_nope_norm': out['m_b_k_nope_norm'], 'm_b_k_rope_norm': out['m_b_k_rope_norm'], 'm_w_out': out['m_w_out'], 'm_ffn2_norm': out['m_ffn2_norm'], 'm_ffn2_w_gate': out['m_ffn2_w_gate'], 'm_ffn2_w_up': out['m_ffn2_w_up'], 'm_ffn2_w_down': out['m_ffn2_w_down'], 'm_final_norm': out['m_final_norm'], 'v_ffn1_norm': out['v_ffn1_norm'], 'v_ffn1_w_gate': out['v_ffn1_w_gate'], 'v_ffn1_w_up': out['v_ffn1_w_up'], 'v_ffn1_w_down': out['v_ffn1_w_down'], 'v_mix_norm': out['v_mix_norm'], 'v_w_in': out['v_w_in'], 'v_a_q_norm': out['v_a_q_norm'], 'v_a_k_norm': out['v_a_k_norm'], 'v_a_rel_bias': out['v_a_rel_bias'], 'v_b_q_lat_norm': out['v_b_q_lat_norm'], 'v_b_w_uq': out['v_b_w_uq'], 'v_b_kv_lat_norm': out['v_b_kv_lat_norm'], 'v_b_w_ukv': out['v_b_w_ukv'], 'v_b_q_nope_norm': out['v_b_q_nope_norm'], 'v_b_q_rope_norm': out['v_b_q_rope_norm'], 'v_b_k_nope_norm': out['v_b_k_nope_norm'], 'v_b_k_rope_norm': out['v_b_k_rope_norm'], 'v_w_out': out['v_w_out'], 'v_ffn2_norm': out['v_ffn2_norm'], 'v_ffn2_w_gate': out['v_ffn2_w_gate'], 'v_ffn2_w_up': out['v_ffn2_w_up'], 'v_ffn2_w_down': out['v_ffn2_w_down'], 'v_final_norm': out['v_final_norm']}


def _loss(weights, diff, rest, loss_target):
    with _jax.named_scope("forward"):
        args = {**rest, TWIN_DIFF_INPUT: diff, **{k: w.astype(_WEIGHT_DTYPES[k]) for k, w in weights.items()}}
        y = _forward(args)
    with _jax.named_scope("loss_head"):
        err = _jnp.square(y.astype(_jnp.float32) - loss_target)
        return 0.5 * _jnp.sum(_jnp.mean(err, axis=-1)) if err.ndim else 0.5 * err


def _adamw(w, g, m, v):
    m = ADAM_B1 * m + (1.0 - ADAM_B1) * g
    v = ADAM_B2 * v + (1.0 - ADAM_B2) * _jnp.square(g)
    m_hat = m / (1.0 - ADAM_B1 ** ADAM_STEP)
    v_hat = v / (1.0 - ADAM_B2 ** ADAM_STEP)
    delta = -ADAM_LR * (m_hat / (_jnp.sqrt(v_hat) + ADAM_EPS) + ADAM_WD * w)
    return delta, m, v


def reference(x, ffn1_norm, ffn1_w_gate, ffn1_w_up, ffn1_w_down, mix_norm, w_in, a_q_norm, a_k_norm, a_rel_bias, b_q_lat_norm, b_w_uq, b_kv_lat_norm, b_w_ukv, b_q_nope_norm, b_q_rope_norm, b_k_nope_norm, b_k_rope_norm, w_out, ffn2_norm, ffn2_w_gate, ffn2_w_up, ffn2_w_down, final_norm, loss_target, m_ffn1_norm, m_ffn1_w_gate, m_ffn1_w_up, m_ffn1_w_down, m_mix_norm, m_w_in, m_a_q_norm, m_a_k_norm, m_a_rel_bias, m_b_q_lat_norm, m_b_w_uq, m_b_kv_lat_norm, m_b_w_ukv, m_b_q_nope_norm, m_b_q_rope_norm, m_b_k_nope_norm, m_b_k_rope_norm, m_w_out, m_ffn2_norm, m_ffn2_w_gate, m_ffn2_w_up, m_ffn2_w_down, m_final_norm, v_ffn1_norm, v_ffn1_w_gate, v_ffn1_w_up, v_ffn1_w_down, v_mix_norm, v_w_in, v_a_q_norm, v_a_k_norm, v_a_rel_bias, v_b_q_lat_norm, v_b_w_uq, v_b_kv_lat_norm, v_b_w_ukv, v_b_q_nope_norm, v_b_q_rope_norm, v_b_k_nope_norm, v_b_k_rope_norm, v_w_out, v_ffn2_norm, v_ffn2_w_gate, v_ffn2_w_up, v_ffn2_w_down, v_final_norm):
    given = dict(x=x, ffn1_norm=ffn1_norm, ffn1_w_gate=ffn1_w_gate, ffn1_w_up=ffn1_w_up, ffn1_w_down=ffn1_w_down, mix_norm=mix_norm, w_in=w_in, a_q_norm=a_q_norm, a_k_norm=a_k_norm, a_rel_bias=a_rel_bias, b_q_lat_norm=b_q_lat_norm, b_w_uq=b_w_uq, b_kv_lat_norm=b_kv_lat_norm, b_w_ukv=b_w_ukv, b_q_nope_norm=b_q_nope_norm, b_q_rope_norm=b_q_rope_norm, b_k_nope_norm=b_k_nope_norm, b_k_rope_norm=b_k_rope_norm, w_out=w_out, ffn2_norm=ffn2_norm, ffn2_w_gate=ffn2_w_gate, ffn2_w_up=ffn2_w_up, ffn2_w_down=ffn2_w_down, final_norm=final_norm, loss_target=loss_target, m_ffn1_norm=m_ffn1_norm, m_ffn1_w_gate=m_ffn1_w_gate, m_ffn1_w_up=m_ffn1_w_up, m_ffn1_w_down=m_ffn1_w_down, m_mix_norm=m_mix_norm, m_w_in=m_w_in, m_a_q_norm=m_a_q_norm, m_a_k_norm=m_a_k_norm, m_a_rel_bias=m_a_rel_bias, m_b_q_lat_norm=m_b_q_lat_norm, m_b_w_uq=m_b_w_uq, m_b_kv_lat_norm=m_b_kv_lat_norm, m_b_w_ukv=m_b_w_ukv, m_b_q_nope_norm=m_b_q_nope_norm, m_b_q_rope_norm=m_b_q_rope_norm, m_b_k_nope_norm=m_b_k_nope_norm, m_b_k_rope_norm=m_b_k_rope_norm, m_w_out=m_w_out, m_ffn2_norm=m_ffn2_norm, m_ffn2_w_gate=m_ffn2_w_gate, m_ffn2_w_up=m_ffn2_w_up, m_ffn2_w_down=m_ffn2_w_down, m_final_norm=m_final_norm, v_ffn1_norm=v_ffn1_norm, v_ffn1_w_gate=v_ffn1_w_gate, v_ffn1_w_up=v_ffn1_w_up, v_ffn1_w_down=v_ffn1_w_down, v_mix_norm=v_mix_norm, v_w_in=v_w_in, v_a_q_norm=v_a_q_norm, v_a_k_norm=v_a_k_norm, v_a_rel_bias=v_a_rel_bias, v_b_q_lat_norm=v_b_q_lat_norm, v_b_w_uq=v_b_w_uq, v_b_kv_lat_norm=v_b_kv_lat_norm, v_b_w_ukv=v_b_w_ukv, v_b_q_nope_norm=v_b_q_nope_norm, v_b_q_rope_norm=v_b_q_rope_norm, v_b_k_nope_norm=v_b_k_nope_norm, v_b_k_rope_norm=v_b_k_rope_norm, v_w_out=v_w_out, v_ffn2_norm=v_ffn2_norm, v_ffn2_w_gate=v_ffn2_w_gate, v_ffn2_w_up=v_ffn2_w_up, v_ffn2_w_down=v_ffn2_w_down, v_final_norm=v_final_norm)
    weights = {n: given[n] for n in TWIN_WEIGHTS}
    shared = {n: given[n] for n in SHARED_INPUTS}
    per_example = {n: given[n] for n in ['x']}
    grad_fn = _jax.value_and_grad(_loss, argnums=(0, 1))

    def one_microbatch(ex, loss_target):
        ex = dict(ex)
        diff = ex.pop(TWIN_DIFF_INPUT)
        return grad_fn(weights, diff, {**shared, **ex}, loss_target)

    if N_MICROBATCH == 1:
        loss, (grad_w, grad_x) = one_microbatch(per_example, given["loss_target"])
    else:
        def body(carry, xs):
            loss_sum, grad_sum = carry
            l_k, (gw_k, gx_k) = one_microbatch(xs[0], xs[1])
            with _jax.named_scope("update"):
                return (loss_sum + l_k, _jax.tree.map(_jnp.add, grad_sum, gw_k)), gx_k

        init = (_jnp.zeros((), _jnp.float32), _jax.tree.map(_jnp.zeros_like, weights))
        (loss, grad_w), grad_x = _jax.lax.scan(body, init, (per_example, given["loss_target"]))
    with _jax.named_scope("update"):
        delta_w, new_m, new_v = {}, {}, {}
        for n in TWIN_WEIGHTS:
            delta_w[n], new_m[n], new_v[n] = _adamw(weights[n], grad_w[n], given["m_" + n], given["v_" + n])
    return (loss, grad_x, *[grad_w[n] for n in TWIN_WEIGHTS], *[delta_w[n] for n in TWIN_WEIGHTS],
            *[new_m[n] for n in TWIN_WEIGHTS], *[new_v[n] for n in TWIN_WEIGHTS])
```

```python
import functools
import math

import numpy as np
import jax
import jax.numpy as jnp
from jax import lax
from jax.experimental import pallas as pl
from jax.experimental.pallas import tpu as pltpu

F32 = jnp.float32
BF16 = jnp.bfloat16
EPS = 1e-6
NEG = -1e30

D_MODEL = 1024
D_FF = 2816
N_SHARD = 4
FS = D_FF // N_SHARD
CHUNK = 64
A_LEFT = 8
A_MAX_REL = 128
HEADS = 8
HD = 64
ROPE = 32
PROJ_W = 2048
IN_COLS = 1952
B_SCALE = 96 ** -0.5
LANES = 128

ADAM_LR = 0.001
ADAM_B1 = 0.9
ADAM_B2 = 0.999
ADAM_EPS = 1e-08
ADAM_WD = 0.01
ADAM_STEP = 10

VMEM_LIMIT = 56 * 1024 * 1024

MESH = pl.DeviceIdType.MESH


def _dot(a, b):
    return lax.dot_general(a, b, (((1,), (0,)), ((), ())), preferred_element_type=F32)


def _dot_nt(a, b):
    return lax.dot_general(a, b, (((1,), (1,)), ((), ())), preferred_element_type=F32)


def _dot_tn(a, b):
    return lax.dot_general(a, b, (((0,), (0,)), ((), ())), preferred_element_type=F32)


def _params(sem):
    return pltpu.CompilerParams(dimension_semantics=sem, vmem_limit_bytes=VMEM_LIMIT)


def _rms(xv):
    r = lax.rsqrt(jnp.mean(xv * xv, axis=-1, keepdims=True) + EPS)
    return r, xv * r


def ffn_fwd(x, g, wg, wu, wd, name):
    t, d = x.shape
    tm = 512

    def body(x_ref, g_ref, wg_ref, wu_ref, wd_ref, o_ref, h_ref, acc_ref):
        j = pl.program_id(1)

        @pl.when(j == 0)
        def _():
            _, xn = _rms(x_ref[...])
            h_ref[...] = (xn * g_ref[...]).astype(BF16)
            acc_ref[...] = jnp.zeros_like(acc_ref)

        h = h_ref[...]
        gp = _dot(h, wg_ref[0])
        up = _dot(h, wu_ref[0])
        a = (gp * jax.nn.sigmoid(gp) * up).astype(BF16)
        acc_ref[...] += _dot(a, wd_ref[0])

        @pl.when(j == N_SHARD - 1)
        def _():
            o_ref[...] = x_ref[...] + 0.5 * acc_ref[...]

    return pl.pallas_call(
        body, name=name, grid=(t // tm, N_SHARD),
        in_specs=[pl.BlockSpec((tm, d), lambda i, j: (i, 0)),
                  pl.BlockSpec((1, d), lambda i, j: (0, 0)),
                  pl.BlockSpec((1, d, FS), lambda i, j: (j, 0, 0)),
                  pl.BlockSpec((1, d, FS), lambda i, j: (j, 0, 0)),
                  pl.BlockSpec((1, FS, d), lambda i, j: (j, 0, 0))],
        out_specs=pl.BlockSpec((tm, d), lambda i, j: (i, 0)),
        out_shape=jax.ShapeDtypeStruct((t, d), F32),
        scratch_shapes=[pltpu.VMEM((tm, d), BF16), pltpu.VMEM((tm, d), F32)],
        compiler_params=_params(("parallel", "arbitrary")),
    )(x, g, wg, wu, wd)


def ffn_bwd(x, dout, g, wg, wu, wd, name):
    t, d = x.shape
    tm = 512

    def body(x_ref, do_ref, g_ref, wg_ref, wu_ref, wd_ref, dwg_ref, dwu_ref, dwd_ref, dhp_ref):
        i = pl.program_id(1)
        _, xn = _rms(x_ref[...])
        h = (xn * g_ref[...]).astype(BF16)
        dz = (0.5 * do_ref[...]).astype(BF16)
        wgv, wuv, wdv = wg_ref[0], wu_ref[0], wd_ref[0]
        gp = _dot(h, wgv)
        up = _dot(h, wuv)
        s = jax.nn.sigmoid(gp)
        sg = gp * s
        a = (sg * up).astype(BF16)
        da = _dot_nt(dz, wdv)
        dup = (da * sg).astype(BF16)
        dgp = (da * up * (s * (1.0 + gp * (1.0 - s)))).astype(BF16)

        @pl.when(i == 0)
        def _():
            dwg_ref[...] = jnp.zeros_like(dwg_ref)
            dwu_ref[...] = jnp.zeros_like(dwu_ref)
            dwd_ref[...] = jnp.zeros_like(dwd_ref)

        dwd_ref[0] += _dot_tn(a, dz)
        dwg_ref[0] += _dot_tn(h, dgp)
        dwu_ref[0] += _dot_tn(h, dup)
        dhp_ref[0] = _dot_nt(dgp, wgv) + _dot_nt(dup, wuv)

    return pl.pallas_call(
        body, name=name, grid=(N_SHARD, t // tm),
        in_specs=[pl.BlockSpec((tm, d), lambda j, i: (i, 0)),
                  pl.BlockSpec((tm, d), lambda j, i: (i, 0)),
                  pl.BlockSpec((1, d), lambda j, i: (0, 0)),
                  pl.BlockSpec((1, d, FS), lambda j, i: (j, 0, 0)),
                  pl.BlockSpec((1, d, FS), lambda j, i: (j, 0, 0)),
                  pl.BlockSpec((1, FS, d), lambda j, i: (j, 0, 0))],
        out_specs=[pl.BlockSpec((1, d, FS), lambda j, i: (j, 0, 0)),
                   pl.BlockSpec((1, d, FS), lambda j, i: (j, 0, 0)),
                   pl.BlockSpec((1, FS, d), lambda j, i: (j, 0, 0)),
                   pl.BlockSpec((1, tm, d), lambda j, i: (j, i, 0))],
        out_shape=[jax.ShapeDtypeStruct((N_SHARD, d, FS), F32),
                   jax.ShapeDtypeStruct((N_SHARD, d, FS), F32),
                   jax.ShapeDtypeStruct((N_SHARD, FS, d), F32),
                   jax.ShapeDtypeStruct((N_SHARD, t, d), F32)],
        compiler_params=_params(("arbitrary", "arbitrary")),
    )(x, dout, g, wg, wu, wd)


def norm_bwd(x, g, dhp, dres, name):
    t, d = x.shape
    p = dhp.shape[0]
    tm = 512

    def body(x_ref, g_ref, dhp_ref, dres_ref, dx_ref, dg_ref):
        i = pl.program_id(0)
        r, xn = _rms(x_ref[...])
        dh = dhp_ref[0]
        for q in range(1, p):
            dh = dh + dhp_ref[q]
        dhg = dh * g_ref[...]
        dx_ref[...] = dres_ref[...] + r * (dhg - xn * jnp.mean(dhg * xn, axis=-1, keepdims=True))

        @pl.when(i == 0)
        def _():
            dg_ref[...] = jnp.zeros_like(dg_ref)

        dg_ref[...] += jnp.sum(dh * xn, axis=0, keepdims=True)

    return pl.pallas_call(
        body, name=name, grid=(t // tm,),
        in_specs=[pl.BlockSpec((tm, d), lambda i: (i, 0)),
                  pl.BlockSpec((1, d), lambda i: (0, 0)),
                  pl.BlockSpec((p, tm, d), lambda i: (0, i, 0)),
                  pl.BlockSpec((tm, d), lambda i: (i, 0))],
        out_specs=[pl.BlockSpec((tm, d), lambda i: (i, 0)),
                   pl.BlockSpec((1, d), lambda i: (0, 0))],
        out_shape=[jax.ShapeDtypeStruct((t, d), F32), jax.ShapeDtypeStruct((1, d), F32)],
        compiler_params=_params(("arbitrary",)),
    )(x, g, dhp, dres)


def final_loss(x, g, target):
    t, d = x.shape
    tm = 512

    def body(x_ref, g_ref, t_ref, dx_ref, dg_ref, loss_ref):
        i = pl.program_id(0)
        r, xn = _rms(x_ref[...])
        gv = g_ref[...]
        e = xn * gv - t_ref[...]
        dy = e * (1.0 / d)
        dhg = dy * gv
        dx_ref[...] = r * (dhg - xn * jnp.mean(dhg * xn, axis=-1, keepdims=True))

        @pl.when(i == 0)
        def _():
            dg_ref[...] = jnp.zeros_like(dg_ref)
            loss_ref[...] = jnp.zeros_like(loss_ref)

        dg_ref[...] += jnp.sum(dy * xn, axis=0, keepdims=True)
        part = jnp.sum(jnp.sum(e * e, axis=-1, keepdims=True), axis=0, keepdims=True) * (0.5 / d)
        loss_ref[...] += jnp.broadcast_to(part, loss_ref.shape)

    return pl.pallas_call(
        body, name="final_loss", grid=(t // tm,),
        in_specs=[pl.BlockSpec((tm, d), lambda i: (i, 0)),
                  pl.BlockSpec((1, d), lambda i: (0, 0)),
                  pl.BlockSpec((tm, d), lambda i: (i, 0))],
        out_specs=[pl.BlockSpec((tm, d), lambda i: (i, 0)),
                   pl.BlockSpec((1, d), lambda i: (0, 0)),
                   pl.BlockSpec((1, LANES), lambda i: (0, 0))],
        out_shape=[jax.ShapeDtypeStruct((t, d), F32), jax.ShapeDtypeStruct((1, d), F32),
                   jax.ShapeDtypeStruct((1, LANES), F32)],
        compiler_params=_params(("arbitrary",)),
    )(x, g, target)


def mix_proj(x, g, w):
    t, d = x.shape
    n = w.shape[1]
    tm = 512

    def body(x_ref, g_ref, w_ref, h_ref, p_ref):
        _, xn = _rms(x_ref[...])
        h = (xn * g_ref[...]).astype(BF16)
        h_ref[...] = h
        p_ref[...] = _dot(h, w_ref[...])

    return pl.pallas_call(
        body, name="mix_proj", grid=(t // tm,),
        in_specs=[pl.BlockSpec((tm, d), lambda i: (i, 0)),
                  pl.BlockSpec((1, d), lambda i: (0, 0)),
                  pl.BlockSpec((d, n), lambda i: (0, 0))],
        out_specs=[pl.BlockSpec((tm, d), lambda i: (i, 0)),
                   pl.BlockSpec((tm, n), lambda i: (i, 0))],
        out_shape=[jax.ShapeDtypeStruct((t, d), BF16), jax.ShapeDtypeStruct((t, n), F32)],
        compiler_params=_params(("parallel",)),
    )(x, g, w)


def matmul(a, b, mode, name, tm, tn, tk, out_dtype=F32):
    if mode == "nn":
        (m, k), n = a.shape, b.shape[1]
        a_spec = pl.BlockSpec((tm, tk), lambda i, j, q: (i, q))
        b_spec = pl.BlockSpec((tk, tn), lambda i, j, q: (q, j))
        dot = _dot
    elif mode == "nt":
        (m, k), n = a.shape, b.shape[0]
        a_spec = pl.BlockSpec((tm, tk), lambda i, j, q: (i, q))
        b_spec = pl.BlockSpec((tn, tk), lambda i, j, q: (j, q))
        dot = _dot_nt
    else:
        (k, m), n = a.shape, b.shape[1]
        a_spec = pl.BlockSpec((tk, tm), lambda i, j, q: (q, i))
        b_spec = pl.BlockSpec((tk, tn), lambda i, j, q: (q, j))
        dot = _dot_tn
    assert m % tm == 0 and n % tn == 0 and k % tk == 0, (m, n, k, tm, tn, tk)
    nk = k // tk

    def body(a_ref, b_ref, o_ref, acc_ref):
        q = pl.program_id(2)

        @pl.when(q == 0)
        def _():
            acc_ref[...] = jnp.zeros_like(acc_ref)

        acc_ref[...] += dot(a_ref[...].astype(BF16), b_ref[...].astype(BF16))

        @pl.when(q == nk - 1)
        def _():
            o_ref[...] = acc_ref[...].astype(out_dtype)

    return pl.pallas_call(
        body, name=name, grid=(m // tm, n // tn, nk),
        in_specs=[a_spec, b_spec],
        out_specs=pl.BlockSpec((tm, tn), lambda i, j, q: (i, j)),
        out_shape=jax.ShapeDtypeStruct((m, n), out_dtype),
        scratch_shapes=[pltpu.VMEM((tm, tn), F32)],
        compiler_params=_params(("parallel", "parallel", "arbitrary")),
    )(a, b)


def out_proj(x, oa, ob, w):
    t, d = x.shape
    half = oa.shape[1]
    tm = 512

    def body(x_ref, oa_ref, ob_ref, w_ref, o_ref):
        o_ref[...] = (x_ref[...] + _dot(oa_ref[...], w_ref[0:half, :])
                      + _dot(ob_ref[...], w_ref[half:2 * half, :]))

    return pl.pallas_call(
        body, name="out_proj", grid=(t // tm,),
        in_specs=[pl.BlockSpec((tm, d), lambda i: (i, 0)),
                  pl.BlockSpec((tm, half), lambda i: (i, 0)),
                  pl.BlockSpec((tm, half), lambda i: (i, 0)),
                  pl.BlockSpec((2 * half, d), lambda i: (0, 0))],
        out_specs=pl.BlockSpec((tm, d), lambda i: (i, 0)),
        out_shape=jax.ShapeDtypeStruct((t, d), F32),
        compiler_params=_params(("parallel",)),
    )(x, oa, ob, w)


def out_proj_bwd(dx, w):
    t, d = dx.shape
    half = w.shape[0] // 2
    tm = 512

    def body(dx_ref, w_ref, da_ref, db_ref):
        dxb = dx_ref[...].astype(BF16)
        da_ref[...] = _dot_nt(dxb, w_ref[0:half, :]).astype(BF16)
        db_ref[...] = _dot_nt(dxb, w_ref[half:2 * half, :]).astype(BF16)

    return pl.pallas_call(
        body, name="out_proj_bwd", grid=(t // tm,),
        in_specs=[pl.BlockSpec((tm, d), lambda i: (i, 0)),
                  pl.BlockSpec((2 * half, d), lambda i: (0, 0))],
        out_specs=[pl.BlockSpec((tm, half), lambda i: (i, 0)),
                   pl.BlockSpec((tm, half), lambda i: (i, 0))],
        out_shape=[jax.ShapeDtypeStruct((t, half), BF16), jax.ShapeDtypeStruct((t, half), BF16)],
        compiler_params=_params(("parallel",)),
    )(dx, w)


def _lane(shape):
    return lax.broadcasted_iota(jnp.int32, shape, 1)


def _seg_sum(z, mask):
    return jnp.sum(jnp.where(mask, z, 0.0), axis=-1, keepdims=True)


def _pair_norm(x):
    lo = _lane(x.shape) < HD
    x2 = x * x
    r = jnp.where(lo, lax.rsqrt(_seg_sum(x2, lo) * (1.0 / HD) + EPS),
                  lax.rsqrt(_seg_sum(x2, ~lo) * (1.0 / HD) + EPS))
    return lo, r, x * r


def _pair_norm_bwd(lo, r, xn, dyg):
    z = dyg * xn
    mean = jnp.where(lo, _seg_sum(z, lo), _seg_sum(z, ~lo)) * (1.0 / HD)
    return r * (dyg - xn * mean)


A_TM = 256


def prep1_fwd(proj, gq, gk, gcq, gckv):
    t = proj.shape[0]
    tm = A_TM

    def body(p_ref, gq_ref, gk_ref, gcq_ref, gckv_ref, qa_ref, ka_ref, va_ref, cq_ref, ckv_ref):
        for p in range(4):
            sl = slice(LANES * p, LANES * (p + 1))
            _, _, xn = _pair_norm(p_ref[:, sl])
            qa_ref[:, sl] = (xn * gq_ref[:, sl] * 0.125).astype(BF16)
            _, _, xn = _pair_norm(p_ref[:, 512 + LANES * p:512 + LANES * (p + 1)])
            ka_ref[:, sl] = (xn * gk_ref[:, sl]).astype(BF16)
        va_ref[...] = p_ref[:, 1024:1536].astype(BF16)
        _, xn = _rms(p_ref[:, 1536:1792])
        cq_ref[...] = (xn * gcq_ref[...]).astype(BF16)
        _, xn = _rms(p_ref[:, 1792:1920])
        ckv_ref[...] = (xn * gckv_ref[...]).astype(BF16)

    row = lambda w: pl.BlockSpec((tm, w), lambda i: (i, 0))
    vec = lambda w: pl.BlockSpec((1, w), lambda i: (0, 0))
    return pl.pallas_call(
        body, name="prep1_fwd", grid=(t // tm,),
        in_specs=[row(PROJ_W), vec(512), vec(512), vec(256), vec(128)],
        out_specs=[row(512), row(512), row(512), row(256), row(128)],
        out_shape=[jax.ShapeDtypeStruct((t, w), BF16) for w in (512, 512, 512, 256, 128)],
        compiler_params=_params(("parallel",)),
    )(proj, gq, gk, gcq, gckv)


def prep1_bwd(proj, dqa, dkp, dvp, dcq, dckv, dkr, gq, gk, gcq, gckv):
    t = proj.shape[0]
    tm = A_TM
    nb = t // tm

    def body(p_ref, dqa_ref, dk0_ref, dk1_ref, dk2_ref, dv0_ref, dv1_ref, dv2_ref, dcq_ref, dckv_ref, dkr_ref,
             gq_ref, gk_ref, gcq_ref, gckv_ref, dp_ref, dgq_ref, dgk_ref, dgcq_ref, dgckv_ref):
        i = pl.program_id(0)

        @pl.when(i == 0)
        def _():
            dgq_ref[...] = jnp.zeros_like(dgq_ref)
            dgk_ref[...] = jnp.zeros_like(dgk_ref)
            dgcq_ref[...] = jnp.zeros_like(dgcq_ref)
            dgckv_ref[...] = jnp.zeros_like(dgckv_ref)

        has1 = (i + 1 < nb).astype(F32)
        has2 = (i + 2 < nb).astype(F32)
        for p in range(4):
            sl = slice(LANES * p, LANES * (p + 1))
            lo, r, xn = _pair_norm(p_ref[:, sl])
            dy = dqa_ref[:, sl] * 0.125
            dp_ref[:, sl] = _pair_norm_bwd(lo, r, xn, dy * gq_ref[:, sl]).astype(BF16)
            dgq_ref[:, sl] += jnp.sum(dy * xn, axis=0, keepdims=True)
            ks = slice(512 + LANES * p, 512 + LANES * (p + 1))
            lo, r, xn = _pair_norm(p_ref[:, ks])
            dy = dk0_ref[0, :, sl] + has1 * dk1_ref[0, :, sl] + has2 * dk2_ref[0, :, sl]
            dp_ref[:, ks] = _pair_norm_bwd(lo, r, xn, dy * gk_ref[:, sl]).astype(BF16)
            dgk_ref[:, sl] += jnp.sum(dy * xn, axis=0, keepdims=True)
        dp_ref[:, 1024:1536] = (dv0_ref[0] + has1 * dv1_ref[0] + has2 * dv2_ref[0]).astype(BF16)
        for (a, b, d_ref, g_ref, dg_ref) in ((1536, 1792, dcq_ref, gcq_ref, dgcq_ref),
                                             (1792, 1920, dckv_ref, gckv_ref, dgckv_ref)):
            r, xn = _rms(p_ref[:, a:b])
            dy = d_ref[...]
            dyg = dy * g_ref[...]
            dp_ref[:, a:b] = (r * (dyg - xn * jnp.mean(dyg * xn, axis=-1, keepdims=True))).astype(BF16)
            dg_ref[...] += jnp.sum(dy * xn, axis=0, keepdims=True)
        dp_ref[:, 1920:2048] = dkr_ref[...].astype(BF16)

    row = lambda w: pl.BlockSpec((tm, w), lambda i: (i, 0))
    vec = lambda w: pl.BlockSpec((1, w), lambda i: (0, 0))
    part = lambda s: pl.BlockSpec((1, tm, 512), lambda i: (s, jnp.minimum(i + s, nb - 1), 0))
    return pl.pallas_call(
        body, name="prep1_bwd", grid=(nb,),
        in_specs=[row(PROJ_W), row(512), part(0), part(1), part(2), part(0), part(1), part(2),
                  row(256), row(128), row(128), vec(512), vec(512), vec(256), vec(128)],
        out_specs=[row(PROJ_W), vec(512), vec(512), vec(256), vec(128)],
        out_shape=[jax.ShapeDtypeStruct((t, PROJ_W), BF16)] + [jax.ShapeDtypeStruct((1, w), F32) for w in (512, 512, 256, 128)],
        compiler_params=_params(("arbitrary",)),
    )(proj, dqa, dkp, dkp, dkp, dvp, dvp, dvp, dcq, dckv, dkr, gq, gk, gcq, gckv)


def _roll(x, shift):
    return pltpu.roll(x, shift % LANES, 1)


def _rope(y, c, s1, s2):
    return y * c + _roll(y, -16) * s1 + _roll(y, 16) * s2


def _rope_bwd(d, c, s1, s2):
    return d * c + _roll(d * s1, 16) + _roll(d * s2, -16)


def _q_head_stats(x):
    lane = _lane(x.shape)
    mn = lane < HD
    mr = (lane >= HD) & (lane < HD + ROPE)
    x2 = x * x
    r = jnp.where(mn, lax.rsqrt(_seg_sum(x2, mn) * (1.0 / HD) + EPS),
                  lax.rsqrt(_seg_sum(x2, mr) * (1.0 / ROPE) + EPS))
    return mn, mr, r, x * r


def _kr_stats(x):
    r = lax.rsqrt(jnp.sum(x * x, axis=-1, keepdims=True) * (1.0 / ROPE) + EPS)
    return r, x * r


def prep2_fwd(qlat, kv, proj, gq, gk, gkr, tabs):
    t = qlat.shape[0]
    tm = A_TM

    def body(q_ref, kv_ref, kr_ref, gq_ref, gk_ref, gkr_ref, tab_ref, qf_ref, kf_ref, vp_ref):
        _, xn = _kr_stats(kr_ref[...])
        kpe = _roll(_rope(xn * gkr_ref[...], tab_ref[3], tab_ref[4], tab_ref[5]), 64)
        for h in range(HEADS):
            sl = slice(LANES * h, LANES * (h + 1))
            _, _, _, xn = _q_head_stats(q_ref[:, sl])
            qf_ref[:, sl] = _rope(xn * gq_ref[...], tab_ref[0], tab_ref[1], tab_ref[2]).astype(BF16)
            x = kv_ref[:, sl]
            lo = _lane(x.shape) < HD
            xk = jnp.where(lo, x, 0.0)
            rk = lax.rsqrt(jnp.sum(xk * xk, axis=-1, keepdims=True) * (1.0 / HD) + EPS)
            kf_ref[:, sl] = (xk * rk * gk_ref[...] + kpe).astype(BF16)
            if h % 2 == 0:
                v_even = _roll(x, 64)
            else:
                vp_ref[:, LANES * (h // 2):LANES * (h // 2 + 1)] = jnp.where(lo, v_even, x).astype(BF16)

    row = lambda w: pl.BlockSpec((tm, w), lambda i: (i, 0))
    vec = lambda w: pl.BlockSpec((1, w), lambda i: (0, 0))
    return pl.pallas_call(
        body, name="prep2_fwd", grid=(t // tm,),
        in_specs=[row(1024), row(1024), pl.BlockSpec((tm, LANES), lambda i: (i, 15)), vec(128), vec(128), vec(128),
                  pl.BlockSpec((6, tm, LANES), lambda i: (0, i, 0))],
        out_specs=[row(1024), row(1024), row(512)],
        out_shape=[jax.ShapeDtypeStruct((t, 1024), BF16), jax.ShapeDtypeStruct((t, 1024), BF16),
                   jax.ShapeDtypeStruct((t, 512), BF16)],
        compiler_params=_params(("parallel",)),
    )(qlat, kv, proj, gq, gk, gkr, tabs)


def prep2_bwd(qlat, kv, proj, dqf, dkf, dvp, gq, gk, gkr, tabs):
    t = qlat.shape[0]
    tm = A_TM

    def body(q_ref, kv_ref, kr_ref, dqf_ref, dkf_ref, dvp_ref, gq_ref, gk_ref, gkr_ref, tab_ref,
             dq_ref, dkv_ref, dkr_ref, dgq_ref, dgk_ref, dgkr_ref):
        i = pl.program_id(0)

        @pl.when(i == 0)
        def _():
            dgq_ref[...] = jnp.zeros_like(dgq_ref)
            dgk_ref[...] = jnp.zeros_like(dgk_ref)
            dgkr_ref[...] = jnp.zeros_like(dgkr_ref)

        dgq = jnp.zeros((1, LANES), F32)
        dgk = jnp.zeros((1, LANES), F32)
        dkpe = jnp.zeros((tm, LANES), F32)
        for h in range(HEADS):
            sl = slice(LANES * h, LANES * (h + 1))
            mn, mr, r, xn = _q_head_stats(q_ref[:, sl])
            dy = _rope_bwd(dqf_ref[:, sl], tab_ref[0], tab_ref[1], tab_ref[2])
            dyg = dy * gq_ref[...]
            z = dyg * xn
            mean = jnp.where(mn, _seg_sum(z, mn) * (1.0 / HD), _seg_sum(z, mr) * (1.0 / ROPE))
            dq_ref[:, sl] = (r * (dyg - xn * mean)).astype(BF16)
            dgq = dgq + jnp.sum(dy * xn, axis=0, keepdims=True)

            x = kv_ref[:, sl]
            dk = dkf_ref[:, sl]
            xk = jnp.where(mn, x, 0.0)
            rk = lax.rsqrt(jnp.sum(xk * xk, axis=-1, keepdims=True) * (1.0 / HD) + EPS)
            xkn = xk * rk
            dyk = jnp.where(mn, dk, 0.0)
            dykg = dyk * gk_ref[...]
            dxk = rk * (dykg - xkn * (jnp.sum(dykg * xkn, axis=-1, keepdims=True) * (1.0 / HD)))
            dgk = dgk + jnp.sum(dyk * xkn, axis=0, keepdims=True)
            dkpe = dkpe + jnp.where(mr, dk, 0.0)
            dvpair = dvp_ref[:, LANES * (h // 2):LANES * (h // 2 + 1)]
            dv = _roll(dvpair, 64) if h % 2 == 0 else dvpair
            dkv_ref[:, sl] = jnp.where(mn, dxk, dv).astype(BF16)

        r, xn = _kr_stats(kr_ref[...])
        dy = _rope_bwd(_roll(dkpe, 64), tab_ref[3], tab_ref[4], tab_ref[5])
        dyg = dy * gkr_ref[...]
        dkr_ref[...] = r * (dyg - xn * (jnp.sum(dyg * xn, axis=-1, keepdims=True) * (1.0 / ROPE)))
        dgq_ref[...] += dgq
        dgk_ref[...] += dgk
        dgkr_ref[...] += jnp.sum(dy * xn, axis=0, keepdims=True)

    row = lambda w: pl.BlockSpec((tm, w), lambda i: (i, 0))
    vec = lambda w: pl.BlockSpec((1, w), lambda i: (0, 0))
    return pl.pallas_call(
        body, name="prep2_bwd", grid=(t // tm,),
        in_specs=[row(1024), row(1024), pl.BlockSpec((tm, LANES), lambda i: (i, 15)), row(1024), row(1024), row(512),
                  vec(128), vec(128), vec(128), pl.BlockSpec((6, tm, LANES), lambda i: (0, i, 0))],
        out_specs=[row(1024), row(1024), row(128), vec(128), vec(128), vec(128)],
        out_shape=[jax.ShapeDtypeStruct((t, 1024), BF16), jax.ShapeDtypeStruct((t, 1024), BF16),
                   jax.ShapeDtypeStruct((t, LANES), F32)] + [jax.ShapeDtypeStruct((1, LANES), F32)] * 3,
        compiler_params=_params(("arbitrary",)),
    )(qlat, kv, proj, dqf, dkf, dvp, gq, gk, gkr, tabs)


A_TQ = 256
A_WIN = 3 * A_TQ


def _a_specs(t):
    nb = t // A_TQ
    blk = lambda s: pl.BlockSpec((A_TQ, 512), lambda i: (jnp.maximum(i - s, 0), 0))
    return nb, blk


def _a_probs(qb, q_ref, kc, b_ref, head, sl, lo):
    hm = lo if head % 2 == 0 else ~lo
    qm = jnp.where(hm, q_ref[:, sl], jnp.zeros((), BF16))
    s = _dot_nt(qm, kc) + b_ref[head]
    col = lax.broadcasted_iota(jnp.int32, s.shape, 1)
    s = jnp.where(col >= 2 * A_TQ - A_TQ * qb, s, NEG)
    e = jnp.exp(s - jnp.max(s, axis=-1, keepdims=True))
    pr = e * (1.0 / jnp.sum(e, axis=-1, keepdims=True))
    return hm, qm, pr


def attn_a_fwd(qa, ka, va, bias):
    t = qa.shape[0]
    nb, blk = _a_specs(t)

    def body(q_ref, k2_ref, k1_ref, k0_ref, v2_ref, v1_ref, v0_ref, b_ref, o_ref):
        qb = pl.program_id(0)
        lo = _lane((A_TQ, LANES)) < HD
        for p in range(4):
            sl = slice(LANES * p, LANES * (p + 1))
            kc = jnp.concatenate([k2_ref[:, sl], k1_ref[:, sl], k0_ref[:, sl]], axis=0)
            vc = jnp.concatenate([v2_ref[:, sl], v1_ref[:, sl], v0_ref[:, sl]], axis=0)
            outs = []
            for h2 in range(2):
                _, _, pr = _a_probs(qb, q_ref, kc, b_ref, 2 * p + h2, sl, lo)
                outs.append(_dot(pr.astype(BF16), vc))
            o_ref[:, sl] = jnp.where(lo, outs[0], outs[1]).astype(BF16)

    return pl.pallas_call(
        body, name="attn_a_fwd", grid=(nb,),
        in_specs=[blk(0), blk(2), blk(1), blk(0), blk(2), blk(1), blk(0),
                  pl.BlockSpec((HEADS, A_TQ, A_WIN), lambda i: (0, 0, 0))],
        out_specs=pl.BlockSpec((A_TQ, 512), lambda i: (i, 0)),
        out_shape=jax.ShapeDtypeStruct((t, 512), BF16),
        compiler_params=_params(("parallel",)),
    )(qa, ka, ka, ka, va, va, va, bias)


def attn_a_bwd(qa, ka, va, bias, do):
    t = qa.shape[0]
    nb, blk = _a_specs(t)

    def body(q_ref, k2_ref, k1_ref, k0_ref, v2_ref, v1_ref, v0_ref, b_ref, do_ref, dq_ref, dk_ref, dv_ref, db_ref):
        qb = pl.program_id(0)

        @pl.when(qb == 0)
        def _():
            db_ref[...] = jnp.zeros_like(db_ref)

        lo = _lane((A_TQ, LANES)) < HD
        for p in range(4):
            sl = slice(LANES * p, LANES * (p + 1))
            kc = jnp.concatenate([k2_ref[:, sl], k1_ref[:, sl], k0_ref[:, sl]], axis=0)
            vc = jnp.concatenate([v2_ref[:, sl], v1_ref[:, sl], v0_ref[:, sl]], axis=0)
            dqs = []
            dkc = jnp.zeros((A_WIN, LANES), F32)
            dvc = jnp.zeros((A_WIN, LANES), F32)
            for h2 in range(2):
                head = 2 * p + h2
                hm, qm, pr = _a_probs(qb, q_ref, kc, b_ref, head, sl, lo)
                dom = jnp.where(hm, do_ref[:, sl], jnp.zeros((), BF16))
                dp = _dot_nt(dom, vc)
                ds = pr * (dp - jnp.sum(pr * dp, axis=-1, keepdims=True))
                db_ref[head] += ds
                dsb = ds.astype(BF16)
                dqs.append(_dot(dsb, kc))
                dkc = dkc + _dot_tn(dsb, qm)
                dvc = dvc + _dot_tn(pr.astype(BF16), dom)
            dq_ref[:, sl] = jnp.where(lo, dqs[0], dqs[1])
            for s in range(3):
                rows = slice(A_TQ * (2 - s), A_TQ * (3 - s))
                dk_ref[s, :, sl] = dkc[rows]
                dv_ref[s, :, sl] = dvc[rows]

    share = pl.BlockSpec((3, A_TQ, 512), lambda i: (0, i, 0))
    return pl.pallas_call(
        body, name="attn_a_bwd", grid=(nb,),
        in_specs=[blk(0), blk(2), blk(1), blk(0), blk(2), blk(1), blk(0),
                  pl.BlockSpec((HEADS, A_TQ, A_WIN), lambda i: (0, 0, 0)), blk(0)],
        out_specs=[pl.BlockSpec((A_TQ, 512), lambda i: (i, 0)), share, share,
                   pl.BlockSpec((HEADS, A_TQ, A_WIN), lambda i: (0, 0, 0))],
        out_shape=[jax.ShapeDtypeStruct((t, 512), F32), jax.ShapeDtypeStruct((3, t, 512), F32),
                   jax.ShapeDtypeStruct((3, t, 512), F32), jax.ShapeDtypeStruct((HEADS, A_TQ, A_WIN), F32)],
        compiler_params=_params(("arbitrary",)),
    )(qa, ka, ka, ka, va, va, va, bias, do)


B_T = 512


def _b_mask(s):
    qc = lax.broadcasted_iota(jnp.int32, s.shape, 0) // CHUNK
    kc = lax.broadcasted_iota(jnp.int32, s.shape, 1) // CHUNK
    return jnp.where(kc <= qc, s, NEG)


def attn_b_fwd(qf, kf, vp):
    t = qf.shape[0]
    n = t // B_T

    def body(q_ref, k_ref, v_ref, o_ref, lse_ref, m_s, l_s, acc_s):
        qb, kb = pl.program_id(1), pl.program_id(2)

        @pl.when(kb == 0)
        def _():
            m_s[...] = jnp.full_like(m_s, NEG)
            l_s[...] = jnp.zeros_like(l_s)
            acc_s[...] = jnp.zeros_like(acc_s)

        def step(masked):
            v = v_ref[...]
            for h2 in range(2):
                sl = slice(LANES * h2, LANES * (h2 + 1))
                s = _dot_nt(q_ref[:, sl], k_ref[:, sl]) * B_SCALE
                if masked:
                    s = _b_mask(s)
                m_prev = m_s[h2]
                m_new = jnp.maximum(m_prev, jnp.max(s, axis=-1, keepdims=True))
                alpha = jnp.exp(m_prev - m_new)
                pr = jnp.exp(s - m_new)
                l_s[h2] = alpha * l_s[h2] + jnp.sum(pr, axis=-1, keepdims=True)
                acc_s[h2] = alpha * acc_s[h2] + _dot(pr.astype(BF16), v)
                m_s[h2] = m_new

        @pl.when(kb < qb)
        def _():
            step(False)

        @pl.when(kb == qb)
        def _():
            step(True)
            lo = _lane((B_T, LANES)) < HD
            outs = []
            for h2 in range(2):
                l = l_s[h2]
                outs.append(acc_s[h2] * (1.0 / l))
                lse_ref[h2] = jnp.broadcast_to(m_s[h2] + jnp.log(l), (B_T, LANES))
            o_ref[...] = jnp.where(lo, outs[0], outs[1]).astype(BF16)

    return pl.pallas_call(
        body, name="attn_b_fwd", grid=(4, n, n),
        in_specs=[pl.BlockSpec((B_T, 256), lambda p, i, j: (i, p)),
                  pl.BlockSpec((B_T, 256), lambda p, i, j: (jnp.minimum(j, i), p)),
                  pl.BlockSpec((B_T, LANES), lambda p, i, j: (jnp.minimum(j, i), p))],
        out_specs=[pl.BlockSpec((B_T, LANES), lambda p, i, j: (i, p)),
                   pl.BlockSpec((2, B_T, LANES), lambda p, i, j: (p, i, 0))],
        out_shape=[jax.ShapeDtypeStruct((t, 512), BF16), jax.ShapeDtypeStruct((HEADS, t, LANES), F32)],
        scratch_shapes=[pltpu.VMEM((2, B_T, 1), F32), pltpu.VMEM((2, B_T, 1), F32), pltpu.VMEM((2, B_T, LANES), F32)],
        compiler_params=_params(("parallel", "parallel", "arbitrary")),
    )(qf, kf, vp)


def attn_b_bwd(qf, kf, vp, do, o, lse):
    t = qf.shape[0]
    n = t // B_T

    def body(q_ref, k_ref, v_ref, do_ref, o_ref, lse_ref, dq_ref, dk_ref, dv_ref):
        kb, qb = pl.program_id(1), pl.program_id(2)

        @pl.when((kb == 0) & (qb == 0))
        def _():
            dq_ref[...] = jnp.zeros_like(dq_ref)

        @pl.when(qb == 0)
        def _():
            dk_ref[...] = jnp.zeros_like(dk_ref)
            dv_ref[...] = jnp.zeros_like(dv_ref)

        def step(masked):
            rows = pl.ds(pl.multiple_of(qb * B_T, B_T), B_T)
            v = v_ref[...]
            dov = do_ref[...]
            prod = dov.astype(F32) * o_ref[...].astype(F32)
            lo = _lane((B_T, LANES)) < HD
            for h2 in range(2):
                sl = slice(LANES * h2, LANES * (h2 + 1))
                hm = lo if h2 == 0 else ~lo
                q = q_ref[:, sl]
                k = k_ref[:, sl]
                dom = jnp.where(hm, dov, jnp.zeros((), BF16))
                delta = _seg_sum(prod, hm)
                s = _dot_nt(q, k) * B_SCALE
                if masked:
                    s = _b_mask(s)
                pr = jnp.exp(s - lse_ref[h2][:, 0:1])
                dp = _dot_nt(dom, v)
                ds = (pr * (dp - delta) * B_SCALE).astype(BF16)
                dq_ref[rows, sl] += _dot(ds, k)
                dk_ref[:, sl] += _dot_tn(ds, q)
                dv_ref[...] += _dot_tn(pr.astype(BF16), dom)

        @pl.when(qb > kb)
        def _():
            step(False)

        @pl.when(qb == kb)
        def _():
            step(True)

    qrow = lambda w: pl.BlockSpec((B_T, w), lambda p, j, i: (jnp.maximum(i, j), p))
    krow = lambda w: pl.BlockSpec((B_T, w), lambda p, j, i: (j, p))
    return pl.pallas_call(
        body, name="attn_b_bwd", grid=(4, n, n),
        in_specs=[qrow(256), krow(256), krow(LANES), qrow(LANES), qrow(LANES),
                  pl.BlockSpec((2, B_T, LANES), lambda p, j, i: (p, jnp.maximum(i, j), 0))],
        out_specs=[pl.BlockSpec((t, 256), lambda p, j, i: (0, p)), krow(256), krow(LANES)],
        out_shape=[jax.ShapeDtypeStruct((t, 1024), F32), jax.ShapeDtypeStruct((t, 1024), F32),
                   jax.ShapeDtypeStruct((t, 512), F32)],
        compiler_params=_params(("parallel", "arbitrary", "arbitrary")),
    )(qf, kf, vp, do, o, lse)


_U_LEN = A_TQ + A_WIN - 1


def _band_mask():
    a = np.arange(A_TQ)[:, None] // CHUNK
    b = np.arange(A_WIN)[None, :] // CHUNK
    return (b >= a) & (b <= a + A_LEFT)


def bias_block(table):
    h = table.shape[0]
    n_lo = A_WIN - 1 - 2 * A_TQ - A_MAX_REL
    ext = jnp.concatenate([jnp.repeat(table[:, :1], n_lo, axis=1), table,
                           jnp.repeat(table[:, -1:], _U_LEN - n_lo - table.shape[1], axis=1)], axis=1)
    row = jnp.pad(ext, ((0, 0), (0, 1)))
    flat = jnp.pad(jnp.tile(row, (1, A_TQ)), ((0, 0), (0, A_TQ)))
    hank = flat.reshape(h, A_TQ, _U_LEN + 2)[:, :, :A_WIN]
    toep = hank[:, :, ::-1]
    return jnp.where(jnp.asarray(_band_mask())[None], toep, NEG)


def bias_block_grad(db):
    h = db.shape[0]
    n_lo = A_WIN - 1 - 2 * A_TQ - A_MAX_REL
    hank = jnp.pad(db[:, :, ::-1], ((0, 0), (0, 0), (0, _U_LEN + 2 - A_WIN)))
    flat = hank.reshape(h, A_TQ * (_U_LEN + 2))[:, :A_TQ * (_U_LEN + 1)]
    ext = jnp.sum(flat.reshape(h, A_TQ, _U_LEN + 1), axis=1)[:, :_U_LEN]
    n_tab = 2 * A_MAX_REL + 1
    first = jnp.sum(ext[:, :n_lo + 1], axis=1, keepdims=True)
    last = jnp.sum(ext[:, n_lo + n_tab - 1:], axis=1, keepdims=True)
    return jnp.concatenate([first, ext[:, n_lo + 1:n_lo + n_tab - 1], last], axis=1)


def rope_tabs(t):
    inv = 1.0 / (10000.0 ** (jnp.arange(0, ROPE, 2, dtype=F32) / ROPE))
    ang = jnp.arange(t, dtype=F32)[:, None] * inv[None, :]
    cos, sin = jnp.cos(ang), jnp.sin(ang)
    z = lambda w: jnp.zeros((t, w), F32)
    ck = jnp.concatenate([cos, cos, z(96)], axis=1)
    s1k = jnp.concatenate([-sin, z(112)], axis=1)
    s2k = jnp.concatenate([z(16), sin, z(96)], axis=1)
    cq = jnp.concatenate([jnp.ones((t, HD), F32), cos, cos, z(32)], axis=1)
    s1q = jnp.concatenate([z(HD), -sin, z(48)], axis=1)
    s2q = jnp.concatenate([z(HD + 16), sin, z(32)], axis=1)
    return jnp.stack([cq, s1q, s2q, ck, s1k, s2k])


def _pad_lanes(v, width):
    return jnp.pad(v, ((0, 0), (0, width - v.shape[1])))


def local_step(x, target, w):
    t = x.shape[0]
    gq = jnp.tile(w["a_q_norm"], (1, HEADS))
    gk = jnp.tile(w["a_k_norm"], (1, HEADS))
    gq128 = _pad_lanes(jnp.concatenate([w["b_q_nope_norm"], w["b_q_rope_norm"]], axis=1), LANES)
    gk128 = _pad_lanes(w["b_k_nope_norm"], LANES)
    gkr128 = _pad_lanes(w["b_k_rope_norm"], LANES)
    tabs = rope_tabs(t)
    bias = bias_block(w["a_rel_bias"])

    x1 = ffn_fwd(x, w["ffn1_norm"], w["ffn1_w_gate"], w["ffn1_w_up"], w["ffn1_w_down"], "ffn_fwd")
    h, proj = mix_proj(x1, w["mix_norm"], w["w_in"])
    qa, ka, va, cqn, ckvn = prep1_fwd(proj, gq, gk, w["b_q_lat_norm"], w["b_kv_lat_norm"])
    qlat = matmul(cqn, w["b_w_uq"], "nn", "uq_fwd", 512, 1024, 256)
    kv = matmul(ckvn, w["b_w_ukv"], "nn", "ukv_fwd", 512, 1024, 128)
    qf, kf, vp = prep2_fwd(qlat, kv, proj, gq128, gk128, gkr128, tabs)
    oa = attn_a_fwd(qa, ka, va, bias)
    ob, lse = attn_b_fwd(qf, kf, vp)
    x2 = out_proj(x1, oa, ob, w["w_out"])
    x3 = ffn_fwd(x2, w["ffn2_norm"], w["ffn2_w_gate"], w["ffn2_w_up"], w["ffn2_w_down"], "ffn_fwd")

    g = {}
    dx3, g["final_norm"], loss = final_loss(x3, w["final_norm"], target)
    g["ffn2_w_gate"], g["ffn2_w_up"], g["ffn2_w_down"], dhp = ffn_bwd(
        x2, dx3, w["ffn2_norm"], w["ffn2_w_gate"], w["ffn2_w_up"], w["ffn2_w_down"], "ffn_bwd")
    dx2, g["ffn2_norm"] = norm_bwd(x2, w["ffn2_norm"], dhp, dx3, "ffn_norm_bwd")
    d_oa, d_ob = out_proj_bwd(dx2, w["w_out"])
    g["w_out"] = jnp.concatenate([matmul(oa, dx2, "tn", "w_out_a_bwd", 512, 1024, 512),
                                  matmul(ob, dx2, "tn", "w_out_b_bwd", 512, 1024, 512)], axis=0)
    dqf, dkf, dvp = attn_b_bwd(qf, kf, vp, d_ob, ob, lse)
    dqa, dkp, dvpa, dbias = attn_a_bwd(qa, ka, va, bias, d_oa)
    dqlat, dkv, dkr, dgq128, dgk128, dgkr128 = prep2_bwd(qlat, kv, proj, dqf, dkf, dvp, gq128, gk128, gkr128, tabs)
    dcq = matmul(dqlat, w["b_w_uq"], "nt", "uq_bwd_x", 512, 256, 1024)
    g["b_w_uq"] = matmul(cqn, dqlat, "tn", "uq_bwd_w", 256, 1024, 512)
    dckv = matmul(dkv, w["b_w_ukv"], "nt", "ukv_bwd_x", 512, 128, 1024)
    g["b_w_ukv"] = matmul(ckvn, dkv, "tn", "ukv_bwd_w", 128, 1024, 512)
    dproj, dgq, dgk, g["b_q_lat_norm"], g["b_kv_lat_norm"] = prep1_bwd(
        proj, dqa, dkp, dvpa, dcq, dckv, dkr, gq, gk, w["b_q_lat_norm"], w["b_kv_lat_norm"])
    dh = matmul(dproj, w["w_in"], "nt", "w_in_bwd_x", 512, 1024, PROJ_W)
    g["w_in"] = matmul(h, dproj, "tn", "w_in_bwd_w", 1024, 1024, 512)
    dx1, g["mix_norm"] = norm_bwd(x1, w["mix_norm"], dh[None], dx2, "mix_norm_bwd")
    g["ffn1_w_gate"], g["ffn1_w_up"], g["ffn1_w_down"], dhp = ffn_bwd(
        x, dx1, w["ffn1_norm"], w["ffn1_w_gate"], w["ffn1_w_up"], w["ffn1_w_down"], "ffn_bwd")
    grad_x, g["ffn1_norm"] = norm_bwd(x, w["ffn1_norm"], dhp, dx1, "ffn_norm_bwd")

    g["a_q_norm"] = jnp.sum(dgq.reshape(HEADS, HD), axis=0, keepdims=True)
    g["a_k_norm"] = jnp.sum(dgk.reshape(HEADS, HD), axis=0, keepdims=True)
    g["a_rel_bias"] = bias_block_grad(dbias)
    g["b_q_nope_norm"] = dgq128[:, :HD]
    g["b_q_rope_norm"] = dgq128[:, HD:HD + ROPE]
    g["b_k_nope_norm"] = dgk128[:, :HD]
    g["b_k_rope_norm"] = dgkr128[:, :ROPE]
    return loss, grad_x, g


ANY = pl.BlockSpec(memory_space=pl.ANY)
N_DEV = 8


def _place():
    return lax.axis_index("x"), lax.axis_index("y"), lax.axis_index("c")


def _flip(v, bit):
    return 1 - v if bit else v


def allgather_shards(ws):
    n = len(ws)

    def body(*refs):
        ins, outs = refs[:n], refs[n:2 * n]
        send_sems, recv_sems, local_sems = refs[2 * n:]
        x, y, c = _place()
        s_me = 2 * x + y
        sibling = (x, y, 1 - c)
        chips = [(1 - x, y), (x, 1 - y), (1 - x, 1 - y)]

        def half(a, core):
            hr = ws[a].shape[0] // 2
            return pl.ds(core * hr, hr)

        def remote(k, src, dst, to):
            return pltpu.make_async_remote_copy(src_ref=src, dst_ref=dst, send_sem=send_sems.at[k],
                                                recv_sem=recv_sems.at[k], device_id=to, device_id_type=MESH)

        local, sends = [], []
        for a in range(n):
            local.append(pltpu.make_async_copy(ins[a], outs[a].at[s_me], local_sems.at[a]))
            local[-1].start()
            for j, (cx, cy) in enumerate(chips):
                sends.append(remote(6 * a + j, ins[a].at[half(a, c)], outs[a].at[s_me, half(a, c)], (cx, cy, c)))
                sends[-1].start()
        for a in range(n):
            for j, (cx, cy) in enumerate(chips):
                got = outs[a].at[2 * cx + cy, half(a, c)]
                remote(6 * a + j, got, got, (cx, cy, c)).wait_recv()
                sends.append(remote(6 * a + 3 + j, got, got, sibling))
                sends[-1].start()
        for a in range(n):
            for j, (cx, cy) in enumerate(chips):
                got = outs[a].at[2 * cx + cy, half(a, 1 - c)]
                remote(6 * a + 3 + j, got, got, sibling).wait_recv()
        for cp in sends:
            cp.wait_send()
        for cp in local:
            cp.wait()

    return pl.pallas_call(
        body, name="allgather_shards",
        in_specs=[ANY] * n, out_specs=[ANY] * n,
        out_shape=[jax.ShapeDtypeStruct((N_SHARD,) + w.shape, w.dtype) for w in ws],
        scratch_shapes=[pltpu.SemaphoreType.DMA((6 * n,)), pltpu.SemaphoreType.DMA((6 * n,)),
                        pltpu.SemaphoreType.DMA((n,))],
    )(*ws)


def scatter_partials(gs):
    n = len(gs)

    def body(*refs):
        ins, outs = refs[:n], refs[n:2 * n]
        send_sems, recv_sems, local_sems = refs[2 * n:]
        x, y, c = _place()
        me = 4 * x + 2 * y + c

        def piece(a, px, py, pc):
            hr = gs[a].shape[1] // 2
            return ins[a].at[2 * px + py, pl.ds(pc * hr, hr)]

        local, sends = [], []
        for a in range(n):
            local.append(pltpu.make_async_copy(piece(a, x, y, c), outs[a].at[me], local_sems.at[a]))
            local[-1].start()
            for k in range(1, N_DEV):
                px, py, pc = _flip(x, k & 4), _flip(y, k & 2), _flip(c, k & 1)
                sends.append(pltpu.make_async_remote_copy(
                    src_ref=piece(a, px, py, pc), dst_ref=outs[a].at[me], send_sem=send_sems.at[7 * a + k - 1],
                    recv_sem=recv_sems.at[7 * a + k - 1], device_id=(px, py, pc), device_id_type=MESH))
                sends[-1].start()
        for a in range(n):
            for k in range(1, N_DEV):
                px, py, pc = _flip(x, k & 4), _flip(y, k & 2), _flip(c, k & 1)
                slot = outs[a].at[4 * px + 2 * py + pc]
                pltpu.make_async_remote_copy(
                    src_ref=slot, dst_ref=slot, send_sem=send_sems.at[7 * a + k - 1],
                    recv_sem=recv_sems.at[7 * a + k - 1], device_id=(px, py, pc), device_id_type=MESH).wait_recv()
        for cp in sends:
            cp.wait_send()
        for cp in local:
            cp.wait()

    return pl.pallas_call(
        body, name="scatter_partials",
        in_specs=[ANY] * n, out_specs=[ANY] * n,
        out_shape=[jax.ShapeDtypeStruct((N_DEV, g.shape[1] // 2, g.shape[2]), g.dtype) for g in gs],
        scratch_shapes=[pltpu.SemaphoreType.DMA((7 * n,)), pltpu.SemaphoreType.DMA((7 * n,)),
                        pltpu.SemaphoreType.DMA((n,))],
    )(*gs)


def sum_slots(land, name):
    _, rows, cols = land.shape
    tr = rows if rows <= 128 else rows // 2

    def body(l_ref, o_ref):
        acc = l_ref[0].astype(F32)
        for s in range(1, N_DEV):
            acc = acc + l_ref[s].astype(F32)
        o_ref[...] = acc

    return pl.pallas_call(
        body, name=name, grid=(rows // tr,),
        in_specs=[pl.BlockSpec((N_DEV, tr, cols), lambda i: (0, i, 0))],
        out_specs=pl.BlockSpec((tr, cols), lambda i: (i, 0)),
        out_shape=jax.ShapeDtypeStruct((rows, cols), F32),
        compiler_params=_params(("parallel",)),
    )(land)


def join_halves(hs):
    n = len(hs)

    def body(*refs):
        ins, outs = refs[:n], refs[n:2 * n]
        send_sems, recv_sems, local_sems = refs[2 * n:]
        x, y, c = _place()
        local, sends = [], []
        for a in range(n):
            hr = hs[a].shape[0]
            mine = outs[a].at[pl.ds(c * hr, hr)]
            local.append(pltpu.make_async_copy(ins[a], mine, local_sems.at[a]))
            local[-1].start()
            sends.append(pltpu.make_async_remote_copy(
                src_ref=ins[a], dst_ref=mine, send_sem=send_sems.at[a], recv_sem=recv_sems.at[a],
                device_id=(x, y, 1 - c), device_id_type=MESH))
            sends[-1].start()
        for a in range(n):
            hr = hs[a].shape[0]
            theirs = outs[a].at[pl.ds((1 - c) * hr, hr)]
            pltpu.make_async_remote_copy(
                src_ref=theirs, dst_ref=theirs, send_sem=send_sems.at[a], recv_sem=recv_sems.at[a],
                device_id=(x, y, 1 - c), device_id_type=MESH).wait_recv()
        for cp in sends:
            cp.wait_send()
        for cp in local:
            cp.wait()

    return pl.pallas_call(
        body, name="join_halves",
        in_specs=[ANY] * n, out_specs=[ANY] * n,
        out_shape=[jax.ShapeDtypeStruct((2 * h.shape[0], h.shape[1]), h.dtype) for h in hs],
        scratch_shapes=[pltpu.SemaphoreType.DMA((n,)), pltpu.SemaphoreType.DMA((n,)), pltpu.SemaphoreType.DMA((n,))],
    )(*hs)


def allreduce_small(vec):
    def body(v_ref, o_ref, land_ref, send_sems, recv_sems):
        x, y, c = _place()
        me = 4 * x + 2 * y + c
        land_ref[me] = v_ref[...]
        sends = []
        for k in range(1, N_DEV):
            px, py, pc = _flip(x, k & 4), _flip(y, k & 2), _flip(c, k & 1)
            sends.append(pltpu.make_async_remote_copy(
                src_ref=v_ref, dst_ref=land_ref.at[me], send_sem=send_sems.at[k - 1], recv_sem=recv_sems.at[k - 1],
                device_id=(px, py, pc), device_id_type=MESH))
            sends[-1].start()
        for k in range(1, N_DEV):
            px, py, pc = _flip(x, k & 4), _flip(y, k & 2), _flip(c, k & 1)
            slot = land_ref.at[4 * px + 2 * py + pc]
            pltpu.make_async_remote_copy(
                src_ref=slot, dst_ref=slot, send_sem=send_sems.at[k - 1], recv_sem=recv_sems.at[k - 1],
                device_id=(px, py, pc), device_id_type=MESH).wait_recv()
        for cp in sends:
            cp.wait_send()
        acc = land_ref[0]
        for s in range(1, N_DEV):
            acc = acc + land_ref[s]
        o_ref[...] = acc

    vm = pl.BlockSpec(memory_space=pltpu.VMEM)
    return pl.pallas_call(
        body, name="allreduce_small",
        in_specs=[vm], out_specs=vm,
        out_shape=jax.ShapeDtypeStruct(vec.shape, F32),
        scratch_shapes=[pltpu.VMEM((N_DEV,) + vec.shape, F32), pltpu.SemaphoreType.DMA((N_DEV - 1,)),
                        pltpu.SemaphoreType.DMA((N_DEV - 1,))],
    )(vec)


def adamw(w, g, m, v, name):
    rows, cols = w.shape
    tr = rows
    while tr * cols * 4 * 14 > 24 * 1024 * 1024 and tr % 16 == 0:
        tr //= 2
    c1 = 1.0 - ADAM_B1 ** ADAM_STEP
    c2 = 1.0 - ADAM_B2 ** ADAM_STEP

    def body(w_ref, g_ref, m_ref, v_ref, d_ref, nm_ref, nv_ref):
        gv = g_ref[...]
        nm = ADAM_B1 * m_ref[...] + (1.0 - ADAM_B1) * gv
        nv = ADAM_B2 * v_ref[...] + (1.0 - ADAM_B2) * (gv * gv)
        nm_ref[...] = nm
        nv_ref[...] = nv
        d_ref[...] = -ADAM_LR * ((nm / c1) / (jnp.sqrt(nv / c2) + ADAM_EPS) + ADAM_WD * w_ref[...])

    blk = pl.BlockSpec((tr, cols), lambda i: (i, 0))
    return pl.pallas_call(
        body, name=name, grid=(rows // tr,),
        in_specs=[blk] * 4, out_specs=[blk] * 3,
        out_shape=[jax.ShapeDtypeStruct((rows, cols), F32)] * 3,
        compiler_params=_params(("parallel",)),
    )(w, g, m, v)


BIG = ("ffn1_w_gate", "ffn1_w_up", "ffn1_w_down", "w_in", "b_w_uq", "b_w_ukv", "w_out",
       "ffn2_w_gate", "ffn2_w_up", "ffn2_w_down")
SMALL = ("ffn1_norm", "mix_norm", "a_q_norm", "a_k_norm", "a_rel_bias", "b_q_lat_norm", "b_kv_lat_norm",
         "b_q_nope_norm", "b_q_rope_norm", "b_k_nope_norm", "b_k_rope_norm", "ffn2_norm", "final_norm")
WEIGHTS = ("ffn1_norm", "ffn1_w_gate", "ffn1_w_up", "ffn1_w_down", "mix_norm", "w_in", "a_q_norm", "a_k_norm",
           "a_rel_bias", "b_q_lat_norm", "b_w_uq", "b_kv_lat_norm", "b_w_ukv", "b_q_nope_norm", "b_q_rope_norm",
           "b_k_nope_norm", "b_k_rope_norm", "w_out", "ffn2_norm", "ffn2_w_gate", "ffn2_w_up", "ffn2_w_down",
           "final_norm")
PACK_SHAPE = (8, 1024)


def _pack_small(d, last=None):
    flat = [d[n].reshape(-1) for n in SMALL]
    used = sum(f.shape[0] for f in flat)
    total = PACK_SHAPE[0] * PACK_SHAPE[1]
    tail = jnp.zeros((total - used - 1,), F32)
    end = jnp.zeros((1,), F32) if last is None else last.reshape(1)
    return jnp.concatenate(flat + [tail, end]).reshape(PACK_SHAPE)


def _unpack_small(p, like):
    flat = p.reshape(-1)
    out, off = {}, 0
    for n in SMALL:
        size = like[n].size
        out[n] = flat[off:off + size].reshape(like[n].shape)
        off += size
    return out, flat[-1]


def _cols_to_shards(g):
    rows, cols = g.shape
    return g.reshape(rows, N_SHARD, cols // N_SHARD).transpose(1, 0, 2)


def _shards_to_cols(g):
    return g.transpose(1, 0, 2).reshape(g.shape[1], -1)


def kernel(x, ffn1_norm, ffn1_w_gate, ffn1_w_up, ffn1_w_down, mix_norm, w_in, a_q_norm, a_k_norm, a_rel_bias, b_q_lat_norm, b_w_uq, b_kv_lat_norm, b_w_ukv, b_q_nope_norm, b_q_rope_norm, b_k_nope_norm, b_k_rope_norm, w_out, ffn2_norm, ffn2_w_gate, ffn2_w_up, ffn2_w_down, final_norm, loss_target, m_ffn1_norm, m_ffn1_w_gate, m_ffn1_w_up, m_ffn1_w_down, m_mix_norm, m_w_in, m_a_q_norm, m_a_k_norm, m_a_rel_bias, m_b_q_lat_norm, m_b_w_uq, m_b_kv_lat_norm, m_b_w_ukv, m_b_q_nope_norm, m_b_q_rope_norm, m_b_k_nope_norm, m_b_k_rope_norm, m_w_out, m_ffn2_norm, m_ffn2_w_gate, m_ffn2_w_up, m_ffn2_w_down, m_final_norm, v_ffn1_norm, v_ffn1_w_gate, v_ffn1_w_up, v_ffn1_w_down, v_mix_norm, v_w_in, v_a_q_norm, v_a_k_norm, v_a_rel_bias, v_b_q_lat_norm, v_b_w_uq, v_b_kv_lat_norm, v_b_w_ukv, v_b_q_nope_norm, v_b_q_rope_norm, v_b_k_nope_norm, v_b_k_rope_norm, v_w_out, v_ffn2_norm, v_ffn2_w_gate, v_ffn2_w_up, v_ffn2_w_down, v_final_norm):
    args = locals()
    wts = {n: args[n][0] for n in WEIGHTS}
    mom = {n: args["m_" + n][0] for n in WEIGHTS}
    var = {n: args["v_" + n][0] for n in WEIGHTS}

    gathered = dict(zip(BIG, allgather_shards([wts[n].astype(BF16) for n in BIG])))
    w = {n: wts[n] if n == "a_rel_bias" else wts[n][None] for n in SMALL}
    for n in BIG:
        if n.startswith("ffn"):
            w[n] = gathered[n]
    w["w_in"] = jnp.pad(_shards_to_cols(gathered["w_in"]), ((0, 0), (0, PROJ_W - IN_COLS)))
    uq = _shards_to_cols(gathered["b_w_uq"]).reshape(256, HEADS, HD + ROPE)
    w["b_w_uq"] = jnp.pad(uq, ((0, 0), (0, 0), (0, LANES - HD - ROPE))).reshape(256, HEADS * LANES)
    w["b_w_ukv"] = _shards_to_cols(gathered["b_w_ukv"])
    w["w_out"] = gathered["w_out"].reshape(N_SHARD * gathered["w_out"].shape[1], D_MODEL)

    loss, grad_x, g = local_step(x[0], loss_target[0], w)

    part = {n: g[n] for n in BIG if n.startswith("ffn")}
    part["w_in"] = _cols_to_shards(g["w_in"][:, :IN_COLS])
    part["b_w_uq"] = _cols_to_shards(g["b_w_uq"].reshape(256, HEADS, LANES)[:, :, :HD + ROPE].reshape(256, -1))
    part["b_w_ukv"] = _cols_to_shards(g["b_w_ukv"])
    part["w_out"] = g["w_out"].reshape(N_SHARD, -1, D_MODEL)
    landed = scatter_partials([part[n].astype(BF16) for n in BIG])
    halves = [sum_slots(l, "sum_slots") for l in landed]
    grads = dict(zip(BIG, join_halves(halves)))

    small_sum, loss_sum = _unpack_small(allreduce_small(_pack_small(g, loss[0, 0])), wts)
    grads.update(small_sum)

    delta, new_m, new_v = {}, {}, {}
    for n in BIG:
        delta[n], new_m[n], new_v[n] = adamw(wts[n], grads[n], mom[n], var[n], "adamw")
    packed = adamw(_pack_small(wts), _pack_small(grads), _pack_small(mom), _pack_small(var), "adamw_small")
    for dst, p in zip((delta, new_m, new_v), packed):
        dst.update(_unpack_small(p, wts)[0])

    lead = lambda d: [d[n][None] for n in WEIGHTS]
    return (loss_sum, grad_x[None], *lead(grads), *lead(delta), *lead(new_m), *lead(new_v))
```

```python
import functools
import math

import numpy as np
import jax
import jax.numpy as jnp
from jax import lax
from jax.experimental import pallas as pl
from jax.experimental.pallas import tpu as pltpu

F32 = jnp.float32
BF16 = jnp.bfloat16
EPS = 1e-6
NEG = -1e30

D_MODEL = 1024
D_FF = 2816
N_SHARD = 4
FS = D_FF // N_SHARD
CHUNK = 64
A_LEFT = 8
A_MAX_REL = 128
HEADS = 8
HD = 64
ROPE = 32
PROJ_W = 2048
IN_COLS = 1952
B_SCALE = 96 ** -0.5
LANES = 128

ADAM_LR = 0.001
ADAM_B1 = 0.9
ADAM_B2 = 0.999
ADAM_EPS = 1e-08
ADAM_WD = 0.01
ADAM_STEP = 10

VMEM_LIMIT = 56 * 1024 * 1024

MESH = pl.DeviceIdType.MESH


def _dot(a, b):
    return lax.dot_general(a, b, (((1,), (0,)), ((), ())), preferred_element_type=F32)


def _dot_nt(a, b):
    return lax.dot_general(a, b, (((1,), (1,)), ((), ())), preferred_element_type=F32)


def _dot_tn(a, b):
    return lax.dot_general(a, b, (((0,), (0,)), ((), ())), preferred_element_type=F32)


def _params(sem):
    return pltpu.CompilerParams(dimension_semantics=sem, vmem_limit_bytes=VMEM_LIMIT)


def _rms(xv):
    r = lax.rsqrt(jnp.mean(xv * xv, axis=-1, keepdims=True) + EPS)
    return r, xv * r


def ffn_fwd(x, g, wg, wu, wd, name):
    t, d = x.shape
    tm = 512

    def body(x_ref, g_ref, wg_ref, wu_ref, wd_ref, o_ref, h_ref, acc_ref):
        j = pl.program_id(1)

        @pl.when(j == 0)
        def _():
            _, xn = _rms(x_ref[...])
            h_ref[...] = (xn * g_ref[...]).astype(BF16)
            acc_ref[...] = jnp.zeros_like(acc_ref)

        h = h_ref[...]
        gp = _dot(h, wg_ref[0])
        up = _dot(h, wu_ref[0])
        a = (gp * jax.nn.sigmoid(gp) * up).astype(BF16)
        acc_ref[...] += _dot(a, wd_ref[0])

        @pl.when(j == N_SHARD - 1)
        def _():
            o_ref[...] = x_ref[...] + 0.5 * acc_ref[...]

    return pl.pallas_call(
        body, name=name, grid=(t // tm, N_SHARD),
        in_specs=[pl.BlockSpec((tm, d), lambda i, j: (i, 0)),
                  pl.BlockSpec((1, d), lambda i, j: (0, 0)),
                  pl.BlockSpec((1, d, FS), lambda i, j: (j, 0, 0)),
                  pl.BlockSpec((1, d, FS), lambda i, j: (j, 0, 0)),
                  pl.BlockSpec((1, FS, d), lambda i, j: (j, 0, 0))],
        out_specs=pl.BlockSpec((tm, d), lambda i, j: (i, 0)),
        out_shape=jax.ShapeDtypeStruct((t, d), F32),
        scratch_shapes=[pltpu.VMEM((tm, d), BF16), pltpu.VMEM((tm, d), F32)],
        compiler_params=_params(("parallel", "arbitrary")),
    )(x, g, wg, wu, wd)


def ffn_bwd(x, dout, g, wg, wu, wd, name):
    t, d = x.shape
    tm = 512

    def body(x_ref, do_ref, g_ref, wg_ref, wu_ref, wd_ref, dwg_ref, dwu_ref, dwd_ref, dhp_ref):
        i = pl.program_id(1)
        _, xn = _rms(x_ref[...])
        h = (xn * g_ref[...]).astype(BF16)
        dz = (0.5 * do_ref[...]).astype(BF16)
        wgv, wuv, wdv = wg_ref[0], wu_ref[0], wd_ref[0]
        gp = _dot(h, wgv)
        up = _dot(h, wuv)
        s = jax.nn.sigmoid(gp)
        sg = gp * s
        a = (sg * up).astype(BF16)
        da = _dot_nt(dz, wdv)
        dup = (da * sg).astype(BF16)
        dgp = (da * up * (s * (1.0 + gp * (1.0 - s)))).astype(BF16)

        @pl.when(i == 0)
        def _():
            dwg_ref[...] = jnp.zeros_like(dwg_ref)
            dwu_ref[...] = jnp.zeros_like(dwu_ref)
            dwd_ref[...] = jnp.zeros_like(dwd_ref)

        dwd_ref[0] += _dot_tn(a, dz)
        dwg_ref[0] += _dot_tn(h, dgp)
        dwu_ref[0] += _dot_tn(h, dup)
        dhp_ref[0] = _dot_nt(dgp, wgv) + _dot_nt(dup, wuv)

    return pl.pallas_call(
        body, name=name, grid=(N_SHARD, t // tm),
        in_specs=[pl.BlockSpec((tm, d), lambda j, i: (i, 0)),
                  pl.BlockSpec((tm, d), lambda j, i: (i, 0)),
                  pl.BlockSpec((1, d), lambda j, i: (0, 0)),
                  pl.BlockSpec((1, d, FS), lambda j, i: (j, 0, 0)),
                  pl.BlockSpec((1, d, FS), lambda j, i: (j, 0, 0)),
                  pl.BlockSpec((1, FS, d), lambda j, i: (j, 0, 0))],
        out_specs=[pl.BlockSpec((1, d, FS), lambda j, i: (j, 0, 0)),
                   pl.BlockSpec((1, d, FS), lambda j, i: (j, 0, 0)),
                   pl.BlockSpec((1, FS, d), lambda j, i: (j, 0, 0)),
                   pl.BlockSpec((1, tm, d), lambda j, i: (j, i, 0))],
        out_shape=[jax.ShapeDtypeStruct((N_SHARD, d, FS), F32),
                   jax.ShapeDtypeStruct((N_SHARD, d, FS), F32),
                   jax.ShapeDtypeStruct((N_SHARD, FS, d), F32),
                   jax.ShapeDtypeStruct((N_SHARD, t, d), F32)],
        compiler_params=_params(("arbitrary", "arbitrary")),
    )(x, dout, g, wg, wu, wd)


def norm_bwd(x, g, dhp, dres, name):
    t, d = x.shape
    p = dhp.shape[0]
    tm = 512

    def body(x_ref, g_ref, dhp_ref, dres_ref, dx_ref, dg_ref):
        i = pl.program_id(0)
        r, xn = _rms(x_ref[...])
        dh = dhp_ref[0]
        for q in range(1, p):
            dh = dh + dhp_ref[q]
        dhg = dh * g_ref[...]
        dx_ref[...] = dres_ref[...] + r * (dhg - xn * jnp.mean(dhg * xn, axis=-1, keepdims=True))

        @pl.when(i == 0)
        def _():
            dg_ref[...] = jnp.zeros_like(dg_ref)

        dg_ref[...] += jnp.sum(dh * xn, axis=0, keepdims=True)

    return pl.pallas_call(
        body, name=name, grid=(t // tm,),
        in_specs=[pl.BlockSpec((tm, d), lambda i: (i, 0)),
                  pl.BlockSpec((1, d), lambda i: (0, 0)),
                  pl.BlockSpec((p, tm, d), lambda i: (0, i, 0)),
                  pl.BlockSpec((tm, d), lambda i: (i, 0))],
        out_specs=[pl.BlockSpec((tm, d), lambda i: (i, 0)),
                   pl.BlockSpec((1, d), lambda i: (0, 0))],
        out_shape=[jax.ShapeDtypeStruct((t, d), F32), jax.ShapeDtypeStruct((1, d), F32)],
        compiler_params=_params(("arbitrary",)),
    )(x, g, dhp, dres)


def final_loss(x, g, target):
    t, d = x.shape
    tm = 512

    def body(x_ref, g_ref, t_ref, dx_ref, dg_ref, loss_ref):
        i = pl.program_id(0)
        r, xn = _rms(x_ref[...])
        gv = g_ref[...]
        e = xn * gv - t_ref[...]
        dy = e * (1.0 / d)
        dhg = dy * gv
        dx_ref[...] = r * (dhg - xn * jnp.mean(dhg * xn, axis=-1, keepdims=True))

        @pl.when(i == 0)
        def _():
            dg_ref[...] = jnp.zeros_like(dg_ref)
            loss_ref[...] = jnp.zeros_like(loss_ref)

        dg_ref[...] += jnp.sum(dy * xn, axis=0, keepdims=True)
        part = jnp.sum(jnp.sum(e * e, axis=-1, keepdims=True), axis=0, keepdims=True) * (0.5 / d)
        loss_ref[...] += jnp.broadcast_to(part, loss_ref.shape)

    return pl.pallas_call(
        body, name="final_loss", grid=(t // tm,),
        in_specs=[pl.BlockSpec((tm, d), lambda i: (i, 0)),
                  pl.BlockSpec((1, d), lambda i: (0, 0)),
                  pl.BlockSpec((tm, d), lambda i: (i, 0))],
        out_specs=[pl.BlockSpec((tm, d), lambda i: (i, 0)),
                   pl.BlockSpec((1, d), lambda i: (0, 0)),
                   pl.BlockSpec((1, LANES), lambda i: (0, 0))],
        out_shape=[jax.ShapeDtypeStruct((t, d), F32), jax.ShapeDtypeStruct((1, d), F32),
                   jax.ShapeDtypeStruct((1, LANES), F32)],
        compiler_params=_params(("arbitrary",)),
    )(x, g, target)


def mix_proj(x, g, w):
    t, d = x.shape
    n = w.shape[1]
    tm = 512

    def body(x_ref, g_ref, w_ref, h_ref, p_ref):
        _, xn = _rms(x_ref[...])
        h = (xn * g_ref[...]).astype(BF16)
        h_ref[...] = h
        p_ref[...] = _dot(h, w_ref[...])

    return pl.pallas_call(
        body, name="mix_proj", grid=(t // tm,),
        in_specs=[pl.BlockSpec((tm, d), lambda i: (i, 0)),
                  pl.BlockSpec((1, d), lambda i: (0, 0)),
                  pl.BlockSpec((d, n), lambda i: (0, 0))],
        out_specs=[pl.BlockSpec((tm, d), lambda i: (i, 0)),
                   pl.BlockSpec((tm, n), lambda i: (i, 0))],
        out_shape=[jax.ShapeDtypeStruct((t, d), BF16), jax.ShapeDtypeStruct((t, n), F32)],
        compiler_params=_params(("parallel",)),
    )(x, g, w)


def matmul(a, b, mode, name, tm, tn, tk, out_dtype=F32):
    if mode == "nn":
        (m, k), n = a.shape, b.shape[1]
        a_spec = pl.BlockSpec((tm, tk), lambda i, j, q: (i, q))
        b_spec = pl.BlockSpec((tk, tn), lambda i, j, q: (q, j))
        dot = _dot
    elif mode == "nt":
        (m, k), n = a.shape, b.shape[0]
        a_spec = pl.BlockSpec((tm, tk), lambda i, j, q: (i, q))
        b_spec = pl.BlockSpec((tn, tk), lambda i, j, q: (j, q))
        dot = _dot_nt
    else:
        (k, m), n = a.shape, b.shape[1]
        a_spec = pl.BlockSpec((tk, tm), lambda i, j, q: (q, i))
        b_spec = pl.BlockSpec((tk, tn), lambda i, j, q: (q, j))
        dot = _dot_tn
    assert m % tm == 0 and n % tn == 0 and k % tk == 0, (m, n, k, tm, tn, tk)
    nk = k // tk

    def body(a_ref, b_ref, o_ref, acc_ref):
        q = pl.program_id(2)

        @pl.when(q == 0)
        def _():
            acc_ref[...] = jnp.zeros_like(acc_ref)

        acc_ref[...] += dot(a_ref[...].astype(BF16), b_ref[...].astype(BF16))

        @pl.when(q == nk - 1)
        def _():
            o_ref[...] = acc_ref[...].astype(out_dtype)

    return pl.pallas_call(
        body, name=name, grid=(m // tm, n // tn, nk),
        in_specs=[a_spec, b_spec],
        out_specs=pl.BlockSpec((tm, tn), lambda i, j, q: (i, j)),
        out_shape=jax.ShapeDtypeStruct((m, n), out_dtype),
        scratch_shapes=[pltpu.VMEM((tm, tn), F32)],
        compiler_params=_params(("parallel", "parallel", "arbitrary")),
    )(a, b)


def out_proj(x, oa, ob_t, w):
    t, d = x.shape
    half = oa.shape[1]
    tm = 512

    def body(x_ref, oa_ref, obt_ref, w_ref, o_ref):
        o_ref[...] = (x_ref[...] + _dot(oa_ref[...], w_ref[0:half, :])
                      + _dot_tn(obt_ref[...], w_ref[half:2 * half, :]))

    return pl.pallas_call(
        body, name="out_proj", grid=(t // tm,),
        in_specs=[pl.BlockSpec((tm, d), lambda i: (i, 0)),
                  pl.BlockSpec((tm, half), lambda i: (i, 0)),
                  pl.BlockSpec((half, tm), lambda i: (0, i)),
                  pl.BlockSpec((2 * half, d), lambda i: (0, 0))],
        out_specs=pl.BlockSpec((tm, d), lambda i: (i, 0)),
        out_shape=jax.ShapeDtypeStruct((t, d), F32),
        compiler_params=_params(("parallel",)),
    )(x, oa, ob_t, w)


def out_proj_bwd(dx, w):
    t, d = dx.shape
    half = w.shape[0] // 2
    tm = 512

    def body(dx_ref, w_ref, da_ref, db_ref, dbt_ref):
        dxb = dx_ref[...].astype(BF16)
        da_ref[...] = _dot_nt(dxb, w_ref[0:half, :]).astype(BF16)
        db_ref[...] = _dot_nt(dxb, w_ref[half:2 * half, :]).astype(BF16)
        dbt_ref[...] = _dot_nt(w_ref[half:2 * half, :], dxb).astype(BF16)

    return pl.pallas_call(
        body, name="out_proj_bwd", grid=(t // tm,),
        in_specs=[pl.BlockSpec((tm, d), lambda i: (i, 0)),
                  pl.BlockSpec((2 * half, d), lambda i: (0, 0))],
        out_specs=[pl.BlockSpec((tm, half), lambda i: (i, 0)),
                   pl.BlockSpec((tm, half), lambda i: (i, 0)),
                   pl.BlockSpec((half, tm), lambda i: (0, i))],
        out_shape=[jax.ShapeDtypeStruct((t, half), BF16), jax.ShapeDtypeStruct((t, half), BF16),
                   jax.ShapeDtypeStruct((half, t), BF16)],
        compiler_params=_params(("parallel",)),
    )(dx, w)


def _lane(shape):
    return lax.broadcasted_iota(jnp.int32, shape, 1)


def _seg_sum(z, mask):
    return jnp.sum(jnp.where(mask, z, 0.0), axis=-1, keepdims=True)


def _pair_norm(x):
    lo = _lane(x.shape) < HD
    x2 = x * x
    r = jnp.where(lo, lax.rsqrt(_seg_sum(x2, lo) * (1.0 / HD) + EPS),
                  lax.rsqrt(_seg_sum(x2, ~lo) * (1.0 / HD) + EPS))
    return lo, r, x * r


def _pair_norm_bwd(lo, r, xn, dyg):
    z = dyg * xn
    mean = jnp.where(lo, _seg_sum(z, lo), _seg_sum(z, ~lo)) * (1.0 / HD)
    return r * (dyg - xn * mean)


A_TM = 256


def prep1_fwd(proj, gq, gk, gcq, gckv):
    t = proj.shape[0]
    tm = A_TM

    def body(p_ref, gq_ref, gk_ref, gcq_ref, gckv_ref, qa_ref, ka_ref, va_ref, cq_ref, ckv_ref):
        for p in range(4):
            sl = slice(LANES * p, LANES * (p + 1))
            _, _, xn = _pair_norm(p_ref[:, sl])
            qa_ref[:, sl] = (xn * gq_ref[:, sl] * 0.125).astype(BF16)
            _, _, xn = _pair_norm(p_ref[:, 512 + LANES * p:512 + LANES * (p + 1)])
            ka_ref[:, sl] = (xn * gk_ref[:, sl]).astype(BF16)
        va_ref[...] = p_ref[:, 1024:1536].astype(BF16)
        _, xn = _rms(p_ref[:, 1536:1792])
        cq_ref[...] = (xn * gcq_ref[...]).astype(BF16)
        _, xn = _rms(p_ref[:, 1792:1920])
        ckv_ref[...] = (xn * gckv_ref[...]).astype(BF16)

    row = lambda w: pl.BlockSpec((tm, w), lambda i: (i, 0))
    vec = lambda w: pl.BlockSpec((1, w), lambda i: (0, 0))
    return pl.pallas_call(
        body, name="prep1_fwd", grid=(t // tm,),
        in_specs=[row(PROJ_W), vec(512), vec(512), vec(256), vec(128)],
        out_specs=[row(512), row(512), row(512), row(256), row(128)],
        out_shape=[jax.ShapeDtypeStruct((t, w), BF16) for w in (512, 512, 512, 256, 128)],
        compiler_params=_params(("parallel",)),
    )(proj, gq, gk, gcq, gckv)


def prep1_bwd(proj, dqa, dkp, dvp, dcq, dckv, dkr, gq, gk, gcq, gckv):
    t = proj.shape[0]
    tm = A_TM
    nb = t // tm

    def body(p_ref, dqa_ref, dk0_ref, dk1_ref, dk2_ref, dv0_ref, dv1_ref, dv2_ref, dcq_ref, dckv_ref, dkr_ref,
             gq_ref, gk_ref, gcq_ref, gckv_ref, dp_ref, dgq_ref, dgk_ref, dgcq_ref, dgckv_ref):
        i = pl.program_id(0)

        @pl.when(i == 0)
        def _():
            dgq_ref[...] = jnp.zeros_like(dgq_ref)
            dgk_ref[...] = jnp.zeros_like(dgk_ref)
            dgcq_ref[...] = jnp.zeros_like(dgcq_ref)
            dgckv_ref[...] = jnp.zeros_like(dgckv_ref)

        has1 = (i + 1 < nb).astype(F32)
        has2 = (i + 2 < nb).astype(F32)
        for p in range(4):
            sl = slice(LANES * p, LANES * (p + 1))
            lo, r, xn = _pair_norm(p_ref[:, sl])
            dy = dqa_ref[:, sl] * 0.125
            dp_ref[:, sl] = _pair_norm_bwd(lo, r, xn, dy * gq_ref[:, sl]).astype(BF16)
            dgq_ref[:, sl] += jnp.sum(dy * xn, axis=0, keepdims=True)
            ks = slice(512 + LANES * p, 512 + LANES * (p + 1))
            lo, r, xn = _pair_norm(p_ref[:, ks])
            dy = dk0_ref[0, :, sl] + has1 * dk1_ref[0, :, sl] + has2 * dk2_ref[0, :, sl]
            dp_ref[:, ks] = _pair_norm_bwd(lo, r, xn, dy * gk_ref[:, sl]).astype(BF16)
            dgk_ref[:, sl] += jnp.sum(dy * xn, axis=0, keepdims=True)
        dp_ref[:, 1024:1536] = (dv0_ref[0] + has1 * dv1_ref[0] + has2 * dv2_ref[0]).astype(BF16)
        for (a, b, d_ref, g_ref, dg_ref) in ((1536, 1792, dcq_ref, gcq_ref, dgcq_ref),
                                             (1792, 1920, dckv_ref, gckv_ref, dgckv_ref)):
            r, xn = _rms(p_ref[:, a:b])
            dy = d_ref[...]
            dyg = dy * g_ref[...]
            dp_ref[:, a:b] = (r * (dyg - xn * jnp.mean(dyg * xn, axis=-1, keepdims=True))).astype(BF16)
            dg_ref[...] += jnp.sum(dy * xn, axis=0, keepdims=True)
        dp_ref[:, 1920:2048] = dkr_ref[...].astype(BF16)

    row = lambda w: pl.BlockSpec((tm, w), lambda i: (i, 0))
    vec = lambda w: pl.BlockSpec((1, w), lambda i: (0, 0))
    part = lambda s: pl.BlockSpec((1, tm, 512), lambda i: (s, jnp.minimum(i + s, nb - 1), 0))
    return pl.pallas_call(
        body, name="prep1_bwd", grid=(nb,),
        in_specs=[row(PROJ_W), row(512), part(0), part(1), part(2), part(0), part(1), part(2),
                  row(256), row(128), row(128), vec(512), vec(512), vec(256), vec(128)],
        out_specs=[row(PROJ_W), vec(512), vec(512), vec(256), vec(128)],
        out_shape=[jax.ShapeDtypeStruct((t, PROJ_W), BF16)] + [jax.ShapeDtypeStruct((1, w), F32) for w in (512, 512, 256, 128)],
        compiler_params=_params(("arbitrary",)),
    )(proj, dqa, dkp, dkp, dkp, dvp, dvp, dvp, dcq, dckv, dkr, gq, gk, gcq, gckv)


def _roll(x, shift):
    return pltpu.roll(x, shift % LANES, 1)


def _rope(y, c, s1, s2):
    return y * c + _roll(y, -16) * s1 + _roll(y, 16) * s2


def _rope_bwd(d, c, s1, s2):
    return d * c + _roll(d * s1, 16) + _roll(d * s2, -16)


def _q_head_stats(x):
    lane = _lane(x.shape)
    mn = lane < HD
    mr = (lane >= HD) & (lane < HD + ROPE)
    x2 = x * x
    r = jnp.where(mn, lax.rsqrt(_seg_sum(x2, mn) * (1.0 / HD) + EPS),
                  lax.rsqrt(_seg_sum(x2, mr) * (1.0 / ROPE) + EPS))
    return mn, mr, r, x * r


def _kr_stats(x):
    r = lax.rsqrt(jnp.sum(x * x, axis=-1, keepdims=True) * (1.0 / ROPE) + EPS)
    return r, x * r


def prep2_fwd(qlat, kv, proj, gq, gk, gkr, tabs):
    t = qlat.shape[0]
    tm = A_TM

    def body(q_ref, kv_ref, kr_ref, gq_ref, gk_ref, gkr_ref, tab_ref, qf_ref, kf_ref, vp_ref):
        _, xn = _kr_stats(kr_ref[...])
        kpe = _roll(_rope(xn * gkr_ref[...], tab_ref[3], tab_ref[4], tab_ref[5]), 64)
        for h in range(HEADS):
            sl = slice(LANES * h, LANES * (h + 1))
            _, _, _, xn = _q_head_stats(q_ref[:, sl])
            qf_ref[:, sl] = _rope(xn * gq_ref[...], tab_ref[0], tab_ref[1], tab_ref[2]).astype(BF16)
            x = kv_ref[:, sl]
            lo = _lane(x.shape) < HD
            xk = jnp.where(lo, x, 0.0)
            rk = lax.rsqrt(jnp.sum(xk * xk, axis=-1, keepdims=True) * (1.0 / HD) + EPS)
            kf_ref[:, sl] = (xk * rk * gk_ref[...] + kpe).astype(BF16)
            if h % 2 == 0:
                v_even = _roll(x, 64)
            else:
                vp_ref[:, LANES * (h // 2):LANES * (h // 2 + 1)] = jnp.where(lo, v_even, x).astype(BF16)

    row = lambda w: pl.BlockSpec((tm, w), lambda i: (i, 0))
    vec = lambda w: pl.BlockSpec((1, w), lambda i: (0, 0))
    return pl.pallas_call(
        body, name="prep2_fwd", grid=(t // tm,),
        in_specs=[row(1024), row(1024), pl.BlockSpec((tm, LANES), lambda i: (i, 15)), vec(128), vec(128), vec(128),
                  pl.BlockSpec((6, tm, LANES), lambda i: (0, i, 0))],
        out_specs=[row(1024), row(1024), row(512)],
        out_shape=[jax.ShapeDtypeStruct((t, 1024), BF16), jax.ShapeDtypeStruct((t, 1024), BF16),
                   jax.ShapeDtypeStruct((t, 512), BF16)],
        compiler_params=_params(("parallel",)),
    )(qlat, kv, proj, gq, gk, gkr, tabs)


def prep2_bwd(qlat, kv, proj, dqf, dkf, dvp, gq, gk, gkr, tabs):
    t = qlat.shape[0]
    tm = A_TM

    def body(q_ref, kv_ref, kr_ref, dqf_ref, dkf_ref, dvp_ref, gq_ref, gk_ref, gkr_ref, tab_ref,
             dq_ref, dkv_ref, dkr_ref, dgq_ref, dgk_ref, dgkr_ref):
        i = pl.program_id(0)

        @pl.when(i == 0)
        def _():
            dgq_ref[...] = jnp.zeros_like(dgq_ref)
            dgk_ref[...] = jnp.zeros_like(dgk_ref)
            dgkr_ref[...] = jnp.zeros_like(dgkr_ref)

        dgq = jnp.zeros((1, LANES), F32)
        dgk = jnp.zeros((1, LANES), F32)
        dkpe = jnp.zeros((tm, LANES), F32)
        for h in range(HEADS):
            sl = slice(LANES * h, LANES * (h + 1))
            mn, mr, r, xn = _q_head_stats(q_ref[:, sl])
            dy = _rope_bwd(dqf_ref[sl, :].T, tab_ref[0], tab_ref[1], tab_ref[2])
            dyg = dy * gq_ref[...]
            z = dyg * xn
            mean = jnp.where(mn, _seg_sum(z, mn) * (1.0 / HD), _seg_sum(z, mr) * (1.0 / ROPE))
            dq_ref[:, sl] = (r * (dyg - xn * mean)).astype(BF16)
            dgq = dgq + jnp.sum(dy * xn, axis=0, keepdims=True)

            x = kv_ref[:, sl]
            dk = dkf_ref[:, sl]
            xk = jnp.where(mn, x, 0.0)
            rk = lax.rsqrt(jnp.sum(xk * xk, axis=-1, keepdims=True) * (1.0 / HD) + EPS)
            xkn = xk * rk
            dyk = jnp.where(mn, dk, 0.0)
            dykg = dyk * gk_ref[...]
            dxk = rk * (dykg - xkn * (jnp.sum(dykg * xkn, axis=-1, keepdims=True) * (1.0 / HD)))
            dgk = dgk + jnp.sum(dyk * xkn, axis=0, keepdims=True)
            dkpe = dkpe + jnp.where(mr, dk, 0.0)
            dvpair = dvp_ref[:, LANES * (h // 2):LANES * (h // 2 + 1)]
            dv = _roll(dvpair, 64) if h % 2 == 0 else dvpair
            dkv_ref[:, sl] = jnp.where(mn, dxk, dv).astype(BF16)

        r, xn = _kr_stats(kr_ref[...])
        dy = _rope_bwd(_roll(dkpe, 64), tab_ref[3], tab_ref[4], tab_ref[5])
        dyg = dy * gkr_ref[...]
        dkr_ref[...] = r * (dyg - xn * (jnp.sum(dyg * xn, axis=-1, keepdims=True) * (1.0 / ROPE)))
        dgq_ref[...] += dgq
        dgk_ref[...] += dgk
        dgkr_ref[...] += jnp.sum(dy * xn, axis=0, keepdims=True)

    row = lambda w: pl.BlockSpec((tm, w), lambda i: (i, 0))
    vec = lambda w: pl.BlockSpec((1, w), lambda i: (0, 0))
    return pl.pallas_call(
        body, name="prep2_bwd", grid=(t // tm,),
        in_specs=[row(1024), row(1024), pl.BlockSpec((tm, LANES), lambda i: (i, 15)),
                  pl.BlockSpec((1024, tm), lambda i: (0, i)), row(1024), row(512),
                  vec(128), vec(128), vec(128), pl.BlockSpec((6, tm, LANES), lambda i: (0, i, 0))],
        out_specs=[row(1024), row(1024), row(128), vec(128), vec(128), vec(128)],
        out_shape=[jax.ShapeDtypeStruct((t, 1024), BF16), jax.ShapeDtypeStruct((t, 1024), BF16),
                   jax.ShapeDtypeStruct((t, LANES), F32)] + [jax.ShapeDtypeStruct((1, LANES), F32)] * 3,
        compiler_params=_params(("arbitrary",)),
    )(qlat, kv, proj, dqf, dkf, dvp, gq, gk, gkr, tabs)


A_TQ = 256
A_WIN = 3 * A_TQ


def _a_specs(t):
    nb = t // A_TQ
    blk = lambda s: pl.BlockSpec((A_TQ, 512), lambda i: (jnp.maximum(i - s, 0), 0))
    return nb, blk


def _a_probs(qb, q_ref, kc, b_ref, head, sl, lo):
    hm = lo if head % 2 == 0 else ~lo
    qm = jnp.where(hm, q_ref[:, sl], jnp.zeros((), BF16))
    s = _dot_nt(qm, kc) + b_ref[head]
    col = lax.broadcasted_iota(jnp.int32, s.shape, 1)
    s = jnp.where(col >= 2 * A_TQ - A_TQ * qb, s, NEG)
    e = jnp.exp(s - jnp.max(s, axis=-1, keepdims=True))
    pr = e * (1.0 / jnp.sum(e, axis=-1, keepdims=True))
    return hm, qm, pr


def attn_a_fwd(qa, ka, va, bias):
    t = qa.shape[0]
    nb, blk = _a_specs(t)

    def body(q_ref, k2_ref, k1_ref, k0_ref, v2_ref, v1_ref, v0_ref, b_ref, o_ref):
        qb = pl.program_id(0)
        lo = _lane((A_TQ, LANES)) < HD
        for p in range(4):
            sl = slice(LANES * p, LANES * (p + 1))
            kc = jnp.concatenate([k2_ref[:, sl], k1_ref[:, sl], k0_ref[:, sl]], axis=0)
            vc = jnp.concatenate([v2_ref[:, sl], v1_ref[:, sl], v0_ref[:, sl]], axis=0)
            outs = []
            for h2 in range(2):
                _, _, pr = _a_probs(qb, q_ref, kc, b_ref, 2 * p + h2, sl, lo)
                outs.append(_dot(pr.astype(BF16), vc))
            o_ref[:, sl] = jnp.where(lo, outs[0], outs[1]).astype(BF16)

    return pl.pallas_call(
        body, name="attn_a_fwd", grid=(nb,),
        in_specs=[blk(0), blk(2), blk(1), blk(0), blk(2), blk(1), blk(0),
                  pl.BlockSpec((HEADS, A_TQ, A_WIN), lambda i: (0, 0, 0))],
        out_specs=pl.BlockSpec((A_TQ, 512), lambda i: (i, 0)),
        out_shape=jax.ShapeDtypeStruct((t, 512), BF16),
        compiler_params=_params(("parallel",)),
    )(qa, ka, ka, ka, va, va, va, bias)


def attn_a_bwd(qa, ka, va, bias, do):
    t = qa.shape[0]
    nb, blk = _a_specs(t)

    def body(q_ref, k2_ref, k1_ref, k0_ref, v2_ref, v1_ref, v0_ref, b_ref, do_ref, dq_ref, dk_ref, dv_ref, db_ref):
        qb = pl.program_id(0)

        @pl.when(qb == 0)
        def _():
            db_ref[...] = jnp.zeros_like(db_ref)

        lo = _lane((A_TQ, LANES)) < HD
        for p in range(4):
            sl = slice(LANES * p, LANES * (p + 1))
            kc = jnp.concatenate([k2_ref[:, sl], k1_ref[:, sl], k0_ref[:, sl]], axis=0)
            vc = jnp.concatenate([v2_ref[:, sl], v1_ref[:, sl], v0_ref[:, sl]], axis=0)
            dqs = []
            dkc = jnp.zeros((A_WIN, LANES), F32)
            dvc = jnp.zeros((A_WIN, LANES), F32)
            for h2 in range(2):
                head = 2 * p + h2
                hm, qm, pr = _a_probs(qb, q_ref, kc, b_ref, head, sl, lo)
                dom = jnp.where(hm, do_ref[:, sl], jnp.zeros((), BF16))
                dp = _dot_nt(dom, vc)
                ds = pr * (dp - jnp.sum(pr * dp, axis=-1, keepdims=True))
                db_ref[head] += ds
                dsb = ds.astype(BF16)
                dqs.append(_dot(dsb, kc))
                dkc = dkc + _dot_tn(dsb, qm)
                dvc = dvc + _dot_tn(pr.astype(BF16), dom)
            dq_ref[:, sl] = jnp.where(lo, dqs[0], dqs[1])
            for s in range(3):
                rows = slice(A_TQ * (2 - s), A_TQ * (3 - s))
                dk_ref[s, :, sl] = dkc[rows]
                dv_ref[s, :, sl] = dvc[rows]

    share = pl.BlockSpec((3, A_TQ, 512), lambda i: (0, i, 0))
    return pl.pallas_call(
        body, name="attn_a_bwd", grid=(nb,),
        in_specs=[blk(0), blk(2), blk(1), blk(0), blk(2), blk(1), blk(0),
                  pl.BlockSpec((HEADS, A_TQ, A_WIN), lambda i: (0, 0, 0)), blk(0)],
        out_specs=[pl.BlockSpec((A_TQ, 512), lambda i: (i, 0)), share, share,
                   pl.BlockSpec((HEADS, A_TQ, A_WIN), lambda i: (0, 0, 0))],
        out_shape=[jax.ShapeDtypeStruct((t, 512), F32), jax.ShapeDtypeStruct((3, t, 512), F32),
                   jax.ShapeDtypeStruct((3, t, 512), F32), jax.ShapeDtypeStruct((HEADS, A_TQ, A_WIN), F32)],
        compiler_params=_params(("arbitrary",)),
    )(qa, ka, ka, ka, va, va, va, bias, do)


B_T = 512


B_SCALE2 = B_SCALE * 1.4426950408889634


def _tri_tables(n, by_query):
    pairs = [(i, j) for i in range(n) for j in range(i + 1)] if by_query else [(i, j) for j in range(n) for i in range(j, n)]
    return (np.asarray([p[0] for p in pairs], np.int32), np.asarray([p[1] for p in pairs], np.int32))


def _b_mask_t(s):
    kc = lax.broadcasted_iota(jnp.int32, s.shape, 0) // CHUNK
    qc = lax.broadcasted_iota(jnp.int32, s.shape, 1) // CHUNK
    return jnp.where(kc <= qc, s, NEG)


def attn_b_fwd(qf, kf, vp):
    t = qf.shape[0]
    n = t // B_T
    qtab, ktab = _tri_tables(n, by_query=True)

    def body(qt_ref, kt_ref, q_ref, k_ref, v_ref, o_ref, lse_ref, m_s, l_s, acc_s):
        qb, kb = qt_ref[pl.program_id(1)], kt_ref[pl.program_id(1)]

        @pl.when(kb == 0)
        def _():
            m_s[...] = jnp.full_like(m_s, NEG)
            l_s[...] = jnp.zeros_like(l_s)
            acc_s[...] = jnp.zeros_like(acc_s)

        def step(masked):
            v = v_ref[...]
            for h2 in range(2):
                sl = slice(LANES * h2, LANES * (h2 + 1))
                s = _dot_nt(k_ref[:, sl], q_ref[:, sl]) * B_SCALE2
                if masked:
                    s = _b_mask_t(s)
                m_prev = m_s[h2]
                m_new = jnp.maximum(m_prev, jnp.max(s, axis=0, keepdims=True))
                alpha = jnp.exp2(m_prev - m_new)
                pr = jnp.exp2(s - m_new)
                l_s[h2] = alpha * l_s[h2] + jnp.sum(pr, axis=0, keepdims=True)
                acc_s[h2] = alpha * acc_s[h2] + _dot_tn(v, pr.astype(BF16))
                m_s[h2] = m_new

        @pl.when(kb < qb)
        def _():
            step(False)

        @pl.when(kb == qb)
        def _():
            step(True)
            for h2 in range(2):
                l = l_s[h2]
                rows = slice(HD * h2, HD * (h2 + 1))
                o_ref[rows, :] = (acc_s[h2, rows, :] * (1.0 / l)).astype(BF16)
                lse_ref[0, h2:h2 + 1, :] = m_s[h2] + jnp.log2(l)

    grid_spec = pltpu.PrefetchScalarGridSpec(
        num_scalar_prefetch=2, grid=(4, len(qtab)),
        in_specs=[pl.BlockSpec((B_T, 256), lambda p, s, qt, kt: (qt[s], p)),
                  pl.BlockSpec((B_T, 256), lambda p, s, qt, kt: (kt[s], p)),
                  pl.BlockSpec((B_T, LANES), lambda p, s, qt, kt: (kt[s], p))],
        out_specs=[pl.BlockSpec((LANES, B_T), lambda p, s, qt, kt: (p, qt[s])),
                   pl.BlockSpec((1, 2, B_T), lambda p, s, qt, kt: (p, 0, qt[s]))],
        scratch_shapes=[pltpu.VMEM((2, 1, B_T), F32), pltpu.VMEM((2, 1, B_T), F32), pltpu.VMEM((2, LANES, B_T), F32)])
    return pl.pallas_call(
        body, name="attn_b_fwd", grid_spec=grid_spec,
        out_shape=[jax.ShapeDtypeStruct((512, t), BF16), jax.ShapeDtypeStruct((4, 2, t), F32)],
        compiler_params=_params(("parallel", "arbitrary")),
    )(jnp.asarray(qtab), jnp.asarray(ktab), qf, kf, vp)


def attn_b_bwd(qf, kf, vp, do, do_t, o_t, lse):
    t = qf.shape[0]
    n = t // B_T
    qtab, ktab = _tri_tables(n, by_query=False)

    def body(qt_ref, kt_ref, q_ref, k_ref, v_ref, do_ref, dot_ref, ot_ref, lse_ref, dq_ref, dk_ref, dv_ref):
        qb, kb = qt_ref[pl.program_id(1)], kt_ref[pl.program_id(1)]

        @pl.when(pl.program_id(1) == 0)
        def _():
            dq_ref[...] = jnp.zeros_like(dq_ref)

        @pl.when(qb == kb)
        def _():
            dk_ref[...] = jnp.zeros_like(dk_ref)
            dv_ref[...] = jnp.zeros_like(dv_ref)

        def step(masked):
            cols = pl.ds(pl.multiple_of(qb * B_T, B_T), B_T)
            v = v_ref[...]
            dov = do_ref[...]
            prod = dot_ref[...].astype(F32) * ot_ref[...].astype(F32)
            lo = _lane((B_T, LANES)) < HD
            for h2 in range(2):
                sl = slice(LANES * h2, LANES * (h2 + 1))
                hm = lo if h2 == 0 else ~lo
                q = q_ref[:, sl]
                k = k_ref[:, sl]
                dom = jnp.where(hm, dov, jnp.zeros((), BF16))
                delta = jnp.sum(prod[HD * h2:HD * (h2 + 1), :], axis=0, keepdims=True)
                s = _dot_nt(k, q) * B_SCALE2
                if masked:
                    s = _b_mask_t(s)
                pr = jnp.exp2(s - lse_ref[0, h2:h2 + 1, :])
                dp = _dot_nt(v, dom)
                ds = (pr * (dp - delta) * B_SCALE).astype(BF16)
                dk_ref[:, sl] += _dot(ds, q)
                dv_ref[...] += _dot(pr.astype(BF16), dom)
                dq_ref[sl, cols] += _dot_tn(k, ds)

        @pl.when(qb > kb)
        def _():
            step(False)

        @pl.when(qb == kb)
        def _():
            step(True)

    qrow = lambda w: pl.BlockSpec((B_T, w), lambda p, s, qt, kt: (qt[s], p))
    qcol = pl.BlockSpec((LANES, B_T), lambda p, s, qt, kt: (p, qt[s]))
    krow = lambda w: pl.BlockSpec((B_T, w), lambda p, s, qt, kt: (kt[s], p))
    grid_spec = pltpu.PrefetchScalarGridSpec(
        num_scalar_prefetch=2, grid=(4, len(qtab)),
        in_specs=[qrow(256), krow(256), krow(LANES), qrow(LANES), qcol, qcol,
                  pl.BlockSpec((1, 2, B_T), lambda p, s, qt, kt: (p, 0, qt[s]))],
        out_specs=[pl.BlockSpec((256, t), lambda p, s, qt, kt: (p, 0)), krow(256), krow(LANES)])
    return pl.pallas_call(
        body, name="attn_b_bwd", grid_spec=grid_spec,
        out_shape=[jax.ShapeDtypeStruct((1024, t), F32), jax.ShapeDtypeStruct((t, 1024), F32),
                   jax.ShapeDtypeStruct((t, 512), F32)],
        compiler_params=_params(("parallel", "arbitrary")),
    )(jnp.asarray(qtab), jnp.asarray(ktab), qf, kf, vp, do, do_t, o_t, lse)


_U_LEN = A_TQ + A_WIN - 1


def _band_mask():
    a = np.arange(A_TQ)[:, None] // CHUNK
    b = np.arange(A_WIN)[None, :] // CHUNK
    return (b >= a) & (b <= a + A_LEFT)


def bias_block(table):
    h = table.shape[0]
    n_lo = A_WIN - 1 - 2 * A_TQ - A_MAX_REL
    ext = jnp.concatenate([jnp.repeat(table[:, :1], n_lo, axis=1), table,
                           jnp.repeat(table[:, -1:], _U_LEN - n_lo - table.shape[1], axis=1)], axis=1)
    row = jnp.pad(ext[:, ::-1], ((0, 0), (0, 1)))
    flat = jnp.tile(row, (1, A_TQ))[:, :A_TQ * _U_LEN]
    skew = flat.reshape(h, A_TQ, _U_LEN)
    toep = skew[:, :, A_TQ - 1:A_TQ - 1 + A_WIN]
    return jnp.where(jnp.asarray(_band_mask())[None], toep, NEG)


def bias_block_grad(db):
    h = db.shape[0]
    n_lo = A_WIN - 1 - 2 * A_TQ - A_MAX_REL
    skew = jnp.pad(db, ((0, 0), (0, 0), (A_TQ - 1, 0)))
    flat = jnp.pad(skew.reshape(h, A_TQ * _U_LEN), ((0, 0), (0, A_TQ)))
    ext = jnp.sum(flat.reshape(h, A_TQ, _U_LEN + 1), axis=1)[:, :_U_LEN][:, ::-1]
    n_tab = 2 * A_MAX_REL + 1
    first = jnp.sum(ext[:, :n_lo + 1], axis=1, keepdims=True)
    last = jnp.sum(ext[:, n_lo + n_tab - 1:], axis=1, keepdims=True)
    return jnp.concatenate([first, ext[:, n_lo + 1:n_lo + n_tab - 1], last], axis=1)


def rope_tabs(t):
    inv = 1.0 / (10000.0 ** (jnp.arange(0, ROPE, 2, dtype=F32) / ROPE))
    ang = jnp.arange(t, dtype=F32)[:, None] * inv[None, :]
    cos, sin = jnp.cos(ang), jnp.sin(ang)
    z = lambda w: jnp.zeros((t, w), F32)
    ck = jnp.concatenate([cos, cos, z(96)], axis=1)
    s1k = jnp.concatenate([-sin, z(112)], axis=1)
    s2k = jnp.concatenate([z(16), sin, z(96)], axis=1)
    cq = jnp.concatenate([jnp.ones((t, HD), F32), cos, cos, z(32)], axis=1)
    s1q = jnp.concatenate([z(HD), -sin, z(48)], axis=1)
    s2q = jnp.concatenate([z(HD + 16), sin, z(32)], axis=1)
    return jnp.stack([cq, s1q, s2q, ck, s1k, s2k])


def _pad_lanes(v, width):
    return jnp.pad(v, ((0, 0), (0, width - v.shape[1])))


def local_step(x, target, w):
    t = x.shape[0]
    gq = jnp.tile(w["a_q_norm"], (1, HEADS))
    gk = jnp.tile(w["a_k_norm"], (1, HEADS))
    gq128 = _pad_lanes(jnp.concatenate([w["b_q_nope_norm"], w["b_q_rope_norm"]], axis=1), LANES)
    gk128 = _pad_lanes(w["b_k_nope_norm"], LANES)
    gkr128 = _pad_lanes(w["b_k_rope_norm"], LANES)
    tabs = rope_tabs(t)
    bias = bias_block(w["a_rel_bias"])

    x1 = ffn_fwd(x, w["ffn1_norm"], w["ffn1_w_gate"], w["ffn1_w_up"], w["ffn1_w_down"], "ffn_fwd")
    h, proj = mix_proj(x1, w["mix_norm"], w["w_in"])
    qa, ka, va, cqn, ckvn = prep1_fwd(proj, gq, gk, w["b_q_lat_norm"], w["b_kv_lat_norm"])
    qlat = matmul(cqn, w["b_w_uq"], "nn", "uq_fwd", 512, 1024, 256)
    kv = matmul(ckvn, w["b_w_ukv"], "nn", "ukv_fwd", 512, 1024, 128)
    qf, kf, vp = prep2_fwd(qlat, kv, proj, gq128, gk128, gkr128, tabs)
    oa = attn_a_fwd(qa, ka, va, bias)
    ob_t, lse = attn_b_fwd(qf, kf, vp)
    x2 = out_proj(x1, oa, ob_t, w["w_out"])
    x3 = ffn_fwd(x2, w["ffn2_norm"], w["ffn2_w_gate"], w["ffn2_w_up"], w["ffn2_w_down"], "ffn_fwd")

    g = {}
    dx3, g["final_norm"], loss = final_loss(x3, w["final_norm"], target)
    g["ffn2_w_gate"], g["ffn2_w_up"], g["ffn2_w_down"], dhp = ffn_bwd(
        x2, dx3, w["ffn2_norm"], w["ffn2_w_gate"], w["ffn2_w_up"], w["ffn2_w_down"], "ffn_bwd")
    dx2, g["ffn2_norm"] = norm_bwd(x2, w["ffn2_norm"], dhp, dx3, "ffn_norm_bwd")
    d_oa, d_ob, d_ob_t = out_proj_bwd(dx2, w["w_out"])
    g["w_out"] = jnp.concatenate([matmul(oa, dx2, "tn", "w_out_a_bwd", 512, 1024, 512),
                                  matmul(ob_t, dx2, "nn", "w_out_b_bwd", 512, 1024, 512)], axis=0)
    dqf, dkf, dvp = attn_b_bwd(qf, kf, vp, d_ob, d_ob_t, ob_t, lse)
    dqa, dkp, dvpa, dbias = attn_a_bwd(qa, ka, va, bias, d_oa)
    dqlat, dkv, dkr, dgq128, dgk128, dgkr128 = prep2_bwd(qlat, kv, proj, dqf, dkf, dvp, gq128, gk128, gkr128, tabs)
    dcq = matmul(dqlat, w["b_w_uq"], "nt", "uq_bwd_x", 512, 256, 1024)
    g["b_w_uq"] = matmul(cqn, dqlat, "tn", "uq_bwd_w", 256, 1024, 512)
    dckv = matmul(dkv, w["b_w_ukv"], "nt", "ukv_bwd_x", 512, 128, 1024)
    g["b_w_ukv"] = matmul(ckvn, dkv, "tn", "ukv_bwd_w", 128, 1024, 512)
    dproj, dgq, dgk, g["b_q_lat_norm"], g["b_kv_lat_norm"] = prep1_bwd(
        proj, dqa, dkp, dvpa, dcq, dckv, dkr, gq, gk, w["b_q_lat_norm"], w["b_kv_lat_norm"])
    dh = matmul(dproj, w["w_in"], "nt", "w_in_bwd_x", 512, 1024, PROJ_W)
    g["w_in"] = matmul(h, dproj, "tn", "w_in_bwd_w", 1024, 1024, 512)
    dx1, g["mix_norm"] = norm_bwd(x1, w["mix_norm"], dh[None], dx2, "mix_norm_bwd")
    g["ffn1_w_gate"], g["ffn1_w_up"], g["ffn1_w_down"], dhp = ffn_bwd(
        x, dx1, w["ffn1_norm"], w["ffn1_w_gate"], w["ffn1_w_up"], w["ffn1_w_down"], "ffn_bwd")
    grad_x, g["ffn1_norm"] = norm_bwd(x, w["ffn1_norm"], dhp, dx1, "ffn_norm_bwd")

    g["a_q_norm"] = jnp.sum(dgq.reshape(HEADS, HD), axis=0, keepdims=True)
    g["a_k_norm"] = jnp.sum(dgk.reshape(HEADS, HD), axis=0, keepdims=True)
    g["a_rel_bias"] = bias_block_grad(dbias)
    g["b_q_nope_norm"] = dgq128[:, :HD]
    g["b_q_rope_norm"] = dgq128[:, HD:HD + ROPE]
    g["b_k_nope_norm"] = dgk128[:, :HD]
    g["b_k_rope_norm"] = dgkr128[:, :ROPE]
    return loss, grad_x, g


ANY = pl.BlockSpec(memory_space=pl.ANY)
N_DEV = 8


def _place():
    return lax.axis_index("x"), lax.axis_index("y"), lax.axis_index("c")


def _flip(v, bit):
    return 1 - v if bit else v


def allgather_shards(ws):
    n = len(ws)

    def body(*refs):
        ins, outs = refs[:n], refs[n:2 * n]
        send_sems, recv_sems = refs[2 * n:]
        x, y, c = _place()
        s_me = 2 * x + y
        sibling = (x, y, 1 - c)
        chips = [(1 - x, y), (x, 1 - y), (1 - x, 1 - y)]

        def half(a, core):
            hr = ws[a].shape[0] // 2
            return pl.ds(core * hr, hr)

        def remote(k, src, dst, to):
            return pltpu.make_async_remote_copy(src_ref=src, dst_ref=dst, send_sem=send_sems.at[k],
                                                recv_sem=recv_sems.at[k], device_id=to, device_id_type=MESH)

        sends = []
        for a in range(n):
            for j, (cx, cy) in enumerate(chips):
                sends.append(remote(6 * a + j, ins[a].at[half(a, c)], outs[a].at[s_me, half(a, c)], (cx, cy, c)))
                sends[-1].start()
        for a in range(n):
            for j, (cx, cy) in enumerate(chips):
                got = outs[a].at[2 * cx + cy, half(a, c)]
                remote(6 * a + j, got, got, (cx, cy, c)).wait_recv()
                sends.append(remote(6 * a + 3 + j, got, got, sibling))
                sends[-1].start()
        for a in range(n):
            for j, (cx, cy) in enumerate(chips):
                got = outs[a].at[2 * cx + cy, half(a, 1 - c)]
                remote(6 * a + 3 + j, got, got, sibling).wait_recv()
        for cp in sends:
            cp.wait_send()

    return pl.pallas_call(
        body, name="allgather_shards",
        in_specs=[ANY] * n, out_specs=[ANY] * n,
        out_shape=[jax.ShapeDtypeStruct((N_SHARD,) + w.shape, w.dtype) for w in ws],
        scratch_shapes=[pltpu.SemaphoreType.DMA((6 * n,)), pltpu.SemaphoreType.DMA((6 * n,))],
    )(*ws)


def scatter_partials(gs):
    n = len(gs)

    def body(*refs):
        ins, outs = refs[:n], refs[n:2 * n]
        send_sems, recv_sems, local_sems = refs[2 * n:]
        x, y, c = _place()
        me = 4 * x + 2 * y + c

        def piece(a, px, py, pc):
            hr = gs[a].shape[1] // 2
            return ins[a].at[2 * px + py, pl.ds(pc * hr, hr)]

        local, sends = [], []
        for a in range(n):
            local.append(pltpu.make_async_copy(piece(a, x, y, c), outs[a].at[me], local_sems.at[a]))
            local[-1].start()
            for k in range(1, N_DEV):
                px, py, pc = _flip(x, k & 4), _flip(y, k & 2), _flip(c, k & 1)
                sends.append(pltpu.make_async_remote_copy(
                    src_ref=piece(a, px, py, pc), dst_ref=outs[a].at[me], send_sem=send_sems.at[7 * a + k - 1],
                    recv_sem=recv_sems.at[7 * a + k - 1], device_id=(px, py, pc), device_id_type=MESH))
                sends[-1].start()
        for a in range(n):
            for k in range(1, N_DEV):
                px, py, pc = _flip(x, k & 4), _flip(y, k & 2), _flip(c, k & 1)
                slot = outs[a].at[4 * px + 2 * py + pc]
                pltpu.make_async_remote_copy(
                    src_ref=slot, dst_ref=slot, send_sem=send_sems.at[7 * a + k - 1],
                    recv_sem=recv_sems.at[7 * a + k - 1], device_id=(px, py, pc), device_id_type=MESH).wait_recv()
        for cp in sends:
            cp.wait_send()
        for cp in local:
            cp.wait()

    return pl.pallas_call(
        body, name="scatter_partials",
        in_specs=[ANY] * n, out_specs=[ANY] * n,
        out_shape=[jax.ShapeDtypeStruct((N_DEV, g.shape[1] // 2, g.shape[2]), g.dtype) for g in gs],
        scratch_shapes=[pltpu.SemaphoreType.DMA((7 * n,)), pltpu.SemaphoreType.DMA((7 * n,)),
                        pltpu.SemaphoreType.DMA((n,))],
    )(*gs)


def sum_slots(land, name):
    _, rows, cols = land.shape
    tr = rows if rows <= 128 else rows // 2

    def body(l_ref, o_ref):
        acc = l_ref[0].astype(F32)
        for s in range(1, N_DEV):
            acc = acc + l_ref[s].astype(F32)
        o_ref[...] = acc

    return pl.pallas_call(
        body, name=name, grid=(rows // tr,),
        in_specs=[pl.BlockSpec((N_DEV, tr, cols), lambda i: (0, i, 0))],
        out_specs=pl.BlockSpec((tr, cols), lambda i: (i, 0)),
        out_shape=jax.ShapeDtypeStruct((rows, cols), F32),
        compiler_params=_params(("parallel",)),
    )(land)


def join_halves(hs):
    n = len(hs)

    def body(*refs):
        ins, outs = refs[:n], refs[n:2 * n]
        send_sems, recv_sems = refs[2 * n:]
        x, y, c = _place()
        sends = []
        for a in range(n):
            hr = hs[a].shape[0]
            mine = outs[a].at[pl.ds(c * hr, hr)]
            sends.append(pltpu.make_async_remote_copy(
                src_ref=ins[a], dst_ref=mine, send_sem=send_sems.at[a], recv_sem=recv_sems.at[a],
                device_id=(x, y, 1 - c), device_id_type=MESH))
            sends[-1].start()
        for a in range(n):
            hr = hs[a].shape[0]
            theirs = outs[a].at[pl.ds((1 - c) * hr, hr)]
            pltpu.make_async_remote_copy(
                src_ref=theirs, dst_ref=theirs, send_sem=send_sems.at[a], recv_sem=recv_sems.at[a],
                device_id=(x, y, 1 - c), device_id_type=MESH).wait_recv()
        for cp in sends:
            cp.wait_send()

    return pl.pallas_call(
        body, name="join_halves",
        in_specs=[ANY] * n, out_specs=[ANY] * n,
        out_shape=[jax.ShapeDtypeStruct((2 * h.shape[0], h.shape[1]), h.dtype) for h in hs],
        scratch_shapes=[pltpu.SemaphoreType.DMA((n,)), pltpu.SemaphoreType.DMA((n,))],
    )(*hs)


def allreduce_small(vec):
    def body(v_ref, o_ref, land_ref, send_sems, recv_sems):
        x, y, c = _place()
        me = 4 * x + 2 * y + c
        land_ref[me] = v_ref[...]
        sends = []
        for k in range(1, N_DEV):
            px, py, pc = _flip(x, k & 4), _flip(y, k & 2), _flip(c, k & 1)
            sends.append(pltpu.make_async_remote_copy(
                src_ref=v_ref, dst_ref=land_ref.at[me], send_sem=send_sems.at[k - 1], recv_sem=recv_sems.at[k - 1],
                device_id=(px, py, pc), device_id_type=MESH))
            sends[-1].start()
        for k in range(1, N_DEV):
            px, py, pc = _flip(x, k & 4), _flip(y, k & 2), _flip(c, k & 1)
            slot = land_ref.at[4 * px + 2 * py + pc]
            pltpu.make_async_remote_copy(
                src_ref=slot, dst_ref=slot, send_sem=send_sems.at[k - 1], recv_sem=recv_sems.at[k - 1],
                device_id=(px, py, pc), device_id_type=MESH).wait_recv()
        for cp in sends:
            cp.wait_send()
        acc = land_ref[0]
        for s in range(1, N_DEV):
            acc = acc + land_ref[s]
        o_ref[...] = acc

    vm = pl.BlockSpec(memory_space=pltpu.VMEM)
    return pl.pallas_call(
        body, name="allreduce_small",
        in_specs=[vm], out_specs=vm,
        out_shape=jax.ShapeDtypeStruct(vec.shape, F32),
        scratch_shapes=[pltpu.VMEM((N_DEV,) + vec.shape, F32), pltpu.SemaphoreType.DMA((N_DEV - 1,)),
                        pltpu.SemaphoreType.DMA((N_DEV - 1,))],
    )(vec)


def adamw(w, g, m, v, name):
    rows, cols = w.shape
    tr = rows
    while tr * cols * 4 * 14 > 24 * 1024 * 1024 and tr % 16 == 0:
        tr //= 2
    c1 = 1.0 - ADAM_B1 ** ADAM_STEP
    c2 = 1.0 - ADAM_B2 ** ADAM_STEP

    def body(w_ref, g_ref, m_ref, v_ref, d_ref, nm_ref, nv_ref):
        gv = g_ref[...]
        nm = ADAM_B1 * m_ref[...] + (1.0 - ADAM_B1) * gv
        nv = ADAM_B2 * v_ref[...] + (1.0 - ADAM_B2) * (gv * gv)
        nm_ref[...] = nm
        nv_ref[...] = nv
        d_ref[...] = -ADAM_LR * ((nm / c1) / (jnp.sqrt(nv / c2) + ADAM_EPS) + ADAM_WD * w_ref[...])

    blk = pl.BlockSpec((tr, cols), lambda i: (i, 0))
    return pl.pallas_call(
        body, name=name, grid=(rows // tr,),
        in_specs=[blk] * 4, out_specs=[blk] * 3,
        out_shape=[jax.ShapeDtypeStruct((rows, cols), F32)] * 3,
        compiler_params=_params(("parallel",)),
    )(w, g, m, v)


BIG = ("ffn1_w_gate", "ffn1_w_up", "ffn1_w_down", "w_in", "b_w_uq", "b_w_ukv", "w_out",
       "ffn2_w_gate", "ffn2_w_up", "ffn2_w_down")
SMALL = ("ffn1_norm", "mix_norm", "a_q_norm", "a_k_norm", "a_rel_bias", "b_q_lat_norm", "b_kv_lat_norm",
         "b_q_nope_norm", "b_q_rope_norm", "b_k_nope_norm", "b_k_rope_norm", "ffn2_norm", "final_norm")
WEIGHTS = ("ffn1_norm", "ffn1_w_gate", "ffn1_w_up", "ffn1_w_down", "mix_norm", "w_in", "a_q_norm", "a_k_norm",
           "a_rel_bias", "b_q_lat_norm", "b_w_uq", "b_kv_lat_norm", "b_w_ukv", "b_q_nope_norm", "b_q_rope_norm",
           "b_k_nope_norm", "b_k_rope_norm", "w_out", "ffn2_norm", "ffn2_w_gate", "ffn2_w_up", "ffn2_w_down",
           "final_norm")
PACK_SHAPE = (8, 1024)


def _pack_small(d, last=None):
    flat = [d[n].reshape(-1) for n in SMALL]
    used = sum(f.shape[0] for f in flat)
    total = PACK_SHAPE[0] * PACK_SHAPE[1]
    tail = jnp.zeros((total - used - 1,), F32)
    end = jnp.zeros((1,), F32) if last is None else last.reshape(1)
    return jnp.concatenate(flat + [tail, end]).reshape(PACK_SHAPE)


def _unpack_small(p, like):
    flat = p.reshape(-1)
    out, off = {}, 0
    for n in SMALL:
        size = like[n].size
        out[n] = flat[off:off + size].reshape(like[n].shape)
        off += size
    return out, flat[-1]


def _cols_to_shards(g):
    rows, cols = g.shape
    return g.reshape(rows, N_SHARD, cols // N_SHARD).transpose(1, 0, 2)


def _shards_to_cols(g):
    return g.transpose(1, 0, 2).reshape(g.shape[1], -1)


def kernel(x, ffn1_norm, ffn1_w_gate, ffn1_w_up, ffn1_w_down, mix_norm, w_in, a_q_norm, a_k_norm, a_rel_bias, b_q_lat_norm, b_w_uq, b_kv_lat_norm, b_w_ukv, b_q_nope_norm, b_q_rope_norm, b_k_nope_norm, b_k_rope_norm, w_out, ffn2_norm, ffn2_w_gate, ffn2_w_up, ffn2_w_down, final_norm, loss_target, m_ffn1_norm, m_ffn1_w_gate, m_ffn1_w_up, m_ffn1_w_down, m_mix_norm, m_w_in, m_a_q_norm, m_a_k_norm, m_a_rel_bias, m_b_q_lat_norm, m_b_w_uq, m_b_kv_lat_norm, m_b_w_ukv, m_b_q_nope_norm, m_b_q_rope_norm, m_b_k_nope_norm, m_b_k_rope_norm, m_w_out, m_ffn2_norm, m_ffn2_w_gate, m_ffn2_w_up, m_ffn2_w_down, m_final_norm, v_ffn1_norm, v_ffn1_w_gate, v_ffn1_w_up, v_ffn1_w_down, v_mix_norm, v_w_in, v_a_q_norm, v_a_k_norm, v_a_rel_bias, v_b_q_lat_norm, v_b_w_uq, v_b_kv_lat_norm, v_b_w_ukv, v_b_q_nope_norm, v_b_q_rope_norm, v_b_k_nope_norm, v_b_k_rope_norm, v_w_out, v_ffn2_norm, v_ffn2_w_gate, v_ffn2_w_up, v_ffn2_w_down, v_final_norm):
    args = locals()
    wts = {n: args[n][0] for n in WEIGHTS}
    mom = {n: args["m_" + n][0] for n in WEIGHTS}
    var = {n: args["v_" + n][0] for n in WEIGHTS}

    shard = 2 * lax.axis_index("x") + lax.axis_index("y")
    core = lax.axis_index("c")
    own = [wts[n].astype(BF16) for n in BIG]
    gathered = dict(zip(BIG, (lax.dynamic_update_index_in_dim(got, mine, shard, 0)
                              for got, mine in zip(allgather_shards(own), own))))
    w = {n: wts[n] if n == "a_rel_bias" else wts[n][None] for n in SMALL}
    for n in BIG:
        if n.startswith("ffn"):
            w[n] = gathered[n]
    w["w_in"] = jnp.pad(_shards_to_cols(gathered["w_in"]), ((0, 0), (0, PROJ_W - IN_COLS)))
    uq = _shards_to_cols(gathered["b_w_uq"]).reshape(256, HEADS, HD + ROPE)
    w["b_w_uq"] = jnp.pad(uq, ((0, 0), (0, 0), (0, LANES - HD - ROPE))).reshape(256, HEADS * LANES)
    w["b_w_ukv"] = _shards_to_cols(gathered["b_w_ukv"])
    w["w_out"] = gathered["w_out"].reshape(N_SHARD * gathered["w_out"].shape[1], D_MODEL)

    loss, grad_x, g = local_step(x[0], loss_target[0], w)

    part = {n: g[n] for n in BIG if n.startswith("ffn")}
    part["w_in"] = _cols_to_shards(g["w_in"][:, :IN_COLS])
    part["b_w_uq"] = _cols_to_shards(g["b_w_uq"].reshape(256, HEADS, LANES)[:, :, :HD + ROPE].reshape(256, -1))
    part["b_w_ukv"] = _cols_to_shards(g["b_w_ukv"])
    part["w_out"] = g["w_out"].reshape(N_SHARD, -1, D_MODEL)
    landed = scatter_partials([part[n].astype(BF16) for n in BIG])
    halves = [sum_slots(l, "sum_slots") for l in landed]
    grads = dict(zip(BIG, (lax.dynamic_update_slice_in_dim(got, mine, core * mine.shape[0], 0)
                           for got, mine in zip(join_halves(halves), halves))))

    small_sum, loss_sum = _unpack_small(allreduce_small(_pack_small(g, loss[0, 0])), wts)
    grads.update(small_sum)

    delta, new_m, new_v = {}, {}, {}
    for n in BIG:
        delta[n], new_m[n], new_v[n] = adamw(wts[n], grads[n], mom[n], var[n], "adamw")
    packed = adamw(_pack_small(wts), _pack_small(grads), _pack_small(mom), _pack_small(var), "adamw_small")
    for dst, p in zip((delta, new_m, new_v), packed):
        dst.update(_unpack_small(p, wts)[0])

    lead = lambda d: [d[n][None] for n in WEIGHTS]
    return (loss_sum, grad_x[None], *lead(grads), *lead(delta), *lead(new_m), *lead(new_v))
```

```python
import functools
import math

import numpy as np
import jax
import jax.numpy as jnp
from jax import lax
from jax.experimental import pallas as pl
from jax.experimental.pallas import tpu as pltpu

F32 = jnp.float32
BF16 = jnp.bfloat16
EPS = 1e-6
NEG = -1e30

D_MODEL = 1024
D_FF = 2816
N_SHARD = 4
FS = D_FF // N_SHARD
CHUNK = 64
A_LEFT = 8
A_MAX_REL = 128
HEADS = 8
HD = 64
ROPE = 32
PROJ_W = 2048
IN_COLS = 1952
B_SCALE = 96 ** -0.5
LANES = 128

ADAM_LR = 0.001
ADAM_B1 = 0.9
ADAM_B2 = 0.999
ADAM_EPS = 1e-08
ADAM_WD = 0.01
ADAM_STEP = 10

VMEM_LIMIT = 56 * 1024 * 1024

MESH = pl.DeviceIdType.MESH


def _dot(a, b):
    return lax.dot_general(a, b, (((1,), (0,)), ((), ())), preferred_element_type=F32)


def _dot_nt(a, b):
    return lax.dot_general(a, b, (((1,), (1,)), ((), ())), preferred_element_type=F32)


def _dot_tn(a, b):
    return lax.dot_general(a, b, (((0,), (0,)), ((), ())), preferred_element_type=F32)


def _params(sem):
    return pltpu.CompilerParams(dimension_semantics=sem, vmem_limit_bytes=VMEM_LIMIT)


def _rms(xv):
    r = lax.rsqrt(jnp.mean(xv * xv, axis=-1, keepdims=True) + EPS)
    return r, xv * r


def ffn_fwd(x, g, wg, wu, wd, name, gather=()):
    t, d = x.shape
    tm = 512
    ni = t // tm
    plan = GatherPlan(gather)
    n = plan.n

    def body(*refs):
        x_ref, g_ref, wg_ref, wu_ref, wd_ref = refs[:5]
        ins, o_ref, outs = refs[5:5 + n], refs[5 + n], refs[6 + n:6 + 2 * n]
        h_ref, acc_ref = refs[6 + 2 * n:8 + 2 * n]
        sems = refs[8 + 2 * n:]
        i, j = pl.program_id(0), pl.program_id(1)
        if n:
            pl.when((i == 0) & (j == 0))(lambda: plan.start(ins, outs, sems))
            pl.when((i == ni // 2) & (j == 0))(lambda: plan.forward(ins, outs, sems))

        @pl.when(j == 0)
        def _():
            _, xn = _rms(x_ref[...])
            h_ref[...] = (xn * g_ref[...]).astype(BF16)
            acc_ref[...] = jnp.zeros_like(acc_ref)

        h = h_ref[...]
        gp = _dot(h, wg_ref[0])
        up = _dot(h, wu_ref[0])
        a = (gp * jax.nn.sigmoid(gp) * up).astype(BF16)
        acc_ref[...] += _dot(a, wd_ref[0])

        @pl.when(j == N_SHARD - 1)
        def _():
            o_ref[...] = x_ref[...] + 0.5 * acc_ref[...]

        if n:
            pl.when((i == ni - 1) & (j == N_SHARD - 1))(lambda: plan.finish(ins, outs, sems))

    res = pl.pallas_call(
        body, name=name, grid=(ni, N_SHARD),
        in_specs=[pl.BlockSpec((tm, d), lambda i, j: (i, 0)),
                  pl.BlockSpec((1, d), lambda i, j: (0, 0)),
                  pl.BlockSpec((1, d, FS), lambda i, j: (j, 0, 0)),
                  pl.BlockSpec((1, d, FS), lambda i, j: (j, 0, 0)),
                  pl.BlockSpec((1, FS, d), lambda i, j: (j, 0, 0))] + [ANY] * n,
        out_specs=[pl.BlockSpec((tm, d), lambda i, j: (i, 0))] + [ANY] * n,
        out_shape=[jax.ShapeDtypeStruct((t, d), F32)] + plan.out_shape,
        scratch_shapes=[pltpu.VMEM((tm, d), BF16), pltpu.VMEM((tm, d), F32)] + (plan.scratch if n else []),
        compiler_params=_params(("arbitrary", "arbitrary")),
    )(x, g, wg, wu, wd, *gather)
    return res if n else res[0]


def ffn_bwd(x, dout, g, wg, wu, wd, name):
    t, d = x.shape
    tm = 512
    ni = t // tm

    def body(x_ref, do_ref, g_ref, wg_ref, wu_ref, wd_ref, dwg_out, dwu_out, dwd_out, dhp_ref,
             dwg_ref, dwu_ref, dwd_ref):
        i = pl.program_id(1)
        _, xn = _rms(x_ref[...])
        h = (xn * g_ref[...]).astype(BF16)
        dz = (0.5 * do_ref[...]).astype(BF16)
        wgv, wuv, wdv = wg_ref[0], wu_ref[0], wd_ref[0]
        gp = _dot(h, wgv)
        up = _dot(h, wuv)
        s = jax.nn.sigmoid(gp)
        sg = gp * s
        a = (sg * up).astype(BF16)
        da = _dot_nt(dz, wdv)
        dup = (da * sg).astype(BF16)
        dgp = (da * up * (s * (1.0 + gp * (1.0 - s)))).astype(BF16)

        @pl.when(i == 0)
        def _():
            dwg_ref[...] = jnp.zeros_like(dwg_ref)
            dwu_ref[...] = jnp.zeros_like(dwu_ref)
            dwd_ref[...] = jnp.zeros_like(dwd_ref)

        dwd_ref[...] += _dot_tn(a, dz)
        dwg_ref[...] += _dot_tn(h, dgp)
        dwu_ref[...] += _dot_tn(h, dup)
        dhp_ref[0] = _dot_nt(dgp, wgv) + _dot_nt(dup, wuv)

        @pl.when(i == ni - 1)
        def _():
            dwg_out[0] = dwg_ref[...].astype(BF16)
            dwu_out[0] = dwu_ref[...].astype(BF16)
            dwd_out[0] = dwd_ref[...].astype(BF16)

    return pl.pallas_call(
        body, name=name, grid=(N_SHARD, ni),
        in_specs=[pl.BlockSpec((tm, d), lambda j, i: (i, 0)),
                  pl.BlockSpec((tm, d), lambda j, i: (i, 0)),
                  pl.BlockSpec((1, d), lambda j, i: (0, 0)),
                  pl.BlockSpec((1, d, FS), lambda j, i: (j, 0, 0)),
                  pl.BlockSpec((1, d, FS), lambda j, i: (j, 0, 0)),
                  pl.BlockSpec((1, FS, d), lambda j, i: (j, 0, 0))],
        out_specs=[pl.BlockSpec((1, d, FS), lambda j, i: (j, 0, 0)),
                   pl.BlockSpec((1, d, FS), lambda j, i: (j, 0, 0)),
                   pl.BlockSpec((1, FS, d), lambda j, i: (j, 0, 0)),
                   pl.BlockSpec((1, tm, d), lambda j, i: (j, i, 0))],
        out_shape=[jax.ShapeDtypeStruct((N_SHARD, d, FS), BF16),
                   jax.ShapeDtypeStruct((N_SHARD, d, FS), BF16),
                   jax.ShapeDtypeStruct((N_SHARD, FS, d), BF16),
                   jax.ShapeDtypeStruct((N_SHARD, t, d), F32)],
        scratch_shapes=[pltpu.VMEM((d, FS), F32), pltpu.VMEM((d, FS), F32), pltpu.VMEM((FS, d), F32)],
        compiler_params=_params(("arbitrary", "arbitrary")),
    )(x, dout, g, wg, wu, wd)


def norm_bwd(x, g, dhp, dres, name):
    t, d = x.shape
    p = dhp.shape[0]
    tm = 512

    def body(x_ref, g_ref, dhp_ref, dres_ref, dx_ref, dg_ref):
        i = pl.program_id(0)
        r, xn = _rms(x_ref[...])
        dh = dhp_ref[0]
        for q in range(1, p):
            dh = dh + dhp_ref[q]
        dhg = dh * g_ref[...]
        dx_ref[...] = dres_ref[...] + r * (dhg - xn * jnp.mean(dhg * xn, axis=-1, keepdims=True))

        @pl.when(i == 0)
        def _():
            dg_ref[...] = jnp.zeros_like(dg_ref)

        dg_ref[...] += jnp.sum(dh * xn, axis=0, keepdims=True)

    return pl.pallas_call(
        body, name=name, grid=(t // tm,),
        in_specs=[pl.BlockSpec((tm, d), lambda i: (i, 0)),
                  pl.BlockSpec((1, d), lambda i: (0, 0)),
                  pl.BlockSpec((p, tm, d), lambda i: (0, i, 0)),
                  pl.BlockSpec((tm, d), lambda i: (i, 0))],
        out_specs=[pl.BlockSpec((tm, d), lambda i: (i, 0)),
                   pl.BlockSpec((1, d), lambda i: (0, 0))],
        out_shape=[jax.ShapeDtypeStruct((t, d), F32), jax.ShapeDtypeStruct((1, d), F32)],
        compiler_params=_params(("arbitrary",)),
    )(x, g, dhp, dres)


def final_loss(x, g, target):
    t, d = x.shape
    tm = 512

    def body(x_ref, g_ref, t_ref, dx_ref, dg_ref, loss_ref):
        i = pl.program_id(0)
        r, xn = _rms(x_ref[...])
        gv = g_ref[...]
        e = xn * gv - t_ref[...]
        dy = e * (1.0 / d)
        dhg = dy * gv
        dx_ref[...] = r * (dhg - xn * jnp.mean(dhg * xn, axis=-1, keepdims=True))

        @pl.when(i == 0)
        def _():
            dg_ref[...] = jnp.zeros_like(dg_ref)
            loss_ref[...] = jnp.zeros_like(loss_ref)

        dg_ref[...] += jnp.sum(dy * xn, axis=0, keepdims=True)
        part = jnp.sum(jnp.sum(e * e, axis=-1, keepdims=True), axis=0, keepdims=True) * (0.5 / d)
        loss_ref[...] += jnp.broadcast_to(part, loss_ref.shape)

    return pl.pallas_call(
        body, name="final_loss", grid=(t // tm,),
        in_specs=[pl.BlockSpec((tm, d), lambda i: (i, 0)),
                  pl.BlockSpec((1, d), lambda i: (0, 0)),
                  pl.BlockSpec((tm, d), lambda i: (i, 0))],
        out_specs=[pl.BlockSpec((tm, d), lambda i: (i, 0)),
                   pl.BlockSpec((1, d), lambda i: (0, 0)),
                   pl.BlockSpec((1, LANES), lambda i: (0, 0))],
        out_shape=[jax.ShapeDtypeStruct((t, d), F32), jax.ShapeDtypeStruct((1, d), F32),
                   jax.ShapeDtypeStruct((1, LANES), F32)],
        compiler_params=_params(("arbitrary",)),
    )(x, g, target)


def mix_proj(x, g, w):
    t, d = x.shape
    n = w.shape[1]
    tm = 512

    def body(x_ref, g_ref, w_ref, h_ref, p_ref):
        _, xn = _rms(x_ref[...])
        h = (xn * g_ref[...]).astype(BF16)
        h_ref[...] = h
        p_ref[...] = _dot(h, w_ref[...])

    return pl.pallas_call(
        body, name="mix_proj", grid=(t // tm,),
        in_specs=[pl.BlockSpec((tm, d), lambda i: (i, 0)),
                  pl.BlockSpec((1, d), lambda i: (0, 0)),
                  pl.BlockSpec((d, n), lambda i: (0, 0))],
        out_specs=[pl.BlockSpec((tm, d), lambda i: (i, 0)),
                   pl.BlockSpec((tm, n), lambda i: (i, 0))],
        out_shape=[jax.ShapeDtypeStruct((t, d), BF16), jax.ShapeDtypeStruct((t, n), F32)],
        compiler_params=_params(("parallel",)),
    )(x, g, w)


def matmul(a, b, mode, name, tm, tn, tk, out_dtype=F32):
    if mode == "nn":
        (m, k), n = a.shape, b.shape[1]
        a_spec = pl.BlockSpec((tm, tk), lambda i, j, q: (i, q))
        b_spec = pl.BlockSpec((tk, tn), lambda i, j, q: (q, j))
        dot = _dot
    elif mode == "nt":
        (m, k), n = a.shape, b.shape[0]
        a_spec = pl.BlockSpec((tm, tk), lambda i, j, q: (i, q))
        b_spec = pl.BlockSpec((tn, tk), lambda i, j, q: (j, q))
        dot = _dot_nt
    else:
        (k, m), n = a.shape, b.shape[1]
        a_spec = pl.BlockSpec((tk, tm), lambda i, j, q: (q, i))
        b_spec = pl.BlockSpec((tk, tn), lambda i, j, q: (q, j))
        dot = _dot_tn
    assert m % tm == 0 and n % tn == 0 and k % tk == 0, (m, n, k, tm, tn, tk)
    nk = k // tk

    def body(a_ref, b_ref, o_ref, acc_ref):
        q = pl.program_id(2)

        @pl.when(q == 0)
        def _():
            acc_ref[...] = jnp.zeros_like(acc_ref)

        acc_ref[...] += dot(a_ref[...].astype(BF16), b_ref[...].astype(BF16))

        @pl.when(q == nk - 1)
        def _():
            o_ref[...] = acc_ref[...].astype(out_dtype)

    return pl.pallas_call(
        body, name=name, grid=(m // tm, n // tn, nk),
        in_specs=[a_spec, b_spec],
        out_specs=pl.BlockSpec((tm, tn), lambda i, j, q: (i, j)),
        out_shape=jax.ShapeDtypeStruct((m, n), out_dtype),
        scratch_shapes=[pltpu.VMEM((tm, tn), F32)],
        compiler_params=_params(("parallel", "parallel", "arbitrary")),
    )(a, b)


def out_proj(x, oa, ob_t, w):
    t, d = x.shape
    half = oa.shape[1]
    tm = 512

    def body(x_ref, oa_ref, obt_ref, w_ref, o_ref):
        o_ref[...] = (x_ref[...] + _dot(oa_ref[...], w_ref[0:half, :])
                      + _dot_tn(obt_ref[...], w_ref[half:2 * half, :]))

    return pl.pallas_call(
        body, name="out_proj", grid=(t // tm,),
        in_specs=[pl.BlockSpec((tm, d), lambda i: (i, 0)),
                  pl.BlockSpec((tm, half), lambda i: (i, 0)),
                  pl.BlockSpec((half, tm), lambda i: (0, i)),
                  pl.BlockSpec((2 * half, d), lambda i: (0, 0))],
        out_specs=pl.BlockSpec((tm, d), lambda i: (i, 0)),
        out_shape=jax.ShapeDtypeStruct((t, d), F32),
        compiler_params=_params(("parallel",)),
    )(x, oa, ob_t, w)


def out_proj_bwd(dx, w):
    t, d = dx.shape
    half = w.shape[0] // 2
    tm = 512

    def body(dx_ref, w_ref, da_ref, db_ref, dbt_ref):
        dxb = dx_ref[...].astype(BF16)
        da_ref[...] = _dot_nt(dxb, w_ref[0:half, :]).astype(BF16)
        db_ref[...] = _dot_nt(dxb, w_ref[half:2 * half, :]).astype(BF16)
        dbt_ref[...] = _dot_nt(w_ref[half:2 * half, :], dxb).astype(BF16)

    return pl.pallas_call(
        body, name="out_proj_bwd", grid=(t // tm,),
        in_specs=[pl.BlockSpec((tm, d), lambda i: (i, 0)),
                  pl.BlockSpec((2 * half, d), lambda i: (0, 0))],
        out_specs=[pl.BlockSpec((tm, half), lambda i: (i, 0)),
                   pl.BlockSpec((tm, half), lambda i: (i, 0)),
                   pl.BlockSpec((half, tm), lambda i: (0, i))],
        out_shape=[jax.ShapeDtypeStruct((t, half), BF16), jax.ShapeDtypeStruct((t, half), BF16),
                   jax.ShapeDtypeStruct((half, t), BF16)],
        compiler_params=_params(("parallel",)),
    )(dx, w)


def _lane(shape):
    return lax.broadcasted_iota(jnp.int32, shape, 1)


def _seg_sum(z, mask):
    return jnp.sum(jnp.where(mask, z, 0.0), axis=-1, keepdims=True)


def _pair_norm(x):
    lo = _lane(x.shape) < HD
    x2 = x * x
    r = jnp.where(lo, lax.rsqrt(_seg_sum(x2, lo) * (1.0 / HD) + EPS),
                  lax.rsqrt(_seg_sum(x2, ~lo) * (1.0 / HD) + EPS))
    return lo, r, x * r


def _pair_norm_bwd(lo, r, xn, dyg):
    z = dyg * xn
    mean = jnp.where(lo, _seg_sum(z, lo), _seg_sum(z, ~lo)) * (1.0 / HD)
    return r * (dyg - xn * mean)


A_TM = 256


def prep1_fwd(proj, gq, gk, gcq, gckv):
    t = proj.shape[0]
    tm = A_TM

    def body(p_ref, gq_ref, gk_ref, gcq_ref, gckv_ref, qa_ref, ka_ref, va_ref, cq_ref, ckv_ref):
        for p in range(4):
            sl = slice(LANES * p, LANES * (p + 1))
            _, _, xn = _pair_norm(p_ref[:, sl])
            qa_ref[:, sl] = (xn * gq_ref[:, sl] * 0.125).astype(BF16)
            _, _, xn = _pair_norm(p_ref[:, 512 + LANES * p:512 + LANES * (p + 1)])
            ka_ref[:, sl] = (xn * gk_ref[:, sl]).astype(BF16)
        va_ref[...] = p_ref[:, 1024:1536].astype(BF16)
        _, xn = _rms(p_ref[:, 1536:1792])
        cq_ref[...] = (xn * gcq_ref[...]).astype(BF16)
        _, xn = _rms(p_ref[:, 1792:1920])
        ckv_ref[...] = (xn * gckv_ref[...]).astype(BF16)

    row = lambda w: pl.BlockSpec((tm, w), lambda i: (i, 0))
    vec = lambda w: pl.BlockSpec((1, w), lambda i: (0, 0))
    return pl.pallas_call(
        body, name="prep1_fwd", grid=(t // tm,),
        in_specs=[row(PROJ_W), vec(512), vec(512), vec(256), vec(128)],
        out_specs=[row(512), row(512), row(512), row(256), row(128)],
        out_shape=[jax.ShapeDtypeStruct((t, w), BF16) for w in (512, 512, 512, 256, 128)],
        compiler_params=_params(("parallel",)),
    )(proj, gq, gk, gcq, gckv)


def prep1_bwd(proj, dqa, dkp, dvp, dcq, dckv, dkr, gq, gk, gcq, gckv):
    t = proj.shape[0]
    tm = A_TM
    nb = t // tm

    def body(p_ref, dqa_ref, dk0_ref, dk1_ref, dk2_ref, dv0_ref, dv1_ref, dv2_ref, dcq_ref, dckv_ref, dkr_ref,
             gq_ref, gk_ref, gcq_ref, gckv_ref, dp_ref, dgq_ref, dgk_ref, dgcq_ref, dgckv_ref):
        i = pl.program_id(0)

        @pl.when(i == 0)
        def _():
            dgq_ref[...] = jnp.zeros_like(dgq_ref)
            dgk_ref[...] = jnp.zeros_like(dgk_ref)
            dgcq_ref[...] = jnp.zeros_like(dgcq_ref)
            dgckv_ref[...] = jnp.zeros_like(dgckv_ref)

        has1 = (i + 1 < nb).astype(F32)
        has2 = (i + 2 < nb).astype(F32)
        for p in range(4):
            sl = slice(LANES * p, LANES * (p + 1))
            lo, r, xn = _pair_norm(p_ref[:, sl])
            dy = dqa_ref[:, sl] * 0.125
            dp_ref[:, sl] = _pair_norm_bwd(lo, r, xn, dy * gq_ref[:, sl]).astype(BF16)
            dgq_ref[:, sl] += jnp.sum(dy * xn, axis=0, keepdims=True)
            ks = slice(512 + LANES * p, 512 + LANES * (p + 1))
            lo, r, xn = _pair_norm(p_ref[:, ks])
            dy = dk0_ref[0, :, sl] + has1 * dk1_ref[0, :, sl] + has2 * dk2_ref[0, :, sl]
            dp_ref[:, ks] = _pair_norm_bwd(lo, r, xn, dy * gk_ref[:, sl]).astype(BF16)
            dgk_ref[:, sl] += jnp.sum(dy * xn, axis=0, keepdims=True)
        dp_ref[:, 1024:1536] = (dv0_ref[0] + has1 * dv1_ref[0] + has2 * dv2_ref[0]).astype(BF16)
        for (a, b, d_ref, g_ref, dg_ref) in ((1536, 1792, dcq_ref, gcq_ref, dgcq_ref),
                                             (1792, 1920, dckv_ref, gckv_ref, dgckv_ref)):
            r, xn = _rms(p_ref[:, a:b])
            dy = d_ref[...]
            dyg = dy * g_ref[...]
            dp_ref[:, a:b] = (r * (dyg - xn * jnp.mean(dyg * xn, axis=-1, keepdims=True))).astype(BF16)
            dg_ref[...] += jnp.sum(dy * xn, axis=0, keepdims=True)
        dp_ref[:, 1920:2048] = dkr_ref[...].astype(BF16)

    row = lambda w: pl.BlockSpec((tm, w), lambda i: (i, 0))
    vec = lambda w: pl.BlockSpec((1, w), lambda i: (0, 0))
    part = lambda s: pl.BlockSpec((1, tm, 512), lambda i: (s, jnp.minimum(i + s, nb - 1), 0))
    return pl.pallas_call(
        body, name="prep1_bwd", grid=(nb,),
        in_specs=[row(PROJ_W), row(512), part(0), part(1), part(2), part(0), part(1), part(2),
                  row(256), row(128), row(128), vec(512), vec(512), vec(256), vec(128)],
        out_specs=[row(PROJ_W), vec(512), vec(512), vec(256), vec(128)],
        out_shape=[jax.ShapeDtypeStruct((t, PROJ_W), BF16)] + [jax.ShapeDtypeStruct((1, w), F32) for w in (512, 512, 256, 128)],
        compiler_params=_params(("arbitrary",)),
    )(proj, dqa, dkp, dkp, dkp, dvp, dvp, dvp, dcq, dckv, dkr, gq, gk, gcq, gckv)


def _roll(x, shift):
    return pltpu.roll(x, shift % LANES, 1)


def _rope(y, c, s1, s2):
    return y * c + _roll(y, -16) * s1 + _roll(y, 16) * s2


def _rope_bwd(d, c, s1, s2):
    return d * c + _roll(d * s1, 16) + _roll(d * s2, -16)


def _q_head_stats(x):
    lane = _lane(x.shape)
    mn = lane < HD
    mr = (lane >= HD) & (lane < HD + ROPE)
    x2 = x * x
    r = jnp.where(mn, lax.rsqrt(_seg_sum(x2, mn) * (1.0 / HD) + EPS),
                  lax.rsqrt(_seg_sum(x2, mr) * (1.0 / ROPE) + EPS))
    return mn, mr, r, x * r


def _kr_stats(x):
    r = lax.rsqrt(jnp.sum(x * x, axis=-1, keepdims=True) * (1.0 / ROPE) + EPS)
    return r, x * r


def prep2_fwd(qlat, kv, proj, gq, gk, gkr, tabs):
    t = qlat.shape[0]
    tm = A_TM

    def body(q_ref, kv_ref, kr_ref, gq_ref, gk_ref, gkr_ref, tab_ref, qf_ref, kf_ref, vp_ref):
        _, xn = _kr_stats(kr_ref[...])
        kpe = _roll(_rope(xn * gkr_ref[...], tab_ref[3], tab_ref[4], tab_ref[5]), 64)
        for h in range(HEADS):
            sl = slice(LANES * h, LANES * (h + 1))
            _, _, _, xn = _q_head_stats(q_ref[:, sl])
            qf_ref[:, sl] = _rope(xn * gq_ref[...], tab_ref[0], tab_ref[1], tab_ref[2]).astype(BF16)
            x = kv_ref[:, sl]
            lo = _lane(x.shape) < HD
            xk = jnp.where(lo, x, 0.0)
            rk = lax.rsqrt(jnp.sum(xk * xk, axis=-1, keepdims=True) * (1.0 / HD) + EPS)
            kf_ref[:, sl] = (xk * rk * gk_ref[...] + kpe).astype(BF16)
            if h % 2 == 0:
                v_even = _roll(x, 64)
            else:
                vp_ref[:, LANES * (h // 2):LANES * (h // 2 + 1)] = jnp.where(lo, v_even, x).astype(BF16)

    row = lambda w: pl.BlockSpec((tm, w), lambda i: (i, 0))
    vec = lambda w: pl.BlockSpec((1, w), lambda i: (0, 0))
    return pl.pallas_call(
        body, name="prep2_fwd", grid=(t // tm,),
        in_specs=[row(1024), row(1024), pl.BlockSpec((tm, LANES), lambda i: (i, 15)), vec(128), vec(128), vec(128),
                  pl.BlockSpec((6, tm, LANES), lambda i: (0, i, 0))],
        out_specs=[row(1024), row(1024), row(512)],
        out_shape=[jax.ShapeDtypeStruct((t, 1024), BF16), jax.ShapeDtypeStruct((t, 1024), BF16),
                   jax.ShapeDtypeStruct((t, 512), BF16)],
        compiler_params=_params(("parallel",)),
    )(qlat, kv, proj, gq, gk, gkr, tabs)


def prep2_bwd(qlat, kv, proj, dqf, dkf, dvp, gq, gk, gkr, tabs):
    t = qlat.shape[0]
    tm = A_TM

    def body(q_ref, kv_ref, kr_ref, dqf_ref, dkf_ref, dvp_ref, gq_ref, gk_ref, gkr_ref, tab_ref,
             dq_ref, dkv_ref, dkr_ref, dgq_ref, dgk_ref, dgkr_ref):
        i = pl.program_id(0)

        @pl.when(i == 0)
        def _():
            dgq_ref[...] = jnp.zeros_like(dgq_ref)
            dgk_ref[...] = jnp.zeros_like(dgk_ref)
            dgkr_ref[...] = jnp.zeros_like(dgkr_ref)

        dgq = jnp.zeros((1, LANES), F32)
        dgk = jnp.zeros((1, LANES), F32)
        dkpe = jnp.zeros((tm, LANES), F32)
        for h in range(HEADS):
            sl = slice(LANES * h, LANES * (h + 1))
            mn, mr, r, xn = _q_head_stats(q_ref[:, sl])
            dy = _rope_bwd(dqf_ref[sl, :].T, tab_ref[0], tab_ref[1], tab_ref[2])
            dyg = dy * gq_ref[...]
            z = dyg * xn
            mean = jnp.where(mn, _seg_sum(z, mn) * (1.0 / HD), _seg_sum(z, mr) * (1.0 / ROPE))
            dq_ref[:, sl] = (r * (dyg - xn * mean)).astype(BF16)
            dgq = dgq + jnp.sum(dy * xn, axis=0, keepdims=True)

            x = kv_ref[:, sl]
            dk = dkf_ref[:, sl]
            xk = jnp.where(mn, x, 0.0)
            rk = lax.rsqrt(jnp.sum(xk * xk, axis=-1, keepdims=True) * (1.0 / HD) + EPS)
            xkn = xk * rk
            dyk = jnp.where(mn, dk, 0.0)
            dykg = dyk * gk_ref[...]
            dxk = rk * (dykg - xkn * (jnp.sum(dykg * xkn, axis=-1, keepdims=True) * (1.0 / HD)))
            dgk = dgk + jnp.sum(dyk * xkn, axis=0, keepdims=True)
            dkpe = dkpe + jnp.where(mr, dk, 0.0)
            dvpair = dvp_ref[:, LANES * (h // 2):LANES * (h // 2 + 1)]
            dv = _roll(dvpair, 64) if h % 2 == 0 else dvpair
            dkv_ref[:, sl] = jnp.where(mn, dxk, dv).astype(BF16)

        r, xn = _kr_stats(kr_ref[...])
        dy = _rope_bwd(_roll(dkpe, 64), tab_ref[3], tab_ref[4], tab_ref[5])
        dyg = dy * gkr_ref[...]
        dkr_ref[...] = r * (dyg - xn * (jnp.sum(dyg * xn, axis=-1, keepdims=True) * (1.0 / ROPE)))
        dgq_ref[...] += dgq
        dgk_ref[...] += dgk
        dgkr_ref[...] += jnp.sum(dy * xn, axis=0, keepdims=True)

    row = lambda w: pl.BlockSpec((tm, w), lambda i: (i, 0))
    vec = lambda w: pl.BlockSpec((1, w), lambda i: (0, 0))
    return pl.pallas_call(
        body, name="prep2_bwd", grid=(t // tm,),
        in_specs=[row(1024), row(1024), pl.BlockSpec((tm, LANES), lambda i: (i, 15)),
                  pl.BlockSpec((1024, tm), lambda i: (0, i)), row(1024), row(512),
                  vec(128), vec(128), vec(128), pl.BlockSpec((6, tm, LANES), lambda i: (0, i, 0))],
        out_specs=[row(1024), row(1024), row(128), vec(128), vec(128), vec(128)],
        out_shape=[jax.ShapeDtypeStruct((t, 1024), BF16), jax.ShapeDtypeStruct((t, 1024), BF16),
                   jax.ShapeDtypeStruct((t, LANES), F32)] + [jax.ShapeDtypeStruct((1, LANES), F32)] * 3,
        compiler_params=_params(("arbitrary",)),
    )(qlat, kv, proj, dqf, dkf, dvp, gq, gk, gkr, tabs)


A_TQ = 256
A_WIN = 3 * A_TQ


def _a_specs(t):
    nb = t // A_TQ
    blk = lambda s: pl.BlockSpec((A_TQ, 512), lambda i: (jnp.maximum(i - s, 0), 0))
    return nb, blk


def _a_probs(qb, q_ref, kc, b_ref, head, sl, lo):
    hm = lo if head % 2 == 0 else ~lo
    qm = jnp.where(hm, q_ref[:, sl], jnp.zeros((), BF16))
    s = _dot_nt(qm, kc) + b_ref[head]
    col = lax.broadcasted_iota(jnp.int32, s.shape, 1)
    s = jnp.where(col >= 2 * A_TQ - A_TQ * qb, s, NEG)
    e = jnp.exp(s - jnp.max(s, axis=-1, keepdims=True))
    pr = e * (1.0 / jnp.sum(e, axis=-1, keepdims=True))
    return hm, qm, pr


def attn_a_fwd(qa, ka, va, bias):
    t = qa.shape[0]
    nb, blk = _a_specs(t)

    def body(q_ref, k2_ref, k1_ref, k0_ref, v2_ref, v1_ref, v0_ref, b_ref, o_ref):
        qb = pl.program_id(0)
        lo = _lane((A_TQ, LANES)) < HD
        for p in range(4):
            sl = slice(LANES * p, LANES * (p + 1))
            kc = jnp.concatenate([k2_ref[:, sl], k1_ref[:, sl], k0_ref[:, sl]], axis=0)
            vc = jnp.concatenate([v2_ref[:, sl], v1_ref[:, sl], v0_ref[:, sl]], axis=0)
            outs = []
            for h2 in range(2):
                _, _, pr = _a_probs(qb, q_ref, kc, b_ref, 2 * p + h2, sl, lo)
                outs.append(_dot(pr.astype(BF16), vc))
            o_ref[:, sl] = jnp.where(lo, outs[0], outs[1]).astype(BF16)

    return pl.pallas_call(
        body, name="attn_a_fwd", grid=(nb,),
        in_specs=[blk(0), blk(2), blk(1), blk(0), blk(2), blk(1), blk(0),
                  pl.BlockSpec((HEADS, A_TQ, A_WIN), lambda i: (0, 0, 0))],
        out_specs=pl.BlockSpec((A_TQ, 512), lambda i: (i, 0)),
        out_shape=jax.ShapeDtypeStruct((t, 512), BF16),
        compiler_params=_params(("parallel",)),
    )(qa, ka, ka, ka, va, va, va, bias)


def attn_a_bwd(qa, ka, va, bias, do):
    t = qa.shape[0]
    nb, blk = _a_specs(t)

    def body(q_ref, k2_ref, k1_ref, k0_ref, v2_ref, v1_ref, v0_ref, b_ref, do_ref, dq_ref, dk_ref, dv_ref, db_ref):
        qb = pl.program_id(0)

        @pl.when(qb == 0)
        def _():
            db_ref[...] = jnp.zeros_like(db_ref)

        lo = _lane((A_TQ, LANES)) < HD
        for p in range(4):
            sl = slice(LANES * p, LANES * (p + 1))
            kc = jnp.concatenate([k2_ref[:, sl], k1_ref[:, sl], k0_ref[:, sl]], axis=0)
            vc = jnp.concatenate([v2_ref[:, sl], v1_ref[:, sl], v0_ref[:, sl]], axis=0)
            dqs = []
            dkc = jnp.zeros((A_WIN, LANES), F32)
            dvc = jnp.zeros((A_WIN, LANES), F32)
            for h2 in range(2):
                head = 2 * p + h2
                hm, qm, pr = _a_probs(qb, q_ref, kc, b_ref, head, sl, lo)
                dom = jnp.where(hm, do_ref[:, sl], jnp.zeros((), BF16))
                dp = _dot_nt(dom, vc)
                ds = pr * (dp - jnp.sum(pr * dp, axis=-1, keepdims=True))
                db_ref[head] += ds
                dsb = ds.astype(BF16)
                dqs.append(_dot(dsb, kc))
                dkc = dkc + _dot_tn(dsb, qm)
                dvc = dvc + _dot_tn(pr.astype(BF16), dom)
            dq_ref[:, sl] = jnp.where(lo, dqs[0], dqs[1])
            for s in range(3):
                rows = slice(A_TQ * (2 - s), A_TQ * (3 - s))
                dk_ref[s, :, sl] = dkc[rows]
                dv_ref[s, :, sl] = dvc[rows]

    share = pl.BlockSpec((3, A_TQ, 512), lambda i: (0, i, 0))
    return pl.pallas_call(
        body, name="attn_a_bwd", grid=(nb,),
        in_specs=[blk(0), blk(2), blk(1), blk(0), blk(2), blk(1), blk(0),
                  pl.BlockSpec((HEADS, A_TQ, A_WIN), lambda i: (0, 0, 0)), blk(0)],
        out_specs=[pl.BlockSpec((A_TQ, 512), lambda i: (i, 0)), share, share,
                   pl.BlockSpec((HEADS, A_TQ, A_WIN), lambda i: (0, 0, 0))],
        out_shape=[jax.ShapeDtypeStruct((t, 512), F32), jax.ShapeDtypeStruct((3, t, 512), F32),
                   jax.ShapeDtypeStruct((3, t, 512), F32), jax.ShapeDtypeStruct((HEADS, A_TQ, A_WIN), F32)],
        compiler_params=_params(("arbitrary",)),
    )(qa, ka, ka, ka, va, va, va, bias, do)


B_T = 512


B_SCALE2 = B_SCALE * 1.4426950408889634


def _tri_tables(n, by_query):
    pairs = [(i, j) for i in range(n) for j in range(i + 1)] if by_query else [(i, j) for j in range(n) for i in range(j, n)]
    return (np.asarray([p[0] for p in pairs], np.int32), np.asarray([p[1] for p in pairs], np.int32))


def _b_mask_t(s):
    kc = lax.broadcasted_iota(jnp.int32, s.shape, 0) // CHUNK
    qc = lax.broadcasted_iota(jnp.int32, s.shape, 1) // CHUNK
    return jnp.where(kc <= qc, s, NEG)


def attn_b_fwd(qf, kf, vp):
    t = qf.shape[0]
    n = t // B_T
    qtab, ktab = _tri_tables(n, by_query=True)

    def body(qt_ref, kt_ref, q_ref, k_ref, v_ref, o_ref, lse_ref, m_s, l_s, acc_s):
        qb, kb = qt_ref[pl.program_id(1)], kt_ref[pl.program_id(1)]

        @pl.when(kb == 0)
        def _():
            m_s[...] = jnp.full_like(m_s, NEG)
            l_s[...] = jnp.zeros_like(l_s)
            acc_s[...] = jnp.zeros_like(acc_s)

        def step(masked):
            v = v_ref[...]
            for h2 in range(2):
                sl = slice(LANES * h2, LANES * (h2 + 1))
                s = _dot_nt(k_ref[:, sl], q_ref[:, sl]) * B_SCALE2
                if masked:
                    s = _b_mask_t(s)
                m_prev = m_s[h2]
                m_new = jnp.maximum(m_prev, jnp.max(s, axis=0, keepdims=True))
                alpha = jnp.exp2(m_prev - m_new)
                pr = jnp.exp2(s - m_new)
                l_s[h2] = alpha * l_s[h2] + jnp.sum(pr, axis=0, keepdims=True)
                acc_s[h2] = alpha * acc_s[h2] + _dot_tn(v, pr.astype(BF16))
                m_s[h2] = m_new

        @pl.when(kb < qb)
        def _():
            step(False)

        @pl.when(kb == qb)
        def _():
            step(True)
            for h2 in range(2):
                l = l_s[h2]
                rows = slice(HD * h2, HD * (h2 + 1))
                o_ref[rows, :] = (acc_s[h2, rows, :] * (1.0 / l)).astype(BF16)
                lse_ref[0, h2:h2 + 1, :] = m_s[h2] + jnp.log2(l)

    grid_spec = pltpu.PrefetchScalarGridSpec(
        num_scalar_prefetch=2, grid=(4, len(qtab)),
        in_specs=[pl.BlockSpec((B_T, 256), lambda p, s, qt, kt: (qt[s], p)),
                  pl.BlockSpec((B_T, 256), lambda p, s, qt, kt: (kt[s], p)),
                  pl.BlockSpec((B_T, LANES), lambda p, s, qt, kt: (kt[s], p))],
        out_specs=[pl.BlockSpec((LANES, B_T), lambda p, s, qt, kt: (p, qt[s])),
                   pl.BlockSpec((1, 2, B_T), lambda p, s, qt, kt: (p, 0, qt[s]))],
        scratch_shapes=[pltpu.VMEM((2, 1, B_T), F32), pltpu.VMEM((2, 1, B_T), F32), pltpu.VMEM((2, LANES, B_T), F32)])
    return pl.pallas_call(
        body, name="attn_b_fwd", grid_spec=grid_spec,
        out_shape=[jax.ShapeDtypeStruct((512, t), BF16), jax.ShapeDtypeStruct((4, 2, t), F32)],
        compiler_params=_params(("parallel", "arbitrary")),
    )(jnp.asarray(qtab), jnp.asarray(ktab), qf, kf, vp)


def attn_b_bwd(qf, kf, vp, do, do_t, o_t, lse, scatter=()):
    t = qf.shape[0]
    n = t // B_T
    qtab, ktab = _tri_tables(n, by_query=False)
    plan = ScatterPlan(scatter)
    m = plan.n
    last = len(qtab) - 1

    def body(*refs):
        qt_ref, kt_ref, q_ref, k_ref, v_ref, do_ref, dot_ref, ot_ref, lse_ref = refs[:9]
        ins, (dq_ref, dk_ref, dv_ref), outs = refs[9:9 + m], refs[9 + m:12 + m], refs[12 + m:12 + 2 * m]
        sems = refs[12 + 2 * m:]
        qb, kb = qt_ref[pl.program_id(1)], kt_ref[pl.program_id(1)]
        if m:
            pl.when((pl.program_id(0) == 0) & (pl.program_id(1) == 0))(lambda: plan.start(ins, outs, sems))

        @pl.when(pl.program_id(1) == 0)
        def _():
            dq_ref[...] = jnp.zeros_like(dq_ref)

        @pl.when(qb == kb)
        def _():
            dk_ref[...] = jnp.zeros_like(dk_ref)
            dv_ref[...] = jnp.zeros_like(dv_ref)

        def step(masked):
            cols = pl.ds(pl.multiple_of(qb * B_T, B_T), B_T)
            v = v_ref[...]
            dov = do_ref[...]
            prod = dot_ref[...].astype(F32) * ot_ref[...].astype(F32)
            lo = _lane((B_T, LANES)) < HD
            for h2 in range(2):
                sl = slice(LANES * h2, LANES * (h2 + 1))
                hm = lo if h2 == 0 else ~lo
                q = q_ref[:, sl]
                k = k_ref[:, sl]
                dom = jnp.where(hm, dov, jnp.zeros((), BF16))
                delta = jnp.sum(prod[HD * h2:HD * (h2 + 1), :], axis=0, keepdims=True)
                s = _dot_nt(k, q) * B_SCALE2
                if masked:
                    s = _b_mask_t(s)
                pr = jnp.exp2(s - lse_ref[0, h2:h2 + 1, :])
                dp = _dot_nt(v, dom)
                ds = (pr * (dp - delta) * B_SCALE).astype(BF16)
                dk_ref[:, sl] += _dot(ds, q)
                dv_ref[...] += _dot(pr.astype(BF16), dom)
                dq_ref[sl, cols] += _dot_tn(k, ds)

        @pl.when(qb > kb)
        def _():
            step(False)

        @pl.when(qb == kb)
        def _():
            step(True)

        if m:
            pl.when((pl.program_id(0) == 3) & (pl.program_id(1) == last))(lambda: plan.finish(ins, outs, sems))

    qrow = lambda w: pl.BlockSpec((B_T, w), lambda p, s, qt, kt: (qt[s], p))
    qcol = pl.BlockSpec((LANES, B_T), lambda p, s, qt, kt: (p, qt[s]))
    krow = lambda w: pl.BlockSpec((B_T, w), lambda p, s, qt, kt: (kt[s], p))
    grid_spec = pltpu.PrefetchScalarGridSpec(
        num_scalar_prefetch=2, grid=(4, len(qtab)),
        in_specs=[qrow(256), krow(256), krow(LANES), qrow(LANES), qcol, qcol,
                  pl.BlockSpec((1, 2, B_T), lambda p, s, qt, kt: (p, 0, qt[s]))] + [ANY] * m,
        out_specs=[pl.BlockSpec((256, t), lambda p, s, qt, kt: (p, 0)), krow(256), krow(LANES)] + [ANY] * m,
        scratch_shapes=plan.scratch if m else [])
    return pl.pallas_call(
        body, name="attn_b_bwd", grid_spec=grid_spec,
        out_shape=[jax.ShapeDtypeStruct((1024, t), F32), jax.ShapeDtypeStruct((t, 1024), F32),
                   jax.ShapeDtypeStruct((t, 512), F32)] + plan.out_shape,
        compiler_params=_params(("arbitrary", "arbitrary")),
    )(jnp.asarray(qtab), jnp.asarray(ktab), qf, kf, vp, do, do_t, o_t, lse, *scatter)


_U_LEN = A_TQ + A_WIN - 1


def _band_mask():
    a = np.arange(A_TQ)[:, None] // CHUNK
    b = np.arange(A_WIN)[None, :] // CHUNK
    return (b >= a) & (b <= a + A_LEFT)


def bias_block(table):
    h = table.shape[0]
    n_lo = A_WIN - 1 - 2 * A_TQ - A_MAX_REL
    ext = jnp.concatenate([jnp.repeat(table[:, :1], n_lo, axis=1), table,
                           jnp.repeat(table[:, -1:], _U_LEN - n_lo - table.shape[1], axis=1)], axis=1)
    row = jnp.pad(ext[:, ::-1], ((0, 0), (0, 1)))
    flat = jnp.tile(row, (1, A_TQ))[:, :A_TQ * _U_LEN]
    skew = flat.reshape(h, A_TQ, _U_LEN)
    toep = skew[:, :, A_TQ - 1:A_TQ - 1 + A_WIN]
    return jnp.where(jnp.asarray(_band_mask())[None], toep, NEG)


def bias_block_grad(db):
    h = db.shape[0]
    n_lo = A_WIN - 1 - 2 * A_TQ - A_MAX_REL
    skew = jnp.pad(db, ((0, 0), (0, 0), (A_TQ - 1, 0)))
    flat = jnp.pad(skew.reshape(h, A_TQ * _U_LEN), ((0, 0), (0, A_TQ)))
    ext = jnp.sum(flat.reshape(h, A_TQ, _U_LEN + 1), axis=1)[:, :_U_LEN][:, ::-1]
    n_tab = 2 * A_MAX_REL + 1
    first = jnp.sum(ext[:, :n_lo + 1], axis=1, keepdims=True)
    last = jnp.sum(ext[:, n_lo + n_tab - 1:], axis=1, keepdims=True)
    return jnp.concatenate([first, ext[:, n_lo + 1:n_lo + n_tab - 1], last], axis=1)


def rope_tabs(t):
    inv = 1.0 / (10000.0 ** (jnp.arange(0, ROPE, 2, dtype=F32) / ROPE))
    ang = jnp.arange(t, dtype=F32)[:, None] * inv[None, :]
    cos, sin = jnp.cos(ang), jnp.sin(ang)
    z = lambda w: jnp.zeros((t, w), F32)
    ck = jnp.concatenate([cos, cos, z(96)], axis=1)
    s1k = jnp.concatenate([-sin, z(112)], axis=1)
    s2k = jnp.concatenate([z(16), sin, z(96)], axis=1)
    cq = jnp.concatenate([jnp.ones((t, HD), F32), cos, cos, z(32)], axis=1)
    s1q = jnp.concatenate([z(HD), -sin, z(48)], axis=1)
    s2q = jnp.concatenate([z(HD + 16), sin, z(32)], axis=1)
    return jnp.stack([cq, s1q, s2q, ck, s1k, s2k])


def _pad_lanes(v, width):
    return jnp.pad(v, ((0, 0), (0, width - v.shape[1])))


LATE = ("w_in", "b_w_uq", "b_w_ukv", "w_out", "ffn2_w_gate", "ffn2_w_up", "ffn2_w_down")
FFN2 = ("ffn2_w_gate", "ffn2_w_up", "ffn2_w_down")


def kernel_layout(gathered):
    w = {n: v for n, v in gathered.items() if n.startswith("ffn")}
    if "w_in" in gathered:
        w["w_in"] = jnp.pad(_shards_to_cols(gathered["w_in"]), ((0, 0), (0, PROJ_W - IN_COLS)))
        uq = _shards_to_cols(gathered["b_w_uq"]).reshape(256, HEADS, HD + ROPE)
        w["b_w_uq"] = jnp.pad(uq, ((0, 0), (0, 0), (0, LANES - HD - ROPE))).reshape(256, HEADS * LANES)
        w["b_w_ukv"] = _shards_to_cols(gathered["b_w_ukv"])
        w["w_out"] = gathered["w_out"].reshape(N_SHARD * gathered["w_out"].shape[1], D_MODEL)
    return w


def local_step(x, target, w, late=None):
    t = x.shape[0]
    gq = jnp.tile(w["a_q_norm"], (1, HEADS))
    gk = jnp.tile(w["a_k_norm"], (1, HEADS))
    gq128 = _pad_lanes(jnp.concatenate([w["b_q_nope_norm"], w["b_q_rope_norm"]], axis=1), LANES)
    gk128 = _pad_lanes(w["b_k_nope_norm"], LANES)
    gkr128 = _pad_lanes(w["b_k_rope_norm"], LANES)
    tabs = rope_tabs(t)
    bias = bias_block(w["a_rel_bias"])

    if late is None:
        x1 = ffn_fwd(x, w["ffn1_norm"], w["ffn1_w_gate"], w["ffn1_w_up"], w["ffn1_w_down"], "ffn_fwd")
    else:
        own, shard = late
        x1, *got = ffn_fwd(x, w["ffn1_norm"], w["ffn1_w_gate"], w["ffn1_w_up"], w["ffn1_w_down"], "ffn_fwd_gather",
                           gather=own)
        w = dict(w, **kernel_layout({n: lax.dynamic_update_index_in_dim(g_, o_, shard, 0)
                                     for n, g_, o_ in zip(LATE, got, own)}))
    h, proj = mix_proj(x1, w["mix_norm"], w["w_in"])
    qa, ka, va, cqn, ckvn = prep1_fwd(proj, gq, gk, w["b_q_lat_norm"], w["b_kv_lat_norm"])
    qlat = matmul(cqn, w["b_w_uq"], "nn", "uq_fwd", 512, 1024, 256)
    kv = matmul(ckvn, w["b_w_ukv"], "nn", "ukv_fwd", 512, 1024, 128)
    qf, kf, vp = prep2_fwd(qlat, kv, proj, gq128, gk128, gkr128, tabs)
    oa = attn_a_fwd(qa, ka, va, bias)
    ob_t, lse = attn_b_fwd(qf, kf, vp)
    x2 = out_proj(x1, oa, ob_t, w["w_out"])
    x3 = ffn_fwd(x2, w["ffn2_norm"], w["ffn2_w_gate"], w["ffn2_w_up"], w["ffn2_w_down"], "ffn_fwd")

    g = {}
    dx3, g["final_norm"], loss = final_loss(x3, w["final_norm"], target)
    g["ffn2_w_gate"], g["ffn2_w_up"], g["ffn2_w_down"], dhp = ffn_bwd(
        x2, dx3, w["ffn2_norm"], w["ffn2_w_gate"], w["ffn2_w_up"], w["ffn2_w_down"], "ffn_bwd")
    dx2, g["ffn2_norm"] = norm_bwd(x2, w["ffn2_norm"], dhp, dx3, "ffn_norm_bwd")
    d_oa, d_ob, d_ob_t = out_proj_bwd(dx2, w["w_out"])
    g["w_out"] = jnp.concatenate([matmul(oa, dx2, "tn", "w_out_a_bwd", 512, 1024, 512, BF16),
                                  matmul(ob_t, dx2, "nn", "w_out_b_bwd", 512, 1024, 512, BF16)], axis=0)
    dqf, dkf, dvp, *landed = attn_b_bwd(qf, kf, vp, d_ob, d_ob_t, ob_t, lse,
                                        scatter=() if late is None else [g[n] for n in FFN2])
    dqa, dkp, dvpa, dbias = attn_a_bwd(qa, ka, va, bias, d_oa)
    dqlat, dkv, dkr, dgq128, dgk128, dgkr128 = prep2_bwd(qlat, kv, proj, dqf, dkf, dvp, gq128, gk128, gkr128, tabs)
    dcq = matmul(dqlat, w["b_w_uq"], "nt", "uq_bwd_x", 512, 256, 1024)
    g["b_w_uq"] = matmul(cqn, dqlat, "tn", "uq_bwd_w", 256, 1024, 512, BF16)
    dckv = matmul(dkv, w["b_w_ukv"], "nt", "ukv_bwd_x", 512, 128, 1024)
    g["b_w_ukv"] = matmul(ckvn, dkv, "tn", "ukv_bwd_w", 128, 1024, 512, BF16)
    dproj, dgq, dgk, g["b_q_lat_norm"], g["b_kv_lat_norm"] = prep1_bwd(
        proj, dqa, dkp, dvpa, dcq, dckv, dkr, gq, gk, w["b_q_lat_norm"], w["b_kv_lat_norm"])
    dh = matmul(dproj, w["w_in"], "nt", "w_in_bwd_x", 512, 1024, PROJ_W)
    g["w_in"] = matmul(h, dproj, "tn", "w_in_bwd_w", 1024, 1024, 512, BF16)
    dx1, g["mix_norm"] = norm_bwd(x1, w["mix_norm"], dh[None], dx2, "mix_norm_bwd")
    g["ffn1_w_gate"], g["ffn1_w_up"], g["ffn1_w_down"], dhp = ffn_bwd(
        x, dx1, w["ffn1_norm"], w["ffn1_w_gate"], w["ffn1_w_up"], w["ffn1_w_down"], "ffn_bwd")
    grad_x, g["ffn1_norm"] = norm_bwd(x, w["ffn1_norm"], dhp, dx1, "ffn_norm_bwd")

    g["a_q_norm"] = jnp.sum(dgq.reshape(HEADS, HD), axis=0, keepdims=True)
    g["a_k_norm"] = jnp.sum(dgk.reshape(HEADS, HD), axis=0, keepdims=True)
    g["a_rel_bias"] = bias_block_grad(dbias)
    g["b_q_nope_norm"] = dgq128[:, :HD]
    g["b_q_rope_norm"] = dgq128[:, HD:HD + ROPE]
    g["b_k_nope_norm"] = dgk128[:, :HD]
    g["b_k_rope_norm"] = dgkr128[:, :ROPE]
    return loss, grad_x, g, landed


ANY = pl.BlockSpec(memory_space=pl.ANY)
N_DEV = 8


def _place():
    return lax.axis_index("x"), lax.axis_index("y"), lax.axis_index("c")


def _flip(v, bit):
    return 1 - v if bit else v


class GatherPlan:
    def __init__(self, ws):
        self.rows = [w.shape[0] for w in ws]
        self.n = len(ws)
        self.out_shape = [jax.ShapeDtypeStruct((N_SHARD,) + w.shape, w.dtype) for w in ws]
        self.scratch = [pltpu.SemaphoreType.DMA((6 * self.n,)), pltpu.SemaphoreType.DMA((6 * self.n,))]

    def _copies(self, ins, outs, sems):
        x, y, c = _place()
        s_me = 2 * x + y
        sibling = (x, y, 1 - c)
        send_sems, recv_sems = sems

        def remote(k, src, dst, to):
            return pltpu.make_async_remote_copy(src_ref=src, dst_ref=dst, send_sem=send_sems.at[k],
                                                recv_sem=recv_sems.at[k], device_id=to, device_id_type=MESH)

        ici, fwd = [], []
        for a in range(self.n):
            hr = self.rows[a] // 2
            mine, theirs = pl.ds(c * hr, hr), pl.ds((1 - c) * hr, hr)
            for j, (cx, cy) in enumerate([(1 - x, y), (x, 1 - y), (1 - x, 1 - y)]):
                got = outs[a].at[2 * cx + cy, mine]
                ici.append((remote(6 * a + j, ins[a].at[mine], outs[a].at[s_me, mine], (cx, cy, c)),
                            remote(6 * a + j, got, got, (cx, cy, c))))
                passed = outs[a].at[2 * cx + cy, theirs]
                fwd.append((remote(6 * a + 3 + j, got, got, sibling), remote(6 * a + 3 + j, passed, passed, sibling)))
        return ici, fwd

    def start(self, ins, outs, sems):
        for send, _ in self._copies(ins, outs, sems)[0]:
            send.start()

    def forward(self, ins, outs, sems):
        ici, fwd = self._copies(ins, outs, sems)
        for (_, arrival), (send, _) in zip(ici, fwd):
            arrival.wait_recv()
            send.start()

    def finish(self, ins, outs, sems):
        ici, fwd = self._copies(ins, outs, sems)
        for _, arrival in fwd:
            arrival.wait_recv()
        for send, _ in ici + fwd:
            send.wait_send()


def allgather_shards(ws):
    plan = GatherPlan(ws)
    n = plan.n

    def body(*refs):
        ins, outs, sems = refs[:n], refs[n:2 * n], refs[2 * n:]
        plan.start(ins, outs, sems)
        plan.forward(ins, outs, sems)
        plan.finish(ins, outs, sems)

    return pl.pallas_call(
        body, name="allgather_shards", in_specs=[ANY] * n, out_specs=[ANY] * n,
        out_shape=plan.out_shape, scratch_shapes=plan.scratch,
    )(*ws)


class ScatterPlan:
    def __init__(self, gs):
        self.rows = [g.shape[1] for g in gs]
        self.n = len(gs)
        self.out_shape = [jax.ShapeDtypeStruct((N_DEV, g.shape[1] // 2, g.shape[2]), g.dtype) for g in gs]
        self.scratch = [pltpu.SemaphoreType.DMA((7 * self.n,)), pltpu.SemaphoreType.DMA((7 * self.n,)),
                        pltpu.SemaphoreType.DMA((self.n,))]

    def _copies(self, ins, outs, sems):
        x, y, c = _place()
        me = 4 * x + 2 * y + c
        send_sems, recv_sems, local_sems = sems
        local, sends, arrivals = [], [], []
        for a in range(self.n):
            hr = self.rows[a] // 2
            local.append(pltpu.make_async_copy(ins[a].at[2 * x + y, pl.ds(c * hr, hr)], outs[a].at[me], local_sems.at[a]))
            for k in range(1, N_DEV):
                px, py, pc = _flip(x, k & 4), _flip(y, k & 2), _flip(c, k & 1)
                sem = dict(send_sem=send_sems.at[7 * a + k - 1], recv_sem=recv_sems.at[7 * a + k - 1],
                           device_id=(px, py, pc), device_id_type=MESH)
                sends.append(pltpu.make_async_remote_copy(
                    src_ref=ins[a].at[2 * px + py, pl.ds(pc * hr, hr)], dst_ref=outs[a].at[me], **sem))
                slot = outs[a].at[4 * px + 2 * py + pc]
                arrivals.append(pltpu.make_async_remote_copy(src_ref=slot, dst_ref=slot, **sem))
        return local, sends, arrivals

    def start(self, ins, outs, sems):
        local, sends, _ = self._copies(ins, outs, sems)
        for cp in local + sends:
            cp.start()

    def finish(self, ins, outs, sems):
        local, sends, arrivals = self._copies(ins, outs, sems)
        for cp in arrivals:
            cp.wait_recv()
        for cp in sends:
            cp.wait_send()
        for cp in local:
            cp.wait()


def scatter_partials(gs):
    plan = ScatterPlan(gs)
    n = plan.n

    def body(*refs):
        ins, outs, sems = refs[:n], refs[n:2 * n], refs[2 * n:]
        plan.start(ins, outs, sems)
        plan.finish(ins, outs, sems)

    return pl.pallas_call(
        body, name="scatter_partials", in_specs=[ANY] * n, out_specs=[ANY] * n,
        out_shape=plan.out_shape, scratch_shapes=plan.scratch,
    )(*gs)


def sum_slots(land, name):
    _, rows, cols = land.shape
    tr = rows if rows <= 128 else rows // 2

    def body(l_ref, o_ref):
        acc = l_ref[0].astype(F32)
        for s in range(1, N_DEV):
            acc = acc + l_ref[s].astype(F32)
        o_ref[...] = acc

    return pl.pallas_call(
        body, name=name, grid=(rows // tr,),
        in_specs=[pl.BlockSpec((N_DEV, tr, cols), lambda i: (0, i, 0))],
        out_specs=pl.BlockSpec((tr, cols), lambda i: (i, 0)),
        out_shape=jax.ShapeDtypeStruct((rows, cols), F32),
        compiler_params=_params(("parallel",)),
    )(land)


def join_halves(hs):
    n = len(hs)

    def body(*refs):
        ins, outs = refs[:n], refs[n:2 * n]
        send_sems, recv_sems = refs[2 * n:]
        x, y, c = _place()
        sends = []
        for a in range(n):
            hr = hs[a].shape[0]
            mine = outs[a].at[pl.ds(c * hr, hr)]
            sends.append(pltpu.make_async_remote_copy(
                src_ref=ins[a], dst_ref=mine, send_sem=send_sems.at[a], recv_sem=recv_sems.at[a],
                device_id=(x, y, 1 - c), device_id_type=MESH))
            sends[-1].start()
        for a in range(n):
            hr = hs[a].shape[0]
            theirs = outs[a].at[pl.ds((1 - c) * hr, hr)]
            pltpu.make_async_remote_copy(
                src_ref=theirs, dst_ref=theirs, send_sem=send_sems.at[a], recv_sem=recv_sems.at[a],
                device_id=(x, y, 1 - c), device_id_type=MESH).wait_recv()
        for cp in sends:
            cp.wait_send()

    return pl.pallas_call(
        body, name="join_halves",
        in_specs=[ANY] * n, out_specs=[ANY] * n,
        out_shape=[jax.ShapeDtypeStruct((2 * h.shape[0], h.shape[1]), h.dtype) for h in hs],
        scratch_shapes=[pltpu.SemaphoreType.DMA((n,)), pltpu.SemaphoreType.DMA((n,))],
    )(*hs)


def allreduce_small(vec):
    def body(v_ref, o_ref, land_ref, send_sems, recv_sems):
        x, y, c = _place()
        me = 4 * x + 2 * y + c
        land_ref[me] = v_ref[...]
        sends = []
        for k in range(1, N_DEV):
            px, py, pc = _flip(x, k & 4), _flip(y, k & 2), _flip(c, k & 1)
            sends.append(pltpu.make_async_remote_copy(
                src_ref=v_ref, dst_ref=land_ref.at[me], send_sem=send_sems.at[k - 1], recv_sem=recv_sems.at[k - 1],
                device_id=(px, py, pc), device_id_type=MESH))
            sends[-1].start()
        for k in range(1, N_DEV):
            px, py, pc = _flip(x, k & 4), _flip(y, k & 2), _flip(c, k & 1)
            slot = land_ref.at[4 * px + 2 * py + pc]
            pltpu.make_async_remote_copy(
                src_ref=slot, dst_ref=slot, send_sem=send_sems.at[k - 1], recv_sem=recv_sems.at[k - 1],
                device_id=(px, py, pc), device_id_type=MESH).wait_recv()
        for cp in sends:
            cp.wait_send()
        acc = land_ref[0]
        for s in range(1, N_DEV):
            acc = acc + land_ref[s]
        o_ref[...] = acc

    vm = pl.BlockSpec(memory_space=pltpu.VMEM)
    return pl.pallas_call(
        body, name="allreduce_small",
        in_specs=[vm], out_specs=vm,
        out_shape=jax.ShapeDtypeStruct(vec.shape, F32),
        scratch_shapes=[pltpu.VMEM((N_DEV,) + vec.shape, F32), pltpu.SemaphoreType.DMA((N_DEV - 1,)),
                        pltpu.SemaphoreType.DMA((N_DEV - 1,))],
    )(vec)


def adamw(w, g, m, v, name):
    rows, cols = w.shape
    tr = rows
    while tr * cols * 4 * 14 > 24 * 1024 * 1024 and tr % 16 == 0:
        tr //= 2
    c1 = 1.0 - ADAM_B1 ** ADAM_STEP
    c2 = 1.0 - ADAM_B2 ** ADAM_STEP

    def body(w_ref, g_ref, m_ref, v_ref, d_ref, nm_ref, nv_ref):
        gv = g_ref[...]
        nm = ADAM_B1 * m_ref[...] + (1.0 - ADAM_B1) * gv
        nv = ADAM_B2 * v_ref[...] + (1.0 - ADAM_B2) * (gv * gv)
        nm_ref[...] = nm
        nv_ref[...] = nv
        d_ref[...] = -ADAM_LR * ((nm / c1) / (jnp.sqrt(nv / c2) + ADAM_EPS) + ADAM_WD * w_ref[...])

    blk = pl.BlockSpec((tr, cols), lambda i: (i, 0))
    return pl.pallas_call(
        body, name=name, grid=(rows // tr,),
        in_specs=[blk] * 4, out_specs=[blk] * 3,
        out_shape=[jax.ShapeDtypeStruct((rows, cols), F32)] * 3,
        compiler_params=_params(("parallel",)),
    )(w, g, m, v)


BIG = ("ffn1_w_gate", "ffn1_w_up", "ffn1_w_down", "w_in", "b_w_uq", "b_w_ukv", "w_out",
       "ffn2_w_gate", "ffn2_w_up", "ffn2_w_down")
SMALL = ("ffn1_norm", "mix_norm", "a_q_norm", "a_k_norm", "a_rel_bias", "b_q_lat_norm", "b_kv_lat_norm",
         "b_q_nope_norm", "b_q_rope_norm", "b_k_nope_norm", "b_k_rope_norm", "ffn2_norm", "final_norm")
WEIGHTS = ("ffn1_norm", "ffn1_w_gate", "ffn1_w_up", "ffn1_w_down", "mix_norm", "w_in", "a_q_norm", "a_k_norm",
           "a_rel_bias", "b_q_lat_norm", "b_w_uq", "b_kv_lat_norm", "b_w_ukv", "b_q_nope_norm", "b_q_rope_norm",
           "b_k_nope_norm", "b_k_rope_norm", "w_out", "ffn2_norm", "ffn2_w_gate", "ffn2_w_up", "ffn2_w_down",
           "final_norm")
PACK_SHAPE = (8, 1024)


def _pack_small(d, last=None):
    flat = [d[n].reshape(-1) for n in SMALL]
    used = sum(f.shape[0] for f in flat)
    total = PACK_SHAPE[0] * PACK_SHAPE[1]
    tail = jnp.zeros((total - used - 1,), F32)
    end = jnp.zeros((1,), F32) if last is None else last.reshape(1)
    return jnp.concatenate(flat + [tail, end]).reshape(PACK_SHAPE)


def _unpack_small(p, like):
    flat = p.reshape(-1)
    out, off = {}, 0
    for n in SMALL:
        size = like[n].size
        out[n] = flat[off:off + size].reshape(like[n].shape)
        off += size
    return out, flat[-1]


def _cols_to_shards(g):
    rows, cols = g.shape
    return g.reshape(rows, N_SHARD, cols // N_SHARD).transpose(1, 0, 2)


def _shards_to_cols(g):
    return g.transpose(1, 0, 2).reshape(g.shape[1], -1)


def kernel(x, ffn1_norm, ffn1_w_gate, ffn1_w_up, ffn1_w_down, mix_norm, w_in, a_q_norm, a_k_norm, a_rel_bias, b_q_lat_norm, b_w_uq, b_kv_lat_norm, b_w_ukv, b_q_nope_norm, b_q_rope_norm, b_k_nope_norm, b_k_rope_norm, w_out, ffn2_norm, ffn2_w_gate, ffn2_w_up, ffn2_w_down, final_norm, loss_target, m_ffn1_norm, m_ffn1_w_gate, m_ffn1_w_up, m_ffn1_w_down, m_mix_norm, m_w_in, m_a_q_norm, m_a_k_norm, m_a_rel_bias, m_b_q_lat_norm, m_b_w_uq, m_b_kv_lat_norm, m_b_w_ukv, m_b_q_nope_norm, m_b_q_rope_norm, m_b_k_nope_norm, m_b_k_rope_norm, m_w_out, m_ffn2_norm, m_ffn2_w_gate, m_ffn2_w_up, m_ffn2_w_down, m_final_norm, v_ffn1_norm, v_ffn1_w_gate, v_ffn1_w_up, v_ffn1_w_down, v_mix_norm, v_w_in, v_a_q_norm, v_a_k_norm, v_a_rel_bias, v_b_q_lat_norm, v_b_w_uq, v_b_kv_lat_norm, v_b_w_ukv, v_b_q_nope_norm, v_b_q_rope_norm, v_b_k_nope_norm, v_b_k_rope_norm, v_w_out, v_ffn2_norm, v_ffn2_w_gate, v_ffn2_w_up, v_ffn2_w_down, v_final_norm):
    args = locals()
    wts = {n: args[n][0] for n in WEIGHTS}
    mom = {n: args["m_" + n][0] for n in WEIGHTS}
    var = {n: args["v_" + n][0] for n in WEIGHTS}

    shard = 2 * lax.axis_index("x") + lax.axis_index("y")
    core = lax.axis_index("c")
    first = [n for n in BIG if n not in LATE]
    own = [wts[n].astype(BF16) for n in first]
    w = {n: wts[n] if n == "a_rel_bias" else wts[n][None] for n in SMALL}
    w.update(kernel_layout({n: lax.dynamic_update_index_in_dim(got, mine, shard, 0)
                            for n, got, mine in zip(first, allgather_shards(own), own)}))

    loss, grad_x, g, landed_ffn2 = local_step(x[0], loss_target[0], w,
                                              late=([wts[n].astype(BF16) for n in LATE], shard))

    part = {n: g[n] for n in first}
    part["w_in"] = _cols_to_shards(g["w_in"][:, :IN_COLS])
    part["b_w_uq"] = _cols_to_shards(g["b_w_uq"].reshape(256, HEADS, LANES)[:, :, :HD + ROPE].reshape(256, -1))
    part["b_w_ukv"] = _cols_to_shards(g["b_w_ukv"])
    part["w_out"] = g["w_out"].reshape(N_SHARD, -1, D_MODEL)
    rest = [n for n in BIG if n not in FFN2]
    landed = dict(zip(rest, scatter_partials([part[n] for n in rest])))
    landed.update(zip(FFN2, landed_ffn2))
    halves = [sum_slots(landed[n], "sum_slots") for n in BIG]
    grads = dict(zip(BIG, (lax.dynamic_update_slice_in_dim(got, mine, core * mine.shape[0], 0)
                           for got, mine in zip(join_halves(halves), halves))))

    small_sum, loss_sum = _unpack_small(allreduce_small(_pack_small(g, loss[0, 0])), wts)
    grads.update(small_sum)

    delta, new_m, new_v = {}, {}, {}
    for n in BIG:
        delta[n], new_m[n], new_v[n] = adamw(wts[n], grads[n], mom[n], var[n], "adamw")
    packed = adamw(_pack_small(wts), _pack_small(grads), _pack_small(mom), _pack_small(var), "adamw_small")
    for dst, p in zip((delta, new_m, new_v), packed):
        dst.update(_unpack_small(p, wts)[0])

    lead = lambda d: [d[n][None] for n in WEIGHTS]
    return (loss_sum, grad_x[None], *lead(grads), *lead(delta), *lead(new_m), *lead(new_v))
```

```python
import functools
import math

import numpy as np
import jax
import jax.numpy as jnp
from jax import lax
from jax.experimental import pallas as pl
from jax.experimental.pallas import tpu as pltpu

F32 = jnp.float32
BF16 = jnp.bfloat16
EPS = 1e-6
NEG = -1e30

D_MODEL = 1024
D_FF = 2816
N_SHARD = 4
FS = D_FF // N_SHARD
CHUNK = 64
A_LEFT = 8
A_MAX_REL = 128
HEADS = 8
HD = 64
ROPE = 32
PROJ_W = 2048
IN_COLS = 1952
B_SCALE = 96 ** -0.5
LANES = 128

ADAM_LR = 0.001
ADAM_B1 = 0.9
ADAM_B2 = 0.999
ADAM_EPS = 1e-08
ADAM_WD = 0.01
ADAM_STEP = 10

VMEM_LIMIT = 56 * 1024 * 1024

MESH = pl.DeviceIdType.MESH


def _dot(a, b):
    return lax.dot_general(a, b, (((1,), (0,)), ((), ())), preferred_element_type=F32)


def _dot_nt(a, b):
    return lax.dot_general(a, b, (((1,), (1,)), ((), ())), preferred_element_type=F32)


def _dot_tn(a, b):
    return lax.dot_general(a, b, (((0,), (0,)), ((), ())), preferred_element_type=F32)


def _params(sem):
    return pltpu.CompilerParams(dimension_semantics=sem, vmem_limit_bytes=VMEM_LIMIT)


def _rms(xv):
    r = lax.rsqrt(jnp.mean(xv * xv, axis=-1, keepdims=True) + EPS)
    return r, xv * r


def ffn_fwd(x, g, wg, wu, wd, name, gather=()):
    t, d = x.shape
    tm = 512
    ni = t // tm
    plan = GatherPlan(gather)
    n = plan.n

    def body(*refs):
        x_ref, g_ref, wg_ref, wu_ref, wd_ref = refs[:5]
        ins, o_ref, outs = refs[5:5 + n], refs[5 + n], refs[6 + n:6 + 2 * n]
        h_ref, acc_ref = refs[6 + 2 * n:8 + 2 * n]
        sems = refs[8 + 2 * n:]
        i, j = pl.program_id(0), pl.program_id(1)
        if n:
            pl.when((i == 0) & (j == 0))(lambda: plan.start(ins, outs, sems))
            pl.when((i == (3 * ni) // 4) & (j == 0))(lambda: plan.forward(ins, outs, sems))

        @pl.when(j == 0)
        def _():
            _, xn = _rms(x_ref[...])
            h_ref[...] = (xn * g_ref[...]).astype(BF16)
            acc_ref[...] = jnp.zeros_like(acc_ref)

        h = h_ref[...]
        gp = _dot(h, wg_ref[0])
        up = _dot(h, wu_ref[0])
        a = (gp * jax.nn.sigmoid(gp) * up).astype(BF16)
        acc_ref[...] += _dot(a, wd_ref[0])

        @pl.when(j == N_SHARD - 1)
        def _():
            o_ref[...] = x_ref[...] + 0.5 * acc_ref[...]

        if n:
            pl.when((i == ni - 1) & (j == N_SHARD - 1))(lambda: plan.finish(ins, outs, sems))

    res = pl.pallas_call(
        body, name=name, grid=(ni, N_SHARD),
        in_specs=[pl.BlockSpec((tm, d), lambda i, j: (i, 0)),
                  pl.BlockSpec((1, d), lambda i, j: (0, 0)),
                  pl.BlockSpec((1, d, FS), lambda i, j: (j, 0, 0)),
                  pl.BlockSpec((1, d, FS), lambda i, j: (j, 0, 0)),
                  pl.BlockSpec((1, FS, d), lambda i, j: (j, 0, 0))] + [ANY] * n,
        out_specs=[pl.BlockSpec((tm, d), lambda i, j: (i, 0))] + [ANY] * n,
        out_shape=[jax.ShapeDtypeStruct((t, d), F32)] + plan.out_shape,
        scratch_shapes=[pltpu.VMEM((tm, d), BF16), pltpu.VMEM((tm, d), F32)] + (plan.scratch if n else []),
        compiler_params=_params(("arbitrary", "arbitrary")),
    )(x, g, wg, wu, wd, *gather)
    return res if n else res[0]


def ffn_bwd(x, dout, g, wg, wu, wd, name, scatter=()):
    t, d = x.shape
    tm = 512
    ni = t // tm
    plan = ScatterPlan(scatter)
    m = plan.n

    def body(*refs):
        x_ref, do_ref, g_ref, wg_ref, wu_ref, wd_ref = refs[:6]
        ins, (dwg_out, dwu_out, dwd_out, dhp_ref), outs = refs[6:6 + m], refs[6 + m:10 + m], refs[10 + m:10 + 2 * m]
        dwg_ref, dwu_ref, dwd_ref = refs[10 + 2 * m:13 + 2 * m]
        sems = refs[13 + 2 * m:]
        i = pl.program_id(1)
        if m:
            pl.when((pl.program_id(0) == 0) & (i == 0))(lambda: plan.start(ins, outs, sems))
        _, xn = _rms(x_ref[...])
        h = (xn * g_ref[...]).astype(BF16)
        dz = (0.5 * do_ref[...]).astype(BF16)
        wgv, wuv, wdv = wg_ref[0], wu_ref[0], wd_ref[0]
        gp = _dot(h, wgv)
        up = _dot(h, wuv)
        s = jax.nn.sigmoid(gp)
        sg = gp * s
        a = (sg * up).astype(BF16)
        da = _dot_nt(dz, wdv)
        dup = (da * sg).astype(BF16)
        dgp = (da * up * (s * (1.0 + gp * (1.0 - s)))).astype(BF16)

        @pl.when(i == 0)
        def _():
            dwg_ref[...] = jnp.zeros_like(dwg_ref)
            dwu_ref[...] = jnp.zeros_like(dwu_ref)
            dwd_ref[...] = jnp.zeros_like(dwd_ref)

        dwd_ref[...] += _dot_tn(a, dz)
        dwg_ref[...] += _dot_tn(h, dgp)
        dwu_ref[...] += _dot_tn(h, dup)
        dhp_ref[0] = _dot_nt(dgp, wgv) + _dot_nt(dup, wuv)

        @pl.when(i == ni - 1)
        def _():
            dwg_out[0] = dwg_ref[...].astype(BF16)
            dwu_out[0] = dwu_ref[...].astype(BF16)
            dwd_out[0] = dwd_ref[...].astype(BF16)

        if m:
            pl.when((pl.program_id(0) == N_SHARD - 1) & (i == ni - 1))(lambda: plan.finish(ins, outs, sems))

    return pl.pallas_call(
        body, name=name, grid=(N_SHARD, ni),
        in_specs=[pl.BlockSpec((tm, d), lambda j, i: (i, 0)),
                  pl.BlockSpec((tm, d), lambda j, i: (i, 0)),
                  pl.BlockSpec((1, d), lambda j, i: (0, 0)),
                  pl.BlockSpec((1, d, FS), lambda j, i: (j, 0, 0)),
                  pl.BlockSpec((1, d, FS), lambda j, i: (j, 0, 0)),
                  pl.BlockSpec((1, FS, d), lambda j, i: (j, 0, 0))] + [ANY] * m,
        out_specs=[pl.BlockSpec((1, d, FS), lambda j, i: (j, 0, 0)),
                   pl.BlockSpec((1, d, FS), lambda j, i: (j, 0, 0)),
                   pl.BlockSpec((1, FS, d), lambda j, i: (j, 0, 0)),
                   pl.BlockSpec((1, tm, d), lambda j, i: (j, i, 0))] + [ANY] * m,
        out_shape=[jax.ShapeDtypeStruct((N_SHARD, d, FS), BF16),
                   jax.ShapeDtypeStruct((N_SHARD, d, FS), BF16),
                   jax.ShapeDtypeStruct((N_SHARD, FS, d), BF16),
                   jax.ShapeDtypeStruct((N_SHARD, t, d), F32)] + plan.out_shape,
        scratch_shapes=[pltpu.VMEM((d, FS), F32), pltpu.VMEM((d, FS), F32), pltpu.VMEM((FS, d), F32)]
        + (plan.scratch if m else []),
        compiler_params=_params(("arbitrary", "arbitrary")),
    )(x, dout, g, wg, wu, wd, *scatter)


def norm_bwd(x, g, dhp, dres, name):
    t, d = x.shape
    p = dhp.shape[0]
    tm = 512

    def body(x_ref, g_ref, dhp_ref, dres_ref, dx_ref, dg_ref):
        i = pl.program_id(0)
        r, xn = _rms(x_ref[...])
        dh = dhp_ref[0]
        for q in range(1, p):
            dh = dh + dhp_ref[q]
        dhg = dh * g_ref[...]
        dx_ref[...] = dres_ref[...] + r * (dhg - xn * jnp.mean(dhg * xn, axis=-1, keepdims=True))

        @pl.when(i == 0)
        def _():
            dg_ref[...] = jnp.zeros_like(dg_ref)

        dg_ref[...] += jnp.sum(dh * xn, axis=0, keepdims=True)

    return pl.pallas_call(
        body, name=name, grid=(t // tm,),
        in_specs=[pl.BlockSpec((tm, d), lambda i: (i, 0)),
                  pl.BlockSpec((1, d), lambda i: (0, 0)),
                  pl.BlockSpec((p, tm, d), lambda i: (0, i, 0)),
                  pl.BlockSpec((tm, d), lambda i: (i, 0))],
        out_specs=[pl.BlockSpec((tm, d), lambda i: (i, 0)),
                   pl.BlockSpec((1, d), lambda i: (0, 0))],
        out_shape=[jax.ShapeDtypeStruct((t, d), F32), jax.ShapeDtypeStruct((1, d), F32)],
        compiler_params=_params(("arbitrary",)),
    )(x, g, dhp, dres)


def final_loss(x, g, target):
    t, d = x.shape
    tm = 512

    def body(x_ref, g_ref, t_ref, dx_ref, dg_ref, loss_ref):
        i = pl.program_id(0)
        r, xn = _rms(x_ref[...])
        gv = g_ref[...]
        e = xn * gv - t_ref[...]
        dy = e * (1.0 / d)
        dhg = dy * gv
        dx_ref[...] = r * (dhg - xn * jnp.mean(dhg * xn, axis=-1, keepdims=True))

        @pl.when(i == 0)
        def _():
            dg_ref[...] = jnp.zeros_like(dg_ref)
            loss_ref[...] = jnp.zeros_like(loss_ref)

        dg_ref[...] += jnp.sum(dy * xn, axis=0, keepdims=True)
        part = jnp.sum(jnp.sum(e * e, axis=-1, keepdims=True), axis=0, keepdims=True) * (0.5 / d)
        loss_ref[...] += jnp.broadcast_to(part, loss_ref.shape)

    return pl.pallas_call(
        body, name="final_loss", grid=(t // tm,),
        in_specs=[pl.BlockSpec((tm, d), lambda i: (i, 0)),
                  pl.BlockSpec((1, d), lambda i: (0, 0)),
                  pl.BlockSpec((tm, d), lambda i: (i, 0))],
        out_specs=[pl.BlockSpec((tm, d), lambda i: (i, 0)),
                   pl.BlockSpec((1, d), lambda i: (0, 0)),
                   pl.BlockSpec((1, LANES), lambda i: (0, 0))],
        out_shape=[jax.ShapeDtypeStruct((t, d), F32), jax.ShapeDtypeStruct((1, d), F32),
                   jax.ShapeDtypeStruct((1, LANES), F32)],
        compiler_params=_params(("arbitrary",)),
    )(x, g, target)


def mix_proj(x, g, w):
    t, d = x.shape
    n = w.shape[1]
    tm = 512

    def body(x_ref, g_ref, w_ref, h_ref, p_ref):
        _, xn = _rms(x_ref[...])
        h = (xn * g_ref[...]).astype(BF16)
        h_ref[...] = h
        p_ref[...] = _dot(h, w_ref[...])

    return pl.pallas_call(
        body, name="mix_proj", grid=(t // tm,),
        in_specs=[pl.BlockSpec((tm, d), lambda i: (i, 0)),
                  pl.BlockSpec((1, d), lambda i: (0, 0)),
                  pl.BlockSpec((d, n), lambda i: (0, 0))],
        out_specs=[pl.BlockSpec((tm, d), lambda i: (i, 0)),
                   pl.BlockSpec((tm, n), lambda i: (i, 0))],
        out_shape=[jax.ShapeDtypeStruct((t, d), BF16), jax.ShapeDtypeStruct((t, n), F32)],
        compiler_params=_params(("parallel",)),
    )(x, g, w)


def matmul(a, b, mode, name, tm, tn, tk, out_dtype=F32):
    if mode == "nn":
        (m, k), n = a.shape, b.shape[1]
        a_spec = pl.BlockSpec((tm, tk), lambda i, j, q: (i, q))
        b_spec = pl.BlockSpec((tk, tn), lambda i, j, q: (q, j))
        dot = _dot
    elif mode == "nt":
        (m, k), n = a.shape, b.shape[0]
        a_spec = pl.BlockSpec((tm, tk), lambda i, j, q: (i, q))
        b_spec = pl.BlockSpec((tn, tk), lambda i, j, q: (j, q))
        dot = _dot_nt
    else:
        (k, m), n = a.shape, b.shape[1]
        a_spec = pl.BlockSpec((tk, tm), lambda i, j, q: (q, i))
        b_spec = pl.BlockSpec((tk, tn), lambda i, j, q: (q, j))
        dot = _dot_tn
    assert m % tm == 0 and n % tn == 0 and k % tk == 0, (m, n, k, tm, tn, tk)
    nk = k // tk

    def body(a_ref, b_ref, o_ref, acc_ref):
        q = pl.program_id(2)

        @pl.when(q == 0)
        def _():
            acc_ref[...] = jnp.zeros_like(acc_ref)

        acc_ref[...] += dot(a_ref[...].astype(BF16), b_ref[...].astype(BF16))

        @pl.when(q == nk - 1)
        def _():
            o_ref[...] = acc_ref[...].astype(out_dtype)

    return pl.pallas_call(
        body, name=name, grid=(m // tm, n // tn, nk),
        in_specs=[a_spec, b_spec],
        out_specs=pl.BlockSpec((tm, tn), lambda i, j, q: (i, j)),
        out_shape=jax.ShapeDtypeStruct((m, n), out_dtype),
        scratch_shapes=[pltpu.VMEM((tm, tn), F32)],
        compiler_params=_params(("parallel", "parallel", "arbitrary")),
    )(a, b)


def out_proj(x, oa, ob_t, w):
    t, d = x.shape
    half = oa.shape[1]
    tm = 512

    def body(x_ref, oa_ref, obt_ref, w_ref, o_ref):
        o_ref[...] = (x_ref[...] + _dot(oa_ref[...], w_ref[0:half, :])
                      + _dot_tn(obt_ref[...], w_ref[half:2 * half, :]))

    return pl.pallas_call(
        body, name="out_proj", grid=(t // tm,),
        in_specs=[pl.BlockSpec((tm, d), lambda i: (i, 0)),
                  pl.BlockSpec((tm, half), lambda i: (i, 0)),
                  pl.BlockSpec((half, tm), lambda i: (0, i)),
                  pl.BlockSpec((2 * half, d), lambda i: (0, 0))],
        out_specs=pl.BlockSpec((tm, d), lambda i: (i, 0)),
        out_shape=jax.ShapeDtypeStruct((t, d), F32),
        compiler_params=_params(("parallel",)),
    )(x, oa, ob_t, w)


def out_proj_bwd(dx, w):
    t, d = dx.shape
    half = w.shape[0] // 2
    tm = 512

    def body(dx_ref, w_ref, da_ref, db_ref, dbt_ref):
        dxb = dx_ref[...].astype(BF16)
        da_ref[...] = _dot_nt(dxb, w_ref[0:half, :]).astype(BF16)
        db_ref[...] = _dot_nt(dxb, w_ref[half:2 * half, :]).astype(BF16)
        dbt_ref[...] = _dot_nt(w_ref[half:2 * half, :], dxb).astype(BF16)

    return pl.pallas_call(
        body, name="out_proj_bwd", grid=(t // tm,),
        in_specs=[pl.BlockSpec((tm, d), lambda i: (i, 0)),
                  pl.BlockSpec((2 * half, d), lambda i: (0, 0))],
        out_specs=[pl.BlockSpec((tm, half), lambda i: (i, 0)),
                   pl.BlockSpec((tm, half), lambda i: (i, 0)),
                   pl.BlockSpec((half, tm), lambda i: (0, i))],
        out_shape=[jax.ShapeDtypeStruct((t, half), BF16), jax.ShapeDtypeStruct((t, half), BF16),
                   jax.ShapeDtypeStruct((half, t), BF16)],
        compiler_params=_params(("parallel",)),
    )(dx, w)


def _lane(shape):
    return lax.broadcasted_iota(jnp.int32, shape, 1)


def _seg_sum(z, mask):
    return jnp.sum(jnp.where(mask, z, 0.0), axis=-1, keepdims=True)


def _pair_norm(x):
    lo = _lane(x.shape) < HD
    x2 = x * x
    r = jnp.where(lo, lax.rsqrt(_seg_sum(x2, lo) * (1.0 / HD) + EPS),
                  lax.rsqrt(_seg_sum(x2, ~lo) * (1.0 / HD) + EPS))
    return lo, r, x * r


def _pair_norm_bwd(lo, r, xn, dyg):
    z = dyg * xn
    mean = jnp.where(lo, _seg_sum(z, lo), _seg_sum(z, ~lo)) * (1.0 / HD)
    return r * (dyg - xn * mean)


A_TM = 256


def prep1_fwd(proj, gq, gk, gcq, gckv):
    t = proj.shape[0]
    tm = A_TM

    def body(p_ref, gq_ref, gk_ref, gcq_ref, gckv_ref, qa_ref, ka_ref, va_ref, cq_ref, ckv_ref):
        for p in range(4):
            sl = slice(LANES * p, LANES * (p + 1))
            _, _, xn = _pair_norm(p_ref[:, sl])
            qa_ref[:, sl] = (xn * gq_ref[:, sl] * 0.125).astype(BF16)
            _, _, xn = _pair_norm(p_ref[:, 512 + LANES * p:512 + LANES * (p + 1)])
            ka_ref[:, sl] = (xn * gk_ref[:, sl]).astype(BF16)
        va_ref[...] = p_ref[:, 1024:1536].astype(BF16)
        _, xn = _rms(p_ref[:, 1536:1792])
        cq_ref[...] = (xn * gcq_ref[...]).astype(BF16)
        _, xn = _rms(p_ref[:, 1792:1920])
        ckv_ref[...] = (xn * gckv_ref[...]).astype(BF16)

    row = lambda w: pl.BlockSpec((tm, w), lambda i: (i, 0))
    vec = lambda w: pl.BlockSpec((1, w), lambda i: (0, 0))
    return pl.pallas_call(
        body, name="prep1_fwd", grid=(t // tm,),
        in_specs=[row(PROJ_W), vec(512), vec(512), vec(256), vec(128)],
        out_specs=[row(512), row(512), row(512), row(256), row(128)],
        out_shape=[jax.ShapeDtypeStruct((t, w), BF16) for w in (512, 512, 512, 256, 128)],
        compiler_params=_params(("parallel",)),
    )(proj, gq, gk, gcq, gckv)


def prep1_bwd(proj, dqa, dkp, dvp, dcq, dckv, dkr, gq, gk, gcq, gckv):
    t = proj.shape[0]
    tm = A_TM
    nb = t // tm

    def body(p_ref, dqa_ref, dk0_ref, dk1_ref, dk2_ref, dv0_ref, dv1_ref, dv2_ref, dcq_ref, dckv_ref, dkr_ref,
             gq_ref, gk_ref, gcq_ref, gckv_ref, dp_ref, dgq_ref, dgk_ref, dgcq_ref, dgckv_ref):
        i = pl.program_id(0)

        @pl.when(i == 0)
        def _():
            dgq_ref[...] = jnp.zeros_like(dgq_ref)
            dgk_ref[...] = jnp.zeros_like(dgk_ref)
            dgcq_ref[...] = jnp.zeros_like(dgcq_ref)
            dgckv_ref[...] = jnp.zeros_like(dgckv_ref)

        has1 = (i + 1 < nb).astype(F32)
        has2 = (i + 2 < nb).astype(F32)
        for p in range(4):
            sl = slice(LANES * p, LANES * (p + 1))
            lo, r, xn = _pair_norm(p_ref[:, sl])
            dy = dqa_ref[:, sl] * 0.125
            dp_ref[:, sl] = _pair_norm_bwd(lo, r, xn, dy * gq_ref[:, sl]).astype(BF16)
            dgq_ref[:, sl] += jnp.sum(dy * xn, axis=0, keepdims=True)
            ks = slice(512 + LANES * p, 512 + LANES * (p + 1))
            lo, r, xn = _pair_norm(p_ref[:, ks])
            dy = dk0_ref[0, :, sl] + has1 * dk1_ref[0, :, sl] + has2 * dk2_ref[0, :, sl]
            dp_ref[:, ks] = _pair_norm_bwd(lo, r, xn, dy * gk_ref[:, sl]).astype(BF16)
            dgk_ref[:, sl] += jnp.sum(dy * xn, axis=0, keepdims=True)
        dp_ref[:, 1024:1536] = (dv0_ref[0] + has1 * dv1_ref[0] + has2 * dv2_ref[0]).astype(BF16)
        for (a, b, d_ref, g_ref, dg_ref) in ((1536, 1792, dcq_ref, gcq_ref, dgcq_ref),
                                             (1792, 1920, dckv_ref, gckv_ref, dgckv_ref)):
            r, xn = _rms(p_ref[:, a:b])
            dy = d_ref[...]
            dyg = dy * g_ref[...]
            dp_ref[:, a:b] = (r * (dyg - xn * jnp.mean(dyg * xn, axis=-1, keepdims=True))).astype(BF16)
            dg_ref[...] += jnp.sum(dy * xn, axis=0, keepdims=True)
        dp_ref[:, 1920:2048] = dkr_ref[...].astype(BF16)

    row = lambda w: pl.BlockSpec((tm, w), lambda i: (i, 0))
    vec = lambda w: pl.BlockSpec((1, w), lambda i: (0, 0))
    part = lambda s: pl.BlockSpec((1, tm, 512), lambda i: (s, jnp.minimum(i + s, nb - 1), 0))
    return pl.pallas_call(
        body, name="prep1_bwd", grid=(nb,),
        in_specs=[row(PROJ_W), row(512), part(0), part(1), part(2), part(0), part(1), part(2),
                  row(256), row(128), row(128), vec(512), vec(512), vec(256), vec(128)],
        out_specs=[row(PROJ_W), vec(512), vec(512), vec(256), vec(128)],
        out_shape=[jax.ShapeDtypeStruct((t, PROJ_W), BF16)] + [jax.ShapeDtypeStruct((1, w), F32) for w in (512, 512, 256, 128)],
        compiler_params=_params(("arbitrary",)),
    )(proj, dqa, dkp, dkp, dkp, dvp, dvp, dvp, dcq, dckv, dkr, gq, gk, gcq, gckv)


def _roll(x, shift):
    return pltpu.roll(x, shift % LANES, 1)


def _rope(y, c, s1, s2):
    return y * c + _roll(y, -16) * s1 + _roll(y, 16) * s2


def _rope_bwd(d, c, s1, s2):
    return d * c + _roll(d * s1, 16) + _roll(d * s2, -16)


def _q_head_stats(x):
    lane = _lane(x.shape)
    mn = lane < HD
    mr = (lane >= HD) & (lane < HD + ROPE)
    x2 = x * x
    r = jnp.where(mn, lax.rsqrt(_seg_sum(x2, mn) * (1.0 / HD) + EPS),
                  lax.rsqrt(_seg_sum(x2, mr) * (1.0 / ROPE) + EPS))
    return mn, mr, r, x * r


def _kr_stats(x):
    r = lax.rsqrt(jnp.sum(x * x, axis=-1, keepdims=True) * (1.0 / ROPE) + EPS)
    return r, x * r


def prep2_fwd(qlat, kv, proj, gq, gk, gkr, tabs):
    t = qlat.shape[0]
    tm = A_TM

    def body(q_ref, kv_ref, kr_ref, gq_ref, gk_ref, gkr_ref, tab_ref, qf_ref, kf_ref, vp_ref):
        _, xn = _kr_stats(kr_ref[...])
        kpe = _roll(_rope(xn * gkr_ref[...], tab_ref[3], tab_ref[4], tab_ref[5]), 64)
        for h in range(HEADS):
            sl = slice(LANES * h, LANES * (h + 1))
            _, _, _, xn = _q_head_stats(q_ref[:, sl])
            qf_ref[:, sl] = _rope(xn * gq_ref[...], tab_ref[0], tab_ref[1], tab_ref[2]).astype(BF16)
            x = kv_ref[:, sl]
            lo = _lane(x.shape) < HD
            xk = jnp.where(lo, x, 0.0)
            rk = lax.rsqrt(jnp.sum(xk * xk, axis=-1, keepdims=True) * (1.0 / HD) + EPS)
            kf_ref[:, sl] = (xk * rk * gk_ref[...] + kpe).astype(BF16)
            if h % 2 == 0:
                v_even = _roll(x, 64)
            else:
                vp_ref[:, LANES * (h // 2):LANES * (h // 2 + 1)] = jnp.where(lo, v_even, x).astype(BF16)

    row = lambda w: pl.BlockSpec((tm, w), lambda i: (i, 0))
    vec = lambda w: pl.BlockSpec((1, w), lambda i: (0, 0))
    return pl.pallas_call(
        body, name="prep2_fwd", grid=(t // tm,),
        in_specs=[row(1024), row(1024), pl.BlockSpec((tm, LANES), lambda i: (i, 15)), vec(128), vec(128), vec(128),
                  pl.BlockSpec((6, tm, LANES), lambda i: (0, i, 0))],
        out_specs=[row(1024), row(1024), row(512)],
        out_shape=[jax.ShapeDtypeStruct((t, 1024), BF16), jax.ShapeDtypeStruct((t, 1024), BF16),
                   jax.ShapeDtypeStruct((t, 512), BF16)],
        compiler_params=_params(("parallel",)),
    )(qlat, kv, proj, gq, gk, gkr, tabs)


def prep2_bwd(qlat, kv, proj, dqf, dkf, dvp, gq, gk, gkr, tabs):
    t = qlat.shape[0]
    tm = A_TM

    def body(q_ref, kv_ref, kr_ref, dqf_ref, dkf_ref, dvp_ref, gq_ref, gk_ref, gkr_ref, tab_ref,
             dq_ref, dkv_ref, dkr_ref, dgq_ref, dgk_ref, dgkr_ref):
        i = pl.program_id(0)

        @pl.when(i == 0)
        def _():
            dgq_ref[...] = jnp.zeros_like(dgq_ref)
            dgk_ref[...] = jnp.zeros_like(dgk_ref)
            dgkr_ref[...] = jnp.zeros_like(dgkr_ref)

        dgq = jnp.zeros((1, LANES), F32)
        dgk = jnp.zeros((1, LANES), F32)
        dkpe = jnp.zeros((tm, LANES), F32)
        for h in range(HEADS):
            sl = slice(LANES * h, LANES * (h + 1))
            mn, mr, r, xn = _q_head_stats(q_ref[:, sl])
            dy = _rope_bwd(dqf_ref[sl, :].T, tab_ref[0], tab_ref[1], tab_ref[2])
            dyg = dy * gq_ref[...]
            z = dyg * xn
            mean = jnp.where(mn, _seg_sum(z, mn) * (1.0 / HD), _seg_sum(z, mr) * (1.0 / ROPE))
            dq_ref[:, sl] = (r * (dyg - xn * mean)).astype(BF16)
            dgq = dgq + jnp.sum(dy * xn, axis=0, keepdims=True)

            x = kv_ref[:, sl]
            dk = dkf_ref[:, sl]
            xk = jnp.where(mn, x, 0.0)
            rk = lax.rsqrt(jnp.sum(xk * xk, axis=-1, keepdims=True) * (1.0 / HD) + EPS)
            xkn = xk * rk
            dyk = jnp.where(mn, dk, 0.0)
            dykg = dyk * gk_ref[...]
            dxk = rk * (dykg - xkn * (jnp.sum(dykg * xkn, axis=-1, keepdims=True) * (1.0 / HD)))
            dgk = dgk + jnp.sum(dyk * xkn, axis=0, keepdims=True)
            dkpe = dkpe + jnp.where(mr, dk, 0.0)
            dvpair = dvp_ref[:, LANES * (h // 2):LANES * (h // 2 + 1)]
            dv = _roll(dvpair, 64) if h % 2 == 0 else dvpair
            dkv_ref[:, sl] = jnp.where(mn, dxk, dv).astype(BF16)

        r, xn = _kr_stats(kr_ref[...])
        dy = _rope_bwd(_roll(dkpe, 64), tab_ref[3], tab_ref[4], tab_ref[5])
        dyg = dy * gkr_ref[...]
        dkr_ref[...] = r * (dyg - xn * (jnp.sum(dyg * xn, axis=-1, keepdims=True) * (1.0 / ROPE)))
        dgq_ref[...] += dgq
        dgk_ref[...] += dgk
        dgkr_ref[...] += jnp.sum(dy * xn, axis=0, keepdims=True)

    row = lambda w: pl.BlockSpec((tm, w), lambda i: (i, 0))
    vec = lambda w: pl.BlockSpec((1, w), lambda i: (0, 0))
    return pl.pallas_call(
        body, name="prep2_bwd", grid=(t // tm,),
        in_specs=[row(1024), row(1024), pl.BlockSpec((tm, LANES), lambda i: (i, 15)),
                  pl.BlockSpec((1024, tm), lambda i: (0, i)), row(1024), row(512),
                  vec(128), vec(128), vec(128), pl.BlockSpec((6, tm, LANES), lambda i: (0, i, 0))],
        out_specs=[row(1024), row(1024), row(128), vec(128), vec(128), vec(128)],
        out_shape=[jax.ShapeDtypeStruct((t, 1024), BF16), jax.ShapeDtypeStruct((t, 1024), BF16),
                   jax.ShapeDtypeStruct((t, LANES), F32)] + [jax.ShapeDtypeStruct((1, LANES), F32)] * 3,
        compiler_params=_params(("arbitrary",)),
    )(qlat, kv, proj, dqf, dkf, dvp, gq, gk, gkr, tabs)


A_TQ = 256
A_WIN = 3 * A_TQ


def _a_specs(t):
    nb = t // A_TQ
    blk = lambda s: pl.BlockSpec((A_TQ, 512), lambda i: (jnp.maximum(i - s, 0), 0))
    return nb, blk


def _a_probs(qb, q_ref, kc, b_ref, head, sl, lo):
    hm = lo if head % 2 == 0 else ~lo
    qm = jnp.where(hm, q_ref[:, sl], jnp.zeros((), BF16))
    s = _dot_nt(qm, kc) + b_ref[head]
    col = lax.broadcasted_iota(jnp.int32, s.shape, 1)
    s = jnp.where(col >= 2 * A_TQ - A_TQ * qb, s, NEG)
    e = jnp.exp(s - jnp.max(s, axis=-1, keepdims=True))
    pr = e * (1.0 / jnp.sum(e, axis=-1, keepdims=True))
    return hm, qm, pr


def attn_a_fwd(qa, ka, va, bias):
    t = qa.shape[0]
    nb, blk = _a_specs(t)

    def body(q_ref, k2_ref, k1_ref, k0_ref, v2_ref, v1_ref, v0_ref, b_ref, o_ref):
        qb = pl.program_id(0)
        lo = _lane((A_TQ, LANES)) < HD
        for p in range(4):
            sl = slice(LANES * p, LANES * (p + 1))
            kc = jnp.concatenate([k2_ref[:, sl], k1_ref[:, sl], k0_ref[:, sl]], axis=0)
            vc = jnp.concatenate([v2_ref[:, sl], v1_ref[:, sl], v0_ref[:, sl]], axis=0)
            outs = []
            for h2 in range(2):
                _, _, pr = _a_probs(qb, q_ref, kc, b_ref, 2 * p + h2, sl, lo)
                outs.append(_dot(pr.astype(BF16), vc))
            o_ref[:, sl] = jnp.where(lo, outs[0], outs[1]).astype(BF16)

    return pl.pallas_call(
        body, name="attn_a_fwd", grid=(nb,),
        in_specs=[blk(0), blk(2), blk(1), blk(0), blk(2), blk(1), blk(0),
                  pl.BlockSpec((HEADS, A_TQ, A_WIN), lambda i: (0, 0, 0))],
        out_specs=pl.BlockSpec((A_TQ, 512), lambda i: (i, 0)),
        out_shape=jax.ShapeDtypeStruct((t, 512), BF16),
        compiler_params=_params(("parallel",)),
    )(qa, ka, ka, ka, va, va, va, bias)


def attn_a_bwd(qa, ka, va, bias, do):
    t = qa.shape[0]
    nb, blk = _a_specs(t)

    def body(q_ref, k2_ref, k1_ref, k0_ref, v2_ref, v1_ref, v0_ref, b_ref, do_ref, dq_ref, dk_ref, dv_ref, db_ref):
        qb = pl.program_id(0)

        @pl.when(qb == 0)
        def _():
            db_ref[...] = jnp.zeros_like(db_ref)

        lo = _lane((A_TQ, LANES)) < HD
        for p in range(4):
            sl = slice(LANES * p, LANES * (p + 1))
            kc = jnp.concatenate([k2_ref[:, sl], k1_ref[:, sl], k0_ref[:, sl]], axis=0)
            vc = jnp.concatenate([v2_ref[:, sl], v1_ref[:, sl], v0_ref[:, sl]], axis=0)
            dqs = []
            dkc = jnp.zeros((A_WIN, LANES), F32)
            dvc = jnp.zeros((A_WIN, LANES), F32)
            for h2 in range(2):
                head = 2 * p + h2
                hm, qm, pr = _a_probs(qb, q_ref, kc, b_ref, head, sl, lo)
                dom = jnp.where(hm, do_ref[:, sl], jnp.zeros((), BF16))
                dp = _dot_nt(dom, vc)
                ds = pr * (dp - jnp.sum(pr * dp, axis=-1, keepdims=True))
                db_ref[head] += ds
                dsb = ds.astype(BF16)
                dqs.append(_dot(dsb, kc))
                dkc = dkc + _dot_tn(dsb, qm)
                dvc = dvc + _dot_tn(pr.astype(BF16), dom)
            dq_ref[:, sl] = jnp.where(lo, dqs[0], dqs[1])
            for s in range(3):
                rows = slice(A_TQ * (2 - s), A_TQ * (3 - s))
                dk_ref[s, :, sl] = dkc[rows]
                dv_ref[s, :, sl] = dvc[rows]

    share = pl.BlockSpec((3, A_TQ, 512), lambda i: (0, i, 0))
    return pl.pallas_call(
        body, name="attn_a_bwd", grid=(nb,),
        in_specs=[blk(0), blk(2), blk(1), blk(0), blk(2), blk(1), blk(0),
                  pl.BlockSpec((HEADS, A_TQ, A_WIN), lambda i: (0, 0, 0)), blk(0)],
        out_specs=[pl.BlockSpec((A_TQ, 512), lambda i: (i, 0)), share, share,
                   pl.BlockSpec((HEADS, A_TQ, A_WIN), lambda i: (0, 0, 0))],
        out_shape=[jax.ShapeDtypeStruct((t, 512), F32), jax.ShapeDtypeStruct((3, t, 512), F32),
                   jax.ShapeDtypeStruct((3, t, 512), F32), jax.ShapeDtypeStruct((HEADS, A_TQ, A_WIN), F32)],
        compiler_params=_params(("arbitrary",)),
    )(qa, ka, ka, ka, va, va, va, bias, do)


B_T = 1024


B_SCALE2 = B_SCALE * 1.4426950408889634


def _tri_tables(n, by_query):
    pairs = [(i, j) for i in range(n) for j in range(i + 1)] if by_query else [(i, j) for j in range(n) for i in range(j, n)]
    return (np.asarray([p[0] for p in pairs], np.int32), np.asarray([p[1] for p in pairs], np.int32))


def _b_mask_t(s):
    kc = lax.broadcasted_iota(jnp.int32, s.shape, 0) // CHUNK
    qc = lax.broadcasted_iota(jnp.int32, s.shape, 1) // CHUNK
    return jnp.where(kc <= qc, s, NEG)


def attn_b_fwd(qf, kf, vp):
    t = qf.shape[0]
    n = t // B_T
    qtab, ktab = _tri_tables(n, by_query=True)

    def body(qt_ref, kt_ref, q_ref, k_ref, v_ref, o_ref, lse_ref, m_s, l_s, acc_s):
        qb, kb = qt_ref[pl.program_id(1)], kt_ref[pl.program_id(1)]

        @pl.when(kb == 0)
        def _():
            m_s[...] = jnp.full_like(m_s, NEG)
            l_s[...] = jnp.zeros_like(l_s)
            acc_s[...] = jnp.zeros_like(acc_s)

        def step(masked):
            v = v_ref[...]
            for h2 in range(2):
                sl = slice(LANES * h2, LANES * (h2 + 1))
                s = _dot_nt(k_ref[:, sl], q_ref[:, sl]) * B_SCALE2
                if masked:
                    s = _b_mask_t(s)
                m_prev = m_s[h2]
                m_new = jnp.maximum(m_prev, jnp.max(s, axis=0, keepdims=True))
                alpha = jnp.exp2(m_prev - m_new)
                pr = jnp.exp2(s - m_new)
                l_s[h2] = alpha * l_s[h2] + jnp.sum(pr, axis=0, keepdims=True)
                acc_s[h2] = alpha * acc_s[h2] + _dot_tn(v, pr.astype(BF16))
                m_s[h2] = m_new

        @pl.when(kb < qb)
        def _():
            step(False)

        @pl.when(kb == qb)
        def _():
            step(True)
            for h2 in range(2):
                l = l_s[h2]
                rows = slice(HD * h2, HD * (h2 + 1))
                o_ref[rows, :] = (acc_s[h2, rows, :] * (1.0 / l)).astype(BF16)
                lse_ref[0, h2:h2 + 1, :] = m_s[h2] + jnp.log2(l)

    grid_spec = pltpu.PrefetchScalarGridSpec(
        num_scalar_prefetch=2, grid=(4, len(qtab)),
        in_specs=[pl.BlockSpec((B_T, 256), lambda p, s, qt, kt: (qt[s], p)),
                  pl.BlockSpec((B_T, 256), lambda p, s, qt, kt: (kt[s], p)),
                  pl.BlockSpec((B_T, LANES), lambda p, s, qt, kt: (kt[s], p))],
        out_specs=[pl.BlockSpec((LANES, B_T), lambda p, s, qt, kt: (p, qt[s])),
                   pl.BlockSpec((1, 2, B_T), lambda p, s, qt, kt: (p, 0, qt[s]))],
        scratch_shapes=[pltpu.VMEM((2, 1, B_T), F32), pltpu.VMEM((2, 1, B_T), F32), pltpu.VMEM((2, LANES, B_T), F32)])
    return pl.pallas_call(
        body, name="attn_b_fwd", grid_spec=grid_spec,
        out_shape=[jax.ShapeDtypeStruct((512, t), BF16), jax.ShapeDtypeStruct((4, 2, t), F32)],
        compiler_params=_params(("parallel", "arbitrary")),
    )(jnp.asarray(qtab), jnp.asarray(ktab), qf, kf, vp)


def attn_b_bwd(qf, kf, vp, do, do_t, o_t, lse, scatter=()):
    t = qf.shape[0]
    n = t // B_T
    qtab, ktab = _tri_tables(n, by_query=False)
    plan = ScatterPlan(scatter)
    m = plan.n
    last = len(qtab) - 1

    def body(*refs):
        qt_ref, kt_ref, q_ref, k_ref, v_ref, do_ref, dot_ref, ot_ref, lse_ref = refs[:9]
        ins, (dq_ref, dk_ref, dv_ref), outs = refs[9:9 + m], refs[9 + m:12 + m], refs[12 + m:12 + 2 * m]
        sems = refs[12 + 2 * m:]
        qb, kb = qt_ref[pl.program_id(1)], kt_ref[pl.program_id(1)]
        if m:
            pl.when((pl.program_id(0) == 0) & (pl.program_id(1) == 0))(lambda: plan.start(ins, outs, sems))

        @pl.when(pl.program_id(1) == 0)
        def _():
            dq_ref[...] = jnp.zeros_like(dq_ref)

        @pl.when(qb == kb)
        def _():
            dk_ref[...] = jnp.zeros_like(dk_ref)
            dv_ref[...] = jnp.zeros_like(dv_ref)

        def step(masked):
            cols = pl.ds(pl.multiple_of(qb * B_T, B_T), B_T)
            v = v_ref[...]
            dov = do_ref[...]
            prod = dot_ref[...].astype(F32) * ot_ref[...].astype(F32)
            lo = _lane((B_T, LANES)) < HD
            for h2 in range(2):
                sl = slice(LANES * h2, LANES * (h2 + 1))
                hm = lo if h2 == 0 else ~lo
                q = q_ref[:, sl]
                k = k_ref[:, sl]
                dom = jnp.where(hm, dov, jnp.zeros((), BF16))
                delta = jnp.sum(prod[HD * h2:HD * (h2 + 1), :], axis=0, keepdims=True)
                s = _dot_nt(k, q) * B_SCALE2
                if masked:
                    s = _b_mask_t(s)
                pr = jnp.exp2(s - lse_ref[0, h2:h2 + 1, :])
                dp = _dot_nt(v, dom)
                ds = (pr * (dp - delta) * B_SCALE).astype(BF16)
                dk_ref[:, sl] += _dot(ds, q)
                dv_ref[...] += _dot(pr.astype(BF16), dom)
                dq_ref[sl, cols] += _dot_tn(k, ds)

        @pl.when(qb > kb)
        def _():
            step(False)

        @pl.when(qb == kb)
        def _():
            step(True)

        if m:
            pl.when((pl.program_id(0) == 3) & (pl.program_id(1) == last))(lambda: plan.finish(ins, outs, sems))

    qrow = lambda w: pl.BlockSpec((B_T, w), lambda p, s, qt, kt: (qt[s], p))
    qcol = pl.BlockSpec((LANES, B_T), lambda p, s, qt, kt: (p, qt[s]))
    krow = lambda w: pl.BlockSpec((B_T, w), lambda p, s, qt, kt: (kt[s], p))
    grid_spec = pltpu.PrefetchScalarGridSpec(
        num_scalar_prefetch=2, grid=(4, len(qtab)),
        in_specs=[qrow(256), krow(256), krow(LANES), qrow(LANES), qcol, qcol,
                  pl.BlockSpec((1, 2, B_T), lambda p, s, qt, kt: (p, 0, qt[s]))] + [ANY] * m,
        out_specs=[pl.BlockSpec((256, t), lambda p, s, qt, kt: (p, 0)), krow(256), krow(LANES)] + [ANY] * m,
        scratch_shapes=plan.scratch if m else [])
    return pl.pallas_call(
        body, name="attn_b_bwd", grid_spec=grid_spec,
        out_shape=[jax.ShapeDtypeStruct((1024, t), F32), jax.ShapeDtypeStruct((t, 1024), F32),
                   jax.ShapeDtypeStruct((t, 512), F32)] + plan.out_shape,
        compiler_params=_params(("arbitrary", "arbitrary")),
    )(jnp.asarray(qtab), jnp.asarray(ktab), qf, kf, vp, do, do_t, o_t, lse, *scatter)


_U_LEN = A_TQ + A_WIN - 1


def _band_mask():
    a = np.arange(A_TQ)[:, None] // CHUNK
    b = np.arange(A_WIN)[None, :] // CHUNK
    return (b >= a) & (b <= a + A_LEFT)


def bias_block(table):
    h = table.shape[0]
    n_lo = A_WIN - 1 - 2 * A_TQ - A_MAX_REL
    ext = jnp.concatenate([jnp.repeat(table[:, :1], n_lo, axis=1), table,
                           jnp.repeat(table[:, -1:], _U_LEN - n_lo - table.shape[1], axis=1)], axis=1)
    row = jnp.pad(ext[:, ::-1], ((0, 0), (0, 1)))
    flat = jnp.tile(row, (1, A_TQ))[:, :A_TQ * _U_LEN]
    skew = flat.reshape(h, A_TQ, _U_LEN)
    toep = skew[:, :, A_TQ - 1:A_TQ - 1 + A_WIN]
    return jnp.where(jnp.asarray(_band_mask())[None], toep, NEG)


def bias_block_grad(db):
    h = db.shape[0]
    n_lo = A_WIN - 1 - 2 * A_TQ - A_MAX_REL
    skew = jnp.pad(db, ((0, 0), (0, 0), (A_TQ - 1, 0)))
    flat = jnp.pad(skew.reshape(h, A_TQ * _U_LEN), ((0, 0), (0, A_TQ)))
    ext = jnp.sum(flat.reshape(h, A_TQ, _U_LEN + 1), axis=1)[:, :_U_LEN][:, ::-1]
    n_tab = 2 * A_MAX_REL + 1
    first = jnp.sum(ext[:, :n_lo + 1], axis=1, keepdims=True)
    last = jnp.sum(ext[:, n_lo + n_tab - 1:], axis=1, keepdims=True)
    return jnp.concatenate([first, ext[:, n_lo + 1:n_lo + n_tab - 1], last], axis=1)


def rope_tabs(t):
    inv = 1.0 / (10000.0 ** (jnp.arange(0, ROPE, 2, dtype=F32) / ROPE))
    ang = jnp.arange(t, dtype=F32)[:, None] * inv[None, :]
    cos, sin = jnp.cos(ang), jnp.sin(ang)
    z = lambda w: jnp.zeros((t, w), F32)
    ck = jnp.concatenate([cos, cos, z(96)], axis=1)
    s1k = jnp.concatenate([-sin, z(112)], axis=1)
    s2k = jnp.concatenate([z(16), sin, z(96)], axis=1)
    cq = jnp.concatenate([jnp.ones((t, HD), F32), cos, cos, z(32)], axis=1)
    s1q = jnp.concatenate([z(HD), -sin, z(48)], axis=1)
    s2q = jnp.concatenate([z(HD + 16), sin, z(32)], axis=1)
    return jnp.stack([cq, s1q, s2q, ck, s1k, s2k])


def _pad_lanes(v, width):
    return jnp.pad(v, ((0, 0), (0, width - v.shape[1])))


LATE = ("w_in", "b_w_uq", "b_w_ukv", "w_out", "ffn2_w_gate", "ffn2_w_up", "ffn2_w_down")
FFN2 = ("ffn2_w_gate", "ffn2_w_up", "ffn2_w_down")


def kernel_layout(gathered):
    w = {n: v for n, v in gathered.items() if n.startswith("ffn")}
    if "w_in" in gathered:
        w["w_in"] = jnp.pad(_shards_to_cols(gathered["w_in"]), ((0, 0), (0, PROJ_W - IN_COLS)))
        uq = _shards_to_cols(gathered["b_w_uq"]).reshape(256, HEADS, HD + ROPE)
        w["b_w_uq"] = jnp.pad(uq, ((0, 0), (0, 0), (0, LANES - HD - ROPE))).reshape(256, HEADS * LANES)
        w["b_w_ukv"] = _shards_to_cols(gathered["b_w_ukv"])
        w["w_out"] = gathered["w_out"].reshape(N_SHARD * gathered["w_out"].shape[1], D_MODEL)
    return w


def local_step(x, target, w, late=None):
    t = x.shape[0]
    gq = jnp.tile(w["a_q_norm"], (1, HEADS))
    gk = jnp.tile(w["a_k_norm"], (1, HEADS))
    gq128 = _pad_lanes(jnp.concatenate([w["b_q_nope_norm"], w["b_q_rope_norm"]], axis=1), LANES)
    gk128 = _pad_lanes(w["b_k_nope_norm"], LANES)
    gkr128 = _pad_lanes(w["b_k_rope_norm"], LANES)
    tabs = rope_tabs(t)
    bias = bias_block(w["a_rel_bias"])

    if late is None:
        x1 = ffn_fwd(x, w["ffn1_norm"], w["ffn1_w_gate"], w["ffn1_w_up"], w["ffn1_w_down"], "ffn_fwd")
    else:
        own, shard = late
        x1, *got = ffn_fwd(x, w["ffn1_norm"], w["ffn1_w_gate"], w["ffn1_w_up"], w["ffn1_w_down"], "ffn_fwd_gather",
                           gather=own)
        w = dict(w, **kernel_layout({n: lax.dynamic_update_index_in_dim(g_, o_, shard, 0)
                                     for n, g_, o_ in zip(LATE, got, own)}))
    h, proj = mix_proj(x1, w["mix_norm"], w["w_in"])
    qa, ka, va, cqn, ckvn = prep1_fwd(proj, gq, gk, w["b_q_lat_norm"], w["b_kv_lat_norm"])
    qlat = matmul(cqn, w["b_w_uq"], "nn", "uq_fwd", 512, 1024, 256)
    kv = matmul(ckvn, w["b_w_ukv"], "nn", "ukv_fwd", 512, 1024, 128)
    qf, kf, vp = prep2_fwd(qlat, kv, proj, gq128, gk128, gkr128, tabs)
    oa = attn_a_fwd(qa, ka, va, bias)
    ob_t, lse = attn_b_fwd(qf, kf, vp)
    x2 = out_proj(x1, oa, ob_t, w["w_out"])
    x3 = ffn_fwd(x2, w["ffn2_norm"], w["ffn2_w_gate"], w["ffn2_w_up"], w["ffn2_w_down"], "ffn_fwd")

    g = {}
    dx3, g["final_norm"], loss = final_loss(x3, w["final_norm"], target)
    g["ffn2_w_gate"], g["ffn2_w_up"], g["ffn2_w_down"], dhp = ffn_bwd(
        x2, dx3, w["ffn2_norm"], w["ffn2_w_gate"], w["ffn2_w_up"], w["ffn2_w_down"], "ffn_bwd")
    dx2, g["ffn2_norm"] = norm_bwd(x2, w["ffn2_norm"], dhp, dx3, "ffn_norm_bwd")
    d_oa, d_ob, d_ob_t = out_proj_bwd(dx2, w["w_out"])
    g["w_out"] = jnp.concatenate([matmul(oa, dx2, "tn", "w_out_a_bwd", 512, 1024, 512, BF16),
                                  matmul(ob_t, dx2, "nn", "w_out_b_bwd", 512, 1024, 512, BF16)], axis=0)
    early = [g[n] for n in FFN2] + [g["w_out"].reshape(N_SHARD, -1, D_MODEL)]
    dqf, dkf, dvp, *landed_early = attn_b_bwd(qf, kf, vp, d_ob, d_ob_t, ob_t, lse,
                                              scatter=() if late is None else early)
    dqa, dkp, dvpa, dbias = attn_a_bwd(qa, ka, va, bias, d_oa)
    dqlat, dkv, dkr, dgq128, dgk128, dgkr128 = prep2_bwd(qlat, kv, proj, dqf, dkf, dvp, gq128, gk128, gkr128, tabs)
    dcq = matmul(dqlat, w["b_w_uq"], "nt", "uq_bwd_x", 512, 256, 1024)
    g["b_w_uq"] = matmul(cqn, dqlat, "tn", "uq_bwd_w", 256, 1024, 512, BF16)
    dckv = matmul(dkv, w["b_w_ukv"], "nt", "ukv_bwd_x", 512, 128, 1024)
    g["b_w_ukv"] = matmul(ckvn, dkv, "tn", "ukv_bwd_w", 128, 1024, 512, BF16)
    dproj, dgq, dgk, g["b_q_lat_norm"], g["b_kv_lat_norm"] = prep1_bwd(
        proj, dqa, dkp, dvpa, dcq, dckv, dkr, gq, gk, w["b_q_lat_norm"], w["b_kv_lat_norm"])
    dh = matmul(dproj, w["w_in"], "nt", "w_in_bwd_x", 512, 1024, PROJ_W)
    g["w_in"] = matmul(h, dproj, "tn", "w_in_bwd_w", 1024, 1024, 512, BF16)
    dx1, g["mix_norm"] = norm_bwd(x1, w["mix_norm"], dh[None], dx2, "mix_norm_bwd")
    mid = [_cols_to_shards(g["w_in"][:, :IN_COLS]),
           _cols_to_shards(g["b_w_uq"].reshape(256, HEADS, LANES)[:, :, :HD + ROPE].reshape(256, -1)),
           _cols_to_shards(g["b_w_ukv"])]
    g["ffn1_w_gate"], g["ffn1_w_up"], g["ffn1_w_down"], dhp, *landed_mid = ffn_bwd(
        x, dx1, w["ffn1_norm"], w["ffn1_w_gate"], w["ffn1_w_up"], w["ffn1_w_down"],
        "ffn_bwd" if late is None else "ffn_bwd_scatter", scatter=() if late is None else mid)
    grad_x, g["ffn1_norm"] = norm_bwd(x, w["ffn1_norm"], dhp, dx1, "ffn_norm_bwd")
    landed = dict(zip(FFN2 + ("w_out", "w_in", "b_w_uq", "b_w_ukv"), landed_early + landed_mid))

    g["a_q_norm"] = jnp.sum(dgq.reshape(HEADS, HD), axis=0, keepdims=True)
    g["a_k_norm"] = jnp.sum(dgk.reshape(HEADS, HD), axis=0, keepdims=True)
    g["a_rel_bias"] = bias_block_grad(dbias)
    g["b_q_nope_norm"] = dgq128[:, :HD]
    g["b_q_rope_norm"] = dgq128[:, HD:HD + ROPE]
    g["b_k_nope_norm"] = dgk128[:, :HD]
    g["b_k_rope_norm"] = dgkr128[:, :ROPE]
    return loss, grad_x, g, landed


ANY = pl.BlockSpec(memory_space=pl.ANY)
N_DEV = 8


def _place():
    return lax.axis_index("x"), lax.axis_index("y"), lax.axis_index("c")


def _flip(v, bit):
    return 1 - v if bit else v


class GatherPlan:
    def __init__(self, ws):
        self.rows = [w.shape[0] for w in ws]
        self.n = len(ws)
        self.out_shape = [jax.ShapeDtypeStruct((N_SHARD,) + w.shape, w.dtype) for w in ws]
        self.scratch = [pltpu.SemaphoreType.DMA((6 * self.n,)), pltpu.SemaphoreType.DMA((6 * self.n,))]

    def _copies(self, ins, outs, sems):
        x, y, c = _place()
        s_me = 2 * x + y
        sibling = (x, y, 1 - c)
        send_sems, recv_sems = sems

        def remote(k, src, dst, to):
            return pltpu.make_async_remote_copy(src_ref=src, dst_ref=dst, send_sem=send_sems.at[k],
                                                recv_sem=recv_sems.at[k], device_id=to, device_id_type=MESH)

        ici, fwd = [], []
        for a in range(self.n):
            hr = self.rows[a] // 2
            mine, theirs = pl.ds(c * hr, hr), pl.ds((1 - c) * hr, hr)
            for j, (cx, cy) in enumerate([(1 - x, y), (x, 1 - y), (1 - x, 1 - y)]):
                got = outs[a].at[2 * cx + cy, mine]
                ici.append((remote(6 * a + j, ins[a].at[mine], outs[a].at[s_me, mine], (cx, cy, c)),
                            remote(6 * a + j, got, got, (cx, cy, c))))
                passed = outs[a].at[2 * cx + cy, theirs]
                fwd.append((remote(6 * a + 3 + j, got, got, sibling), remote(6 * a + 3 + j, passed, passed, sibling)))
        return ici, fwd

    def start(self, ins, outs, sems):
        for send, _ in self._copies(ins, outs, sems)[0]:
            send.start()

    def forward(self, ins, outs, sems):
        ici, fwd = self._copies(ins, outs, sems)
        for (_, arrival), (send, _) in zip(ici, fwd):
            arrival.wait_recv()
            send.start()

    def finish(self, ins, outs, sems):
        ici, fwd = self._copies(ins, outs, sems)
        for _, arrival in fwd:
            arrival.wait_recv()
        for send, _ in ici + fwd:
            send.wait_send()


def allgather_shards(ws):
    plan = GatherPlan(ws)
    n = plan.n

    def body(*refs):
        ins, outs, sems = refs[:n], refs[n:2 * n], refs[2 * n:]
        plan.start(ins, outs, sems)
        plan.forward(ins, outs, sems)
        plan.finish(ins, outs, sems)

    return pl.pallas_call(
        body, name="allgather_shards", in_specs=[ANY] * n, out_specs=[ANY] * n,
        out_shape=plan.out_shape, scratch_shapes=plan.scratch,
    )(*ws)


class ScatterPlan:
    def __init__(self, gs):
        self.rows = [g.shape[1] for g in gs]
        self.n = len(gs)
        self.out_shape = [jax.ShapeDtypeStruct((N_DEV, g.shape[1] // 2, g.shape[2]), g.dtype) for g in gs]
        self.scratch = [pltpu.SemaphoreType.DMA((7 * self.n,)), pltpu.SemaphoreType.DMA((7 * self.n,)),
                        pltpu.SemaphoreType.DMA((self.n,))]

    def _copies(self, ins, outs, sems):
        x, y, c = _place()
        me = 4 * x + 2 * y + c
        send_sems, recv_sems, local_sems = sems
        local, sends, arrivals = [], [], []
        for a in range(self.n):
            hr = self.rows[a] // 2
            local.append(pltpu.make_async_copy(ins[a].at[2 * x + y, pl.ds(c * hr, hr)], outs[a].at[me], local_sems.at[a]))
            for k in range(1, N_DEV):
                px, py, pc = _flip(x, k & 4), _flip(y, k & 2), _flip(c, k & 1)
                sem = dict(send_sem=send_sems.at[7 * a + k - 1], recv_sem=recv_sems.at[7 * a + k - 1],
                           device_id=(px, py, pc), device_id_type=MESH)
                sends.append(pltpu.make_async_remote_copy(
                    src_ref=ins[a].at[2 * px + py, pl.ds(pc * hr, hr)], dst_ref=outs[a].at[me], **sem))
                slot = outs[a].at[4 * px + 2 * py + pc]
                arrivals.append(pltpu.make_async_remote_copy(src_ref=slot, dst_ref=slot, **sem))
        return local, sends, arrivals

    def start(self, ins, outs, sems):
        local, sends, _ = self._copies(ins, outs, sems)
        for cp in local + sends:
            cp.start()

    def finish(self, ins, outs, sems):
        local, sends, arrivals = self._copies(ins, outs, sems)
        for cp in arrivals:
            cp.wait_recv()
        for cp in sends:
            cp.wait_send()
        for cp in local:
            cp.wait()


def scatter_partials(gs):
    plan = ScatterPlan(gs)
    n = plan.n

    def body(*refs):
        ins, outs, sems = refs[:n], refs[n:2 * n], refs[2 * n:]
        plan.start(ins, outs, sems)
        plan.finish(ins, outs, sems)

    return pl.pallas_call(
        body, name="scatter_partials", in_specs=[ANY] * n, out_specs=[ANY] * n,
        out_shape=plan.out_shape, scratch_shapes=plan.scratch,
    )(*gs)


def sum_slots(land, name):
    _, rows, cols = land.shape
    tr = rows if rows <= 128 else rows // 2

    def body(l_ref, o_ref):
        acc = l_ref[0].astype(F32)
        for s in range(1, N_DEV):
            acc = acc + l_ref[s].astype(F32)
        o_ref[...] = acc

    return pl.pallas_call(
        body, name=name, grid=(rows // tr,),
        in_specs=[pl.BlockSpec((N_DEV, tr, cols), lambda i: (0, i, 0))],
        out_specs=pl.BlockSpec((tr, cols), lambda i: (i, 0)),
        out_shape=jax.ShapeDtypeStruct((rows, cols), F32),
        compiler_params=_params(("parallel",)),
    )(land)


def join_halves(hs):
    n = len(hs)

    def body(*refs):
        ins, outs = refs[:n], refs[n:2 * n]
        send_sems, recv_sems = refs[2 * n:]
        x, y, c = _place()
        sends = []
        for a in range(n):
            hr = hs[a].shape[0]
            mine = outs[a].at[pl.ds(c * hr, hr)]
            sends.append(pltpu.make_async_remote_copy(
                src_ref=ins[a], dst_ref=mine, send_sem=send_sems.at[a], recv_sem=recv_sems.at[a],
                device_id=(x, y, 1 - c), device_id_type=MESH))
            sends[-1].start()
        for a in range(n):
            hr = hs[a].shape[0]
            theirs = outs[a].at[pl.ds((1 - c) * hr, hr)]
            pltpu.make_async_remote_copy(
                src_ref=theirs, dst_ref=theirs, send_sem=send_sems.at[a], recv_sem=recv_sems.at[a],
                device_id=(x, y, 1 - c), device_id_type=MESH).wait_recv()
        for cp in sends:
            cp.wait_send()

    return pl.pallas_call(
        body, name="join_halves",
        in_specs=[ANY] * n, out_specs=[ANY] * n,
        out_shape=[jax.ShapeDtypeStruct((2 * h.shape[0], h.shape[1]), h.dtype) for h in hs],
        scratch_shapes=[pltpu.SemaphoreType.DMA((n,)), pltpu.SemaphoreType.DMA((n,))],
    )(*hs)


def allreduce_small(vec):
    def body(v_ref, o_ref, land_ref, send_sems, recv_sems):
        x, y, c = _place()
        me = 4 * x + 2 * y + c
        land_ref[me] = v_ref[...]
        sends = []
        for k in range(1, N_DEV):
            px, py, pc = _flip(x, k & 4), _flip(y, k & 2), _flip(c, k & 1)
            sends.append(pltpu.make_async_remote_copy(
                src_ref=v_ref, dst_ref=land_ref.at[me], send_sem=send_sems.at[k - 1], recv_sem=recv_sems.at[k - 1],
                device_id=(px, py, pc), device_id_type=MESH))
            sends[-1].start()
        for k in range(1, N_DEV):
            px, py, pc = _flip(x, k & 4), _flip(y, k & 2), _flip(c, k & 1)
            slot = land_ref.at[4 * px + 2 * py + pc]
            pltpu.make_async_remote_copy(
                src_ref=slot, dst_ref=slot, send_sem=send_sems.at[k - 1], recv_sem=recv_sems.at[k - 1],
                device_id=(px, py, pc), device_id_type=MESH).wait_recv()
        for cp in sends:
            cp.wait_send()
        acc = land_ref[0]
        for s in range(1, N_DEV):
            acc = acc + land_ref[s]
        o_ref[...] = acc

    vm = pl.BlockSpec(memory_space=pltpu.VMEM)
    return pl.pallas_call(
        body, name="allreduce_small",
        in_specs=[vm], out_specs=vm,
        out_shape=jax.ShapeDtypeStruct(vec.shape, F32),
        scratch_shapes=[pltpu.VMEM((N_DEV,) + vec.shape, F32), pltpu.SemaphoreType.DMA((N_DEV - 1,)),
                        pltpu.SemaphoreType.DMA((N_DEV - 1,))],
    )(vec)


def adamw(w, g, m, v, name):
    rows, cols = w.shape
    tr = rows
    while tr * cols * 4 * 14 > 24 * 1024 * 1024 and tr % 16 == 0:
        tr //= 2
    c1 = 1.0 - ADAM_B1 ** ADAM_STEP
    c2 = 1.0 - ADAM_B2 ** ADAM_STEP

    def body(w_ref, g_ref, m_ref, v_ref, d_ref, nm_ref, nv_ref):
        gv = g_ref[...]
        nm = ADAM_B1 * m_ref[...] + (1.0 - ADAM_B1) * gv
        nv = ADAM_B2 * v_ref[...] + (1.0 - ADAM_B2) * (gv * gv)
        nm_ref[...] = nm
        nv_ref[...] = nv
        d_ref[...] = -ADAM_LR * ((nm / c1) / (jnp.sqrt(nv / c2) + ADAM_EPS) + ADAM_WD * w_ref[...])

    blk = pl.BlockSpec((tr, cols), lambda i: (i, 0))
    return pl.pallas_call(
        body, name=name, grid=(rows // tr,),
        in_specs=[blk] * 4, out_specs=[blk] * 3,
        out_shape=[jax.ShapeDtypeStruct((rows, cols), F32)] * 3,
        compiler_params=_params(("parallel",)),
    )(w, g, m, v)


BIG = ("ffn1_w_gate", "ffn1_w_up", "ffn1_w_down", "w_in", "b_w_uq", "b_w_ukv", "w_out",
       "ffn2_w_gate", "ffn2_w_up", "ffn2_w_down")
SMALL = ("ffn1_norm", "mix_norm", "a_q_norm", "a_k_norm", "a_rel_bias", "b_q_lat_norm", "b_kv_lat_norm",
         "b_q_nope_norm", "b_q_rope_norm", "b_k_nope_norm", "b_k_rope_norm", "ffn2_norm", "final_norm")
WEIGHTS = ("ffn1_norm", "ffn1_w_gate", "ffn1_w_up", "ffn1_w_down", "mix_norm", "w_in", "a_q_norm", "a_k_norm",
           "a_rel_bias", "b_q_lat_norm", "b_w_uq", "b_kv_lat_norm", "b_w_ukv", "b_q_nope_norm", "b_q_rope_norm",
           "b_k_nope_norm", "b_k_rope_norm", "w_out", "ffn2_norm", "ffn2_w_gate", "ffn2_w_up", "ffn2_w_down",
           "final_norm")
PACK_SHAPE = (8, 1024)


def _pack_small(d, last=None):
    flat = [d[n].reshape(-1) for n in SMALL]
    used = sum(f.shape[0] for f in flat)
    total = PACK_SHAPE[0] * PACK_SHAPE[1]
    tail = jnp.zeros((total - used - 1,), F32)
    end = jnp.zeros((1,), F32) if last is None else last.reshape(1)
    return jnp.concatenate(flat + [tail, end]).reshape(PACK_SHAPE)


def _unpack_small(p, like):
    flat = p.reshape(-1)
    out, off = {}, 0
    for n in SMALL:
        size = like[n].size
        out[n] = flat[off:off + size].reshape(like[n].shape)
        off += size
    return out, flat[-1]


def _cols_to_shards(g):
    rows, cols = g.shape
    return g.reshape(rows, N_SHARD, cols // N_SHARD).transpose(1, 0, 2)


def _shards_to_cols(g):
    return g.transpose(1, 0, 2).reshape(g.shape[1], -1)


def kernel(x, ffn1_norm, ffn1_w_gate, ffn1_w_up, ffn1_w_down, mix_norm, w_in, a_q_norm, a_k_norm, a_rel_bias, b_q_lat_norm, b_w_uq, b_kv_lat_norm, b_w_ukv, b_q_nope_norm, b_q_rope_norm, b_k_nope_norm, b_k_rope_norm, w_out, ffn2_norm, ffn2_w_gate, ffn2_w_up, ffn2_w_down, final_norm, loss_target, m_ffn1_norm, m_ffn1_w_gate, m_ffn1_w_up, m_ffn1_w_down, m_mix_norm, m_w_in, m_a_q_norm, m_a_k_norm, m_a_rel_bias, m_b_q_lat_norm, m_b_w_uq, m_b_kv_lat_norm, m_b_w_ukv, m_b_q_nope_norm, m_b_q_rope_norm, m_b_k_nope_norm, m_b_k_rope_norm, m_w_out, m_ffn2_norm, m_ffn2_w_gate, m_ffn2_w_up, m_ffn2_w_down, m_final_norm, v_ffn1_norm, v_ffn1_w_gate, v_ffn1_w_up, v_ffn1_w_down, v_mix_norm, v_w_in, v_a_q_norm, v_a_k_norm, v_a_rel_bias, v_b_q_lat_norm, v_b_w_uq, v_b_kv_lat_norm, v_b_w_ukv, v_b_q_nope_norm, v_b_q_rope_norm, v_b_k_nope_norm, v_b_k_rope_norm, v_w_out, v_ffn2_norm, v_ffn2_w_gate, v_ffn2_w_up, v_ffn2_w_down, v_final_norm):
    args = locals()
    wts = {n: args[n][0] for n in WEIGHTS}
    mom = {n: args["m_" + n][0] for n in WEIGHTS}
    var = {n: args["v_" + n][0] for n in WEIGHTS}

    shard = 2 * lax.axis_index("x") + lax.axis_index("y")
    core = lax.axis_index("c")
    first = [n for n in BIG if n not in LATE]
    own = [wts[n].astype(BF16) for n in first]
    w = {n: wts[n] if n == "a_rel_bias" else wts[n][None] for n in SMALL}
    w.update(kernel_layout({n: lax.dynamic_update_index_in_dim(got, mine, shard, 0)
                            for n, got, mine in zip(first, allgather_shards(own), own)}))

    loss, grad_x, g, landed = local_step(x[0], loss_target[0], w,
                                         late=([wts[n].astype(BF16) for n in LATE], shard))

    landed.update(zip(first, scatter_partials([g[n] for n in first])))
    halves = [sum_slots(landed[n], "sum_slots") for n in BIG]
    grads = dict(zip(BIG, (lax.dynamic_update_slice_in_dim(got, mine, core * mine.shape[0], 0)
                           for got, mine in zip(join_halves(halves), halves))))

    small_sum, loss_sum = _unpack_small(allreduce_small(_pack_small(g, loss[0, 0])), wts)
    grads.update(small_sum)

    delta, new_m, new_v = {}, {}, {}
    for n in BIG:
        delta[n], new_m[n], new_v[n] = adamw(wts[n], grads[n], mom[n], var[n], "adamw")
    packed = adamw(_pack_small(wts), _pack_small(grads), _pack_small(mom), _pack_small(var), "adamw_small")
    for dst, p in zip((delta, new_m, new_v), packed):
        dst.update(_unpack_small(p, wts)[0])

    lead = lambda d: [d[n][None] for n in WEIGHTS]
    return (loss_sum, grad_x[None], *lead(grads), *lead(delta), *lead(new_m), *lead(new_v))
```

```python
import functools
import math

import numpy as np
import jax
import jax.numpy as jnp
from jax import lax
from jax.experimental import pallas as pl
from jax.experimental.pallas import tpu as pltpu

F32 = jnp.float32
BF16 = jnp.bfloat16
EPS = 1e-6
NEG = -1e30

D_MODEL = 1024
D_FF = 2816
N_SHARD = 4
FS = D_FF // N_SHARD
CHUNK = 64
A_LEFT = 8
A_MAX_REL = 128
HEADS = 8
HD = 64
ROPE = 32
PROJ_W = 2048
IN_COLS = 1952
B_SCALE = 96 ** -0.5
LANES = 128

ADAM_LR = 0.001
ADAM_B1 = 0.9
ADAM_B2 = 0.999
ADAM_EPS = 1e-08
ADAM_WD = 0.01
ADAM_STEP = 10

VMEM_LIMIT = 56 * 1024 * 1024

MESH = pl.DeviceIdType.MESH


def _dot(a, b):
    return lax.dot_general(a, b, (((1,), (0,)), ((), ())), preferred_element_type=F32)


def _dot_nt(a, b):
    return lax.dot_general(a, b, (((1,), (1,)), ((), ())), preferred_element_type=F32)


def _dot_tn(a, b):
    return lax.dot_general(a, b, (((0,), (0,)), ((), ())), preferred_element_type=F32)


def _params(sem):
    return pltpu.CompilerParams(dimension_semantics=sem, vmem_limit_bytes=VMEM_LIMIT)


def _rms(xv):
    r = lax.rsqrt(jnp.mean(xv * xv, axis=-1, keepdims=True) + EPS)
    return r, xv * r


def ffn_fwd(x, g, wg, wu, wd, name, gather=()):
    t, d = x.shape
    tm = 512
    ni = t // tm
    plan = GatherPlan(gather)
    n = plan.n

    def body(*refs):
        x_ref, g_ref, wg_ref, wu_ref, wd_ref = refs[:5]
        ins, o_ref, outs = refs[5:5 + n], refs[5 + n], refs[6 + n:6 + 2 * n]
        h_ref, acc_ref = refs[6 + 2 * n:8 + 2 * n]
        sems = refs[8 + 2 * n:]
        i, j = pl.program_id(0), pl.program_id(1)
        if n:
            pl.when((i == 0) & (j == 0))(lambda: plan.start(ins, outs, sems))
            pl.when((i == (3 * ni) // 4) & (j == 0))(lambda: plan.forward(ins, outs, sems))

        @pl.when(j == 0)
        def _():
            _, xn = _rms(x_ref[...])
            h_ref[...] = (xn * g_ref[...]).astype(BF16)
            acc_ref[...] = jnp.zeros_like(acc_ref)

        h = h_ref[...]
        gp = _dot_nt(h, wg_ref[0])
        up = _dot_nt(h, wu_ref[0])
        a = (gp * jax.nn.sigmoid(gp) * up).astype(BF16)
        acc_ref[...] += _dot(a, wd_ref[0])

        @pl.when(j == N_SHARD - 1)
        def _():
            o_ref[...] = x_ref[...] + 0.5 * acc_ref[...]

        if n:
            pl.when((i == ni - 1) & (j == N_SHARD - 1))(lambda: plan.finish(ins, outs, sems))

    res = pl.pallas_call(
        body, name=name, grid=(ni, N_SHARD),
        in_specs=[pl.BlockSpec((tm, d), lambda i, j: (i, 0)),
                  pl.BlockSpec((1, d), lambda i, j: (0, 0)),
                  pl.BlockSpec((1, FS, d), lambda i, j: (j, 0, 0)),
                  pl.BlockSpec((1, FS, d), lambda i, j: (j, 0, 0)),
                  pl.BlockSpec((1, FS, d), lambda i, j: (j, 0, 0))] + [ANY] * n,
        out_specs=[pl.BlockSpec((tm, d), lambda i, j: (i, 0))] + [ANY] * n,
        out_shape=[jax.ShapeDtypeStruct((t, d), F32)] + plan.out_shape,
        scratch_shapes=[pltpu.VMEM((tm, d), BF16), pltpu.VMEM((tm, d), F32)] + (plan.scratch if n else []),
        compiler_params=_params(("arbitrary", "arbitrary")),
    )(x, g, wg, wu, wd, *gather)
    return res if n else res[0]


def ffn_bwd(x, dout, g, wg, wu, wd, name, scatter=()):
    t, d = x.shape
    tm = 512
    ni = t // tm
    plan = ScatterPlan(scatter)
    m = plan.n

    def body(*refs):
        x_ref, do_ref, g_ref, wg_ref, wu_ref, wd_ref = refs[:6]
        ins, (dwg_out, dwu_out, dwd_out, dhp_ref), outs = refs[6:6 + m], refs[6 + m:10 + m], refs[10 + m:10 + 2 * m]
        dwg_ref, dwu_ref, dwd_ref = refs[10 + 2 * m:13 + 2 * m]
        sems = refs[13 + 2 * m:]
        i = pl.program_id(1)
        if m:
            pl.when((pl.program_id(0) == 0) & (i == 0))(lambda: plan.start(ins, outs, sems))
        _, xn = _rms(x_ref[...])
        h = (xn * g_ref[...]).astype(BF16)
        dz = (0.5 * do_ref[...]).astype(BF16)
        wgv, wuv, wdv = wg_ref[0], wu_ref[0], wd_ref[0]
        gp = _dot_nt(h, wgv)
        up = _dot_nt(h, wuv)
        s = jax.nn.sigmoid(gp)
        sg = gp * s
        a = (sg * up).astype(BF16)
        da = _dot_nt(dz, wdv)
        dup = (da * sg).astype(BF16)
        dgp = (da * up * (s * (1.0 + gp * (1.0 - s)))).astype(BF16)

        @pl.when(i == 0)
        def _():
            dwg_ref[...] = jnp.zeros_like(dwg_ref)
            dwu_ref[...] = jnp.zeros_like(dwu_ref)
            dwd_ref[...] = jnp.zeros_like(dwd_ref)

        dwd_ref[...] += _dot_tn(a, dz)
        dwg_ref[...] += _dot_tn(dgp, h)
        dwu_ref[...] += _dot_tn(dup, h)
        dhp_ref[0] = _dot(dgp, wgv) + _dot(dup, wuv)

        @pl.when(i == ni - 1)
        def _():
            dwg_out[0] = dwg_ref[...].astype(BF16)
            dwu_out[0] = dwu_ref[...].astype(BF16)
            dwd_out[0] = dwd_ref[...].astype(BF16)

        if m:
            pl.when((pl.program_id(0) == N_SHARD - 1) & (i == ni - 1))(lambda: plan.finish(ins, outs, sems))

    return pl.pallas_call(
        body, name=name, grid=(N_SHARD, ni),
        in_specs=[pl.BlockSpec((tm, d), lambda j, i: (i, 0)),
                  pl.BlockSpec((tm, d), lambda j, i: (i, 0)),
                  pl.BlockSpec((1, d), lambda j, i: (0, 0)),
                  pl.BlockSpec((1, FS, d), lambda j, i: (j, 0, 0)),
                  pl.BlockSpec((1, FS, d), lambda j, i: (j, 0, 0)),
                  pl.BlockSpec((1, FS, d), lambda j, i: (j, 0, 0))] + [ANY] * m,
        out_specs=[pl.BlockSpec((1, FS, d), lambda j, i: (j, 0, 0)),
                   pl.BlockSpec((1, FS, d), lambda j, i: (j, 0, 0)),
                   pl.BlockSpec((1, FS, d), lambda j, i: (j, 0, 0)),
                   pl.BlockSpec((1, tm, d), lambda j, i: (j, i, 0))] + [ANY] * m,
        out_shape=[jax.ShapeDtypeStruct((N_SHARD, FS, d), BF16),
                   jax.ShapeDtypeStruct((N_SHARD, FS, d), BF16),
                   jax.ShapeDtypeStruct((N_SHARD, FS, d), BF16),
                   jax.ShapeDtypeStruct((N_SHARD, t, d), F32)] + plan.out_shape,
        scratch_shapes=[pltpu.VMEM((FS, d), F32), pltpu.VMEM((FS, d), F32), pltpu.VMEM((FS, d), F32)]
        + (plan.scratch if m else []),
        compiler_params=_params(("arbitrary", "arbitrary")),
    )(x, dout, g, wg, wu, wd, *scatter)


def norm_bwd(x, g, dhp, dres, name):
    t, d = x.shape
    p = dhp.shape[0]
    tm = 512

    def body(x_ref, g_ref, dhp_ref, dres_ref, dx_ref, dg_ref):
        i = pl.program_id(0)
        r, xn = _rms(x_ref[...])
        dh = dhp_ref[0]
        for q in range(1, p):
            dh = dh + dhp_ref[q]
        dhg = dh * g_ref[...]
        dx_ref[...] = dres_ref[...] + r * (dhg - xn * jnp.mean(dhg * xn, axis=-1, keepdims=True))

        @pl.when(i == 0)
        def _():
            dg_ref[...] = jnp.zeros_like(dg_ref)

        dg_ref[...] += jnp.sum(dh * xn, axis=0, keepdims=True)

    return pl.pallas_call(
        body, name=name, grid=(t // tm,),
        in_specs=[pl.BlockSpec((tm, d), lambda i: (i, 0)),
                  pl.BlockSpec((1, d), lambda i: (0, 0)),
                  pl.BlockSpec((p, tm, d), lambda i: (0, i, 0)),
                  pl.BlockSpec((tm, d), lambda i: (i, 0))],
        out_specs=[pl.BlockSpec((tm, d), lambda i: (i, 0)),
                   pl.BlockSpec((1, d), lambda i: (0, 0))],
        out_shape=[jax.ShapeDtypeStruct((t, d), F32), jax.ShapeDtypeStruct((1, d), F32)],
        compiler_params=_params(("arbitrary",)),
    )(x, g, dhp, dres)


def final_loss(x, g, target):
    t, d = x.shape
    tm = 512

    def body(x_ref, g_ref, t_ref, dx_ref, dg_ref, loss_ref):
        i = pl.program_id(0)
        r, xn = _rms(x_ref[...])
        gv = g_ref[...]
        e = xn * gv - t_ref[...]
        dy = e * (1.0 / d)
        dhg = dy * gv
        dx_ref[...] = r * (dhg - xn * jnp.mean(dhg * xn, axis=-1, keepdims=True))

        @pl.when(i == 0)
        def _():
            dg_ref[...] = jnp.zeros_like(dg_ref)
            loss_ref[...] = jnp.zeros_like(loss_ref)

        dg_ref[...] += jnp.sum(dy * xn, axis=0, keepdims=True)
        part = jnp.sum(jnp.sum(e * e, axis=-1, keepdims=True), axis=0, keepdims=True) * (0.5 / d)
        loss_ref[...] += jnp.broadcast_to(part, loss_ref.shape)

    return pl.pallas_call(
        body, name="final_loss", grid=(t // tm,),
        in_specs=[pl.BlockSpec((tm, d), lambda i: (i, 0)),
                  pl.BlockSpec((1, d), lambda i: (0, 0)),
                  pl.BlockSpec((tm, d), lambda i: (i, 0))],
        out_specs=[pl.BlockSpec((tm, d), lambda i: (i, 0)),
                   pl.BlockSpec((1, d), lambda i: (0, 0)),
                   pl.BlockSpec((1, LANES), lambda i: (0, 0))],
        out_shape=[jax.ShapeDtypeStruct((t, d), F32), jax.ShapeDtypeStruct((1, d), F32),
                   jax.ShapeDtypeStruct((1, LANES), F32)],
        compiler_params=_params(("arbitrary",)),
    )(x, g, target)


def mix_proj(x, g, w):
    t, d = x.shape
    n = w.shape[0]
    tm = 512

    def body(x_ref, g_ref, w_ref, h_ref, p_ref):
        _, xn = _rms(x_ref[...])
        h = (xn * g_ref[...]).astype(BF16)
        h_ref[...] = h
        p_ref[...] = _dot_nt(h, w_ref[...])

    return pl.pallas_call(
        body, name="mix_proj", grid=(t // tm,),
        in_specs=[pl.BlockSpec((tm, d), lambda i: (i, 0)),
                  pl.BlockSpec((1, d), lambda i: (0, 0)),
                  pl.BlockSpec((n, d), lambda i: (0, 0))],
        out_specs=[pl.BlockSpec((tm, d), lambda i: (i, 0)),
                   pl.BlockSpec((tm, n), lambda i: (i, 0))],
        out_shape=[jax.ShapeDtypeStruct((t, d), BF16), jax.ShapeDtypeStruct((t, n), F32)],
        compiler_params=_params(("parallel",)),
    )(x, g, w)


def matmul(a, b, mode, name, tm, tn, tk, out_dtype=F32):
    if mode == "nn":
        (m, k), n = a.shape, b.shape[1]
        a_spec = pl.BlockSpec((tm, tk), lambda i, j, q: (i, q))
        b_spec = pl.BlockSpec((tk, tn), lambda i, j, q: (q, j))
        dot = _dot
    elif mode == "nt":
        (m, k), n = a.shape, b.shape[0]
        a_spec = pl.BlockSpec((tm, tk), lambda i, j, q: (i, q))
        b_spec = pl.BlockSpec((tn, tk), lambda i, j, q: (j, q))
        dot = _dot_nt
    else:
        (k, m), n = a.shape, b.shape[1]
        a_spec = pl.BlockSpec((tk, tm), lambda i, j, q: (q, i))
        b_spec = pl.BlockSpec((tk, tn), lambda i, j, q: (q, j))
        dot = _dot_tn
    assert m % tm == 0 and n % tn == 0 and k % tk == 0, (m, n, k, tm, tn, tk)
    nk = k // tk

    def body(a_ref, b_ref, o_ref, acc_ref):
        q = pl.program_id(2)

        @pl.when(q == 0)
        def _():
            acc_ref[...] = jnp.zeros_like(acc_ref)

        acc_ref[...] += dot(a_ref[...].astype(BF16), b_ref[...].astype(BF16))

        @pl.when(q == nk - 1)
        def _():
            o_ref[...] = acc_ref[...].astype(out_dtype)

    return pl.pallas_call(
        body, name=name, grid=(m // tm, n // tn, nk),
        in_specs=[a_spec, b_spec],
        out_specs=pl.BlockSpec((tm, tn), lambda i, j, q: (i, j)),
        out_shape=jax.ShapeDtypeStruct((m, n), out_dtype),
        scratch_shapes=[pltpu.VMEM((tm, tn), F32)],
        compiler_params=_params(("parallel", "parallel", "arbitrary")),
    )(a, b)


def out_proj(x, oa, ob_t, w):
    t, d = x.shape
    half = oa.shape[1]
    tm = 512

    def body(x_ref, oa_ref, obt_ref, w_ref, o_ref):
        o_ref[...] = (x_ref[...] + _dot(oa_ref[...], w_ref[0:half, :])
                      + _dot_tn(obt_ref[...], w_ref[half:2 * half, :]))

    return pl.pallas_call(
        body, name="out_proj", grid=(t // tm,),
        in_specs=[pl.BlockSpec((tm, d), lambda i: (i, 0)),
                  pl.BlockSpec((tm, half), lambda i: (i, 0)),
                  pl.BlockSpec((half, tm), lambda i: (0, i)),
                  pl.BlockSpec((2 * half, d), lambda i: (0, 0))],
        out_specs=pl.BlockSpec((tm, d), lambda i: (i, 0)),
        out_shape=jax.ShapeDtypeStruct((t, d), F32),
        compiler_params=_params(("parallel",)),
    )(x, oa, ob_t, w)


def out_proj_bwd(dx, w):
    t, d = dx.shape
    half = w.shape[0] // 2
    tm = 512

    def body(dx_ref, w_ref, da_ref, db_ref, dbt_ref):
        dxb = dx_ref[...].astype(BF16)
        da_ref[...] = _dot_nt(dxb, w_ref[0:half, :]).astype(BF16)
        db_ref[...] = _dot_nt(dxb, w_ref[half:2 * half, :]).astype(BF16)
        dbt_ref[...] = _dot_nt(w_ref[half:2 * half, :], dxb).astype(BF16)

    return pl.pallas_call(
        body, name="out_proj_bwd", grid=(t // tm,),
        in_specs=[pl.BlockSpec((tm, d), lambda i: (i, 0)),
                  pl.BlockSpec((2 * half, d), lambda i: (0, 0))],
        out_specs=[pl.BlockSpec((tm, half), lambda i: (i, 0)),
                   pl.BlockSpec((tm, half), lambda i: (i, 0)),
                   pl.BlockSpec((half, tm), lambda i: (0, i))],
        out_shape=[jax.ShapeDtypeStruct((t, half), BF16), jax.ShapeDtypeStruct((t, half), BF16),
                   jax.ShapeDtypeStruct((half, t), BF16)],
        compiler_params=_params(("parallel",)),
    )(dx, w)


def _lane(shape):
    return lax.broadcasted_iota(jnp.int32, shape, 1)


def _seg_sum(z, mask):
    return jnp.sum(jnp.where(mask, z, 0.0), axis=-1, keepdims=True)


def _pair_norm(x):
    lo = _lane(x.shape) < HD
    x2 = x * x
    r = jnp.where(lo, lax.rsqrt(_seg_sum(x2, lo) * (1.0 / HD) + EPS),
                  lax.rsqrt(_seg_sum(x2, ~lo) * (1.0 / HD) + EPS))
    return lo, r, x * r


def _pair_norm_bwd(lo, r, xn, dyg):
    z = dyg * xn
    mean = jnp.where(lo, _seg_sum(z, lo), _seg_sum(z, ~lo)) * (1.0 / HD)
    return r * (dyg - xn * mean)


A_TM = 256


def prep1_fwd(proj, gq, gk, gcq, gckv):
    t = proj.shape[0]
    tm = A_TM

    def body(p_ref, gq_ref, gk_ref, gcq_ref, gckv_ref, qa_ref, ka_ref, va_ref, cq_ref, ckv_ref):
        for p in range(4):
            sl = slice(LANES * p, LANES * (p + 1))
            _, _, xn = _pair_norm(p_ref[:, sl])
            qa_ref[:, sl] = (xn * gq_ref[:, sl] * 0.125).astype(BF16)
            _, _, xn = _pair_norm(p_ref[:, 512 + LANES * p:512 + LANES * (p + 1)])
            ka_ref[:, sl] = (xn * gk_ref[:, sl]).astype(BF16)
        va_ref[...] = p_ref[:, 1024:1536].astype(BF16)
        _, xn = _rms(p_ref[:, 1536:1792])
        cq_ref[...] = (xn * gcq_ref[...]).astype(BF16)
        _, xn = _rms(p_ref[:, 1792:1920])
        ckv_ref[...] = (xn * gckv_ref[...]).astype(BF16)

    row = lambda w: pl.BlockSpec((tm, w), lambda i: (i, 0))
    vec = lambda w: pl.BlockSpec((1, w), lambda i: (0, 0))
    return pl.pallas_call(
        body, name="prep1_fwd", grid=(t // tm,),
        in_specs=[row(PROJ_W), vec(512), vec(512), vec(256), vec(128)],
        out_specs=[row(512), row(512), row(512), row(256), row(128)],
        out_shape=[jax.ShapeDtypeStruct((t, w), BF16) for w in (512, 512, 512, 256, 128)],
        compiler_params=_params(("parallel",)),
    )(proj, gq, gk, gcq, gckv)


def prep1_bwd(proj, dqa, dkp, dvp, dcq, dckv, dkr, gq, gk, gcq, gckv):
    t = proj.shape[0]
    tm = A_TM
    nb = t // tm

    def body(p_ref, dqa_ref, dk0_ref, dk1_ref, dk2_ref, dv0_ref, dv1_ref, dv2_ref, dcq_ref, dckv_ref, dkr_ref,
             gq_ref, gk_ref, gcq_ref, gckv_ref, dp_ref, dgq_ref, dgk_ref, dgcq_ref, dgckv_ref):
        i = pl.program_id(0)

        @pl.when(i == 0)
        def _():
            dgq_ref[...] = jnp.zeros_like(dgq_ref)
            dgk_ref[...] = jnp.zeros_like(dgk_ref)
            dgcq_ref[...] = jnp.zeros_like(dgcq_ref)
            dgckv_ref[...] = jnp.zeros_like(dgckv_ref)

        has1 = (i + 1 < nb).astype(F32)
        has2 = (i + 2 < nb).astype(F32)
        for p in range(4):
            sl = slice(LANES * p, LANES * (p + 1))
            lo, r, xn = _pair_norm(p_ref[:, sl])
            dy = dqa_ref[:, sl] * 0.125
            dp_ref[:, sl] = _pair_norm_bwd(lo, r, xn, dy * gq_ref[:, sl]).astype(BF16)
            dgq_ref[:, sl] += jnp.sum(dy * xn, axis=0, keepdims=True)
            ks = slice(512 + LANES * p, 512 + LANES * (p + 1))
            lo, r, xn = _pair_norm(p_ref[:, ks])
            dy = dk0_ref[0, :, sl] + has1 * dk1_ref[0, :, sl] + has2 * dk2_ref[0, :, sl]
            dp_ref[:, ks] = _pair_norm_bwd(lo, r, xn, dy * gk_ref[:, sl]).astype(BF16)
            dgk_ref[:, sl] += jnp.sum(dy * xn, axis=0, keepdims=True)
        dp_ref[:, 1024:1536] = (dv0_ref[0] + has1 * dv1_ref[0] + has2 * dv2_ref[0]).astype(BF16)
        for (a, b, d_ref, g_ref, dg_ref) in ((1536, 1792, dcq_ref, gcq_ref, dgcq_ref),
                                             (1792, 1920, dckv_ref, gckv_ref, dgckv_ref)):
            r, xn = _rms(p_ref[:, a:b])
            dy = d_ref[...]
            dyg = dy * g_ref[...]
            dp_ref[:, a:b] = (r * (dyg - xn * jnp.mean(dyg * xn, axis=-1, keepdims=True))).astype(BF16)
            dg_ref[...] += jnp.sum(dy * xn, axis=0, keepdims=True)
        dp_ref[:, 1920:2048] = dkr_ref[...].astype(BF16)

    row = lambda w: pl.BlockSpec((tm, w), lambda i: (i, 0))
    vec = lambda w: pl.BlockSpec((1, w), lambda i: (0, 0))
    part = lambda s: pl.BlockSpec((1, tm, 512), lambda i: (s, jnp.minimum(i + s, nb - 1), 0))
    return pl.pallas_call(
        body, name="prep1_bwd", grid=(nb,),
        in_specs=[row(PROJ_W), row(512), part(0), part(1), part(2), part(0), part(1), part(2),
                  row(256), row(128), row(128), vec(512), vec(512), vec(256), vec(128)],
        out_specs=[row(PROJ_W), vec(512), vec(512), vec(256), vec(128)],
        out_shape=[jax.ShapeDtypeStruct((t, PROJ_W), BF16)] + [jax.ShapeDtypeStruct((1, w), F32) for w in (512, 512, 256, 128)],
        compiler_params=_params(("arbitrary",)),
    )(proj, dqa, dkp, dkp, dkp, dvp, dvp, dvp, dcq, dckv, dkr, gq, gk, gcq, gckv)


def _roll(x, shift):
    return pltpu.roll(x, shift % LANES, 1)


def _rope(y, c, s1, s2):
    return y * c + _roll(y, -16) * s1 + _roll(y, 16) * s2


def _rope_bwd(d, c, s1, s2):
    return d * c + _roll(d * s1, 16) + _roll(d * s2, -16)


def _q_head_stats(x):
    lane = _lane(x.shape)
    mn = lane < HD
    mr = (lane >= HD) & (lane < HD + ROPE)
    x2 = x * x
    r = jnp.where(mn, lax.rsqrt(_seg_sum(x2, mn) * (1.0 / HD) + EPS),
                  lax.rsqrt(_seg_sum(x2, mr) * (1.0 / ROPE) + EPS))
    return mn, mr, r, x * r


def _kr_stats(x):
    r = lax.rsqrt(jnp.sum(x * x, axis=-1, keepdims=True) * (1.0 / ROPE) + EPS)
    return r, x * r


def prep2_fwd(qlat, kv, proj, gq, gk, gkr, tabs):
    t = qlat.shape[0]
    tm = A_TM

    def body(q_ref, kv_ref, kr_ref, gq_ref, gk_ref, gkr_ref, tab_ref, qf_ref, kf_ref, vp_ref):
        _, xn = _kr_stats(kr_ref[...])
        kpe = _roll(_rope(xn * gkr_ref[...], tab_ref[3], tab_ref[4], tab_ref[5]), 64)
        for h in range(HEADS):
            sl = slice(LANES * h, LANES * (h + 1))
            _, _, _, xn = _q_head_stats(q_ref[:, sl])
            qf_ref[:, sl] = _rope(xn * gq_ref[...], tab_ref[0], tab_ref[1], tab_ref[2]).astype(BF16)
            x = kv_ref[:, sl]
            lo = _lane(x.shape) < HD
            xk = jnp.where(lo, x, 0.0)
            rk = lax.rsqrt(jnp.sum(xk * xk, axis=-1, keepdims=True) * (1.0 / HD) + EPS)
            kf_ref[:, sl] = (xk * rk * gk_ref[...] + kpe).astype(BF16)
            if h % 2 == 0:
                v_even = _roll(x, 64)
            else:
                vp_ref[:, LANES * (h // 2):LANES * (h // 2 + 1)] = jnp.where(lo, v_even, x).astype(BF16)

    row = lambda w: pl.BlockSpec((tm, w), lambda i: (i, 0))
    vec = lambda w: pl.BlockSpec((1, w), lambda i: (0, 0))
    return pl.pallas_call(
        body, name="prep2_fwd", grid=(t // tm,),
        in_specs=[row(1024), row(1024), pl.BlockSpec((tm, LANES), lambda i: (i, 15)), vec(128), vec(128), vec(128),
                  pl.BlockSpec((6, tm, LANES), lambda i: (0, i, 0))],
        out_specs=[row(1024), row(1024), row(512)],
        out_shape=[jax.ShapeDtypeStruct((t, 1024), BF16), jax.ShapeDtypeStruct((t, 1024), BF16),
                   jax.ShapeDtypeStruct((t, 512), BF16)],
        compiler_params=_params(("parallel",)),
    )(qlat, kv, proj, gq, gk, gkr, tabs)


def prep2_bwd(qlat, kv, proj, dqf, dkf, dvp, gq, gk, gkr, tabs):
    t = qlat.shape[0]
    tm = A_TM

    def body(q_ref, kv_ref, kr_ref, dqf_ref, dkf_ref, dvp_ref, gq_ref, gk_ref, gkr_ref, tab_ref,
             dq_ref, dkv_ref, dkr_ref, dgq_ref, dgk_ref, dgkr_ref):
        i = pl.program_id(0)

        @pl.when(i == 0)
        def _():
            dgq_ref[...] = jnp.zeros_like(dgq_ref)
            dgk_ref[...] = jnp.zeros_like(dgk_ref)
            dgkr_ref[...] = jnp.zeros_like(dgkr_ref)

        dgq = jnp.zeros((1, LANES), F32)
        dgk = jnp.zeros((1, LANES), F32)
        dkpe = jnp.zeros((tm, LANES), F32)
        for h in range(HEADS):
            sl = slice(LANES * h, LANES * (h + 1))
            mn, mr, r, xn = _q_head_stats(q_ref[:, sl])
            dy = _rope_bwd(dqf_ref[sl, :].T, tab_ref[0], tab_ref[1], tab_ref[2])
            dyg = dy * gq_ref[...]
            z = dyg * xn
            mean = jnp.where(mn, _seg_sum(z, mn) * (1.0 / HD), _seg_sum(z, mr) * (1.0 / ROPE))
            dq_ref[:, sl] = (r * (dyg - xn * mean)).astype(BF16)
            dgq = dgq + jnp.sum(dy * xn, axis=0, keepdims=True)

            x = kv_ref[:, sl]
            dk = dkf_ref[:, sl]
            xk = jnp.where(mn, x, 0.0)
            rk = lax.rsqrt(jnp.sum(xk * xk, axis=-1, keepdims=True) * (1.0 / HD) + EPS)
            xkn = xk * rk
            dyk = jnp.where(mn, dk, 0.0)
            dykg = dyk * gk_ref[...]
            dxk = rk * (dykg - xkn * (jnp.sum(dykg * xkn, axis=-1, keepdims=True) * (1.0 / HD)))
            dgk = dgk + jnp.sum(dyk * xkn, axis=0, keepdims=True)
            dkpe = dkpe + jnp.where(mr, dk, 0.0)
            dvpair = dvp_ref[:, LANES * (h // 2):LANES * (h // 2 + 1)]
            dv = _roll(dvpair, 64) if h % 2 == 0 else dvpair
            dkv_ref[:, sl] = jnp.where(mn, dxk, dv).astype(BF16)

        r, xn = _kr_stats(kr_ref[...])
        dy = _rope_bwd(_roll(dkpe, 64), tab_ref[3], tab_ref[4], tab_ref[5])
        dyg = dy * gkr_ref[...]
        dkr_ref[...] = r * (dyg - xn * (jnp.sum(dyg * xn, axis=-1, keepdims=True) * (1.0 / ROPE)))
        dgq_ref[...] += dgq
        dgk_ref[...] += dgk
        dgkr_ref[...] += jnp.sum(dy * xn, axis=0, keepdims=True)

    row = lambda w: pl.BlockSpec((tm, w), lambda i: (i, 0))
    vec = lambda w: pl.BlockSpec((1, w), lambda i: (0, 0))
    return pl.pallas_call(
        body, name="prep2_bwd", grid=(t // tm,),
        in_specs=[row(1024), row(1024), pl.BlockSpec((tm, LANES), lambda i: (i, 15)),
                  pl.BlockSpec((1024, tm), lambda i: (0, i)), row(1024), row(512),
                  vec(128), vec(128), vec(128), pl.BlockSpec((6, tm, LANES), lambda i: (0, i, 0))],
        out_specs=[row(1024), row(1024), row(128), vec(128), vec(128), vec(128)],
        out_shape=[jax.ShapeDtypeStruct((t, 1024), BF16), jax.ShapeDtypeStruct((t, 1024), BF16),
                   jax.ShapeDtypeStruct((t, LANES), F32)] + [jax.ShapeDtypeStruct((1, LANES), F32)] * 3,
        compiler_params=_params(("arbitrary",)),
    )(qlat, kv, proj, dqf, dkf, dvp, gq, gk, gkr, tabs)


A_TQ = 256
A_WIN = 3 * A_TQ


def _a_specs(t):
    nb = t // A_TQ
    blk = lambda s: pl.BlockSpec((A_TQ, 512), lambda i: (jnp.maximum(i - s, 0), 0))
    return nb, blk


def _a_probs(qb, q_ref, kc, b_ref, head, sl, lo):
    hm = lo if head % 2 == 0 else ~lo
    qm = jnp.where(hm, q_ref[:, sl], jnp.zeros((), BF16))
    s = _dot_nt(qm, kc) + b_ref[head]
    col = lax.broadcasted_iota(jnp.int32, s.shape, 1)
    s = jnp.where(col >= 2 * A_TQ - A_TQ * qb, s, NEG)
    e = jnp.exp(s - jnp.max(s, axis=-1, keepdims=True))
    pr = e * (1.0 / jnp.sum(e, axis=-1, keepdims=True))
    return hm, qm, pr


def attn_a_fwd(qa, ka, va, bias):
    t = qa.shape[0]
    nb, blk = _a_specs(t)

    def body(q_ref, k2_ref, k1_ref, k0_ref, v2_ref, v1_ref, v0_ref, b_ref, o_ref):
        qb = pl.program_id(0)
        lo = _lane((A_TQ, LANES)) < HD
        for p in range(4):
            sl = slice(LANES * p, LANES * (p + 1))
            kc = jnp.concatenate([k2_ref[:, sl], k1_ref[:, sl], k0_ref[:, sl]], axis=0)
            vc = jnp.concatenate([v2_ref[:, sl], v1_ref[:, sl], v0_ref[:, sl]], axis=0)
            outs = []
            for h2 in range(2):
                _, _, pr = _a_probs(qb, q_ref, kc, b_ref, 2 * p + h2, sl, lo)
                outs.append(_dot(pr.astype(BF16), vc))
            o_ref[:, sl] = jnp.where(lo, outs[0], outs[1]).astype(BF16)

    return pl.pallas_call(
        body, name="attn_a_fwd", grid=(nb,),
        in_specs=[blk(0), blk(2), blk(1), blk(0), blk(2), blk(1), blk(0),
                  pl.BlockSpec((HEADS, A_TQ, A_WIN), lambda i: (0, 0, 0))],
        out_specs=pl.BlockSpec((A_TQ, 512), lambda i: (i, 0)),
        out_shape=jax.ShapeDtypeStruct((t, 512), BF16),
        compiler_params=_params(("parallel",)),
    )(qa, ka, ka, ka, va, va, va, bias)


def attn_a_bwd(qa, ka, va, bias, do):
    t = qa.shape[0]
    nb, blk = _a_specs(t)

    def body(q_ref, k2_ref, k1_ref, k0_ref, v2_ref, v1_ref, v0_ref, b_ref, do_ref, dq_ref, dk_ref, dv_ref, db_ref):
        qb = pl.program_id(0)

        @pl.when(qb == 0)
        def _():
            db_ref[...] = jnp.zeros_like(db_ref)

        lo = _lane((A_TQ, LANES)) < HD
        for p in range(4):
            sl = slice(LANES * p, LANES * (p + 1))
            kc = jnp.concatenate([k2_ref[:, sl], k1_ref[:, sl], k0_ref[:, sl]], axis=0)
            vc = jnp.concatenate([v2_ref[:, sl], v1_ref[:, sl], v0_ref[:, sl]], axis=0)
            dqs = []
            dkc = jnp.zeros((A_WIN, LANES), F32)
            dvc = jnp.zeros((A_WIN, LANES), F32)
            for h2 in range(2):
                head = 2 * p + h2
                hm, qm, pr = _a_probs(qb, q_ref, kc, b_ref, head, sl, lo)
                dom = jnp.where(hm, do_ref[:, sl], jnp.zeros((), BF16))
                dp = _dot_nt(dom, vc)
                ds = pr * (dp - jnp.sum(pr * dp, axis=-1, keepdims=True))
                db_ref[head] += ds
                dsb = ds.astype(BF16)
                dqs.append(_dot(dsb, kc))
                dkc = dkc + _dot_tn(dsb, qm)
                dvc = dvc + _dot_tn(pr.astype(BF16), dom)
            dq_ref[:, sl] = jnp.where(lo, dqs[0], dqs[1])
            for s in range(3):
                rows = slice(A_TQ * (2 - s), A_TQ * (3 - s))
                dk_ref[s, :, sl] = dkc[rows]
                dv_ref[s, :, sl] = dvc[rows]

    share = pl.BlockSpec((3, A_TQ, 512), lambda i: (0, i, 0))
    return pl.pallas_call(
        body, name="attn_a_bwd", grid=(nb,),
        in_specs=[blk(0), blk(2), blk(1), blk(0), blk(2), blk(1), blk(0),
                  pl.BlockSpec((HEADS, A_TQ, A_WIN), lambda i: (0, 0, 0)), blk(0)],
        out_specs=[pl.BlockSpec((A_TQ, 512), lambda i: (i, 0)), share, share,
                   pl.BlockSpec((HEADS, A_TQ, A_WIN), lambda i: (0, 0, 0))],
        out_shape=[jax.ShapeDtypeStruct((t, 512), F32), jax.ShapeDtypeStruct((3, t, 512), F32),
                   jax.ShapeDtypeStruct((3, t, 512), F32), jax.ShapeDtypeStruct((HEADS, A_TQ, A_WIN), F32)],
        compiler_params=_params(("arbitrary",)),
    )(qa, ka, ka, ka, va, va, va, bias, do)


B_T = 1024


B_SCALE2 = B_SCALE * 1.4426950408889634


def _tri_tables(n, by_query):
    pairs = [(i, j) for i in range(n) for j in range(i + 1)] if by_query else [(i, j) for j in range(n) for i in range(j, n)]
    return (np.asarray([p[0] for p in pairs], np.int32), np.asarray([p[1] for p in pairs], np.int32))


def _b_mask_t(s):
    kc = lax.broadcasted_iota(jnp.int32, s.shape, 0) // CHUNK
    qc = lax.broadcasted_iota(jnp.int32, s.shape, 1) // CHUNK
    return jnp.where(kc <= qc, s, NEG)


def attn_b_fwd(qf, kf, vp):
    t = qf.shape[0]
    n = t // B_T
    qtab, ktab = _tri_tables(n, by_query=True)

    def body(qt_ref, kt_ref, q_ref, k_ref, v_ref, o_ref, lse_ref, m_s, l_s, acc_s):
        qb, kb = qt_ref[pl.program_id(1)], kt_ref[pl.program_id(1)]

        @pl.when(kb == 0)
        def _():
            m_s[...] = jnp.full_like(m_s, NEG)
            l_s[...] = jnp.zeros_like(l_s)
            acc_s[...] = jnp.zeros_like(acc_s)

        def step(masked):
            v = v_ref[...]
            for h2 in range(2):
                sl = slice(LANES * h2, LANES * (h2 + 1))
                s = _dot_nt(k_ref[:, sl], q_ref[:, sl]) * B_SCALE2
                if masked:
                    s = _b_mask_t(s)
                m_prev = m_s[h2]
                m_new = jnp.maximum(m_prev, jnp.max(s, axis=0, keepdims=True))
                alpha = jnp.exp2(m_prev - m_new)
                pr = jnp.exp2(s - m_new)
                l_s[h2] = alpha * l_s[h2] + jnp.sum(pr, axis=0, keepdims=True)
                acc_s[h2] = alpha * acc_s[h2] + _dot_tn(v, pr.astype(BF16))
                m_s[h2] = m_new

        @pl.when(kb < qb)
        def _():
            step(False)

        @pl.when(kb == qb)
        def _():
            step(True)
            for h2 in range(2):
                l = l_s[h2]
                rows = slice(HD * h2, HD * (h2 + 1))
                o_ref[rows, :] = (acc_s[h2, rows, :] * (1.0 / l)).astype(BF16)
                lse_ref[0, h2:h2 + 1, :] = m_s[h2] + jnp.log2(l)

    grid_spec = pltpu.PrefetchScalarGridSpec(
        num_scalar_prefetch=2, grid=(4, len(qtab)),
        in_specs=[pl.BlockSpec((B_T, 256), lambda p, s, qt, kt: (qt[s], p)),
                  pl.BlockSpec((B_T, 256), lambda p, s, qt, kt: (kt[s], p)),
                  pl.BlockSpec((B_T, LANES), lambda p, s, qt, kt: (kt[s], p))],
        out_specs=[pl.BlockSpec((LANES, B_T), lambda p, s, qt, kt: (p, qt[s])),
                   pl.BlockSpec((1, 2, B_T), lambda p, s, qt, kt: (p, 0, qt[s]))],
        scratch_shapes=[pltpu.VMEM((2, 1, B_T), F32), pltpu.VMEM((2, 1, B_T), F32), pltpu.VMEM((2, LANES, B_T), F32)])
    return pl.pallas_call(
        body, name="attn_b_fwd", grid_spec=grid_spec,
        out_shape=[jax.ShapeDtypeStruct((512, t), BF16), jax.ShapeDtypeStruct((4, 2, t), F32)],
        compiler_params=_params(("parallel", "arbitrary")),
    )(jnp.asarray(qtab), jnp.asarray(ktab), qf, kf, vp)


def attn_b_bwd(qf, kf, vp, do, do_t, o_t, lse, scatter=()):
    t = qf.shape[0]
    n = t // B_T
    qtab, ktab = _tri_tables(n, by_query=False)
    plan = ScatterPlan(scatter)
    m = plan.n
    last = len(qtab) - 1

    def body(*refs):
        qt_ref, kt_ref, q_ref, k_ref, v_ref, do_ref, dot_ref, ot_ref, lse_ref = refs[:9]
        ins, (dq_ref, dk_ref, dv_ref), outs = refs[9:9 + m], refs[9 + m:12 + m], refs[12 + m:12 + 2 * m]
        sems = refs[12 + 2 * m:]
        qb, kb = qt_ref[pl.program_id(1)], kt_ref[pl.program_id(1)]
        if m:
            pl.when((pl.program_id(0) == 0) & (pl.program_id(1) == 0))(lambda: plan.start(ins, outs, sems))

        @pl.when(pl.program_id(1) == 0)
        def _():
            dq_ref[...] = jnp.zeros_like(dq_ref)

        @pl.when(qb == kb)
        def _():
            dk_ref[...] = jnp.zeros_like(dk_ref)
            dv_ref[...] = jnp.zeros_like(dv_ref)

        def step(masked):
            cols = pl.ds(pl.multiple_of(qb * B_T, B_T), B_T)
            v = v_ref[...]
            dov = do_ref[...]
            prod = dot_ref[...].astype(F32) * ot_ref[...].astype(F32)
            lo = _lane((B_T, LANES)) < HD
            for h2 in range(2):
                sl = slice(LANES * h2, LANES * (h2 + 1))
                hm = lo if h2 == 0 else ~lo
                q = q_ref[:, sl]
                k = k_ref[:, sl]
                dom = jnp.where(hm, dov, jnp.zeros((), BF16))
                delta = jnp.sum(prod[HD * h2:HD * (h2 + 1), :], axis=0, keepdims=True)
                s = _dot_nt(k, q) * B_SCALE2
                if masked:
                    s = _b_mask_t(s)
                pr = jnp.exp2(s - lse_ref[0, h2:h2 + 1, :])
                dp = _dot_nt(v, dom)
                ds = (pr * (dp - delta) * B_SCALE).astype(BF16)
                dk_ref[:, sl] += _dot(ds, q)
                dv_ref[...] += _dot(pr.astype(BF16), dom)
                dq_ref[sl, cols] += _dot_tn(k, ds)

        @pl.when(qb > kb)
        def _():
            step(False)

        @pl.when(qb == kb)
        def _():
            step(True)

        if m:
            pl.when((pl.program_id(0) == 3) & (pl.program_id(1) == last))(lambda: plan.finish(ins, outs, sems))

    qrow = lambda w: pl.BlockSpec((B_T, w), lambda p, s, qt, kt: (qt[s], p))
    qcol = pl.BlockSpec((LANES, B_T), lambda p, s, qt, kt: (p, qt[s]))
    krow = lambda w: pl.BlockSpec((B_T, w), lambda p, s, qt, kt: (kt[s], p))
    grid_spec = pltpu.PrefetchScalarGridSpec(
        num_scalar_prefetch=2, grid=(4, len(qtab)),
        in_specs=[qrow(256), krow(256), krow(LANES), qrow(LANES), qcol, qcol,
                  pl.BlockSpec((1, 2, B_T), lambda p, s, qt, kt: (p, 0, qt[s]))] + [ANY] * m,
        out_specs=[pl.BlockSpec((256, t), lambda p, s, qt, kt: (p, 0)), krow(256), krow(LANES)] + [ANY] * m,
        scratch_shapes=plan.scratch if m else [])
    return pl.pallas_call(
        body, name="attn_b_bwd", grid_spec=grid_spec,
        out_shape=[jax.ShapeDtypeStruct((1024, t), F32), jax.ShapeDtypeStruct((t, 1024), F32),
                   jax.ShapeDtypeStruct((t, 512), F32)] + plan.out_shape,
        compiler_params=_params(("arbitrary", "arbitrary")),
    )(jnp.asarray(qtab), jnp.asarray(ktab), qf, kf, vp, do, do_t, o_t, lse, *scatter)


_U_LEN = A_TQ + A_WIN - 1


def _band_mask():
    a = np.arange(A_TQ)[:, None] // CHUNK
    b = np.arange(A_WIN)[None, :] // CHUNK
    return (b >= a) & (b <= a + A_LEFT)


def bias_block(table):
    h = table.shape[0]
    n_lo = A_WIN - 1 - 2 * A_TQ - A_MAX_REL
    ext = jnp.concatenate([jnp.repeat(table[:, :1], n_lo, axis=1), table,
                           jnp.repeat(table[:, -1:], _U_LEN - n_lo - table.shape[1], axis=1)], axis=1)
    row = jnp.pad(ext[:, ::-1], ((0, 0), (0, 1)))
    flat = jnp.tile(row, (1, A_TQ))[:, :A_TQ * _U_LEN]
    skew = flat.reshape(h, A_TQ, _U_LEN)
    toep = skew[:, :, A_TQ - 1:A_TQ - 1 + A_WIN]
    return jnp.where(jnp.asarray(_band_mask())[None], toep, NEG)


def bias_block_grad(db):
    h = db.shape[0]
    n_lo = A_WIN - 1 - 2 * A_TQ - A_MAX_REL
    skew = jnp.pad(db, ((0, 0), (0, 0), (A_TQ - 1, 0)))
    flat = jnp.pad(skew.reshape(h, A_TQ * _U_LEN), ((0, 0), (0, A_TQ)))
    ext = jnp.sum(flat.reshape(h, A_TQ, _U_LEN + 1), axis=1)[:, :_U_LEN][:, ::-1]
    n_tab = 2 * A_MAX_REL + 1
    first = jnp.sum(ext[:, :n_lo + 1], axis=1, keepdims=True)
    last = jnp.sum(ext[:, n_lo + n_tab - 1:], axis=1, keepdims=True)
    return jnp.concatenate([first, ext[:, n_lo + 1:n_lo + n_tab - 1], last], axis=1)


def rope_tabs(t):
    inv = 1.0 / (10000.0 ** (jnp.arange(0, ROPE, 2, dtype=F32) / ROPE))
    ang = jnp.arange(t, dtype=F32)[:, None] * inv[None, :]
    cos, sin = jnp.cos(ang), jnp.sin(ang)
    z = lambda w: jnp.zeros((t, w), F32)
    ck = jnp.concatenate([cos, cos, z(96)], axis=1)
    s1k = jnp.concatenate([-sin, z(112)], axis=1)
    s2k = jnp.concatenate([z(16), sin, z(96)], axis=1)
    cq = jnp.concatenate([jnp.ones((t, HD), F32), cos, cos, z(32)], axis=1)
    s1q = jnp.concatenate([z(HD), -sin, z(48)], axis=1)
    s2q = jnp.concatenate([z(HD + 16), sin, z(32)], axis=1)
    return jnp.stack([cq, s1q, s2q, ck, s1k, s2k])


def _pad_lanes(v, width):
    return jnp.pad(v, ((0, 0), (0, width - v.shape[1])))


LATE = ("w_in", "b_w_uq", "b_w_ukv", "w_out", "ffn2_w_gate", "ffn2_w_up", "ffn2_w_down")
FFN2 = ("ffn2_w_gate", "ffn2_w_up", "ffn2_w_down")


def kernel_layout(gathered):
    w = {n: v for n, v in gathered.items() if n.startswith("ffn")}
    if "w_in" in gathered:
        w["w_in"] = jnp.pad(gathered["w_in"].reshape(IN_COLS, D_MODEL), ((0, PROJ_W - IN_COLS), (0, 0)))
        uq = gathered["b_w_uq"].reshape(HEADS, HD + ROPE, 256)
        w["b_w_uq"] = jnp.pad(uq, ((0, 0), (0, LANES - HD - ROPE), (0, 0))).reshape(HEADS * LANES, 256)
        w["b_w_ukv"] = _shards_to_cols(gathered["b_w_ukv"])
        w["w_out"] = gathered["w_out"].reshape(N_SHARD * gathered["w_out"].shape[1], D_MODEL)
    return w


def local_step(x, target, w, late=None):
    t = x.shape[0]
    gq = jnp.tile(w["a_q_norm"], (1, HEADS))
    gk = jnp.tile(w["a_k_norm"], (1, HEADS))
    gq128 = _pad_lanes(jnp.concatenate([w["b_q_nope_norm"], w["b_q_rope_norm"]], axis=1), LANES)
    gk128 = _pad_lanes(w["b_k_nope_norm"], LANES)
    gkr128 = _pad_lanes(w["b_k_rope_norm"], LANES)
    tabs = rope_tabs(t)
    bias = bias_block(w["a_rel_bias"])

    if late is None:
        x1 = ffn_fwd(x, w["ffn1_norm"], w["ffn1_w_gate"], w["ffn1_w_up"], w["ffn1_w_down"], "ffn_fwd")
    else:
        own, shard = late
        x1, *got = ffn_fwd(x, w["ffn1_norm"], w["ffn1_w_gate"], w["ffn1_w_up"], w["ffn1_w_down"], "ffn_fwd_gather",
                           gather=own)
        w = dict(w, **kernel_layout({n: lax.dynamic_update_index_in_dim(g_, o_, shard, 0)
                                     for n, g_, o_ in zip(LATE, got, own)}))
    h, proj = mix_proj(x1, w["mix_norm"], w["w_in"])
    qa, ka, va, cqn, ckvn = prep1_fwd(proj, gq, gk, w["b_q_lat_norm"], w["b_kv_lat_norm"])
    qlat = matmul(cqn, w["b_w_uq"], "nt", "uq_fwd", 512, 1024, 256)
    kv = matmul(ckvn, w["b_w_ukv"], "nn", "ukv_fwd", 512, 1024, 128)
    qf, kf, vp = prep2_fwd(qlat, kv, proj, gq128, gk128, gkr128, tabs)
    oa = attn_a_fwd(qa, ka, va, bias)
    ob_t, lse = attn_b_fwd(qf, kf, vp)
    x2 = out_proj(x1, oa, ob_t, w["w_out"])
    x3 = ffn_fwd(x2, w["ffn2_norm"], w["ffn2_w_gate"], w["ffn2_w_up"], w["ffn2_w_down"], "ffn_fwd")

    g = {}
    dx3, g["final_norm"], loss = final_loss(x3, w["final_norm"], target)
    g["ffn2_w_gate"], g["ffn2_w_up"], g["ffn2_w_down"], dhp = ffn_bwd(
        x2, dx3, w["ffn2_norm"], w["ffn2_w_gate"], w["ffn2_w_up"], w["ffn2_w_down"], "ffn_bwd")
    dx2, g["ffn2_norm"] = norm_bwd(x2, w["ffn2_norm"], dhp, dx3, "ffn_norm_bwd")
    d_oa, d_ob, d_ob_t = out_proj_bwd(dx2, w["w_out"])
    g["w_out"] = jnp.concatenate([matmul(oa, dx2, "tn", "w_out_a_bwd", 512, 1024, 512, BF16),
                                  matmul(ob_t, dx2, "nn", "w_out_b_bwd", 512, 1024, 512, BF16)], axis=0)
    early = [g[n] for n in FFN2] + [g["w_out"].reshape(N_SHARD, -1, D_MODEL)]
    dqf, dkf, dvp, *landed_early = attn_b_bwd(qf, kf, vp, d_ob, d_ob_t, ob_t, lse,
                                              scatter=() if late is None else early)
    dqa, dkp, dvpa, dbias = attn_a_bwd(qa, ka, va, bias, d_oa)
    dqlat, dkv, dkr, dgq128, dgk128, dgkr128 = prep2_bwd(qlat, kv, proj, dqf, dkf, dvp, gq128, gk128, gkr128, tabs)
    dcq = matmul(dqlat, w["b_w_uq"], "nn", "uq_bwd_x", 512, 256, 1024)
    g["b_w_uq"] = matmul(dqlat, cqn, "tn", "uq_bwd_w", 1024, 256, 512, BF16)
    dckv = matmul(dkv, w["b_w_ukv"], "nt", "ukv_bwd_x", 512, 128, 1024)
    g["b_w_ukv"] = matmul(ckvn, dkv, "tn", "ukv_bwd_w", 128, 1024, 512, BF16)
    dproj, dgq, dgk, g["b_q_lat_norm"], g["b_kv_lat_norm"] = prep1_bwd(
        proj, dqa, dkp, dvpa, dcq, dckv, dkr, gq, gk, w["b_q_lat_norm"], w["b_kv_lat_norm"])
    dh = matmul(dproj, w["w_in"], "nn", "w_in_bwd_x", 512, 1024, PROJ_W)
    g["w_in"] = matmul(dproj, h, "tn", "w_in_bwd_w", 1024, 1024, 512, BF16)
    dx1, g["mix_norm"] = norm_bwd(x1, w["mix_norm"], dh[None], dx2, "mix_norm_bwd")
    mid = [g["w_in"][:IN_COLS].reshape(N_SHARD, IN_COLS // N_SHARD, D_MODEL),
           g["b_w_uq"].reshape(HEADS, LANES, 256)[:, :HD + ROPE].reshape(N_SHARD, -1, 256),
           _cols_to_shards(g["b_w_ukv"])]
    g["ffn1_w_gate"], g["ffn1_w_up"], g["ffn1_w_down"], dhp, *landed_mid = ffn_bwd(
        x, dx1, w["ffn1_norm"], w["ffn1_w_gate"], w["ffn1_w_up"], w["ffn1_w_down"],
        "ffn_bwd" if late is None else "ffn_bwd_scatter", scatter=() if late is None else mid)
    grad_x, g["ffn1_norm"] = norm_bwd(x, w["ffn1_norm"], dhp, dx1, "ffn_norm_bwd")
    landed = dict(zip(FFN2 + ("w_out", "w_in", "b_w_uq", "b_w_ukv"), landed_early + landed_mid))

    g["a_q_norm"] = jnp.sum(dgq.reshape(HEADS, HD), axis=0, keepdims=True)
    g["a_k_norm"] = jnp.sum(dgk.reshape(HEADS, HD), axis=0, keepdims=True)
    g["a_rel_bias"] = bias_block_grad(dbias)
    g["b_q_nope_norm"] = dgq128[:, :HD]
    g["b_q_rope_norm"] = dgq128[:, HD:HD + ROPE]
    g["b_k_nope_norm"] = dgk128[:, :HD]
    g["b_k_rope_norm"] = dgkr128[:, :ROPE]
    return loss, grad_x, g, landed


ANY = pl.BlockSpec(memory_space=pl.ANY)
N_DEV = 8


def _place():
    return lax.axis_index("x"), lax.axis_index("y"), lax.axis_index("c")


def _flip(v, bit):
    return 1 - v if bit else v


BF16_ROWS = 16


def _split_axis(shape):
    return 0 if (shape[0] // 2) % BF16_ROWS == 0 else 1


def _half_shape(shape):
    axis = _split_axis(shape)
    return tuple(s // 2 if a == axis else s for a, s in enumerate(shape))


def _half(shape, core):
    axis = _split_axis(shape)
    size = shape[axis] // 2
    return tuple(pl.ds(core * size, size) if a == axis else slice(None) for a in range(2))


class GatherPlan:
    def __init__(self, ws):
        self.shapes = [w.shape for w in ws]
        self.n = len(ws)
        self.out_shape = [jax.ShapeDtypeStruct((N_SHARD,) + w.shape, w.dtype) for w in ws]
        self.scratch = [pltpu.SemaphoreType.DMA((6 * self.n,)), pltpu.SemaphoreType.DMA((6 * self.n,))]

    def _copies(self, ins, outs, sems):
        x, y, c = _place()
        s_me = 2 * x + y
        sibling = (x, y, 1 - c)
        send_sems, recv_sems = sems

        def remote(k, src, dst, to):
            return pltpu.make_async_remote_copy(src_ref=src, dst_ref=dst, send_sem=send_sems.at[k],
                                                recv_sem=recv_sems.at[k], device_id=to, device_id_type=MESH)

        ici, fwd = [], []
        for a in range(self.n):
            mine, theirs = _half(self.shapes[a], c), _half(self.shapes[a], 1 - c)
            for j, (cx, cy) in enumerate([(1 - x, y), (x, 1 - y), (1 - x, 1 - y)]):
                got = outs[a].at[(2 * cx + cy,) + mine]
                ici.append((remote(6 * a + j, ins[a].at[mine], outs[a].at[(s_me,) + mine], (cx, cy, c)),
                            remote(6 * a + j, got, got, (cx, cy, c))))
                passed = outs[a].at[(2 * cx + cy,) + theirs]
                fwd.append((remote(6 * a + 3 + j, got, got, sibling), remote(6 * a + 3 + j, passed, passed, sibling)))
        return ici, fwd

    def start(self, ins, outs, sems):
        for send, _ in self._copies(ins, outs, sems)[0]:
            send.start()

    def forward(self, ins, outs, sems):
        ici, fwd = self._copies(ins, outs, sems)
        for (_, arrival), (send, _) in zip(ici, fwd):
            arrival.wait_recv()
            send.start()

    def finish(self, ins, outs, sems):
        ici, fwd = self._copies(ins, outs, sems)
        for _, arrival in fwd:
            arrival.wait_recv()
        for send, _ in ici + fwd:
            send.wait_send()


def allgather_shards(ws):
    plan = GatherPlan(ws)
    n = plan.n

    def body(*refs):
        ins, outs, sems = refs[:n], refs[n:2 * n], refs[2 * n:]
        plan.start(ins, outs, sems)
        plan.forward(ins, outs, sems)
        plan.finish(ins, outs, sems)

    return pl.pallas_call(
        body, name="allgather_shards", in_specs=[ANY] * n, out_specs=[ANY] * n,
        out_shape=plan.out_shape, scratch_shapes=plan.scratch,
    )(*ws)


class ScatterPlan:
    def __init__(self, gs):
        self.shapes = [g.shape[1:] for g in gs]
        self.n = len(gs)
        self.out_shape = [jax.ShapeDtypeStruct((N_DEV,) + _half_shape(g.shape[1:]), g.dtype) for g in gs]
        self.scratch = [pltpu.SemaphoreType.DMA((7 * self.n,)), pltpu.SemaphoreType.DMA((7 * self.n,)),
                        pltpu.SemaphoreType.DMA((self.n,))]

    def _copies(self, ins, outs, sems):
        x, y, c = _place()
        me = 4 * x + 2 * y + c
        send_sems, recv_sems, local_sems = sems
        local, sends, arrivals = [], [], []
        for a in range(self.n):
            piece = lambda px, py, pc, a=a: ins[a].at[(2 * px + py,) + _half(self.shapes[a], pc)]
            local.append(pltpu.make_async_copy(piece(x, y, c), outs[a].at[me], local_sems.at[a]))
            for k in range(1, N_DEV):
                px, py, pc = _flip(x, k & 4), _flip(y, k & 2), _flip(c, k & 1)
                sem = dict(send_sem=send_sems.at[7 * a + k - 1], recv_sem=recv_sems.at[7 * a + k - 1],
                           device_id=(px, py, pc), device_id_type=MESH)
                sends.append(pltpu.make_async_remote_copy(
                    src_ref=piece(px, py, pc), dst_ref=outs[a].at[me], **sem))
                slot = outs[a].at[4 * px + 2 * py + pc]
                arrivals.append(pltpu.make_async_remote_copy(src_ref=slot, dst_ref=slot, **sem))
        return local, sends, arrivals

    def start(self, ins, outs, sems):
        local, sends, _ = self._copies(ins, outs, sems)
        for cp in local + sends:
            cp.start()

    def finish(self, ins, outs, sems):
        local, sends, arrivals = self._copies(ins, outs, sems)
        for cp in arrivals:
            cp.wait_recv()
        for cp in sends:
            cp.wait_send()
        for cp in local:
            cp.wait()


def scatter_partials(gs):
    plan = ScatterPlan(gs)
    n = plan.n

    def body(*refs):
        ins, outs, sems = refs[:n], refs[n:2 * n], refs[2 * n:]
        plan.start(ins, outs, sems)
        plan.finish(ins, outs, sems)

    return pl.pallas_call(
        body, name="scatter_partials", in_specs=[ANY] * n, out_specs=[ANY] * n,
        out_shape=plan.out_shape, scratch_shapes=plan.scratch,
    )(*gs)


def sum_slots(land, name):
    _, rows, cols = land.shape
    tr = rows // 2 if rows > 128 and (rows // 2) % BF16_ROWS == 0 else rows

    def body(l_ref, o_ref):
        acc = l_ref[0].astype(F32)
        for s in range(1, N_DEV):
            acc = acc + l_ref[s].astype(F32)
        o_ref[...] = acc

    return pl.pallas_call(
        body, name=name, grid=(rows // tr,),
        in_specs=[pl.BlockSpec((N_DEV, tr, cols), lambda i: (0, i, 0))],
        out_specs=pl.BlockSpec((tr, cols), lambda i: (i, 0)),
        out_shape=jax.ShapeDtypeStruct((rows, cols), F32),
        compiler_params=_params(("parallel",)),
    )(land)


def join_halves(hs, shapes):
    n = len(hs)

    def body(*refs):
        ins, outs = refs[:n], refs[n:2 * n]
        send_sems, recv_sems = refs[2 * n:]
        x, y, c = _place()
        sends = []
        for a in range(n):
            mine = outs[a].at[_half(shapes[a], c)]
            sends.append(pltpu.make_async_remote_copy(
                src_ref=ins[a], dst_ref=mine, send_sem=send_sems.at[a], recv_sem=recv_sems.at[a],
                device_id=(x, y, 1 - c), device_id_type=MESH))
            sends[-1].start()
        for a in range(n):
            theirs = outs[a].at[_half(shapes[a], 1 - c)]
            pltpu.make_async_remote_copy(
                src_ref=theirs, dst_ref=theirs, send_sem=send_sems.at[a], recv_sem=recv_sems.at[a],
                device_id=(x, y, 1 - c), device_id_type=MESH).wait_recv()
        for cp in sends:
            cp.wait_send()

    return pl.pallas_call(
        body, name="join_halves",
        in_specs=[ANY] * n, out_specs=[ANY] * n,
        out_shape=[jax.ShapeDtypeStruct(tuple(s), h.dtype) for s, h in zip(shapes, hs)],
        scratch_shapes=[pltpu.SemaphoreType.DMA((n,)), pltpu.SemaphoreType.DMA((n,))],
    )(*hs)


def allreduce_small(vec):
    def body(v_ref, o_ref, land_ref, send_sems, recv_sems):
        x, y, c = _place()
        me = 4 * x + 2 * y + c
        land_ref[me] = v_ref[...]
        sends = []
        for k in range(1, N_DEV):
            px, py, pc = _flip(x, k & 4), _flip(y, k & 2), _flip(c, k & 1)
            sends.append(pltpu.make_async_remote_copy(
                src_ref=v_ref, dst_ref=land_ref.at[me], send_sem=send_sems.at[k - 1], recv_sem=recv_sems.at[k - 1],
                device_id=(px, py, pc), device_id_type=MESH))
            sends[-1].start()
        for k in range(1, N_DEV):
            px, py, pc = _flip(x, k & 4), _flip(y, k & 2), _flip(c, k & 1)
            slot = land_ref.at[4 * px + 2 * py + pc]
            pltpu.make_async_remote_copy(
                src_ref=slot, dst_ref=slot, send_sem=send_sems.at[k - 1], recv_sem=recv_sems.at[k - 1],
                device_id=(px, py, pc), device_id_type=MESH).wait_recv()
        for cp in sends:
            cp.wait_send()
        acc = land_ref[0]
        for s in range(1, N_DEV):
            acc = acc + land_ref[s]
        o_ref[...] = acc

    vm = pl.BlockSpec(memory_space=pltpu.VMEM)
    return pl.pallas_call(
        body, name="allreduce_small",
        in_specs=[vm], out_specs=vm,
        out_shape=jax.ShapeDtypeStruct(vec.shape, F32),
        scratch_shapes=[pltpu.VMEM((N_DEV,) + vec.shape, F32), pltpu.SemaphoreType.DMA((N_DEV - 1,)),
                        pltpu.SemaphoreType.DMA((N_DEV - 1,))],
    )(vec)


def adamw(w, g, m, v, name):
    rows, cols = w.shape
    tr = rows
    while tr * cols * 4 * 14 > 24 * 1024 * 1024 and tr % 16 == 0:
        tr //= 2
    c1 = 1.0 - ADAM_B1 ** ADAM_STEP
    c2 = 1.0 - ADAM_B2 ** ADAM_STEP

    def body(w_ref, g_ref, m_ref, v_ref, d_ref, nm_ref, nv_ref):
        gv = g_ref[...]
        nm = ADAM_B1 * m_ref[...] + (1.0 - ADAM_B1) * gv
        nv = ADAM_B2 * v_ref[...] + (1.0 - ADAM_B2) * (gv * gv)
        nm_ref[...] = nm
        nv_ref[...] = nv
        d_ref[...] = -ADAM_LR * ((nm / c1) / (jnp.sqrt(nv / c2) + ADAM_EPS) + ADAM_WD * w_ref[...])

    blk = pl.BlockSpec((tr, cols), lambda i: (i, 0))
    return pl.pallas_call(
        body, name=name, grid=(rows // tr,),
        in_specs=[blk] * 4, out_specs=[blk] * 3,
        out_shape=[jax.ShapeDtypeStruct((rows, cols), F32)] * 3,
        compiler_params=_params(("parallel",)),
    )(w, g, m, v)


BIG = ("ffn1_w_gate", "ffn1_w_up", "ffn1_w_down", "w_in", "b_w_uq", "b_w_ukv", "w_out",
       "ffn2_w_gate", "ffn2_w_up", "ffn2_w_down")
SMALL = ("ffn1_norm", "mix_norm", "a_q_norm", "a_k_norm", "a_rel_bias", "b_q_lat_norm", "b_kv_lat_norm",
         "b_q_nope_norm", "b_q_rope_norm", "b_k_nope_norm", "b_k_rope_norm", "ffn2_norm", "final_norm")
WEIGHTS = ("ffn1_norm", "ffn1_w_gate", "ffn1_w_up", "ffn1_w_down", "mix_norm", "w_in", "a_q_norm", "a_k_norm",
           "a_rel_bias", "b_q_lat_norm", "b_w_uq", "b_kv_lat_norm", "b_w_ukv", "b_q_nope_norm", "b_q_rope_norm",
           "b_k_nope_norm", "b_k_rope_norm", "w_out", "ffn2_norm", "ffn2_w_gate", "ffn2_w_up", "ffn2_w_down",
           "final_norm")
TRANSPOSED = ("ffn1_w_gate", "ffn1_w_up", "ffn2_w_gate", "ffn2_w_up", "w_in", "b_w_uq")
PACK_SHAPE = (8, 1024)


def _pack_small(d, last=None):
    flat = [d[n].reshape(-1) for n in SMALL]
    used = sum(f.shape[0] for f in flat)
    total = PACK_SHAPE[0] * PACK_SHAPE[1]
    tail = jnp.zeros((total - used - 1,), F32)
    end = jnp.zeros((1,), F32) if last is None else last.reshape(1)
    return jnp.concatenate(flat + [tail, end]).reshape(PACK_SHAPE)


def _unpack_small(p, like):
    flat = p.reshape(-1)
    out, off = {}, 0
    for n in SMALL:
        size = like[n].size
        out[n] = flat[off:off + size].reshape(like[n].shape)
        off += size
    return out, flat[-1]


def _cols_to_shards(g):
    rows, cols = g.shape
    return g.reshape(rows, N_SHARD, cols // N_SHARD).transpose(1, 0, 2)


def _shards_to_cols(g):
    return g.transpose(1, 0, 2).reshape(g.shape[1], -1)


def kernel(x, ffn1_norm, ffn1_w_gate, ffn1_w_up, ffn1_w_down, mix_norm, w_in, a_q_norm, a_k_norm, a_rel_bias, b_q_lat_norm, b_w_uq, b_kv_lat_norm, b_w_ukv, b_q_nope_norm, b_q_rope_norm, b_k_nope_norm, b_k_rope_norm, w_out, ffn2_norm, ffn2_w_gate, ffn2_w_up, ffn2_w_down, final_norm, loss_target, m_ffn1_norm, m_ffn1_w_gate, m_ffn1_w_up, m_ffn1_w_down, m_mix_norm, m_w_in, m_a_q_norm, m_a_k_norm, m_a_rel_bias, m_b_q_lat_norm, m_b_w_uq, m_b_kv_lat_norm, m_b_w_ukv, m_b_q_nope_norm, m_b_q_rope_norm, m_b_k_nope_norm, m_b_k_rope_norm, m_w_out, m_ffn2_norm, m_ffn2_w_gate, m_ffn2_w_up, m_ffn2_w_down, m_final_norm, v_ffn1_norm, v_ffn1_w_gate, v_ffn1_w_up, v_ffn1_w_down, v_mix_norm, v_w_in, v_a_q_norm, v_a_k_norm, v_a_rel_bias, v_b_q_lat_norm, v_b_w_uq, v_b_kv_lat_norm, v_b_w_ukv, v_b_q_nope_norm, v_b_q_rope_norm, v_b_k_nope_norm, v_b_k_rope_norm, v_w_out, v_ffn2_norm, v_ffn2_w_gate, v_ffn2_w_up, v_ffn2_w_down, v_final_norm):
    args = locals()
    view = lambda a, n: a[0].T if n in TRANSPOSED else a[0]
    wts = {n: view(args[n], n) for n in WEIGHTS}
    mom = {n: view(args["m_" + n], n) for n in WEIGHTS}
    var = {n: view(args["v_" + n], n) for n in WEIGHTS}

    shard = 2 * lax.axis_index("x") + lax.axis_index("y")
    core = lax.axis_index("c")
    first = [n for n in BIG if n not in LATE]
    own = [wts[n].astype(BF16) for n in first]
    w = {n: wts[n] if n == "a_rel_bias" else wts[n][None] for n in SMALL}
    w.update(kernel_layout({n: lax.dynamic_update_index_in_dim(got, mine, shard, 0)
                            for n, got, mine in zip(first, allgather_shards(own), own)}))

    loss, grad_x, g, landed = local_step(x[0], loss_target[0], w,
                                         late=([wts[n].astype(BF16) for n in LATE], shard))

    landed.update(zip(first, scatter_partials([g[n] for n in first])))
    halves = [sum_slots(landed[n], "sum_slots") for n in BIG]
    shapes = [wts[n].shape for n in BIG]
    axes = [_split_axis(s) for s in shapes]
    grads = dict(zip(BIG, (lax.dynamic_update_slice_in_dim(got, mine, core * mine.shape[ax], ax)
                           for got, mine, ax in zip(join_halves(halves, shapes), halves, axes))))

    small_sum, loss_sum = _unpack_small(allreduce_small(_pack_small(g, loss[0, 0])), wts)
    grads.update(small_sum)

    delta, new_m, new_v = {}, {}, {}
    for n in BIG:
        delta[n], new_m[n], new_v[n] = adamw(wts[n], grads[n], mom[n], var[n], "adamw")
    packed = adamw(_pack_small(wts), _pack_small(grads), _pack_small(mom), _pack_small(var), "adamw_small")
    for dst, p in zip((delta, new_m, new_v), packed):
        dst.update(_unpack_small(p, wts)[0])

    lead = lambda d: [(d[n].T if n in TRANSPOSED else d[n])[None] for n in WEIGHTS]
    return (loss_sum, grad_x[None], *lead(grads), *lead(delta), *lead(new_m), *lead(new_v))
```

```python
import functools
import math

import numpy as np
import jax
import jax.numpy as jnp
from jax import lax
from jax.experimental import pallas as pl
from jax.experimental.pallas import tpu as pltpu

F32 = jnp.float32
BF16 = jnp.bfloat16
EPS = 1e-6
NEG = -1e30

D_MODEL = 1024
D_FF = 2816
N_SHARD = 4
FS = D_FF // N_SHARD
CHUNK = 64
A_LEFT = 8
A_MAX_REL = 128
HEADS = 8
HD = 64
ROPE = 32
PROJ_W = 2048
IN_COLS = 1952
B_SCALE = 96 ** -0.5
LANES = 128

ADAM_LR = 0.001
ADAM_B1 = 0.9
ADAM_B2 = 0.999
ADAM_EPS = 1e-08
ADAM_WD = 0.01
ADAM_STEP = 10

VMEM_LIMIT = 56 * 1024 * 1024

MESH = pl.DeviceIdType.MESH


def _dot(a, b):
    return lax.dot_general(a, b, (((1,), (0,)), ((), ())), preferred_element_type=F32)


def _dot_nt(a, b):
    return lax.dot_general(a, b, (((1,), (1,)), ((), ())), preferred_element_type=F32)


def _dot_tn(a, b):
    return lax.dot_general(a, b, (((0,), (0,)), ((), ())), preferred_element_type=F32)


def _params(sem):
    return pltpu.CompilerParams(dimension_semantics=sem, vmem_limit_bytes=VMEM_LIMIT)


def _rms(xv):
    r = lax.rsqrt(jnp.mean(xv * xv, axis=-1, keepdims=True) + EPS)
    return r, xv * r


def ffn_fwd(x, g, wg, wu, wd, name, gather=()):
    t, d = x.shape
    tm = 512
    ni = t // tm
    plan = GatherPlan(gather)
    n = plan.n

    def body(*refs):
        x_ref, g_ref, wg_ref, wu_ref, wd_ref = refs[:5]
        ins, o_ref, outs = refs[5:5 + n], refs[5 + n], refs[6 + n:6 + 2 * n]
        h_ref, acc_ref = refs[6 + 2 * n:8 + 2 * n]
        sems = refs[8 + 2 * n:]
        i, j = pl.program_id(0), pl.program_id(1)
        if n:
            pl.when((i == 0) & (j == 0))(lambda: plan.start(ins, outs, sems))
            pl.when((i == (3 * ni) // 4) & (j == 0))(lambda: plan.forward(ins, outs, sems))

        @pl.when(j == 0)
        def _():
            _, xn = _rms(x_ref[...])
            h_ref[...] = (xn * g_ref[...]).astype(BF16)
            acc_ref[...] = jnp.zeros_like(acc_ref)

        h = h_ref[...]
        gp = _dot_nt(h, wg_ref[0])
        up = _dot_nt(h, wu_ref[0])
        a = (gp * jax.nn.sigmoid(gp) * up).astype(BF16)
        acc_ref[...] += _dot(a, wd_ref[0])

        @pl.when(j == N_SHARD - 1)
        def _():
            o_ref[...] = x_ref[...] + 0.5 * acc_ref[...]

        if n:
            pl.when((i == ni - 1) & (j == N_SHARD - 1))(lambda: plan.finish(ins, outs, sems))

    res = pl.pallas_call(
        body, name=name, grid=(ni, N_SHARD),
        in_specs=[pl.BlockSpec((tm, d), lambda i, j: (i, 0)),
                  pl.BlockSpec((1, d), lambda i, j: (0, 0)),
                  pl.BlockSpec((1, FS, d), lambda i, j: (j, 0, 0)),
                  pl.BlockSpec((1, FS, d), lambda i, j: (j, 0, 0)),
                  pl.BlockSpec((1, FS, d), lambda i, j: (j, 0, 0))] + [ANY] * n,
        out_specs=[pl.BlockSpec((tm, d), lambda i, j: (i, 0))] + [ANY] * n,
        out_shape=[jax.ShapeDtypeStruct((t, d), F32)] + plan.out_shape,
        scratch_shapes=[pltpu.VMEM((tm, d), BF16), pltpu.VMEM((tm, d), F32)] + (plan.scratch if n else []),
        compiler_params=_params(("arbitrary", "arbitrary")),
    )(x, g, wg, wu, wd, *gather)
    return res if n else res[0]


def ffn_bwd(x, dout, g, wg, wu, wd, name, scatter=(), spread=None):
    t, d = x.shape
    tm = 512
    ni = t // tm
    hf = FS // 2
    plan = ScatterPlan(scatter)
    m = plan.n
    k = 0 if spread is None else 3
    steps = jnp.arange(N_SHARD, dtype=jnp.int32)
    order = steps if spread is None else (spread + 1 + steps) % N_SHARD

    def body(*refs):
        ord_ref, x_ref, do_ref, g_ref, wg_ref, wu_ref, wd_ref = refs[:7]
        ins, (dwg_out, dwu_out, dwd_out, dhp_ref) = refs[7:7 + m], refs[7 + m:11 + m]
        outs, lands = refs[11 + m:11 + 2 * m], refs[11 + 2 * m:11 + 2 * m + k]
        dwg_ref, dwu_ref, dwd_ref = refs[11 + 2 * m + k:14 + 2 * m + k]
        sems = refs[14 + 2 * m + k:17 + 2 * m + k] if m else ()
        stage_ref = refs[-3] if k else None
        j, i = pl.program_id(0), pl.program_id(1)
        if m:
            pl.when((j == 0) & (i == 0))(lambda: plan.start(ins, outs, sems))

        def chunk_copies(jj):
            send_sems, recv_sems = refs[-2:]
            px, py, pc = _place()
            me = 4 * px + 2 * py + pc
            tx, ty = ord_ref[jj] // 2, ord_ref[jj] % 2
            copies = []
            for n_ in range(3):
                for h_ in range(2):
                    copies.append((pltpu.make_async_remote_copy(
                        src_ref=stage_ref.at[n_, pl.ds(h_ * hf, hf)], dst_ref=lands[n_].at[me],
                        send_sem=send_sems.at[6 * jj + 2 * n_ + h_], recv_sem=recv_sems.at[3 * me + n_],
                        device_id=(tx, ty, h_), device_id_type=MESH), (tx != px) | (ty != py) | (pc != h_)))
            return copies

        def arrivals():
            send_sems, recv_sems = refs[-2:]
            px, py, pc = _place()
            me = 4 * px + 2 * py + pc
            for s_ in range(N_DEV):
                for n_ in range(3):
                    slot = lands[n_].at[s_]
                    cp = pltpu.make_async_remote_copy(
                        src_ref=slot, dst_ref=slot, send_sem=send_sems.at[0], recv_sem=recv_sems.at[3 * s_ + n_],
                        device_id=(px, py, pc), device_id_type=MESH)
                    pl.when(me != s_)(cp.wait_recv)

        _, xn = _rms(x_ref[...])
        h = (xn * g_ref[...]).astype(BF16)
        dz = (0.5 * do_ref[...]).astype(BF16)
        wgv, wuv, wdv = wg_ref[0], wu_ref[0], wd_ref[0]
        gp = _dot_nt(h, wgv)
        up = _dot_nt(h, wuv)
        s = jax.nn.sigmoid(gp)
        sg = gp * s
        a = (sg * up).astype(BF16)
        da = _dot_nt(dz, wdv)
        dup = (da * sg).astype(BF16)
        dgp = (da * up * (s * (1.0 + gp * (1.0 - s)))).astype(BF16)

        @pl.when(i == 0)
        def _():
            dwg_ref[...] = jnp.zeros_like(dwg_ref)
            dwu_ref[...] = jnp.zeros_like(dwu_ref)
            dwd_ref[...] = jnp.zeros_like(dwd_ref)

        dwd_ref[...] += _dot_tn(a, dz)
        dwg_ref[...] += _dot_tn(dgp, h)
        dwu_ref[...] += _dot_tn(dup, h)
        dhp_ref[0] = (_dot(dgp, wgv) + _dot(dup, wuv)).astype(BF16)

        @pl.when(i == ni - 1)
        def _():
            dwg_out[0] = dwg_ref[...].astype(BF16)
            dwu_out[0] = dwu_ref[...].astype(BF16)
            dwd_out[0] = dwd_ref[...].astype(BF16)
            if k:
                @pl.when(j >= 1)
                def _():
                    for cp, leaves in chunk_copies(j - 1):
                        pl.when(leaves)(cp.wait_send)
                for n_, acc in enumerate((dwg_ref, dwu_ref, dwd_ref)):
                    stage_ref[n_] = acc[...].astype(BF16)
                for cp, leaves in chunk_copies(j):
                    pl.when(leaves)(cp.start)

                @pl.when(j == N_SHARD - 1)
                def _():
                    for cp, leaves in chunk_copies(N_SHARD - 1):
                        pl.when(leaves)(cp.wait_send)
                    arrivals()

        if m:
            pl.when((j == N_SHARD - 1) & (i == ni - 1))(lambda: plan.finish(ins, outs, sems))

    chunk = pl.BlockSpec((1, FS, d), lambda j, i, o: (o[j], 0, 0))
    tok = pl.BlockSpec((tm, d), lambda j, i, o: (i, 0))
    grid_spec = pltpu.PrefetchScalarGridSpec(
        num_scalar_prefetch=1, grid=(N_SHARD, ni),
        in_specs=[tok, tok, pl.BlockSpec((1, d), lambda j, i, o: (0, 0)), chunk, chunk, chunk] + [ANY] * m,
        out_specs=[chunk, chunk, chunk, pl.BlockSpec((1, tm, d), lambda j, i, o: (o[j], i, 0))] + [ANY] * (m + k),
        scratch_shapes=[pltpu.VMEM((FS, d), F32), pltpu.VMEM((FS, d), F32), pltpu.VMEM((FS, d), F32)]
        + (plan.scratch if m else [])
        + ([pltpu.VMEM((3, FS, d), BF16), pltpu.SemaphoreType.DMA((6 * N_SHARD,)),
            pltpu.SemaphoreType.DMA((3 * N_DEV,))] if k else []))
    return pl.pallas_call(
        body, name=name, grid_spec=grid_spec,
        out_shape=[jax.ShapeDtypeStruct((N_SHARD, FS, d), BF16),
                   jax.ShapeDtypeStruct((N_SHARD, FS, d), BF16),
                   jax.ShapeDtypeStruct((N_SHARD, FS, d), BF16),
                   jax.ShapeDtypeStruct((N_SHARD, t, d), BF16)] + plan.out_shape
        + [jax.ShapeDtypeStruct((N_DEV, hf, d), BF16)] * k,
        compiler_params=_params(("arbitrary", "arbitrary")),
    )(order, x, dout, g, wg, wu, wd, *scatter)


def norm_bwd(x, g, dhp, dres, name):
    t, d = x.shape
    p = dhp.shape[0]
    tm = 512

    def body(x_ref, g_ref, dhp_ref, dres_ref, dx_ref, dg_ref):
        i = pl.program_id(0)
        r, xn = _rms(x_ref[...])
        dh = dhp_ref[0].astype(F32)
        for q in range(1, p):
            dh = dh + dhp_ref[q].astype(F32)
        dhg = dh * g_ref[...]
        dx_ref[...] = dres_ref[...] + r * (dhg - xn * jnp.mean(dhg * xn, axis=-1, keepdims=True))

        @pl.when(i == 0)
        def _():
            dg_ref[...] = jnp.zeros_like(dg_ref)

        dg_ref[...] += jnp.sum(dh * xn, axis=0, keepdims=True)

    return pl.pallas_call(
        body, name=name, grid=(t // tm,),
        in_specs=[pl.BlockSpec((tm, d), lambda i: (i, 0)),
                  pl.BlockSpec((1, d), lambda i: (0, 0)),
                  pl.BlockSpec((p, tm, d), lambda i: (0, i, 0)),
                  pl.BlockSpec((tm, d), lambda i: (i, 0))],
        out_specs=[pl.BlockSpec((tm, d), lambda i: (i, 0)),
                   pl.BlockSpec((1, d), lambda i: (0, 0))],
        out_shape=[jax.ShapeDtypeStruct((t, d), F32), jax.ShapeDtypeStruct((1, d), F32)],
        compiler_params=_params(("arbitrary",)),
    )(x, g, dhp, dres)


def final_loss(x, g, target):
    t, d = x.shape
    tm = 512

    def body(x_ref, g_ref, t_ref, dx_ref, dg_ref, loss_ref):
        i = pl.program_id(0)
        r, xn = _rms(x_ref[...])
        gv = g_ref[...]
        e = xn * gv - t_ref[...]
        dy = e * (1.0 / d)
        dhg = dy * gv
        dx_ref[...] = r * (dhg - xn * jnp.mean(dhg * xn, axis=-1, keepdims=True))

        @pl.when(i == 0)
        def _():
            dg_ref[...] = jnp.zeros_like(dg_ref)
            loss_ref[...] = jnp.zeros_like(loss_ref)

        dg_ref[...] += jnp.sum(dy * xn, axis=0, keepdims=True)
        part = jnp.sum(jnp.sum(e * e, axis=-1, keepdims=True), axis=0, keepdims=True) * (0.5 / d)
        loss_ref[...] += jnp.broadcast_to(part, loss_ref.shape)

    return pl.pallas_call(
        body, name="final_loss", grid=(t // tm,),
        in_specs=[pl.BlockSpec((tm, d), lambda i: (i, 0)),
                  pl.BlockSpec((1, d), lambda i: (0, 0)),
                  pl.BlockSpec((tm, d), lambda i: (i, 0))],
        out_specs=[pl.BlockSpec((tm, d), lambda i: (i, 0)),
                   pl.BlockSpec((1, d), lambda i: (0, 0)),
                   pl.BlockSpec((1, LANES), lambda i: (0, 0))],
        out_shape=[jax.ShapeDtypeStruct((t, d), F32), jax.ShapeDtypeStruct((1, d), F32),
                   jax.ShapeDtypeStruct((1, LANES), F32)],
        compiler_params=_params(("arbitrary",)),
    )(x, g, target)


def mix_proj(x, g, w):
    t, d = x.shape
    n = w.shape[0]
    tm = 512

    def body(x_ref, g_ref, w_ref, h_ref, p_ref):
        _, xn = _rms(x_ref[...])
        h = (xn * g_ref[...]).astype(BF16)
        h_ref[...] = h
        p_ref[...] = _dot_nt(h, w_ref[...])

    return pl.pallas_call(
        body, name="mix_proj", grid=(t // tm,),
        in_specs=[pl.BlockSpec((tm, d), lambda i: (i, 0)),
                  pl.BlockSpec((1, d), lambda i: (0, 0)),
                  pl.BlockSpec((n, d), lambda i: (0, 0))],
        out_specs=[pl.BlockSpec((tm, d), lambda i: (i, 0)),
                   pl.BlockSpec((tm, n), lambda i: (i, 0))],
        out_shape=[jax.ShapeDtypeStruct((t, d), BF16), jax.ShapeDtypeStruct((t, n), F32)],
        compiler_params=_params(("parallel",)),
    )(x, g, w)


def matmul(a, b, mode, name, tm, tn, tk, out_dtype=F32):
    if mode == "nn":
        (m, k), n = a.shape, b.shape[1]
        a_spec = pl.BlockSpec((tm, tk), lambda i, j, q: (i, q))
        b_spec = pl.BlockSpec((tk, tn), lambda i, j, q: (q, j))
        dot = _dot
    elif mode == "nt":
        (m, k), n = a.shape, b.shape[0]
        a_spec = pl.BlockSpec((tm, tk), lambda i, j, q: (i, q))
        b_spec = pl.BlockSpec((tn, tk), lambda i, j, q: (j, q))
        dot = _dot_nt
    else:
        (k, m), n = a.shape, b.shape[1]
        a_spec = pl.BlockSpec((tk, tm), lambda i, j, q: (q, i))
        b_spec = pl.BlockSpec((tk, tn), lambda i, j, q: (q, j))
        dot = _dot_tn
    assert m % tm == 0 and n % tn == 0 and k % tk == 0, (m, n, k, tm, tn, tk)
    nk = k // tk

    def body(a_ref, b_ref, o_ref, acc_ref):
        q = pl.program_id(2)

        @pl.when(q == 0)
        def _():
            acc_ref[...] = jnp.zeros_like(acc_ref)

        acc_ref[...] += dot(a_ref[...].astype(BF16), b_ref[...].astype(BF16))

        @pl.when(q == nk - 1)
        def _():
            o_ref[...] = acc_ref[...].astype(out_dtype)

    return pl.pallas_call(
        body, name=name, grid=(m // tm, n // tn, nk),
        in_specs=[a_spec, b_spec],
        out_specs=pl.BlockSpec((tm, tn), lambda i, j, q: (i, j)),
        out_shape=jax.ShapeDtypeStruct((m, n), out_dtype),
        scratch_shapes=[pltpu.VMEM((tm, tn), F32)],
        compiler_params=_params(("parallel", "parallel", "arbitrary")),
    )(a, b)


def out_proj(x, oa, ob_t, w):
    t, d = x.shape
    half = oa.shape[1]
    tm = 512

    def body(x_ref, oa_ref, obt_ref, w_ref, o_ref):
        o_ref[...] = (x_ref[...] + _dot(oa_ref[...], w_ref[0:half, :])
                      + _dot_tn(obt_ref[...], w_ref[half:2 * half, :]))

    return pl.pallas_call(
        body, name="out_proj", grid=(t // tm,),
        in_specs=[pl.BlockSpec((tm, d), lambda i: (i, 0)),
                  pl.BlockSpec((tm, half), lambda i: (i, 0)),
                  pl.BlockSpec((half, tm), lambda i: (0, i)),
                  pl.BlockSpec((2 * half, d), lambda i: (0, 0))],
        out_specs=pl.BlockSpec((tm, d), lambda i: (i, 0)),
        out_shape=jax.ShapeDtypeStruct((t, d), F32),
        compiler_params=_params(("parallel",)),
    )(x, oa, ob_t, w)


def out_proj_bwd(dx, w):
    t, d = dx.shape
    half = w.shape[0] // 2
    tm = 512

    def body(dx_ref, w_ref, da_ref, db_ref, dbt_ref):
        dxb = dx_ref[...].astype(BF16)
        da_ref[...] = _dot_nt(dxb, w_ref[0:half, :]).astype(BF16)
        db_ref[...] = _dot_nt(dxb, w_ref[half:2 * half, :]).astype(BF16)
        dbt_ref[...] = _dot_nt(w_ref[half:2 * half, :], dxb).astype(BF16)

    return pl.pallas_call(
        body, name="out_proj_bwd", grid=(t // tm,),
        in_specs=[pl.BlockSpec((tm, d), lambda i: (i, 0)),
                  pl.BlockSpec((2 * half, d), lambda i: (0, 0))],
        out_specs=[pl.BlockSpec((tm, half), lambda i: (i, 0)),
                   pl.BlockSpec((tm, half), lambda i: (i, 0)),
                   pl.BlockSpec((half, tm), lambda i: (0, i))],
        out_shape=[jax.ShapeDtypeStruct((t, half), BF16), jax.ShapeDtypeStruct((t, half), BF16),
                   jax.ShapeDtypeStruct((half, t), BF16)],
        compiler_params=_params(("parallel",)),
    )(dx, w)


def _lane(shape):
    return lax.broadcasted_iota(jnp.int32, shape, 1)


def _seg_sum(z, mask):
    return jnp.sum(jnp.where(mask, z, 0.0), axis=-1, keepdims=True)


def _pair_norm(x):
    lo = _lane(x.shape) < HD
    x2 = x * x
    r = jnp.where(lo, lax.rsqrt(_seg_sum(x2, lo) * (1.0 / HD) + EPS),
                  lax.rsqrt(_seg_sum(x2, ~lo) * (1.0 / HD) + EPS))
    return lo, r, x * r


def _pair_norm_bwd(lo, r, xn, dyg):
    z = dyg * xn
    mean = jnp.where(lo, _seg_sum(z, lo), _seg_sum(z, ~lo)) * (1.0 / HD)
    return r * (dyg - xn * mean)


A_TM = 256


def prep1_fwd(proj, gq, gk, gcq, gckv):
    t = proj.shape[0]
    tm = A_TM

    def body(p_ref, gq_ref, gk_ref, gcq_ref, gckv_ref, qa_ref, ka_ref, va_ref, cq_ref, ckv_ref):
        for p in range(4):
            sl = slice(LANES * p, LANES * (p + 1))
            _, _, xn = _pair_norm(p_ref[:, sl])
            qa_ref[:, sl] = (xn * gq_ref[:, sl] * 0.125).astype(BF16)
            _, _, xn = _pair_norm(p_ref[:, 512 + LANES * p:512 + LANES * (p + 1)])
            ka_ref[:, sl] = (xn * gk_ref[:, sl]).astype(BF16)
        va_ref[...] = p_ref[:, 1024:1536].astype(BF16)
        _, xn = _rms(p_ref[:, 1536:1792])
        cq_ref[...] = (xn * gcq_ref[...]).astype(BF16)
        _, xn = _rms(p_ref[:, 1792:1920])
        ckv_ref[...] = (xn * gckv_ref[...]).astype(BF16)

    row = lambda w: pl.BlockSpec((tm, w), lambda i: (i, 0))
    vec = lambda w: pl.BlockSpec((1, w), lambda i: (0, 0))
    return pl.pallas_call(
        body, name="prep1_fwd", grid=(t // tm,),
        in_specs=[row(PROJ_W), vec(512), vec(512), vec(256), vec(128)],
        out_specs=[row(512), row(512), row(512), row(256), row(128)],
        out_shape=[jax.ShapeDtypeStruct((t, w), BF16) for w in (512, 512, 512, 256, 128)],
        compiler_params=_params(("parallel",)),
    )(proj, gq, gk, gcq, gckv)


def prep1_bwd(proj, dqa, dkp, dvp, dcq, dckv, dkr, gq, gk, gcq, gckv):
    t = proj.shape[0]
    tm = A_TM
    nb = t // tm

    def body(p_ref, dqa_ref, dk0_ref, dk1_ref, dk2_ref, dv0_ref, dv1_ref, dv2_ref, dcq_ref, dckv_ref, dkr_ref,
             gq_ref, gk_ref, gcq_ref, gckv_ref, dp_ref, dgq_ref, dgk_ref, dgcq_ref, dgckv_ref):
        i = pl.program_id(0)

        @pl.when(i == 0)
        def _():
            dgq_ref[...] = jnp.zeros_like(dgq_ref)
            dgk_ref[...] = jnp.zeros_like(dgk_ref)
            dgcq_ref[...] = jnp.zeros_like(dgcq_ref)
            dgckv_ref[...] = jnp.zeros_like(dgckv_ref)

        has1 = (i + 1 < nb).astype(F32)
        has2 = (i + 2 < nb).astype(F32)
        for p in range(4):
            sl = slice(LANES * p, LANES * (p + 1))
            lo, r, xn = _pair_norm(p_ref[:, sl])
            dy = dqa_ref[:, sl] * 0.125
            dp_ref[:, sl] = _pair_norm_bwd(lo, r, xn, dy * gq_ref[:, sl]).astype(BF16)
            dgq_ref[:, sl] += jnp.sum(dy * xn, axis=0, keepdims=True)
            ks = slice(512 + LANES * p, 512 + LANES * (p + 1))
            lo, r, xn = _pair_norm(p_ref[:, ks])
            dy = dk0_ref[0, :, sl] + has1 * dk1_ref[0, :, sl] + has2 * dk2_ref[0, :, sl]
            dp_ref[:, ks] = _pair_norm_bwd(lo, r, xn, dy * gk_ref[:, sl]).astype(BF16)
            dgk_ref[:, sl] += jnp.sum(dy * xn, axis=0, keepdims=True)
        dp_ref[:, 1024:1536] = (dv0_ref[0] + has1 * dv1_ref[0] + has2 * dv2_ref[0]).astype(BF16)
        for (a, b, d_ref, g_ref, dg_ref) in ((1536, 1792, dcq_ref, gcq_ref, dgcq_ref),
                                             (1792, 1920, dckv_ref, gckv_ref, dgckv_ref)):
            r, xn = _rms(p_ref[:, a:b])
            dy = d_ref[...]
            dyg = dy * g_ref[...]
            dp_ref[:, a:b] = (r * (dyg - xn * jnp.mean(dyg * xn, axis=-1, keepdims=True))).astype(BF16)
            dg_ref[...] += jnp.sum(dy * xn, axis=0, keepdims=True)
        dp_ref[:, 1920:2048] = dkr_ref[...].astype(BF16)

    row = lambda w: pl.BlockSpec((tm, w), lambda i: (i, 0))
    vec = lambda w: pl.BlockSpec((1, w), lambda i: (0, 0))
    part = lambda s: pl.BlockSpec((1, tm, 512), lambda i: (s, jnp.minimum(i + s, nb - 1), 0))
    return pl.pallas_call(
        body, name="prep1_bwd", grid=(nb,),
        in_specs=[row(PROJ_W), row(512), part(0), part(1), part(2), part(0), part(1), part(2),
                  row(256), row(128), row(128), vec(512), vec(512), vec(256), vec(128)],
        out_specs=[row(PROJ_W), vec(512), vec(512), vec(256), vec(128)],
        out_shape=[jax.ShapeDtypeStruct((t, PROJ_W), BF16)] + [jax.ShapeDtypeStruct((1, w), F32) for w in (512, 512, 256, 128)],
        compiler_params=_params(("arbitrary",)),
    )(proj, dqa, dkp, dkp, dkp, dvp, dvp, dvp, dcq, dckv, dkr, gq, gk, gcq, gckv)


def _roll(x, shift):
    return pltpu.roll(x, shift % LANES, 1)


def _rope(y, c, s1, s2):
    return y * c + _roll(y, -16) * s1 + _roll(y, 16) * s2


def _rope_bwd(d, c, s1, s2):
    return d * c + _roll(d * s1, 16) + _roll(d * s2, -16)


def _q_head_stats(x):
    lane = _lane(x.shape)
    mn = lane < HD
    mr = (lane >= HD) & (lane < HD + ROPE)
    x2 = x * x
    r = jnp.where(mn, lax.rsqrt(_seg_sum(x2, mn) * (1.0 / HD) + EPS),
                  lax.rsqrt(_seg_sum(x2, mr) * (1.0 / ROPE) + EPS))
    return mn, mr, r, x * r


def _kr_stats(x):
    r = lax.rsqrt(jnp.sum(x * x, axis=-1, keepdims=True) * (1.0 / ROPE) + EPS)
    return r, x * r


def prep2_fwd(qlat, kv, proj, gq, gk, gkr, tabs):
    t = qlat.shape[0]
    tm = A_TM

    def body(q_ref, kv_ref, kr_ref, gq_ref, gk_ref, gkr_ref, tab_ref, qf_ref, kf_ref, vp_ref):
        _, xn = _kr_stats(kr_ref[...])
        kpe = _roll(_rope(xn * gkr_ref[...], tab_ref[3], tab_ref[4], tab_ref[5]), 64)
        for h in range(HEADS):
            sl = slice(LANES * h, LANES * (h + 1))
            _, _, _, xn = _q_head_stats(q_ref[:, sl])
            qf_ref[:, sl] = _rope(xn * gq_ref[...], tab_ref[0], tab_ref[1], tab_ref[2]).astype(BF16)
            x = kv_ref[:, sl]
            lo = _lane(x.shape) < HD
            xk = jnp.where(lo, x, 0.0)
            rk = lax.rsqrt(jnp.sum(xk * xk, axis=-1, keepdims=True) * (1.0 / HD) + EPS)
            kf_ref[:, sl] = (xk * rk * gk_ref[...] + kpe).astype(BF16)
            if h % 2 == 0:
                v_even = _roll(x, 64)
            else:
                vp_ref[:, LANES * (h // 2):LANES * (h // 2 + 1)] = jnp.where(lo, v_even, x).astype(BF16)

    row = lambda w: pl.BlockSpec((tm, w), lambda i: (i, 0))
    vec = lambda w: pl.BlockSpec((1, w), lambda i: (0, 0))
    return pl.pallas_call(
        body, name="prep2_fwd", grid=(t // tm,),
        in_specs=[row(1024), row(1024), pl.BlockSpec((tm, LANES), lambda i: (i, 15)), vec(128), vec(128), vec(128),
                  pl.BlockSpec((6, tm, LANES), lambda i: (0, i, 0))],
        out_specs=[row(1024), row(1024), row(512)],
        out_shape=[jax.ShapeDtypeStruct((t, 1024), BF16), jax.ShapeDtypeStruct((t, 1024), BF16),
                   jax.ShapeDtypeStruct((t, 512), BF16)],
        compiler_params=_params(("parallel",)),
    )(qlat, kv, proj, gq, gk, gkr, tabs)


def prep2_bwd(qlat, kv, proj, dqf, dkf, dvp, gq, gk, gkr, tabs):
    t = qlat.shape[0]
    tm = A_TM

    def body(q_ref, kv_ref, kr_ref, dqf_ref, dkf_ref, dvp_ref, gq_ref, gk_ref, gkr_ref, tab_ref,
             dq_ref, dkv_ref, dkr_ref, dgq_ref, dgk_ref, dgkr_ref):
        i = pl.program_id(0)

        @pl.when(i == 0)
        def _():
            dgq_ref[...] = jnp.zeros_like(dgq_ref)
            dgk_ref[...] = jnp.zeros_like(dgk_ref)
            dgkr_ref[...] = jnp.zeros_like(dgkr_ref)

        dgq = jnp.zeros((1, LANES), F32)
        dgk = jnp.zeros((1, LANES), F32)
        dkpe = jnp.zeros((tm, LANES), F32)
        for h in range(HEADS):
            sl = slice(LANES * h, LANES * (h + 1))
            mn, mr, r, xn = _q_head_stats(q_ref[:, sl])
            dy = _rope_bwd(dqf_ref[sl, :].T, tab_ref[0], tab_ref[1], tab_ref[2])
            dyg = dy * gq_ref[...]
            z = dyg * xn
            mean = jnp.where(mn, _seg_sum(z, mn) * (1.0 / HD), _seg_sum(z, mr) * (1.0 / ROPE))
            dq_ref[:, sl] = (r * (dyg - xn * mean)).astype(BF16)
            dgq = dgq + jnp.sum(dy * xn, axis=0, keepdims=True)

            x = kv_ref[:, sl]
            dk = dkf_ref[:, sl]
            xk = jnp.where(mn, x, 0.0)
            rk = lax.rsqrt(jnp.sum(xk * xk, axis=-1, keepdims=True) * (1.0 / HD) + EPS)
            xkn = xk * rk
            dyk = jnp.where(mn, dk, 0.0)
            dykg = dyk * gk_ref[...]
            dxk = rk * (dykg - xkn * (jnp.sum(dykg * xkn, axis=-1, keepdims=True) * (1.0 / HD)))
            dgk = dgk + jnp.sum(dyk * xkn, axis=0, keepdims=True)
            dkpe = dkpe + jnp.where(mr, dk, 0.0)
            dvpair = dvp_ref[:, LANES * (h // 2):LANES * (h // 2 + 1)]
            dv = _roll(dvpair, 64) if h % 2 == 0 else dvpair
            dkv_ref[:, sl] = jnp.where(mn, dxk, dv).astype(BF16)

        r, xn = _kr_stats(kr_ref[...])
        dy = _rope_bwd(_roll(dkpe, 64), tab_ref[3], tab_ref[4], tab_ref[5])
        dyg = dy * gkr_ref[...]
        dkr_ref[...] = r * (dyg - xn * (jnp.sum(dyg * xn, axis=-1, keepdims=True) * (1.0 / ROPE)))
        dgq_ref[...] += dgq
        dgk_ref[...] += dgk
        dgkr_ref[...] += jnp.sum(dy * xn, axis=0, keepdims=True)

    row = lambda w: pl.BlockSpec((tm, w), lambda i: (i, 0))
    vec = lambda w: pl.BlockSpec((1, w), lambda i: (0, 0))
    return pl.pallas_call(
        body, name="prep2_bwd", grid=(t // tm,),
        in_specs=[row(1024), row(1024), pl.BlockSpec((tm, LANES), lambda i: (i, 15)),
                  pl.BlockSpec((1024, tm), lambda i: (0, i)), row(1024), row(512),
                  vec(128), vec(128), vec(128), pl.BlockSpec((6, tm, LANES), lambda i: (0, i, 0))],
        out_specs=[row(1024), row(1024), row(128), vec(128), vec(128), vec(128)],
        out_shape=[jax.ShapeDtypeStruct((t, 1024), BF16), jax.ShapeDtypeStruct((t, 1024), BF16),
                   jax.ShapeDtypeStruct((t, LANES), F32)] + [jax.ShapeDtypeStruct((1, LANES), F32)] * 3,
        compiler_params=_params(("arbitrary",)),
    )(qlat, kv, proj, dqf, dkf, dvp, gq, gk, gkr, tabs)


A_TQ = 256
A_WIN = 3 * A_TQ


def _a_specs(t):
    nb = t // A_TQ
    blk = lambda s: pl.BlockSpec((A_TQ, 512), lambda i: (jnp.maximum(i - s, 0), 0))
    return nb, blk


def _a_exp(q_ref, kc, b_ref, head, sl, lo):
    hm = lo if head % 2 == 0 else ~lo
    qm = jnp.where(hm, q_ref[:, sl], jnp.zeros((), BF16))
    s = _dot_nt(qm, kc) + b_ref[0, head]
    e = jnp.exp(s - jnp.max(s, axis=-1, keepdims=True))
    return hm, qm, e, 1.0 / jnp.sum(e, axis=-1, keepdims=True)


def _a_bias_spec():
    return pl.BlockSpec((1, HEADS, A_TQ, A_WIN), lambda i: (jnp.minimum(i, 2), 0, 0, 0))


def attn_a_fwd(qa, ka, va, bias):
    t = qa.shape[0]
    nb, blk = _a_specs(t)

    def body(q_ref, k2_ref, k1_ref, k0_ref, v2_ref, v1_ref, v0_ref, b_ref, o_ref):
        lo = _lane((A_TQ, LANES)) < HD
        for p in range(4):
            sl = slice(LANES * p, LANES * (p + 1))
            kc = jnp.concatenate([k2_ref[:, sl], k1_ref[:, sl], k0_ref[:, sl]], axis=0)
            vc = jnp.concatenate([v2_ref[:, sl], v1_ref[:, sl], v0_ref[:, sl]], axis=0)
            outs = []
            for h2 in range(2):
                _, _, e, inv = _a_exp(q_ref, kc, b_ref, 2 * p + h2, sl, lo)
                outs.append(_dot(e.astype(BF16), vc) * inv)
            o_ref[:, sl] = jnp.where(lo, outs[0], outs[1]).astype(BF16)

    return pl.pallas_call(
        body, name="attn_a_fwd", grid=(nb,),
        in_specs=[blk(0), blk(2), blk(1), blk(0), blk(2), blk(1), blk(0), _a_bias_spec()],
        out_specs=pl.BlockSpec((A_TQ, 512), lambda i: (i, 0)),
        out_shape=jax.ShapeDtypeStruct((t, 512), BF16),
        compiler_params=_params(("parallel",)),
    )(qa, ka, ka, ka, va, va, va, bias)


def attn_a_bwd(qa, ka, va, bias, do):
    t = qa.shape[0]
    nb, blk = _a_specs(t)

    def body(q_ref, k2_ref, k1_ref, k0_ref, v2_ref, v1_ref, v0_ref, b_ref, do_ref, dq_ref, dk_ref, dv_ref, db_ref):
        qb = pl.program_id(0)

        @pl.when(qb == 0)
        def _():
            db_ref[...] = jnp.zeros_like(db_ref)

        lo = _lane((A_TQ, LANES)) < HD
        for p in range(4):
            sl = slice(LANES * p, LANES * (p + 1))
            kc = jnp.concatenate([k2_ref[:, sl], k1_ref[:, sl], k0_ref[:, sl]], axis=0)
            vc = jnp.concatenate([v2_ref[:, sl], v1_ref[:, sl], v0_ref[:, sl]], axis=0)
            dqs = []
            dkc = jnp.zeros((A_WIN, LANES), F32)
            dvc = jnp.zeros((A_WIN, LANES), F32)
            for h2 in range(2):
                head = 2 * p + h2
                hm, qm, e, inv = _a_exp(q_ref, kc, b_ref, head, sl, lo)
                pr = e * inv
                dom = jnp.where(hm, do_ref[:, sl], jnp.zeros((), BF16))
                dp = _dot_nt(dom, vc)
                ds = pr * (dp - jnp.sum(pr * dp, axis=-1, keepdims=True))
                db_ref[head] += ds
                dsb = ds.astype(BF16)
                dqs.append(_dot(dsb, kc))
                dkc = dkc + _dot_tn(dsb, qm)
                dvc = dvc + _dot_tn(pr.astype(BF16), dom)
            dq_ref[:, sl] = jnp.where(lo, dqs[0], dqs[1])
            for s in range(3):
                rows = slice(A_TQ * (2 - s), A_TQ * (3 - s))
                dk_ref[s, :, sl] = dkc[rows]
                dv_ref[s, :, sl] = dvc[rows]

    share = pl.BlockSpec((3, A_TQ, 512), lambda i: (0, i, 0))
    return pl.pallas_call(
        body, name="attn_a_bwd", grid=(nb,),
        in_specs=[blk(0), blk(2), blk(1), blk(0), blk(2), blk(1), blk(0), _a_bias_spec(), blk(0)],
        out_specs=[pl.BlockSpec((A_TQ, 512), lambda i: (i, 0)), share, share,
                   pl.BlockSpec((HEADS, A_TQ, A_WIN), lambda i: (0, 0, 0))],
        out_shape=[jax.ShapeDtypeStruct((t, 512), F32), jax.ShapeDtypeStruct((3, t, 512), F32),
                   jax.ShapeDtypeStruct((3, t, 512), F32), jax.ShapeDtypeStruct((HEADS, A_TQ, A_WIN), F32)],
        compiler_params=_params(("arbitrary",)),
    )(qa, ka, ka, ka, va, va, va, bias, do)


B_T = 1024


B_SCALE2 = B_SCALE * 1.4426950408889634


def _tri_tables(n, by_query):
    pairs = [(i, j) for i in range(n) for j in range(i + 1)] if by_query else [(i, j) for j in range(n) for i in range(j, n)]
    return (np.asarray([p[0] for p in pairs], np.int32), np.asarray([p[1] for p in pairs], np.int32))


def _b_mask_t(s):
    kc = lax.broadcasted_iota(jnp.int32, s.shape, 0) // CHUNK
    qc = lax.broadcasted_iota(jnp.int32, s.shape, 1) // CHUNK
    return jnp.where(kc <= qc, s, NEG)


def attn_b_fwd(qf, kf, vp):
    t = qf.shape[0]
    n = t // B_T
    qtab, ktab = _tri_tables(n, by_query=True)

    def body(qt_ref, kt_ref, q_ref, k_ref, v_ref, o_ref, lse_ref, m_s, l_s, acc_s):
        qb, kb = qt_ref[pl.program_id(1)], kt_ref[pl.program_id(1)]

        @pl.when(kb == 0)
        def _():
            m_s[...] = jnp.full_like(m_s, NEG)
            l_s[...] = jnp.zeros_like(l_s)
            acc_s[...] = jnp.zeros_like(acc_s)

        def step(masked):
            v = v_ref[...]
            for h2 in range(2):
                sl = slice(LANES * h2, LANES * (h2 + 1))
                s = _dot_nt(k_ref[:, sl], q_ref[:, sl]) * B_SCALE2
                if masked:
                    s = _b_mask_t(s)
                m_prev = m_s[h2]
                m_new = jnp.maximum(m_prev, jnp.max(s, axis=0, keepdims=True))
                alpha = jnp.exp2(m_prev - m_new)
                pr = jnp.exp2(s - m_new)
                l_s[h2] = alpha * l_s[h2] + jnp.sum(pr, axis=0, keepdims=True)
                acc_s[h2] = alpha * acc_s[h2] + _dot_tn(v, pr.astype(BF16))
                m_s[h2] = m_new

        @pl.when(kb < qb)
        def _():
            step(False)

        @pl.when(kb == qb)
        def _():
            step(True)
            for h2 in range(2):
                l = l_s[h2]
                rows = slice(HD * h2, HD * (h2 + 1))
                o_ref[rows, :] = (acc_s[h2, rows, :] * (1.0 / l)).astype(BF16)
                lse_ref[0, h2:h2 + 1, :] = m_s[h2] + jnp.log2(l)

    grid_spec = pltpu.PrefetchScalarGridSpec(
        num_scalar_prefetch=2, grid=(4, len(qtab)),
        in_specs=[pl.BlockSpec((B_T, 256), lambda p, s, qt, kt: (qt[s], p)),
                  pl.BlockSpec((B_T, 256), lambda p, s, qt, kt: (kt[s], p)),
                  pl.BlockSpec((B_T, LANES), lambda p, s, qt, kt: (kt[s], p))],
        out_specs=[pl.BlockSpec((LANES, B_T), lambda p, s, qt, kt: (p, qt[s])),
                   pl.BlockSpec((1, 2, B_T), lambda p, s, qt, kt: (p, 0, qt[s]))],
        scratch_shapes=[pltpu.VMEM((2, 1, B_T), F32), pltpu.VMEM((2, 1, B_T), F32), pltpu.VMEM((2, LANES, B_T), F32)])
    return pl.pallas_call(
        body, name="attn_b_fwd", grid_spec=grid_spec,
        out_shape=[jax.ShapeDtypeStruct((512, t), BF16), jax.ShapeDtypeStruct((4, 2, t), F32)],
        compiler_params=_params(("parallel", "arbitrary")),
    )(jnp.asarray(qtab), jnp.asarray(ktab), qf, kf, vp)


def attn_b_bwd(qf, kf, vp, do, do_t, o_t, lse, scatter=()):
    t = qf.shape[0]
    n = t // B_T
    qtab, ktab = _tri_tables(n, by_query=False)
    plan = ScatterPlan(scatter)
    m = plan.n
    last = len(qtab) - 1

    def body(*refs):
        qt_ref, kt_ref, q_ref, k_ref, v_ref, do_ref, dot_ref, ot_ref, lse_ref = refs[:9]
        ins, (dq_ref, dk_ref, dv_ref), outs = refs[9:9 + m], refs[9 + m:12 + m], refs[12 + m:12 + 2 * m]
        sems = refs[12 + 2 * m:]
        qb, kb = qt_ref[pl.program_id(1)], kt_ref[pl.program_id(1)]
        if m:
            pl.when((pl.program_id(0) == 0) & (pl.program_id(1) == 0))(lambda: plan.start(ins, outs, sems))

        @pl.when(pl.program_id(1) == 0)
        def _():
            dq_ref[...] = jnp.zeros_like(dq_ref)

        @pl.when(qb == kb)
        def _():
            dk_ref[...] = jnp.zeros_like(dk_ref)
            dv_ref[...] = jnp.zeros_like(dv_ref)

        def step(masked):
            cols = pl.ds(pl.multiple_of(qb * B_T, B_T), B_T)
            v = v_ref[...]
            dov = do_ref[...]
            prod = dot_ref[...].astype(F32) * ot_ref[...].astype(F32)
            lo = _lane((B_T, LANES)) < HD
            for h2 in range(2):
                sl = slice(LANES * h2, LANES * (h2 + 1))
                hm = lo if h2 == 0 else ~lo
                q = q_ref[:, sl]
                k = k_ref[:, sl]
                dom = jnp.where(hm, dov, jnp.zeros((), BF16))
                delta = jnp.sum(prod[HD * h2:HD * (h2 + 1), :], axis=0, keepdims=True)
                s = _dot_nt(k, q) * B_SCALE2
                if masked:
                    s = _b_mask_t(s)
                pr = jnp.exp2(s - lse_ref[0, h2:h2 + 1, :])
                dp = _dot_nt(v, dom)
                ds = (pr * (dp - delta) * B_SCALE).astype(BF16)
                dk_ref[:, sl] += _dot(ds, q)
                dv_ref[...] += _dot(pr.astype(BF16), dom)
                dq_ref[sl, cols] += _dot_tn(k, ds)

        @pl.when(qb > kb)
        def _():
            step(False)

        @pl.when(qb == kb)
        def _():
            step(True)

        if m:
            pl.when((pl.program_id(0) == 3) & (pl.program_id(1) == last))(lambda: plan.finish(ins, outs, sems))

    qrow = lambda w: pl.BlockSpec((B_T, w), lambda p, s, qt, kt: (qt[s], p))
    qcol = pl.BlockSpec((LANES, B_T), lambda p, s, qt, kt: (p, qt[s]))
    krow = lambda w: pl.BlockSpec((B_T, w), lambda p, s, qt, kt: (kt[s], p))
    grid_spec = pltpu.PrefetchScalarGridSpec(
        num_scalar_prefetch=2, grid=(4, len(qtab)),
        in_specs=[qrow(256), krow(256), krow(LANES), qrow(LANES), qcol, qcol,
                  pl.BlockSpec((1, 2, B_T), lambda p, s, qt, kt: (p, 0, qt[s]))] + [ANY] * m,
        out_specs=[pl.BlockSpec((256, t), lambda p, s, qt, kt: (p, 0)), krow(256), krow(LANES)] + [ANY] * m,
        scratch_shapes=plan.scratch if m else [])
    return pl.pallas_call(
        body, name="attn_b_bwd", grid_spec=grid_spec,
        out_shape=[jax.ShapeDtypeStruct((1024, t), F32), jax.ShapeDtypeStruct((t, 1024), F32),
                   jax.ShapeDtypeStruct((t, 512), F32)] + plan.out_shape,
        compiler_params=_params(("arbitrary", "arbitrary")),
    )(jnp.asarray(qtab), jnp.asarray(ktab), qf, kf, vp, do, do_t, o_t, lse, *scatter)


_U_LEN = A_TQ + A_WIN - 1


def _band_mask():
    a = np.arange(A_TQ)[:, None] // CHUNK
    b = np.arange(A_WIN)[None, :] // CHUNK
    return (b >= a) & (b <= a + A_LEFT)


def bias_block(table):
    h = table.shape[0]
    n_lo = A_WIN - 1 - 2 * A_TQ - A_MAX_REL
    ext = jnp.concatenate([jnp.repeat(table[:, :1], n_lo, axis=1), table,
                           jnp.repeat(table[:, -1:], _U_LEN - n_lo - table.shape[1], axis=1)], axis=1)
    row = jnp.pad(ext[:, ::-1], ((0, 0), (0, 1)))
    flat = jnp.tile(row, (1, A_TQ))[:, :A_TQ * _U_LEN]
    skew = flat.reshape(h, A_TQ, _U_LEN)
    toep = skew[:, :, A_TQ - 1:A_TQ - 1 + A_WIN]
    band = _band_mask()
    first = [band & (np.arange(A_WIN)[None, :] >= 2 * A_TQ - A_TQ * v) for v in range(3)]
    return jnp.where(jnp.asarray(np.stack(first))[:, None], toep[None], NEG)


def bias_block_grad(db):
    h = db.shape[0]
    n_lo = A_WIN - 1 - 2 * A_TQ - A_MAX_REL
    skew = jnp.pad(db, ((0, 0), (0, 0), (A_TQ - 1, 0)))
    flat = jnp.pad(skew.reshape(h, A_TQ * _U_LEN), ((0, 0), (0, A_TQ)))
    ext = jnp.sum(flat.reshape(h, A_TQ, _U_LEN + 1), axis=1)[:, :_U_LEN][:, ::-1]
    n_tab = 2 * A_MAX_REL + 1
    first = jnp.sum(ext[:, :n_lo + 1], axis=1, keepdims=True)
    last = jnp.sum(ext[:, n_lo + n_tab - 1:], axis=1, keepdims=True)
    return jnp.concatenate([first, ext[:, n_lo + 1:n_lo + n_tab - 1], last], axis=1)


def rope_tabs(t):
    inv = 1.0 / (10000.0 ** (jnp.arange(0, ROPE, 2, dtype=F32) / ROPE))
    ang = jnp.arange(t, dtype=F32)[:, None] * inv[None, :]
    cos, sin = jnp.cos(ang), jnp.sin(ang)
    z = lambda w: jnp.zeros((t, w), F32)
    ck = jnp.concatenate([cos, cos, z(96)], axis=1)
    s1k = jnp.concatenate([-sin, z(112)], axis=1)
    s2k = jnp.concatenate([z(16), sin, z(96)], axis=1)
    cq = jnp.concatenate([jnp.ones((t, HD), F32), cos, cos, z(32)], axis=1)
    s1q = jnp.concatenate([z(HD), -sin, z(48)], axis=1)
    s2q = jnp.concatenate([z(HD + 16), sin, z(32)], axis=1)
    return jnp.stack([cq, s1q, s2q, ck, s1k, s2k])


def _pad_lanes(v, width):
    return jnp.pad(v, ((0, 0), (0, width - v.shape[1])))


LATE = ("w_in", "b_w_uq", "b_w_ukv", "w_out", "ffn2_w_gate", "ffn2_w_up", "ffn2_w_down")
FFN2 = ("ffn2_w_gate", "ffn2_w_up", "ffn2_w_down")


def kernel_layout(gathered):
    w = {n: v for n, v in gathered.items() if n.startswith("ffn")}
    if "w_in" in gathered:
        w["w_in"] = jnp.pad(gathered["w_in"].reshape(IN_COLS, D_MODEL), ((0, PROJ_W - IN_COLS), (0, 0)))
        uq = gathered["b_w_uq"].reshape(HEADS, HD + ROPE, 256)
        w["b_w_uq"] = jnp.pad(uq, ((0, 0), (0, LANES - HD - ROPE), (0, 0))).reshape(HEADS * LANES, 256)
        w["b_w_ukv"] = _shards_to_cols(gathered["b_w_ukv"])
        w["w_out"] = gathered["w_out"].reshape(N_SHARD * gathered["w_out"].shape[1], D_MODEL)
    return w


def local_step(x, target, w, late=None):
    t = x.shape[0]
    gq = jnp.tile(w["a_q_norm"], (1, HEADS))
    gk = jnp.tile(w["a_k_norm"], (1, HEADS))
    gq128 = _pad_lanes(jnp.concatenate([w["b_q_nope_norm"], w["b_q_rope_norm"]], axis=1), LANES)
    gk128 = _pad_lanes(w["b_k_nope_norm"], LANES)
    gkr128 = _pad_lanes(w["b_k_rope_norm"], LANES)
    tabs = rope_tabs(t)
    bias = bias_block(w["a_rel_bias"])

    if late is None:
        x1 = ffn_fwd(x, w["ffn1_norm"], w["ffn1_w_gate"], w["ffn1_w_up"], w["ffn1_w_down"], "ffn_fwd")
    else:
        own, shard = late
        x1, *got = ffn_fwd(x, w["ffn1_norm"], w["ffn1_w_gate"], w["ffn1_w_up"], w["ffn1_w_down"], "ffn_fwd_gather",
                           gather=own)
        w = dict(w, **kernel_layout({n: lax.dynamic_update_index_in_dim(g_, o_, shard, 0)
                                     for n, g_, o_ in zip(LATE, got, own)}))
    h, proj = mix_proj(x1, w["mix_norm"], w["w_in"])
    qa, ka, va, cqn, ckvn = prep1_fwd(proj, gq, gk, w["b_q_lat_norm"], w["b_kv_lat_norm"])
    qlat = matmul(cqn, w["b_w_uq"], "nt", "uq_fwd", 512, 1024, 256)
    kv = matmul(ckvn, w["b_w_ukv"], "nn", "ukv_fwd", 512, 1024, 128)
    qf, kf, vp = prep2_fwd(qlat, kv, proj, gq128, gk128, gkr128, tabs)
    oa = attn_a_fwd(qa, ka, va, bias)
    ob_t, lse = attn_b_fwd(qf, kf, vp)
    x2 = out_proj(x1, oa, ob_t, w["w_out"])
    x3 = ffn_fwd(x2, w["ffn2_norm"], w["ffn2_w_gate"], w["ffn2_w_up"], w["ffn2_w_down"], "ffn_fwd")

    g = {}
    dx3, g["final_norm"], loss = final_loss(x3, w["final_norm"], target)
    g["ffn2_w_gate"], g["ffn2_w_up"], g["ffn2_w_down"], dhp = ffn_bwd(
        x2, dx3, w["ffn2_norm"], w["ffn2_w_gate"], w["ffn2_w_up"], w["ffn2_w_down"], "ffn_bwd")
    dx2, g["ffn2_norm"] = norm_bwd(x2, w["ffn2_norm"], dhp, dx3, "ffn_norm_bwd")
    d_oa, d_ob, d_ob_t = out_proj_bwd(dx2, w["w_out"])
    g["w_out"] = jnp.concatenate([matmul(oa, dx2, "tn", "w_out_a_bwd", 512, 1024, 512, BF16),
                                  matmul(ob_t, dx2, "nn", "w_out_b_bwd", 512, 1024, 512, BF16)], axis=0)
    early = [g[n] for n in FFN2] + [g["w_out"].reshape(N_SHARD, -1, D_MODEL)]
    dqf, dkf, dvp, *landed_early = attn_b_bwd(qf, kf, vp, d_ob, d_ob_t, ob_t, lse,
                                              scatter=() if late is None else early)
    dqa, dkp, dvpa, dbias = attn_a_bwd(qa, ka, va, bias, d_oa)
    dqlat, dkv, dkr, dgq128, dgk128, dgkr128 = prep2_bwd(qlat, kv, proj, dqf, dkf, dvp, gq128, gk128, gkr128, tabs)
    dcq = matmul(dqlat, w["b_w_uq"], "nn", "uq_bwd_x", 512, 256, 1024)
    g["b_w_uq"] = matmul(dqlat, cqn, "tn", "uq_bwd_w", 1024, 256, 512, BF16)
    dckv = matmul(dkv, w["b_w_ukv"], "nt", "ukv_bwd_x", 512, 128, 1024)
    g["b_w_ukv"] = matmul(ckvn, dkv, "tn", "ukv_bwd_w", 128, 1024, 512, BF16)
    dproj, dgq, dgk, g["b_q_lat_norm"], g["b_kv_lat_norm"] = prep1_bwd(
        proj, dqa, dkp, dvpa, dcq, dckv, dkr, gq, gk, w["b_q_lat_norm"], w["b_kv_lat_norm"])
    dh = matmul(dproj, w["w_in"], "nn", "w_in_bwd_x", 512, 1024, PROJ_W)
    g["w_in"] = matmul(dproj, h, "tn", "w_in_bwd_w", 1024, 1024, 512, BF16)
    dx1, g["mix_norm"] = norm_bwd(x1, w["mix_norm"], dh[None], dx2, "mix_norm_bwd")
    mid = [g["w_in"][:IN_COLS].reshape(N_SHARD, IN_COLS // N_SHARD, D_MODEL),
           g["b_w_uq"].reshape(HEADS, LANES, 256)[:, :HD + ROPE].reshape(N_SHARD, -1, 256),
           _cols_to_shards(g["b_w_ukv"])]
    g["ffn1_w_gate"], g["ffn1_w_up"], g["ffn1_w_down"], dhp, *landed_late = ffn_bwd(
        x, dx1, w["ffn1_norm"], w["ffn1_w_gate"], w["ffn1_w_up"], w["ffn1_w_down"],
        "ffn_bwd" if late is None else "ffn_bwd_scatter", scatter=() if late is None else mid,
        spread=None if late is None else late[1])
    grad_x, g["ffn1_norm"] = norm_bwd(x, w["ffn1_norm"], dhp, dx1, "ffn_norm_bwd")
    landed = dict(zip(FFN2 + ("w_out", "w_in", "b_w_uq", "b_w_ukv", "ffn1_w_gate", "ffn1_w_up", "ffn1_w_down"),
                      landed_early + landed_late))

    g["a_q_norm"] = jnp.sum(dgq.reshape(HEADS, HD), axis=0, keepdims=True)
    g["a_k_norm"] = jnp.sum(dgk.reshape(HEADS, HD), axis=0, keepdims=True)
    g["a_rel_bias"] = bias_block_grad(dbias)
    g["b_q_nope_norm"] = dgq128[:, :HD]
    g["b_q_rope_norm"] = dgq128[:, HD:HD + ROPE]
    g["b_k_nope_norm"] = dgk128[:, :HD]
    g["b_k_rope_norm"] = dgkr128[:, :ROPE]
    return loss, grad_x, g, landed


ANY = pl.BlockSpec(memory_space=pl.ANY)
N_DEV = 8


def _place():
    return lax.axis_index("x"), lax.axis_index("y"), lax.axis_index("c")


def _flip(v, bit):
    return 1 - v if bit else v


BF16_ROWS = 16


def _split_axis(shape):
    return 0 if (shape[0] // 2) % BF16_ROWS == 0 else 1


def _half_shape(shape):
    axis = _split_axis(shape)
    return tuple(s // 2 if a == axis else s for a, s in enumerate(shape))


def _half(shape, core):
    axis = _split_axis(shape)
    size = shape[axis] // 2
    return tuple(pl.ds(core * size, size) if a == axis else slice(None) for a in range(2))


class GatherPlan:
    def __init__(self, ws):
        self.shapes = [w.shape for w in ws]
        self.n = len(ws)
        self.out_shape = [jax.ShapeDtypeStruct((N_SHARD,) + w.shape, w.dtype) for w in ws]
        self.scratch = [pltpu.SemaphoreType.DMA((6 * self.n,)), pltpu.SemaphoreType.DMA((6 * self.n,))]

    def _copies(self, ins, outs, sems):
        x, y, c = _place()
        s_me = 2 * x + y
        sibling = (x, y, 1 - c)
        send_sems, recv_sems = sems

        def remote(k, src, dst, to):
            return pltpu.make_async_remote_copy(src_ref=src, dst_ref=dst, send_sem=send_sems.at[k],
                                                recv_sem=recv_sems.at[k], device_id=to, device_id_type=MESH)

        ici, fwd = [], []
        for a in range(self.n):
            mine, theirs = _half(self.shapes[a], c), _half(self.shapes[a], 1 - c)
            for j, (cx, cy) in enumerate([(1 - x, y), (x, 1 - y), (1 - x, 1 - y)]):
                got = outs[a].at[(2 * cx + cy,) + mine]
                ici.append((remote(6 * a + j, ins[a].at[mine], outs[a].at[(s_me,) + mine], (cx, cy, c)),
                            remote(6 * a + j, got, got, (cx, cy, c))))
                passed = outs[a].at[(2 * cx + cy,) + theirs]
                fwd.append((remote(6 * a + 3 + j, got, got, sibling), remote(6 * a + 3 + j, passed, passed, sibling)))
        return ici, fwd

    def start(self, ins, outs, sems):
        for send, _ in self._copies(ins, outs, sems)[0]:
            send.start()

    def forward(self, ins, outs, sems):
        ici, fwd = self._copies(ins, outs, sems)
        for (_, arrival), (send, _) in zip(ici, fwd):
            arrival.wait_recv()
            send.start()

    def finish(self, ins, outs, sems):
        ici, fwd = self._copies(ins, outs, sems)
        for _, arrival in fwd:
            arrival.wait_recv()
        for send, _ in ici + fwd:
            send.wait_send()


def allgather_shards(ws):
    plan = GatherPlan(ws)
    n = plan.n

    def body(*refs):
        ins, outs, sems = refs[:n], refs[n:2 * n], refs[2 * n:]
        plan.start(ins, outs, sems)
        plan.forward(ins, outs, sems)
        plan.finish(ins, outs, sems)

    return pl.pallas_call(
        body, name="allgather_shards", in_specs=[ANY] * n, out_specs=[ANY] * n,
        out_shape=plan.out_shape, scratch_shapes=plan.scratch,
    )(*ws)


class ScatterPlan:
    def __init__(self, gs):
        self.shapes = [g.shape[1:] for g in gs]
        self.n = len(gs)
        self.out_shape = [jax.ShapeDtypeStruct((N_DEV,) + _half_shape(g.shape[1:]), g.dtype) for g in gs]
        self.scratch = [pltpu.SemaphoreType.DMA((7 * self.n,)), pltpu.SemaphoreType.DMA((7 * self.n,)),
                        pltpu.SemaphoreType.DMA((self.n,))]

    def _copies(self, ins, outs, sems):
        x, y, c = _place()
        me = 4 * x + 2 * y + c
        send_sems, recv_sems, local_sems = sems
        local, sends, arrivals = [], [], []
        for a in range(self.n):
            piece = lambda px, py, pc, a=a: ins[a].at[(2 * px + py,) + _half(self.shapes[a], pc)]
            local.append(pltpu.make_async_copy(piece(x, y, c), outs[a].at[me], local_sems.at[a]))
            for k in range(1, N_DEV):
                px, py, pc = _flip(x, k & 4), _flip(y, k & 2), _flip(c, k & 1)
                sem = dict(send_sem=send_sems.at[7 * a + k - 1], recv_sem=recv_sems.at[7 * a + k - 1],
                           device_id=(px, py, pc), device_id_type=MESH)
                sends.append(pltpu.make_async_remote_copy(
                    src_ref=piece(px, py, pc), dst_ref=outs[a].at[me], **sem))
                slot = outs[a].at[4 * px + 2 * py + pc]
                arrivals.append(pltpu.make_async_remote_copy(src_ref=slot, dst_ref=slot, **sem))
        return local, sends, arrivals

    def start(self, ins, outs, sems):
        local, sends, _ = self._copies(ins, outs, sems)
        for cp in local + sends:
            cp.start()

    def finish(self, ins, outs, sems):
        local, sends, arrivals = self._copies(ins, outs, sems)
        for cp in arrivals:
            cp.wait_recv()
        for cp in sends:
            cp.wait_send()
        for cp in local:
            cp.wait()


def scatter_partials(gs):
    plan = ScatterPlan(gs)
    n = plan.n

    def body(*refs):
        ins, outs, sems = refs[:n], refs[n:2 * n], refs[2 * n:]
        plan.start(ins, outs, sems)
        plan.finish(ins, outs, sems)

    return pl.pallas_call(
        body, name="scatter_partials", in_specs=[ANY] * n, out_specs=[ANY] * n,
        out_shape=plan.out_shape, scratch_shapes=plan.scratch,
    )(*gs)


def sum_slots(land, name):
    _, rows, cols = land.shape
    tr = rows // 2 if rows > 128 and (rows // 2) % BF16_ROWS == 0 else rows

    def body(l_ref, o_ref):
        acc = l_ref[0].astype(F32)
        for s in range(1, N_DEV):
            acc = acc + l_ref[s].astype(F32)
        o_ref[...] = acc

    return pl.pallas_call(
        body, name=name, grid=(rows // tr,),
        in_specs=[pl.BlockSpec((N_DEV, tr, cols), lambda i: (0, i, 0))],
        out_specs=pl.BlockSpec((tr, cols), lambda i: (i, 0)),
        out_shape=jax.ShapeDtypeStruct((rows, cols), F32),
        compiler_params=_params(("parallel",)),
    )(land)


def join_halves(hs, shapes):
    n = len(hs)

    def body(*refs):
        ins, outs = refs[:n], refs[n:2 * n]
        send_sems, recv_sems = refs[2 * n:]
        x, y, c = _place()
        sends = []
        for a in range(n):
            mine = outs[a].at[_half(shapes[a], c)]
            sends.append(pltpu.make_async_remote_copy(
                src_ref=ins[a], dst_ref=mine, send_sem=send_sems.at[a], recv_sem=recv_sems.at[a],
                device_id=(x, y, 1 - c), device_id_type=MESH))
            sends[-1].start()
        for a in range(n):
            theirs = outs[a].at[_half(shapes[a], 1 - c)]
            pltpu.make_async_remote_copy(
                src_ref=theirs, dst_ref=theirs, send_sem=send_sems.at[a], recv_sem=recv_sems.at[a],
                device_id=(x, y, 1 - c), device_id_type=MESH).wait_recv()
        for cp in sends:
            cp.wait_send()

    return pl.pallas_call(
        body, name="join_halves",
        in_specs=[ANY] * n, out_specs=[ANY] * n,
        out_shape=[jax.ShapeDtypeStruct(tuple(s), h.dtype) for s, h in zip(shapes, hs)],
        scratch_shapes=[pltpu.SemaphoreType.DMA((n,)), pltpu.SemaphoreType.DMA((n,))],
    )(*hs)


def allreduce_small(vec):
    def body(v_ref, o_ref, land_ref, send_sems, recv_sems):
        x, y, c = _place()
        me = 4 * x + 2 * y + c
        land_ref[me] = v_ref[...]
        sends = []
        for k in range(1, N_DEV):
            px, py, pc = _flip(x, k & 4), _flip(y, k & 2), _flip(c, k & 1)
            sends.append(pltpu.make_async_remote_copy(
                src_ref=v_ref, dst_ref=land_ref.at[me], send_sem=send_sems.at[k - 1], recv_sem=recv_sems.at[k - 1],
                device_id=(px, py, pc), device_id_type=MESH))
            sends[-1].start()
        for k in range(1, N_DEV):
            px, py, pc = _flip(x, k & 4), _flip(y, k & 2), _flip(c, k & 1)
            slot = land_ref.at[4 * px + 2 * py + pc]
            pltpu.make_async_remote_copy(
                src_ref=slot, dst_ref=slot, send_sem=send_sems.at[k - 1], recv_sem=recv_sems.at[k - 1],
                device_id=(px, py, pc), device_id_type=MESH).wait_recv()
        for cp in sends:
            cp.wait_send()
        acc = land_ref[0]
        for s in range(1, N_DEV):
            acc = acc + land_ref[s]
        o_ref[...] = acc

    vm = pl.BlockSpec(memory_space=pltpu.VMEM)
    return pl.pallas_call(
        body, name="allreduce_small",
        in_specs=[vm], out_specs=vm,
        out_shape=jax.ShapeDtypeStruct(vec.shape, F32),
        scratch_shapes=[pltpu.VMEM((N_DEV,) + vec.shape, F32), pltpu.SemaphoreType.DMA((N_DEV - 1,)),
                        pltpu.SemaphoreType.DMA((N_DEV - 1,))],
    )(vec)


def adamw(w, g, m, v, name):
    rows, cols = w.shape
    tr = rows
    while tr * cols * 4 * 14 > 24 * 1024 * 1024 and tr % 16 == 0:
        tr //= 2
    c1 = 1.0 - ADAM_B1 ** ADAM_STEP
    c2 = 1.0 - ADAM_B2 ** ADAM_STEP

    def body(w_ref, g_ref, m_ref, v_ref, d_ref, nm_ref, nv_ref):
        gv = g_ref[...]
        nm = ADAM_B1 * m_ref[...] + (1.0 - ADAM_B1) * gv
        nv = ADAM_B2 * v_ref[...] + (1.0 - ADAM_B2) * (gv * gv)
        nm_ref[...] = nm
        nv_ref[...] = nv
        d_ref[...] = -ADAM_LR * ((nm / c1) / (jnp.sqrt(nv / c2) + ADAM_EPS) + ADAM_WD * w_ref[...])

    blk = pl.BlockSpec((tr, cols), lambda i: (i, 0))
    return pl.pallas_call(
        body, name=name, grid=(rows // tr,),
        in_specs=[blk] * 4, out_specs=[blk] * 3,
        out_shape=[jax.ShapeDtypeStruct((rows, cols), F32)] * 3,
        compiler_params=_params(("parallel",)),
    )(w, g, m, v)


BIG = ("ffn1_w_gate", "ffn1_w_up", "ffn1_w_down", "w_in", "b_w_uq", "b_w_ukv", "w_out",
       "ffn2_w_gate", "ffn2_w_up", "ffn2_w_down")
SMALL = ("ffn1_norm", "mix_norm", "a_q_norm", "a_k_norm", "a_rel_bias", "b_q_lat_norm", "b_kv_lat_norm",
         "b_q_nope_norm", "b_q_rope_norm", "b_k_nope_norm", "b_k_rope_norm", "ffn2_norm", "final_norm")
WEIGHTS = ("ffn1_norm", "ffn1_w_gate", "ffn1_w_up", "ffn1_w_down", "mix_norm", "w_in", "a_q_norm", "a_k_norm",
           "a_rel_bias", "b_q_lat_norm", "b_w_uq", "b_kv_lat_norm", "b_w_ukv", "b_q_nope_norm", "b_q_rope_norm",
           "b_k_nope_norm", "b_k_rope_norm", "w_out", "ffn2_norm", "ffn2_w_gate", "ffn2_w_up", "ffn2_w_down",
           "final_norm")
TRANSPOSED = ("ffn1_w_gate", "ffn1_w_up", "ffn2_w_gate", "ffn2_w_up", "w_in", "b_w_uq")
PACK_SHAPE = (8, 1024)


def _pack_small(d, last=None):
    flat = [d[n].reshape(-1) for n in SMALL]
    used = sum(f.shape[0] for f in flat)
    total = PACK_SHAPE[0] * PACK_SHAPE[1]
    tail = jnp.zeros((total - used - 1,), F32)
    end = jnp.zeros((1,), F32) if last is None else last.reshape(1)
    return jnp.concatenate(flat + [tail, end]).reshape(PACK_SHAPE)


def _unpack_small(p, like):
    flat = p.reshape(-1)
    out, off = {}, 0
    for n in SMALL:
        size = like[n].size
        out[n] = flat[off:off + size].reshape(like[n].shape)
        off += size
    return out, flat[-1]


def _cols_to_shards(g):
    rows, cols = g.shape
    return g.reshape(rows, N_SHARD, cols // N_SHARD).transpose(1, 0, 2)


def _shards_to_cols(g):
    return g.transpose(1, 0, 2).reshape(g.shape[1], -1)


def kernel(x, ffn1_norm, ffn1_w_gate, ffn1_w_up, ffn1_w_down, mix_norm, w_in, a_q_norm, a_k_norm, a_rel_bias, b_q_lat_norm, b_w_uq, b_kv_lat_norm, b_w_ukv, b_q_nope_norm, b_q_rope_norm, b_k_nope_norm, b_k_rope_norm, w_out, ffn2_norm, ffn2_w_gate, ffn2_w_up, ffn2_w_down, final_norm, loss_target, m_ffn1_norm, m_ffn1_w_gate, m_ffn1_w_up, m_ffn1_w_down, m_mix_norm, m_w_in, m_a_q_norm, m_a_k_norm, m_a_rel_bias, m_b_q_lat_norm, m_b_w_uq, m_b_kv_lat_norm, m_b_w_ukv, m_b_q_nope_norm, m_b_q_rope_norm, m_b_k_nope_norm, m_b_k_rope_norm, m_w_out, m_ffn2_norm, m_ffn2_w_gate, m_ffn2_w_up, m_ffn2_w_down, m_final_norm, v_ffn1_norm, v_ffn1_w_gate, v_ffn1_w_up, v_ffn1_w_down, v_mix_norm, v_w_in, v_a_q_norm, v_a_k_norm, v_a_rel_bias, v_b_q_lat_norm, v_b_w_uq, v_b_kv_lat_norm, v_b_w_ukv, v_b_q_nope_norm, v_b_q_rope_norm, v_b_k_nope_norm, v_b_k_rope_norm, v_w_out, v_ffn2_norm, v_ffn2_w_gate, v_ffn2_w_up, v_ffn2_w_down, v_final_norm):
    args = locals()
    view = lambda a, n: a[0].T if n in TRANSPOSED else a[0]
    wts = {n: view(args[n], n) for n in WEIGHTS}
    mom = {n: view(args["m_" + n], n) for n in WEIGHTS}
    var = {n: view(args["v_" + n], n) for n in WEIGHTS}

    shard = 2 * lax.axis_index("x") + lax.axis_index("y")
    core = lax.axis_index("c")
    first = [n for n in BIG if n not in LATE]
    own = [wts[n].astype(BF16) for n in first]
    w = {n: wts[n] if n == "a_rel_bias" else wts[n][None] for n in SMALL}
    w.update(kernel_layout({n: lax.dynamic_update_index_in_dim(got, mine, shard, 0)
                            for n, got, mine in zip(first, allgather_shards(own), own)}))

    loss, grad_x, g, landed = local_step(x[0], loss_target[0], w,
                                         late=([wts[n].astype(BF16) for n in LATE], shard))

    me = 2 * shard + core
    for n in first:
        piece = lax.dynamic_slice(g[n], (shard, core * (FS // 2), 0), (1, FS // 2, D_MODEL))
        landed[n] = lax.dynamic_update_slice(landed[n], piece, (me, 0, 0))
    halves = [sum_slots(landed[n], "sum_slots") for n in BIG]
    shapes = [wts[n].shape for n in BIG]
    axes = [_split_axis(s) for s in shapes]
    grads = dict(zip(BIG, (lax.dynamic_update_slice_in_dim(got, mine, core * mine.shape[ax], ax)
                           for got, mine, ax in zip(join_halves(halves, shapes), halves, axes))))

    small_sum, loss_sum = _unpack_small(allreduce_small(_pack_small(g, loss[0, 0])), wts)
    grads.update(small_sum)

    delta, new_m, new_v = {}, {}, {}
    for n in BIG:
        delta[n], new_m[n], new_v[n] = adamw(wts[n], grads[n], mom[n], var[n], "adamw")
    packed = adamw(_pack_small(wts), _pack_small(grads), _pack_small(mom), _pack_small(var), "adamw_small")
    for dst, p in zip((delta, new_m, new_v), packed):
        dst.update(_unpack_small(p, wts)[0])

    lead = lambda d: [(d[n].T if n in TRANSPOSED else d[n])[None] for n in WEIGHTS]
    return (loss_sum, grad_x[None], *lead(grads), *lead(delta), *lead(new_m), *lead(new_v))
```

```python
import functools
import math

import numpy as np
import jax
import jax.numpy as jnp
from jax import lax
from jax.experimental import pallas as pl
from jax.experimental.pallas import tpu as pltpu

F32 = jnp.float32
BF16 = jnp.bfloat16
EPS = 1e-6
NEG = -1e30

D_MODEL = 1024
D_FF = 2816
N_SHARD = 4
FS = D_FF // N_SHARD
CHUNK = 64
A_LEFT = 8
A_MAX_REL = 128
HEADS = 8
HD = 64
ROPE = 32
PROJ_W = 2048
IN_COLS = 1952
B_SCALE = 96 ** -0.5
LANES = 128

ADAM_LR = 0.001
ADAM_B1 = 0.9
ADAM_B2 = 0.999
ADAM_EPS = 1e-08
ADAM_WD = 0.01
ADAM_STEP = 10

VMEM_LIMIT = 56 * 1024 * 1024

MESH = pl.DeviceIdType.MESH


def _dot(a, b):
    return lax.dot_general(a, b, (((1,), (0,)), ((), ())), preferred_element_type=F32)


def _dot_nt(a, b):
    return lax.dot_general(a, b, (((1,), (1,)), ((), ())), preferred_element_type=F32)


def _dot_tn(a, b):
    return lax.dot_general(a, b, (((0,), (0,)), ((), ())), preferred_element_type=F32)


def _params(sem):
    return pltpu.CompilerParams(dimension_semantics=sem, vmem_limit_bytes=VMEM_LIMIT)


def _rms(xv):
    r = lax.rsqrt(jnp.mean(xv * xv, axis=-1, keepdims=True) + EPS)
    return r, xv * r


def ffn_fwd(x, g, wg, wu, wd, name, gather=()):
    t, d = x.shape
    tm = 512
    ni = t // tm
    plan = GatherPlan(gather)
    n = plan.n

    def body(*refs):
        x_ref, g_ref, wg_ref, wu_ref, wd_ref = refs[:5]
        ins, o_ref, outs = refs[5:5 + n], refs[5 + n], refs[6 + n:6 + 2 * n]
        h_ref, acc_ref = refs[6 + 2 * n:8 + 2 * n]
        sems = refs[8 + 2 * n:]
        i, j = pl.program_id(0), pl.program_id(1)
        if n:
            pl.when((i == 0) & (j == 0))(lambda: plan.start(ins, outs, sems))
            pl.when((i == (3 * ni) // 4) & (j == 0))(lambda: plan.forward(ins, outs, sems))

        @pl.when(j == 0)
        def _():
            _, xn = _rms(x_ref[...])
            h_ref[...] = (xn * g_ref[...]).astype(BF16)
            acc_ref[...] = jnp.zeros_like(acc_ref)

        h = h_ref[...]
        gp = _dot_nt(h, wg_ref[0])
        up = _dot_nt(h, wu_ref[0])
        a = (gp * jax.nn.sigmoid(gp) * up).astype(BF16)
        acc_ref[...] += _dot(a, wd_ref[0])

        @pl.when(j == N_SHARD - 1)
        def _():
            o_ref[...] = x_ref[...] + 0.5 * acc_ref[...]

        if n:
            pl.when((i == ni - 1) & (j == N_SHARD - 1))(lambda: plan.finish(ins, outs, sems))

    res = pl.pallas_call(
        body, name=name, grid=(ni, N_SHARD),
        in_specs=[pl.BlockSpec((tm, d), lambda i, j: (i, 0)),
                  pl.BlockSpec((1, d), lambda i, j: (0, 0)),
                  pl.BlockSpec((1, FS, d), lambda i, j: (j, 0, 0)),
                  pl.BlockSpec((1, FS, d), lambda i, j: (j, 0, 0)),
                  pl.BlockSpec((1, FS, d), lambda i, j: (j, 0, 0))] + [ANY] * n,
        out_specs=[pl.BlockSpec((tm, d), lambda i, j: (i, 0))] + [ANY] * n,
        out_shape=[jax.ShapeDtypeStruct((t, d), F32)] + plan.out_shape,
        scratch_shapes=[pltpu.VMEM((tm, d), BF16), pltpu.VMEM((tm, d), F32)] + (plan.scratch if n else []),
        compiler_params=_params(("arbitrary", "arbitrary")),
    )(x, g, wg, wu, wd, *gather)
    return res if n else res[0]


def ffn_bwd(x, dout, g, wg, wu, wd, name, scatter=(), spread=None):
    t, d = x.shape
    tm = 512
    ni = t // tm
    hf = FS // 2
    plan = ScatterPlan(scatter)
    m = plan.n
    k = 0 if spread is None else 3
    steps = jnp.arange(N_SHARD, dtype=jnp.int32)
    order = steps if spread is None else (spread + 1 + steps) % N_SHARD

    def body(*refs):
        ord_ref, x_ref, do_ref, g_ref, wg_ref, wu_ref, wd_ref = refs[:7]
        ins, (dwg_out, dwu_out, dwd_out, dhp_ref) = refs[7:7 + m], refs[7 + m:11 + m]
        outs, lands = refs[11 + m:11 + 2 * m], refs[11 + 2 * m:11 + 2 * m + k]
        dwg_ref, dwu_ref, dwd_ref = refs[11 + 2 * m + k:14 + 2 * m + k]
        sems = refs[14 + 2 * m + k:17 + 2 * m + k] if m else ()
        stage_ref = refs[-3] if k else None
        j, i = pl.program_id(0), pl.program_id(1)
        if m:
            pl.when((j == 0) & (i == 0))(lambda: plan.start(ins, outs, sems))

        def chunk_copies(jj):
            send_sems, recv_sems = refs[-2:]
            px, py, pc = _place()
            me = 4 * px + 2 * py + pc
            tx, ty = ord_ref[jj] // 2, ord_ref[jj] % 2
            copies = []
            for n_ in range(3):
                for h_ in range(2):
                    copies.append((pltpu.make_async_remote_copy(
                        src_ref=stage_ref.at[n_, pl.ds(h_ * hf, hf)], dst_ref=lands[n_].at[me],
                        send_sem=send_sems.at[6 * jj + 2 * n_ + h_], recv_sem=recv_sems.at[3 * me + n_],
                        device_id=(tx, ty, h_), device_id_type=MESH), (tx != px) | (ty != py) | (pc != h_)))
            return copies

        def arrivals():
            send_sems, recv_sems = refs[-2:]
            px, py, pc = _place()
            me = 4 * px + 2 * py + pc
            for s_ in range(N_DEV):
                for n_ in range(3):
                    slot = lands[n_].at[s_]
                    cp = pltpu.make_async_remote_copy(
                        src_ref=slot, dst_ref=slot, send_sem=send_sems.at[0], recv_sem=recv_sems.at[3 * s_ + n_],
                        device_id=(px, py, pc), device_id_type=MESH)
                    pl.when(me != s_)(cp.wait_recv)

        _, xn = _rms(x_ref[...])
        h = (xn * g_ref[...]).astype(BF16)
        dz = (0.5 * do_ref[...]).astype(BF16)
        wgv, wuv, wdv = wg_ref[0], wu_ref[0], wd_ref[0]
        gp = _dot_nt(h, wgv)
        up = _dot_nt(h, wuv)
        s = jax.nn.sigmoid(gp)
        sg = gp * s
        a = (sg * up).astype(BF16)
        da = _dot_nt(dz, wdv)
        dup = (da * sg).astype(BF16)
        dgp = (da * up * (s * (1.0 + gp * (1.0 - s)))).astype(BF16)

        @pl.when(i == 0)
        def _():
            dwg_ref[...] = jnp.zeros_like(dwg_ref)
            dwu_ref[...] = jnp.zeros_like(dwu_ref)
            dwd_ref[...] = jnp.zeros_like(dwd_ref)

        dwd_ref[...] += _dot_tn(a, dz)
        dwg_ref[...] += _dot_tn(dgp, h)
        dwu_ref[...] += _dot_tn(dup, h)
        dhp_ref[0] = (_dot(dgp, wgv) + _dot(dup, wuv)).astype(BF16)

        @pl.when(i == ni - 1)
        def _():
            dwg_out[0] = dwg_ref[...].astype(BF16)
            dwu_out[0] = dwu_ref[...].astype(BF16)
            dwd_out[0] = dwd_ref[...].astype(BF16)
            if k:
                @pl.when(j >= 1)
                def _():
                    for cp, leaves in chunk_copies(j - 1):
                        pl.when(leaves)(cp.wait_send)
                for n_, acc in enumerate((dwg_ref, dwu_ref, dwd_ref)):
                    stage_ref[n_] = acc[...].astype(BF16)
                for cp, leaves in chunk_copies(j):
                    pl.when(leaves)(cp.start)

                @pl.when(j == N_SHARD - 1)
                def _():
                    for cp, leaves in chunk_copies(N_SHARD - 1):
                        pl.when(leaves)(cp.wait_send)
                    arrivals()

        if m:
            pl.when((j == N_SHARD - 1) & (i == ni - 1))(lambda: plan.finish(ins, outs, sems))

    chunk = pl.BlockSpec((1, FS, d), lambda j, i, o: (o[j], 0, 0))
    tok = pl.BlockSpec((tm, d), lambda j, i, o: (i, 0))
    grid_spec = pltpu.PrefetchScalarGridSpec(
        num_scalar_prefetch=1, grid=(N_SHARD, ni),
        in_specs=[tok, tok, pl.BlockSpec((1, d), lambda j, i, o: (0, 0)), chunk, chunk, chunk] + [ANY] * m,
        out_specs=[chunk, chunk, chunk, pl.BlockSpec((1, tm, d), lambda j, i, o: (o[j], i, 0))] + [ANY] * (m + k),
        scratch_shapes=[pltpu.VMEM((FS, d), F32), pltpu.VMEM((FS, d), F32), pltpu.VMEM((FS, d), F32)]
        + (plan.scratch if m else [])
        + ([pltpu.VMEM((3, FS, d), BF16), pltpu.SemaphoreType.DMA((6 * N_SHARD,)),
            pltpu.SemaphoreType.DMA((3 * N_DEV,))] if k else []))
    return pl.pallas_call(
        body, name=name, grid_spec=grid_spec,
        out_shape=[jax.ShapeDtypeStruct((N_SHARD, FS, d), BF16),
                   jax.ShapeDtypeStruct((N_SHARD, FS, d), BF16),
                   jax.ShapeDtypeStruct((N_SHARD, FS, d), BF16),
                   jax.ShapeDtypeStruct((N_SHARD, t, d), BF16)] + plan.out_shape
        + [jax.ShapeDtypeStruct((N_DEV, hf, d), BF16)] * k,
        compiler_params=_params(("arbitrary", "arbitrary")),
    )(order, x, dout, g, wg, wu, wd, *scatter)


def norm_bwd(x, g, dhp, dres, name):
    t, d = x.shape
    p = dhp.shape[0]
    tm = 512

    def body(x_ref, g_ref, dhp_ref, dres_ref, dx_ref, dg_ref):
        i = pl.program_id(0)
        r, xn = _rms(x_ref[...])
        dh = dhp_ref[0].astype(F32)
        for q in range(1, p):
            dh = dh + dhp_ref[q].astype(F32)
        dhg = dh * g_ref[...]
        dx_ref[...] = dres_ref[...] + r * (dhg - xn * jnp.mean(dhg * xn, axis=-1, keepdims=True))

        @pl.when(i == 0)
        def _():
            dg_ref[...] = jnp.zeros_like(dg_ref)

        dg_ref[...] += jnp.sum(dh * xn, axis=0, keepdims=True)

    return pl.pallas_call(
        body, name=name, grid=(t // tm,),
        in_specs=[pl.BlockSpec((tm, d), lambda i: (i, 0)),
                  pl.BlockSpec((1, d), lambda i: (0, 0)),
                  pl.BlockSpec((p, tm, d), lambda i: (0, i, 0)),
                  pl.BlockSpec((tm, d), lambda i: (i, 0))],
        out_specs=[pl.BlockSpec((tm, d), lambda i: (i, 0)),
                   pl.BlockSpec((1, d), lambda i: (0, 0))],
        out_shape=[jax.ShapeDtypeStruct((t, d), F32), jax.ShapeDtypeStruct((1, d), F32)],
        compiler_params=_params(("arbitrary",)),
    )(x, g, dhp, dres)


def final_loss(x, g, target):
    t, d = x.shape
    tm = 512

    def body(x_ref, g_ref, t_ref, dx_ref, dg_ref, loss_ref):
        i = pl.program_id(0)
        r, xn = _rms(x_ref[...])
        gv = g_ref[...]
        e = xn * gv - t_ref[...]
        dy = e * (1.0 / d)
        dhg = dy * gv
        dx_ref[...] = r * (dhg - xn * jnp.mean(dhg * xn, axis=-1, keepdims=True))

        @pl.when(i == 0)
        def _():
            dg_ref[...] = jnp.zeros_like(dg_ref)
            loss_ref[...] = jnp.zeros_like(loss_ref)

        dg_ref[...] += jnp.sum(dy * xn, axis=0, keepdims=True)
        part = jnp.sum(jnp.sum(e * e, axis=-1, keepdims=True), axis=0, keepdims=True) * (0.5 / d)
        loss_ref[...] += jnp.broadcast_to(part, loss_ref.shape)

    return pl.pallas_call(
        body, name="final_loss", grid=(t // tm,),
        in_specs=[pl.BlockSpec((tm, d), lambda i: (i, 0)),
                  pl.BlockSpec((1, d), lambda i: (0, 0)),
                  pl.BlockSpec((tm, d), lambda i: (i, 0))],
        out_specs=[pl.BlockSpec((tm, d), lambda i: (i, 0)),
                   pl.BlockSpec((1, d), lambda i: (0, 0)),
                   pl.BlockSpec((1, LANES), lambda i: (0, 0))],
        out_shape=[jax.ShapeDtypeStruct((t, d), F32), jax.ShapeDtypeStruct((1, d), F32),
                   jax.ShapeDtypeStruct((1, LANES), F32)],
        compiler_params=_params(("arbitrary",)),
    )(x, g, target)


def mix_proj(x, g, w):
    t, d = x.shape
    n = w.shape[0]
    tm = 512

    def body(x_ref, g_ref, w_ref, h_ref, p_ref):
        _, xn = _rms(x_ref[...])
        h = (xn * g_ref[...]).astype(BF16)
        h_ref[...] = h
        p_ref[...] = _dot_nt(h, w_ref[...])

    return pl.pallas_call(
        body, name="mix_proj", grid=(t // tm,),
        in_specs=[pl.BlockSpec((tm, d), lambda i: (i, 0)),
                  pl.BlockSpec((1, d), lambda i: (0, 0)),
                  pl.BlockSpec((n, d), lambda i: (0, 0))],
        out_specs=[pl.BlockSpec((tm, d), lambda i: (i, 0)),
                   pl.BlockSpec((tm, n), lambda i: (i, 0))],
        out_shape=[jax.ShapeDtypeStruct((t, d), BF16), jax.ShapeDtypeStruct((t, n), F32)],
        compiler_params=_params(("parallel",)),
    )(x, g, w)


def matmul(a, b, mode, name, tm, tn, tk, out_dtype=F32):
    if mode == "nn":
        (m, k), n = a.shape, b.shape[1]
        a_spec = pl.BlockSpec((tm, tk), lambda i, j, q: (i, q))
        b_spec = pl.BlockSpec((tk, tn), lambda i, j, q: (q, j))
        dot = _dot
    elif mode == "nt":
        (m, k), n = a.shape, b.shape[0]
        a_spec = pl.BlockSpec((tm, tk), lambda i, j, q: (i, q))
        b_spec = pl.BlockSpec((tn, tk), lambda i, j, q: (j, q))
        dot = _dot_nt
    else:
        (k, m), n = a.shape, b.shape[1]
        a_spec = pl.BlockSpec((tk, tm), lambda i, j, q: (q, i))
        b_spec = pl.BlockSpec((tk, tn), lambda i, j, q: (q, j))
        dot = _dot_tn
    assert m % tm == 0 and n % tn == 0 and k % tk == 0, (m, n, k, tm, tn, tk)
    nk = k // tk

    def body(a_ref, b_ref, o_ref, acc_ref):
        q = pl.program_id(2)

        @pl.when(q == 0)
        def _():
            acc_ref[...] = jnp.zeros_like(acc_ref)

        acc_ref[...] += dot(a_ref[...].astype(BF16), b_ref[...].astype(BF16))

        @pl.when(q == nk - 1)
        def _():
            o_ref[...] = acc_ref[...].astype(out_dtype)

    return pl.pallas_call(
        body, name=name, grid=(m // tm, n // tn, nk),
        in_specs=[a_spec, b_spec],
        out_specs=pl.BlockSpec((tm, tn), lambda i, j, q: (i, j)),
        out_shape=jax.ShapeDtypeStruct((m, n), out_dtype),
        scratch_shapes=[pltpu.VMEM((tm, tn), F32)],
        compiler_params=_params(("parallel", "parallel", "arbitrary")),
    )(a, b)


def out_proj(x, oa, ob_t, w):
    t, d = x.shape
    half = oa.shape[1]
    tm = 512

    def body(x_ref, oa_ref, obt_ref, w_ref, o_ref):
        o_ref[...] = (x_ref[...] + _dot(oa_ref[...], w_ref[0:half, :])
                      + _dot_tn(obt_ref[...], w_ref[half:2 * half, :]))

    return pl.pallas_call(
        body, name="out_proj", grid=(t // tm,),
        in_specs=[pl.BlockSpec((tm, d), lambda i: (i, 0)),
                  pl.BlockSpec((tm, half), lambda i: (i, 0)),
                  pl.BlockSpec((half, tm), lambda i: (0, i)),
                  pl.BlockSpec((2 * half, d), lambda i: (0, 0))],
        out_specs=pl.BlockSpec((tm, d), lambda i: (i, 0)),
        out_shape=jax.ShapeDtypeStruct((t, d), F32),
        compiler_params=_params(("parallel",)),
    )(x, oa, ob_t, w)


def out_proj_bwd(dx, w):
    t, d = dx.shape
    half = w.shape[0] // 2
    tm = 512

    def body(dx_ref, w_ref, da_ref, db_ref, dbt_ref):
        dxb = dx_ref[...].astype(BF16)
        da_ref[...] = _dot_nt(dxb, w_ref[0:half, :]).astype(BF16)
        db_ref[...] = _dot_nt(dxb, w_ref[half:2 * half, :]).astype(BF16)
        dbt_ref[...] = _dot_nt(w_ref[half:2 * half, :], dxb).astype(BF16)

    return pl.pallas_call(
        body, name="out_proj_bwd", grid=(t // tm,),
        in_specs=[pl.BlockSpec((tm, d), lambda i: (i, 0)),
                  pl.BlockSpec((2 * half, d), lambda i: (0, 0))],
        out_specs=[pl.BlockSpec((tm, half), lambda i: (i, 0)),
                   pl.BlockSpec((tm, half), lambda i: (i, 0)),
                   pl.BlockSpec((half, tm), lambda i: (0, i))],
        out_shape=[jax.ShapeDtypeStruct((t, half), BF16), jax.ShapeDtypeStruct((t, half), BF16),
                   jax.ShapeDtypeStruct((half, t), BF16)],
        compiler_params=_params(("parallel",)),
    )(dx, w)


def _lane(shape):
    return lax.broadcasted_iota(jnp.int32, shape, 1)


def _seg_sum(z, mask):
    return jnp.sum(jnp.where(mask, z, 0.0), axis=-1, keepdims=True)


def _pair_norm(x):
    lo = _lane(x.shape) < HD
    x2 = x * x
    r = jnp.where(lo, lax.rsqrt(_seg_sum(x2, lo) * (1.0 / HD) + EPS),
                  lax.rsqrt(_seg_sum(x2, ~lo) * (1.0 / HD) + EPS))
    return lo, r, x * r


def _pair_norm_bwd(lo, r, xn, dyg):
    z = dyg * xn
    mean = jnp.where(lo, _seg_sum(z, lo), _seg_sum(z, ~lo)) * (1.0 / HD)
    return r * (dyg - xn * mean)


A_TM = 256


def prep1_fwd(proj, gq, gk, gcq, gckv):
    t = proj.shape[0]
    tm = A_TM

    def body(p_ref, gq_ref, gk_ref, gcq_ref, gckv_ref, qa_ref, ka_ref, va_ref, cq_ref, ckv_ref):
        for p in range(4):
            sl = slice(LANES * p, LANES * (p + 1))
            _, _, xn = _pair_norm(p_ref[:, sl])
            qa_ref[:, sl] = (xn * gq_ref[:, sl] * 0.125).astype(BF16)
            _, _, xn = _pair_norm(p_ref[:, 512 + LANES * p:512 + LANES * (p + 1)])
            ka_ref[:, sl] = (xn * gk_ref[:, sl]).astype(BF16)
        va_ref[...] = p_ref[:, 1024:1536].astype(BF16)
        _, xn = _rms(p_ref[:, 1536:1792])
        cq_ref[...] = (xn * gcq_ref[...]).astype(BF16)
        _, xn = _rms(p_ref[:, 1792:1920])
        ckv_ref[...] = (xn * gckv_ref[...]).astype(BF16)

    row = lambda w: pl.BlockSpec((tm, w), lambda i: (i, 0))
    vec = lambda w: pl.BlockSpec((1, w), lambda i: (0, 0))
    return pl.pallas_call(
        body, name="prep1_fwd", grid=(t // tm,),
        in_specs=[row(PROJ_W), vec(512), vec(512), vec(256), vec(128)],
        out_specs=[row(512), row(512), row(512), row(256), row(128)],
        out_shape=[jax.ShapeDtypeStruct((t, w), BF16) for w in (512, 512, 512, 256, 128)],
        compiler_params=_params(("parallel",)),
    )(proj, gq, gk, gcq, gckv)


def prep1_bwd(proj, dqa, dkp, dvp, dcq, dckv, dkr, gq, gk, gcq, gckv):
    t = proj.shape[0]
    tm = A_TM
    nb = t // tm

    def body(p_ref, dqa_ref, dk0_ref, dk1_ref, dk2_ref, dv0_ref, dv1_ref, dv2_ref, dcq_ref, dckv_ref, dkr_ref,
             gq_ref, gk_ref, gcq_ref, gckv_ref, dp_ref, dgq_ref, dgk_ref, dgcq_ref, dgckv_ref):
        i = pl.program_id(0)

        @pl.when(i == 0)
        def _():
            dgq_ref[...] = jnp.zeros_like(dgq_ref)
            dgk_ref[...] = jnp.zeros_like(dgk_ref)
            dgcq_ref[...] = jnp.zeros_like(dgcq_ref)
            dgckv_ref[...] = jnp.zeros_like(dgckv_ref)

        has1 = (i + 1 < nb).astype(F32)
        has2 = (i + 2 < nb).astype(F32)
        for p in range(4):
            sl = slice(LANES * p, LANES * (p + 1))
            lo, r, xn = _pair_norm(p_ref[:, sl])
            dy = dqa_ref[:, sl] * 0.125
            dp_ref[:, sl] = _pair_norm_bwd(lo, r, xn, dy * gq_ref[:, sl]).astype(BF16)
            dgq_ref[:, sl] += jnp.sum(dy * xn, axis=0, keepdims=True)
            ks = slice(512 + LANES * p, 512 + LANES * (p + 1))
            lo, r, xn = _pair_norm(p_ref[:, ks])
            dy = dk0_ref[0, :, sl] + has1 * dk1_ref[0, :, sl] + has2 * dk2_ref[0, :, sl]
            dp_ref[:, ks] = _pair_norm_bwd(lo, r, xn, dy * gk_ref[:, sl]).astype(BF16)
            dgk_ref[:, sl] += jnp.sum(dy * xn, axis=0, keepdims=True)
        dp_ref[:, 1024:1536] = (dv0_ref[0] + has1 * dv1_ref[0] + has2 * dv2_ref[0]).astype(BF16)
        for (a, b, d_ref, g_ref, dg_ref) in ((1536, 1792, dcq_ref, gcq_ref, dgcq_ref),
                                             (1792, 1920, dckv_ref, gckv_ref, dgckv_ref)):
            r, xn = _rms(p_ref[:, a:b])
            dy = d_ref[...]
            dyg = dy * g_ref[...]
            dp_ref[:, a:b] = (r * (dyg - xn * jnp.mean(dyg * xn, axis=-1, keepdims=True))).astype(BF16)
            dg_ref[...] += jnp.sum(dy * xn, axis=0, keepdims=True)
        dp_ref[:, 1920:2048] = dkr_ref[...].astype(BF16)

    row = lambda w: pl.BlockSpec((tm, w), lambda i: (i, 0))
    vec = lambda w: pl.BlockSpec((1, w), lambda i: (0, 0))
    part = lambda s: pl.BlockSpec((1, tm, 512), lambda i: (s, jnp.minimum(i + s, nb - 1), 0))
    return pl.pallas_call(
        body, name="prep1_bwd", grid=(nb,),
        in_specs=[row(PROJ_W), row(512), part(0), part(1), part(2), part(0), part(1), part(2),
                  row(256), row(128), row(128), vec(512), vec(512), vec(256), vec(128)],
        out_specs=[row(PROJ_W), vec(512), vec(512), vec(256), vec(128)],
        out_shape=[jax.ShapeDtypeStruct((t, PROJ_W), BF16)] + [jax.ShapeDtypeStruct((1, w), F32) for w in (512, 512, 256, 128)],
        compiler_params=_params(("arbitrary",)),
    )(proj, dqa, dkp, dkp, dkp, dvp, dvp, dvp, dcq, dckv, dkr, gq, gk, gcq, gckv)


def _roll(x, shift):
    return pltpu.roll(x, shift % LANES, 1)


def _rope(y, c, s1, s2):
    return y * c + _roll(y, -16) * s1 + _roll(y, 16) * s2


def _rope_bwd(d, c, s1, s2):
    return d * c + _roll(d * s1, 16) + _roll(d * s2, -16)


def _q_head_stats(x):
    lane = _lane(x.shape)
    mn = lane < HD
    mr = (lane >= HD) & (lane < HD + ROPE)
    x2 = x * x
    r = jnp.where(mn, lax.rsqrt(_seg_sum(x2, mn) * (1.0 / HD) + EPS),
                  lax.rsqrt(_seg_sum(x2, mr) * (1.0 / ROPE) + EPS))
    return mn, mr, r, x * r


def _kr_stats(x):
    r = lax.rsqrt(jnp.sum(x * x, axis=-1, keepdims=True) * (1.0 / ROPE) + EPS)
    return r, x * r


def _up_proj(cq_ref, ckv_ref, wuq_ref, wukv_ref):
    return _dot_nt(cq_ref[...], wuq_ref[...]), _dot(ckv_ref[...], wukv_ref[...])


def prep2_fwd(cqn, ckvn, proj, wuq, wukv, gq, gk, gkr, tabs):
    t = cqn.shape[0]
    tm = A_TM

    def body(cq_ref, ckv_ref, kr_ref, wuq_ref, wukv_ref, gq_ref, gk_ref, gkr_ref, tab_ref, qf_ref, kf_ref, vp_ref):
        q_all, kv_all = _up_proj(cq_ref, ckv_ref, wuq_ref, wukv_ref)
        _, xn = _kr_stats(kr_ref[...])
        kpe = _roll(_rope(xn * gkr_ref[...], tab_ref[3], tab_ref[4], tab_ref[5]), 64)
        for h in range(HEADS):
            sl = slice(LANES * h, LANES * (h + 1))
            _, _, _, xn = _q_head_stats(q_all[:, sl])
            qf_ref[:, sl] = _rope(xn * gq_ref[...], tab_ref[0], tab_ref[1], tab_ref[2]).astype(BF16)
            x = kv_all[:, sl]
            lo = _lane(x.shape) < HD
            xk = jnp.where(lo, x, 0.0)
            rk = lax.rsqrt(jnp.sum(xk * xk, axis=-1, keepdims=True) * (1.0 / HD) + EPS)
            kf_ref[:, sl] = (xk * rk * gk_ref[...] + kpe).astype(BF16)
            if h % 2 == 0:
                v_even = _roll(x, 64)
            else:
                vp_ref[:, LANES * (h // 2):LANES * (h // 2 + 1)] = jnp.where(lo, v_even, x).astype(BF16)

    row = lambda w: pl.BlockSpec((tm, w), lambda i: (i, 0))
    vec = lambda w: pl.BlockSpec((1, w), lambda i: (0, 0))
    full = lambda a: pl.BlockSpec(a.shape, lambda i: (0, 0))
    return pl.pallas_call(
        body, name="prep2_fwd", grid=(t // tm,),
        in_specs=[row(256), row(128), pl.BlockSpec((tm, LANES), lambda i: (i, 15)), full(wuq), full(wukv),
                  vec(128), vec(128), vec(128), pl.BlockSpec((6, tm, LANES), lambda i: (0, i, 0))],
        out_specs=[row(1024), row(1024), row(512)],
        out_shape=[jax.ShapeDtypeStruct((t, 1024), BF16), jax.ShapeDtypeStruct((t, 1024), BF16),
                   jax.ShapeDtypeStruct((t, 512), BF16)],
        compiler_params=_params(("parallel",)),
    )(cqn, ckvn, proj, wuq, wukv, gq, gk, gkr, tabs)


def prep2_bwd(cqn, ckvn, proj, wuq, wukv, dqf, dkf, dvp, gq, gk, gkr, tabs):
    t = cqn.shape[0]
    tm = A_TM

    def body(cq_ref, ckv_ref, kr_ref, wuq_ref, wukv_ref, dqf_ref, dkf_ref, dvp_ref, gq_ref, gk_ref, gkr_ref, tab_ref,
             dcq_ref, dckv_ref, dkr_ref, dwuq_ref, dwukv_ref, dgq_ref, dgk_ref, dgkr_ref, dq_ref, dkv_ref):
        i = pl.program_id(0)

        @pl.when(i == 0)
        def _():
            for ref in (dwuq_ref, dwukv_ref, dgq_ref, dgk_ref, dgkr_ref):
                ref[...] = jnp.zeros_like(ref)

        q_all, kv_all = _up_proj(cq_ref, ckv_ref, wuq_ref, wukv_ref)
        dgq = jnp.zeros((1, LANES), F32)
        dgk = jnp.zeros((1, LANES), F32)
        dkpe = jnp.zeros((tm, LANES), F32)
        for h in range(HEADS):
            sl = slice(LANES * h, LANES * (h + 1))
            mn, mr, r, xn = _q_head_stats(q_all[:, sl])
            dy = _rope_bwd(dqf_ref[sl, :].T, tab_ref[0], tab_ref[1], tab_ref[2])
            dyg = dy * gq_ref[...]
            z = dyg * xn
            mean = jnp.where(mn, _seg_sum(z, mn) * (1.0 / HD), _seg_sum(z, mr) * (1.0 / ROPE))
            dq_ref[:, sl] = (r * (dyg - xn * mean)).astype(BF16)
            dgq = dgq + jnp.sum(dy * xn, axis=0, keepdims=True)

            x = kv_all[:, sl]
            dk = dkf_ref[:, sl]
            xk = jnp.where(mn, x, 0.0)
            rk = lax.rsqrt(jnp.sum(xk * xk, axis=-1, keepdims=True) * (1.0 / HD) + EPS)
            xkn = xk * rk
            dyk = jnp.where(mn, dk, 0.0)
            dykg = dyk * gk_ref[...]
            dxk = rk * (dykg - xkn * (jnp.sum(dykg * xkn, axis=-1, keepdims=True) * (1.0 / HD)))
            dgk = dgk + jnp.sum(dyk * xkn, axis=0, keepdims=True)
            dkpe = dkpe + jnp.where(mr, dk, 0.0)
            dvpair = dvp_ref[:, LANES * (h // 2):LANES * (h // 2 + 1)]
            dv = _roll(dvpair, 64) if h % 2 == 0 else dvpair
            dkv_ref[:, sl] = jnp.where(mn, dxk, dv).astype(BF16)

        r, xn = _kr_stats(kr_ref[...])
        dy = _rope_bwd(_roll(dkpe, 64), tab_ref[3], tab_ref[4], tab_ref[5])
        dyg = dy * gkr_ref[...]
        dkr_ref[...] = r * (dyg - xn * (jnp.sum(dyg * xn, axis=-1, keepdims=True) * (1.0 / ROPE)))
        dgq_ref[...] += dgq
        dgk_ref[...] += dgk
        dgkr_ref[...] += jnp.sum(dy * xn, axis=0, keepdims=True)
        dqb, dkvb = dq_ref[...], dkv_ref[...]
        dcq_ref[...] = _dot(dqb, wuq_ref[...])
        dckv_ref[...] = _dot_nt(dkvb, wukv_ref[...])
        dwuq_ref[...] += _dot_tn(dqb, cq_ref[...])
        dwukv_ref[...] += _dot_tn(ckv_ref[...], dkvb)

    row = lambda w: pl.BlockSpec((tm, w), lambda i: (i, 0))
    vec = lambda w: pl.BlockSpec((1, w), lambda i: (0, 0))
    full = lambda a: pl.BlockSpec(a.shape, lambda i: (0, 0))
    return pl.pallas_call(
        body, name="prep2_bwd", grid=(t // tm,),
        in_specs=[row(256), row(128), pl.BlockSpec((tm, LANES), lambda i: (i, 15)), full(wuq), full(wukv),
                  pl.BlockSpec((1024, tm), lambda i: (0, i)), row(1024), row(512),
                  vec(128), vec(128), vec(128), pl.BlockSpec((6, tm, LANES), lambda i: (0, i, 0))],
        out_specs=[row(256), row(128), row(128), full(wuq), full(wukv), vec(128), vec(128), vec(128)],
        out_shape=[jax.ShapeDtypeStruct((t, 256), F32), jax.ShapeDtypeStruct((t, 128), F32),
                   jax.ShapeDtypeStruct((t, LANES), F32), jax.ShapeDtypeStruct(wuq.shape, F32),
                   jax.ShapeDtypeStruct(wukv.shape, F32)] + [jax.ShapeDtypeStruct((1, LANES), F32)] * 3,
        scratch_shapes=[pltpu.VMEM((tm, 1024), BF16), pltpu.VMEM((tm, 1024), BF16)],
        compiler_params=_params(("arbitrary",)),
    )(cqn, ckvn, proj, wuq, wukv, dqf, dkf, dvp, gq, gk, gkr, tabs)


A_TQ = 256
A_WIN = 3 * A_TQ


def _a_specs(t):
    nb = t // A_TQ
    blk = lambda s: pl.BlockSpec((A_TQ, 512), lambda i: (jnp.maximum(i - s, 0), 0))
    return nb, blk


def _a_exp(q_ref, kc, b_ref, head, sl, lo):
    hm = lo if head % 2 == 0 else ~lo
    qm = jnp.where(hm, q_ref[:, sl], jnp.zeros((), BF16))
    s = _dot_nt(qm, kc) + b_ref[0, head]
    e = jnp.exp(s - jnp.max(s, axis=-1, keepdims=True))
    return hm, qm, e, 1.0 / jnp.sum(e, axis=-1, keepdims=True)


def _a_bias_spec():
    return pl.BlockSpec((1, HEADS, A_TQ, A_WIN), lambda i: (jnp.minimum(i, 2), 0, 0, 0))


def attn_a_fwd(qa, ka, va, bias):
    t = qa.shape[0]
    nb, blk = _a_specs(t)

    def body(q_ref, k2_ref, k1_ref, k0_ref, v2_ref, v1_ref, v0_ref, b_ref, o_ref):
        lo = _lane((A_TQ, LANES)) < HD
        for p in range(4):
            sl = slice(LANES * p, LANES * (p + 1))
            kc = jnp.concatenate([k2_ref[:, sl], k1_ref[:, sl], k0_ref[:, sl]], axis=0)
            vc = jnp.concatenate([v2_ref[:, sl], v1_ref[:, sl], v0_ref[:, sl]], axis=0)
            outs = []
            for h2 in range(2):
                _, _, e, inv = _a_exp(q_ref, kc, b_ref, 2 * p + h2, sl, lo)
                outs.append(_dot(e.astype(BF16), vc) * inv)
            o_ref[:, sl] = jnp.where(lo, outs[0], outs[1]).astype(BF16)

    return pl.pallas_call(
        body, name="attn_a_fwd", grid=(nb,),
        in_specs=[blk(0), blk(2), blk(1), blk(0), blk(2), blk(1), blk(0), _a_bias_spec()],
        out_specs=pl.BlockSpec((A_TQ, 512), lambda i: (i, 0)),
        out_shape=jax.ShapeDtypeStruct((t, 512), BF16),
        compiler_params=_params(("parallel",)),
    )(qa, ka, ka, ka, va, va, va, bias)


def attn_a_bwd(qa, ka, va, bias, do):
    t = qa.shape[0]
    nb, blk = _a_specs(t)

    def body(q_ref, k2_ref, k1_ref, k0_ref, v2_ref, v1_ref, v0_ref, b_ref, do_ref, dq_ref, dk_ref, dv_ref, db_ref):
        qb = pl.program_id(0)

        @pl.when(qb == 0)
        def _():
            db_ref[...] = jnp.zeros_like(db_ref)

        lo = _lane((A_TQ, LANES)) < HD
        for p in range(4):
            sl = slice(LANES * p, LANES * (p + 1))
            kc = jnp.concatenate([k2_ref[:, sl], k1_ref[:, sl], k0_ref[:, sl]], axis=0)
            vc = jnp.concatenate([v2_ref[:, sl], v1_ref[:, sl], v0_ref[:, sl]], axis=0)
            dqs = []
            dkc = jnp.zeros((A_WIN, LANES), F32)
            dvc = jnp.zeros((A_WIN, LANES), F32)
            for h2 in range(2):
                head = 2 * p + h2
                hm, qm, e, inv = _a_exp(q_ref, kc, b_ref, head, sl, lo)
                pr = e * inv
                dom = jnp.where(hm, do_ref[:, sl], jnp.zeros((), BF16))
                dp = _dot_nt(dom, vc)
                ds = pr * (dp - jnp.sum(pr * dp, axis=-1, keepdims=True))
                db_ref[head] += ds
                dsb = ds.astype(BF16)
                dqs.append(_dot(dsb, kc))
                dkc = dkc + _dot_tn(dsb, qm)
                dvc = dvc + _dot_tn(pr.astype(BF16), dom)
            dq_ref[:, sl] = jnp.where(lo, dqs[0], dqs[1])
            for s in range(3):
                rows = slice(A_TQ * (2 - s), A_TQ * (3 - s))
                dk_ref[s, :, sl] = dkc[rows]
                dv_ref[s, :, sl] = dvc[rows]

    share = pl.BlockSpec((3, A_TQ, 512), lambda i: (0, i, 0))
    return pl.pallas_call(
        body, name="attn_a_bwd", grid=(nb,),
        in_specs=[blk(0), blk(2), blk(1), blk(0), blk(2), blk(1), blk(0), _a_bias_spec(), blk(0)],
        out_specs=[pl.BlockSpec((A_TQ, 512), lambda i: (i, 0)), share, share,
                   pl.BlockSpec((HEADS, A_TQ, A_WIN), lambda i: (0, 0, 0))],
        out_shape=[jax.ShapeDtypeStruct((t, 512), F32), jax.ShapeDtypeStruct((3, t, 512), F32),
                   jax.ShapeDtypeStruct((3, t, 512), F32), jax.ShapeDtypeStruct((HEADS, A_TQ, A_WIN), F32)],
        compiler_params=_params(("arbitrary",)),
    )(qa, ka, ka, ka, va, va, va, bias, do)


B_T = 1024


B_SCALE2 = B_SCALE * 1.4426950408889634
_B_ALL = slice(0, B_T)
_B_LO, _B_HI = slice(0, B_T // 2), slice(B_T // 2, B_T)
_B_DIAG = ((_B_LO, _B_LO), (_B_LO, _B_HI), (_B_HI, _B_HI))


def _tri_tables(n, by_query):
    pairs = [(i, j) for i in range(n) for j in range(i + 1)] if by_query else [(i, j) for j in range(n) for i in range(j, n)]
    return (np.asarray([p[0] for p in pairs], np.int32), np.asarray([p[1] for p in pairs], np.int32))


def _b_mask_t(s):
    kc = lax.broadcasted_iota(jnp.int32, s.shape, 0) // CHUNK
    qc = lax.broadcasted_iota(jnp.int32, s.shape, 1) // CHUNK
    return jnp.where(kc <= qc, s, NEG)


def attn_b_fwd(qf, kf, vp):
    t = qf.shape[0]
    n = t // B_T
    qtab, ktab = _tri_tables(n, by_query=True)

    def body(qt_ref, kt_ref, q_ref, k_ref, v_ref, o_ref, lse_ref, m_s, l_s, acc_s):
        qb, kb = qt_ref[pl.program_id(1)], kt_ref[pl.program_id(1)]

        @pl.when(kb == 0)
        def _():
            m_s[...] = jnp.full_like(m_s, NEG)
            l_s[...] = jnp.zeros_like(l_s)
            acc_s[...] = jnp.zeros_like(acc_s)

        def block(kr, qr, masked):
            v = v_ref[kr, :]
            for h2 in range(2):
                sl = slice(LANES * h2, LANES * (h2 + 1))
                s = _dot_nt(k_ref[kr, sl], q_ref[qr, sl]) * B_SCALE2
                if masked:
                    s = _b_mask_t(s)
                m_prev = m_s[h2, :, qr]
                m_new = jnp.maximum(m_prev, jnp.max(s, axis=0, keepdims=True))
                alpha = jnp.exp2(m_prev - m_new)
                pr = jnp.exp2(s - m_new)
                l_s[h2, :, qr] = alpha * l_s[h2, :, qr] + jnp.sum(pr, axis=0, keepdims=True)
                acc_s[h2, :, qr] = alpha * acc_s[h2, :, qr] + _dot_tn(v, pr.astype(BF16))
                m_s[h2, :, qr] = m_new

        @pl.when(kb < qb)
        def _():
            block(_B_ALL, _B_ALL, False)

        @pl.when(kb == qb)
        def _():
            for kr, qr in _B_DIAG:
                block(kr, qr, kr == qr)
            for h2 in range(2):
                l = l_s[h2]
                rows = slice(HD * h2, HD * (h2 + 1))
                o_ref[rows, :] = (acc_s[h2, rows, :] * (1.0 / l)).astype(BF16)
                lse_ref[0, h2:h2 + 1, :] = m_s[h2] + jnp.log2(l)

    grid_spec = pltpu.PrefetchScalarGridSpec(
        num_scalar_prefetch=2, grid=(4, len(qtab)),
        in_specs=[pl.BlockSpec((B_T, 256), lambda p, s, qt, kt: (qt[s], p)),
                  pl.BlockSpec((B_T, 256), lambda p, s, qt, kt: (kt[s], p)),
                  pl.BlockSpec((B_T, LANES), lambda p, s, qt, kt: (kt[s], p))],
        out_specs=[pl.BlockSpec((LANES, B_T), lambda p, s, qt, kt: (p, qt[s])),
                   pl.BlockSpec((1, 2, B_T), lambda p, s, qt, kt: (p, 0, qt[s]))],
        scratch_shapes=[pltpu.VMEM((2, 1, B_T), F32), pltpu.VMEM((2, 1, B_T), F32), pltpu.VMEM((2, LANES, B_T), F32)])
    return pl.pallas_call(
        body, name="attn_b_fwd", grid_spec=grid_spec,
        out_shape=[jax.ShapeDtypeStruct((512, t), BF16), jax.ShapeDtypeStruct((4, 2, t), F32)],
        compiler_params=_params(("parallel", "arbitrary")),
    )(jnp.asarray(qtab), jnp.asarray(ktab), qf, kf, vp)


def attn_b_bwd(qf, kf, vp, do, do_t, o_t, lse, scatter=()):
    t = qf.shape[0]
    n = t // B_T
    qtab, ktab = _tri_tables(n, by_query=False)
    plan = ScatterPlan(scatter)
    m = plan.n
    last = len(qtab) - 1

    def body(*refs):
        qt_ref, kt_ref, q_ref, k_ref, v_ref, do_ref, dot_ref, ot_ref, lse_ref = refs[:9]
        ins, (dq_ref, dk_ref, dv_ref), outs = refs[9:9 + m], refs[9 + m:12 + m], refs[12 + m:12 + 2 * m]
        sems = refs[12 + 2 * m:]
        qb, kb = qt_ref[pl.program_id(1)], kt_ref[pl.program_id(1)]
        if m:
            pl.when((pl.program_id(0) == 0) & (pl.program_id(1) == 0))(lambda: plan.start(ins, outs, sems))

        @pl.when(pl.program_id(1) == 0)
        def _():
            dq_ref[...] = jnp.zeros_like(dq_ref)

        @pl.when(qb == kb)
        def _():
            dk_ref[...] = jnp.zeros_like(dk_ref)
            dv_ref[...] = jnp.zeros_like(dv_ref)

        def block(kr, qr, masked):
            nq = qr.stop - qr.start
            cols = pl.ds(pl.multiple_of(qb * B_T + qr.start, LANES), nq)
            v = v_ref[kr, :]
            dov = do_ref[qr, :]
            prod = dot_ref[:, qr].astype(F32) * ot_ref[:, qr].astype(F32)
            lo = _lane((nq, LANES)) < HD
            for h2 in range(2):
                sl = slice(LANES * h2, LANES * (h2 + 1))
                hm = lo if h2 == 0 else ~lo
                q = q_ref[qr, sl]
                k = k_ref[kr, sl]
                dom = jnp.where(hm, dov, jnp.zeros((), BF16))
                delta = jnp.sum(prod[HD * h2:HD * (h2 + 1), :], axis=0, keepdims=True)
                s = _dot_nt(k, q) * B_SCALE2
                if masked:
                    s = _b_mask_t(s)
                pr = jnp.exp2(s - lse_ref[0, h2:h2 + 1, qr])
                dp = _dot_nt(v, dom)
                ds = (pr * (dp - delta) * B_SCALE).astype(BF16)
                dk_ref[kr, sl] += _dot(ds, q)
                dv_ref[kr, :] += _dot(pr.astype(BF16), dom)
                dq_ref[sl, cols] += _dot_tn(k, ds)

        @pl.when(qb > kb)
        def _():
            block(_B_ALL, _B_ALL, False)

        @pl.when(qb == kb)
        def _():
            for kr, qr in _B_DIAG:
                block(kr, qr, kr == qr)

        if m:
            pl.when((pl.program_id(0) == 3) & (pl.program_id(1) == last))(lambda: plan.finish(ins, outs, sems))

    qrow = lambda w: pl.BlockSpec((B_T, w), lambda p, s, qt, kt: (qt[s], p))
    qcol = pl.BlockSpec((LANES, B_T), lambda p, s, qt, kt: (p, qt[s]))
    krow = lambda w: pl.BlockSpec((B_T, w), lambda p, s, qt, kt: (kt[s], p))
    grid_spec = pltpu.PrefetchScalarGridSpec(
        num_scalar_prefetch=2, grid=(4, len(qtab)),
        in_specs=[qrow(256), krow(256), krow(LANES), qrow(LANES), qcol, qcol,
                  pl.BlockSpec((1, 2, B_T), lambda p, s, qt, kt: (p, 0, qt[s]))] + [ANY] * m,
        out_specs=[pl.BlockSpec((256, t), lambda p, s, qt, kt: (p, 0)), krow(256), krow(LANES)] + [ANY] * m,
        scratch_shapes=plan.scratch if m else [])
    return pl.pallas_call(
        body, name="attn_b_bwd", grid_spec=grid_spec,
        out_shape=[jax.ShapeDtypeStruct((1024, t), F32), jax.ShapeDtypeStruct((t, 1024), F32),
                   jax.ShapeDtypeStruct((t, 512), F32)] + plan.out_shape,
        compiler_params=_params(("arbitrary", "arbitrary")),
    )(jnp.asarray(qtab), jnp.asarray(ktab), qf, kf, vp, do, do_t, o_t, lse, *scatter)


_U_LEN = A_TQ + A_WIN - 1


def _band_mask():
    a = np.arange(A_TQ)[:, None] // CHUNK
    b = np.arange(A_WIN)[None, :] // CHUNK
    return (b >= a) & (b <= a + A_LEFT)


def bias_block(table):
    h = table.shape[0]
    n_lo = A_WIN - 1 - 2 * A_TQ - A_MAX_REL
    ext = jnp.concatenate([jnp.repeat(table[:, :1], n_lo, axis=1), table,
                           jnp.repeat(table[:, -1:], _U_LEN - n_lo - table.shape[1], axis=1)], axis=1)
    row = jnp.pad(ext[:, ::-1], ((0, 0), (0, 1)))
    flat = jnp.tile(row, (1, A_TQ))[:, :A_TQ * _U_LEN]
    skew = flat.reshape(h, A_TQ, _U_LEN)
    toep = skew[:, :, A_TQ - 1:A_TQ - 1 + A_WIN]
    band = _band_mask()
    first = [band & (np.arange(A_WIN)[None, :] >= 2 * A_TQ - A_TQ * v) for v in range(3)]
    return jnp.where(jnp.asarray(np.stack(first))[:, None], toep[None], NEG)


def bias_block_grad(db):
    h = db.shape[0]
    n_lo = A_WIN - 1 - 2 * A_TQ - A_MAX_REL
    skew = jnp.pad(db, ((0, 0), (0, 0), (A_TQ - 1, 0)))
    flat = jnp.pad(skew.reshape(h, A_TQ * _U_LEN), ((0, 0), (0, A_TQ)))
    ext = jnp.sum(flat.reshape(h, A_TQ, _U_LEN + 1), axis=1)[:, :_U_LEN][:, ::-1]
    n_tab = 2 * A_MAX_REL + 1
    first = jnp.sum(ext[:, :n_lo + 1], axis=1, keepdims=True)
    last = jnp.sum(ext[:, n_lo + n_tab - 1:], axis=1, keepdims=True)
    return jnp.concatenate([first, ext[:, n_lo + 1:n_lo + n_tab - 1], last], axis=1)


def rope_tabs(t):
    inv = 1.0 / (10000.0 ** (jnp.arange(0, ROPE, 2, dtype=F32) / ROPE))
    ang = jnp.arange(t, dtype=F32)[:, None] * inv[None, :]
    cos, sin = jnp.cos(ang), jnp.sin(ang)
    z = lambda w: jnp.zeros((t, w), F32)
    ck = jnp.concatenate([cos, cos, z(96)], axis=1)
    s1k = jnp.concatenate([-sin, z(112)], axis=1)
    s2k = jnp.concatenate([z(16), sin, z(96)], axis=1)
    cq = jnp.concatenate([jnp.ones((t, HD), F32), cos, cos, z(32)], axis=1)
    s1q = jnp.concatenate([z(HD), -sin, z(48)], axis=1)
    s2q = jnp.concatenate([z(HD + 16), sin, z(32)], axis=1)
    return jnp.stack([cq, s1q, s2q, ck, s1k, s2k])


def _pad_lanes(v, width):
    return jnp.pad(v, ((0, 0), (0, width - v.shape[1])))


LATE = ("w_in", "b_w_uq", "b_w_ukv", "w_out", "ffn2_w_gate", "ffn2_w_up", "ffn2_w_down")
FFN2 = ("ffn2_w_gate", "ffn2_w_up", "ffn2_w_down")


def kernel_layout(gathered):
    w = {n: v for n, v in gathered.items() if n.startswith("ffn")}
    if "w_in" in gathered:
        w["w_in"] = jnp.pad(gathered["w_in"].reshape(IN_COLS, D_MODEL), ((0, PROJ_W - IN_COLS), (0, 0)))
        uq = gathered["b_w_uq"].reshape(HEADS, HD + ROPE, 256)
        w["b_w_uq"] = jnp.pad(uq, ((0, 0), (0, LANES - HD - ROPE), (0, 0))).reshape(HEADS * LANES, 256)
        w["b_w_ukv"] = _shards_to_cols(gathered["b_w_ukv"])
        w["w_out"] = gathered["w_out"].reshape(N_SHARD * gathered["w_out"].shape[1], D_MODEL)
    return w


def local_step(x, target, w, late=None):
    t = x.shape[0]
    gq = jnp.tile(w["a_q_norm"], (1, HEADS))
    gk = jnp.tile(w["a_k_norm"], (1, HEADS))
    gq128 = _pad_lanes(jnp.concatenate([w["b_q_nope_norm"], w["b_q_rope_norm"]], axis=1), LANES)
    gk128 = _pad_lanes(w["b_k_nope_norm"], LANES)
    gkr128 = _pad_lanes(w["b_k_rope_norm"], LANES)
    tabs = rope_tabs(t)
    bias = bias_block(w["a_rel_bias"])

    if late is None:
        x1 = ffn_fwd(x, w["ffn1_norm"], w["ffn1_w_gate"], w["ffn1_w_up"], w["ffn1_w_down"], "ffn_fwd")
    else:
        own, shard = late
        x1, *got = ffn_fwd(x, w["ffn1_norm"], w["ffn1_w_gate"], w["ffn1_w_up"], w["ffn1_w_down"], "ffn_fwd_gather",
                           gather=own)
        w = dict(w, **kernel_layout({n: lax.dynamic_update_index_in_dim(g_, o_, shard, 0)
                                     for n, g_, o_ in zip(LATE, got, own)}))
    h, proj = mix_proj(x1, w["mix_norm"], w["w_in"])
    qa, ka, va, cqn, ckvn = prep1_fwd(proj, gq, gk, w["b_q_lat_norm"], w["b_kv_lat_norm"])
    qf, kf, vp = prep2_fwd(cqn, ckvn, proj, w["b_w_uq"], w["b_w_ukv"], gq128, gk128, gkr128, tabs)
    oa = attn_a_fwd(qa, ka, va, bias)
    ob_t, lse = attn_b_fwd(qf, kf, vp)
    x2 = out_proj(x1, oa, ob_t, w["w_out"])
    x3 = ffn_fwd(x2, w["ffn2_norm"], w["ffn2_w_gate"], w["ffn2_w_up"], w["ffn2_w_down"], "ffn_fwd")

    g = {}
    dx3, g["final_norm"], loss = final_loss(x3, w["final_norm"], target)
    g["ffn2_w_gate"], g["ffn2_w_up"], g["ffn2_w_down"], dhp = ffn_bwd(
        x2, dx3, w["ffn2_norm"], w["ffn2_w_gate"], w["ffn2_w_up"], w["ffn2_w_down"], "ffn_bwd")
    dx2, g["ffn2_norm"] = norm_bwd(x2, w["ffn2_norm"], dhp, dx3, "ffn_norm_bwd")
    d_oa, d_ob, d_ob_t = out_proj_bwd(dx2, w["w_out"])
    g["w_out"] = jnp.concatenate([matmul(oa, dx2, "tn", "w_out_a_bwd", 512, 1024, 512, BF16),
                                  matmul(ob_t, dx2, "nn", "w_out_b_bwd", 512, 1024, 512, BF16)], axis=0)
    early = [g[n] for n in FFN2] + [g["w_out"].reshape(N_SHARD, -1, D_MODEL)]
    dqf, dkf, dvp, *landed_early = attn_b_bwd(qf, kf, vp, d_ob, d_ob_t, ob_t, lse,
                                              scatter=() if late is None else early)
    dqa, dkp, dvpa, dbias = attn_a_bwd(qa, ka, va, bias, d_oa)
    dcq, dckv, dkr, dwuq, dwukv, dgq128, dgk128, dgkr128 = prep2_bwd(
        cqn, ckvn, proj, w["b_w_uq"], w["b_w_ukv"], dqf, dkf, dvp, gq128, gk128, gkr128, tabs)
    g["b_w_uq"], g["b_w_ukv"] = dwuq.astype(BF16), dwukv.astype(BF16)
    dproj, dgq, dgk, g["b_q_lat_norm"], g["b_kv_lat_norm"] = prep1_bwd(
        proj, dqa, dkp, dvpa, dcq, dckv, dkr, gq, gk, w["b_q_lat_norm"], w["b_kv_lat_norm"])
    dh = matmul(dproj, w["w_in"], "nn", "w_in_bwd_x", 512, 1024, PROJ_W)
    g["w_in"] = matmul(dproj, h, "tn", "w_in_bwd_w", 1024, 1024, 512, BF16)
    dx1, g["mix_norm"] = norm_bwd(x1, w["mix_norm"], dh[None], dx2, "mix_norm_bwd")
    mid = [g["w_in"][:IN_COLS].reshape(N_SHARD, IN_COLS // N_SHARD, D_MODEL),
           g["b_w_uq"].reshape(HEADS, LANES, 256)[:, :HD + ROPE].reshape(N_SHARD, -1, 256),
           _cols_to_shards(g["b_w_ukv"])]
    g["ffn1_w_gate"], g["ffn1_w_up"], g["ffn1_w_down"], dhp, *landed_late = ffn_bwd(
        x, dx1, w["ffn1_norm"], w["ffn1_w_gate"], w["ffn1_w_up"], w["ffn1_w_down"],
        "ffn_bwd" if late is None else "ffn_bwd_scatter", scatter=() if late is None else mid,
        spread=None if late is None else late[1])
    grad_x, g["ffn1_norm"] = norm_bwd(x, w["ffn1_norm"], dhp, dx1, "ffn_norm_bwd")
    landed = dict(zip(FFN2 + ("w_out", "w_in", "b_w_uq", "b_w_ukv", "ffn1_w_gate", "ffn1_w_up", "ffn1_w_down"),
                      landed_early + landed_late))

    g["a_q_norm"] = jnp.sum(dgq.reshape(HEADS, HD), axis=0, keepdims=True)
    g["a_k_norm"] = jnp.sum(dgk.reshape(HEADS, HD), axis=0, keepdims=True)
    g["a_rel_bias"] = bias_block_grad(dbias)
    g["b_q_nope_norm"] = dgq128[:, :HD]
    g["b_q_rope_norm"] = dgq128[:, HD:HD + ROPE]
    g["b_k_nope_norm"] = dgk128[:, :HD]
    g["b_k_rope_norm"] = dgkr128[:, :ROPE]
    return loss, grad_x, g, landed


ANY = pl.BlockSpec(memory_space=pl.ANY)
N_DEV = 8


def _place():
    return lax.axis_index("x"), lax.axis_index("y"), lax.axis_index("c")


def _flip(v, bit):
    return 1 - v if bit else v


BF16_ROWS = 16


def _split_axis(shape):
    return 0 if (shape[0] // 2) % BF16_ROWS == 0 else 1


def _half_shape(shape):
    axis = _split_axis(shape)
    return tuple(s // 2 if a == axis else s for a, s in enumerate(shape))


def _half(shape, core):
    axis = _split_axis(shape)
    size = shape[axis] // 2
    return tuple(pl.ds(core * size, size) if a == axis else slice(None) for a in range(2))


class GatherPlan:
    def __init__(self, ws):
        self.shapes = [w.shape for w in ws]
        self.n = len(ws)
        self.out_shape = [jax.ShapeDtypeStruct((N_SHARD,) + w.shape, w.dtype) for w in ws]
        self.scratch = [pltpu.SemaphoreType.DMA((6 * self.n,)), pltpu.SemaphoreType.DMA((6 * self.n,))]

    def _copies(self, ins, outs, sems):
        x, y, c = _place()
        s_me = 2 * x + y
        sibling = (x, y, 1 - c)
        send_sems, recv_sems = sems

        def remote(k, src, dst, to):
            return pltpu.make_async_remote_copy(src_ref=src, dst_ref=dst, send_sem=send_sems.at[k],
                                                recv_sem=recv_sems.at[k], device_id=to, device_id_type=MESH)

        ici, fwd = [], []
        for a in range(self.n):
            mine, theirs = _half(self.shapes[a], c), _half(self.shapes[a], 1 - c)
            for j, (cx, cy) in enumerate([(1 - x, y), (x, 1 - y), (1 - x, 1 - y)]):
                got = outs[a].at[(2 * cx + cy,) + mine]
                ici.append((remote(6 * a + j, ins[a].at[mine], outs[a].at[(s_me,) + mine], (cx, cy, c)),
                            remote(6 * a + j, got, got, (cx, cy, c))))
                passed = outs[a].at[(2 * cx + cy,) + theirs]
                fwd.append((remote(6 * a + 3 + j, got, got, sibling), remote(6 * a + 3 + j, passed, passed, sibling)))
        return ici, fwd

    def start(self, ins, outs, sems):
        for send, _ in self._copies(ins, outs, sems)[0]:
            send.start()

    def forward(self, ins, outs, sems):
        ici, fwd = self._copies(ins, outs, sems)
        for (_, arrival), (send, _) in zip(ici, fwd):
            arrival.wait_recv()
            send.start()

    def finish(self, ins, outs, sems):
        ici, fwd = self._copies(ins, outs, sems)
        for _, arrival in fwd:
            arrival.wait_recv()
        for send, _ in ici + fwd:
            send.wait_send()


def allgather_shards(ws):
    plan = GatherPlan(ws)
    n = plan.n

    def body(*refs):
        ins, outs, sems = refs[:n], refs[n:2 * n], refs[2 * n:]
        plan.start(ins, outs, sems)
        plan.forward(ins, outs, sems)
        plan.finish(ins, outs, sems)

    return pl.pallas_call(
        body, name="allgather_shards", in_specs=[ANY] * n, out_specs=[ANY] * n,
        out_shape=plan.out_shape, scratch_shapes=plan.scratch,
    )(*ws)


class ScatterPlan:
    def __init__(self, gs):
        self.shapes = [g.shape[1:] for g in gs]
        self.n = len(gs)
        self.out_shape = [jax.ShapeDtypeStruct((N_DEV,) + _half_shape(g.shape[1:]), g.dtype) for g in gs]
        self.scratch = [pltpu.SemaphoreType.DMA((7 * self.n,)), pltpu.SemaphoreType.DMA((7 * self.n,)),
                        pltpu.SemaphoreType.DMA((self.n,))]

    def _copies(self, ins, outs, sems):
        x, y, c = _place()
        me = 4 * x + 2 * y + c
        send_sems, recv_sems, local_sems = sems
        local, sends, arrivals = [], [], []
        for a in range(self.n):
            piece = lambda px, py, pc, a=a: ins[a].at[(2 * px + py,) + _half(self.shapes[a], pc)]
            local.append(pltpu.make_async_copy(piece(x, y, c), outs[a].at[me], local_sems.at[a]))
            for k in range(1, N_DEV):
                px, py, pc = _flip(x, k & 4), _flip(y, k & 2), _flip(c, k & 1)
                sem = dict(send_sem=send_sems.at[7 * a + k - 1], recv_sem=recv_sems.at[7 * a + k - 1],
                           device_id=(px, py, pc), device_id_type=MESH)
                sends.append(pltpu.make_async_remote_copy(
                    src_ref=piece(px, py, pc), dst_ref=outs[a].at[me], **sem))
                slot = outs[a].at[4 * px + 2 * py + pc]
                arrivals.append(pltpu.make_async_remote_copy(src_ref=slot, dst_ref=slot, **sem))
        return local, sends, arrivals

    def start(self, ins, outs, sems):
        local, sends, _ = self._copies(ins, outs, sems)
        for cp in local + sends:
            cp.start()

    def finish(self, ins, outs, sems):
        local, sends, arrivals = self._copies(ins, outs, sems)
        for cp in arrivals:
            cp.wait_recv()
        for cp in sends:
            cp.wait_send()
        for cp in local:
            cp.wait()


def scatter_partials(gs):
    plan = ScatterPlan(gs)
    n = plan.n

    def body(*refs):
        ins, outs, sems = refs[:n], refs[n:2 * n], refs[2 * n:]
        plan.start(ins, outs, sems)
        plan.finish(ins, outs, sems)

    return pl.pallas_call(
        body, name="scatter_partials", in_specs=[ANY] * n, out_specs=[ANY] * n,
        out_shape=plan.out_shape, scratch_shapes=plan.scratch,
    )(*gs)


def sum_slots(land, name):
    _, rows, cols = land.shape
    tr = rows // 2 if rows > 128 and (rows // 2) % BF16_ROWS == 0 else rows

    def body(l_ref, o_ref):
        acc = l_ref[0].astype(F32)
        for s in range(1, N_DEV):
            acc = acc + l_ref[s].astype(F32)
        o_ref[...] = acc

    return pl.pallas_call(
        body, name=name, grid=(rows // tr,),
        in_specs=[pl.BlockSpec((N_DEV, tr, cols), lambda i: (0, i, 0))],
        out_specs=pl.BlockSpec((tr, cols), lambda i: (i, 0)),
        out_shape=jax.ShapeDtypeStruct((rows, cols), F32),
        compiler_params=_params(("parallel",)),
    )(land)


def join_halves(hs, shapes):
    n = len(hs)

    def body(*refs):
        ins, outs = refs[:n], refs[n:2 * n]
        send_sems, recv_sems = refs[2 * n:]
        x, y, c = _place()
        sends = []
        for a in range(n):
            mine = outs[a].at[_half(shapes[a], c)]
            sends.append(pltpu.make_async_remote_copy(
                src_ref=ins[a], dst_ref=mine, send_sem=send_sems.at[a], recv_sem=recv_sems.at[a],
                device_id=(x, y, 1 - c), device_id_type=MESH))
            sends[-1].start()
        for a in range(n):
            theirs = outs[a].at[_half(shapes[a], 1 - c)]
            pltpu.make_async_remote_copy(
                src_ref=theirs, dst_ref=theirs, send_sem=send_sems.at[a], recv_sem=recv_sems.at[a],
                device_id=(x, y, 1 - c), device_id_type=MESH).wait_recv()
        for cp in sends:
            cp.wait_send()

    return pl.pallas_call(
        body, name="join_halves",
        in_specs=[ANY] * n, out_specs=[ANY] * n,
        out_shape=[jax.ShapeDtypeStruct(tuple(s), h.dtype) for s, h in zip(shapes, hs)],
        scratch_shapes=[pltpu.SemaphoreType.DMA((n,)), pltpu.SemaphoreType.DMA((n,))],
    )(*hs)


def allreduce_small(vec):
    def body(v_ref, o_ref, land_ref, send_sems, recv_sems):
        x, y, c = _place()
        me = 4 * x + 2 * y + c
        land_ref[me] = v_ref[...]
        sends = []
        for k in range(1, N_DEV):
            px, py, pc = _flip(x, k & 4), _flip(y, k & 2), _flip(c, k & 1)
            sends.append(pltpu.make_async_remote_copy(
                src_ref=v_ref, dst_ref=land_ref.at[me], send_sem=send_sems.at[k - 1], recv_sem=recv_sems.at[k - 1],
                device_id=(px, py, pc), device_id_type=MESH))
            sends[-1].start()
        for k in range(1, N_DEV):
            px, py, pc = _flip(x, k & 4), _flip(y, k & 2), _flip(c, k & 1)
            slot = land_ref.at[4 * px + 2 * py + pc]
            pltpu.make_async_remote_copy(
                src_ref=slot, dst_ref=slot, send_sem=send_sems.at[k - 1], recv_sem=recv_sems.at[k - 1],
                device_id=(px, py, pc), device_id_type=MESH).wait_recv()
        for cp in sends:
            cp.wait_send()
        acc = land_ref[0]
        for s in range(1, N_DEV):
            acc = acc + land_ref[s]
        o_ref[...] = acc

    vm = pl.BlockSpec(memory_space=pltpu.VMEM)
    return pl.pallas_call(
        body, name="allreduce_small",
        in_specs=[vm], out_specs=vm,
        out_shape=jax.ShapeDtypeStruct(vec.shape, F32),
        scratch_shapes=[pltpu.VMEM((N_DEV,) + vec.shape, F32), pltpu.SemaphoreType.DMA((N_DEV - 1,)),
                        pltpu.SemaphoreType.DMA((N_DEV - 1,))],
    )(vec)


def adamw(w, g, m, v, name):
    rows, cols = w.shape
    tr = rows
    while tr * cols * 4 * 14 > 24 * 1024 * 1024 and tr % 16 == 0:
        tr //= 2
    c1 = 1.0 - ADAM_B1 ** ADAM_STEP
    c2 = 1.0 - ADAM_B2 ** ADAM_STEP

    def body(w_ref, g_ref, m_ref, v_ref, d_ref, nm_ref, nv_ref):
        gv = g_ref[...]
        nm = ADAM_B1 * m_ref[...] + (1.0 - ADAM_B1) * gv
        nv = ADAM_B2 * v_ref[...] + (1.0 - ADAM_B2) * (gv * gv)
        nm_ref[...] = nm
        nv_ref[...] = nv
        d_ref[...] = -ADAM_LR * ((nm / c1) / (jnp.sqrt(nv / c2) + ADAM_EPS) + ADAM_WD * w_ref[...])

    blk = pl.BlockSpec((tr, cols), lambda i: (i, 0))
    return pl.pallas_call(
        body, name=name, grid=(rows // tr,),
        in_specs=[blk] * 4, out_specs=[blk] * 3,
        out_shape=[jax.ShapeDtypeStruct((rows, cols), F32)] * 3,
        compiler_params=_params(("parallel",)),
    )(w, g, m, v)


BIG = ("ffn1_w_gate", "ffn1_w_up", "ffn1_w_down", "w_in", "b_w_uq", "b_w_ukv", "w_out",
       "ffn2_w_gate", "ffn2_w_up", "ffn2_w_down")
SMALL = ("ffn1_norm", "mix_norm", "a_q_norm", "a_k_norm", "a_rel_bias", "b_q_lat_norm", "b_kv_lat_norm",
         "b_q_nope_norm", "b_q_rope_norm", "b_k_nope_norm", "b_k_rope_norm", "ffn2_norm", "final_norm")
WEIGHTS = ("ffn1_norm", "ffn1_w_gate", "ffn1_w_up", "ffn1_w_down", "mix_norm", "w_in", "a_q_norm", "a_k_norm",
           "a_rel_bias", "b_q_lat_norm", "b_w_uq", "b_kv_lat_norm", "b_w_ukv", "b_q_nope_norm", "b_q_rope_norm",
           "b_k_nope_norm", "b_k_rope_norm", "w_out", "ffn2_norm", "ffn2_w_gate", "ffn2_w_up", "ffn2_w_down",
           "final_norm")
TRANSPOSED = ("ffn1_w_gate", "ffn1_w_up", "ffn2_w_gate", "ffn2_w_up", "w_in", "b_w_uq")
PACK_SHAPE = (8, 1024)


def _pack_small(d, last=None):
    flat = [d[n].reshape(-1) for n in SMALL]
    used = sum(f.shape[0] for f in flat)
    total = PACK_SHAPE[0] * PACK_SHAPE[1]
    tail = jnp.zeros((total - used - 1,), F32)
    end = jnp.zeros((1,), F32) if last is None else last.reshape(1)
    return jnp.concatenate(flat + [tail, end]).reshape(PACK_SHAPE)


def _unpack_small(p, like):
    flat = p.reshape(-1)
    out, off = {}, 0
    for n in SMALL:
        size = like[n].size
        out[n] = flat[off:off + size].reshape(like[n].shape)
        off += size
    return out, flat[-1]


def _cols_to_shards(g):
    rows, cols = g.shape
    return g.reshape(rows, N_SHARD, cols // N_SHARD).transpose(1, 0, 2)


def _shards_to_cols(g):
    return g.transpose(1, 0, 2).reshape(g.shape[1], -1)


def kernel(x, ffn1_norm, ffn1_w_gate, ffn1_w_up, ffn1_w_down, mix_norm, w_in, a_q_norm, a_k_norm, a_rel_bias, b_q_lat_norm, b_w_uq, b_kv_lat_norm, b_w_ukv, b_q_nope_norm, b_q_rope_norm, b_k_nope_norm, b_k_rope_norm, w_out, ffn2_norm, ffn2_w_gate, ffn2_w_up, ffn2_w_down, final_norm, loss_target, m_ffn1_norm, m_ffn1_w_gate, m_ffn1_w_up, m_ffn1_w_down, m_mix_norm, m_w_in, m_a_q_norm, m_a_k_norm, m_a_rel_bias, m_b_q_lat_norm, m_b_w_uq, m_b_kv_lat_norm, m_b_w_ukv, m_b_q_nope_norm, m_b_q_rope_norm, m_b_k_nope_norm, m_b_k_rope_norm, m_w_out, m_ffn2_norm, m_ffn2_w_gate, m_ffn2_w_up, m_ffn2_w_down, m_final_norm, v_ffn1_norm, v_ffn1_w_gate, v_ffn1_w_up, v_ffn1_w_down, v_mix_norm, v_w_in, v_a_q_norm, v_a_k_norm, v_a_rel_bias, v_b_q_lat_norm, v_b_w_uq, v_b_kv_lat_norm, v_b_w_ukv, v_b_q_nope_norm, v_b_q_rope_norm, v_b_k_nope_norm, v_b_k_rope_norm, v_w_out, v_ffn2_norm, v_ffn2_w_gate, v_ffn2_w_up, v_ffn2_w_down, v_final_norm):
    args = locals()
    view = lambda a, n: a[0].T if n in TRANSPOSED else a[0]
    wts = {n: view(args[n], n) for n in WEIGHTS}
    mom = {n: view(args["m_" + n], n) for n in WEIGHTS}
    var = {n: view(args["v_" + n], n) for n in WEIGHTS}

    shard = 2 * lax.axis_index("x") + lax.axis_index("y")
    core = lax.axis_index("c")
    first = [n for n in BIG if n not in LATE]
    own = [wts[n].astype(BF16) for n in first]
    w = {n: wts[n] if n == "a_rel_bias" else wts[n][None] for n in SMALL}
    w.update(kernel_layout({n: lax.dynamic_update_index_in_dim(got, mine, shard, 0)
                            for n, got, mine in zip(first, allgather_shards(own), own)}))

    loss, grad_x, g, landed = local_step(x[0], loss_target[0], w,
                                         late=([wts[n].astype(BF16) for n in LATE], shard))

    me = 2 * shard + core
    for n in first:
        piece = lax.dynamic_slice(g[n], (shard, core * (FS // 2), 0), (1, FS // 2, D_MODEL))
        landed[n] = lax.dynamic_update_slice(landed[n], piece, (me, 0, 0))
    halves = [sum_slots(landed[n], "sum_slots") for n in BIG]
    shapes = [wts[n].shape for n in BIG]
    axes = [_split_axis(s) for s in shapes]
    grads = dict(zip(BIG, (lax.dynamic_update_slice_in_dim(got, mine, core * mine.shape[ax], ax)
                           for got, mine, ax in zip(join_halves(halves, shapes), halves, axes))))

    small_sum, loss_sum = _unpack_small(allreduce_small(_pack_small(g, loss[0, 0])), wts)
    grads.update(small_sum)

    delta, new_m, new_v = {}, {}, {}
    for n in BIG:
        delta[n], new_m[n], new_v[n] = adamw(wts[n], grads[n], mom[n], var[n], "adamw")
    packed = adamw(_pack_small(wts), _pack_small(grads), _pack_small(mom), _pack_small(var), "adamw_small")
    for dst, p in zip((delta, new_m, new_v), packed):
        dst.update(_unpack_small(p, wts)[0])

    lead = lambda d: [(d[n].T if n in TRANSPOSED else d[n])[None] for n in WEIGHTS]
    return (loss_sum, grad_x[None], *lead(grads), *lead(delta), *lead(new_m), *lead(new_v))
```

```python
import functools
import math

import numpy as np
import jax
import jax.numpy as jnp
from jax import lax
from jax.experimental import pallas as pl
from jax.experimental.pallas import tpu as pltpu

F32 = jnp.float32
BF16 = jnp.bfloat16
EPS = 1e-6
NEG = -1e30

D_MODEL = 1024
D_FF = 2816
N_SHARD = 4
FS = D_FF // N_SHARD
CHUNK = 64
A_LEFT = 8
A_MAX_REL = 128
HEADS = 8
HD = 64
ROPE = 32
PROJ_W = 2048
IN_COLS = 1952
B_SCALE = 96 ** -0.5
LANES = 128

ADAM_LR = 0.001
ADAM_B1 = 0.9
ADAM_B2 = 0.999
ADAM_EPS = 1e-08
ADAM_WD = 0.01
ADAM_STEP = 10

VMEM_LIMIT = 56 * 1024 * 1024

MESH = pl.DeviceIdType.MESH


def _dot(a, b):
    return lax.dot_general(a, b, (((1,), (0,)), ((), ())), preferred_element_type=F32)


def _dot_nt(a, b):
    return lax.dot_general(a, b, (((1,), (1,)), ((), ())), preferred_element_type=F32)


def _dot_tn(a, b):
    return lax.dot_general(a, b, (((0,), (0,)), ((), ())), preferred_element_type=F32)


def _params(sem):
    return pltpu.CompilerParams(dimension_semantics=sem, vmem_limit_bytes=VMEM_LIMIT)


def _rms(xv):
    r = lax.rsqrt(jnp.mean(xv * xv, axis=-1, keepdims=True) + EPS)
    return r, xv * r


def ffn_fwd(x, g, wg, wu, wd, name, gather=()):
    t, d = x.shape
    tm = 512
    ni = t // tm
    plan = GatherPlan(gather)
    n = plan.n

    def body(*refs):
        x_ref, g_ref, wg_ref, wu_ref, wd_ref = refs[:5]
        ins, o_ref, outs = refs[5:5 + n], refs[5 + n], refs[6 + n:6 + 2 * n]
        h_ref, acc_ref = refs[6 + 2 * n:8 + 2 * n]
        sems = refs[8 + 2 * n:]
        i, j = pl.program_id(0), pl.program_id(1)
        if n:
            pl.when((i == 0) & (j == 0))(lambda: plan.start(ins, outs, sems))
            pl.when((i == (3 * ni) // 4) & (j == 0))(lambda: plan.forward(ins, outs, sems))

        @pl.when(j == 0)
        def _():
            _, xn = _rms(x_ref[...])
            h_ref[...] = (xn * g_ref[...]).astype(BF16)
            acc_ref[...] = jnp.zeros_like(acc_ref)

        h = h_ref[...]
        gp = _dot_nt(h, wg_ref[0])
        up = _dot_nt(h, wu_ref[0])
        a = (gp * jax.nn.sigmoid(gp) * up).astype(BF16)
        acc_ref[...] += _dot(a, wd_ref[0])

        @pl.when(j == N_SHARD - 1)
        def _():
            o_ref[...] = x_ref[...] + 0.5 * acc_ref[...]

        if n:
            pl.when((i == ni - 1) & (j == N_SHARD - 1))(lambda: plan.finish(ins, outs, sems))

    res = pl.pallas_call(
        body, name=name, grid=(ni, N_SHARD),
        in_specs=[pl.BlockSpec((tm, d), lambda i, j: (i, 0)),
                  pl.BlockSpec((1, d), lambda i, j: (0, 0)),
                  pl.BlockSpec((1, FS, d), lambda i, j: (j, 0, 0)),
                  pl.BlockSpec((1, FS, d), lambda i, j: (j, 0, 0)),
                  pl.BlockSpec((1, FS, d), lambda i, j: (j, 0, 0))] + [ANY] * n,
        out_specs=[pl.BlockSpec((tm, d), lambda i, j: (i, 0))] + [ANY] * n,
        out_shape=[jax.ShapeDtypeStruct((t, d), F32)] + plan.out_shape,
        scratch_shapes=[pltpu.VMEM((tm, d), BF16), pltpu.VMEM((tm, d), F32)] + (plan.scratch if n else []),
        compiler_params=_params(("arbitrary", "arbitrary")),
    )(x, g, wg, wu, wd, *gather)
    return res if n else res[0]


def ffn_bwd(x, dout, g, wg, wu, wd, name, scatter=(), spread=None):
    t, d = x.shape
    tm = 512
    ni = t // tm
    hf = FS // 2
    plan = ScatterPlan(scatter)
    m = plan.n
    k = 0 if spread is None else 3
    steps = jnp.arange(N_SHARD, dtype=jnp.int32)
    order = steps if spread is None else (spread + 1 + steps) % N_SHARD

    def body(*refs):
        ord_ref, x_ref, do_ref, g_ref, wg_ref, wu_ref, wd_ref = refs[:7]
        ins, (dwg_out, dwu_out, dwd_out, dhp_ref) = refs[7:7 + m], refs[7 + m:11 + m]
        outs, lands = refs[11 + m:11 + 2 * m], refs[11 + 2 * m:11 + 2 * m + k]
        dwg_ref, dwu_ref, dwd_ref = refs[11 + 2 * m + k:14 + 2 * m + k]
        sems = refs[14 + 2 * m + k:17 + 2 * m + k] if m else ()
        stage_ref = refs[-3] if k else None
        j, i = pl.program_id(0), pl.program_id(1)
        if m:
            pl.when((j == 0) & (i == 0))(lambda: plan.start(ins, outs, sems))

        def chunk_copies(jj):
            send_sems, recv_sems = refs[-2:]
            px, py, pc = _place()
            me = 4 * px + 2 * py + pc
            tx, ty = ord_ref[jj] // 2, ord_ref[jj] % 2
            copies = []
            for n_ in range(3):
                for h_ in range(2):
                    copies.append((pltpu.make_async_remote_copy(
                        src_ref=stage_ref.at[n_, pl.ds(h_ * hf, hf)], dst_ref=lands[n_].at[me],
                        send_sem=send_sems.at[6 * jj + 2 * n_ + h_], recv_sem=recv_sems.at[3 * me + n_],
                        device_id=(tx, ty, h_), device_id_type=MESH), (tx != px) | (ty != py) | (pc != h_)))
            return copies

        def arrivals():
            send_sems, recv_sems = refs[-2:]
            px, py, pc = _place()
            me = 4 * px + 2 * py + pc
            for s_ in range(N_DEV):
                for n_ in range(3):
                    slot = lands[n_].at[s_]
                    cp = pltpu.make_async_remote_copy(
                        src_ref=slot, dst_ref=slot, send_sem=send_sems.at[0], recv_sem=recv_sems.at[3 * s_ + n_],
                        device_id=(px, py, pc), device_id_type=MESH)
                    pl.when(me != s_)(cp.wait_recv)

        _, xn = _rms(x_ref[...])
        h = (xn * g_ref[...]).astype(BF16)
        dz = (0.5 * do_ref[...]).astype(BF16)
        wgv, wuv, wdv = wg_ref[0], wu_ref[0], wd_ref[0]
        gp = _dot_nt(h, wgv)
        up = _dot_nt(h, wuv)
        s = jax.nn.sigmoid(gp)
        sg = gp * s
        a = (sg * up).astype(BF16)
        da = _dot_nt(dz, wdv)
        dup = (da * sg).astype(BF16)
        dgp = (da * up * (s * (1.0 + gp * (1.0 - s)))).astype(BF16)

        @pl.when(i == 0)
        def _():
            dwg_ref[...] = jnp.zeros_like(dwg_ref)
            dwu_ref[...] = jnp.zeros_like(dwu_ref)
            dwd_ref[...] = jnp.zeros_like(dwd_ref)

        dwd_ref[...] += _dot_tn(a, dz)
        dwg_ref[...] += _dot_tn(dgp, h)
        dwu_ref[...] += _dot_tn(dup, h)
        dhp_ref[0] = (_dot(dgp, wgv) + _dot(dup, wuv)).astype(BF16)

        @pl.when(i == ni - 1)
        def _():
            dwg_out[0] = dwg_ref[...].astype(BF16)
            dwu_out[0] = dwu_ref[...].astype(BF16)
            dwd_out[0] = dwd_ref[...].astype(BF16)
            if k:
                @pl.when(j >= 1)
                def _():
                    for cp, leaves in chunk_copies(j - 1):
                        pl.when(leaves)(cp.wait_send)
                for n_, acc in enumerate((dwg_ref, dwu_ref, dwd_ref)):
                    stage_ref[n_] = acc[...].astype(BF16)
                for cp, leaves in chunk_copies(j):
                    pl.when(leaves)(cp.start)

                @pl.when(j == N_SHARD - 1)
                def _():
                    for cp, leaves in chunk_copies(N_SHARD - 1):
                        pl.when(leaves)(cp.wait_send)
                    arrivals()

        if m:
            pl.when((j == N_SHARD - 1) & (i == ni - 1))(lambda: plan.finish(ins, outs, sems))

    chunk = pl.BlockSpec((1, FS, d), lambda j, i, o: (o[j], 0, 0))
    tok = pl.BlockSpec((tm, d), lambda j, i, o: (i, 0))
    grid_spec = pltpu.PrefetchScalarGridSpec(
        num_scalar_prefetch=1, grid=(N_SHARD, ni),
        in_specs=[tok, tok, pl.BlockSpec((1, d), lambda j, i, o: (0, 0)), chunk, chunk, chunk] + [ANY] * m,
        out_specs=[chunk, chunk, chunk, pl.BlockSpec((1, tm, d), lambda j, i, o: (o[j], i, 0))] + [ANY] * (m + k),
        scratch_shapes=[pltpu.VMEM((FS, d), F32), pltpu.VMEM((FS, d), F32), pltpu.VMEM((FS, d), F32)]
        + (plan.scratch if m else [])
        + ([pltpu.VMEM((3, FS, d), BF16), pltpu.SemaphoreType.DMA((6 * N_SHARD,)),
            pltpu.SemaphoreType.DMA((3 * N_DEV,))] if k else []))
    return pl.pallas_call(
        body, name=name, grid_spec=grid_spec,
        out_shape=[jax.ShapeDtypeStruct((N_SHARD, FS, d), BF16),
                   jax.ShapeDtypeStruct((N_SHARD, FS, d), BF16),
                   jax.ShapeDtypeStruct((N_SHARD, FS, d), BF16),
                   jax.ShapeDtypeStruct((N_SHARD, t, d), BF16)] + plan.out_shape
        + [jax.ShapeDtypeStruct((N_DEV, hf, d), BF16)] * k,
        compiler_params=_params(("arbitrary", "arbitrary")),
    )(order, x, dout, g, wg, wu, wd, *scatter)


def norm_bwd(x, g, dhp, dres, name):
    t, d = x.shape
    p = dhp.shape[0]
    tm = 512

    def body(x_ref, g_ref, dhp_ref, dres_ref, dx_ref, dg_ref):
        i = pl.program_id(0)
        r, xn = _rms(x_ref[...])
        dh = dhp_ref[0].astype(F32)
        for q in range(1, p):
            dh = dh + dhp_ref[q].astype(F32)
        dhg = dh * g_ref[...]
        dx_ref[...] = dres_ref[...] + r * (dhg - xn * jnp.mean(dhg * xn, axis=-1, keepdims=True))

        @pl.when(i == 0)
        def _():
            dg_ref[...] = jnp.zeros_like(dg_ref)

        dg_ref[...] += jnp.sum(dh * xn, axis=0, keepdims=True)

    return pl.pallas_call(
        body, name=name, grid=(t // tm,),
        in_specs=[pl.BlockSpec((tm, d), lambda i: (i, 0)),
                  pl.BlockSpec((1, d), lambda i: (0, 0)),
                  pl.BlockSpec((p, tm, d), lambda i: (0, i, 0)),
                  pl.BlockSpec((tm, d), lambda i: (i, 0))],
        out_specs=[pl.BlockSpec((tm, d), lambda i: (i, 0)),
                   pl.BlockSpec((1, d), lambda i: (0, 0))],
        out_shape=[jax.ShapeDtypeStruct((t, d), F32), jax.ShapeDtypeStruct((1, d), F32)],
        compiler_params=_params(("arbitrary",)),
    )(x, g, dhp, dres)


def final_loss(x, g, target):
    t, d = x.shape
    tm = 512

    def body(x_ref, g_ref, t_ref, dx_ref, dg_ref, loss_ref):
        i = pl.program_id(0)
        r, xn = _rms(x_ref[...])
        gv = g_ref[...]
        e = xn * gv - t_ref[...]
        dy = e * (1.0 / d)
        dhg = dy * gv
        dx_ref[...] = r * (dhg - xn * jnp.mean(dhg * xn, axis=-1, keepdims=True))

        @pl.when(i == 0)
        def _():
            dg_ref[...] = jnp.zeros_like(dg_ref)
            loss_ref[...] = jnp.zeros_like(loss_ref)

        dg_ref[...] += jnp.sum(dy * xn, axis=0, keepdims=True)
        part = jnp.sum(jnp.sum(e * e, axis=-1, keepdims=True), axis=0, keepdims=True) * (0.5 / d)
        loss_ref[...] += jnp.broadcast_to(part, loss_ref.shape)

    return pl.pallas_call(
        body, name="final_loss", grid=(t // tm,),
        in_specs=[pl.BlockSpec((tm, d), lambda i: (i, 0)),
                  pl.BlockSpec((1, d), lambda i: (0, 0)),
                  pl.BlockSpec((tm, d), lambda i: (i, 0))],
        out_specs=[pl.BlockSpec((tm, d), lambda i: (i, 0)),
                   pl.BlockSpec((1, d), lambda i: (0, 0)),
                   pl.BlockSpec((1, LANES), lambda i: (0, 0))],
        out_shape=[jax.ShapeDtypeStruct((t, d), F32), jax.ShapeDtypeStruct((1, d), F32),
                   jax.ShapeDtypeStruct((1, LANES), F32)],
        compiler_params=_params(("arbitrary",)),
    )(x, g, target)


def mix_proj(x, g, w):
    t, d = x.shape
    n = w.shape[0]
    tm = 512

    def body(x_ref, g_ref, w_ref, h_ref, p_ref):
        _, xn = _rms(x_ref[...])
        h = (xn * g_ref[...]).astype(BF16)
        h_ref[...] = h
        p_ref[...] = _dot_nt(h, w_ref[...])

    return pl.pallas_call(
        body, name="mix_proj", grid=(t // tm,),
        in_specs=[pl.BlockSpec((tm, d), lambda i: (i, 0)),
                  pl.BlockSpec((1, d), lambda i: (0, 0)),
                  pl.BlockSpec((n, d), lambda i: (0, 0))],
        out_specs=[pl.BlockSpec((tm, d), lambda i: (i, 0)),
                   pl.BlockSpec((tm, n), lambda i: (i, 0))],
        out_shape=[jax.ShapeDtypeStruct((t, d), BF16), jax.ShapeDtypeStruct((t, n), F32)],
        compiler_params=_params(("parallel",)),
    )(x, g, w)


def matmul(a, b, mode, name, tm, tn, tk, out_dtype=F32):
    if mode == "nn":
        (m, k), n = a.shape, b.shape[1]
        a_spec = pl.BlockSpec((tm, tk), lambda i, j, q: (i, q))
        b_spec = pl.BlockSpec((tk, tn), lambda i, j, q: (q, j))
        dot = _dot
    elif mode == "nt":
        (m, k), n = a.shape, b.shape[0]
        a_spec = pl.BlockSpec((tm, tk), lambda i, j, q: (i, q))
        b_spec = pl.BlockSpec((tn, tk), lambda i, j, q: (j, q))
        dot = _dot_nt
    else:
        (k, m), n = a.shape, b.shape[1]
        a_spec = pl.BlockSpec((tk, tm), lambda i, j, q: (q, i))
        b_spec = pl.BlockSpec((tk, tn), lambda i, j, q: (q, j))
        dot = _dot_tn
    assert m % tm == 0 and n % tn == 0 and k % tk == 0, (m, n, k, tm, tn, tk)
    nk = k // tk

    def body(a_ref, b_ref, o_ref, acc_ref):
        q = pl.program_id(2)

        @pl.when(q == 0)
        def _():
            acc_ref[...] = jnp.zeros_like(acc_ref)

        acc_ref[...] += dot(a_ref[...].astype(BF16), b_ref[...].astype(BF16))

        @pl.when(q == nk - 1)
        def _():
            o_ref[...] = acc_ref[...].astype(out_dtype)

    return pl.pallas_call(
        body, name=name, grid=(m // tm, n // tn, nk),
        in_specs=[a_spec, b_spec],
        out_specs=pl.BlockSpec((tm, tn), lambda i, j, q: (i, j)),
        out_shape=jax.ShapeDtypeStruct((m, n), out_dtype),
        scratch_shapes=[pltpu.VMEM((tm, tn), F32)],
        compiler_params=_params(("parallel", "parallel", "arbitrary")),
    )(a, b)


def out_proj(x, oa, ob_t, w):
    t, d = x.shape
    half = oa.shape[1]
    tm = 512

    def body(x_ref, oa_ref, obt_ref, w_ref, o_ref):
        o_ref[...] = (x_ref[...] + _dot(oa_ref[...], w_ref[0:half, :])
                      + _dot_tn(obt_ref[...], w_ref[half:2 * half, :]))

    return pl.pallas_call(
        body, name="out_proj", grid=(t // tm,),
        in_specs=[pl.BlockSpec((tm, d), lambda i: (i, 0)),
                  pl.BlockSpec((tm, half), lambda i: (i, 0)),
                  pl.BlockSpec((half, tm), lambda i: (0, i)),
                  pl.BlockSpec((2 * half, d), lambda i: (0, 0))],
        out_specs=pl.BlockSpec((tm, d), lambda i: (i, 0)),
        out_shape=jax.ShapeDtypeStruct((t, d), F32),
        compiler_params=_params(("parallel",)),
    )(x, oa, ob_t, w)


def out_proj_bwd(dx, w):
    t, d = dx.shape
    half = w.shape[0] // 2
    tm = 512

    def body(dx_ref, w_ref, da_ref, db_ref, dbt_ref):
        dxb = dx_ref[...].astype(BF16)
        da_ref[...] = _dot_nt(dxb, w_ref[0:half, :]).astype(BF16)
        db_ref[...] = _dot_nt(dxb, w_ref[half:2 * half, :]).astype(BF16)
        dbt_ref[...] = _dot_nt(w_ref[half:2 * half, :], dxb).astype(BF16)

    return pl.pallas_call(
        body, name="out_proj_bwd", grid=(t // tm,),
        in_specs=[pl.BlockSpec((tm, d), lambda i: (i, 0)),
                  pl.BlockSpec((2 * half, d), lambda i: (0, 0))],
        out_specs=[pl.BlockSpec((tm, half), lambda i: (i, 0)),
                   pl.BlockSpec((tm, half), lambda i: (i, 0)),
                   pl.BlockSpec((half, tm), lambda i: (0, i))],
        out_shape=[jax.ShapeDtypeStruct((t, half), BF16), jax.ShapeDtypeStruct((t, half), BF16),
                   jax.ShapeDtypeStruct((half, t), BF16)],
        compiler_params=_params(("parallel",)),
    )(dx, w)


def _lane(shape):
    return lax.broadcasted_iota(jnp.int32, shape, 1)


def _seg_sum(z, mask):
    return jnp.sum(jnp.where(mask, z, 0.0), axis=-1, keepdims=True)


def _pair_norm(x):
    lo = _lane(x.shape) < HD
    x2 = x * x
    r = jnp.where(lo, lax.rsqrt(_seg_sum(x2, lo) * (1.0 / HD) + EPS),
                  lax.rsqrt(_seg_sum(x2, ~lo) * (1.0 / HD) + EPS))
    return lo, r, x * r


def _pair_norm_bwd(lo, r, xn, dyg):
    z = dyg * xn
    mean = jnp.where(lo, _seg_sum(z, lo), _seg_sum(z, ~lo)) * (1.0 / HD)
    return r * (dyg - xn * mean)


A_TM = 256


def prep1_fwd(proj, gq, gk, gcq, gckv):
    t = proj.shape[0]
    tm = A_TM

    def body(p_ref, gq_ref, gk_ref, gcq_ref, gckv_ref, qa_ref, ka_ref, va_ref, cq_ref, ckv_ref):
        for p in range(4):
            sl = slice(LANES * p, LANES * (p + 1))
            _, _, xn = _pair_norm(p_ref[:, sl])
            qa_ref[:, sl] = (xn * gq_ref[:, sl] * 0.125).astype(BF16)
            _, _, xn = _pair_norm(p_ref[:, 512 + LANES * p:512 + LANES * (p + 1)])
            ka_ref[:, sl] = (xn * gk_ref[:, sl]).astype(BF16)
        va_ref[...] = p_ref[:, 1024:1536].astype(BF16)
        _, xn = _rms(p_ref[:, 1536:1792])
        cq_ref[...] = (xn * gcq_ref[...]).astype(BF16)
        _, xn = _rms(p_ref[:, 1792:1920])
        ckv_ref[...] = (xn * gckv_ref[...]).astype(BF16)

    row = lambda w: pl.BlockSpec((tm, w), lambda i: (i, 0))
    vec = lambda w: pl.BlockSpec((1, w), lambda i: (0, 0))
    return pl.pallas_call(
        body, name="prep1_fwd", grid=(t // tm,),
        in_specs=[row(PROJ_W), vec(512), vec(512), vec(256), vec(128)],
        out_specs=[row(512), row(512), row(512), row(256), row(128)],
        out_shape=[jax.ShapeDtypeStruct((t, w), BF16) for w in (512, 512, 512, 256, 128)],
        compiler_params=_params(("parallel",)),
    )(proj, gq, gk, gcq, gckv)


def mix_bwd(proj, x, h, dres, w, g, dqa, dkp, dvp, dcq, dckv, dkr, gq, gk, gcq, gckv):
    t, d = x.shape
    tm = A_TM
    nb = t // tm

    def body(p_ref, x_ref, h_ref, dres_ref, w_ref, g_ref, dqa_ref, dk0_ref, dk1_ref, dk2_ref, dv0_ref, dv1_ref,
             dv2_ref, dcq_ref, dckv_ref, dkr_ref, gq_ref, gk_ref, gcq_ref, gckv_ref,
             dx_ref, dw_ref, dg_ref, dgq_ref, dgk_ref, dgcq_ref, dgckv_ref, dp_ref, acc_ref):
        i = pl.program_id(0)

        @pl.when(i == 0)
        def _():
            for ref in (acc_ref, dg_ref, dgq_ref, dgk_ref, dgcq_ref, dgckv_ref):
                ref[...] = jnp.zeros_like(ref)

        has1 = (i + 1 < nb).astype(F32)
        has2 = (i + 2 < nb).astype(F32)
        for p in range(4):
            sl = slice(LANES * p, LANES * (p + 1))
            lo, r, xn = _pair_norm(p_ref[:, sl])
            dy = dqa_ref[:, sl] * 0.125
            dp_ref[:, sl] = _pair_norm_bwd(lo, r, xn, dy * gq_ref[:, sl]).astype(BF16)
            dgq_ref[:, sl] += jnp.sum(dy * xn, axis=0, keepdims=True)
            ks = slice(512 + LANES * p, 512 + LANES * (p + 1))
            lo, r, xn = _pair_norm(p_ref[:, ks])
            dy = dk0_ref[0, :, sl] + has1 * dk1_ref[0, :, sl] + has2 * dk2_ref[0, :, sl]
            dp_ref[:, ks] = _pair_norm_bwd(lo, r, xn, dy * gk_ref[:, sl]).astype(BF16)
            dgk_ref[:, sl] += jnp.sum(dy * xn, axis=0, keepdims=True)
        dp_ref[:, 1024:1536] = (dv0_ref[0] + has1 * dv1_ref[0] + has2 * dv2_ref[0]).astype(BF16)
        for (a, b, dlat_ref, glat_ref, dglat_ref) in ((1536, 1792, dcq_ref, gcq_ref, dgcq_ref),
                                                      (1792, 1920, dckv_ref, gckv_ref, dgckv_ref)):
            r, xn = _rms(p_ref[:, a:b])
            dy = dlat_ref[...]
            dyg = dy * glat_ref[...]
            dp_ref[:, a:b] = (r * (dyg - xn * jnp.mean(dyg * xn, axis=-1, keepdims=True))).astype(BF16)
            dglat_ref[...] += jnp.sum(dy * xn, axis=0, keepdims=True)
        dp_ref[:, 1920:2048] = dkr_ref[...].astype(BF16)

        dproj = dp_ref[...]
        acc_ref[...] += _dot_tn(dproj, h_ref[...])
        dh = _dot(dproj, w_ref[...])
        r, xn = _rms(x_ref[...])
        dhg = dh * g_ref[...]
        dx_ref[...] = dres_ref[...] + r * (dhg - xn * jnp.mean(dhg * xn, axis=-1, keepdims=True))
        dg_ref[...] += jnp.sum(dh * xn, axis=0, keepdims=True)

        @pl.when(i == nb - 1)
        def _():
            dw_ref[...] = acc_ref[...].astype(BF16)

    row = lambda w_: pl.BlockSpec((tm, w_), lambda i: (i, 0))
    vec = lambda w_: pl.BlockSpec((1, w_), lambda i: (0, 0))
    part = lambda s: pl.BlockSpec((1, tm, 512), lambda i: (s, jnp.minimum(i + s, nb - 1), 0))
    whole = pl.BlockSpec((PROJ_W, d), lambda i: (0, 0))
    return pl.pallas_call(
        body, name="mix_bwd", grid=(nb,),
        in_specs=[row(PROJ_W), row(d), row(d), row(d), whole, vec(d), row(512), part(0), part(1), part(2),
                  part(0), part(1), part(2), row(256), row(128), row(128), vec(512), vec(512), vec(256), vec(128)],
        out_specs=[row(d), whole, vec(d), vec(512), vec(512), vec(256), vec(128)],
        out_shape=[jax.ShapeDtypeStruct((t, d), F32), jax.ShapeDtypeStruct((PROJ_W, d), BF16)]
        + [jax.ShapeDtypeStruct((1, w_), F32) for w_ in (d, 512, 512, 256, 128)],
        scratch_shapes=[pltpu.VMEM((tm, PROJ_W), BF16), pltpu.VMEM((PROJ_W, d), F32)],
        compiler_params=_params(("arbitrary",)),
    )(proj, x, h, dres, w, g, dqa, dkp, dkp, dkp, dvp, dvp, dvp, dcq, dckv, dkr, gq, gk, gcq, gckv)


def _roll(x, shift):
    return pltpu.roll(x, shift % LANES, 1)


def _rope(y, c, s1, s2):
    return y * c + _roll(y, -16) * s1 + _roll(y, 16) * s2


def _rope_bwd(d, c, s1, s2):
    return d * c + _roll(d * s1, 16) + _roll(d * s2, -16)


def _q_head_stats(x):
    lane = _lane(x.shape)
    mn = lane < HD
    mr = (lane >= HD) & (lane < HD + ROPE)
    x2 = x * x
    r = jnp.where(mn, lax.rsqrt(_seg_sum(x2, mn) * (1.0 / HD) + EPS),
                  lax.rsqrt(_seg_sum(x2, mr) * (1.0 / ROPE) + EPS))
    return mn, mr, r, x * r


def _kr_stats(x):
    r = lax.rsqrt(jnp.sum(x * x, axis=-1, keepdims=True) * (1.0 / ROPE) + EPS)
    return r, x * r


def _up_proj(cq_ref, ckv_ref, wuq_ref, wukv_ref):
    return _dot_nt(cq_ref[...], wuq_ref[...]), _dot(ckv_ref[...], wukv_ref[...])


def prep2_fwd(cqn, ckvn, proj, wuq, wukv, gq, gk, gkr, tabs):
    t = cqn.shape[0]
    tm = A_TM

    def body(cq_ref, ckv_ref, kr_ref, wuq_ref, wukv_ref, gq_ref, gk_ref, gkr_ref, tab_ref, qf_ref, kf_ref, vp_ref):
        q_all, kv_all = _up_proj(cq_ref, ckv_ref, wuq_ref, wukv_ref)
        _, xn = _kr_stats(kr_ref[...])
        kpe = _roll(_rope(xn * gkr_ref[...], tab_ref[3], tab_ref[4], tab_ref[5]), 64)
        for h in range(HEADS):
            sl = slice(LANES * h, LANES * (h + 1))
            _, _, _, xn = _q_head_stats(q_all[:, sl])
            qf_ref[:, sl] = (_rope(xn * gq_ref[...], tab_ref[0], tab_ref[1], tab_ref[2]) * B_SCALE2).astype(BF16)
            x = kv_all[:, sl]
            lo = _lane(x.shape) < HD
            xk = jnp.where(lo, x, 0.0)
            rk = lax.rsqrt(jnp.sum(xk * xk, axis=-1, keepdims=True) * (1.0 / HD) + EPS)
            kf_ref[:, sl] = (xk * rk * gk_ref[...] + kpe).astype(BF16)
            if h % 2 == 0:
                v_even = _roll(x, 64)
            else:
                vp_ref[:, LANES * (h // 2):LANES * (h // 2 + 1)] = jnp.where(lo, v_even, x).astype(BF16)

    row = lambda w: pl.BlockSpec((tm, w), lambda i: (i, 0))
    vec = lambda w: pl.BlockSpec((1, w), lambda i: (0, 0))
    full = lambda a: pl.BlockSpec(a.shape, lambda i: (0, 0))
    return pl.pallas_call(
        body, name="prep2_fwd", grid=(t // tm,),
        in_specs=[row(256), row(128), pl.BlockSpec((tm, LANES), lambda i: (i, 15)), full(wuq), full(wukv),
                  vec(128), vec(128), vec(128), pl.BlockSpec((6, tm, LANES), lambda i: (0, i, 0))],
        out_specs=[row(1024), row(1024), row(512)],
        out_shape=[jax.ShapeDtypeStruct((t, 1024), BF16), jax.ShapeDtypeStruct((t, 1024), BF16),
                   jax.ShapeDtypeStruct((t, 512), BF16)],
        compiler_params=_params(("parallel",)),
    )(cqn, ckvn, proj, wuq, wukv, gq, gk, gkr, tabs)


def prep2_bwd(cqn, ckvn, proj, wuq, wukv, dqf, dkf, dvp, gq, gk, gkr, tabs):
    t = cqn.shape[0]
    tm = A_TM

    def body(cq_ref, ckv_ref, kr_ref, wuq_ref, wukv_ref, dqf_ref, dkf_ref, dvp_ref, gq_ref, gk_ref, gkr_ref, tab_ref,
             dcq_ref, dckv_ref, dkr_ref, dwuq_ref, dwukv_ref, dgq_ref, dgk_ref, dgkr_ref, dq_ref, dkv_ref):
        i = pl.program_id(0)

        @pl.when(i == 0)
        def _():
            for ref in (dwuq_ref, dwukv_ref, dgq_ref, dgk_ref, dgkr_ref):
                ref[...] = jnp.zeros_like(ref)

        q_all, kv_all = _up_proj(cq_ref, ckv_ref, wuq_ref, wukv_ref)
        dgq = jnp.zeros((1, LANES), F32)
        dgk = jnp.zeros((1, LANES), F32)
        dkpe = jnp.zeros((tm, LANES), F32)
        for h in range(HEADS):
            sl = slice(LANES * h, LANES * (h + 1))
            mn, mr, r, xn = _q_head_stats(q_all[:, sl])
            dy = _rope_bwd(dqf_ref[sl, :].T, tab_ref[0], tab_ref[1], tab_ref[2])
            dyg = dy * gq_ref[...]
            z = dyg * xn
            mean = jnp.where(mn, _seg_sum(z, mn) * (1.0 / HD), _seg_sum(z, mr) * (1.0 / ROPE))
            dq_ref[:, sl] = (r * (dyg - xn * mean)).astype(BF16)
            dgq = dgq + jnp.sum(dy * xn, axis=0, keepdims=True)

            x = kv_all[:, sl]
            dk = dkf_ref[:, sl]
            xk = jnp.where(mn, x, 0.0)
            rk = lax.rsqrt(jnp.sum(xk * xk, axis=-1, keepdims=True) * (1.0 / HD) + EPS)
            xkn = xk * rk
            dyk = jnp.where(mn, dk, 0.0)
            dykg = dyk * gk_ref[...]
            dxk = rk * (dykg - xkn * (jnp.sum(dykg * xkn, axis=-1, keepdims=True) * (1.0 / HD)))
            dgk = dgk + jnp.sum(dyk * xkn, axis=0, keepdims=True)
            dkpe = dkpe + jnp.where(mr, dk, 0.0)
            dvpair = dvp_ref[:, LANES * (h // 2):LANES * (h // 2 + 1)]
            dv = _roll(dvpair, 64) if h % 2 == 0 else dvpair
            dkv_ref[:, sl] = jnp.where(mn, dxk, dv).astype(BF16)

        r, xn = _kr_stats(kr_ref[...])
        dy = _rope_bwd(_roll(dkpe, 64), tab_ref[3], tab_ref[4], tab_ref[5])
        dyg = dy * gkr_ref[...]
        dkr_ref[...] = r * (dyg - xn * (jnp.sum(dyg * xn, axis=-1, keepdims=True) * (1.0 / ROPE)))
        dgq_ref[...] += dgq
        dgk_ref[...] += dgk
        dgkr_ref[...] += jnp.sum(dy * xn, axis=0, keepdims=True)
        dqb, dkvb = dq_ref[...], dkv_ref[...]
        dcq_ref[...] = _dot(dqb, wuq_ref[...])
        dckv_ref[...] = _dot_nt(dkvb, wukv_ref[...])
        dwuq_ref[...] += _dot_tn(dqb, cq_ref[...])
        dwukv_ref[...] += _dot_tn(ckv_ref[...], dkvb)

    row = lambda w: pl.BlockSpec((tm, w), lambda i: (i, 0))
    vec = lambda w: pl.BlockSpec((1, w), lambda i: (0, 0))
    full = lambda a: pl.BlockSpec(a.shape, lambda i: (0, 0))
    return pl.pallas_call(
        body, name="prep2_bwd", grid=(t // tm,),
        in_specs=[row(256), row(128), pl.BlockSpec((tm, LANES), lambda i: (i, 15)), full(wuq), full(wukv),
                  pl.BlockSpec((1024, tm), lambda i: (0, i)), row(1024), row(512),
                  vec(128), vec(128), vec(128), pl.BlockSpec((6, tm, LANES), lambda i: (0, i, 0))],
        out_specs=[row(256), row(128), row(128), full(wuq), full(wukv), vec(128), vec(128), vec(128)],
        out_shape=[jax.ShapeDtypeStruct((t, 256), F32), jax.ShapeDtypeStruct((t, 128), F32),
                   jax.ShapeDtypeStruct((t, LANES), F32), jax.ShapeDtypeStruct(wuq.shape, F32),
                   jax.ShapeDtypeStruct(wukv.shape, F32)] + [jax.ShapeDtypeStruct((1, LANES), F32)] * 3,
        scratch_shapes=[pltpu.VMEM((tm, 1024), BF16), pltpu.VMEM((tm, 1024), BF16)],
        compiler_params=_params(("arbitrary",)),
    )(cqn, ckvn, proj, wuq, wukv, dqf, dkf, dvp, gq, gk, gkr, tabs)


A_TQ = 256
A_WIN = 3 * A_TQ


def _a_specs(t):
    nb = t // A_TQ
    blk = lambda s: pl.BlockSpec((A_TQ, 512), lambda i: (jnp.maximum(i - s, 0), 0))
    return nb, blk


def _a_exp(q_ref, kc, b_ref, head, sl, lo):
    hm = lo if head % 2 == 0 else ~lo
    qm = jnp.where(hm, q_ref[:, sl], jnp.zeros((), BF16))
    s = _dot_nt(qm, kc) + b_ref[0, head]
    e = jnp.exp(s - jnp.max(s, axis=-1, keepdims=True))
    return hm, qm, e, 1.0 / jnp.sum(e, axis=-1, keepdims=True)


def _a_bias_spec():
    return pl.BlockSpec((1, HEADS, A_TQ, A_WIN), lambda i: (jnp.minimum(i, 2), 0, 0, 0))


def attn_a_fwd(qa, ka, va, bias):
    t = qa.shape[0]
    nb, blk = _a_specs(t)

    def body(q_ref, k2_ref, k1_ref, k0_ref, v2_ref, v1_ref, v0_ref, b_ref, o_ref):
        lo = _lane((A_TQ, LANES)) < HD
        for p in range(4):
            sl = slice(LANES * p, LANES * (p + 1))
            kc = jnp.concatenate([k2_ref[:, sl], k1_ref[:, sl], k0_ref[:, sl]], axis=0)
            vc = jnp.concatenate([v2_ref[:, sl], v1_ref[:, sl], v0_ref[:, sl]], axis=0)
            outs = []
            for h2 in range(2):
                _, _, e, inv = _a_exp(q_ref, kc, b_ref, 2 * p + h2, sl, lo)
                outs.append(_dot(e.astype(BF16), vc) * inv)
            o_ref[:, sl] = jnp.where(lo, outs[0], outs[1]).astype(BF16)

    return pl.pallas_call(
        body, name="attn_a_fwd", grid=(nb,),
        in_specs=[blk(0), blk(2), blk(1), blk(0), blk(2), blk(1), blk(0), _a_bias_spec()],
        out_specs=pl.BlockSpec((A_TQ, 512), lambda i: (i, 0)),
        out_shape=jax.ShapeDtypeStruct((t, 512), BF16),
        compiler_params=_params(("parallel",)),
    )(qa, ka, ka, ka, va, va, va, bias)


def attn_a_bwd(qa, ka, va, bias, do):
    t = qa.shape[0]
    nb, blk = _a_specs(t)

    def body(q_ref, k2_ref, k1_ref, k0_ref, v2_ref, v1_ref, v0_ref, b_ref, do_ref, dq_ref, dk_ref, dv_ref, db_ref):
        qb = pl.program_id(0)

        @pl.when(qb == 0)
        def _():
            db_ref[...] = jnp.zeros_like(db_ref)

        lo = _lane((A_TQ, LANES)) < HD
        for p in range(4):
            sl = slice(LANES * p, LANES * (p + 1))
            kc = jnp.concatenate([k2_ref[:, sl], k1_ref[:, sl], k0_ref[:, sl]], axis=0)
            vc = jnp.concatenate([v2_ref[:, sl], v1_ref[:, sl], v0_ref[:, sl]], axis=0)
            dqs = []
            dkc = jnp.zeros((A_WIN, LANES), F32)
            dvc = jnp.zeros((A_WIN, LANES), F32)
            for h2 in range(2):
                head = 2 * p + h2
                hm, qm, e, inv = _a_exp(q_ref, kc, b_ref, head, sl, lo)
                pr = e * inv
                dom = jnp.where(hm, do_ref[:, sl], jnp.zeros((), BF16))
                dp = _dot_nt(dom, vc)
                ds = pr * (dp - jnp.sum(pr * dp, axis=-1, keepdims=True))
                db_ref[head] += ds
                dsb = ds.astype(BF16)
                dqs.append(_dot(dsb, kc))
                dkc = dkc + _dot_tn(dsb, qm)
                dvc = dvc + _dot_tn(pr.astype(BF16), dom)
            dq_ref[:, sl] = jnp.where(lo, dqs[0], dqs[1])
            for s in range(3):
                rows = slice(A_TQ * (2 - s), A_TQ * (3 - s))
                dk_ref[s, :, sl] = dkc[rows]
                dv_ref[s, :, sl] = dvc[rows]

    share = pl.BlockSpec((3, A_TQ, 512), lambda i: (0, i, 0))
    return pl.pallas_call(
        body, name="attn_a_bwd", grid=(nb,),
        in_specs=[blk(0), blk(2), blk(1), blk(0), blk(2), blk(1), blk(0), _a_bias_spec(), blk(0)],
        out_specs=[pl.BlockSpec((A_TQ, 512), lambda i: (i, 0)), share, share,
                   pl.BlockSpec((HEADS, A_TQ, A_WIN), lambda i: (0, 0, 0))],
        out_shape=[jax.ShapeDtypeStruct((t, 512), F32), jax.ShapeDtypeStruct((3, t, 512), F32),
                   jax.ShapeDtypeStruct((3, t, 512), F32), jax.ShapeDtypeStruct((HEADS, A_TQ, A_WIN), F32)],
        compiler_params=_params(("arbitrary",)),
    )(qa, ka, ka, ka, va, va, va, bias, do)


B_T = 1024


B_SCALE2 = B_SCALE * 1.4426950408889634
_B_ALL = slice(0, B_T)
_B_LO, _B_HI = slice(0, B_T // 2), slice(B_T // 2, B_T)
_B_DIAG = ((_B_LO, _B_LO), (_B_LO, _B_HI), (_B_HI, _B_HI))


def _tri_tables(n, by_query):
    pairs = [(i, j) for i in range(n) for j in range(i + 1)] if by_query else [(i, j) for j in range(n) for i in range(j, n)]
    return (np.asarray([p[0] for p in pairs], np.int32), np.asarray([p[1] for p in pairs], np.int32))


def _b_mask_t(s):
    kc = lax.broadcasted_iota(jnp.int32, s.shape, 0) // CHUNK
    qc = lax.broadcasted_iota(jnp.int32, s.shape, 1) // CHUNK
    return jnp.where(kc <= qc, s, NEG)


def attn_b_fwd(qf, kf, vp):
    t = qf.shape[0]
    n = t // B_T
    qtab, ktab = _tri_tables(n, by_query=True)

    def body(qt_ref, kt_ref, q_ref, k_ref, v_ref, o_ref, lse_ref, m_s, l_s, acc_s):
        qb, kb = qt_ref[pl.program_id(1)], kt_ref[pl.program_id(1)]

        @pl.when(kb == 0)
        def _():
            m_s[...] = jnp.full_like(m_s, NEG)
            l_s[...] = jnp.zeros_like(l_s)
            acc_s[...] = jnp.zeros_like(acc_s)

        def block(kr, qr, masked):
            v = v_ref[kr, :]
            for h2 in range(2):
                sl = slice(LANES * h2, LANES * (h2 + 1))
                s = _dot_nt(k_ref[kr, sl], q_ref[qr, sl])
                if masked:
                    s = _b_mask_t(s)
                m_prev = m_s[h2, :, qr]
                m_new = jnp.maximum(m_prev, jnp.max(s, axis=0, keepdims=True))
                alpha = jnp.exp2(m_prev - m_new)
                pr = jnp.exp2(s - m_new)
                l_s[h2, :, qr] = alpha * l_s[h2, :, qr] + jnp.sum(pr, axis=0, keepdims=True)
                acc_s[h2, :, qr] = alpha * acc_s[h2, :, qr] + _dot_tn(v, pr.astype(BF16))
                m_s[h2, :, qr] = m_new

        @pl.when(kb < qb)
        def _():
            block(_B_ALL, _B_ALL, False)

        @pl.when(kb == qb)
        def _():
            for kr, qr in _B_DIAG:
                block(kr, qr, kr == qr)
            for h2 in range(2):
                l = l_s[h2]
                rows = slice(HD * h2, HD * (h2 + 1))
                o_ref[rows, :] = (acc_s[h2, rows, :] * (1.0 / l)).astype(BF16)
                lse_ref[0, h2:h2 + 1, :] = m_s[h2] + jnp.log2(l)

    grid_spec = pltpu.PrefetchScalarGridSpec(
        num_scalar_prefetch=2, grid=(4, len(qtab)),
        in_specs=[pl.BlockSpec((B_T, 256), lambda p, s, qt, kt: (qt[s], p)),
                  pl.BlockSpec((B_T, 256), lambda p, s, qt, kt: (kt[s], p)),
                  pl.BlockSpec((B_T, LANES), lambda p, s, qt, kt: (kt[s], p))],
        out_specs=[pl.BlockSpec((LANES, B_T), lambda p, s, qt, kt: (p, qt[s])),
                   pl.BlockSpec((1, 2, B_T), lambda p, s, qt, kt: (p, 0, qt[s]))],
        scratch_shapes=[pltpu.VMEM((2, 1, B_T), F32), pltpu.VMEM((2, 1, B_T), F32), pltpu.VMEM((2, LANES, B_T), F32)])
    return pl.pallas_call(
        body, name="attn_b_fwd", grid_spec=grid_spec,
        out_shape=[jax.ShapeDtypeStruct((512, t), BF16), jax.ShapeDtypeStruct((4, 2, t), F32)],
        compiler_params=_params(("parallel", "arbitrary")),
    )(jnp.asarray(qtab), jnp.asarray(ktab), qf, kf, vp)


def attn_b_bwd(qf, kf, vp, do, do_t, o_t, lse, scatter=()):
    t = qf.shape[0]
    n = t // B_T
    qtab, ktab = _tri_tables(n, by_query=False)
    plan = ScatterPlan(scatter)
    m = plan.n
    last = len(qtab) - 1

    def body(*refs):
        qt_ref, kt_ref, q_ref, k_ref, v_ref, do_ref, dot_ref, ot_ref, lse_ref = refs[:9]
        ins, (dq_ref, dk_ref, dv_ref), outs = refs[9:9 + m], refs[9 + m:12 + m], refs[12 + m:12 + 2 * m]
        sems = refs[12 + 2 * m:]
        qb, kb = qt_ref[pl.program_id(1)], kt_ref[pl.program_id(1)]
        if m:
            pl.when((pl.program_id(0) == 0) & (pl.program_id(1) == 0))(lambda: plan.start(ins, outs, sems))

        @pl.when(pl.program_id(1) == 0)
        def _():
            dq_ref[...] = jnp.zeros_like(dq_ref)

        @pl.when(qb == kb)
        def _():
            dk_ref[...] = jnp.zeros_like(dk_ref)
            dv_ref[...] = jnp.zeros_like(dv_ref)

        def block(kr, qr, masked):
            nq = qr.stop - qr.start
            cols = pl.ds(pl.multiple_of(qb * B_T + qr.start, LANES), nq)
            v = v_ref[kr, :]
            dov = do_ref[qr, :]
            prod = dot_ref[:, qr].astype(F32) * ot_ref[:, qr].astype(F32)
            lo = _lane((nq, LANES)) < HD
            for h2 in range(2):
                sl = slice(LANES * h2, LANES * (h2 + 1))
                hm = lo if h2 == 0 else ~lo
                q = q_ref[qr, sl]
                k = k_ref[kr, sl]
                dom = jnp.where(hm, dov, jnp.zeros((), BF16))
                delta = jnp.sum(prod[HD * h2:HD * (h2 + 1), :], axis=0, keepdims=True)
                s = _dot_nt(k, q)
                if masked:
                    s = _b_mask_t(s)
                pr = jnp.exp2(s - lse_ref[0, h2:h2 + 1, qr])
                dp = _dot_nt(v, dom)
                ds = (pr * (dp - delta)).astype(BF16)
                dk_ref[kr, sl] += _dot(ds, q) * (B_SCALE / B_SCALE2)
                dv_ref[kr, :] += _dot(pr.astype(BF16), dom)
                dq_ref[sl, cols] += _dot_tn(k, ds) * B_SCALE

        @pl.when(qb > kb)
        def _():
            block(_B_ALL, _B_ALL, False)

        @pl.when(qb == kb)
        def _():
            for kr, qr in _B_DIAG:
                block(kr, qr, kr == qr)

        if m:
            pl.when((pl.program_id(0) == 3) & (pl.program_id(1) == last))(lambda: plan.finish(ins, outs, sems))

    qrow = lambda w: pl.BlockSpec((B_T, w), lambda p, s, qt, kt: (qt[s], p))
    qcol = pl.BlockSpec((LANES, B_T), lambda p, s, qt, kt: (p, qt[s]))
    krow = lambda w: pl.BlockSpec((B_T, w), lambda p, s, qt, kt: (kt[s], p))
    grid_spec = pltpu.PrefetchScalarGridSpec(
        num_scalar_prefetch=2, grid=(4, len(qtab)),
        in_specs=[qrow(256), krow(256), krow(LANES), qrow(LANES), qcol, qcol,
                  pl.BlockSpec((1, 2, B_T), lambda p, s, qt, kt: (p, 0, qt[s]))] + [ANY] * m,
        out_specs=[pl.BlockSpec((256, t), lambda p, s, qt, kt: (p, 0)), krow(256), krow(LANES)] + [ANY] * m,
        scratch_shapes=plan.scratch if m else [])
    return pl.pallas_call(
        body, name="attn_b_bwd", grid_spec=grid_spec,
        out_shape=[jax.ShapeDtypeStruct((1024, t), F32), jax.ShapeDtypeStruct((t, 1024), F32),
                   jax.ShapeDtypeStruct((t, 512), F32)] + plan.out_shape,
        compiler_params=_params(("arbitrary", "arbitrary")),
    )(jnp.asarray(qtab), jnp.asarray(ktab), qf, kf, vp, do, do_t, o_t, lse, *scatter)


_U_LEN = A_TQ + A_WIN - 1


def _band_mask():
    a = np.arange(A_TQ)[:, None] // CHUNK
    b = np.arange(A_WIN)[None, :] // CHUNK
    return (b >= a) & (b <= a + A_LEFT)


def bias_block(table):
    h = table.shape[0]
    n_lo = A_WIN - 1 - 2 * A_TQ - A_MAX_REL
    ext = jnp.concatenate([jnp.repeat(table[:, :1], n_lo, axis=1), table,
                           jnp.repeat(table[:, -1:], _U_LEN - n_lo - table.shape[1], axis=1)], axis=1)
    row = jnp.pad(ext[:, ::-1], ((0, 0), (0, 1)))
    flat = jnp.tile(row, (1, A_TQ))[:, :A_TQ * _U_LEN]
    skew = flat.reshape(h, A_TQ, _U_LEN)
    toep = skew[:, :, A_TQ - 1:A_TQ - 1 + A_WIN]
    band = _band_mask()
    first = [band & (np.arange(A_WIN)[None, :] >= 2 * A_TQ - A_TQ * v) for v in range(3)]
    return jnp.where(jnp.asarray(np.stack(first))[:, None], toep[None], NEG)


def bias_block_grad(db):
    h = db.shape[0]
    n_lo = A_WIN - 1 - 2 * A_TQ - A_MAX_REL
    skew = jnp.pad(db, ((0, 0), (0, 0), (A_TQ - 1, 0)))
    flat = jnp.pad(skew.reshape(h, A_TQ * _U_LEN), ((0, 0), (0, A_TQ)))
    ext = jnp.sum(flat.reshape(h, A_TQ, _U_LEN + 1), axis=1)[:, :_U_LEN][:, ::-1]
    n_tab = 2 * A_MAX_REL + 1
    first = jnp.sum(ext[:, :n_lo + 1], axis=1, keepdims=True)
    last = jnp.sum(ext[:, n_lo + n_tab - 1:], axis=1, keepdims=True)
    return jnp.concatenate([first, ext[:, n_lo + 1:n_lo + n_tab - 1], last], axis=1)


def rope_tabs(t):
    inv = 1.0 / (10000.0 ** (jnp.arange(0, ROPE, 2, dtype=F32) / ROPE))
    ang = jnp.arange(t, dtype=F32)[:, None] * inv[None, :]
    cos, sin = jnp.cos(ang), jnp.sin(ang)
    z = lambda w: jnp.zeros((t, w), F32)
    ck = jnp.concatenate([cos, cos, z(96)], axis=1)
    s1k = jnp.concatenate([-sin, z(112)], axis=1)
    s2k = jnp.concatenate([z(16), sin, z(96)], axis=1)
    cq = jnp.concatenate([jnp.ones((t, HD), F32), cos, cos, z(32)], axis=1)
    s1q = jnp.concatenate([z(HD), -sin, z(48)], axis=1)
    s2q = jnp.concatenate([z(HD + 16), sin, z(32)], axis=1)
    return jnp.stack([cq, s1q, s2q, ck, s1k, s2k])


def _pad_lanes(v, width):
    return jnp.pad(v, ((0, 0), (0, width - v.shape[1])))


LATE = ("w_in", "b_w_uq", "b_w_ukv", "w_out", "ffn2_w_gate", "ffn2_w_up", "ffn2_w_down")
FFN2 = ("ffn2_w_gate", "ffn2_w_up", "ffn2_w_down")


def kernel_layout(gathered):
    w = {n: v for n, v in gathered.items() if n.startswith("ffn")}
    if "w_in" in gathered:
        w["w_in"] = jnp.pad(gathered["w_in"].reshape(IN_COLS, D_MODEL), ((0, PROJ_W - IN_COLS), (0, 0)))
        uq = gathered["b_w_uq"].reshape(HEADS, HD + ROPE, 256)
        w["b_w_uq"] = jnp.pad(uq, ((0, 0), (0, LANES - HD - ROPE), (0, 0))).reshape(HEADS * LANES, 256)
        w["b_w_ukv"] = _shards_to_cols(gathered["b_w_ukv"])
        w["w_out"] = gathered["w_out"].reshape(N_SHARD * gathered["w_out"].shape[1], D_MODEL)
    return w


def local_step(x, target, w, late=None):
    t = x.shape[0]
    gq = jnp.tile(w["a_q_norm"], (1, HEADS))
    gk = jnp.tile(w["a_k_norm"], (1, HEADS))
    gq128 = _pad_lanes(jnp.concatenate([w["b_q_nope_norm"], w["b_q_rope_norm"]], axis=1), LANES)
    gk128 = _pad_lanes(w["b_k_nope_norm"], LANES)
    gkr128 = _pad_lanes(w["b_k_rope_norm"], LANES)
    tabs = rope_tabs(t)
    bias = bias_block(w["a_rel_bias"])

    if late is None:
        x1 = ffn_fwd(x, w["ffn1_norm"], w["ffn1_w_gate"], w["ffn1_w_up"], w["ffn1_w_down"], "ffn_fwd")
    else:
        own, shard = late
        x1, *got = ffn_fwd(x, w["ffn1_norm"], w["ffn1_w_gate"], w["ffn1_w_up"], w["ffn1_w_down"], "ffn_fwd_gather",
                           gather=own)
        w = dict(w, **kernel_layout({n: lax.dynamic_update_index_in_dim(g_, o_, shard, 0)
                                     for n, g_, o_ in zip(LATE, got, own)}))
    h, proj = mix_proj(x1, w["mix_norm"], w["w_in"])
    qa, ka, va, cqn, ckvn = prep1_fwd(proj, gq, gk, w["b_q_lat_norm"], w["b_kv_lat_norm"])
    qf, kf, vp = prep2_fwd(cqn, ckvn, proj, w["b_w_uq"], w["b_w_ukv"], gq128, gk128, gkr128, tabs)
    oa = attn_a_fwd(qa, ka, va, bias)
    ob_t, lse = attn_b_fwd(qf, kf, vp)
    x2 = out_proj(x1, oa, ob_t, w["w_out"])
    x3 = ffn_fwd(x2, w["ffn2_norm"], w["ffn2_w_gate"], w["ffn2_w_up"], w["ffn2_w_down"], "ffn_fwd")

    g = {}
    dx3, g["final_norm"], loss = final_loss(x3, w["final_norm"], target)
    g["ffn2_w_gate"], g["ffn2_w_up"], g["ffn2_w_down"], dhp = ffn_bwd(
        x2, dx3, w["ffn2_norm"], w["ffn2_w_gate"], w["ffn2_w_up"], w["ffn2_w_down"], "ffn_bwd")
    dx2, g["ffn2_norm"] = norm_bwd(x2, w["ffn2_norm"], dhp, dx3, "ffn_norm_bwd")
    d_oa, d_ob, d_ob_t = out_proj_bwd(dx2, w["w_out"])
    g["w_out"] = jnp.concatenate([matmul(oa, dx2, "tn", "w_out_a_bwd", 512, 1024, 512, BF16),
                                  matmul(ob_t, dx2, "nn", "w_out_b_bwd", 512, 1024, 512, BF16)], axis=0)
    early = [g[n] for n in FFN2] + [g["w_out"].reshape(N_SHARD, -1, D_MODEL)]
    dqf, dkf, dvp, *landed_early = attn_b_bwd(qf, kf, vp, d_ob, d_ob_t, ob_t, lse,
                                              scatter=() if late is None else early)
    dqa, dkp, dvpa, dbias = attn_a_bwd(qa, ka, va, bias, d_oa)
    dcq, dckv, dkr, dwuq, dwukv, dgq128, dgk128, dgkr128 = prep2_bwd(
        cqn, ckvn, proj, w["b_w_uq"], w["b_w_ukv"], dqf, dkf, dvp, gq128, gk128, gkr128, tabs)
    g["b_w_uq"], g["b_w_ukv"] = dwuq.astype(BF16), dwukv.astype(BF16)
    dx1, g["w_in"], g["mix_norm"], dgq, dgk, g["b_q_lat_norm"], g["b_kv_lat_norm"] = mix_bwd(
        proj, x1, h, dx2, w["w_in"], w["mix_norm"], dqa, dkp, dvpa, dcq, dckv, dkr, gq, gk,
        w["b_q_lat_norm"], w["b_kv_lat_norm"])
    mid = [g["w_in"][:IN_COLS].reshape(N_SHARD, IN_COLS // N_SHARD, D_MODEL),
           g["b_w_uq"].reshape(HEADS, LANES, 256)[:, :HD + ROPE].reshape(N_SHARD, -1, 256),
           _cols_to_shards(g["b_w_ukv"])]
    g["ffn1_w_gate"], g["ffn1_w_up"], g["ffn1_w_down"], dhp, *landed_late = ffn_bwd(
        x, dx1, w["ffn1_norm"], w["ffn1_w_gate"], w["ffn1_w_up"], w["ffn1_w_down"],
        "ffn_bwd" if late is None else "ffn_bwd_scatter", scatter=() if late is None else mid,
        spread=None if late is None else late[1])
    grad_x, g["ffn1_norm"] = norm_bwd(x, w["ffn1_norm"], dhp, dx1, "ffn_norm_bwd")
    landed = dict(zip(FFN2 + ("w_out", "w_in", "b_w_uq", "b_w_ukv", "ffn1_w_gate", "ffn1_w_up", "ffn1_w_down"),
                      landed_early + landed_late))

    g["a_q_norm"] = jnp.sum(dgq.reshape(HEADS, HD), axis=0, keepdims=True)
    g["a_k_norm"] = jnp.sum(dgk.reshape(HEADS, HD), axis=0, keepdims=True)
    g["a_rel_bias"] = bias_block_grad(dbias)
    g["b_q_nope_norm"] = dgq128[:, :HD]
    g["b_q_rope_norm"] = dgq128[:, HD:HD + ROPE]
    g["b_k_nope_norm"] = dgk128[:, :HD]
    g["b_k_rope_norm"] = dgkr128[:, :ROPE]
    return loss, grad_x, g, landed


ANY = pl.BlockSpec(memory_space=pl.ANY)
N_DEV = 8


def _place():
    return lax.axis_index("x"), lax.axis_index("y"), lax.axis_index("c")


def _flip(v, bit):
    return 1 - v if bit else v


BF16_ROWS = 16


def _split_axis(shape):
    return 0 if (shape[0] // 2) % BF16_ROWS == 0 else 1


def _half_shape(shape):
    axis = _split_axis(shape)
    return tuple(s // 2 if a == axis else s for a, s in enumerate(shape))


def _half(shape, core):
    axis = _split_axis(shape)
    size = shape[axis] // 2
    return tuple(pl.ds(core * size, size) if a == axis else slice(None) for a in range(2))


class GatherPlan:
    def __init__(self, ws):
        self.shapes = [w.shape for w in ws]
        self.n = len(ws)
        self.out_shape = [jax.ShapeDtypeStruct((N_SHARD,) + w.shape, w.dtype) for w in ws]
        self.scratch = [pltpu.SemaphoreType.DMA((6 * self.n,)), pltpu.SemaphoreType.DMA((6 * self.n,))]

    def _copies(self, ins, outs, sems):
        x, y, c = _place()
        s_me = 2 * x + y
        sibling = (x, y, 1 - c)
        send_sems, recv_sems = sems

        def remote(k, src, dst, to):
            return pltpu.make_async_remote_copy(src_ref=src, dst_ref=dst, send_sem=send_sems.at[k],
                                                recv_sem=recv_sems.at[k], device_id=to, device_id_type=MESH)

        ici, fwd = [], []
        for a in range(self.n):
            mine, theirs = _half(self.shapes[a], c), _half(self.shapes[a], 1 - c)
            for j, (cx, cy) in enumerate([(1 - x, y), (x, 1 - y), (1 - x, 1 - y)]):
                got = outs[a].at[(2 * cx + cy,) + mine]
                ici.append((remote(6 * a + j, ins[a].at[mine], outs[a].at[(s_me,) + mine], (cx, cy, c)),
                            remote(6 * a + j, got, got, (cx, cy, c))))
                passed = outs[a].at[(2 * cx + cy,) + theirs]
                fwd.append((remote(6 * a + 3 + j, got, got, sibling), remote(6 * a + 3 + j, passed, passed, sibling)))
        return ici, fwd

    def start(self, ins, outs, sems):
        for send, _ in self._copies(ins, outs, sems)[0]:
            send.start()

    def forward(self, ins, outs, sems):
        ici, fwd = self._copies(ins, outs, sems)
        for (_, arrival), (send, _) in zip(ici, fwd):
            arrival.wait_recv()
            send.start()

    def finish(self, ins, outs, sems):
        ici, fwd = self._copies(ins, outs, sems)
        for _, arrival in fwd:
            arrival.wait_recv()
        for send, _ in ici + fwd:
            send.wait_send()


def allgather_shards(ws):
    plan = GatherPlan(ws)
    n = plan.n

    def body(*refs):
        ins, outs, sems = refs[:n], refs[n:2 * n], refs[2 * n:]
        plan.start(ins, outs, sems)
        plan.forward(ins, outs, sems)
        plan.finish(ins, outs, sems)

    return pl.pallas_call(
        body, name="allgather_shards", in_specs=[ANY] * n, out_specs=[ANY] * n,
        out_shape=plan.out_shape, scratch_shapes=plan.scratch,
    )(*ws)


class ScatterPlan:
    def __init__(self, gs):
        self.shapes = [g.shape[1:] for g in gs]
        self.n = len(gs)
        self.out_shape = [jax.ShapeDtypeStruct((N_DEV,) + _half_shape(g.shape[1:]), g.dtype) for g in gs]
        self.scratch = [pltpu.SemaphoreType.DMA((7 * self.n,)), pltpu.SemaphoreType.DMA((7 * self.n,)),
                        pltpu.SemaphoreType.DMA((self.n,))]

    def _copies(self, ins, outs, sems):
        x, y, c = _place()
        me = 4 * x + 2 * y + c
        send_sems, recv_sems, local_sems = sems
        local, sends, arrivals = [], [], []
        for a in range(self.n):
            piece = lambda px, py, pc, a=a: ins[a].at[(2 * px + py,) + _half(self.shapes[a], pc)]
            local.append(pltpu.make_async_copy(piece(x, y, c), outs[a].at[me], local_sems.at[a]))
            for k in range(1, N_DEV):
                px, py, pc = _flip(x, k & 4), _flip(y, k & 2), _flip(c, k & 1)
                sem = dict(send_sem=send_sems.at[7 * a + k - 1], recv_sem=recv_sems.at[7 * a + k - 1],
                           device_id=(px, py, pc), device_id_type=MESH)
                sends.append(pltpu.make_async_remote_copy(
                    src_ref=piece(px, py, pc), dst_ref=outs[a].at[me], **sem))
                slot = outs[a].at[4 * px + 2 * py + pc]
                arrivals.append(pltpu.make_async_remote_copy(src_ref=slot, dst_ref=slot, **sem))
        return local, sends, arrivals

    def start(self, ins, outs, sems):
        local, sends, _ = self._copies(ins, outs, sems)
        for cp in local + sends:
            cp.start()

    def finish(self, ins, outs, sems):
        local, sends, arrivals = self._copies(ins, outs, sems)
        for cp in arrivals:
            cp.wait_recv()
        for cp in sends:
            cp.wait_send()
        for cp in local:
            cp.wait()


def scatter_partials(gs):
    plan = ScatterPlan(gs)
    n = plan.n

    def body(*refs):
        ins, outs, sems = refs[:n], refs[n:2 * n], refs[2 * n:]
        plan.start(ins, outs, sems)
        plan.finish(ins, outs, sems)

    return pl.pallas_call(
        body, name="scatter_partials", in_specs=[ANY] * n, out_specs=[ANY] * n,
        out_shape=plan.out_shape, scratch_shapes=plan.scratch,
    )(*gs)


def sum_slots(land, name):
    _, rows, cols = land.shape
    tr = rows // 2 if rows > 128 and (rows // 2) % BF16_ROWS == 0 else rows

    def body(l_ref, o_ref):
        acc = l_ref[0].astype(F32)
        for s in range(1, N_DEV):
            acc = acc + l_ref[s].astype(F32)
        o_ref[...] = acc

    return pl.pallas_call(
        body, name=name, grid=(rows // tr,),
        in_specs=[pl.BlockSpec((N_DEV, tr, cols), lambda i: (0, i, 0))],
        out_specs=pl.BlockSpec((tr, cols), lambda i: (i, 0)),
        out_shape=jax.ShapeDtypeStruct((rows, cols), F32),
        compiler_params=_params(("parallel",)),
    )(land)


def join_halves(hs, shapes):
    n = len(hs)

    def body(*refs):
        ins, outs = refs[:n], refs[n:2 * n]
        send_sems, recv_sems = refs[2 * n:]
        x, y, c = _place()
        sends = []
        for a in range(n):
            mine = outs[a].at[_half(shapes[a], c)]
            sends.append(pltpu.make_async_remote_copy(
                src_ref=ins[a], dst_ref=mine, send_sem=send_sems.at[a], recv_sem=recv_sems.at[a],
                device_id=(x, y, 1 - c), device_id_type=MESH))
            sends[-1].start()
        for a in range(n):
            theirs = outs[a].at[_half(shapes[a], 1 - c)]
            pltpu.make_async_remote_copy(
                src_ref=theirs, dst_ref=theirs, send_sem=send_sems.at[a], recv_sem=recv_sems.at[a],
                device_id=(x, y, 1 - c), device_id_type=MESH).wait_recv()
        for cp in sends:
            cp.wait_send()

    return pl.pallas_call(
        body, name="join_halves",
        in_specs=[ANY] * n, out_specs=[ANY] * n,
        out_shape=[jax.ShapeDtypeStruct(tuple(s), h.dtype) for s, h in zip(shapes, hs)],
        scratch_shapes=[pltpu.SemaphoreType.DMA((n,)), pltpu.SemaphoreType.DMA((n,))],
    )(*hs)


def allreduce_small(vec):
    def body(v_ref, o_ref, land_ref, send_sems, recv_sems):
        x, y, c = _place()
        me = 4 * x + 2 * y + c
        land_ref[me] = v_ref[...]
        sends = []
        for k in range(1, N_DEV):
            px, py, pc = _flip(x, k & 4), _flip(y, k & 2), _flip(c, k & 1)
            sends.append(pltpu.make_async_remote_copy(
                src_ref=v_ref, dst_ref=land_ref.at[me], send_sem=send_sems.at[k - 1], recv_sem=recv_sems.at[k - 1],
                device_id=(px, py, pc), device_id_type=MESH))
            sends[-1].start()
        for k in range(1, N_DEV):
            px, py, pc = _flip(x, k & 4), _flip(y, k & 2), _flip(c, k & 1)
            slot = land_ref.at[4 * px + 2 * py + pc]
            pltpu.make_async_remote_copy(
                src_ref=slot, dst_ref=slot, send_sem=send_sems.at[k - 1], recv_sem=recv_sems.at[k - 1],
                device_id=(px, py, pc), device_id_type=MESH).wait_recv()
        for cp in sends:
            cp.wait_send()
        acc = land_ref[0]
        for s in range(1, N_DEV):
            acc = acc + land_ref[s]
        o_ref[...] = acc

    vm = pl.BlockSpec(memory_space=pltpu.VMEM)
    return pl.pallas_call(
        body, name="allreduce_small",
        in_specs=[vm], out_specs=vm,
        out_shape=jax.ShapeDtypeStruct(vec.shape, F32),
        scratch_shapes=[pltpu.VMEM((N_DEV,) + vec.shape, F32), pltpu.SemaphoreType.DMA((N_DEV - 1,)),
                        pltpu.SemaphoreType.DMA((N_DEV - 1,))],
    )(vec)


def adamw(w, g, m, v, name):
    rows, cols = w.shape
    tr = rows
    while tr * cols * 4 * 14 > 24 * 1024 * 1024 and tr % 16 == 0:
        tr //= 2
    c1 = 1.0 - ADAM_B1 ** ADAM_STEP
    c2 = 1.0 - ADAM_B2 ** ADAM_STEP

    def body(w_ref, g_ref, m_ref, v_ref, d_ref, nm_ref, nv_ref):
        gv = g_ref[...]
        nm = ADAM_B1 * m_ref[...] + (1.0 - ADAM_B1) * gv
        nv = ADAM_B2 * v_ref[...] + (1.0 - ADAM_B2) * (gv * gv)
        nm_ref[...] = nm
        nv_ref[...] = nv
        d_ref[...] = -ADAM_LR * ((nm / c1) / (jnp.sqrt(nv / c2) + ADAM_EPS) + ADAM_WD * w_ref[...])

    blk = pl.BlockSpec((tr, cols), lambda i: (i, 0))
    return pl.pallas_call(
        body, name=name, grid=(rows // tr,),
        in_specs=[blk] * 4, out_specs=[blk] * 3,
        out_shape=[jax.ShapeDtypeStruct((rows, cols), F32)] * 3,
        compiler_params=_params(("parallel",)),
    )(w, g, m, v)


BIG = ("ffn1_w_gate", "ffn1_w_up", "ffn1_w_down", "w_in", "b_w_uq", "b_w_ukv", "w_out",
       "ffn2_w_gate", "ffn2_w_up", "ffn2_w_down")
SMALL = ("ffn1_norm", "mix_norm", "a_q_norm", "a_k_norm", "a_rel_bias", "b_q_lat_norm", "b_kv_lat_norm",
         "b_q_nope_norm", "b_q_rope_norm", "b_k_nope_norm", "b_k_rope_norm", "ffn2_norm", "final_norm")
WEIGHTS = ("ffn1_norm", "ffn1_w_gate", "ffn1_w_up", "ffn1_w_down", "mix_norm", "w_in", "a_q_norm", "a_k_norm",
           "a_rel_bias", "b_q_lat_norm", "b_w_uq", "b_kv_lat_norm", "b_w_ukv", "b_q_nope_norm", "b_q_rope_norm",
           "b_k_nope_norm", "b_k_rope_norm", "w_out", "ffn2_norm", "ffn2_w_gate", "ffn2_w_up", "ffn2_w_down",
           "final_norm")
TRANSPOSED = ("ffn1_w_gate", "ffn1_w_up", "ffn2_w_gate", "ffn2_w_up", "w_in", "b_w_uq")
PACK_SHAPE = (8, 1024)


def _pack_small(d, last=None):
    flat = [d[n].reshape(-1) for n in SMALL]
    used = sum(f.shape[0] for f in flat)
    total = PACK_SHAPE[0] * PACK_SHAPE[1]
    tail = jnp.zeros((total - used - 1,), F32)
    end = jnp.zeros((1,), F32) if last is None else last.reshape(1)
    return jnp.concatenate(flat + [tail, end]).reshape(PACK_SHAPE)


def _unpack_small(p, like):
    flat = p.reshape(-1)
    out, off = {}, 0
    for n in SMALL:
        size = like[n].size
        out[n] = flat[off:off + size].reshape(like[n].shape)
        off += size
    return out, flat[-1]


def _cols_to_shards(g):
    rows, cols = g.shape
    return g.reshape(rows, N_SHARD, cols // N_SHARD).transpose(1, 0, 2)


def _shards_to_cols(g):
    return g.transpose(1, 0, 2).reshape(g.shape[1], -1)


def kernel(x, ffn1_norm, ffn1_w_gate, ffn1_w_up, ffn1_w_down, mix_norm, w_in, a_q_norm, a_k_norm, a_rel_bias, b_q_lat_norm, b_w_uq, b_kv_lat_norm, b_w_ukv, b_q_nope_norm, b_q_rope_norm, b_k_nope_norm, b_k_rope_norm, w_out, ffn2_norm, ffn2_w_gate, ffn2_w_up, ffn2_w_down, final_norm, loss_target, m_ffn1_norm, m_ffn1_w_gate, m_ffn1_w_up, m_ffn1_w_down, m_mix_norm, m_w_in, m_a_q_norm, m_a_k_norm, m_a_rel_bias, m_b_q_lat_norm, m_b_w_uq, m_b_kv_lat_norm, m_b_w_ukv, m_b_q_nope_norm, m_b_q_rope_norm, m_b_k_nope_norm, m_b_k_rope_norm, m_w_out, m_ffn2_norm, m_ffn2_w_gate, m_ffn2_w_up, m_ffn2_w_down, m_final_norm, v_ffn1_norm, v_ffn1_w_gate, v_ffn1_w_up, v_ffn1_w_down, v_mix_norm, v_w_in, v_a_q_norm, v_a_k_norm, v_a_rel_bias, v_b_q_lat_norm, v_b_w_uq, v_b_kv_lat_norm, v_b_w_ukv, v_b_q_nope_norm, v_b_q_rope_norm, v_b_k_nope_norm, v_b_k_rope_norm, v_w_out, v_ffn2_norm, v_ffn2_w_gate, v_ffn2_w_up, v_ffn2_w_down, v_final_norm):
    args = locals()
    view = lambda a, n: a[0].T if n in TRANSPOSED else a[0]
    wts = {n: view(args[n], n) for n in WEIGHTS}
    mom = {n: view(args["m_" + n], n) for n in WEIGHTS}
    var = {n: view(args["v_" + n], n) for n in WEIGHTS}

    shard = 2 * lax.axis_index("x") + lax.axis_index("y")
    core = lax.axis_index("c")
    first = [n for n in BIG if n not in LATE]
    own = [wts[n].astype(BF16) for n in first]
    w = {n: wts[n] if n == "a_rel_bias" else wts[n][None] for n in SMALL}
    w.update(kernel_layout({n: lax.dynamic_update_index_in_dim(got, mine, shard, 0)
                            for n, got, mine in zip(first, allgather_shards(own), own)}))

    loss, grad_x, g, landed = local_step(x[0], loss_target[0], w,
                                         late=([wts[n].astype(BF16) for n in LATE], shard))

    me = 2 * shard + core
    for n in first:
        piece = lax.dynamic_slice(g[n], (shard, core * (FS // 2), 0), (1, FS // 2, D_MODEL))
        landed[n] = lax.dynamic_update_slice(landed[n], piece, (me, 0, 0))
    halves = [sum_slots(landed[n], "sum_slots") for n in BIG]
    shapes = [wts[n].shape for n in BIG]
    axes = [_split_axis(s) for s in shapes]
    grads = dict(zip(BIG, (lax.dynamic_update_slice_in_dim(got, mine, core * mine.shape[ax], ax)
                           for got, mine, ax in zip(join_halves(halves, shapes), halves, axes))))

    small_sum, loss_sum = _unpack_small(allreduce_small(_pack_small(g, loss[0, 0])), wts)
    grads.update(small_sum)

    delta, new_m, new_v = {}, {}, {}
    for n in BIG:
        delta[n], new_m[n], new_v[n] = adamw(wts[n], grads[n], mom[n], var[n], "adamw")
    packed = adamw(_pack_small(wts), _pack_small(grads), _pack_small(mom), _pack_small(var), "adamw_small")
    for dst, p in zip((delta, new_m, new_v), packed):
        dst.update(_unpack_small(p, wts)[0])

    lead = lambda d: [(d[n].T if n in TRANSPOSED else d[n])[None] for n in WEIGHTS]
    return (loss_sum, grad_x[None], *lead(grads), *lead(delta), *lead(new_m), *lead(new_v))
```

```python
import functools
import math

import numpy as np
import jax
import jax.numpy as jnp
from jax import lax
from jax.experimental import pallas as pl
from jax.experimental.pallas import tpu as pltpu

F32 = jnp.float32
BF16 = jnp.bfloat16
EPS = 1e-6
NEG = -1e30

D_MODEL = 1024
D_FF = 2816
N_SHARD = 4
FS = D_FF // N_SHARD
CHUNK = 64
A_LEFT = 8
A_MAX_REL = 128
HEADS = 8
HD = 64
ROPE = 32
PROJ_W = 2048
IN_COLS = 1952
B_SCALE = 96 ** -0.5
LANES = 128

ADAM_LR = 0.001
ADAM_B1 = 0.9
ADAM_B2 = 0.999
ADAM_EPS = 1e-08
ADAM_WD = 0.01
ADAM_STEP = 10

VMEM_LIMIT = 56 * 1024 * 1024

MESH = pl.DeviceIdType.MESH


def _dot(a, b):
    return lax.dot_general(a, b, (((1,), (0,)), ((), ())), preferred_element_type=F32)


def _dot_nt(a, b):
    return lax.dot_general(a, b, (((1,), (1,)), ((), ())), preferred_element_type=F32)


def _dot_tn(a, b):
    return lax.dot_general(a, b, (((0,), (0,)), ((), ())), preferred_element_type=F32)


def _params(sem):
    return pltpu.CompilerParams(dimension_semantics=sem, vmem_limit_bytes=VMEM_LIMIT)


def _rms(xv):
    r = lax.rsqrt(jnp.mean(xv * xv, axis=-1, keepdims=True) + EPS)
    return r, xv * r


def ffn_fwd(x, g, wg, wu, wd, name, gather=()):
    t, d = x.shape
    tm = 512
    ni = t // tm
    plan = GatherPlan(gather)
    n = plan.n

    def body(*refs):
        x_ref, g_ref, wg_ref, wu_ref, wd_ref = refs[:5]
        ins, (o_ref, gp_ref, up_ref), outs = refs[5:5 + n], refs[5 + n:8 + n], refs[8 + n:8 + 2 * n]
        h_ref, acc_ref = refs[8 + 2 * n:10 + 2 * n]
        sems = refs[10 + 2 * n:]
        i, j = pl.program_id(0), pl.program_id(1)
        if n:
            pl.when((i == 0) & (j == 0))(lambda: plan.start(ins, outs, sems))
            pl.when((i == (3 * ni) // 4) & (j == 0))(lambda: plan.forward(ins, outs, sems))

        @pl.when(j == 0)
        def _():
            _, xn = _rms(x_ref[...])
            h_ref[...] = (xn * g_ref[...]).astype(BF16)
            acc_ref[...] = jnp.zeros_like(acc_ref)

        h = h_ref[...]
        gp = _dot_nt(h, wg_ref[0])
        up = _dot_nt(h, wu_ref[0])
        gp_ref[0] = gp
        up_ref[0] = up
        a = (gp * jax.nn.sigmoid(gp) * up).astype(BF16)
        acc_ref[...] += _dot(a, wd_ref[0])

        @pl.when(j == N_SHARD - 1)
        def _():
            o_ref[...] = x_ref[...] + 0.5 * acc_ref[...]

        if n:
            pl.when((i == ni - 1) & (j == N_SHARD - 1))(lambda: plan.finish(ins, outs, sems))

    res = pl.pallas_call(
        body, name=name, grid=(ni, N_SHARD),
        in_specs=[pl.BlockSpec((tm, d), lambda i, j: (i, 0)),
                  pl.BlockSpec((1, d), lambda i, j: (0, 0)),
                  pl.BlockSpec((1, FS, d), lambda i, j: (j, 0, 0)),
                  pl.BlockSpec((1, FS, d), lambda i, j: (j, 0, 0)),
                  pl.BlockSpec((1, FS, d), lambda i, j: (j, 0, 0))] + [ANY] * n,
        out_specs=[pl.BlockSpec((tm, d), lambda i, j: (i, 0)), pl.BlockSpec((1, tm, FS), lambda i, j: (j, i, 0)),
                   pl.BlockSpec((1, tm, FS), lambda i, j: (j, i, 0))] + [ANY] * n,
        out_shape=[jax.ShapeDtypeStruct((t, d), F32), jax.ShapeDtypeStruct((N_SHARD, t, FS), F32),
                   jax.ShapeDtypeStruct((N_SHARD, t, FS), F32)] + plan.out_shape,
        scratch_shapes=[pltpu.VMEM((tm, d), BF16), pltpu.VMEM((tm, d), F32)] + (plan.scratch if n else []),
        compiler_params=_params(("arbitrary", "arbitrary")),
    )(x, g, wg, wu, wd, *gather)
    return res


def ffn_bwd(x, dout, g, wg, wu, wd, gate, up_pre, name, scatter=(), spread=None):
    t, d = x.shape
    tm = 512
    ni = t // tm
    hf = FS // 2
    plan = ScatterPlan(scatter)
    m = plan.n
    k = 0 if spread is None else 3
    steps = jnp.arange(N_SHARD, dtype=jnp.int32)
    order = steps if spread is None else (spread + 1 + steps) % N_SHARD

    def body(*refs):
        ord_ref, x_ref, do_ref, g_ref, wg_ref, wu_ref, wd_ref, gp_ref, up_ref = refs[:9]
        ins, (dwg_out, dwu_out, dwd_out, dhp_ref) = refs[9:9 + m], refs[9 + m:13 + m]
        outs, lands = refs[13 + m:13 + 2 * m], refs[13 + 2 * m:13 + 2 * m + k]
        dwg_ref, dwu_ref, dwd_ref = refs[13 + 2 * m + k:16 + 2 * m + k]
        sems = refs[16 + 2 * m + k:19 + 2 * m + k] if m else ()
        stage_ref = refs[-3] if k else None
        j, i = pl.program_id(0), pl.program_id(1)
        if m:
            pl.when((j == 0) & (i == 0))(lambda: plan.start(ins, outs, sems))

        def chunk_copies(jj):
            send_sems, recv_sems = refs[-2:]
            px, py, pc = _place()
            me = 4 * px + 2 * py + pc
            tx, ty = ord_ref[jj] // 2, ord_ref[jj] % 2
            copies = []
            for n_ in range(3):
                for h_ in range(2):
                    copies.append((pltpu.make_async_remote_copy(
                        src_ref=stage_ref.at[n_, pl.ds(h_ * hf, hf)], dst_ref=lands[n_].at[me],
                        send_sem=send_sems.at[6 * jj + 2 * n_ + h_], recv_sem=recv_sems.at[3 * me + n_],
                        device_id=(tx, ty, h_), device_id_type=MESH), (tx != px) | (ty != py) | (pc != h_)))
            return copies

        def arrivals():
            send_sems, recv_sems = refs[-2:]
            px, py, pc = _place()
            me = 4 * px + 2 * py + pc
            for s_ in range(N_DEV):
                for n_ in range(3):
                    slot = lands[n_].at[s_]
                    cp = pltpu.make_async_remote_copy(
                        src_ref=slot, dst_ref=slot, send_sem=send_sems.at[0], recv_sem=recv_sems.at[3 * s_ + n_],
                        device_id=(px, py, pc), device_id_type=MESH)
                    pl.when(me != s_)(cp.wait_recv)

        _, xn = _rms(x_ref[...])
        h = (xn * g_ref[...]).astype(BF16)
        dz = (0.5 * do_ref[...]).astype(BF16)
        wgv, wuv, wdv = wg_ref[0], wu_ref[0], wd_ref[0]
        gp, up = gp_ref[0], up_ref[0]
        s = jax.nn.sigmoid(gp)
        sg = gp * s
        a = (sg * up).astype(BF16)
        da = _dot_nt(dz, wdv)
        dup = (da * sg).astype(BF16)
        dgp = (da * up * (s * (1.0 + gp * (1.0 - s)))).astype(BF16)

        @pl.when(i == 0)
        def _():
            dwg_ref[...] = jnp.zeros_like(dwg_ref)
            dwu_ref[...] = jnp.zeros_like(dwu_ref)
            dwd_ref[...] = jnp.zeros_like(dwd_ref)

        dwd_ref[...] += _dot_tn(a, dz)
        dwg_ref[...] += _dot_tn(dgp, h)
        dwu_ref[...] += _dot_tn(dup, h)
        dhp_ref[0] = (_dot(dgp, wgv) + _dot(dup, wuv)).astype(BF16)

        @pl.when(i == ni - 1)
        def _():
            dwg_out[0] = dwg_ref[...].astype(BF16)
            dwu_out[0] = dwu_ref[...].astype(BF16)
            dwd_out[0] = dwd_ref[...].astype(BF16)
            if k:
                @pl.when(j >= 1)
                def _():
                    for cp, leaves in chunk_copies(j - 1):
                        pl.when(leaves)(cp.wait_send)
                for n_, acc in enumerate((dwg_ref, dwu_ref, dwd_ref)):
                    stage_ref[n_] = acc[...].astype(BF16)
                for cp, leaves in chunk_copies(j):
                    pl.when(leaves)(cp.start)

                @pl.when(j == N_SHARD - 1)
                def _():
                    for cp, leaves in chunk_copies(N_SHARD - 1):
                        pl.when(leaves)(cp.wait_send)
                    arrivals()

        if m:
            pl.when((j == N_SHARD - 1) & (i == ni - 1))(lambda: plan.finish(ins, outs, sems))

    chunk = pl.BlockSpec((1, FS, d), lambda j, i, o: (o[j], 0, 0))
    tok = pl.BlockSpec((tm, d), lambda j, i, o: (i, 0))
    pre = pl.BlockSpec((1, tm, FS), lambda j, i, o: (o[j], i, 0))
    grid_spec = pltpu.PrefetchScalarGridSpec(
        num_scalar_prefetch=1, grid=(N_SHARD, ni),
        in_specs=[tok, tok, pl.BlockSpec((1, d), lambda j, i, o: (0, 0)), chunk, chunk, chunk, pre, pre] + [ANY] * m,
        out_specs=[chunk, chunk, chunk, pl.BlockSpec((1, tm, d), lambda j, i, o: (o[j], i, 0))] + [ANY] * (m + k),
        scratch_shapes=[pltpu.VMEM((FS, d), F32), pltpu.VMEM((FS, d), F32), pltpu.VMEM((FS, d), F32)]
        + (plan.scratch if m else [])
        + ([pltpu.VMEM((3, FS, d), BF16), pltpu.SemaphoreType.DMA((6 * N_SHARD,)),
            pltpu.SemaphoreType.DMA((3 * N_DEV,))] if k else []))
    return pl.pallas_call(
        body, name=name, grid_spec=grid_spec,
        out_shape=[jax.ShapeDtypeStruct((N_SHARD, FS, d), BF16),
                   jax.ShapeDtypeStruct((N_SHARD, FS, d), BF16),
                   jax.ShapeDtypeStruct((N_SHARD, FS, d), BF16),
                   jax.ShapeDtypeStruct((N_SHARD, t, d), BF16)] + plan.out_shape
        + [jax.ShapeDtypeStruct((N_DEV, hf, d), BF16)] * k,
        compiler_params=_params(("arbitrary", "arbitrary")),
    )(order, x, dout, g, wg, wu, wd, gate, up_pre, *scatter)


def norm_bwd(x, g, dhp, dres, name):
    t, d = x.shape
    p = dhp.shape[0]
    tm = 512

    def body(x_ref, g_ref, dhp_ref, dres_ref, dx_ref, dg_ref):
        i = pl.program_id(0)
        r, xn = _rms(x_ref[...])
        dh = dhp_ref[0].astype(F32)
        for q in range(1, p):
            dh = dh + dhp_ref[q].astype(F32)
        dhg = dh * g_ref[...]
        dx_ref[...] = dres_ref[...] + r * (dhg - xn * jnp.mean(dhg * xn, axis=-1, keepdims=True))

        @pl.when(i == 0)
        def _():
            dg_ref[...] = jnp.zeros_like(dg_ref)

        dg_ref[...] += jnp.sum(dh * xn, axis=0, keepdims=True)

    return pl.pallas_call(
        body, name=name, grid=(t // tm,),
        in_specs=[pl.BlockSpec((tm, d), lambda i: (i, 0)),
                  pl.BlockSpec((1, d), lambda i: (0, 0)),
                  pl.BlockSpec((p, tm, d), lambda i: (0, i, 0)),
                  pl.BlockSpec((tm, d), lambda i: (i, 0))],
        out_specs=[pl.BlockSpec((tm, d), lambda i: (i, 0)),
                   pl.BlockSpec((1, d), lambda i: (0, 0))],
        out_shape=[jax.ShapeDtypeStruct((t, d), F32), jax.ShapeDtypeStruct((1, d), F32)],
        compiler_params=_params(("arbitrary",)),
    )(x, g, dhp, dres)


def final_loss(x, g, target):
    t, d = x.shape
    tm = 512

    def body(x_ref, g_ref, t_ref, dx_ref, dg_ref, loss_ref):
        i = pl.program_id(0)
        r, xn = _rms(x_ref[...])
        gv = g_ref[...]
        e = xn * gv - t_ref[...]
        dy = e * (1.0 / d)
        dhg = dy * gv
        dx_ref[...] = r * (dhg - xn * jnp.mean(dhg * xn, axis=-1, keepdims=True))

        @pl.when(i == 0)
        def _():
            dg_ref[...] = jnp.zeros_like(dg_ref)
            loss_ref[...] = jnp.zeros_like(loss_ref)

        dg_ref[...] += jnp.sum(dy * xn, axis=0, keepdims=True)
        part = jnp.sum(jnp.sum(e * e, axis=-1, keepdims=True), axis=0, keepdims=True) * (0.5 / d)
        loss_ref[...] += jnp.broadcast_to(part, loss_ref.shape)

    return pl.pallas_call(
        body, name="final_loss", grid=(t // tm,),
        in_specs=[pl.BlockSpec((tm, d), lambda i: (i, 0)),
                  pl.BlockSpec((1, d), lambda i: (0, 0)),
                  pl.BlockSpec((tm, d), lambda i: (i, 0))],
        out_specs=[pl.BlockSpec((tm, d), lambda i: (i, 0)),
                   pl.BlockSpec((1, d), lambda i: (0, 0)),
                   pl.BlockSpec((1, LANES), lambda i: (0, 0))],
        out_shape=[jax.ShapeDtypeStruct((t, d), F32), jax.ShapeDtypeStruct((1, d), F32),
                   jax.ShapeDtypeStruct((1, LANES), F32)],
        compiler_params=_params(("arbitrary",)),
    )(x, g, target)


def mix_proj(x, g, w):
    t, d = x.shape
    n = w.shape[0]
    tm = 512

    def body(x_ref, g_ref, w_ref, h_ref, p_ref):
        _, xn = _rms(x_ref[...])
        h = (xn * g_ref[...]).astype(BF16)
        h_ref[...] = h
        p_ref[...] = _dot_nt(h, w_ref[...])

    return pl.pallas_call(
        body, name="mix_proj", grid=(t // tm,),
        in_specs=[pl.BlockSpec((tm, d), lambda i: (i, 0)),
                  pl.BlockSpec((1, d), lambda i: (0, 0)),
                  pl.BlockSpec((n, d), lambda i: (0, 0))],
        out_specs=[pl.BlockSpec((tm, d), lambda i: (i, 0)),
                   pl.BlockSpec((tm, n), lambda i: (i, 0))],
        out_shape=[jax.ShapeDtypeStruct((t, d), BF16), jax.ShapeDtypeStruct((t, n), F32)],
        compiler_params=_params(("parallel",)),
    )(x, g, w)


def matmul(a, b, mode, name, tm, tn, tk, out_dtype=F32):
    if mode == "nn":
        (m, k), n = a.shape, b.shape[1]
        a_spec = pl.BlockSpec((tm, tk), lambda i, j, q: (i, q))
        b_spec = pl.BlockSpec((tk, tn), lambda i, j, q: (q, j))
        dot = _dot
    elif mode == "nt":
        (m, k), n = a.shape, b.shape[0]
        a_spec = pl.BlockSpec((tm, tk), lambda i, j, q: (i, q))
        b_spec = pl.BlockSpec((tn, tk), lambda i, j, q: (j, q))
        dot = _dot_nt
    else:
        (k, m), n = a.shape, b.shape[1]
        a_spec = pl.BlockSpec((tk, tm), lambda i, j, q: (q, i))
        b_spec = pl.BlockSpec((tk, tn), lambda i, j, q: (q, j))
        dot = _dot_tn
    assert m % tm == 0 and n % tn == 0 and k % tk == 0, (m, n, k, tm, tn, tk)
    nk = k // tk

    def body(a_ref, b_ref, o_ref, acc_ref):
        q = pl.program_id(2)

        @pl.when(q == 0)
        def _():
            acc_ref[...] = jnp.zeros_like(acc_ref)

        acc_ref[...] += dot(a_ref[...].astype(BF16), b_ref[...].astype(BF16))

        @pl.when(q == nk - 1)
        def _():
            o_ref[...] = acc_ref[...].astype(out_dtype)

    return pl.pallas_call(
        body, name=name, grid=(m // tm, n // tn, nk),
        in_specs=[a_spec, b_spec],
        out_specs=pl.BlockSpec((tm, tn), lambda i, j, q: (i, j)),
        out_shape=jax.ShapeDtypeStruct((m, n), out_dtype),
        scratch_shapes=[pltpu.VMEM((tm, tn), F32)],
        compiler_params=_params(("parallel", "parallel", "arbitrary")),
    )(a, b)


def out_proj(x, oa, ob_t, w):
    t, d = x.shape
    half = oa.shape[1]
    tm = 512

    def body(x_ref, oa_ref, obt_ref, w_ref, o_ref):
        o_ref[...] = (x_ref[...] + _dot(oa_ref[...], w_ref[0:half, :])
                      + _dot_tn(obt_ref[...], w_ref[half:2 * half, :]))

    return pl.pallas_call(
        body, name="out_proj", grid=(t // tm,),
        in_specs=[pl.BlockSpec((tm, d), lambda i: (i, 0)),
                  pl.BlockSpec((tm, half), lambda i: (i, 0)),
                  pl.BlockSpec((half, tm), lambda i: (0, i)),
                  pl.BlockSpec((2 * half, d), lambda i: (0, 0))],
        out_specs=pl.BlockSpec((tm, d), lambda i: (i, 0)),
        out_shape=jax.ShapeDtypeStruct((t, d), F32),
        compiler_params=_params(("parallel",)),
    )(x, oa, ob_t, w)


def out_proj_bwd(dx, w):
    t, d = dx.shape
    half = w.shape[0] // 2
    tm = 512

    def body(dx_ref, w_ref, da_ref, db_ref, dbt_ref):
        dxb = dx_ref[...].astype(BF16)
        da_ref[...] = _dot_nt(dxb, w_ref[0:half, :]).astype(BF16)
        db_ref[...] = _dot_nt(dxb, w_ref[half:2 * half, :]).astype(BF16)
        dbt_ref[...] = _dot_nt(w_ref[half:2 * half, :], dxb).astype(BF16)

    return pl.pallas_call(
        body, name="out_proj_bwd", grid=(t // tm,),
        in_specs=[pl.BlockSpec((tm, d), lambda i: (i, 0)),
                  pl.BlockSpec((2 * half, d), lambda i: (0, 0))],
        out_specs=[pl.BlockSpec((tm, half), lambda i: (i, 0)),
                   pl.BlockSpec((tm, half), lambda i: (i, 0)),
                   pl.BlockSpec((half, tm), lambda i: (0, i))],
        out_shape=[jax.ShapeDtypeStruct((t, half), BF16), jax.ShapeDtypeStruct((t, half), BF16),
                   jax.ShapeDtypeStruct((half, t), BF16)],
        compiler_params=_params(("parallel",)),
    )(dx, w)


def _lane(shape):
    return lax.broadcasted_iota(jnp.int32, shape, 1)


def _seg_sum(z, mask):
    return jnp.sum(jnp.where(mask, z, 0.0), axis=-1, keepdims=True)


def _pair_norm(x):
    lo = _lane(x.shape) < HD
    x2 = x * x
    r = jnp.where(lo, lax.rsqrt(_seg_sum(x2, lo) * (1.0 / HD) + EPS),
                  lax.rsqrt(_seg_sum(x2, ~lo) * (1.0 / HD) + EPS))
    return lo, r, x * r


def _pair_norm_bwd(lo, r, xn, dyg):
    z = dyg * xn
    mean = jnp.where(lo, _seg_sum(z, lo), _seg_sum(z, ~lo)) * (1.0 / HD)
    return r * (dyg - xn * mean)


A_TM = 256


def prep1_fwd(proj, gq, gk, gcq, gckv):
    t = proj.shape[0]
    tm = A_TM

    def body(p_ref, gq_ref, gk_ref, gcq_ref, gckv_ref, qa_ref, ka_ref, va_ref, cq_ref, ckv_ref):
        for p in range(4):
            sl = slice(LANES * p, LANES * (p + 1))
            _, _, xn = _pair_norm(p_ref[:, sl])
            qa_ref[:, sl] = (xn * gq_ref[:, sl] * 0.125).astype(BF16)
            _, _, xn = _pair_norm(p_ref[:, 512 + LANES * p:512 + LANES * (p + 1)])
            ka_ref[:, sl] = (xn * gk_ref[:, sl]).astype(BF16)
        va_ref[...] = p_ref[:, 1024:1536].astype(BF16)
        _, xn = _rms(p_ref[:, 1536:1792])
        cq_ref[...] = (xn * gcq_ref[...]).astype(BF16)
        _, xn = _rms(p_ref[:, 1792:1920])
        ckv_ref[...] = (xn * gckv_ref[...]).astype(BF16)

    row = lambda w: pl.BlockSpec((tm, w), lambda i: (i, 0))
    vec = lambda w: pl.BlockSpec((1, w), lambda i: (0, 0))
    return pl.pallas_call(
        body, name="prep1_fwd", grid=(t // tm,),
        in_specs=[row(PROJ_W), vec(512), vec(512), vec(256), vec(128)],
        out_specs=[row(512), row(512), row(512), row(256), row(128)],
        out_shape=[jax.ShapeDtypeStruct((t, w), BF16) for w in (512, 512, 512, 256, 128)],
        compiler_params=_params(("parallel",)),
    )(proj, gq, gk, gcq, gckv)


def mix_bwd(proj, x, h, dres, w, g, dqa, dkp, dvp, dcq, dckv, dkr, gq, gk, gcq, gckv):
    t, d = x.shape
    tm = A_TM
    nb = t // tm

    def body(p_ref, x_ref, h_ref, dres_ref, w_ref, g_ref, dqa_ref, dk0_ref, dk1_ref, dk2_ref, dv0_ref, dv1_ref,
             dv2_ref, dcq_ref, dckv_ref, dkr_ref, gq_ref, gk_ref, gcq_ref, gckv_ref,
             dx_ref, dw_ref, dg_ref, dgq_ref, dgk_ref, dgcq_ref, dgckv_ref, dp_ref, acc_ref):
        i = pl.program_id(0)

        @pl.when(i == 0)
        def _():
            for ref in (acc_ref, dg_ref, dgq_ref, dgk_ref, dgcq_ref, dgckv_ref):
                ref[...] = jnp.zeros_like(ref)

        has1 = (i + 1 < nb).astype(F32)
        has2 = (i + 2 < nb).astype(F32)
        for p in range(4):
            sl = slice(LANES * p, LANES * (p + 1))
            lo, r, xn = _pair_norm(p_ref[:, sl])
            dy = dqa_ref[:, sl] * 0.125
            dp_ref[:, sl] = _pair_norm_bwd(lo, r, xn, dy * gq_ref[:, sl]).astype(BF16)
            dgq_ref[:, sl] += jnp.sum(dy * xn, axis=0, keepdims=True)
            ks = slice(512 + LANES * p, 512 + LANES * (p + 1))
            lo, r, xn = _pair_norm(p_ref[:, ks])
            dy = dk0_ref[0, :, sl] + has1 * dk1_ref[0, :, sl] + has2 * dk2_ref[0, :, sl]
            dp_ref[:, ks] = _pair_norm_bwd(lo, r, xn, dy * gk_ref[:, sl]).astype(BF16)
            dgk_ref[:, sl] += jnp.sum(dy * xn, axis=0, keepdims=True)
        dp_ref[:, 1024:1536] = (dv0_ref[0] + has1 * dv1_ref[0] + has2 * dv2_ref[0]).astype(BF16)
        for (a, b, dlat_ref, glat_ref, dglat_ref) in ((1536, 1792, dcq_ref, gcq_ref, dgcq_ref),
                                                      (1792, 1920, dckv_ref, gckv_ref, dgckv_ref)):
            r, xn = _rms(p_ref[:, a:b])
            dy = dlat_ref[...]
            dyg = dy * glat_ref[...]
            dp_ref[:, a:b] = (r * (dyg - xn * jnp.mean(dyg * xn, axis=-1, keepdims=True))).astype(BF16)
            dglat_ref[...] += jnp.sum(dy * xn, axis=0, keepdims=True)
        dp_ref[:, 1920:2048] = dkr_ref[...].astype(BF16)

        dproj = dp_ref[...]
        acc_ref[...] += _dot_tn(dproj, h_ref[...])
        dh = _dot(dproj, w_ref[...])
        r, xn = _rms(x_ref[...])
        dhg = dh * g_ref[...]
        dx_ref[...] = dres_ref[...] + r * (dhg - xn * jnp.mean(dhg * xn, axis=-1, keepdims=True))
        dg_ref[...] += jnp.sum(dh * xn, axis=0, keepdims=True)

        @pl.when(i == nb - 1)
        def _():
            dw_ref[...] = acc_ref[...].astype(BF16)

    row = lambda w_: pl.BlockSpec((tm, w_), lambda i: (i, 0))
    vec = lambda w_: pl.BlockSpec((1, w_), lambda i: (0, 0))
    part = lambda s: pl.BlockSpec((1, tm, 512), lambda i: (s, jnp.minimum(i + s, nb - 1), 0))
    whole = pl.BlockSpec((PROJ_W, d), lambda i: (0, 0))
    return pl.pallas_call(
        body, name="mix_bwd", grid=(nb,),
        in_specs=[row(PROJ_W), row(d), row(d), row(d), whole, vec(d), row(512), part(0), part(1), part(2),
                  part(0), part(1), part(2), row(256), row(128), row(128), vec(512), vec(512), vec(256), vec(128)],
        out_specs=[row(d), whole, vec(d), vec(512), vec(512), vec(256), vec(128)],
        out_shape=[jax.ShapeDtypeStruct((t, d), F32), jax.ShapeDtypeStruct((PROJ_W, d), BF16)]
        + [jax.ShapeDtypeStruct((1, w_), F32) for w_ in (d, 512, 512, 256, 128)],
        scratch_shapes=[pltpu.VMEM((tm, PROJ_W), BF16), pltpu.VMEM((PROJ_W, d), F32)],
        compiler_params=_params(("arbitrary",)),
    )(proj, x, h, dres, w, g, dqa, dkp, dkp, dkp, dvp, dvp, dvp, dcq, dckv, dkr, gq, gk, gcq, gckv)


def _roll(x, shift):
    return pltpu.roll(x, shift % LANES, 1)


def _rope(y, c, s1, s2):
    return y * c + _roll(y, -16) * s1 + _roll(y, 16) * s2


def _rope_bwd(d, c, s1, s2):
    return d * c + _roll(d * s1, 16) + _roll(d * s2, -16)


def _q_head_stats(x):
    lane = _lane(x.shape)
    mn = lane < HD
    mr = (lane >= HD) & (lane < HD + ROPE)
    x2 = x * x
    r = jnp.where(mn, lax.rsqrt(_seg_sum(x2, mn) * (1.0 / HD) + EPS),
                  lax.rsqrt(_seg_sum(x2, mr) * (1.0 / ROPE) + EPS))
    return mn, mr, r, x * r


def _kr_stats(x):
    r = lax.rsqrt(jnp.sum(x * x, axis=-1, keepdims=True) * (1.0 / ROPE) + EPS)
    return r, x * r


def _up_proj(cq_ref, ckv_ref, wuq_ref, wukv_ref):
    return _dot_nt(cq_ref[...], wuq_ref[...]), _dot(ckv_ref[...], wukv_ref[...])


def prep2_fwd(cqn, ckvn, proj, wuq, wukv, gq, gk, gkr, tabs):
    t = cqn.shape[0]
    tm = A_TM

    def body(cq_ref, ckv_ref, kr_ref, wuq_ref, wukv_ref, gq_ref, gk_ref, gkr_ref, tab_ref, qf_ref, kf_ref, vp_ref):
        q_all, kv_all = _up_proj(cq_ref, ckv_ref, wuq_ref, wukv_ref)
        _, xn = _kr_stats(kr_ref[...])
        kpe = _roll(_rope(xn * gkr_ref[...], tab_ref[3], tab_ref[4], tab_ref[5]), 64)
        for h in range(HEADS):
            sl = slice(LANES * h, LANES * (h + 1))
            _, _, _, xn = _q_head_stats(q_all[:, sl])
            qf_ref[:, sl] = (_rope(xn * gq_ref[...], tab_ref[0], tab_ref[1], tab_ref[2]) * B_SCALE2).astype(BF16)
            x = kv_all[:, sl]
            lo = _lane(x.shape) < HD
            xk = jnp.where(lo, x, 0.0)
            rk = lax.rsqrt(jnp.sum(xk * xk, axis=-1, keepdims=True) * (1.0 / HD) + EPS)
            kf_ref[:, sl] = (xk * rk * gk_ref[...] + kpe).astype(BF16)
            if h % 2 == 0:
                v_even = _roll(x, 64)
            else:
                vp_ref[:, LANES * (h // 2):LANES * (h // 2 + 1)] = jnp.where(lo, v_even, x).astype(BF16)

    row = lambda w: pl.BlockSpec((tm, w), lambda i: (i, 0))
    vec = lambda w: pl.BlockSpec((1, w), lambda i: (0, 0))
    full = lambda a: pl.BlockSpec(a.shape, lambda i: (0, 0))
    return pl.pallas_call(
        body, name="prep2_fwd", grid=(t // tm,),
        in_specs=[row(256), row(128), pl.BlockSpec((tm, LANES), lambda i: (i, 15)), full(wuq), full(wukv),
                  vec(128), vec(128), vec(128), pl.BlockSpec((6, tm, LANES), lambda i: (0, i, 0))],
        out_specs=[row(1024), row(1024), row(512)],
        out_shape=[jax.ShapeDtypeStruct((t, 1024), BF16), jax.ShapeDtypeStruct((t, 1024), BF16),
                   jax.ShapeDtypeStruct((t, 512), BF16)],
        compiler_params=_params(("parallel",)),
    )(cqn, ckvn, proj, wuq, wukv, gq, gk, gkr, tabs)


def prep2_bwd(cqn, ckvn, proj, wuq, wukv, dqf, dkf, dvp, gq, gk, gkr, tabs):
    t = cqn.shape[0]
    tm = A_TM

    def body(cq_ref, ckv_ref, kr_ref, wuq_ref, wukv_ref, dqf_ref, dkf_ref, dvp_ref, gq_ref, gk_ref, gkr_ref, tab_ref,
             dcq_ref, dckv_ref, dkr_ref, dwuq_ref, dwukv_ref, dgq_ref, dgk_ref, dgkr_ref, dq_ref, dkv_ref):
        i = pl.program_id(0)

        @pl.when(i == 0)
        def _():
            for ref in (dwuq_ref, dwukv_ref, dgq_ref, dgk_ref, dgkr_ref):
                ref[...] = jnp.zeros_like(ref)

        q_all, kv_all = _up_proj(cq_ref, ckv_ref, wuq_ref, wukv_ref)
        dgq = jnp.zeros((1, LANES), F32)
        dgk = jnp.zeros((1, LANES), F32)
        dkpe = jnp.zeros((tm, LANES), F32)
        for h in range(HEADS):
            sl = slice(LANES * h, LANES * (h + 1))
            mn, mr, r, xn = _q_head_stats(q_all[:, sl])
            dy = _rope_bwd(dqf_ref[sl, :].T, tab_ref[0], tab_ref[1], tab_ref[2])
            dyg = dy * gq_ref[...]
            z = dyg * xn
            mean = jnp.where(mn, _seg_sum(z, mn) * (1.0 / HD), _seg_sum(z, mr) * (1.0 / ROPE))
            dq_ref[:, sl] = (r * (dyg - xn * mean)).astype(BF16)
            dgq = dgq + jnp.sum(dy * xn, axis=0, keepdims=True)

            x = kv_all[:, sl]
            dk = dkf_ref[:, sl]
            xk = jnp.where(mn, x, 0.0)
            rk = lax.rsqrt(jnp.sum(xk * xk, axis=-1, keepdims=True) * (1.0 / HD) + EPS)
            xkn = xk * rk
            dyk = jnp.where(mn, dk, 0.0)
            dykg = dyk * gk_ref[...]
            dxk = rk * (dykg - xkn * (jnp.sum(dykg * xkn, axis=-1, keepdims=True) * (1.0 / HD)))
            dgk = dgk + jnp.sum(dyk * xkn, axis=0, keepdims=True)
            dkpe = dkpe + jnp.where(mr, dk, 0.0)
            dvpair = dvp_ref[:, LANES * (h // 2):LANES * (h // 2 + 1)]
            dv = _roll(dvpair, 64) if h % 2 == 0 else dvpair
            dkv_ref[:, sl] = jnp.where(mn, dxk, dv).astype(BF16)

        r, xn = _kr_stats(kr_ref[...])
        dy = _rope_bwd(_roll(dkpe, 64), tab_ref[3], tab_ref[4], tab_ref[5])
        dyg = dy * gkr_ref[...]
        dkr_ref[...] = r * (dyg - xn * (jnp.sum(dyg * xn, axis=-1, keepdims=True) * (1.0 / ROPE)))
        dgq_ref[...] += dgq
        dgk_ref[...] += dgk
        dgkr_ref[...] += jnp.sum(dy * xn, axis=0, keepdims=True)
        dqb, dkvb = dq_ref[...], dkv_ref[...]
        dcq_ref[...] = _dot(dqb, wuq_ref[...])
        dckv_ref[...] = _dot_nt(dkvb, wukv_ref[...])
        dwuq_ref[...] += _dot_tn(dqb, cq_ref[...])
        dwukv_ref[...] += _dot_tn(ckv_ref[...], dkvb)

    row = lambda w: pl.BlockSpec((tm, w), lambda i: (i, 0))
    vec = lambda w: pl.BlockSpec((1, w), lambda i: (0, 0))
    full = lambda a: pl.BlockSpec(a.shape, lambda i: (0, 0))
    return pl.pallas_call(
        body, name="prep2_bwd", grid=(t // tm,),
        in_specs=[row(256), row(128), pl.BlockSpec((tm, LANES), lambda i: (i, 15)), full(wuq), full(wukv),
                  pl.BlockSpec((1024, tm), lambda i: (0, i)), row(1024), row(512),
                  vec(128), vec(128), vec(128), pl.BlockSpec((6, tm, LANES), lambda i: (0, i, 0))],
        out_specs=[row(256), row(128), row(128), full(wuq), full(wukv), vec(128), vec(128), vec(128)],
        out_shape=[jax.ShapeDtypeStruct((t, 256), F32), jax.ShapeDtypeStruct((t, 128), F32),
                   jax.ShapeDtypeStruct((t, LANES), F32), jax.ShapeDtypeStruct(wuq.shape, F32),
                   jax.ShapeDtypeStruct(wukv.shape, F32)] + [jax.ShapeDtypeStruct((1, LANES), F32)] * 3,
        scratch_shapes=[pltpu.VMEM((tm, 1024), BF16), pltpu.VMEM((tm, 1024), BF16)],
        compiler_params=_params(("arbitrary",)),
    )(cqn, ckvn, proj, wuq, wukv, dqf, dkf, dvp, gq, gk, gkr, tabs)


A_TQ = 256
A_WIN = 3 * A_TQ


def _a_specs(t):
    nb = t // A_TQ
    blk = lambda s: pl.BlockSpec((A_TQ, 512), lambda i: (jnp.maximum(i - s, 0), 0))
    return nb, blk


def _a_exp(q_ref, kc, b_ref, head, sl, lo):
    hm = lo if head % 2 == 0 else ~lo
    qm = jnp.where(hm, q_ref[:, sl], jnp.zeros((), BF16))
    s = _dot_nt(qm, kc) + b_ref[0, head]
    e = jnp.exp(s - jnp.max(s, axis=-1, keepdims=True))
    return hm, qm, e, 1.0 / jnp.sum(e, axis=-1, keepdims=True)


def _a_bias_spec():
    return pl.BlockSpec((1, HEADS, A_TQ, A_WIN), lambda i: (jnp.minimum(i, 2), 0, 0, 0))


def attn_a_fwd(qa, ka, va, bias):
    t = qa.shape[0]
    nb, blk = _a_specs(t)

    def body(q_ref, k2_ref, k1_ref, k0_ref, v2_ref, v1_ref, v0_ref, b_ref, o_ref):
        lo = _lane((A_TQ, LANES)) < HD
        for p in range(4):
            sl = slice(LANES * p, LANES * (p + 1))
            kc = jnp.concatenate([k2_ref[:, sl], k1_ref[:, sl], k0_ref[:, sl]], axis=0)
            vc = jnp.concatenate([v2_ref[:, sl], v1_ref[:, sl], v0_ref[:, sl]], axis=0)
            outs = []
            for h2 in range(2):
                _, _, e, inv = _a_exp(q_ref, kc, b_ref, 2 * p + h2, sl, lo)
                outs.append(_dot(e.astype(BF16), vc) * inv)
            o_ref[:, sl] = jnp.where(lo, outs[0], outs[1]).astype(BF16)

    return pl.pallas_call(
        body, name="attn_a_fwd", grid=(nb,),
        in_specs=[blk(0), blk(2), blk(1), blk(0), blk(2), blk(1), blk(0), _a_bias_spec()],
        out_specs=pl.BlockSpec((A_TQ, 512), lambda i: (i, 0)),
        out_shape=jax.ShapeDtypeStruct((t, 512), BF16),
        compiler_params=_params(("parallel",)),
    )(qa, ka, ka, ka, va, va, va, bias)


def attn_a_bwd(qa, ka, va, bias, do):
    t = qa.shape[0]
    nb, blk = _a_specs(t)

    def body(q_ref, k2_ref, k1_ref, k0_ref, v2_ref, v1_ref, v0_ref, b_ref, do_ref, dq_ref, dk_ref, dv_ref, db_ref):
        qb = pl.program_id(0)

        @pl.when(qb == 0)
        def _():
            db_ref[...] = jnp.zeros_like(db_ref)

        lo = _lane((A_TQ, LANES)) < HD
        for p in range(4):
            sl = slice(LANES * p, LANES * (p + 1))
            kc = jnp.concatenate([k2_ref[:, sl], k1_ref[:, sl], k0_ref[:, sl]], axis=0)
            vc = jnp.concatenate([v2_ref[:, sl], v1_ref[:, sl], v0_ref[:, sl]], axis=0)
            dqs = []
            dkc = jnp.zeros((A_WIN, LANES), F32)
            dvc = jnp.zeros((A_WIN, LANES), F32)
            for h2 in range(2):
                head = 2 * p + h2
                hm, qm, e, inv = _a_exp(q_ref, kc, b_ref, head, sl, lo)
                pr = e * inv
                dom = jnp.where(hm, do_ref[:, sl], jnp.zeros((), BF16))
                dp = _dot_nt(dom, vc)
                ds = pr * (dp - jnp.sum(pr * dp, axis=-1, keepdims=True))
                db_ref[head] += ds
                dsb = ds.astype(BF16)
                dqs.append(_dot(dsb, kc))
                dkc = dkc + _dot_tn(dsb, qm)
                dvc = dvc + _dot_tn(pr.astype(BF16), dom)
            dq_ref[:, sl] = jnp.where(lo, dqs[0], dqs[1])
            for s in range(3):
                rows = slice(A_TQ * (2 - s), A_TQ * (3 - s))
                dk_ref[s, :, sl] = dkc[rows]
                dv_ref[s, :, sl] = dvc[rows]

    share = pl.BlockSpec((3, A_TQ, 512), lambda i: (0, i, 0))
    return pl.pallas_call(
        body, name="attn_a_bwd", grid=(nb,),
        in_specs=[blk(0), blk(2), blk(1), blk(0), blk(2), blk(1), blk(0), _a_bias_spec(), blk(0)],
        out_specs=[pl.BlockSpec((A_TQ, 512), lambda i: (i, 0)), share, share,
                   pl.BlockSpec((HEADS, A_TQ, A_WIN), lambda i: (0, 0, 0))],
        out_shape=[jax.ShapeDtypeStruct((t, 512), F32), jax.ShapeDtypeStruct((3, t, 512), F32),
                   jax.ShapeDtypeStruct((3, t, 512), F32), jax.ShapeDtypeStruct((HEADS, A_TQ, A_WIN), F32)],
        compiler_params=_params(("arbitrary",)),
    )(qa, ka, ka, ka, va, va, va, bias, do)


B_T = 1024


B_SCALE2 = B_SCALE * 1.4426950408889634
_B_ALL = slice(0, B_T)
_B_LO, _B_HI = slice(0, B_T // 2), slice(B_T // 2, B_T)
_B_DIAG = ((_B_LO, _B_LO), (_B_LO, _B_HI), (_B_HI, _B_HI))


def _tri_tables(n, by_query):
    pairs = [(i, j) for i in range(n) for j in range(i + 1)] if by_query else [(i, j) for j in range(n) for i in range(j, n)]
    return (np.asarray([p[0] for p in pairs], np.int32), np.asarray([p[1] for p in pairs], np.int32))


def _b_mask_t(s):
    kc = lax.broadcasted_iota(jnp.int32, s.shape, 0) // CHUNK
    qc = lax.broadcasted_iota(jnp.int32, s.shape, 1) // CHUNK
    return jnp.where(kc <= qc, s, NEG)


def attn_b_fwd(qf, kf, vp):
    t = qf.shape[0]
    n = t // B_T
    qtab, ktab = _tri_tables(n, by_query=True)

    def body(qt_ref, kt_ref, q_ref, k_ref, v_ref, o_ref, lse_ref, m_s, l_s, acc_s):
        qb, kb = qt_ref[pl.program_id(1)], kt_ref[pl.program_id(1)]

        @pl.when(kb == 0)
        def _():
            m_s[...] = jnp.full_like(m_s, NEG)
            l_s[...] = jnp.zeros_like(l_s)
            acc_s[...] = jnp.zeros_like(acc_s)

        def block(kr, qr, masked):
            v = v_ref[kr, :]
            for h2 in range(2):
                sl = slice(LANES * h2, LANES * (h2 + 1))
                s = _dot_nt(k_ref[kr, sl], q_ref[qr, sl])
                if masked:
                    s = _b_mask_t(s)
                m_prev = m_s[h2, :, qr]
                m_new = jnp.maximum(m_prev, jnp.max(s, axis=0, keepdims=True))
                alpha = jnp.exp2(m_prev - m_new)
                pr = jnp.exp2(s - m_new)
                l_s[h2, :, qr] = alpha * l_s[h2, :, qr] + jnp.sum(pr, axis=0, keepdims=True)
                acc_s[h2, :, qr] = alpha * acc_s[h2, :, qr] + _dot_tn(v, pr.astype(BF16))
                m_s[h2, :, qr] = m_new

        @pl.when(kb < qb)
        def _():
            block(_B_ALL, _B_ALL, False)

        @pl.when(kb == qb)
        def _():
            for kr, qr in _B_DIAG:
                block(kr, qr, kr == qr)
            for h2 in range(2):
                l = l_s[h2]
                rows = slice(HD * h2, HD * (h2 + 1))
                o_ref[rows, :] = (acc_s[h2, rows, :] * (1.0 / l)).astype(BF16)
                lse_ref[0, h2:h2 + 1, :] = m_s[h2] + jnp.log2(l)

    grid_spec = pltpu.PrefetchScalarGridSpec(
        num_scalar_prefetch=2, grid=(4, len(qtab)),
        in_specs=[pl.BlockSpec((B_T, 256), lambda p, s, qt, kt: (qt[s], p)),
                  pl.BlockSpec((B_T, 256), lambda p, s, qt, kt: (kt[s], p)),
                  pl.BlockSpec((B_T, LANES), lambda p, s, qt, kt: (kt[s], p))],
        out_specs=[pl.BlockSpec((LANES, B_T), lambda p, s, qt, kt: (p, qt[s])),
                   pl.BlockSpec((1, 2, B_T), lambda p, s, qt, kt: (p, 0, qt[s]))],
        scratch_shapes=[pltpu.VMEM((2, 1, B_T), F32), pltpu.VMEM((2, 1, B_T), F32), pltpu.VMEM((2, LANES, B_T), F32)])
    return pl.pallas_call(
        body, name="attn_b_fwd", grid_spec=grid_spec,
        out_shape=[jax.ShapeDtypeStruct((512, t), BF16), jax.ShapeDtypeStruct((4, 2, t), F32)],
        compiler_params=_params(("parallel", "arbitrary")),
    )(jnp.asarray(qtab), jnp.asarray(ktab), qf, kf, vp)


def attn_b_bwd(qf, kf, vp, do, do_t, o_t, lse, scatter=()):
    t = qf.shape[0]
    n = t // B_T
    qtab, ktab = _tri_tables(n, by_query=False)
    plan = ScatterPlan(scatter)
    m = plan.n
    last = len(qtab) - 1

    def body(*refs):
        qt_ref, kt_ref, q_ref, k_ref, v_ref, do_ref, dot_ref, ot_ref, lse_ref = refs[:9]
        ins, (dq_ref, dk_ref, dv_ref), outs = refs[9:9 + m], refs[9 + m:12 + m], refs[12 + m:12 + 2 * m]
        sems = refs[12 + 2 * m:]
        qb, kb = qt_ref[pl.program_id(1)], kt_ref[pl.program_id(1)]
        if m:
            pl.when((pl.program_id(0) == 0) & (pl.program_id(1) == 0))(lambda: plan.start(ins, outs, sems))

        @pl.when(pl.program_id(1) == 0)
        def _():
            dq_ref[...] = jnp.zeros_like(dq_ref)

        @pl.when(qb == kb)
        def _():
            dk_ref[...] = jnp.zeros_like(dk_ref)
            dv_ref[...] = jnp.zeros_like(dv_ref)

        def block(kr, qr, masked):
            nq = qr.stop - qr.start
            cols = pl.ds(pl.multiple_of(qb * B_T + qr.start, LANES), nq)
            v = v_ref[kr, :]
            dov = do_ref[qr, :]
            prod = dot_ref[:, qr].astype(F32) * ot_ref[:, qr].astype(F32)
            lo = _lane((nq, LANES)) < HD
            for h2 in range(2):
                sl = slice(LANES * h2, LANES * (h2 + 1))
                hm = lo if h2 == 0 else ~lo
                q = q_ref[qr, sl]
                k = k_ref[kr, sl]
                dom = jnp.where(hm, dov, jnp.zeros((), BF16))
                delta = jnp.sum(prod[HD * h2:HD * (h2 + 1), :], axis=0, keepdims=True)
                s = _dot_nt(k, q)
                if masked:
                    s = _b_mask_t(s)
                pr = jnp.exp2(s - lse_ref[0, h2:h2 + 1, qr])
                dp = _dot_nt(v, dom)
                ds = (pr * (dp - delta)).astype(BF16)
                dk_ref[kr, sl] += _dot(ds, q) * (B_SCALE / B_SCALE2)
                dv_ref[kr, :] += _dot(pr.astype(BF16), dom)
                dq_ref[sl, cols] += _dot_tn(k, ds) * B_SCALE

        @pl.when(qb > kb)
        def _():
            block(_B_ALL, _B_ALL, False)

        @pl.when(qb == kb)
        def _():
            for kr, qr in _B_DIAG:
                block(kr, qr, kr == qr)

        if m:
            pl.when((pl.program_id(0) == 3) & (pl.program_id(1) == last))(lambda: plan.finish(ins, outs, sems))

    qrow = lambda w: pl.BlockSpec((B_T, w), lambda p, s, qt, kt: (qt[s], p))
    qcol = pl.BlockSpec((LANES, B_T), lambda p, s, qt, kt: (p, qt[s]))
    krow = lambda w: pl.BlockSpec((B_T, w), lambda p, s, qt, kt: (kt[s], p))
    grid_spec = pltpu.PrefetchScalarGridSpec(
        num_scalar_prefetch=2, grid=(4, len(qtab)),
        in_specs=[qrow(256), krow(256), krow(LANES), qrow(LANES), qcol, qcol,
                  pl.BlockSpec((1, 2, B_T), lambda p, s, qt, kt: (p, 0, qt[s]))] + [ANY] * m,
        out_specs=[pl.BlockSpec((256, t), lambda p, s, qt, kt: (p, 0)), krow(256), krow(LANES)] + [ANY] * m,
        scratch_shapes=plan.scratch if m else [])
    return pl.pallas_call(
        body, name="attn_b_bwd", grid_spec=grid_spec,
        out_shape=[jax.ShapeDtypeStruct((1024, t), F32), jax.ShapeDtypeStruct((t, 1024), F32),
                   jax.ShapeDtypeStruct((t, 512), F32)] + plan.out_shape,
        compiler_params=_params(("arbitrary", "arbitrary")),
    )(jnp.asarray(qtab), jnp.asarray(ktab), qf, kf, vp, do, do_t, o_t, lse, *scatter)


_U_LEN = A_TQ + A_WIN - 1


def _band_mask():
    a = np.arange(A_TQ)[:, None] // CHUNK
    b = np.arange(A_WIN)[None, :] // CHUNK
    return (b >= a) & (b <= a + A_LEFT)


def bias_block(table):
    h = table.shape[0]
    n_lo = A_WIN - 1 - 2 * A_TQ - A_MAX_REL
    ext = jnp.concatenate([jnp.repeat(table[:, :1], n_lo, axis=1), table,
                           jnp.repeat(table[:, -1:], _U_LEN - n_lo - table.shape[1], axis=1)], axis=1)
    row = jnp.pad(ext[:, ::-1], ((0, 0), (0, 1)))
    flat = jnp.tile(row, (1, A_TQ))[:, :A_TQ * _U_LEN]
    skew = flat.reshape(h, A_TQ, _U_LEN)
    toep = skew[:, :, A_TQ - 1:A_TQ - 1 + A_WIN]
    band = _band_mask()
    first = [band & (np.arange(A_WIN)[None, :] >= 2 * A_TQ - A_TQ * v) for v in range(3)]
    return jnp.where(jnp.asarray(np.stack(first))[:, None], toep[None], NEG)


def bias_block_grad(db):
    h = db.shape[0]
    n_lo = A_WIN - 1 - 2 * A_TQ - A_MAX_REL
    skew = jnp.pad(db, ((0, 0), (0, 0), (A_TQ - 1, 0)))
    flat = jnp.pad(skew.reshape(h, A_TQ * _U_LEN), ((0, 0), (0, A_TQ)))
    ext = jnp.sum(flat.reshape(h, A_TQ, _U_LEN + 1), axis=1)[:, :_U_LEN][:, ::-1]
    n_tab = 2 * A_MAX_REL + 1
    first = jnp.sum(ext[:, :n_lo + 1], axis=1, keepdims=True)
    last = jnp.sum(ext[:, n_lo + n_tab - 1:], axis=1, keepdims=True)
    return jnp.concatenate([first, ext[:, n_lo + 1:n_lo + n_tab - 1], last], axis=1)


def rope_tabs(t):
    inv = 1.0 / (10000.0 ** (jnp.arange(0, ROPE, 2, dtype=F32) / ROPE))
    ang = jnp.arange(t, dtype=F32)[:, None] * inv[None, :]
    cos, sin = jnp.cos(ang), jnp.sin(ang)
    z = lambda w: jnp.zeros((t, w), F32)
    ck = jnp.concatenate([cos, cos, z(96)], axis=1)
    s1k = jnp.concatenate([-sin, z(112)], axis=1)
    s2k = jnp.concatenate([z(16), sin, z(96)], axis=1)
    cq = jnp.concatenate([jnp.ones((t, HD), F32), cos, cos, z(32)], axis=1)
    s1q = jnp.concatenate([z(HD), -sin, z(48)], axis=1)
    s2q = jnp.concatenate([z(HD + 16), sin, z(32)], axis=1)
    return jnp.stack([cq, s1q, s2q, ck, s1k, s2k])


def _pad_lanes(v, width):
    return jnp.pad(v, ((0, 0), (0, width - v.shape[1])))


LATE = ("w_in", "b_w_uq", "b_w_ukv", "w_out", "ffn2_w_gate", "ffn2_w_up", "ffn2_w_down")
FFN2 = ("ffn2_w_gate", "ffn2_w_up", "ffn2_w_down")


def kernel_layout(gathered):
    w = {n: v for n, v in gathered.items() if n.startswith("ffn")}
    if "w_in" in gathered:
        w["w_in"] = jnp.pad(gathered["w_in"].reshape(IN_COLS, D_MODEL), ((0, PROJ_W - IN_COLS), (0, 0)))
        uq = gathered["b_w_uq"].reshape(HEADS, HD + ROPE, 256)
        w["b_w_uq"] = jnp.pad(uq, ((0, 0), (0, LANES - HD - ROPE), (0, 0))).reshape(HEADS * LANES, 256)
        w["b_w_ukv"] = _shards_to_cols(gathered["b_w_ukv"])
        w["w_out"] = gathered["w_out"].reshape(N_SHARD * gathered["w_out"].shape[1], D_MODEL)
    return w


def local_step(x, target, w, late=None):
    t = x.shape[0]
    gq = jnp.tile(w["a_q_norm"], (1, HEADS))
    gk = jnp.tile(w["a_k_norm"], (1, HEADS))
    gq128 = _pad_lanes(jnp.concatenate([w["b_q_nope_norm"], w["b_q_rope_norm"]], axis=1), LANES)
    gk128 = _pad_lanes(w["b_k_nope_norm"], LANES)
    gkr128 = _pad_lanes(w["b_k_rope_norm"], LANES)
    tabs = rope_tabs(t)
    bias = bias_block(w["a_rel_bias"])

    if late is None:
        x1, gate1, up1 = ffn_fwd(x, w["ffn1_norm"], w["ffn1_w_gate"], w["ffn1_w_up"], w["ffn1_w_down"], "ffn_fwd")
    else:
        own, shard = late
        x1, gate1, up1, *got = ffn_fwd(x, w["ffn1_norm"], w["ffn1_w_gate"], w["ffn1_w_up"], w["ffn1_w_down"],
                                       "ffn_fwd_gather", gather=own)
        w = dict(w, **kernel_layout({n: lax.dynamic_update_index_in_dim(g_, o_, shard, 0)
                                     for n, g_, o_ in zip(LATE, got, own)}))
    h, proj = mix_proj(x1, w["mix_norm"], w["w_in"])
    qa, ka, va, cqn, ckvn = prep1_fwd(proj, gq, gk, w["b_q_lat_norm"], w["b_kv_lat_norm"])
    qf, kf, vp = prep2_fwd(cqn, ckvn, proj, w["b_w_uq"], w["b_w_ukv"], gq128, gk128, gkr128, tabs)
    oa = attn_a_fwd(qa, ka, va, bias)
    ob_t, lse = attn_b_fwd(qf, kf, vp)
    x2 = out_proj(x1, oa, ob_t, w["w_out"])
    x3, gate2, up2 = ffn_fwd(x2, w["ffn2_norm"], w["ffn2_w_gate"], w["ffn2_w_up"], w["ffn2_w_down"], "ffn_fwd")

    g = {}
    dx3, g["final_norm"], loss = final_loss(x3, w["final_norm"], target)
    g["ffn2_w_gate"], g["ffn2_w_up"], g["ffn2_w_down"], dhp = ffn_bwd(
        x2, dx3, w["ffn2_norm"], w["ffn2_w_gate"], w["ffn2_w_up"], w["ffn2_w_down"], gate2, up2, "ffn_bwd")
    dx2, g["ffn2_norm"] = norm_bwd(x2, w["ffn2_norm"], dhp, dx3, "ffn_norm_bwd")
    d_oa, d_ob, d_ob_t = out_proj_bwd(dx2, w["w_out"])
    g["w_out"] = jnp.concatenate([matmul(oa, dx2, "tn", "w_out_a_bwd", 512, 1024, 512, BF16),
                                  matmul(ob_t, dx2, "nn", "w_out_b_bwd", 512, 1024, 512, BF16)], axis=0)
    early = [g[n] for n in FFN2] + [g["w_out"].reshape(N_SHARD, -1, D_MODEL)]
    dqf, dkf, dvp, *landed_early = attn_b_bwd(qf, kf, vp, d_ob, d_ob_t, ob_t, lse,
                                              scatter=() if late is None else early)
    dqa, dkp, dvpa, dbias = attn_a_bwd(qa, ka, va, bias, d_oa)
    dcq, dckv, dkr, dwuq, dwukv, dgq128, dgk128, dgkr128 = prep2_bwd(
        cqn, ckvn, proj, w["b_w_uq"], w["b_w_ukv"], dqf, dkf, dvp, gq128, gk128, gkr128, tabs)
    g["b_w_uq"], g["b_w_ukv"] = dwuq.astype(BF16), dwukv.astype(BF16)
    dx1, g["w_in"], g["mix_norm"], dgq, dgk, g["b_q_lat_norm"], g["b_kv_lat_norm"] = mix_bwd(
        proj, x1, h, dx2, w["w_in"], w["mix_norm"], dqa, dkp, dvpa, dcq, dckv, dkr, gq, gk,
        w["b_q_lat_norm"], w["b_kv_lat_norm"])
    mid = [g["w_in"][:IN_COLS].reshape(N_SHARD, IN_COLS // N_SHARD, D_MODEL),
           g["b_w_uq"].reshape(HEADS, LANES, 256)[:, :HD + ROPE].reshape(N_SHARD, -1, 256),
           _cols_to_shards(g["b_w_ukv"])]
    g["ffn1_w_gate"], g["ffn1_w_up"], g["ffn1_w_down"], dhp, *landed_late = ffn_bwd(
        x, dx1, w["ffn1_norm"], w["ffn1_w_gate"], w["ffn1_w_up"], w["ffn1_w_down"], gate1, up1,
        "ffn_bwd" if late is None else "ffn_bwd_scatter", scatter=() if late is None else mid,
        spread=None if late is None else late[1])
    grad_x, g["ffn1_norm"] = norm_bwd(x, w["ffn1_norm"], dhp, dx1, "ffn_norm_bwd")
    landed = dict(zip(FFN2 + ("w_out", "w_in", "b_w_uq", "b_w_ukv", "ffn1_w_gate", "ffn1_w_up", "ffn1_w_down"),
                      landed_early + landed_late))

    g["a_q_norm"] = jnp.sum(dgq.reshape(HEADS, HD), axis=0, keepdims=True)
    g["a_k_norm"] = jnp.sum(dgk.reshape(HEADS, HD), axis=0, keepdims=True)
    g["a_rel_bias"] = bias_block_grad(dbias)
    g["b_q_nope_norm"] = dgq128[:, :HD]
    g["b_q_rope_norm"] = dgq128[:, HD:HD + ROPE]
    g["b_k_nope_norm"] = dgk128[:, :HD]
    g["b_k_rope_norm"] = dgkr128[:, :ROPE]
    return loss, grad_x, g, landed


ANY = pl.BlockSpec(memory_space=pl.ANY)
N_DEV = 8


def _place():
    return lax.axis_index("x"), lax.axis_index("y"), lax.axis_index("c")


def _flip(v, bit):
    return 1 - v if bit else v


BF16_ROWS = 16


def _split_axis(shape):
    return 0 if (shape[0] // 2) % BF16_ROWS == 0 else 1


def _half_shape(shape):
    axis = _split_axis(shape)
    return tuple(s // 2 if a == axis else s for a, s in enumerate(shape))


def _half(shape, core):
    axis = _split_axis(shape)
    size = shape[axis] // 2
    return tuple(pl.ds(core * size, size) if a == axis else slice(None) for a in range(2))


class GatherPlan:
    def __init__(self, ws):
        self.shapes = [w.shape for w in ws]
        self.n = len(ws)
        self.out_shape = [jax.ShapeDtypeStruct((N_SHARD,) + w.shape, w.dtype) for w in ws]
        self.scratch = [pltpu.SemaphoreType.DMA((6 * self.n,)), pltpu.SemaphoreType.DMA((6 * self.n,))]

    def _copies(self, ins, outs, sems):
        x, y, c = _place()
        s_me = 2 * x + y
        sibling = (x, y, 1 - c)
        send_sems, recv_sems = sems

        def remote(k, src, dst, to):
            return pltpu.make_async_remote_copy(src_ref=src, dst_ref=dst, send_sem=send_sems.at[k],
                                                recv_sem=recv_sems.at[k], device_id=to, device_id_type=MESH)

        ici, fwd = [], []
        for a in range(self.n):
            mine, theirs = _half(self.shapes[a], c), _half(self.shapes[a], 1 - c)
            for j, (cx, cy) in enumerate([(1 - x, y), (x, 1 - y), (1 - x, 1 - y)]):
                got = outs[a].at[(2 * cx + cy,) + mine]
                ici.append((remote(6 * a + j, ins[a].at[mine], outs[a].at[(s_me,) + mine], (cx, cy, c)),
                            remote(6 * a + j, got, got, (cx, cy, c))))
                passed = outs[a].at[(2 * cx + cy,) + theirs]
                fwd.append((remote(6 * a + 3 + j, got, got, sibling), remote(6 * a + 3 + j, passed, passed, sibling)))
        return ici, fwd

    def start(self, ins, outs, sems):
        for send, _ in self._copies(ins, outs, sems)[0]:
            send.start()

    def forward(self, ins, outs, sems):
        ici, fwd = self._copies(ins, outs, sems)
        for (_, arrival), (send, _) in zip(ici, fwd):
            arrival.wait_recv()
            send.start()

    def finish(self, ins, outs, sems):
        ici, fwd = self._copies(ins, outs, sems)
        for _, arrival in fwd:
            arrival.wait_recv()
        for send, _ in ici + fwd:
            send.wait_send()


def allgather_shards(ws):
    plan = GatherPlan(ws)
    n = plan.n

    def body(*refs):
        ins, outs, sems = refs[:n], refs[n:2 * n], refs[2 * n:]
        plan.start(ins, outs, sems)
        plan.forward(ins, outs, sems)
        plan.finish(ins, outs, sems)

    return pl.pallas_call(
        body, name="allgather_shards", in_specs=[ANY] * n, out_specs=[ANY] * n,
        out_shape=plan.out_shape, scratch_shapes=plan.scratch,
    )(*ws)


class ScatterPlan:
    def __init__(self, gs):
        self.shapes = [g.shape[1:] for g in gs]
        self.n = len(gs)
        self.out_shape = [jax.ShapeDtypeStruct((N_DEV,) + _half_shape(g.shape[1:]), g.dtype) for g in gs]
        self.scratch = [pltpu.SemaphoreType.DMA((7 * self.n,)), pltpu.SemaphoreType.DMA((7 * self.n,)),
                        pltpu.SemaphoreType.DMA((self.n,))]

    def _copies(self, ins, outs, sems):
        x, y, c = _place()
        me = 4 * x + 2 * y + c
        send_sems, recv_sems, local_sems = sems
        local, sends, arrivals = [], [], []
        for a in range(self.n):
            piece = lambda px, py, pc, a=a: ins[a].at[(2 * px + py,) + _half(self.shapes[a], pc)]
            local.append(pltpu.make_async_copy(piece(x, y, c), outs[a].at[me], local_sems.at[a]))
            for k in range(1, N_DEV):
                px, py, pc = _flip(x, k & 4), _flip(y, k & 2), _flip(c, k & 1)
                sem = dict(send_sem=send_sems.at[7 * a + k - 1], recv_sem=recv_sems.at[7 * a + k - 1],
                           device_id=(px, py, pc), device_id_type=MESH)
                sends.append(pltpu.make_async_remote_copy(
                    src_ref=piece(px, py, pc), dst_ref=outs[a].at[me], **sem))
                slot = outs[a].at[4 * px + 2 * py + pc]
                arrivals.append(pltpu.make_async_remote_copy(src_ref=slot, dst_ref=slot, **sem))
        return local, sends, arrivals

    def start(self, ins, outs, sems):
        local, sends, _ = self._copies(ins, outs, sems)
        for cp in local + sends:
            cp.start()

    def finish(self, ins, outs, sems):
        local, sends, arrivals = self._copies(ins, outs, sems)
        for cp in arrivals:
            cp.wait_recv()
        for cp in sends:
            cp.wait_send()
        for cp in local:
            cp.wait()


def scatter_partials(gs):
    plan = ScatterPlan(gs)
    n = plan.n

    def body(*refs):
        ins, outs, sems = refs[:n], refs[n:2 * n], refs[2 * n:]
        plan.start(ins, outs, sems)
        plan.finish(ins, outs, sems)

    return pl.pallas_call(
        body, name="scatter_partials", in_specs=[ANY] * n, out_specs=[ANY] * n,
        out_shape=plan.out_shape, scratch_shapes=plan.scratch,
    )(*gs)


def sum_slots(land, name):
    _, rows, cols = land.shape
    tr = rows // 2 if rows > 128 and (rows // 2) % BF16_ROWS == 0 else rows

    def body(l_ref, o_ref):
        acc = l_ref[0].astype(F32)
        for s in range(1, N_DEV):
            acc = acc + l_ref[s].astype(F32)
        o_ref[...] = acc

    return pl.pallas_call(
        body, name=name, grid=(rows // tr,),
        in_specs=[pl.BlockSpec((N_DEV, tr, cols), lambda i: (0, i, 0))],
        out_specs=pl.BlockSpec((tr, cols), lambda i: (i, 0)),
        out_shape=jax.ShapeDtypeStruct((rows, cols), F32),
        compiler_params=_params(("parallel",)),
    )(land)


def join_halves(hs, shapes):
    n = len(hs)

    def body(*refs):
        ins, outs = refs[:n], refs[n:2 * n]
        send_sems, recv_sems = refs[2 * n:]
        x, y, c = _place()
        sends = []
        for a in range(n):
            mine = outs[a].at[_half(shapes[a], c)]
            sends.append(pltpu.make_async_remote_copy(
                src_ref=ins[a], dst_ref=mine, send_sem=send_sems.at[a], recv_sem=recv_sems.at[a],
                device_id=(x, y, 1 - c), device_id_type=MESH))
            sends[-1].start()
        for a in range(n):
            theirs = outs[a].at[_half(shapes[a], 1 - c)]
            pltpu.make_async_remote_copy(
                src_ref=theirs, dst_ref=theirs, send_sem=send_sems.at[a], recv_sem=recv_sems.at[a],
                device_id=(x, y, 1 - c), device_id_type=MESH).wait_recv()
        for cp in sends:
            cp.wait_send()

    return pl.pallas_call(
        body, name="join_halves",
        in_specs=[ANY] * n, out_specs=[ANY] * n,
        out_shape=[jax.ShapeDtypeStruct(tuple(s), h.dtype) for s, h in zip(shapes, hs)],
        scratch_shapes=[pltpu.SemaphoreType.DMA((n,)), pltpu.SemaphoreType.DMA((n,))],
    )(*hs)


def allreduce_small(vec):
    def body(v_ref, o_ref, land_ref, send_sems, recv_sems):
        x, y, c = _place()
        me = 4 * x + 2 * y + c
        land_ref[me] = v_ref[...]
        sends = []
        for k in range(1, N_DEV):
            px, py, pc = _flip(x, k & 4), _flip(y, k & 2), _flip(c, k & 1)
            sends.append(pltpu.make_async_remote_copy(
                src_ref=v_ref, dst_ref=land_ref.at[me], send_sem=send_sems.at[k - 1], recv_sem=recv_sems.at[k - 1],
                device_id=(px, py, pc), device_id_type=MESH))
            sends[-1].start()
        for k in range(1, N_DEV):
            px, py, pc = _flip(x, k & 4), _flip(y, k & 2), _flip(c, k & 1)
            slot = land_ref.at[4 * px + 2 * py + pc]
            pltpu.make_async_remote_copy(
                src_ref=slot, dst_ref=slot, send_sem=send_sems.at[k - 1], recv_sem=recv_sems.at[k - 1],
                device_id=(px, py, pc), device_id_type=MESH).wait_recv()
        for cp in sends:
            cp.wait_send()
        acc = land_ref[0]
        for s in range(1, N_DEV):
            acc = acc + land_ref[s]
        o_ref[...] = acc

    vm = pl.BlockSpec(memory_space=pltpu.VMEM)
    return pl.pallas_call(
        body, name="allreduce_small",
        in_specs=[vm], out_specs=vm,
        out_shape=jax.ShapeDtypeStruct(vec.shape, F32),
        scratch_shapes=[pltpu.VMEM((N_DEV,) + vec.shape, F32), pltpu.SemaphoreType.DMA((N_DEV - 1,)),
                        pltpu.SemaphoreType.DMA((N_DEV - 1,))],
    )(vec)


def adamw(w, g, m, v, name):
    rows, cols = w.shape
    tr = rows
    while tr * cols * 4 * 14 > 24 * 1024 * 1024 and tr % 16 == 0:
        tr //= 2
    c1 = 1.0 - ADAM_B1 ** ADAM_STEP
    c2 = 1.0 - ADAM_B2 ** ADAM_STEP

    def body(w_ref, g_ref, m_ref, v_ref, d_ref, nm_ref, nv_ref):
        gv = g_ref[...]
        nm = ADAM_B1 * m_ref[...] + (1.0 - ADAM_B1) * gv
        nv = ADAM_B2 * v_ref[...] + (1.0 - ADAM_B2) * (gv * gv)
        nm_ref[...] = nm
        nv_ref[...] = nv
        d_ref[...] = -ADAM_LR * ((nm / c1) / (jnp.sqrt(nv / c2) + ADAM_EPS) + ADAM_WD * w_ref[...])

    blk = pl.BlockSpec((tr, cols), lambda i: (i, 0))
    return pl.pallas_call(
        body, name=name, grid=(rows // tr,),
        in_specs=[blk] * 4, out_specs=[blk] * 3,
        out_shape=[jax.ShapeDtypeStruct((rows, cols), F32)] * 3,
        compiler_params=_params(("parallel",)),
    )(w, g, m, v)


BIG = ("ffn1_w_gate", "ffn1_w_up", "ffn1_w_down", "w_in", "b_w_uq", "b_w_ukv", "w_out",
       "ffn2_w_gate", "ffn2_w_up", "ffn2_w_down")
SMALL = ("ffn1_norm", "mix_norm", "a_q_norm", "a_k_norm", "a_rel_bias", "b_q_lat_norm", "b_kv_lat_norm",
         "b_q_nope_norm", "b_q_rope_norm", "b_k_nope_norm", "b_k_rope_norm", "ffn2_norm", "final_norm")
WEIGHTS = ("ffn1_norm", "ffn1_w_gate", "ffn1_w_up", "ffn1_w_down", "mix_norm", "w_in", "a_q_norm", "a_k_norm",
           "a_rel_bias", "b_q_lat_norm", "b_w_uq", "b_kv_lat_norm", "b_w_ukv", "b_q_nope_norm", "b_q_rope_norm",
           "b_k_nope_norm", "b_k_rope_norm", "w_out", "ffn2_norm", "ffn2_w_gate", "ffn2_w_up", "ffn2_w_down",
           "final_norm")
TRANSPOSED = ("ffn1_w_gate", "ffn1_w_up", "ffn2_w_gate", "ffn2_w_up", "w_in", "b_w_uq")
PACK_SHAPE = (8, 1024)


def _pack_small(d, last=None):
    flat = [d[n].reshape(-1) for n in SMALL]
    used = sum(f.shape[0] for f in flat)
    total = PACK_SHAPE[0] * PACK_SHAPE[1]
    tail = jnp.zeros((total - used - 1,), F32)
    end = jnp.zeros((1,), F32) if last is None else last.reshape(1)
    return jnp.concatenate(flat + [tail, end]).reshape(PACK_SHAPE)


def _unpack_small(p, like):
    flat = p.reshape(-1)
    out, off = {}, 0
    for n in SMALL:
        size = like[n].size
        out[n] = flat[off:off + size].reshape(like[n].shape)
        off += size
    return out, flat[-1]


def _cols_to_shards(g):
    rows, cols = g.shape
    return g.reshape(rows, N_SHARD, cols // N_SHARD).transpose(1, 0, 2)


def _shards_to_cols(g):
    return g.transpose(1, 0, 2).reshape(g.shape[1], -1)


def kernel(x, ffn1_norm, ffn1_w_gate, ffn1_w_up, ffn1_w_down, mix_norm, w_in, a_q_norm, a_k_norm, a_rel_bias, b_q_lat_norm, b_w_uq, b_kv_lat_norm, b_w_ukv, b_q_nope_norm, b_q_rope_norm, b_k_nope_norm, b_k_rope_norm, w_out, ffn2_norm, ffn2_w_gate, ffn2_w_up, ffn2_w_down, final_norm, loss_target, m_ffn1_norm, m_ffn1_w_gate, m_ffn1_w_up, m_ffn1_w_down, m_mix_norm, m_w_in, m_a_q_norm, m_a_k_norm, m_a_rel_bias, m_b_q_lat_norm, m_b_w_uq, m_b_kv_lat_norm, m_b_w_ukv, m_b_q_nope_norm, m_b_q_rope_norm, m_b_k_nope_norm, m_b_k_rope_norm, m_w_out, m_ffn2_norm, m_ffn2_w_gate, m_ffn2_w_up, m_ffn2_w_down, m_final_norm, v_ffn1_norm, v_ffn1_w_gate, v_ffn1_w_up, v_ffn1_w_down, v_mix_norm, v_w_in, v_a_q_norm, v_a_k_norm, v_a_rel_bias, v_b_q_lat_norm, v_b_w_uq, v_b_kv_lat_norm, v_b_w_ukv, v_b_q_nope_norm, v_b_q_rope_norm, v_b_k_nope_norm, v_b_k_rope_norm, v_w_out, v_ffn2_norm, v_ffn2_w_gate, v_ffn2_w_up, v_ffn2_w_down, v_final_norm):
    args = locals()
    view = lambda a, n: a[0].T if n in TRANSPOSED else a[0]
    wts = {n: view(args[n], n) for n in WEIGHTS}
    mom = {n: view(args["m_" + n], n) for n in WEIGHTS}
    var = {n: view(args["v_" + n], n) for n in WEIGHTS}

    shard = 2 * lax.axis_index("x") + lax.axis_index("y")
    core = lax.axis_index("c")
    first = [n for n in BIG if n not in LATE]
    own = [wts[n].astype(BF16) for n in first]
    w = {n: wts[n] if n == "a_rel_bias" else wts[n][None] for n in SMALL}
    w.update(kernel_layout({n: lax.dynamic_update_index_in_dim(got, mine, shard, 0)
                            for n, got, mine in zip(first, allgather_shards(own), own)}))

    loss, grad_x, g, landed = local_step(x[0], loss_target[0], w,
                                         late=([wts[n].astype(BF16) for n in LATE], shard))

    me = 2 * shard + core
    for n in first:
        piece = lax.dynamic_slice(g[n], (shard, core * (FS // 2), 0), (1, FS // 2, D_MODEL))
        landed[n] = lax.dynamic_update_slice(landed[n], piece, (me, 0, 0))
    halves = [sum_slots(landed[n], "sum_slots") for n in BIG]
    shapes = [wts[n].shape for n in BIG]
    axes = [_split_axis(s) for s in shapes]
    grads = dict(zip(BIG, (lax.dynamic_update_slice_in_dim(got, mine, core * mine.shape[ax], ax)
                           for got, mine, ax in zip(join_halves(halves, shapes), halves, axes))))

    small_sum, loss_sum = _unpack_small(allreduce_small(_pack_small(g, loss[0, 0])), wts)
    grads.update(small_sum)

    delta, new_m, new_v = {}, {}, {}
    for n in BIG:
        delta[n], new_m[n], new_v[n] = adamw(wts[n], grads[n], mom[n], var[n], "adamw")
    packed = adamw(_pack_small(wts), _pack_small(grads), _pack_small(mom), _pack_small(var), "adamw_small")
    for dst, p in zip((delta, new_m, new_v), packed):
        dst.update(_unpack_small(p, wts)[0])

    lead = lambda d: [(d[n].T if n in TRANSPOSED else d[n])[None] for n in WEIGHTS]
    return (loss_sum, grad_x[None], *lead(grads), *lead(delta), *lead(new_m), *lead(new_v))
```

```python
import functools
import math

import numpy as np
import jax
import jax.numpy as jnp
from jax import lax
from jax.experimental import pallas as pl
from jax.experimental.pallas import tpu as pltpu

F32 = jnp.float32
BF16 = jnp.bfloat16
EPS = 1e-6
NEG = -1e30

D_MODEL = 1024
D_FF = 2816
N_SHARD = 4
FS = D_FF // N_SHARD
CHUNK = 64
A_LEFT = 8
A_MAX_REL = 128
HEADS = 8
HD = 64
ROPE = 32
PROJ_W = 2048
IN_COLS = 1952
B_SCALE = 96 ** -0.5
LANES = 128

ADAM_LR = 0.001
ADAM_B1 = 0.9
ADAM_B2 = 0.999
ADAM_EPS = 1e-08
ADAM_WD = 0.01
ADAM_STEP = 10

VMEM_LIMIT = 56 * 1024 * 1024

MESH = pl.DeviceIdType.MESH


def _dot(a, b):
    return lax.dot_general(a, b, (((1,), (0,)), ((), ())), preferred_element_type=F32)


def _dot_nt(a, b):
    return lax.dot_general(a, b, (((1,), (1,)), ((), ())), preferred_element_type=F32)


def _dot_tn(a, b):
    return lax.dot_general(a, b, (((0,), (0,)), ((), ())), preferred_element_type=F32)


def _params(sem):
    return pltpu.CompilerParams(dimension_semantics=sem, vmem_limit_bytes=VMEM_LIMIT)


def _rms(xv):
    r = lax.rsqrt(jnp.mean(xv * xv, axis=-1, keepdims=True) + EPS)
    return r, xv * r


def ffn_fwd(x, g, wg, wu, wd, name, gather=()):
    t, d = x.shape
    tm = 512
    ni = t // tm
    plan = GatherPlan(gather)
    n = plan.n

    def body(*refs):
        x_ref, g_ref, wg_ref, wu_ref, wd_ref = refs[:5]
        ins, (o_ref, gp_ref, up_ref), outs = refs[5:5 + n], refs[5 + n:8 + n], refs[8 + n:8 + 2 * n]
        h_ref, acc_ref = refs[8 + 2 * n:10 + 2 * n]
        sems = refs[10 + 2 * n:]
        i, j = pl.program_id(0), pl.program_id(1)
        if n:
            pl.when((i == 0) & (j == 0))(lambda: plan.start(ins, outs, sems))
            pl.when((i == (3 * ni) // 4) & (j == 0))(lambda: plan.forward(ins, outs, sems))

        @pl.when(j == 0)
        def _():
            _, xn = _rms(x_ref[...])
            h_ref[...] = (xn * g_ref[...]).astype(BF16)
            acc_ref[...] = jnp.zeros_like(acc_ref)

        h = h_ref[...]
        gp = _dot_nt(h, wg_ref[0])
        up = _dot_nt(h, wu_ref[0])
        gp_ref[0] = gp
        up_ref[0] = up
        a = (gp * jax.nn.sigmoid(gp) * up).astype(BF16)
        acc_ref[...] += _dot(a, wd_ref[0])

        @pl.when(j == N_SHARD - 1)
        def _():
            o_ref[...] = x_ref[...] + 0.5 * acc_ref[...]

        if n:
            pl.when((i == ni - 1) & (j == N_SHARD - 1))(lambda: plan.finish(ins, outs, sems))

    res = pl.pallas_call(
        body, name=name, grid=(ni, N_SHARD),
        in_specs=[pl.BlockSpec((tm, d), lambda i, j: (i, 0)),
                  pl.BlockSpec((1, d), lambda i, j: (0, 0)),
                  pl.BlockSpec((1, FS, d), lambda i, j: (j, 0, 0)),
                  pl.BlockSpec((1, FS, d), lambda i, j: (j, 0, 0)),
                  pl.BlockSpec((1, FS, d), lambda i, j: (j, 0, 0))] + [ANY] * n,
        out_specs=[pl.BlockSpec((tm, d), lambda i, j: (i, 0)), pl.BlockSpec((1, tm, FS), lambda i, j: (j, i, 0)),
                   pl.BlockSpec((1, tm, FS), lambda i, j: (j, i, 0))] + [ANY] * n,
        out_shape=[jax.ShapeDtypeStruct((t, d), F32), jax.ShapeDtypeStruct((N_SHARD, t, FS), F32),
                   jax.ShapeDtypeStruct((N_SHARD, t, FS), F32)] + plan.out_shape,
        scratch_shapes=[pltpu.VMEM((tm, d), BF16), pltpu.VMEM((tm, d), F32)] + (plan.scratch if n else []),
        compiler_params=_params(("arbitrary", "arbitrary")),
    )(x, g, wg, wu, wd, *gather)
    return res


def ffn_bwd(x, dout, g, wg, wu, wd, gate, up_pre, name, scatter=(), spread=None):
    t, d = x.shape
    tm = 512
    ni = t // tm
    hf = FS // 2
    plan = ScatterPlan(scatter)
    m = plan.n
    k = 0 if spread is None else 3
    steps = jnp.arange(N_SHARD, dtype=jnp.int32)
    order = steps if spread is None else (spread + 1 + steps) % N_SHARD

    def body(*refs):
        ord_ref, x_ref, do_ref, g_ref, wg_ref, wu_ref, wd_ref, gp_ref, up_ref = refs[:9]
        ins, (dwg_out, dwu_out, dwd_out, dhp_ref) = refs[9:9 + m], refs[9 + m:13 + m]
        outs, lands = refs[13 + m:13 + 2 * m], refs[13 + 2 * m:13 + 2 * m + k]
        dwg_ref, dwu_ref, dwd_ref = refs[13 + 2 * m + k:16 + 2 * m + k]
        sems = refs[16 + 2 * m + k:19 + 2 * m + k] if m else ()
        stage_ref = refs[-3] if k else None
        j, i = pl.program_id(0), pl.program_id(1)
        if m:
            pl.when((j == 0) & (i == 0))(lambda: plan.start(ins, outs, sems))

        def chunk_copies(jj):
            send_sems, recv_sems = refs[-2:]
            px, py, pc = _place()
            me = 4 * px + 2 * py + pc
            tx, ty = ord_ref[jj] // 2, ord_ref[jj] % 2
            copies = []
            for n_ in range(3):
                for h_ in range(2):
                    copies.append((pltpu.make_async_remote_copy(
                        src_ref=stage_ref.at[n_, pl.ds(h_ * hf, hf)], dst_ref=lands[n_].at[me],
                        send_sem=send_sems.at[6 * jj + 2 * n_ + h_], recv_sem=recv_sems.at[3 * me + n_],
                        device_id=(tx, ty, h_), device_id_type=MESH), (tx != px) | (ty != py) | (pc != h_)))
            return copies

        def arrivals():
            send_sems, recv_sems = refs[-2:]
            px, py, pc = _place()
            me = 4 * px + 2 * py + pc
            for s_ in range(N_DEV):
                for n_ in range(3):
                    slot = lands[n_].at[s_]
                    cp = pltpu.make_async_remote_copy(
                        src_ref=slot, dst_ref=slot, send_sem=send_sems.at[0], recv_sem=recv_sems.at[3 * s_ + n_],
                        device_id=(px, py, pc), device_id_type=MESH)
                    pl.when(me != s_)(cp.wait_recv)

        _, xn = _rms(x_ref[...])
        h = (xn * g_ref[...]).astype(BF16)
        dz = (0.5 * do_ref[...]).astype(BF16)
        wgv, wuv, wdv = wg_ref[0], wu_ref[0], wd_ref[0]
        gp, up = gp_ref[0], up_ref[0]
        s = jax.nn.sigmoid(gp)
        sg = gp * s
        a = (sg * up).astype(BF16)
        da = _dot_nt(dz, wdv)
        dup = (da * sg).astype(BF16)
        dgp = (da * up * (s * (1.0 + gp * (1.0 - s)))).astype(BF16)

        @pl.when(i == 0)
        def _():
            dwg_ref[...] = jnp.zeros_like(dwg_ref)
            dwu_ref[...] = jnp.zeros_like(dwu_ref)
            dwd_ref[...] = jnp.zeros_like(dwd_ref)

        dwd_ref[...] += _dot_tn(a, dz)
        dwg_ref[...] += _dot_tn(dgp, h)
        dwu_ref[...] += _dot_tn(dup, h)
        dhp_ref[0] = (_dot(dgp, wgv) + _dot(dup, wuv)).astype(BF16)

        @pl.when(i == ni - 1)
        def _():
            dwg_out[0] = dwg_ref[...].astype(BF16)
            dwu_out[0] = dwu_ref[...].astype(BF16)
            dwd_out[0] = dwd_ref[...].astype(BF16)
            if k:
                @pl.when(j >= 1)
                def _():
                    for cp, leaves in chunk_copies(j - 1):
                        pl.when(leaves)(cp.wait_send)
                for n_, acc in enumerate((dwg_ref, dwu_ref, dwd_ref)):
                    stage_ref[n_] = acc[...].astype(BF16)
                for cp, leaves in chunk_copies(j):
                    pl.when(leaves)(cp.start)

                @pl.when(j == N_SHARD - 1)
                def _():
                    for cp, leaves in chunk_copies(N_SHARD - 1):
                        pl.when(leaves)(cp.wait_send)
                    arrivals()

        if m:
            pl.when((j == N_SHARD - 1) & (i == ni - 1))(lambda: plan.finish(ins, outs, sems))

    chunk = pl.BlockSpec((1, FS, d), lambda j, i, o: (o[j], 0, 0))
    tok = pl.BlockSpec((tm, d), lambda j, i, o: (i, 0))
    pre = pl.BlockSpec((1, tm, FS), lambda j, i, o: (o[j], i, 0))
    grid_spec = pltpu.PrefetchScalarGridSpec(
        num_scalar_prefetch=1, grid=(N_SHARD, ni),
        in_specs=[tok, tok, pl.BlockSpec((1, d), lambda j, i, o: (0, 0)), chunk, chunk, chunk, pre, pre] + [ANY] * m,
        out_specs=[chunk, chunk, chunk, pl.BlockSpec((1, tm, d), lambda j, i, o: (o[j], i, 0))] + [ANY] * (m + k),
        scratch_shapes=[pltpu.VMEM((FS, d), F32), pltpu.VMEM((FS, d), F32), pltpu.VMEM((FS, d), F32)]
        + (plan.scratch if m else [])
        + ([pltpu.VMEM((3, FS, d), BF16), pltpu.SemaphoreType.DMA((6 * N_SHARD,)),
            pltpu.SemaphoreType.DMA((3 * N_DEV,))] if k else []))
    return pl.pallas_call(
        body, name=name, grid_spec=grid_spec,
        out_shape=[jax.ShapeDtypeStruct((N_SHARD, FS, d), BF16),
                   jax.ShapeDtypeStruct((N_SHARD, FS, d), BF16),
                   jax.ShapeDtypeStruct((N_SHARD, FS, d), BF16),
                   jax.ShapeDtypeStruct((N_SHARD, t, d), BF16)] + plan.out_shape
        + [jax.ShapeDtypeStruct((N_DEV, hf, d), BF16)] * k,
        compiler_params=_params(("arbitrary", "arbitrary")),
    )(order, x, dout, g, wg, wu, wd, gate, up_pre, *scatter)


def norm_bwd(x, g, dhp, dres, name):
    t, d = x.shape
    p = dhp.shape[0]
    tm = 512

    def body(x_ref, g_ref, dhp_ref, dres_ref, dx_ref, dg_ref):
        i = pl.program_id(0)
        r, xn = _rms(x_ref[...])
        dh = dhp_ref[0].astype(F32)
        for q in range(1, p):
            dh = dh + dhp_ref[q].astype(F32)
        dhg = dh * g_ref[...]
        dx_ref[...] = dres_ref[...] + r * (dhg - xn * jnp.mean(dhg * xn, axis=-1, keepdims=True))

        @pl.when(i == 0)
        def _():
            dg_ref[...] = jnp.zeros_like(dg_ref)

        dg_ref[...] += jnp.sum(dh * xn, axis=0, keepdims=True)

    return pl.pallas_call(
        body, name=name, grid=(t // tm,),
        in_specs=[pl.BlockSpec((tm, d), lambda i: (i, 0)),
                  pl.BlockSpec((1, d), lambda i: (0, 0)),
                  pl.BlockSpec((p, tm, d), lambda i: (0, i, 0)),
                  pl.BlockSpec((tm, d), lambda i: (i, 0))],
        out_specs=[pl.BlockSpec((tm, d), lambda i: (i, 0)),
                   pl.BlockSpec((1, d), lambda i: (0, 0))],
        out_shape=[jax.ShapeDtypeStruct((t, d), F32), jax.ShapeDtypeStruct((1, d), F32)],
        compiler_params=_params(("arbitrary",)),
    )(x, g, dhp, dres)


def final_loss(x, g, target):
    t, d = x.shape
    tm = 512

    def body(x_ref, g_ref, t_ref, dx_ref, dg_ref, loss_ref):
        i = pl.program_id(0)
        r, xn = _rms(x_ref[...])
        gv = g_ref[...]
        e = xn * gv - t_ref[...]
        dy = e * (1.0 / d)
        dhg = dy * gv
        dx_ref[...] = r * (dhg - xn * jnp.mean(dhg * xn, axis=-1, keepdims=True))

        @pl.when(i == 0)
        def _():
            dg_ref[...] = jnp.zeros_like(dg_ref)
            loss_ref[...] = jnp.zeros_like(loss_ref)

        dg_ref[...] += jnp.sum(dy * xn, axis=0, keepdims=True)
        part = jnp.sum(jnp.sum(e * e, axis=-1, keepdims=True), axis=0, keepdims=True) * (0.5 / d)
        loss_ref[...] += jnp.broadcast_to(part, loss_ref.shape)

    return pl.pallas_call(
        body, name="final_loss", grid=(t // tm,),
        in_specs=[pl.BlockSpec((tm, d), lambda i: (i, 0)),
                  pl.BlockSpec((1, d), lambda i: (0, 0)),
                  pl.BlockSpec((tm, d), lambda i: (i, 0))],
        out_specs=[pl.BlockSpec((tm, d), lambda i: (i, 0)),
                   pl.BlockSpec((1, d), lambda i: (0, 0)),
                   pl.BlockSpec((1, LANES), lambda i: (0, 0))],
        out_shape=[jax.ShapeDtypeStruct((t, d), F32), jax.ShapeDtypeStruct((1, d), F32),
                   jax.ShapeDtypeStruct((1, LANES), F32)],
        compiler_params=_params(("arbitrary",)),
    )(x, g, target)


def mix_proj(x, g, w):
    t, d = x.shape
    n = w.shape[0]
    tm = 512

    def body(x_ref, g_ref, w_ref, h_ref, p_ref):
        _, xn = _rms(x_ref[...])
        h = (xn * g_ref[...]).astype(BF16)
        h_ref[...] = h
        p_ref[...] = _dot_nt(h, w_ref[...])

    return pl.pallas_call(
        body, name="mix_proj", grid=(t // tm,),
        in_specs=[pl.BlockSpec((tm, d), lambda i: (i, 0)),
                  pl.BlockSpec((1, d), lambda i: (0, 0)),
                  pl.BlockSpec((n, d), lambda i: (0, 0))],
        out_specs=[pl.BlockSpec((tm, d), lambda i: (i, 0)),
                   pl.BlockSpec((tm, n), lambda i: (i, 0))],
        out_shape=[jax.ShapeDtypeStruct((t, d), BF16), jax.ShapeDtypeStruct((t, n), F32)],
        compiler_params=_params(("parallel",)),
    )(x, g, w)


def matmul(a, b, mode, name, tm, tn, tk, out_dtype=F32):
    if mode == "nn":
        (m, k), n = a.shape, b.shape[1]
        a_spec = pl.BlockSpec((tm, tk), lambda i, j, q: (i, q))
        b_spec = pl.BlockSpec((tk, tn), lambda i, j, q: (q, j))
        dot = _dot
    elif mode == "nt":
        (m, k), n = a.shape, b.shape[0]
        a_spec = pl.BlockSpec((tm, tk), lambda i, j, q: (i, q))
        b_spec = pl.BlockSpec((tn, tk), lambda i, j, q: (j, q))
        dot = _dot_nt
    else:
        (k, m), n = a.shape, b.shape[1]
        a_spec = pl.BlockSpec((tk, tm), lambda i, j, q: (q, i))
        b_spec = pl.BlockSpec((tk, tn), lambda i, j, q: (q, j))
        dot = _dot_tn
    assert m % tm == 0 and n % tn == 0 and k % tk == 0, (m, n, k, tm, tn, tk)
    nk = k // tk

    def body(a_ref, b_ref, o_ref, acc_ref):
        q = pl.program_id(2)

        @pl.when(q == 0)
        def _():
            acc_ref[...] = jnp.zeros_like(acc_ref)

        acc_ref[...] += dot(a_ref[...].astype(BF16), b_ref[...].astype(BF16))

        @pl.when(q == nk - 1)
        def _():
            o_ref[...] = acc_ref[...].astype(out_dtype)

    return pl.pallas_call(
        body, name=name, grid=(m // tm, n // tn, nk),
        in_specs=[a_spec, b_spec],
        out_specs=pl.BlockSpec((tm, tn), lambda i, j, q: (i, j)),
        out_shape=jax.ShapeDtypeStruct((m, n), out_dtype),
        scratch_shapes=[pltpu.VMEM((tm, tn), F32)],
        compiler_params=_params(("parallel", "parallel", "arbitrary")),
    )(a, b)


def out_proj(x, oa, ob_t, w):
    t, d = x.shape
    half = oa.shape[1]
    tm = 512

    def body(x_ref, oa_ref, obt_ref, w_ref, o_ref):
        o_ref[...] = (x_ref[...] + _dot(oa_ref[...], w_ref[0:half, :])
                      + _dot_tn(obt_ref[...], w_ref[half:2 * half, :]))

    return pl.pallas_call(
        body, name="out_proj", grid=(t // tm,),
        in_specs=[pl.BlockSpec((tm, d), lambda i: (i, 0)),
                  pl.BlockSpec((tm, half), lambda i: (i, 0)),
                  pl.BlockSpec((half, tm), lambda i: (0, i)),
                  pl.BlockSpec((2 * half, d), lambda i: (0, 0))],
        out_specs=pl.BlockSpec((tm, d), lambda i: (i, 0)),
        out_shape=jax.ShapeDtypeStruct((t, d), F32),
        compiler_params=_params(("parallel",)),
    )(x, oa, ob_t, w)


def out_proj_bwd(dx, w):
    t, d = dx.shape
    half = w.shape[0] // 2
    tm = 512

    def body(dx_ref, w_ref, da_ref, db_ref, dbt_ref):
        dxb = dx_ref[...].astype(BF16)
        da_ref[...] = _dot_nt(dxb, w_ref[0:half, :]).astype(BF16)
        db_ref[...] = _dot_nt(dxb, w_ref[half:2 * half, :]).astype(BF16)
        dbt_ref[...] = _dot_nt(w_ref[half:2 * half, :], dxb).astype(BF16)

    return pl.pallas_call(
        body, name="out_proj_bwd", grid=(t // tm,),
        in_specs=[pl.BlockSpec((tm, d), lambda i: (i, 0)),
                  pl.BlockSpec((2 * half, d), lambda i: (0, 0))],
        out_specs=[pl.BlockSpec((tm, half), lambda i: (i, 0)),
                   pl.BlockSpec((tm, half), lambda i: (i, 0)),
                   pl.BlockSpec((half, tm), lambda i: (0, i))],
        out_shape=[jax.ShapeDtypeStruct((t, half), BF16), jax.ShapeDtypeStruct((t, half), BF16),
                   jax.ShapeDtypeStruct((half, t), BF16)],
        compiler_params=_params(("parallel",)),
    )(dx, w)


def _lane(shape):
    return lax.broadcasted_iota(jnp.int32, shape, 1)


PAIR = (0, HD, LANES)
Q_HEAD = (0, HD, HD + ROPE, LANES)
K_ROPE = (0, ROPE, LANES)


def _seg_mean(z, bounds):
    seg = lambda v: sum([(v >= b).astype(jnp.int32) for b in bounds[1:-1]], jnp.zeros_like(v))
    rows = seg(lax.broadcasted_iota(jnp.int32, (LANES, LANES), 0))
    cols = seg(lax.broadcasted_iota(jnp.int32, (LANES, LANES), 1))
    same = (rows == cols).astype(BF16)
    lane = _lane((1, LANES))
    inv = sum([jnp.where((lane >= a) & (lane < b), 1.0 / (b - a), 0.0) for a, b in zip(bounds[:-1], bounds[1:])])
    hi = z.astype(BF16)
    lo = (z - hi.astype(F32)).astype(BF16)
    return (_dot(hi, same) + _dot(lo, same)) * inv


def _seg_norm(x, bounds):
    r = lax.rsqrt(_seg_mean(x * x, bounds) + EPS)
    return r, x * r


def _seg_norm_bwd(r, xn, dyg, bounds):
    return r * (dyg - xn * _seg_mean(dyg * xn, bounds))


A_TM = 256


def prep1_fwd(proj, gq, gk, gcq, gckv):
    t = proj.shape[0]
    tm = A_TM

    def body(p_ref, gq_ref, gk_ref, gcq_ref, gckv_ref, qa_ref, ka_ref, va_ref, cq_ref, ckv_ref):
        for p in range(4):
            sl = slice(LANES * p, LANES * (p + 1))
            _, xn = _seg_norm(p_ref[:, sl], PAIR)
            qa_ref[:, sl] = (xn * gq_ref[:, sl] * 0.125).astype(BF16)
            _, xn = _seg_norm(p_ref[:, 512 + LANES * p:512 + LANES * (p + 1)], PAIR)
            ka_ref[:, sl] = (xn * gk_ref[:, sl]).astype(BF16)
        va_ref[...] = p_ref[:, 1024:1536].astype(BF16)
        _, xn = _rms(p_ref[:, 1536:1792])
        cq_ref[...] = (xn * gcq_ref[...]).astype(BF16)
        _, xn = _rms(p_ref[:, 1792:1920])
        ckv_ref[...] = (xn * gckv_ref[...]).astype(BF16)

    row = lambda w: pl.BlockSpec((tm, w), lambda i: (i, 0))
    vec = lambda w: pl.BlockSpec((1, w), lambda i: (0, 0))
    return pl.pallas_call(
        body, name="prep1_fwd", grid=(t // tm,),
        in_specs=[row(PROJ_W), vec(512), vec(512), vec(256), vec(128)],
        out_specs=[row(512), row(512), row(512), row(256), row(128)],
        out_shape=[jax.ShapeDtypeStruct((t, w), BF16) for w in (512, 512, 512, 256, 128)],
        compiler_params=_params(("parallel",)),
    )(proj, gq, gk, gcq, gckv)


def mix_bwd(proj, x, h, dres, w, g, dqa, dkp, dvp, dcq, dckv, dkr, gq, gk, gcq, gckv):
    t, d = x.shape
    tm = A_TM
    nb = t // tm

    def body(p_ref, x_ref, h_ref, dres_ref, w_ref, g_ref, dqa_ref, dk0_ref, dk1_ref, dk2_ref, dv0_ref, dv1_ref,
             dv2_ref, dcq_ref, dckv_ref, dkr_ref, gq_ref, gk_ref, gcq_ref, gckv_ref,
             dx_ref, dw_ref, dg_ref, dgq_ref, dgk_ref, dgcq_ref, dgckv_ref, dp_ref, acc_ref):
        i = pl.program_id(0)

        @pl.when(i == 0)
        def _():
            for ref in (acc_ref, dg_ref, dgq_ref, dgk_ref, dgcq_ref, dgckv_ref):
                ref[...] = jnp.zeros_like(ref)

        has1 = (i + 1 < nb).astype(F32)
        has2 = (i + 2 < nb).astype(F32)
        for p in range(4):
            sl = slice(LANES * p, LANES * (p + 1))
            r, xn = _seg_norm(p_ref[:, sl], PAIR)
            dy = dqa_ref[:, sl] * 0.125
            dp_ref[:, sl] = _seg_norm_bwd(r, xn, dy * gq_ref[:, sl], PAIR).astype(BF16)
            dgq_ref[:, sl] += jnp.sum(dy * xn, axis=0, keepdims=True)
            ks = slice(512 + LANES * p, 512 + LANES * (p + 1))
            r, xn = _seg_norm(p_ref[:, ks], PAIR)
            dy = dk0_ref[0, :, sl] + has1 * dk1_ref[0, :, sl] + has2 * dk2_ref[0, :, sl]
            dp_ref[:, ks] = _seg_norm_bwd(r, xn, dy * gk_ref[:, sl], PAIR).astype(BF16)
            dgk_ref[:, sl] += jnp.sum(dy * xn, axis=0, keepdims=True)
        dp_ref[:, 1024:1536] = (dv0_ref[0] + has1 * dv1_ref[0] + has2 * dv2_ref[0]).astype(BF16)
        for (a, b, dlat_ref, glat_ref, dglat_ref) in ((1536, 1792, dcq_ref, gcq_ref, dgcq_ref),
                                                      (1792, 1920, dckv_ref, gckv_ref, dgckv_ref)):
            r, xn = _rms(p_ref[:, a:b])
            dy = dlat_ref[...]
            dyg = dy * glat_ref[...]
            dp_ref[:, a:b] = (r * (dyg - xn * jnp.mean(dyg * xn, axis=-1, keepdims=True))).astype(BF16)
            dglat_ref[...] += jnp.sum(dy * xn, axis=0, keepdims=True)
        dp_ref[:, 1920:2048] = dkr_ref[...].astype(BF16)

        dproj = dp_ref[...]
        acc_ref[...] += _dot_tn(dproj, h_ref[...])
        dh = _dot(dproj, w_ref[...])
        r, xn = _rms(x_ref[...])
        dhg = dh * g_ref[...]
        dx_ref[...] = dres_ref[...] + r * (dhg - xn * jnp.mean(dhg * xn, axis=-1, keepdims=True))
        dg_ref[...] += jnp.sum(dh * xn, axis=0, keepdims=True)

        @pl.when(i == nb - 1)
        def _():
            dw_ref[...] = acc_ref[...].astype(BF16)

    row = lambda w_: pl.BlockSpec((tm, w_), lambda i: (i, 0))
    vec = lambda w_: pl.BlockSpec((1, w_), lambda i: (0, 0))
    part = lambda s: pl.BlockSpec((1, tm, 512), lambda i: (s, jnp.minimum(i + s, nb - 1), 0))
    whole = pl.BlockSpec((PROJ_W, d), lambda i: (0, 0))
    return pl.pallas_call(
        body, name="mix_bwd", grid=(nb,),
        in_specs=[row(PROJ_W), row(d), row(d), row(d), whole, vec(d), row(512), part(0), part(1), part(2),
                  part(0), part(1), part(2), row(256), row(128), row(128), vec(512), vec(512), vec(256), vec(128)],
        out_specs=[row(d), whole, vec(d), vec(512), vec(512), vec(256), vec(128)],
        out_shape=[jax.ShapeDtypeStruct((t, d), F32), jax.ShapeDtypeStruct((PROJ_W, d), BF16)]
        + [jax.ShapeDtypeStruct((1, w_), F32) for w_ in (d, 512, 512, 256, 128)],
        scratch_shapes=[pltpu.VMEM((tm, PROJ_W), BF16), pltpu.VMEM((PROJ_W, d), F32)],
        compiler_params=_params(("arbitrary",)),
    )(proj, x, h, dres, w, g, dqa, dkp, dkp, dkp, dvp, dvp, dvp, dcq, dckv, dkr, gq, gk, gcq, gckv)


def _roll(x, shift):
    return pltpu.roll(x, shift % LANES, 1)


def _rope(y, c, s1, s2):
    return y * c + _roll(y, -16) * s1 + _roll(y, 16) * s2


def _rope_bwd(d, c, s1, s2):
    return d * c + _roll(d * s1, 16) + _roll(d * s2, -16)


def _up_proj(cq_ref, ckv_ref, wuq_ref, wukv_ref):
    return _dot_nt(cq_ref[...], wuq_ref[...]), _dot(ckv_ref[...], wukv_ref[...])


def prep2_fwd(cqn, ckvn, proj, wuq, wukv, gq, gk, gkr, tabs):
    t = cqn.shape[0]
    tm = A_TM

    def body(cq_ref, ckv_ref, kr_ref, wuq_ref, wukv_ref, gq_ref, gk_ref, gkr_ref, tab_ref, qf_ref, kf_ref, vp_ref):
        q_all, kv_all = _up_proj(cq_ref, ckv_ref, wuq_ref, wukv_ref)
        _, xn = _seg_norm(kr_ref[...], K_ROPE)
        kpe = _roll(_rope(xn * gkr_ref[...], tab_ref[3], tab_ref[4], tab_ref[5]), 64)
        for h in range(HEADS):
            sl = slice(LANES * h, LANES * (h + 1))
            _, xn = _seg_norm(q_all[:, sl], Q_HEAD)
            qf_ref[:, sl] = (_rope(xn * gq_ref[...], tab_ref[0], tab_ref[1], tab_ref[2]) * B_SCALE2).astype(BF16)
            x = kv_all[:, sl]
            lo = _lane(x.shape) < HD
            _, xkn = _seg_norm(jnp.where(lo, x, 0.0), PAIR)
            kf_ref[:, sl] = (xkn * gk_ref[...] + kpe).astype(BF16)
            if h % 2 == 0:
                v_even = _roll(x, 64)
            else:
                vp_ref[:, LANES * (h // 2):LANES * (h // 2 + 1)] = jnp.where(lo, v_even, x).astype(BF16)

    row = lambda w: pl.BlockSpec((tm, w), lambda i: (i, 0))
    vec = lambda w: pl.BlockSpec((1, w), lambda i: (0, 0))
    full = lambda a: pl.BlockSpec(a.shape, lambda i: (0, 0))
    return pl.pallas_call(
        body, name="prep2_fwd", grid=(t // tm,),
        in_specs=[row(256), row(128), pl.BlockSpec((tm, LANES), lambda i: (i, 15)), full(wuq), full(wukv),
                  vec(128), vec(128), vec(128), pl.BlockSpec((6, tm, LANES), lambda i: (0, i, 0))],
        out_specs=[row(1024), row(1024), row(512)],
        out_shape=[jax.ShapeDtypeStruct((t, 1024), BF16), jax.ShapeDtypeStruct((t, 1024), BF16),
                   jax.ShapeDtypeStruct((t, 512), BF16)],
        compiler_params=_params(("parallel",)),
    )(cqn, ckvn, proj, wuq, wukv, gq, gk, gkr, tabs)


def prep2_bwd(cqn, ckvn, proj, wuq, wukv, dqf, dkf, dvp, gq, gk, gkr, tabs):
    t = cqn.shape[0]
    tm = A_TM

    def body(cq_ref, ckv_ref, kr_ref, wuq_ref, wukv_ref, dqf_ref, dkf_ref, dvp_ref, gq_ref, gk_ref, gkr_ref, tab_ref,
             dcq_ref, dckv_ref, dkr_ref, dwuq_ref, dwukv_ref, dgq_ref, dgk_ref, dgkr_ref, dq_ref, dkv_ref):
        i = pl.program_id(0)

        @pl.when(i == 0)
        def _():
            for ref in (dwuq_ref, dwukv_ref, dgq_ref, dgk_ref, dgkr_ref):
                ref[...] = jnp.zeros_like(ref)

        q_all, kv_all = _up_proj(cq_ref, ckv_ref, wuq_ref, wukv_ref)
        dgq = jnp.zeros((1, LANES), F32)
        dgk = jnp.zeros((1, LANES), F32)
        dkpe = jnp.zeros((tm, LANES), F32)
        for h in range(HEADS):
            sl = slice(LANES * h, LANES * (h + 1))
            lane = _lane((tm, LANES))
            mn, mr = lane < HD, (lane >= HD) & (lane < HD + ROPE)
            r, xn = _seg_norm(q_all[:, sl], Q_HEAD)
            dy = _rope_bwd(dqf_ref[sl, :].T, tab_ref[0], tab_ref[1], tab_ref[2])
            dyg = dy * gq_ref[...]
            dq_ref[:, sl] = _seg_norm_bwd(r, xn, dyg, Q_HEAD).astype(BF16)
            dgq = dgq + jnp.sum(dy * xn, axis=0, keepdims=True)

            x = kv_all[:, sl]
            dk = dkf_ref[:, sl]
            rk, xkn = _seg_norm(jnp.where(mn, x, 0.0), PAIR)
            dyk = jnp.where(mn, dk, 0.0)
            dxk = _seg_norm_bwd(rk, xkn, dyk * gk_ref[...], PAIR)
            dgk = dgk + jnp.sum(dyk * xkn, axis=0, keepdims=True)
            dkpe = dkpe + jnp.where(mr, dk, 0.0)
            dvpair = dvp_ref[:, LANES * (h // 2):LANES * (h // 2 + 1)]
            dv = _roll(dvpair, 64) if h % 2 == 0 else dvpair
            dkv_ref[:, sl] = jnp.where(mn, dxk, dv).astype(BF16)

        r, xn = _seg_norm(kr_ref[...], K_ROPE)
        dy = _rope_bwd(_roll(dkpe, 64), tab_ref[3], tab_ref[4], tab_ref[5])
        dkr_ref[...] = _seg_norm_bwd(r, xn, dy * gkr_ref[...], K_ROPE)
        dgq_ref[...] += dgq
        dgk_ref[...] += dgk
        dgkr_ref[...] += jnp.sum(dy * xn, axis=0, keepdims=True)
        dqb, dkvb = dq_ref[...], dkv_ref[...]
        dcq_ref[...] = _dot(dqb, wuq_ref[...])
        dckv_ref[...] = _dot_nt(dkvb, wukv_ref[...])
        dwuq_ref[...] += _dot_tn(dqb, cq_ref[...])
        dwukv_ref[...] += _dot_tn(ckv_ref[...], dkvb)

    row = lambda w: pl.BlockSpec((tm, w), lambda i: (i, 0))
    vec = lambda w: pl.BlockSpec((1, w), lambda i: (0, 0))
    full = lambda a: pl.BlockSpec(a.shape, lambda i: (0, 0))
    return pl.pallas_call(
        body, name="prep2_bwd", grid=(t // tm,),
        in_specs=[row(256), row(128), pl.BlockSpec((tm, LANES), lambda i: (i, 15)), full(wuq), full(wukv),
                  pl.BlockSpec((1024, tm), lambda i: (0, i)), row(1024), row(512),
                  vec(128), vec(128), vec(128), pl.BlockSpec((6, tm, LANES), lambda i: (0, i, 0))],
        out_specs=[row(256), row(128), row(128), full(wuq), full(wukv), vec(128), vec(128), vec(128)],
        out_shape=[jax.ShapeDtypeStruct((t, 256), F32), jax.ShapeDtypeStruct((t, 128), F32),
                   jax.ShapeDtypeStruct((t, LANES), F32), jax.ShapeDtypeStruct(wuq.shape, F32),
                   jax.ShapeDtypeStruct(wukv.shape, F32)] + [jax.ShapeDtypeStruct((1, LANES), F32)] * 3,
        scratch_shapes=[pltpu.VMEM((tm, 1024), BF16), pltpu.VMEM((tm, 1024), BF16)],
        compiler_params=_params(("arbitrary",)),
    )(cqn, ckvn, proj, wuq, wukv, dqf, dkf, dvp, gq, gk, gkr, tabs)


A_TQ = 256
A_WIN = 3 * A_TQ


def _a_specs(t):
    nb = t // A_TQ
    blk = lambda s: pl.BlockSpec((A_TQ, 512), lambda i: (jnp.maximum(i - s, 0), 0))
    return nb, blk


def _a_exp(q_ref, kc, b_ref, head, sl, lo):
    hm = lo if head % 2 == 0 else ~lo
    qm = jnp.where(hm, q_ref[:, sl], jnp.zeros((), BF16))
    s = _dot_nt(qm, kc) + b_ref[0, head]
    e = jnp.exp(s - jnp.max(s, axis=-1, keepdims=True))
    return hm, qm, e, 1.0 / jnp.sum(e, axis=-1, keepdims=True)


def _a_bias_spec():
    return pl.BlockSpec((1, HEADS, A_TQ, A_WIN), lambda i: (jnp.minimum(i, 2), 0, 0, 0))


def attn_a_fwd(qa, ka, va, bias):
    t = qa.shape[0]
    nb, blk = _a_specs(t)

    def body(q_ref, k2_ref, k1_ref, k0_ref, v2_ref, v1_ref, v0_ref, b_ref, o_ref):
        lo = _lane((A_TQ, LANES)) < HD
        for p in range(4):
            sl = slice(LANES * p, LANES * (p + 1))
            kc = jnp.concatenate([k2_ref[:, sl], k1_ref[:, sl], k0_ref[:, sl]], axis=0)
            vc = jnp.concatenate([v2_ref[:, sl], v1_ref[:, sl], v0_ref[:, sl]], axis=0)
            outs = []
            for h2 in range(2):
                _, _, e, inv = _a_exp(q_ref, kc, b_ref, 2 * p + h2, sl, lo)
                outs.append(_dot(e.astype(BF16), vc) * inv)
            o_ref[:, sl] = jnp.where(lo, outs[0], outs[1]).astype(BF16)

    return pl.pallas_call(
        body, name="attn_a_fwd", grid=(nb,),
        in_specs=[blk(0), blk(2), blk(1), blk(0), blk(2), blk(1), blk(0), _a_bias_spec()],
        out_specs=pl.BlockSpec((A_TQ, 512), lambda i: (i, 0)),
        out_shape=jax.ShapeDtypeStruct((t, 512), BF16),
        compiler_params=_params(("parallel",)),
    )(qa, ka, ka, ka, va, va, va, bias)


def attn_a_bwd(qa, ka, va, bias, do):
    t = qa.shape[0]
    nb, blk = _a_specs(t)

    def body(q_ref, k2_ref, k1_ref, k0_ref, v2_ref, v1_ref, v0_ref, b_ref, do_ref, dq_ref, dk_ref, dv_ref, db_ref):
        qb = pl.program_id(0)

        @pl.when(qb == 0)
        def _():
            db_ref[...] = jnp.zeros_like(db_ref)

        lo = _lane((A_TQ, LANES)) < HD
        for p in range(4):
            sl = slice(LANES * p, LANES * (p + 1))
            kc = jnp.concatenate([k2_ref[:, sl], k1_ref[:, sl], k0_ref[:, sl]], axis=0)
            vc = jnp.concatenate([v2_ref[:, sl], v1_ref[:, sl], v0_ref[:, sl]], axis=0)
            dqs = []
            dkc = jnp.zeros((A_WIN, LANES), F32)
            dvc = jnp.zeros((A_WIN, LANES), F32)
            for h2 in range(2):
                head = 2 * p + h2
                hm, qm, e, inv = _a_exp(q_ref, kc, b_ref, head, sl, lo)
                pr = e * inv
                dom = jnp.where(hm, do_ref[:, sl], jnp.zeros((), BF16))
                dp = _dot_nt(dom, vc)
                ds = pr * (dp - jnp.sum(pr * dp, axis=-1, keepdims=True))
                db_ref[head] += ds
                dsb = ds.astype(BF16)
                dqs.append(_dot(dsb, kc))
                dkc = dkc + _dot_tn(dsb, qm)
                dvc = dvc + _dot_tn(pr.astype(BF16), dom)
            dq_ref[:, sl] = jnp.where(lo, dqs[0], dqs[1])
            for s in range(3):
                rows = slice(A_TQ * (2 - s), A_TQ * (3 - s))
                dk_ref[s, :, sl] = dkc[rows]
                dv_ref[s, :, sl] = dvc[rows]

    share = pl.BlockSpec((3, A_TQ, 512), lambda i: (0, i, 0))
    return pl.pallas_call(
        body, name="attn_a_bwd", grid=(nb,),
        in_specs=[blk(0), blk(2), blk(1), blk(0), blk(2), blk(1), blk(0), _a_bias_spec(), blk(0)],
        out_specs=[pl.BlockSpec((A_TQ, 512), lambda i: (i, 0)), share, share,
                   pl.BlockSpec((HEADS, A_TQ, A_WIN), lambda i: (0, 0, 0))],
        out_shape=[jax.ShapeDtypeStruct((t, 512), F32), jax.ShapeDtypeStruct((3, t, 512), F32),
                   jax.ShapeDtypeStruct((3, t, 512), F32), jax.ShapeDtypeStruct((HEADS, A_TQ, A_WIN), F32)],
        compiler_params=_params(("arbitrary",)),
    )(qa, ka, ka, ka, va, va, va, bias, do)


B_T = 1024


B_SCALE2 = B_SCALE * 1.4426950408889634
_B_ALL = slice(0, B_T)
_B_LO, _B_HI = slice(0, B_T // 2), slice(B_T // 2, B_T)
_B_DIAG = ((_B_LO, _B_LO), (_B_LO, _B_HI), (_B_HI, _B_HI))


def _tri_tables(n, by_query):
    pairs = [(i, j) for i in range(n) for j in range(i + 1)] if by_query else [(i, j) for j in range(n) for i in range(j, n)]
    return (np.asarray([p[0] for p in pairs], np.int32), np.asarray([p[1] for p in pairs], np.int32))


def _b_mask_t(s):
    kc = lax.broadcasted_iota(jnp.int32, s.shape, 0) // CHUNK
    qc = lax.broadcasted_iota(jnp.int32, s.shape, 1) // CHUNK
    return jnp.where(kc <= qc, s, NEG)


def attn_b_fwd(qf, kf, vp):
    t = qf.shape[0]
    n = t // B_T
    qtab, ktab = _tri_tables(n, by_query=True)

    def body(qt_ref, kt_ref, q_ref, k_ref, v_ref, o_ref, lse_ref, m_s, l_s, acc_s):
        qb, kb = qt_ref[pl.program_id(1)], kt_ref[pl.program_id(1)]

        @pl.when(kb == 0)
        def _():
            m_s[...] = jnp.full_like(m_s, NEG)
            l_s[...] = jnp.zeros_like(l_s)
            acc_s[...] = jnp.zeros_like(acc_s)

        def block(kr, qr, masked):
            v = v_ref[kr, :]
            for h2 in range(2):
                sl = slice(LANES * h2, LANES * (h2 + 1))
                s = _dot_nt(k_ref[kr, sl], q_ref[qr, sl])
                if masked:
                    s = _b_mask_t(s)
                m_prev = m_s[h2, :, qr]
                m_new = jnp.maximum(m_prev, jnp.max(s, axis=0, keepdims=True))
                alpha = jnp.exp2(m_prev - m_new)
                pr = jnp.exp2(s - m_new)
                l_s[h2, :, qr] = alpha * l_s[h2, :, qr] + jnp.sum(pr, axis=0, keepdims=True)
                acc_s[h2, :, qr] = alpha * acc_s[h2, :, qr] + _dot_tn(v, pr.astype(BF16))
                m_s[h2, :, qr] = m_new

        @pl.when(kb < qb)
        def _():
            block(_B_ALL, _B_ALL, False)

        @pl.when(kb == qb)
        def _():
            for kr, qr in _B_DIAG:
                block(kr, qr, kr == qr)
            for h2 in range(2):
                l = l_s[h2]
                rows = slice(HD * h2, HD * (h2 + 1))
                o_ref[rows, :] = (acc_s[h2, rows, :] * (1.0 / l)).astype(BF16)
                lse_ref[0, h2:h2 + 1, :] = m_s[h2] + jnp.log2(l)

    grid_spec = pltpu.PrefetchScalarGridSpec(
        num_scalar_prefetch=2, grid=(4, len(qtab)),
        in_specs=[pl.BlockSpec((B_T, 256), lambda p, s, qt, kt: (qt[s], p)),
                  pl.BlockSpec((B_T, 256), lambda p, s, qt, kt: (kt[s], p)),
                  pl.BlockSpec((B_T, LANES), lambda p, s, qt, kt: (kt[s], p))],
        out_specs=[pl.BlockSpec((LANES, B_T), lambda p, s, qt, kt: (p, qt[s])),
                   pl.BlockSpec((1, 2, B_T), lambda p, s, qt, kt: (p, 0, qt[s]))],
        scratch_shapes=[pltpu.VMEM((2, 1, B_T), F32), pltpu.VMEM((2, 1, B_T), F32), pltpu.VMEM((2, LANES, B_T), F32)])
    return pl.pallas_call(
        body, name="attn_b_fwd", grid_spec=grid_spec,
        out_shape=[jax.ShapeDtypeStruct((512, t), BF16), jax.ShapeDtypeStruct((4, 2, t), F32)],
        compiler_params=_params(("parallel", "arbitrary")),
    )(jnp.asarray(qtab), jnp.asarray(ktab), qf, kf, vp)


def attn_b_bwd(qf, kf, vp, do, do_t, o_t, lse, scatter=()):
    t = qf.shape[0]
    n = t // B_T
    qtab, ktab = _tri_tables(n, by_query=False)
    plan = ScatterPlan(scatter)
    m = plan.n
    last = len(qtab) - 1

    def body(*refs):
        qt_ref, kt_ref, q_ref, k_ref, v_ref, do_ref, dot_ref, ot_ref, lse_ref = refs[:9]
        ins, (dq_ref, dk_ref, dv_ref), outs = refs[9:9 + m], refs[9 + m:12 + m], refs[12 + m:12 + 2 * m]
        sems = refs[12 + 2 * m:]
        qb, kb = qt_ref[pl.program_id(1)], kt_ref[pl.program_id(1)]
        if m:
            pl.when((pl.program_id(0) == 0) & (pl.program_id(1) == 0))(lambda: plan.start(ins, outs, sems))

        @pl.when(pl.program_id(1) == 0)
        def _():
            dq_ref[...] = jnp.zeros_like(dq_ref)

        @pl.when(qb == kb)
        def _():
            dk_ref[...] = jnp.zeros_like(dk_ref)
            dv_ref[...] = jnp.zeros_like(dv_ref)

        def block(kr, qr, masked):
            nq = qr.stop - qr.start
            cols = pl.ds(pl.multiple_of(qb * B_T + qr.start, LANES), nq)
            v = v_ref[kr, :]
            dov = do_ref[qr, :]
            prod = dot_ref[:, qr].astype(F32) * ot_ref[:, qr].astype(F32)
            lo = _lane((nq, LANES)) < HD
            for h2 in range(2):
                sl = slice(LANES * h2, LANES * (h2 + 1))
                hm = lo if h2 == 0 else ~lo
                q = q_ref[qr, sl]
                k = k_ref[kr, sl]
                dom = jnp.where(hm, dov, jnp.zeros((), BF16))
                delta = jnp.sum(prod[HD * h2:HD * (h2 + 1), :], axis=0, keepdims=True)
                s = _dot_nt(k, q)
                if masked:
                    s = _b_mask_t(s)
                pr = jnp.exp2(s - lse_ref[0, h2:h2 + 1, qr])
                dp = _dot_nt(v, dom)
                ds = (pr * (dp - delta)).astype(BF16)
                dk_ref[kr, sl] += _dot(ds, q) * (B_SCALE / B_SCALE2)
                dv_ref[kr, :] += _dot(pr.astype(BF16), dom)
                dq_ref[sl, cols] += _dot_tn(k, ds) * B_SCALE

        @pl.when(qb > kb)
        def _():
            block(_B_ALL, _B_ALL, False)

        @pl.when(qb == kb)
        def _():
            for kr, qr in _B_DIAG:
                block(kr, qr, kr == qr)

        if m:
            pl.when((pl.program_id(0) == 3) & (pl.program_id(1) == last))(lambda: plan.finish(ins, outs, sems))

    qrow = lambda w: pl.BlockSpec((B_T, w), lambda p, s, qt, kt: (qt[s], p))
    qcol = pl.BlockSpec((LANES, B_T), lambda p, s, qt, kt: (p, qt[s]))
    krow = lambda w: pl.BlockSpec((B_T, w), lambda p, s, qt, kt: (kt[s], p))
    grid_spec = pltpu.PrefetchScalarGridSpec(
        num_scalar_prefetch=2, grid=(4, len(qtab)),
        in_specs=[qrow(256), krow(256), krow(LANES), qrow(LANES), qcol, qcol,
                  pl.BlockSpec((1, 2, B_T), lambda p, s, qt, kt: (p, 0, qt[s]))] + [ANY] * m,
        out_specs=[pl.BlockSpec((256, t), lambda p, s, qt, kt: (p, 0)), krow(256), krow(LANES)] + [ANY] * m,
        scratch_shapes=plan.scratch if m else [])
    return pl.pallas_call(
        body, name="attn_b_bwd", grid_spec=grid_spec,
        out_shape=[jax.ShapeDtypeStruct((1024, t), F32), jax.ShapeDtypeStruct((t, 1024), F32),
                   jax.ShapeDtypeStruct((t, 512), F32)] + plan.out_shape,
        compiler_params=_params(("arbitrary", "arbitrary")),
    )(jnp.asarray(qtab), jnp.asarray(ktab), qf, kf, vp, do, do_t, o_t, lse, *scatter)


_U_LEN = A_TQ + A_WIN - 1


def _band_mask():
    a = np.arange(A_TQ)[:, None] // CHUNK
    b = np.arange(A_WIN)[None, :] // CHUNK
    return (b >= a) & (b <= a + A_LEFT)


def bias_block(table):
    h = table.shape[0]
    n_lo = A_WIN - 1 - 2 * A_TQ - A_MAX_REL
    ext = jnp.concatenate([jnp.repeat(table[:, :1], n_lo, axis=1), table,
                           jnp.repeat(table[:, -1:], _U_LEN - n_lo - table.shape[1], axis=1)], axis=1)
    row = jnp.pad(ext[:, ::-1], ((0, 0), (0, 1)))
    flat = jnp.tile(row, (1, A_TQ))[:, :A_TQ * _U_LEN]
    skew = flat.reshape(h, A_TQ, _U_LEN)
    toep = skew[:, :, A_TQ - 1:A_TQ - 1 + A_WIN]
    band = _band_mask()
    first = [band & (np.arange(A_WIN)[None, :] >= 2 * A_TQ - A_TQ * v) for v in range(3)]
    return jnp.where(jnp.asarray(np.stack(first))[:, None], toep[None], NEG)


def bias_block_grad(db):
    h = db.shape[0]
    n_lo = A_WIN - 1 - 2 * A_TQ - A_MAX_REL
    skew = jnp.pad(db, ((0, 0), (0, 0), (A_TQ - 1, 0)))
    flat = jnp.pad(skew.reshape(h, A_TQ * _U_LEN), ((0, 0), (0, A_TQ)))
    ext = jnp.sum(flat.reshape(h, A_TQ, _U_LEN + 1), axis=1)[:, :_U_LEN][:, ::-1]
    n_tab = 2 * A_MAX_REL + 1
    first = jnp.sum(ext[:, :n_lo + 1], axis=1, keepdims=True)
    last = jnp.sum(ext[:, n_lo + n_tab - 1:], axis=1, keepdims=True)
    return jnp.concatenate([first, ext[:, n_lo + 1:n_lo + n_tab - 1], last], axis=1)


def rope_tabs(t):
    inv = 1.0 / (10000.0 ** (jnp.arange(0, ROPE, 2, dtype=F32) / ROPE))
    ang = jnp.arange(t, dtype=F32)[:, None] * inv[None, :]
    cos, sin = jnp.cos(ang), jnp.sin(ang)
    z = lambda w: jnp.zeros((t, w), F32)
    ck = jnp.concatenate([cos, cos, z(96)], axis=1)
    s1k = jnp.concatenate([-sin, z(112)], axis=1)
    s2k = jnp.concatenate([z(16), sin, z(96)], axis=1)
    cq = jnp.concatenate([jnp.ones((t, HD), F32), cos, cos, z(32)], axis=1)
    s1q = jnp.concatenate([z(HD), -sin, z(48)], axis=1)
    s2q = jnp.concatenate([z(HD + 16), sin, z(32)], axis=1)
    return jnp.stack([cq, s1q, s2q, ck, s1k, s2k])


def _pad_lanes(v, width):
    return jnp.pad(v, ((0, 0), (0, width - v.shape[1])))


LATE = ("w_in", "b_w_uq", "b_w_ukv", "w_out", "ffn2_w_gate", "ffn2_w_up", "ffn2_w_down")
FFN2 = ("ffn2_w_gate", "ffn2_w_up", "ffn2_w_down")


def kernel_layout(gathered):
    w = {n: v for n, v in gathered.items() if n.startswith("ffn")}
    if "w_in" in gathered:
        w["w_in"] = jnp.pad(gathered["w_in"].reshape(IN_COLS, D_MODEL), ((0, PROJ_W - IN_COLS), (0, 0)))
        uq = gathered["b_w_uq"].reshape(HEADS, HD + ROPE, 256)
        w["b_w_uq"] = jnp.pad(uq, ((0, 0), (0, LANES - HD - ROPE), (0, 0))).reshape(HEADS * LANES, 256)
        w["b_w_ukv"] = _shards_to_cols(gathered["b_w_ukv"])
        w["w_out"] = gathered["w_out"].reshape(N_SHARD * gathered["w_out"].shape[1], D_MODEL)
    return w


def local_step(x, target, w, late=None):
    t = x.shape[0]
    gq = jnp.tile(w["a_q_norm"], (1, HEADS))
    gk = jnp.tile(w["a_k_norm"], (1, HEADS))
    gq128 = _pad_lanes(jnp.concatenate([w["b_q_nope_norm"], w["b_q_rope_norm"]], axis=1), LANES)
    gk128 = _pad_lanes(w["b_k_nope_norm"], LANES)
    gkr128 = _pad_lanes(w["b_k_rope_norm"], LANES)
    tabs = rope_tabs(t)
    bias = bias_block(w["a_rel_bias"])

    if late is None:
        x1, gate1, up1 = ffn_fwd(x, w["ffn1_norm"], w["ffn1_w_gate"], w["ffn1_w_up"], w["ffn1_w_down"], "ffn_fwd")
    else:
        own, shard = late
        x1, gate1, up1, *got = ffn_fwd(x, w["ffn1_norm"], w["ffn1_w_gate"], w["ffn1_w_up"], w["ffn1_w_down"],
                                       "ffn_fwd_gather", gather=own)
        w = dict(w, **kernel_layout({n: lax.dynamic_update_index_in_dim(g_, o_, shard, 0)
                                     for n, g_, o_ in zip(LATE, got, own)}))
    h, proj = mix_proj(x1, w["mix_norm"], w["w_in"])
    qa, ka, va, cqn, ckvn = prep1_fwd(proj, gq, gk, w["b_q_lat_norm"], w["b_kv_lat_norm"])
    qf, kf, vp = prep2_fwd(cqn, ckvn, proj, w["b_w_uq"], w["b_w_ukv"], gq128, gk128, gkr128, tabs)
    oa = attn_a_fwd(qa, ka, va, bias)
    ob_t, lse = attn_b_fwd(qf, kf, vp)
    x2 = out_proj(x1, oa, ob_t, w["w_out"])
    x3, gate2, up2 = ffn_fwd(x2, w["ffn2_norm"], w["ffn2_w_gate"], w["ffn2_w_up"], w["ffn2_w_down"], "ffn_fwd")

    g = {}
    dx3, g["final_norm"], loss = final_loss(x3, w["final_norm"], target)
    g["ffn2_w_gate"], g["ffn2_w_up"], g["ffn2_w_down"], dhp = ffn_bwd(
        x2, dx3, w["ffn2_norm"], w["ffn2_w_gate"], w["ffn2_w_up"], w["ffn2_w_down"], gate2, up2, "ffn_bwd")
    dx2, g["ffn2_norm"] = norm_bwd(x2, w["ffn2_norm"], dhp, dx3, "ffn_norm_bwd")
    d_oa, d_ob, d_ob_t = out_proj_bwd(dx2, w["w_out"])
    g["w_out"] = jnp.concatenate([matmul(oa, dx2, "tn", "w_out_a_bwd", 512, 1024, 512, BF16),
                                  matmul(ob_t, dx2, "nn", "w_out_b_bwd", 512, 1024, 512, BF16)], axis=0)
    early = [g[n] for n in FFN2] + [g["w_out"].reshape(N_SHARD, -1, D_MODEL)]
    dqf, dkf, dvp, *landed_early = attn_b_bwd(qf, kf, vp, d_ob, d_ob_t, ob_t, lse,
                                              scatter=() if late is None else early)
    dqa, dkp, dvpa, dbias = attn_a_bwd(qa, ka, va, bias, d_oa)
    dcq, dckv, dkr, dwuq, dwukv, dgq128, dgk128, dgkr128 = prep2_bwd(
        cqn, ckvn, proj, w["b_w_uq"], w["b_w_ukv"], dqf, dkf, dvp, gq128, gk128, gkr128, tabs)
    g["b_w_uq"], g["b_w_ukv"] = dwuq.astype(BF16), dwukv.astype(BF16)
    dx1, g["w_in"], g["mix_norm"], dgq, dgk, g["b_q_lat_norm"], g["b_kv_lat_norm"] = mix_bwd(
        proj, x1, h, dx2, w["w_in"], w["mix_norm"], dqa, dkp, dvpa, dcq, dckv, dkr, gq, gk,
        w["b_q_lat_norm"], w["b_kv_lat_norm"])
    mid = [g["w_in"][:IN_COLS].reshape(N_SHARD, IN_COLS // N_SHARD, D_MODEL),
           g["b_w_uq"].reshape(HEADS, LANES, 256)[:, :HD + ROPE].reshape(N_SHARD, -1, 256),
           _cols_to_shards(g["b_w_ukv"])]
    g["ffn1_w_gate"], g["ffn1_w_up"], g["ffn1_w_down"], dhp, *landed_late = ffn_bwd(
        x, dx1, w["ffn1_norm"], w["ffn1_w_gate"], w["ffn1_w_up"], w["ffn1_w_down"], gate1, up1,
        "ffn_bwd" if late is None else "ffn_bwd_scatter", scatter=() if late is None else mid,
        spread=None if late is None else late[1])
    grad_x, g["ffn1_norm"] = norm_bwd(x, w["ffn1_norm"], dhp, dx1, "ffn_norm_bwd")
    landed = dict(zip(FFN2 + ("w_out", "w_in", "b_w_uq", "b_w_ukv", "ffn1_w_gate", "ffn1_w_up", "ffn1_w_down"),
                      landed_early + landed_late))

    g["a_q_norm"] = jnp.sum(dgq.reshape(HEADS, HD), axis=0, keepdims=True)
    g["a_k_norm"] = jnp.sum(dgk.reshape(HEADS, HD), axis=0, keepdims=True)
    g["a_rel_bias"] = bias_block_grad(dbias)
    g["b_q_nope_norm"] = dgq128[:, :HD]
    g["b_q_rope_norm"] = dgq128[:, HD:HD + ROPE]
    g["b_k_nope_norm"] = dgk128[:, :HD]
    g["b_k_rope_norm"] = dgkr128[:, :ROPE]
    return loss, grad_x, g, landed


ANY = pl.BlockSpec(memory_space=pl.ANY)
N_DEV = 8


def _place():
    return lax.axis_index("x"), lax.axis_index("y"), lax.axis_index("c")


def _flip(v, bit):
    return 1 - v if bit else v


BF16_ROWS = 16


def _split_axis(shape):
    return 0 if (shape[0] // 2) % BF16_ROWS == 0 else 1


def _half_shape(shape):
    axis = _split_axis(shape)
    return tuple(s // 2 if a == axis else s for a, s in enumerate(shape))


def _half(shape, core):
    axis = _split_axis(shape)
    size = shape[axis] // 2
    return tuple(pl.ds(core * size, size) if a == axis else slice(None) for a in range(2))


class GatherPlan:
    def __init__(self, ws):
        self.shapes = [w.shape for w in ws]
        self.n = len(ws)
        self.out_shape = [jax.ShapeDtypeStruct((N_SHARD,) + w.shape, w.dtype) for w in ws]
        self.scratch = [pltpu.SemaphoreType.DMA((6 * self.n,)), pltpu.SemaphoreType.DMA((6 * self.n,))]

    def _copies(self, ins, outs, sems):
        x, y, c = _place()
        s_me = 2 * x + y
        sibling = (x, y, 1 - c)
        send_sems, recv_sems = sems

        def remote(k, src, dst, to):
            return pltpu.make_async_remote_copy(src_ref=src, dst_ref=dst, send_sem=send_sems.at[k],
                                                recv_sem=recv_sems.at[k], device_id=to, device_id_type=MESH)

        ici, fwd = [], []
        for a in range(self.n):
            mine, theirs = _half(self.shapes[a], c), _half(self.shapes[a], 1 - c)
            for j, (cx, cy) in enumerate([(1 - x, y), (x, 1 - y), (1 - x, 1 - y)]):
                got = outs[a].at[(2 * cx + cy,) + mine]
                ici.append((remote(6 * a + j, ins[a].at[mine], outs[a].at[(s_me,) + mine], (cx, cy, c)),
                            remote(6 * a + j, got, got, (cx, cy, c))))
                passed = outs[a].at[(2 * cx + cy,) + theirs]
                fwd.append((remote(6 * a + 3 + j, got, got, sibling), remote(6 * a + 3 + j, passed, passed, sibling)))
        return ici, fwd

    def start(self, ins, outs, sems):
        for send, _ in self._copies(ins, outs, sems)[0]:
            send.start()

    def forward(self, ins, outs, sems):
        ici, fwd = self._copies(ins, outs, sems)
        for (_, arrival), (send, _) in zip(ici, fwd):
            arrival.wait_recv()
            send.start()

    def finish(self, ins, outs, sems):
        ici, fwd = self._copies(ins, outs, sems)
        for _, arrival in fwd:
            arrival.wait_recv()
        for send, _ in ici + fwd:
            send.wait_send()


def allgather_shards(ws):
    plan = GatherPlan(ws)
    n = plan.n

    def body(*refs):
        ins, outs, sems = refs[:n], refs[n:2 * n], refs[2 * n:]
        plan.start(ins, outs, sems)
        plan.forward(ins, outs, sems)
        plan.finish(ins, outs, sems)

    return pl.pallas_call(
        body, name="allgather_shards", in_specs=[ANY] * n, out_specs=[ANY] * n,
        out_shape=plan.out_shape, scratch_shapes=plan.scratch,
    )(*ws)


class ScatterPlan:
    def __init__(self, gs):
        self.shapes = [g.shape[1:] for g in gs]
        self.n = len(gs)
        self.out_shape = [jax.ShapeDtypeStruct((N_DEV,) + _half_shape(g.shape[1:]), g.dtype) for g in gs]
        self.scratch = [pltpu.SemaphoreType.DMA((7 * self.n,)), pltpu.SemaphoreType.DMA((7 * self.n,)),
                        pltpu.SemaphoreType.DMA((self.n,))]

    def _copies(self, ins, outs, sems):
        x, y, c = _place()
        me = 4 * x + 2 * y + c
        send_sems, recv_sems, local_sems = sems
        local, sends, arrivals = [], [], []
        for a in range(self.n):
            piece = lambda px, py, pc, a=a: ins[a].at[(2 * px + py,) + _half(self.shapes[a], pc)]
            local.append(pltpu.make_async_copy(piece(x, y, c), outs[a].at[me], local_sems.at[a]))
            for k in range(1, N_DEV):
                px, py, pc = _flip(x, k & 4), _flip(y, k & 2), _flip(c, k & 1)
                sem = dict(send_sem=send_sems.at[7 * a + k - 1], recv_sem=recv_sems.at[7 * a + k - 1],
                           device_id=(px, py, pc), device_id_type=MESH)
                sends.append(pltpu.make_async_remote_copy(
                    src_ref=piece(px, py, pc), dst_ref=outs[a].at[me], **sem))
                slot = outs[a].at[4 * px + 2 * py + pc]
                arrivals.append(pltpu.make_async_remote_copy(src_ref=slot, dst_ref=slot, **sem))
        return local, sends, arrivals

    def start(self, ins, outs, sems):
        local, sends, _ = self._copies(ins, outs, sems)
        for cp in local + sends:
            cp.start()

    def finish(self, ins, outs, sems):
        local, sends, arrivals = self._copies(ins, outs, sems)
        for cp in arrivals:
            cp.wait_recv()
        for cp in sends:
            cp.wait_send()
        for cp in local:
            cp.wait()


def scatter_partials(gs):
    plan = ScatterPlan(gs)
    n = plan.n

    def body(*refs):
        ins, outs, sems = refs[:n], refs[n:2 * n], refs[2 * n:]
        plan.start(ins, outs, sems)
        plan.finish(ins, outs, sems)

    return pl.pallas_call(
        body, name="scatter_partials", in_specs=[ANY] * n, out_specs=[ANY] * n,
        out_shape=plan.out_shape, scratch_shapes=plan.scratch,
    )(*gs)


def sum_slots(land, name):
    _, rows, cols = land.shape
    tr = rows // 2 if rows > 128 and (rows // 2) % BF16_ROWS == 0 else rows

    def body(l_ref, o_ref):
        acc = l_ref[0].astype(F32)
        for s in range(1, N_DEV):
            acc = acc + l_ref[s].astype(F32)
        o_ref[...] = acc

    return pl.pallas_call(
        body, name=name, grid=(rows // tr,),
        in_specs=[pl.BlockSpec((N_DEV, tr, cols), lambda i: (0, i, 0))],
        out_specs=pl.BlockSpec((tr, cols), lambda i: (i, 0)),
        out_shape=jax.ShapeDtypeStruct((rows, cols), F32),
        compiler_params=_params(("parallel",)),
    )(land)


def join_halves(hs, shapes):
    n = len(hs)

    def body(*refs):
        ins, outs = refs[:n], refs[n:2 * n]
        send_sems, recv_sems = refs[2 * n:]
        x, y, c = _place()
        sends = []
        for a in range(n):
            mine = outs[a].at[_half(shapes[a], c)]
            sends.append(pltpu.make_async_remote_copy(
                src_ref=ins[a], dst_ref=mine, send_sem=send_sems.at[a], recv_sem=recv_sems.at[a],
                device_id=(x, y, 1 - c), device_id_type=MESH))
            sends[-1].start()
        for a in range(n):
            theirs = outs[a].at[_half(shapes[a], 1 - c)]
            pltpu.make_async_remote_copy(
                src_ref=theirs, dst_ref=theirs, send_sem=send_sems.at[a], recv_sem=recv_sems.at[a],
                device_id=(x, y, 1 - c), device_id_type=MESH).wait_recv()
        for cp in sends:
            cp.wait_send()

    return pl.pallas_call(
        body, name="join_halves",
        in_specs=[ANY] * n, out_specs=[ANY] * n,
        out_shape=[jax.ShapeDtypeStruct(tuple(s), h.dtype) for s, h in zip(shapes, hs)],
        scratch_shapes=[pltpu.SemaphoreType.DMA((n,)), pltpu.SemaphoreType.DMA((n,))],
    )(*hs)


def allreduce_small(vec):
    def body(v_ref, o_ref, land_ref, send_sems, recv_sems):
        x, y, c = _place()
        me = 4 * x + 2 * y + c
        land_ref[me] = v_ref[...]
        sends = []
        for k in range(1, N_DEV):
            px, py, pc = _flip(x, k & 4), _flip(y, k & 2), _flip(c, k & 1)
            sends.append(pltpu.make_async_remote_copy(
                src_ref=v_ref, dst_ref=land_ref.at[me], send_sem=send_sems.at[k - 1], recv_sem=recv_sems.at[k - 1],
                device_id=(px, py, pc), device_id_type=MESH))
            sends[-1].start()
        for k in range(1, N_DEV):
            px, py, pc = _flip(x, k & 4), _flip(y, k & 2), _flip(c, k & 1)
            slot = land_ref.at[4 * px + 2 * py + pc]
            pltpu.make_async_remote_copy(
                src_ref=slot, dst_ref=slot, send_sem=send_sems.at[k - 1], recv_sem=recv_sems.at[k - 1],
                device_id=(px, py, pc), device_id_type=MESH).wait_recv()
        for cp in sends:
            cp.wait_send()
        acc = land_ref[0]
        for s in range(1, N_DEV):
            acc = acc + land_ref[s]
        o_ref[...] = acc

    vm = pl.BlockSpec(memory_space=pltpu.VMEM)
    return pl.pallas_call(
        body, name="allreduce_small",
        in_specs=[vm], out_specs=vm,
        out_shape=jax.ShapeDtypeStruct(vec.shape, F32),
        scratch_shapes=[pltpu.VMEM((N_DEV,) + vec.shape, F32), pltpu.SemaphoreType.DMA((N_DEV - 1,)),
                        pltpu.SemaphoreType.DMA((N_DEV - 1,))],
    )(vec)


def adamw(w, g, m, v, name):
    rows, cols = w.shape
    tr = rows
    while tr * cols * 4 * 14 > 24 * 1024 * 1024 and tr % 16 == 0:
        tr //= 2
    c1 = 1.0 - ADAM_B1 ** ADAM_STEP
    c2 = 1.0 - ADAM_B2 ** ADAM_STEP

    def body(w_ref, g_ref, m_ref, v_ref, d_ref, nm_ref, nv_ref):
        gv = g_ref[...]
        nm = ADAM_B1 * m_ref[...] + (1.0 - ADAM_B1) * gv
        nv = ADAM_B2 * v_ref[...] + (1.0 - ADAM_B2) * (gv * gv)
        nm_ref[...] = nm
        nv_ref[...] = nv
        d_ref[...] = -ADAM_LR * ((nm / c1) / (jnp.sqrt(nv / c2) + ADAM_EPS) + ADAM_WD * w_ref[...])

    blk = pl.BlockSpec((tr, cols), lambda i: (i, 0))
    return pl.pallas_call(
        body, name=name, grid=(rows // tr,),
        in_specs=[blk] * 4, out_specs=[blk] * 3,
        out_shape=[jax.ShapeDtypeStruct((rows, cols), F32)] * 3,
        compiler_params=_params(("parallel",)),
    )(w, g, m, v)


BIG = ("ffn1_w_gate", "ffn1_w_up", "ffn1_w_down", "w_in", "b_w_uq", "b_w_ukv", "w_out",
       "ffn2_w_gate", "ffn2_w_up", "ffn2_w_down")
SMALL = ("ffn1_norm", "mix_norm", "a_q_norm", "a_k_norm", "a_rel_bias", "b_q_lat_norm", "b_kv_lat_norm",
         "b_q_nope_norm", "b_q_rope_norm", "b_k_nope_norm", "b_k_rope_norm", "ffn2_norm", "final_norm")
WEIGHTS = ("ffn1_norm", "ffn1_w_gate", "ffn1_w_up", "ffn1_w_down", "mix_norm", "w_in", "a_q_norm", "a_k_norm",
           "a_rel_bias", "b_q_lat_norm", "b_w_uq", "b_kv_lat_norm", "b_w_ukv", "b_q_nope_norm", "b_q_rope_norm",
           "b_k_nope_norm", "b_k_rope_norm", "w_out", "ffn2_norm", "ffn2_w_gate", "ffn2_w_up", "ffn2_w_down",
           "final_norm")
TRANSPOSED = ("ffn1_w_gate", "ffn1_w_up", "ffn2_w_gate", "ffn2_w_up", "w_in", "b_w_uq")
PACK_SHAPE = (8, 1024)


def _pack_small(d, last=None):
    flat = [d[n].reshape(-1) for n in SMALL]
    used = sum(f.shape[0] for f in flat)
    total = PACK_SHAPE[0] * PACK_SHAPE[1]
    tail = jnp.zeros((total - used - 1,), F32)
    end = jnp.zeros((1,), F32) if last is None else last.reshape(1)
    return jnp.concatenate(flat + [tail, end]).reshape(PACK_SHAPE)


def _unpack_small(p, like):
    flat = p.reshape(-1)
    out, off = {}, 0
    for n in SMALL:
        size = like[n].size
        out[n] = flat[off:off + size].reshape(like[n].shape)
        off += size
    return out, flat[-1]


def _cols_to_shards(g):
    rows, cols = g.shape
    return g.reshape(rows, N_SHARD, cols // N_SHARD).transpose(1, 0, 2)


def _shards_to_cols(g):
    return g.transpose(1, 0, 2).reshape(g.shape[1], -1)


def kernel(x, ffn1_norm, ffn1_w_gate, ffn1_w_up, ffn1_w_down, mix_norm, w_in, a_q_norm, a_k_norm, a_rel_bias, b_q_lat_norm, b_w_uq, b_kv_lat_norm, b_w_ukv, b_q_nope_norm, b_q_rope_norm, b_k_nope_norm, b_k_rope_norm, w_out, ffn2_norm, ffn2_w_gate, ffn2_w_up, ffn2_w_down, final_norm, loss_target, m_ffn1_norm, m_ffn1_w_gate, m_ffn1_w_up, m_ffn1_w_down, m_mix_norm, m_w_in, m_a_q_norm, m_a_k_norm, m_a_rel_bias, m_b_q_lat_norm, m_b_w_uq, m_b_kv_lat_norm, m_b_w_ukv, m_b_q_nope_norm, m_b_q_rope_norm, m_b_k_nope_norm, m_b_k_rope_norm, m_w_out, m_ffn2_norm, m_ffn2_w_gate, m_ffn2_w_up, m_ffn2_w_down, m_final_norm, v_ffn1_norm, v_ffn1_w_gate, v_ffn1_w_up, v_ffn1_w_down, v_mix_norm, v_w_in, v_a_q_norm, v_a_k_norm, v_a_rel_bias, v_b_q_lat_norm, v_b_w_uq, v_b_kv_lat_norm, v_b_w_ukv, v_b_q_nope_norm, v_b_q_rope_norm, v_b_k_nope_norm, v_b_k_rope_norm, v_w_out, v_ffn2_norm, v_ffn2_w_gate, v_ffn2_w_up, v_ffn2_w_down, v_final_norm):
    args = locals()
    view = lambda a, n: a[0].T if n in TRANSPOSED else a[0]
    wts = {n: view(args[n], n) for n in WEIGHTS}
    mom = {n: view(args["m_" + n], n) for n in WEIGHTS}
    var = {n: view(args["v_" + n], n) for n in WEIGHTS}

    shard = 2 * lax.axis_index("x") + lax.axis_index("y")
    core = lax.axis_index("c")
    first = [n for n in BIG if n not in LATE]
    own = [wts[n].astype(BF16) for n in first]
    w = {n: wts[n] if n == "a_rel_bias" else wts[n][None] for n in SMALL}
    w.update(kernel_layout({n: lax.dynamic_update_index_in_dim(got, mine, shard, 0)
                            for n, got, mine in zip(first, allgather_shards(own), own)}))

    loss, grad_x, g, landed = local_step(x[0], loss_target[0], w,
                                         late=([wts[n].astype(BF16) for n in LATE], shard))

    me = 2 * shard + core
    for n in first:
        piece = lax.dynamic_slice(g[n], (shard, core * (FS // 2), 0), (1, FS // 2, D_MODEL))
        landed[n] = lax.dynamic_update_slice(landed[n], piece, (me, 0, 0))
    halves = [sum_slots(landed[n], "sum_slots") for n in BIG]
    shapes = [wts[n].shape for n in BIG]
    axes = [_split_axis(s) for s in shapes]
    grads = dict(zip(BIG, (lax.dynamic_update_slice_in_dim(got, mine, core * mine.shape[ax], ax)
                           for got, mine, ax in zip(join_halves(halves, shapes), halves, axes))))

    small_sum, loss_sum = _unpack_small(allreduce_small(_pack_small(g, loss[0, 0])), wts)
    grads.update(small_sum)

    delta, new_m, new_v = {}, {}, {}
    for n in BIG:
        delta[n], new_m[n], new_v[n] = adamw(wts[n], grads[n], mom[n], var[n], "adamw")
    packed = adamw(_pack_small(wts), _pack_small(grads), _pack_small(mom), _pack_small(var), "adamw_small")
    for dst, p in zip((delta, new_m, new_v), packed):
        dst.update(_unpack_small(p, wts)[0])

    lead = lambda d: [(d[n].T if n in TRANSPOSED else d[n])[None] for n in WEIGHTS]
    return (loss_sum, grad_x[None], *lead(grads), *lead(delta), *lead(new_m), *lead(new_v))
```

```python
import numpy as np
import jax
import jax.numpy as jnp
from jax import lax
from jax.experimental import pallas as pl
from jax.experimental.pallas import tpu as pltpu

F32 = jnp.float32
BF16 = jnp.bfloat16
EPS = 1e-6
NEG = -1e30

D_MODEL = 1024
D_FF = 2816
N_SHARD = 4
FS = D_FF // N_SHARD
CHUNK = 64
A_LEFT = 8
A_MAX_REL = 128
HEADS = 8
HD = 64
ROPE = 32
PROJ_W = 2048
IN_COLS = 1952
B_SCALE = 96 ** -0.5
LANES = 128

ADAM_LR = 0.001
ADAM_B1 = 0.9
ADAM_B2 = 0.999
ADAM_EPS = 1e-08
ADAM_WD = 0.01
ADAM_STEP = 10

VMEM_LIMIT = 56 * 1024 * 1024

MESH = pl.DeviceIdType.MESH


def _dot(a, b):
    return lax.dot_general(a, b, (((1,), (0,)), ((), ())), preferred_element_type=F32)


def _dot_nt(a, b):
    return lax.dot_general(a, b, (((1,), (1,)), ((), ())), preferred_element_type=F32)


def _dot_tn(a, b):
    return lax.dot_general(a, b, (((0,), (0,)), ((), ())), preferred_element_type=F32)


def _params(sem):
    return pltpu.CompilerParams(dimension_semantics=sem, vmem_limit_bytes=VMEM_LIMIT)


def _rms(xv):
    r = lax.rsqrt(jnp.mean(xv * xv, axis=-1, keepdims=True) + EPS)
    return r, xv * r


def ffn_fwd(x, g, wg, wu, wd, name, gather=(), loss_head=None):
    t, d = x.shape
    tm = 512
    ni = t // tm
    plan = GatherPlan(gather)
    n = plan.n
    head = () if loss_head is None else tuple(loss_head)
    q = len(head)

    def body(*refs):
        x_ref, g_ref, wg_ref, wu_ref, wd_ref = refs[:5]
        head_in, ins = refs[5:5 + q], refs[5 + q:5 + q + n]
        o_ref, gp_ref, up_ref = refs[5 + q + n:8 + q + n]
        head_out, outs = refs[8 + q + n:8 + 2 * q + n], refs[8 + 2 * q + n:8 + 2 * q + 2 * n]
        h_ref, acc_ref = refs[8 + 2 * q + 2 * n:10 + 2 * q + 2 * n]
        sems = refs[10 + 2 * q + 2 * n:]
        i, j = pl.program_id(0), pl.program_id(1)
        if n:
            pl.when((i == 0) & (j == 0))(lambda: plan.start(ins, outs, sems))
            pl.when((i == (3 * ni) // 4) & (j == 0))(lambda: plan.forward(ins, outs, sems))

        @pl.when(j == 0)
        def _():
            _, xn = _rms(x_ref[...])
            h_ref[...] = (xn * g_ref[...]).astype(BF16)
            acc_ref[...] = jnp.zeros_like(acc_ref)

        h = h_ref[...]
        gp = _dot_nt(h, wg_ref[0])
        up = _dot_nt(h, wu_ref[0])
        gp_ref[0] = gp
        up_ref[0] = up
        a = (gp * jax.nn.sigmoid(gp) * up).astype(BF16)
        acc_ref[...] += _dot(a, wd_ref[0])

        @pl.when(j == N_SHARD - 1)
        def _():
            y = x_ref[...] + 0.5 * acc_ref[...]
            if not q:
                o_ref[...] = y
                return
            (gf_ref, t_ref), (dgf_ref, loss_ref) = head_in, head_out
            r, yn = _rms(y)
            gf = gf_ref[...]
            e = yn * gf - t_ref[...]
            dout = e * (1.0 / d)
            dng = dout * gf
            o_ref[...] = r * (dng - yn * jnp.mean(dng * yn, axis=-1, keepdims=True))

            @pl.when(i == 0)
            def _():
                dgf_ref[...] = jnp.zeros_like(dgf_ref)
                loss_ref[...] = jnp.zeros_like(loss_ref)

            dgf_ref[...] += jnp.sum(dout * yn, axis=0, keepdims=True)
            part = jnp.sum(jnp.sum(e * e, axis=-1, keepdims=True), axis=0, keepdims=True) * (0.5 / d)
            loss_ref[...] += jnp.broadcast_to(part, loss_ref.shape)

        if n:
            pl.when((i == ni - 1) & (j == N_SHARD - 1))(lambda: plan.finish(ins, outs, sems))

    tok = pl.BlockSpec((tm, d), lambda i, j: (i, 0))
    vec = pl.BlockSpec((1, d), lambda i, j: (0, 0))
    chunk = pl.BlockSpec((1, FS, d), lambda i, j: (j, 0, 0))
    pre = pl.BlockSpec((1, tm, FS), lambda i, j: (j, i, 0))
    return pl.pallas_call(
        body, name=name, grid=(ni, N_SHARD),
        in_specs=[tok, vec, chunk, chunk, chunk] + [vec, tok][:q] + [ANY] * n,
        out_specs=[tok, pre, pre] + [vec, pl.BlockSpec((1, LANES), lambda i, j: (0, 0))][:q] + [ANY] * n,
        out_shape=[jax.ShapeDtypeStruct((t, d), F32), jax.ShapeDtypeStruct((N_SHARD, t, FS), F32),
                   jax.ShapeDtypeStruct((N_SHARD, t, FS), F32)]
        + [jax.ShapeDtypeStruct((1, d), F32), jax.ShapeDtypeStruct((1, LANES), F32)][:q] + plan.out_shape,
        scratch_shapes=[pltpu.VMEM((tm, d), BF16), pltpu.VMEM((tm, d), F32)] + (plan.scratch if n else []),
        compiler_params=_params(("arbitrary", "arbitrary")),
    )(x, g, wg, wu, wd, *head, *gather)


def ffn_bwd(x, dout, g, wg, wu, wd, gate, up_pre, name, scatter=(), spread=None):
    t, d = x.shape
    tm = 512
    ni = t // tm
    hf = FS // 2
    plan = ScatterPlan(scatter)
    m = plan.n
    k = 0 if spread is None else 3
    steps = jnp.arange(N_SHARD, dtype=jnp.int32)
    order = steps if spread is None else (spread + 1 + steps) % N_SHARD

    def body(*refs):
        ord_ref, x_ref, do_ref, g_ref, wg_ref, wu_ref, wd_ref, gp_ref, up_ref = refs[:9]
        ins, (dwg_out, dwu_out, dwd_out, dhp_ref) = refs[9:9 + m], refs[9 + m:13 + m]
        outs, lands = refs[13 + m:13 + 2 * m], refs[13 + 2 * m:13 + 2 * m + k]
        dwg_ref, dwu_ref, dwd_ref = refs[13 + 2 * m + k:16 + 2 * m + k]
        sems = refs[16 + 2 * m + k:19 + 2 * m + k] if m else ()
        stage_ref = refs[-3] if k else None
        j, i = pl.program_id(0), pl.program_id(1)
        if m:
            pl.when((j == 0) & (i == 0))(lambda: plan.start(ins, outs, sems))

        def chunk_copies(jj):
            send_sems, recv_sems = refs[-2:]
            px, py, pc = _place()
            me = 4 * px + 2 * py + pc
            tx, ty = ord_ref[jj] // 2, ord_ref[jj] % 2
            copies = []
            for n_ in range(3):
                for h_ in range(2):
                    copies.append((pltpu.make_async_remote_copy(
                        src_ref=stage_ref.at[n_, pl.ds(h_ * hf, hf)], dst_ref=lands[n_].at[me],
                        send_sem=send_sems.at[6 * jj + 2 * n_ + h_], recv_sem=recv_sems.at[3 * me + n_],
                        device_id=(tx, ty, h_), device_id_type=MESH), (tx != px) | (ty != py) | (pc != h_)))
            return copies

        def arrivals():
            send_sems, recv_sems = refs[-2:]
            px, py, pc = _place()
            me = 4 * px + 2 * py + pc
            for s_ in range(N_DEV):
                for n_ in range(3):
                    slot = lands[n_].at[s_]
                    cp = pltpu.make_async_remote_copy(
                        src_ref=slot, dst_ref=slot, send_sem=send_sems.at[0], recv_sem=recv_sems.at[3 * s_ + n_],
                        device_id=(px, py, pc), device_id_type=MESH)
                    pl.when(me != s_)(cp.wait_recv)

        _, xn = _rms(x_ref[...])
        h = (xn * g_ref[...]).astype(BF16)
        dz = (0.5 * do_ref[...]).astype(BF16)
        wgv, wuv, wdv = wg_ref[0], wu_ref[0], wd_ref[0]
        gp, up = gp_ref[0], up_ref[0]
        s = jax.nn.sigmoid(gp)
        sg = gp * s
        a = (sg * up).astype(BF16)
        da = _dot_nt(dz, wdv)
        dup = (da * sg).astype(BF16)
        dgp = (da * up * (s * (1.0 + gp * (1.0 - s)))).astype(BF16)

        @pl.when(i == 0)
        def _():
            dwg_ref[...] = jnp.zeros_like(dwg_ref)
            dwu_ref[...] = jnp.zeros_like(dwu_ref)
            dwd_ref[...] = jnp.zeros_like(dwd_ref)

        dwd_ref[...] += _dot_tn(a, dz)
        dwg_ref[...] += _dot_tn(dgp, h)
        dwu_ref[...] += _dot_tn(dup, h)
        dhp_ref[0] = (_dot(dgp, wgv) + _dot(dup, wuv)).astype(BF16)

        @pl.when(i == ni - 1)
        def _():
            dwg_out[0] = dwg_ref[...].astype(BF16)
            dwu_out[0] = dwu_ref[...].astype(BF16)
            dwd_out[0] = dwd_ref[...].astype(BF16)
            if k:
                @pl.when(j >= 1)
                def _():
                    for cp, leaves in chunk_copies(j - 1):
                        pl.when(leaves)(cp.wait_send)
                for n_, acc in enumerate((dwg_ref, dwu_ref, dwd_ref)):
                    stage_ref[n_] = acc[...].astype(BF16)
                for cp, leaves in chunk_copies(j):
                    pl.when(leaves)(cp.start)

                @pl.when(j == N_SHARD - 1)
                def _():
                    for cp, leaves in chunk_copies(N_SHARD - 1):
                        pl.when(leaves)(cp.wait_send)
                    arrivals()

        if m:
            pl.when((j == N_SHARD - 1) & (i == ni - 1))(lambda: plan.finish(ins, outs, sems))

    chunk = pl.BlockSpec((1, FS, d), lambda j, i, o: (o[j], 0, 0))
    tok = pl.BlockSpec((tm, d), lambda j, i, o: (i, 0))
    pre = pl.BlockSpec((1, tm, FS), lambda j, i, o: (o[j], i, 0))
    grid_spec = pltpu.PrefetchScalarGridSpec(
        num_scalar_prefetch=1, grid=(N_SHARD, ni),
        in_specs=[tok, tok, pl.BlockSpec((1, d), lambda j, i, o: (0, 0)), chunk, chunk, chunk, pre, pre] + [ANY] * m,
        out_specs=[chunk, chunk, chunk, pl.BlockSpec((1, tm, d), lambda j, i, o: (o[j], i, 0))] + [ANY] * (m + k),
        scratch_shapes=[pltpu.VMEM((FS, d), F32), pltpu.VMEM((FS, d), F32), pltpu.VMEM((FS, d), F32)]
        + (plan.scratch if m else [])
        + ([pltpu.VMEM((3, FS, d), BF16), pltpu.SemaphoreType.DMA((6 * N_SHARD,)),
            pltpu.SemaphoreType.DMA((3 * N_DEV,))] if k else []))
    return pl.pallas_call(
        body, name=name, grid_spec=grid_spec,
        out_shape=[jax.ShapeDtypeStruct((N_SHARD, FS, d), BF16),
                   jax.ShapeDtypeStruct((N_SHARD, FS, d), BF16),
                   jax.ShapeDtypeStruct((N_SHARD, FS, d), BF16),
                   jax.ShapeDtypeStruct((N_SHARD, t, d), BF16)] + plan.out_shape
        + [jax.ShapeDtypeStruct((N_DEV, hf, d), BF16)] * k,
        compiler_params=_params(("arbitrary", "arbitrary")),
    )(order, x, dout, g, wg, wu, wd, gate, up_pre, *scatter)


def norm_bwd(x, g, dhp, dres, name):
    t, d = x.shape
    p = dhp.shape[0]
    tm = 512

    def body(x_ref, g_ref, dhp_ref, dres_ref, dx_ref, dg_ref):
        i = pl.program_id(0)
        r, xn = _rms(x_ref[...])
        dh = dhp_ref[0].astype(F32)
        for q in range(1, p):
            dh = dh + dhp_ref[q].astype(F32)
        dhg = dh * g_ref[...]
        dx_ref[...] = dres_ref[...] + r * (dhg - xn * jnp.mean(dhg * xn, axis=-1, keepdims=True))

        @pl.when(i == 0)
        def _():
            dg_ref[...] = jnp.zeros_like(dg_ref)

        dg_ref[...] += jnp.sum(dh * xn, axis=0, keepdims=True)

    return pl.pallas_call(
        body, name=name, grid=(t // tm,),
        in_specs=[pl.BlockSpec((tm, d), lambda i: (i, 0)),
                  pl.BlockSpec((1, d), lambda i: (0, 0)),
                  pl.BlockSpec((p, tm, d), lambda i: (0, i, 0)),
                  pl.BlockSpec((tm, d), lambda i: (i, 0))],
        out_specs=[pl.BlockSpec((tm, d), lambda i: (i, 0)),
                   pl.BlockSpec((1, d), lambda i: (0, 0))],
        out_shape=[jax.ShapeDtypeStruct((t, d), F32), jax.ShapeDtypeStruct((1, d), F32)],
        compiler_params=_params(("arbitrary",)),
    )(x, g, dhp, dres)


def out_proj(x, oa, ob_t, w):
    t, d = x.shape
    half = oa.shape[1]
    tm = 512

    def body(x_ref, oa_ref, obt_ref, w_ref, o_ref):
        o_ref[...] = (x_ref[...] + _dot(oa_ref[...], w_ref[0:half, :])
                      + _dot_tn(obt_ref[...], w_ref[half:2 * half, :]))

    return pl.pallas_call(
        body, name="out_proj", grid=(t // tm,),
        in_specs=[pl.BlockSpec((tm, d), lambda i: (i, 0)),
                  pl.BlockSpec((tm, half), lambda i: (i, 0)),
                  pl.BlockSpec((half, tm), lambda i: (0, i)),
                  pl.BlockSpec((2 * half, d), lambda i: (0, 0))],
        out_specs=pl.BlockSpec((tm, d), lambda i: (i, 0)),
        out_shape=jax.ShapeDtypeStruct((t, d), F32),
        compiler_params=_params(("parallel",)),
    )(x, oa, ob_t, w)


def out_proj_bwd(x, g, dhp, dres, w, oa, ob_t):
    t, d = x.shape
    half = w.shape[0] // 2
    p = dhp.shape[0]
    tm = 512
    ni = t // tm

    def body(x_ref, g_ref, dhp_ref, dres_ref, w_ref, oa_ref, obt_ref,
             dx_ref, dg_ref, da_ref, db_ref, dbt_ref, dw_ref, acc_ref):
        i = pl.program_id(0)

        @pl.when(i == 0)
        def _():
            dg_ref[...] = jnp.zeros_like(dg_ref)
            acc_ref[...] = jnp.zeros_like(acc_ref)

        r, xn = _rms(x_ref[...])
        dh = dhp_ref[0].astype(F32)
        for s in range(1, p):
            dh = dh + dhp_ref[s].astype(F32)
        dhg = dh * g_ref[...]
        dx = dres_ref[...] + r * (dhg - xn * jnp.mean(dhg * xn, axis=-1, keepdims=True))
        dx_ref[...] = dx
        dg_ref[...] += jnp.sum(dh * xn, axis=0, keepdims=True)
        dxb = dx.astype(BF16)
        da_ref[...] = _dot_nt(dxb, w_ref[0:half, :]).astype(BF16)
        db_ref[...] = _dot_nt(dxb, w_ref[half:2 * half, :]).astype(BF16)
        dbt_ref[...] = _dot_nt(w_ref[half:2 * half, :], dxb).astype(BF16)
        acc_ref[0:half, :] += _dot_tn(oa_ref[...], dxb)
        acc_ref[half:2 * half, :] += _dot(obt_ref[...], dxb)

        @pl.when(i == ni - 1)
        def _():
            dw_ref[...] = acc_ref[...].astype(BF16)

    row = lambda w_: pl.BlockSpec((tm, w_), lambda i: (i, 0))
    col = pl.BlockSpec((half, tm), lambda i: (0, i))
    whole = pl.BlockSpec((2 * half, d), lambda i: (0, 0))
    vec = pl.BlockSpec((1, d), lambda i: (0, 0))
    return pl.pallas_call(
        body, name="out_proj_bwd", grid=(ni,),
        in_specs=[row(d), vec, pl.BlockSpec((p, tm, d), lambda i: (0, i, 0)), row(d), whole, row(half), col],
        out_specs=[row(d), vec, row(half), row(half), col, whole],
        out_shape=[jax.ShapeDtypeStruct((t, d), F32), jax.ShapeDtypeStruct((1, d), F32),
                   jax.ShapeDtypeStruct((t, half), BF16), jax.ShapeDtypeStruct((t, half), BF16),
                   jax.ShapeDtypeStruct((half, t), BF16), jax.ShapeDtypeStruct((2 * half, d), BF16)],
        scratch_shapes=[pltpu.VMEM((2 * half, d), F32)],
        compiler_params=_params(("arbitrary",)),
    )(x, g, dhp, dres, w, oa, ob_t)


def _lane(shape):
    return lax.broadcasted_iota(jnp.int32, shape, 1)


PAIR = (0, HD, LANES)
Q_HEAD = (0, HD, HD + ROPE, LANES)
K_ROPE = (0, ROPE, LANES)


def _seg_mean(z, bounds):
    seg = lambda v: sum([(v >= b).astype(jnp.int32) for b in bounds[1:-1]], jnp.zeros_like(v))
    rows = seg(lax.broadcasted_iota(jnp.int32, (LANES, LANES), 0))
    cols = seg(lax.broadcasted_iota(jnp.int32, (LANES, LANES), 1))
    same = (rows == cols).astype(BF16)
    lane = _lane((1, LANES))
    inv = sum([jnp.where((lane >= a) & (lane < b), 1.0 / (b - a), 0.0) for a, b in zip(bounds[:-1], bounds[1:])])
    hi = z.astype(BF16)
    lo = (z - hi.astype(F32)).astype(BF16)
    return (_dot(hi, same) + _dot(lo, same)) * inv


def _seg_norm(x, bounds):
    r = lax.rsqrt(_seg_mean(x * x, bounds) + EPS)
    return r, x * r


def _seg_norm_bwd(r, xn, dyg, bounds):
    return r * (dyg - xn * _seg_mean(dyg * xn, bounds))


A_TM = 256


def mix_fwd(x, g, w, gq, gk, gcq, gckv):
    t, d = x.shape
    tm = A_TM

    def body(x_ref, g_ref, w_ref, gq_ref, gk_ref, gcq_ref, gckv_ref,
             h_ref, p_ref, qa_ref, ka_ref, va_ref, cq_ref, ckv_ref):
        _, xn = _rms(x_ref[...])
        h = (xn * g_ref[...]).astype(BF16)
        h_ref[...] = h
        p_ref[...] = _dot_nt(h, w_ref[...])
        for p in range(4):
            sl = slice(LANES * p, LANES * (p + 1))
            _, xn = _seg_norm(p_ref[:, sl], PAIR)
            qa_ref[:, sl] = (xn * gq_ref[:, sl] * 0.125).astype(BF16)
            _, xn = _seg_norm(p_ref[:, 512 + LANES * p:512 + LANES * (p + 1)], PAIR)
            ka_ref[:, sl] = (xn * gk_ref[:, sl]).astype(BF16)
        va_ref[...] = p_ref[:, 1024:1536].astype(BF16)
        _, xn = _rms(p_ref[:, 1536:1792])
        cq_ref[...] = (xn * gcq_ref[...]).astype(BF16)
        _, xn = _rms(p_ref[:, 1792:1920])
        ckv_ref[...] = (xn * gckv_ref[...]).astype(BF16)

    row = lambda w: pl.BlockSpec((tm, w), lambda i: (i, 0))
    vec = lambda w: pl.BlockSpec((1, w), lambda i: (0, 0))
    return pl.pallas_call(
        body, name="mix_fwd", grid=(t // tm,),
        in_specs=[row(d), vec(d), pl.BlockSpec((PROJ_W, d), lambda i: (0, 0)), vec(512), vec(512), vec(256), vec(128)],
        out_specs=[row(d), row(PROJ_W), row(512), row(512), row(512), row(256), row(128)],
        out_shape=[jax.ShapeDtypeStruct((t, d), BF16), jax.ShapeDtypeStruct((t, PROJ_W), F32)]
        + [jax.ShapeDtypeStruct((t, w_), BF16) for w_ in (512, 512, 512, 256, 128)],
        compiler_params=_params(("parallel",)),
    )(x, g, w, gq, gk, gcq, gckv)


def mix_bwd(proj, x, h, dres, w, g, dqa, dkp, dvp, dcq, dckv, dkr, gq, gk, gcq, gckv):
    t, d = x.shape
    tm = A_TM
    nb = t // tm

    def body(p_ref, x_ref, h_ref, dres_ref, w_ref, g_ref, dqa_ref, dk0_ref, dk1_ref, dk2_ref, dv0_ref, dv1_ref,
             dv2_ref, dcq_ref, dckv_ref, dkr_ref, gq_ref, gk_ref, gcq_ref, gckv_ref,
             dx_ref, dw_ref, dg_ref, dgq_ref, dgk_ref, dgcq_ref, dgckv_ref, dp_ref, acc_ref):
        i = pl.program_id(0)

        @pl.when(i == 0)
        def _():
            for ref in (acc_ref, dg_ref, dgq_ref, dgk_ref, dgcq_ref, dgckv_ref):
                ref[...] = jnp.zeros_like(ref)

        has1 = (i + 1 < nb).astype(F32)
        has2 = (i + 2 < nb).astype(F32)
        for p in range(4):
            sl = slice(LANES * p, LANES * (p + 1))
            r, xn = _seg_norm(p_ref[:, sl], PAIR)
            dy = dqa_ref[:, sl] * 0.125
            dp_ref[:, sl] = _seg_norm_bwd(r, xn, dy * gq_ref[:, sl], PAIR).astype(BF16)
            dgq_ref[:, sl] += jnp.sum(dy * xn, axis=0, keepdims=True)
            ks = slice(512 + LANES * p, 512 + LANES * (p + 1))
            r, xn = _seg_norm(p_ref[:, ks], PAIR)
            dy = dk0_ref[0, :, sl] + has1 * dk1_ref[0, :, sl] + has2 * dk2_ref[0, :, sl]
            dp_ref[:, ks] = _seg_norm_bwd(r, xn, dy * gk_ref[:, sl], PAIR).astype(BF16)
            dgk_ref[:, sl] += jnp.sum(dy * xn, axis=0, keepdims=True)
        dp_ref[:, 1024:1536] = (dv0_ref[0] + has1 * dv1_ref[0] + has2 * dv2_ref[0]).astype(BF16)
        for (a, b, dlat_ref, glat_ref, dglat_ref) in ((1536, 1792, dcq_ref, gcq_ref, dgcq_ref),
                                                      (1792, 1920, dckv_ref, gckv_ref, dgckv_ref)):
            r, xn = _rms(p_ref[:, a:b])
            dy = dlat_ref[...]
            dyg = dy * glat_ref[...]
            dp_ref[:, a:b] = (r * (dyg - xn * jnp.mean(dyg * xn, axis=-1, keepdims=True))).astype(BF16)
            dglat_ref[...] += jnp.sum(dy * xn, axis=0, keepdims=True)
        dp_ref[:, 1920:2048] = dkr_ref[...].astype(BF16)

        dproj = dp_ref[...]
        acc_ref[...] += _dot_tn(dproj, h_ref[...])
        dh = _dot(dproj, w_ref[...])
        r, xn = _rms(x_ref[...])
        dhg = dh * g_ref[...]
        dx_ref[...] = dres_ref[...] + r * (dhg - xn * jnp.mean(dhg * xn, axis=-1, keepdims=True))
        dg_ref[...] += jnp.sum(dh * xn, axis=0, keepdims=True)

        @pl.when(i == nb - 1)
        def _():
            dw_ref[...] = acc_ref[...].astype(BF16)

    row = lambda w_: pl.BlockSpec((tm, w_), lambda i: (i, 0))
    vec = lambda w_: pl.BlockSpec((1, w_), lambda i: (0, 0))
    part = lambda s: pl.BlockSpec((1, tm, 512), lambda i: (s, jnp.minimum(i + s, nb - 1), 0))
    whole = pl.BlockSpec((PROJ_W, d), lambda i: (0, 0))
    return pl.pallas_call(
        body, name="mix_bwd", grid=(nb,),
        in_specs=[row(PROJ_W), row(d), row(d), row(d), whole, vec(d), row(512), part(0), part(1), part(2),
                  part(0), part(1), part(2), row(256), row(128), row(128), vec(512), vec(512), vec(256), vec(128)],
        out_specs=[row(d), whole, vec(d), vec(512), vec(512), vec(256), vec(128)],
        out_shape=[jax.ShapeDtypeStruct((t, d), F32), jax.ShapeDtypeStruct((PROJ_W, d), BF16)]
        + [jax.ShapeDtypeStruct((1, w_), F32) for w_ in (d, 512, 512, 256, 128)],
        scratch_shapes=[pltpu.VMEM((tm, PROJ_W), BF16), pltpu.VMEM((PROJ_W, d), F32)],
        compiler_params=_params(("arbitrary",)),
    )(proj, x, h, dres, w, g, dqa, dkp, dkp, dkp, dvp, dvp, dvp, dcq, dckv, dkr, gq, gk, gcq, gckv)


def _roll(x, shift):
    return pltpu.roll(x, shift % LANES, 1)


def _rope(y, c, s1, s2):
    return y * c + _roll(y, -16) * s1 + _roll(y, 16) * s2


def _rope_bwd(d, c, s1, s2):
    return d * c + _roll(d * s1, 16) + _roll(d * s2, -16)


def _up_proj(cq_ref, ckv_ref, wuq_ref, wukv_ref):
    return _dot_nt(cq_ref[...], wuq_ref[...]), _dot(ckv_ref[...], wukv_ref[...])


def prep2_fwd(cqn, ckvn, proj, wuq, wukv, gq, gk, gkr, tabs):
    t = cqn.shape[0]
    tm = A_TM

    def body(cq_ref, ckv_ref, kr_ref, wuq_ref, wukv_ref, gq_ref, gk_ref, gkr_ref, tab_ref, qf_ref, kf_ref, vp_ref):
        q_all, kv_all = _up_proj(cq_ref, ckv_ref, wuq_ref, wukv_ref)
        _, xn = _seg_norm(kr_ref[...], K_ROPE)
        kpe = _roll(_rope(xn * gkr_ref[...], tab_ref[3], tab_ref[4], tab_ref[5]), 64)
        for h in range(HEADS):
            sl = slice(LANES * h, LANES * (h + 1))
            _, xn = _seg_norm(q_all[:, sl], Q_HEAD)
            qf_ref[:, sl] = (_rope(xn * gq_ref[...], tab_ref[0], tab_ref[1], tab_ref[2]) * B_SCALE2).astype(BF16)
            x = kv_all[:, sl]
            lo = _lane(x.shape) < HD
            _, xkn = _seg_norm(jnp.where(lo, x, 0.0), PAIR)
            kf_ref[:, sl] = (xkn * gk_ref[...] + kpe).astype(BF16)
            if h % 2 == 0:
                v_even = _roll(x, 64)
            else:
                vp_ref[:, LANES * (h // 2):LANES * (h // 2 + 1)] = jnp.where(lo, v_even, x).astype(BF16)

    row = lambda w: pl.BlockSpec((tm, w), lambda i: (i, 0))
    vec = lambda w: pl.BlockSpec((1, w), lambda i: (0, 0))
    full = lambda a: pl.BlockSpec(a.shape, lambda i: (0, 0))
    return pl.pallas_call(
        body, name="prep2_fwd", grid=(t // tm,),
        in_specs=[row(256), row(128), pl.BlockSpec((tm, LANES), lambda i: (i, 15)), full(wuq), full(wukv),
                  vec(128), vec(128), vec(128), pl.BlockSpec((6, tm, LANES), lambda i: (0, i, 0))],
        out_specs=[row(1024), row(1024), row(512)],
        out_shape=[jax.ShapeDtypeStruct((t, 1024), BF16), jax.ShapeDtypeStruct((t, 1024), BF16),
                   jax.ShapeDtypeStruct((t, 512), BF16)],
        compiler_params=_params(("parallel",)),
    )(cqn, ckvn, proj, wuq, wukv, gq, gk, gkr, tabs)


def prep2_bwd(cqn, ckvn, proj, wuq, wukv, dqf, dkf, dvp, gq, gk, gkr, tabs):
    t = cqn.shape[0]
    tm = A_TM

    def body(cq_ref, ckv_ref, kr_ref, wuq_ref, wukv_ref, dqf_ref, dkf_ref, dvp_ref, gq_ref, gk_ref, gkr_ref, tab_ref,
             dcq_ref, dckv_ref, dkr_ref, dwuq_ref, dwukv_ref, dgq_ref, dgk_ref, dgkr_ref, dq_ref, dkv_ref):
        i = pl.program_id(0)

        @pl.when(i == 0)
        def _():
            for ref in (dwuq_ref, dwukv_ref, dgq_ref, dgk_ref, dgkr_ref):
                ref[...] = jnp.zeros_like(ref)

        q_all, kv_all = _up_proj(cq_ref, ckv_ref, wuq_ref, wukv_ref)
        dgq = jnp.zeros((1, LANES), F32)
        dgk = jnp.zeros((1, LANES), F32)
        dkpe = jnp.zeros((tm, LANES), F32)
        for h in range(HEADS):
            sl = slice(LANES * h, LANES * (h + 1))
            lane = _lane((tm, LANES))
            mn, mr = lane < HD, (lane >= HD) & (lane < HD + ROPE)
            r, xn = _seg_norm(q_all[:, sl], Q_HEAD)
            dy = _rope_bwd(dqf_ref[sl, :].T, tab_ref[0], tab_ref[1], tab_ref[2])
            dyg = dy * gq_ref[...]
            dq_ref[:, sl] = _seg_norm_bwd(r, xn, dyg, Q_HEAD).astype(BF16)
            dgq = dgq + jnp.sum(dy * xn, axis=0, keepdims=True)

            x = kv_all[:, sl]
            dk = dkf_ref[:, sl]
            rk, xkn = _seg_norm(jnp.where(mn, x, 0.0), PAIR)
            dyk = jnp.where(mn, dk, 0.0)
            dxk = _seg_norm_bwd(rk, xkn, dyk * gk_ref[...], PAIR)
            dgk = dgk + jnp.sum(dyk * xkn, axis=0, keepdims=True)
            dkpe = dkpe + jnp.where(mr, dk, 0.0)
            dvpair = dvp_ref[:, LANES * (h // 2):LANES * (h // 2 + 1)]
            dv = _roll(dvpair, 64) if h % 2 == 0 else dvpair
            dkv_ref[:, sl] = jnp.where(mn, dxk, dv).astype(BF16)

        r, xn = _seg_norm(kr_ref[...], K_ROPE)
        dy = _rope_bwd(_roll(dkpe, 64), tab_ref[3], tab_ref[4], tab_ref[5])
        dkr_ref[...] = _seg_norm_bwd(r, xn, dy * gkr_ref[...], K_ROPE)
        dgq_ref[...] += dgq
        dgk_ref[...] += dgk
        dgkr_ref[...] += jnp.sum(dy * xn, axis=0, keepdims=True)
        dqb, dkvb = dq_ref[...], dkv_ref[...]
        dcq_ref[...] = _dot(dqb, wuq_ref[...])
        dckv_ref[...] = _dot_nt(dkvb, wukv_ref[...])
        dwuq_ref[...] += _dot_tn(dqb, cq_ref[...])
        dwukv_ref[...] += _dot_tn(ckv_ref[...], dkvb)

    row = lambda w: pl.BlockSpec((tm, w), lambda i: (i, 0))
    vec = lambda w: pl.BlockSpec((1, w), lambda i: (0, 0))
    full = lambda a: pl.BlockSpec(a.shape, lambda i: (0, 0))
    return pl.pallas_call(
        body, name="prep2_bwd", grid=(t // tm,),
        in_specs=[row(256), row(128), pl.BlockSpec((tm, LANES), lambda i: (i, 15)), full(wuq), full(wukv),
                  pl.BlockSpec((1024, tm), lambda i: (0, i)), row(1024), row(512),
                  vec(128), vec(128), vec(128), pl.BlockSpec((6, tm, LANES), lambda i: (0, i, 0))],
        out_specs=[row(256), row(128), row(128), full(wuq), full(wukv), vec(128), vec(128), vec(128)],
        out_shape=[jax.ShapeDtypeStruct((t, 256), F32), jax.ShapeDtypeStruct((t, 128), F32),
                   jax.ShapeDtypeStruct((t, LANES), F32), jax.ShapeDtypeStruct(wuq.shape, F32),
                   jax.ShapeDtypeStruct(wukv.shape, F32)] + [jax.ShapeDtypeStruct((1, LANES), F32)] * 3,
        scratch_shapes=[pltpu.VMEM((tm, 1024), BF16), pltpu.VMEM((tm, 1024), BF16)],
        compiler_params=_params(("arbitrary",)),
    )(cqn, ckvn, proj, wuq, wukv, dqf, dkf, dvp, gq, gk, gkr, tabs)


A_TQ = 256
A_WIN = 3 * A_TQ


def _a_specs(t):
    nb = t // A_TQ
    blk = lambda s: pl.BlockSpec((A_TQ, 512), lambda i: (jnp.maximum(i - s, 0), 0))
    return nb, blk


def _a_exp(q_ref, kc, b_ref, head, sl, lo):
    hm = lo if head % 2 == 0 else ~lo
    qm = jnp.where(hm, q_ref[:, sl], jnp.zeros((), BF16))
    s = _dot_nt(qm, kc) + b_ref[0, head]
    e = jnp.exp(s - jnp.max(s, axis=-1, keepdims=True))
    return hm, qm, e, 1.0 / jnp.sum(e, axis=-1, keepdims=True)


def _a_bias_spec():
    return pl.BlockSpec((1, HEADS, A_TQ, A_WIN), lambda i: (jnp.minimum(i, 2), 0, 0, 0))


def attn_a_fwd(qa, ka, va, bias):
    t = qa.shape[0]
    nb, blk = _a_specs(t)

    def body(q_ref, k2_ref, k1_ref, k0_ref, v2_ref, v1_ref, v0_ref, b_ref, o_ref):
        lo = _lane((A_TQ, LANES)) < HD
        for p in range(4):
            sl = slice(LANES * p, LANES * (p + 1))
            kc = jnp.concatenate([k2_ref[:, sl], k1_ref[:, sl], k0_ref[:, sl]], axis=0)
            vc = jnp.concatenate([v2_ref[:, sl], v1_ref[:, sl], v0_ref[:, sl]], axis=0)
            outs = []
            for h2 in range(2):
                _, _, e, inv = _a_exp(q_ref, kc, b_ref, 2 * p + h2, sl, lo)
                outs.append(_dot(e.astype(BF16), vc) * inv)
            o_ref[:, sl] = jnp.where(lo, outs[0], outs[1]).astype(BF16)

    return pl.pallas_call(
        body, name="attn_a_fwd", grid=(nb,),
        in_specs=[blk(0), blk(2), blk(1), blk(0), blk(2), blk(1), blk(0), _a_bias_spec()],
        out_specs=pl.BlockSpec((A_TQ, 512), lambda i: (i, 0)),
        out_shape=jax.ShapeDtypeStruct((t, 512), BF16),
        compiler_params=_params(("parallel",)),
    )(qa, ka, ka, ka, va, va, va, bias)


def attn_a_bwd(qa, ka, va, bias, do):
    t = qa.shape[0]
    nb, blk = _a_specs(t)

    def body(q_ref, k2_ref, k1_ref, k0_ref, v2_ref, v1_ref, v0_ref, b_ref, do_ref, dq_ref, dk_ref, dv_ref, db_ref):
        qb = pl.program_id(0)

        @pl.when(qb == 0)
        def _():
            db_ref[...] = jnp.zeros_like(db_ref)

        lo = _lane((A_TQ, LANES)) < HD
        for p in range(4):
            sl = slice(LANES * p, LANES * (p + 1))
            kc = jnp.concatenate([k2_ref[:, sl], k1_ref[:, sl], k0_ref[:, sl]], axis=0)
            vc = jnp.concatenate([v2_ref[:, sl], v1_ref[:, sl], v0_ref[:, sl]], axis=0)
            dqs = []
            dkc = jnp.zeros((A_WIN, LANES), F32)
            dvc = jnp.zeros((A_WIN, LANES), F32)
            for h2 in range(2):
                head = 2 * p + h2
                hm, qm, e, inv = _a_exp(q_ref, kc, b_ref, head, sl, lo)
                pr = e * inv
                dom = jnp.where(hm, do_ref[:, sl], jnp.zeros((), BF16))
                dp = _dot_nt(dom, vc)
                ds = pr * (dp - jnp.sum(pr * dp, axis=-1, keepdims=True))
                db_ref[head] += ds
                dsb = ds.astype(BF16)
                dqs.append(_dot(dsb, kc))
                dkc = dkc + _dot_tn(dsb, qm)
                dvc = dvc + _dot_tn(pr.astype(BF16), dom)
            dq_ref[:, sl] = jnp.where(lo, dqs[0], dqs[1])
            for s in range(3):
                rows = slice(A_TQ * (2 - s), A_TQ * (3 - s))
                dk_ref[s, :, sl] = dkc[rows]
                dv_ref[s, :, sl] = dvc[rows]

    share = pl.BlockSpec((3, A_TQ, 512), lambda i: (0, i, 0))
    return pl.pallas_call(
        body, name="attn_a_bwd", grid=(nb,),
        in_specs=[blk(0), blk(2), blk(1), blk(0), blk(2), blk(1), blk(0), _a_bias_spec(), blk(0)],
        out_specs=[pl.BlockSpec((A_TQ, 512), lambda i: (i, 0)), share, share,
                   pl.BlockSpec((HEADS, A_TQ, A_WIN), lambda i: (0, 0, 0))],
        out_shape=[jax.ShapeDtypeStruct((t, 512), F32), jax.ShapeDtypeStruct((3, t, 512), F32),
                   jax.ShapeDtypeStruct((3, t, 512), F32), jax.ShapeDtypeStruct((HEADS, A_TQ, A_WIN), F32)],
        compiler_params=_params(("arbitrary",)),
    )(qa, ka, ka, ka, va, va, va, bias, do)


B_T = 1024


B_SCALE2 = B_SCALE * 1.4426950408889634
_B_ALL = slice(0, B_T)
_B_LO, _B_HI = slice(0, B_T // 2), slice(B_T // 2, B_T)
_B_DIAG = ((_B_LO, _B_LO), (_B_LO, _B_HI), (_B_HI, _B_HI))


def _tri_tables(n, by_query):
    pairs = [(i, j) for i in range(n) for j in range(i + 1)] if by_query else [(i, j) for j in range(n) for i in range(j, n)]
    return (np.asarray([p[0] for p in pairs], np.int32), np.asarray([p[1] for p in pairs], np.int32))


def _b_mask_t(s):
    kc = lax.broadcasted_iota(jnp.int32, s.shape, 0) // CHUNK
    qc = lax.broadcasted_iota(jnp.int32, s.shape, 1) // CHUNK
    return jnp.where(kc <= qc, s, NEG)


def attn_b_fwd(qf, kf, vp):
    t = qf.shape[0]
    n = t // B_T
    qtab, ktab = _tri_tables(n, by_query=True)

    def body(qt_ref, kt_ref, q_ref, k_ref, v_ref, o_ref, lse_ref, m_s, l_s, acc_s):
        qb, kb = qt_ref[pl.program_id(1)], kt_ref[pl.program_id(1)]

        @pl.when(kb == 0)
        def _():
            m_s[...] = jnp.full_like(m_s, NEG)
            l_s[...] = jnp.zeros_like(l_s)
            acc_s[...] = jnp.zeros_like(acc_s)

        def block(kr, qr, masked):
            v = v_ref[kr, :]
            for h2 in range(2):
                sl = slice(LANES * h2, LANES * (h2 + 1))
                s = _dot_nt(k_ref[kr, sl], q_ref[qr, sl])
                if masked:
                    s = _b_mask_t(s)
                m_prev = m_s[h2, :, qr]
                m_new = jnp.maximum(m_prev, jnp.max(s, axis=0, keepdims=True))
                alpha = jnp.exp2(m_prev - m_new)
                pr = jnp.exp2(s - m_new)
                l_s[h2, :, qr] = alpha * l_s[h2, :, qr] + jnp.sum(pr, axis=0, keepdims=True)
                acc_s[h2, :, qr] = alpha * acc_s[h2, :, qr] + _dot_tn(v, pr.astype(BF16))
                m_s[h2, :, qr] = m_new

        @pl.when(kb < qb)
        def _():
            block(_B_ALL, _B_ALL, False)

        @pl.when(kb == qb)
        def _():
            for kr, qr in _B_DIAG:
                block(kr, qr, kr == qr)
            for h2 in range(2):
                l = l_s[h2]
                rows = slice(HD * h2, HD * (h2 + 1))
                o_ref[rows, :] = (acc_s[h2, rows, :] * (1.0 / l)).astype(BF16)
                lse_ref[0, h2:h2 + 1, :] = m_s[h2] + jnp.log2(l)

    grid_spec = pltpu.PrefetchScalarGridSpec(
        num_scalar_prefetch=2, grid=(4, len(qtab)),
        in_specs=[pl.BlockSpec((B_T, 256), lambda p, s, qt, kt: (qt[s], p)),
                  pl.BlockSpec((B_T, 256), lambda p, s, qt, kt: (kt[s], p)),
                  pl.BlockSpec((B_T, LANES), lambda p, s, qt, kt: (kt[s], p))],
        out_specs=[pl.BlockSpec((LANES, B_T), lambda p, s, qt, kt: (p, qt[s])),
                   pl.BlockSpec((1, 2, B_T), lambda p, s, qt, kt: (p, 0, qt[s]))],
        scratch_shapes=[pltpu.VMEM((2, 1, B_T), F32), pltpu.VMEM((2, 1, B_T), F32), pltpu.VMEM((2, LANES, B_T), F32)])
    return pl.pallas_call(
        body, name="attn_b_fwd", grid_spec=grid_spec,
        out_shape=[jax.ShapeDtypeStruct((512, t), BF16), jax.ShapeDtypeStruct((4, 2, t), F32)],
        compiler_params=_params(("parallel", "arbitrary")),
    )(jnp.asarray(qtab), jnp.asarray(ktab), qf, kf, vp)


def attn_b_bwd(qf, kf, vp, do, do_t, o_t, lse, scatter=()):
    t = qf.shape[0]
    n = t // B_T
    qtab, ktab = _tri_tables(n, by_query=False)
    plan = ScatterPlan(scatter)
    m = plan.n
    last = len(qtab) - 1

    def body(*refs):
        qt_ref, kt_ref, q_ref, k_ref, v_ref, do_ref, dot_ref, ot_ref, lse_ref = refs[:9]
        ins, (dq_ref, dk_ref, dv_ref), outs = refs[9:9 + m], refs[9 + m:12 + m], refs[12 + m:12 + 2 * m]
        sems = refs[12 + 2 * m:]
        qb, kb = qt_ref[pl.program_id(1)], kt_ref[pl.program_id(1)]
        if m:
            pl.when((pl.program_id(0) == 0) & (pl.program_id(1) == 0))(lambda: plan.start(ins, outs, sems))

        @pl.when(pl.program_id(1) == 0)
        def _():
            dq_ref[...] = jnp.zeros_like(dq_ref)

        @pl.when(qb == kb)
        def _():
            dk_ref[...] = jnp.zeros_like(dk_ref)
            dv_ref[...] = jnp.zeros_like(dv_ref)

        def block(kr, qr, masked):
            nq = qr.stop - qr.start
            cols = pl.ds(pl.multiple_of(qb * B_T + qr.start, LANES), nq)
            v = v_ref[kr, :]
            dov = do_ref[qr, :]
            prod = dot_ref[:, qr].astype(F32) * ot_ref[:, qr].astype(F32)
            lo = _lane((nq, LANES)) < HD
            for h2 in range(2):
                sl = slice(LANES * h2, LANES * (h2 + 1))
                hm = lo if h2 == 0 else ~lo
                q = q_ref[qr, sl]
                k = k_ref[kr, sl]
                dom = jnp.where(hm, dov, jnp.zeros((), BF16))
                delta = jnp.sum(prod[HD * h2:HD * (h2 + 1), :], axis=0, keepdims=True)
                s = _dot_nt(k, q)
                if masked:
                    s = _b_mask_t(s)
                pr = jnp.exp2(s - lse_ref[0, h2:h2 + 1, qr])
                dp = _dot_nt(v, dom)
                ds = (pr * (dp - delta)).astype(BF16)
                dk_ref[kr, sl] += _dot(ds, q) * (B_SCALE / B_SCALE2)
                dv_ref[kr, :] += _dot(pr.astype(BF16), dom)
                dq_ref[sl, cols] += _dot_tn(k, ds) * B_SCALE

        @pl.when(qb > kb)
        def _():
            block(_B_ALL, _B_ALL, False)

        @pl.when(qb == kb)
        def _():
            for kr, qr in _B_DIAG:
                block(kr, qr, kr == qr)

        if m:
            pl.when((pl.program_id(0) == 3) & (pl.program_id(1) == last))(lambda: plan.finish(ins, outs, sems))

    qrow = lambda w: pl.BlockSpec((B_T, w), lambda p, s, qt, kt: (qt[s], p))
    qcol = pl.BlockSpec((LANES, B_T), lambda p, s, qt, kt: (p, qt[s]))
    krow = lambda w: pl.BlockSpec((B_T, w), lambda p, s, qt, kt: (kt[s], p))
    grid_spec = pltpu.PrefetchScalarGridSpec(
        num_scalar_prefetch=2, grid=(4, len(qtab)),
        in_specs=[qrow(256), krow(256), krow(LANES), qrow(LANES), qcol, qcol,
                  pl.BlockSpec((1, 2, B_T), lambda p, s, qt, kt: (p, 0, qt[s]))] + [ANY] * m,
        out_specs=[pl.BlockSpec((256, t), lambda p, s, qt, kt: (p, 0)), krow(256), krow(LANES)] + [ANY] * m,
        scratch_shapes=plan.scratch if m else [])
    return pl.pallas_call(
        body, name="attn_b_bwd", grid_spec=grid_spec,
        out_shape=[jax.ShapeDtypeStruct((1024, t), F32), jax.ShapeDtypeStruct((t, 1024), F32),
                   jax.ShapeDtypeStruct((t, 512), F32)] + plan.out_shape,
        compiler_params=_params(("arbitrary", "arbitrary")),
    )(jnp.asarray(qtab), jnp.asarray(ktab), qf, kf, vp, do, do_t, o_t, lse, *scatter)


_U_LEN = A_TQ + A_WIN - 1


def _band_mask():
    a = np.arange(A_TQ)[:, None] // CHUNK
    b = np.arange(A_WIN)[None, :] // CHUNK
    return (b >= a) & (b <= a + A_LEFT)


def bias_block(table):
    h = table.shape[0]
    n_lo = A_WIN - 1 - 2 * A_TQ - A_MAX_REL
    ext = jnp.concatenate([jnp.repeat(table[:, :1], n_lo, axis=1), table,
                           jnp.repeat(table[:, -1:], _U_LEN - n_lo - table.shape[1], axis=1)], axis=1)
    row = jnp.pad(ext[:, ::-1], ((0, 0), (0, 1)))
    flat = jnp.tile(row, (1, A_TQ))[:, :A_TQ * _U_LEN]
    skew = flat.reshape(h, A_TQ, _U_LEN)
    toep = skew[:, :, A_TQ - 1:A_TQ - 1 + A_WIN]
    band = _band_mask()
    first = [band & (np.arange(A_WIN)[None, :] >= 2 * A_TQ - A_TQ * v) for v in range(3)]
    return jnp.where(jnp.asarray(np.stack(first))[:, None], toep[None], NEG)


def bias_block_grad(db):
    h = db.shape[0]
    n_lo = A_WIN - 1 - 2 * A_TQ - A_MAX_REL
    skew = jnp.pad(db, ((0, 0), (0, 0), (A_TQ - 1, 0)))
    flat = jnp.pad(skew.reshape(h, A_TQ * _U_LEN), ((0, 0), (0, A_TQ)))
    ext = jnp.sum(flat.reshape(h, A_TQ, _U_LEN + 1), axis=1)[:, :_U_LEN][:, ::-1]
    n_tab = 2 * A_MAX_REL + 1
    first = jnp.sum(ext[:, :n_lo + 1], axis=1, keepdims=True)
    last = jnp.sum(ext[:, n_lo + n_tab - 1:], axis=1, keepdims=True)
    return jnp.concatenate([first, ext[:, n_lo + 1:n_lo + n_tab - 1], last], axis=1)


def rope_tabs(t):
    inv = 1.0 / (10000.0 ** (jnp.arange(0, ROPE, 2, dtype=F32) / ROPE))
    ang = jnp.arange(t, dtype=F32)[:, None] * inv[None, :]
    cos, sin = jnp.cos(ang), jnp.sin(ang)
    z = lambda w: jnp.zeros((t, w), F32)
    ck = jnp.concatenate([cos, cos, z(96)], axis=1)
    s1k = jnp.concatenate([-sin, z(112)], axis=1)
    s2k = jnp.concatenate([z(16), sin, z(96)], axis=1)
    cq = jnp.concatenate([jnp.ones((t, HD), F32), cos, cos, z(32)], axis=1)
    s1q = jnp.concatenate([z(HD), -sin, z(48)], axis=1)
    s2q = jnp.concatenate([z(HD + 16), sin, z(32)], axis=1)
    return jnp.stack([cq, s1q, s2q, ck, s1k, s2k])


def _pad_lanes(v, width):
    return jnp.pad(v, ((0, 0), (0, width - v.shape[1])))


LATE = ("w_in", "b_w_uq", "b_w_ukv", "w_out", "ffn2_w_gate", "ffn2_w_up", "ffn2_w_down")
FFN2 = ("ffn2_w_gate", "ffn2_w_up", "ffn2_w_down")


def kernel_layout(gathered):
    w = {n: v for n, v in gathered.items() if n.startswith("ffn")}
    if "w_in" in gathered:
        w["w_in"] = jnp.pad(gathered["w_in"].reshape(IN_COLS, D_MODEL), ((0, PROJ_W - IN_COLS), (0, 0)))
        uq = gathered["b_w_uq"].reshape(HEADS, HD + ROPE, 256)
        w["b_w_uq"] = jnp.pad(uq, ((0, 0), (0, LANES - HD - ROPE), (0, 0))).reshape(HEADS * LANES, 256)
        w["b_w_ukv"] = _shards_to_cols(gathered["b_w_ukv"])
        w["w_out"] = gathered["w_out"].reshape(N_SHARD * gathered["w_out"].shape[1], D_MODEL)
    return w


def local_step(x, target, w, late=None):
    t = x.shape[0]
    gq = jnp.tile(w["a_q_norm"], (1, HEADS))
    gk = jnp.tile(w["a_k_norm"], (1, HEADS))
    gq128 = _pad_lanes(jnp.concatenate([w["b_q_nope_norm"], w["b_q_rope_norm"]], axis=1), LANES)
    gk128 = _pad_lanes(w["b_k_nope_norm"], LANES)
    gkr128 = _pad_lanes(w["b_k_rope_norm"], LANES)
    tabs = rope_tabs(t)
    bias = bias_block(w["a_rel_bias"])

    if late is None:
        x1, gate1, up1 = ffn_fwd(x, w["ffn1_norm"], w["ffn1_w_gate"], w["ffn1_w_up"], w["ffn1_w_down"], "ffn_fwd")
    else:
        own, shard = late
        x1, gate1, up1, *got = ffn_fwd(x, w["ffn1_norm"], w["ffn1_w_gate"], w["ffn1_w_up"], w["ffn1_w_down"],
                                       "ffn_fwd_gather", gather=own)
        w = dict(w, **kernel_layout({n: lax.dynamic_update_index_in_dim(g_, o_, shard, 0)
                                     for n, g_, o_ in zip(LATE, got, own)}))
    h, proj, qa, ka, va, cqn, ckvn = mix_fwd(x1, w["mix_norm"], w["w_in"], gq, gk,
                                             w["b_q_lat_norm"], w["b_kv_lat_norm"])
    qf, kf, vp = prep2_fwd(cqn, ckvn, proj, w["b_w_uq"], w["b_w_ukv"], gq128, gk128, gkr128, tabs)
    oa = attn_a_fwd(qa, ka, va, bias)
    ob_t, lse = attn_b_fwd(qf, kf, vp)
    x2 = out_proj(x1, oa, ob_t, w["w_out"])
    g = {}
    dx3, gate2, up2, g["final_norm"], loss = ffn_fwd(
        x2, w["ffn2_norm"], w["ffn2_w_gate"], w["ffn2_w_up"], w["ffn2_w_down"], "ffn_fwd_loss",
        loss_head=(w["final_norm"], target))
    g["ffn2_w_gate"], g["ffn2_w_up"], g["ffn2_w_down"], dhp = ffn_bwd(
        x2, dx3, w["ffn2_norm"], w["ffn2_w_gate"], w["ffn2_w_up"], w["ffn2_w_down"], gate2, up2, "ffn_bwd")
    dx2, g["ffn2_norm"], d_oa, d_ob, d_ob_t, g["w_out"] = out_proj_bwd(
        x2, w["ffn2_norm"], dhp, dx3, w["w_out"], oa, ob_t)
    early = [g[n] for n in FFN2] + [g["w_out"].reshape(N_SHARD, -1, D_MODEL)]
    dqf, dkf, dvp, *landed_early = attn_b_bwd(qf, kf, vp, d_ob, d_ob_t, ob_t, lse,
                                              scatter=() if late is None else early)
    dqa, dkp, dvpa, dbias = attn_a_bwd(qa, ka, va, bias, d_oa)
    dcq, dckv, dkr, dwuq, dwukv, dgq128, dgk128, dgkr128 = prep2_bwd(
        cqn, ckvn, proj, w["b_w_uq"], w["b_w_ukv"], dqf, dkf, dvp, gq128, gk128, gkr128, tabs)
    g["b_w_uq"], g["b_w_ukv"] = dwuq.astype(BF16), dwukv.astype(BF16)
    dx1, g["w_in"], g["mix_norm"], dgq, dgk, g["b_q_lat_norm"], g["b_kv_lat_norm"] = mix_bwd(
        proj, x1, h, dx2, w["w_in"], w["mix_norm"], dqa, dkp, dvpa, dcq, dckv, dkr, gq, gk,
        w["b_q_lat_norm"], w["b_kv_lat_norm"])
    mid = [g["w_in"][:IN_COLS].reshape(N_SHARD, IN_COLS // N_SHARD, D_MODEL),
           g["b_w_uq"].reshape(HEADS, LANES, 256)[:, :HD + ROPE].reshape(N_SHARD, -1, 256),
           _cols_to_shards(g["b_w_ukv"])]
    g["ffn1_w_gate"], g["ffn1_w_up"], g["ffn1_w_down"], dhp, *landed_late = ffn_bwd(
        x, dx1, w["ffn1_norm"], w["ffn1_w_gate"], w["ffn1_w_up"], w["ffn1_w_down"], gate1, up1,
        "ffn_bwd" if late is None else "ffn_bwd_scatter", scatter=() if late is None else mid,
        spread=None if late is None else late[1])
    grad_x, g["ffn1_norm"] = norm_bwd(x, w["ffn1_norm"], dhp, dx1, "ffn_norm_bwd")
    landed = dict(zip(FFN2 + ("w_out", "w_in", "b_w_uq", "b_w_ukv", "ffn1_w_gate", "ffn1_w_up", "ffn1_w_down"),
                      landed_early + landed_late))

    g["a_q_norm"] = jnp.sum(dgq.reshape(HEADS, HD), axis=0, keepdims=True)
    g["a_k_norm"] = jnp.sum(dgk.reshape(HEADS, HD), axis=0, keepdims=True)
    g["a_rel_bias"] = bias_block_grad(dbias)
    g["b_q_nope_norm"] = dgq128[:, :HD]
    g["b_q_rope_norm"] = dgq128[:, HD:HD + ROPE]
    g["b_k_nope_norm"] = dgk128[:, :HD]
    g["b_k_rope_norm"] = dgkr128[:, :ROPE]
    return loss, grad_x, g, landed


ANY = pl.BlockSpec(memory_space=pl.ANY)
N_DEV = 8


def _place():
    return lax.axis_index("x"), lax.axis_index("y"), lax.axis_index("c")


def _flip(v, bit):
    return 1 - v if bit else v


BF16_ROWS = 16


def _split_axis(shape):
    return 0 if (shape[0] // 2) % BF16_ROWS == 0 else 1


def _half_shape(shape):
    axis = _split_axis(shape)
    return tuple(s // 2 if a == axis else s for a, s in enumerate(shape))


def _half(shape, core):
    axis = _split_axis(shape)
    size = shape[axis] // 2
    return tuple(pl.ds(core * size, size) if a == axis else slice(None) for a in range(2))


class GatherPlan:
    def __init__(self, ws):
        self.shapes = [w.shape for w in ws]
        self.n = len(ws)
        self.out_shape = [jax.ShapeDtypeStruct((N_SHARD,) + w.shape, w.dtype) for w in ws]
        self.scratch = [pltpu.SemaphoreType.DMA((6 * self.n,)), pltpu.SemaphoreType.DMA((6 * self.n,))]

    def _copies(self, ins, outs, sems):
        x, y, c = _place()
        s_me = 2 * x + y
        sibling = (x, y, 1 - c)
        send_sems, recv_sems = sems

        def remote(k, src, dst, to):
            return pltpu.make_async_remote_copy(src_ref=src, dst_ref=dst, send_sem=send_sems.at[k],
                                                recv_sem=recv_sems.at[k], device_id=to, device_id_type=MESH)

        ici, fwd = [], []
        for a in range(self.n):
            mine, theirs = _half(self.shapes[a], c), _half(self.shapes[a], 1 - c)
            for j, (cx, cy) in enumerate([(1 - x, y), (x, 1 - y), (1 - x, 1 - y)]):
                got = outs[a].at[(2 * cx + cy,) + mine]
                ici.append((remote(6 * a + j, ins[a].at[mine], outs[a].at[(s_me,) + mine], (cx, cy, c)),
                            remote(6 * a + j, got, got, (cx, cy, c))))
                passed = outs[a].at[(2 * cx + cy,) + theirs]
                fwd.append((remote(6 * a + 3 + j, got, got, sibling), remote(6 * a + 3 + j, passed, passed, sibling)))
        return ici, fwd

    def start(self, ins, outs, sems):
        for send, _ in self._copies(ins, outs, sems)[0]:
            send.start()

    def forward(self, ins, outs, sems):
        ici, fwd = self._copies(ins, outs, sems)
        for (_, arrival), (send, _) in zip(ici, fwd):
            arrival.wait_recv()
            send.start()

    def finish(self, ins, outs, sems):
        ici, fwd = self._copies(ins, outs, sems)
        for _, arrival in fwd:
            arrival.wait_recv()
        for send, _ in ici + fwd:
            send.wait_send()


def allgather_shards(ws):
    plan = GatherPlan(ws)
    n = plan.n

    def body(*refs):
        ins, outs, sems = refs[:n], refs[n:2 * n], refs[2 * n:]
        plan.start(ins, outs, sems)
        plan.forward(ins, outs, sems)
        plan.finish(ins, outs, sems)

    return pl.pallas_call(
        body, name="allgather_shards", in_specs=[ANY] * n, out_specs=[ANY] * n,
        out_shape=plan.out_shape, scratch_shapes=plan.scratch,
    )(*ws)


class ScatterPlan:
    def __init__(self, gs):
        self.shapes = [g.shape[1:] for g in gs]
        self.n = len(gs)
        self.out_shape = [jax.ShapeDtypeStruct((N_DEV,) + _half_shape(g.shape[1:]), g.dtype) for g in gs]
        self.scratch = [pltpu.SemaphoreType.DMA((7 * self.n,)), pltpu.SemaphoreType.DMA((7 * self.n,)),
                        pltpu.SemaphoreType.DMA((self.n,))]

    def _copies(self, ins, outs, sems):
        x, y, c = _place()
        me = 4 * x + 2 * y + c
        send_sems, recv_sems, local_sems = sems
        local, sends, arrivals = [], [], []
        for a in range(self.n):
            piece = lambda px, py, pc, a=a: ins[a].at[(2 * px + py,) + _half(self.shapes[a], pc)]
            local.append(pltpu.make_async_copy(piece(x, y, c), outs[a].at[me], local_sems.at[a]))
            for k in range(1, N_DEV):
                px, py, pc = _flip(x, k & 4), _flip(y, k & 2), _flip(c, k & 1)
                sem = dict(send_sem=send_sems.at[7 * a + k - 1], recv_sem=recv_sems.at[7 * a + k - 1],
                           device_id=(px, py, pc), device_id_type=MESH)
                sends.append(pltpu.make_async_remote_copy(
                    src_ref=piece(px, py, pc), dst_ref=outs[a].at[me], **sem))
                slot = outs[a].at[4 * px + 2 * py + pc]
                arrivals.append(pltpu.make_async_remote_copy(src_ref=slot, dst_ref=slot, **sem))
        return local, sends, arrivals

    def start(self, ins, outs, sems):
        local, sends, _ = self._copies(ins, outs, sems)
        for cp in local + sends:
            cp.start()

    def finish(self, ins, outs, sems):
        local, sends, arrivals = self._copies(ins, outs, sems)
        for cp in arrivals:
            cp.wait_recv()
        for cp in sends:
            cp.wait_send()
        for cp in local:
            cp.wait()


def sum_slots(land, name):
    _, rows, cols = land.shape
    tr = rows // 2 if rows > 128 and (rows // 2) % BF16_ROWS == 0 else rows

    def body(l_ref, o_ref):
        acc = l_ref[0].astype(F32)
        for s in range(1, N_DEV):
            acc = acc + l_ref[s].astype(F32)
        o_ref[...] = acc

    return pl.pallas_call(
        body, name=name, grid=(rows // tr,),
        in_specs=[pl.BlockSpec((N_DEV, tr, cols), lambda i: (0, i, 0))],
        out_specs=pl.BlockSpec((tr, cols), lambda i: (i, 0)),
        out_shape=jax.ShapeDtypeStruct((rows, cols), F32),
        compiler_params=_params(("parallel",)),
    )(land)


def join_halves(hs, shapes):
    n = len(hs)

    def body(*refs):
        ins, outs = refs[:n], refs[n:2 * n]
        send_sems, recv_sems = refs[2 * n:]
        x, y, c = _place()
        sends = []
        for a in range(n):
            mine = outs[a].at[_half(shapes[a], c)]
            sends.append(pltpu.make_async_remote_copy(
                src_ref=ins[a], dst_ref=mine, send_sem=send_sems.at[a], recv_sem=recv_sems.at[a],
                device_id=(x, y, 1 - c), device_id_type=MESH))
            sends[-1].start()
        for a in range(n):
            theirs = outs[a].at[_half(shapes[a], 1 - c)]
            pltpu.make_async_remote_copy(
                src_ref=theirs, dst_ref=theirs, send_sem=send_sems.at[a], recv_sem=recv_sems.at[a],
                device_id=(x, y, 1 - c), device_id_type=MESH).wait_recv()
        for cp in sends:
            cp.wait_send()

    return pl.pallas_call(
        body, name="join_halves",
        in_specs=[ANY] * n, out_specs=[ANY] * n,
        out_shape=[jax.ShapeDtypeStruct(tuple(s), h.dtype) for s, h in zip(shapes, hs)],
        scratch_shapes=[pltpu.SemaphoreType.DMA((n,)), pltpu.SemaphoreType.DMA((n,))],
    )(*hs)


def allreduce_small(vec):
    def body(v_ref, o_ref, land_ref, send_sems, recv_sems):
        x, y, c = _place()
        me = 4 * x + 2 * y + c
        land_ref[me] = v_ref[...]
        sends = []
        for k in range(1, N_DEV):
            px, py, pc = _flip(x, k & 4), _flip(y, k & 2), _flip(c, k & 1)
            sends.append(pltpu.make_async_remote_copy(
                src_ref=v_ref, dst_ref=land_ref.at[me], send_sem=send_sems.at[k - 1], recv_sem=recv_sems.at[k - 1],
                device_id=(px, py, pc), device_id_type=MESH))
            sends[-1].start()
        for k in range(1, N_DEV):
            px, py, pc = _flip(x, k & 4), _flip(y, k & 2), _flip(c, k & 1)
            slot = land_ref.at[4 * px + 2 * py + pc]
            pltpu.make_async_remote_copy(
                src_ref=slot, dst_ref=slot, send_sem=send_sems.at[k - 1], recv_sem=recv_sems.at[k - 1],
                device_id=(px, py, pc), device_id_type=MESH).wait_recv()
        for cp in sends:
            cp.wait_send()
        acc = land_ref[0]
        for s in range(1, N_DEV):
            acc = acc + land_ref[s]
        o_ref[...] = acc

    vm = pl.BlockSpec(memory_space=pltpu.VMEM)
    return pl.pallas_call(
        body, name="allreduce_small",
        in_specs=[vm], out_specs=vm,
        out_shape=jax.ShapeDtypeStruct(vec.shape, F32),
        scratch_shapes=[pltpu.VMEM((N_DEV,) + vec.shape, F32), pltpu.SemaphoreType.DMA((N_DEV - 1,)),
                        pltpu.SemaphoreType.DMA((N_DEV - 1,))],
    )(vec)


def adamw(w, g, m, v, name):
    rows, cols = w.shape
    tr = rows
    while tr * cols * 4 * 14 > 24 * 1024 * 1024 and tr % 16 == 0:
        tr //= 2
    c1 = 1.0 - ADAM_B1 ** ADAM_STEP
    c2 = 1.0 - ADAM_B2 ** ADAM_STEP

    def body(w_ref, g_ref, m_ref, v_ref, d_ref, nm_ref, nv_ref):
        gv = g_ref[...]
        nm = ADAM_B1 * m_ref[...] + (1.0 - ADAM_B1) * gv
        nv = ADAM_B2 * v_ref[...] + (1.0 - ADAM_B2) * (gv * gv)
        nm_ref[...] = nm
        nv_ref[...] = nv
        d_ref[...] = -ADAM_LR * ((nm / c1) / (jnp.sqrt(nv / c2) + ADAM_EPS) + ADAM_WD * w_ref[...])

    blk = pl.BlockSpec((tr, cols), lambda i: (i, 0))
    return pl.pallas_call(
        body, name=name, grid=(rows // tr,),
        in_specs=[blk] * 4, out_specs=[blk] * 3,
        out_shape=[jax.ShapeDtypeStruct((rows, cols), F32)] * 3,
        compiler_params=_params(("parallel",)),
    )(w, g, m, v)


BIG = ("ffn1_w_gate", "ffn1_w_up", "ffn1_w_down", "w_in", "b_w_uq", "b_w_ukv", "w_out",
       "ffn2_w_gate", "ffn2_w_up", "ffn2_w_down")
SMALL = ("ffn1_norm", "mix_norm", "a_q_norm", "a_k_norm", "a_rel_bias", "b_q_lat_norm", "b_kv_lat_norm",
         "b_q_nope_norm", "b_q_rope_norm", "b_k_nope_norm", "b_k_rope_norm", "ffn2_norm", "final_norm")
WEIGHTS = ("ffn1_norm", "ffn1_w_gate", "ffn1_w_up", "ffn1_w_down", "mix_norm", "w_in", "a_q_norm", "a_k_norm",
           "a_rel_bias", "b_q_lat_norm", "b_w_uq", "b_kv_lat_norm", "b_w_ukv", "b_q_nope_norm", "b_q_rope_norm",
           "b_k_nope_norm", "b_k_rope_norm", "w_out", "ffn2_norm", "ffn2_w_gate", "ffn2_w_up", "ffn2_w_down",
           "final_norm")
TRANSPOSED = ("ffn1_w_gate", "ffn1_w_up", "ffn2_w_gate", "ffn2_w_up", "w_in", "b_w_uq")
PACK_SHAPE = (8, 1024)


def _pack_small(d, last=None):
    flat = [d[n].reshape(-1) for n in SMALL]
    used = sum(f.shape[0] for f in flat)
    total = PACK_SHAPE[0] * PACK_SHAPE[1]
    tail = jnp.zeros((total - used - 1,), F32)
    end = jnp.zeros((1,), F32) if last is None else last.reshape(1)
    return jnp.concatenate(flat + [tail, end]).reshape(PACK_SHAPE)


def _unpack_small(p, like):
    flat = p.reshape(-1)
    out, off = {}, 0
    for n in SMALL:
        size = like[n].size
        out[n] = flat[off:off + size].reshape(like[n].shape)
        off += size
    return out, flat[-1]


def _cols_to_shards(g):
    rows, cols = g.shape
    return g.reshape(rows, N_SHARD, cols // N_SHARD).transpose(1, 0, 2)


def _shards_to_cols(g):
    return g.transpose(1, 0, 2).reshape(g.shape[1], -1)


def kernel(x, ffn1_norm, ffn1_w_gate, ffn1_w_up, ffn1_w_down, mix_norm, w_in, a_q_norm, a_k_norm, a_rel_bias, b_q_lat_norm, b_w_uq, b_kv_lat_norm, b_w_ukv, b_q_nope_norm, b_q_rope_norm, b_k_nope_norm, b_k_rope_norm, w_out, ffn2_norm, ffn2_w_gate, ffn2_w_up, ffn2_w_down, final_norm, loss_target, m_ffn1_norm, m_ffn1_w_gate, m_ffn1_w_up, m_ffn1_w_down, m_mix_norm, m_w_in, m_a_q_norm, m_a_k_norm, m_a_rel_bias, m_b_q_lat_norm, m_b_w_uq, m_b_kv_lat_norm, m_b_w_ukv, m_b_q_nope_norm, m_b_q_rope_norm, m_b_k_nope_norm, m_b_k_rope_norm, m_w_out, m_ffn2_norm, m_ffn2_w_gate, m_ffn2_w_up, m_ffn2_w_down, m_final_norm, v_ffn1_norm, v_ffn1_w_gate, v_ffn1_w_up, v_ffn1_w_down, v_mix_norm, v_w_in, v_a_q_norm, v_a_k_norm, v_a_rel_bias, v_b_q_lat_norm, v_b_w_uq, v_b_kv_lat_norm, v_b_w_ukv, v_b_q_nope_norm, v_b_q_rope_norm, v_b_k_nope_norm, v_b_k_rope_norm, v_w_out, v_ffn2_norm, v_ffn2_w_gate, v_ffn2_w_up, v_ffn2_w_down, v_final_norm):
    args = locals()
    view = lambda a, n: a[0].T if n in TRANSPOSED else a[0]
    wts = {n: view(args[n], n) for n in WEIGHTS}
    mom = {n: view(args["m_" + n], n) for n in WEIGHTS}
    var = {n: view(args["v_" + n], n) for n in WEIGHTS}

    shard = 2 * lax.axis_index("x") + lax.axis_index("y")
    core = lax.axis_index("c")
    first = [n for n in BIG if n not in LATE]
    own = [wts[n].astype(BF16) for n in first]
    w = {n: wts[n] if n == "a_rel_bias" else wts[n][None] for n in SMALL}
    w.update(kernel_layout({n: lax.dynamic_update_index_in_dim(got, mine, shard, 0)
                            for n, got, mine in zip(first, allgather_shards(own), own)}))

    loss, grad_x, g, landed = local_step(x[0], loss_target[0], w,
                                         late=([wts[n].astype(BF16) for n in LATE], shard))

    me = 2 * shard + core
    for n in first:
        piece = lax.dynamic_slice(g[n], (shard, core * (FS // 2), 0), (1, FS // 2, D_MODEL))
        landed[n] = lax.dynamic_update_slice(landed[n], piece, (me, 0, 0))
    halves = [sum_slots(landed[n], "sum_slots") for n in BIG]
    shapes = [wts[n].shape for n in BIG]
    axes = [_split_axis(s) for s in shapes]
    grads = dict(zip(BIG, (lax.dynamic_update_slice_in_dim(got, mine, core * mine.shape[ax], ax)
                           for got, mine, ax in zip(join_halves(halves, shapes), halves, axes))))

    small_sum, loss_sum = _unpack_small(allreduce_small(_pack_small(g, loss[0, 0])), wts)
    grads.update(small_sum)

    delta, new_m, new_v = {}, {}, {}
    for n in BIG:
        delta[n], new_m[n], new_v[n] = adamw(wts[n], grads[n], mom[n], var[n], "adamw")
    packed = adamw(_pack_small(wts), _pack_small(grads), _pack_small(mom), _pack_small(var), "adamw_small")
    for dst, p in zip((delta, new_m, new_v), packed):
        dst.update(_unpack_small(p, wts)[0])

    lead = lambda d: [(d[n].T if n in TRANSPOSED else d[n])[None] for n in WEIGHTS]
    return (loss_sum, grad_x[None], *lead(grads), *lead(delta), *lead(new_m), *lead(new_v))
```

```python
import numpy as np
import jax
import jax.numpy as jnp
from jax import lax
from jax.experimental import pallas as pl
from jax.experimental.pallas import tpu as pltpu

F32 = jnp.float32
BF16 = jnp.bfloat16
EPS = 1e-6
NEG = -1e30

D_MODEL = 1024
D_FF = 2816
N_SHARD = 4
FS = D_FF // N_SHARD
CHUNK = 64
A_LEFT = 8
A_MAX_REL = 128
HEADS = 8
HD = 64
ROPE = 32
PROJ_W = 2048
IN_COLS = 1952
B_SCALE = 96 ** -0.5
LANES = 128

ADAM_LR = 0.001
ADAM_B1 = 0.9
ADAM_B2 = 0.999
ADAM_EPS = 1e-08
ADAM_WD = 0.01
ADAM_STEP = 10

VMEM_LIMIT = 56 * 1024 * 1024

MESH = pl.DeviceIdType.MESH


def _dot(a, b):
    return lax.dot_general(a, b, (((1,), (0,)), ((), ())), preferred_element_type=F32)


def _dot_nt(a, b):
    return lax.dot_general(a, b, (((1,), (1,)), ((), ())), preferred_element_type=F32)


def _dot_tn(a, b):
    return lax.dot_general(a, b, (((0,), (0,)), ((), ())), preferred_element_type=F32)


def _params(sem):
    return pltpu.CompilerParams(dimension_semantics=sem, vmem_limit_bytes=VMEM_LIMIT)


def _rms(xv):
    r = lax.rsqrt(jnp.mean(xv * xv, axis=-1, keepdims=True) + EPS)
    return r, xv * r


def ffn_fwd(x, g, wg, wu, wd, name, gather=(), loss_head=None):
    t, d = x.shape
    tm = 512
    ni = t // tm
    plan = GatherPlan(gather)
    n = plan.n
    head = () if loss_head is None else tuple(loss_head)
    q = len(head)

    def body(*refs):
        x_ref, g_ref, wg_ref, wu_ref, wd_ref = refs[:5]
        head_in, ins = refs[5:5 + q], refs[5 + q:5 + q + n]
        o_ref, gp_ref, up_ref = refs[5 + q + n:8 + q + n]
        head_out, outs = refs[8 + q + n:8 + 2 * q + n], refs[8 + 2 * q + n:8 + 2 * q + 2 * n]
        h_ref, acc_ref = refs[8 + 2 * q + 2 * n:10 + 2 * q + 2 * n]
        sems = refs[10 + 2 * q + 2 * n:]
        i, j = pl.program_id(0), pl.program_id(1)
        if n:
            pl.when((i == 0) & (j == 0))(lambda: plan.start(ins, outs, sems))
            pl.when((i == (3 * ni) // 4) & (j == 0))(lambda: plan.forward(ins, outs, sems))

        @pl.when(j == 0)
        def _():
            _, xn = _rms(x_ref[...])
            h_ref[...] = (xn * g_ref[...]).astype(BF16)
            acc_ref[...] = jnp.zeros_like(acc_ref)

        h = h_ref[...]
        gp = _dot_nt(h, wg_ref[0])
        up = _dot_nt(h, wu_ref[0])
        gp_ref[0] = gp
        up_ref[0] = up
        a = (gp * jax.nn.sigmoid(gp) * up).astype(BF16)
        acc_ref[...] += _dot(a, wd_ref[0])

        @pl.when(j == N_SHARD - 1)
        def _():
            y = x_ref[...] + 0.5 * acc_ref[...]
            if not q:
                o_ref[...] = y
                return
            (gf_ref, t_ref), (dgf_ref, loss_ref) = head_in, head_out
            r, yn = _rms(y)
            gf = gf_ref[...]
            e = yn * gf - t_ref[...]
            dout = e * (1.0 / d)
            dng = dout * gf
            o_ref[...] = r * (dng - yn * jnp.mean(dng * yn, axis=-1, keepdims=True))

            @pl.when(i == 0)
            def _():
                dgf_ref[...] = jnp.zeros_like(dgf_ref)
                loss_ref[...] = jnp.zeros_like(loss_ref)

            dgf_ref[...] += jnp.sum(dout * yn, axis=0, keepdims=True)
            part = jnp.sum(jnp.sum(e * e, axis=-1, keepdims=True), axis=0, keepdims=True) * (0.5 / d)
            loss_ref[...] += jnp.broadcast_to(part, loss_ref.shape)

        if n:
            pl.when((i == ni - 1) & (j == N_SHARD - 1))(lambda: plan.finish(ins, outs, sems))

    tok = pl.BlockSpec((tm, d), lambda i, j: (i, 0))
    vec = pl.BlockSpec((1, d), lambda i, j: (0, 0))
    chunk = pl.BlockSpec((1, FS, d), lambda i, j: (j, 0, 0))
    pre = pl.BlockSpec((1, tm, FS), lambda i, j: (j, i, 0))
    return pl.pallas_call(
        body, name=name, grid=(ni, N_SHARD),
        in_specs=[tok, vec, chunk, chunk, chunk] + [vec, tok][:q] + [ANY] * n,
        out_specs=[tok, pre, pre] + [vec, pl.BlockSpec((1, LANES), lambda i, j: (0, 0))][:q] + [ANY] * n,
        out_shape=[jax.ShapeDtypeStruct((t, d), F32), jax.ShapeDtypeStruct((N_SHARD, t, FS), F32),
                   jax.ShapeDtypeStruct((N_SHARD, t, FS), F32)]
        + [jax.ShapeDtypeStruct((1, d), F32), jax.ShapeDtypeStruct((1, LANES), F32)][:q] + plan.out_shape,
        scratch_shapes=[pltpu.VMEM((tm, d), BF16), pltpu.VMEM((tm, d), F32)] + (plan.scratch if n else []),
        compiler_params=_params(("arbitrary", "arbitrary")),
    )(x, g, wg, wu, wd, *head, *gather)


def ffn_bwd(x, dout, g, wg, wu, wd, gate, up_pre, name, scatter=(), spread=None):
    t, d = x.shape
    tm = 512
    ni = t // tm
    hf = FS // 2
    plan = ScatterPlan(scatter)
    m = plan.n
    k = 0 if spread is None else 3
    steps = jnp.arange(N_SHARD, dtype=jnp.int32)
    order = steps if spread is None else (spread + 1 + steps) % N_SHARD

    def body(*refs):
        ord_ref, x_ref, do_ref, g_ref, wg_ref, wu_ref, wd_ref, gp_ref, up_ref = refs[:9]
        ins, (dwg_out, dwu_out, dwd_out, dhp_ref) = refs[9:9 + m], refs[9 + m:13 + m]
        outs, lands = refs[13 + m:13 + 2 * m], refs[13 + 2 * m:13 + 2 * m + k]
        dwg_ref, dwu_ref, dwd_ref = refs[13 + 2 * m + k:16 + 2 * m + k]
        sems = refs[16 + 2 * m + k:19 + 2 * m + k] if m else ()
        stage_ref = refs[-3] if k else None
        j, i = pl.program_id(0), pl.program_id(1)
        if m:
            pl.when((j == 0) & (i == 0))(lambda: plan.start(ins, outs, sems))

        def chunk_copies(jj):
            send_sems, recv_sems = refs[-2:]
            px, py, pc = _place()
            me = 4 * px + 2 * py + pc
            tx, ty = ord_ref[jj] // 2, ord_ref[jj] % 2
            copies = []
            for n_ in range(3):
                for h_ in range(2):
                    copies.append((pltpu.make_async_remote_copy(
                        src_ref=stage_ref.at[n_, pl.ds(h_ * hf, hf)], dst_ref=lands[n_].at[me],
                        send_sem=send_sems.at[6 * jj + 2 * n_ + h_], recv_sem=recv_sems.at[3 * me + n_],
                        device_id=(tx, ty, h_), device_id_type=MESH), (tx != px) | (ty != py) | (pc != h_)))
            return copies

        def arrivals():
            send_sems, recv_sems = refs[-2:]
            px, py, pc = _place()
            me = 4 * px + 2 * py + pc
            for s_ in range(N_DEV):
                for n_ in range(3):
                    slot = lands[n_].at[s_]
                    cp = pltpu.make_async_remote_copy(
                        src_ref=slot, dst_ref=slot, send_sem=send_sems.at[0], recv_sem=recv_sems.at[3 * s_ + n_],
                        device_id=(px, py, pc), device_id_type=MESH)
                    pl.when(me != s_)(cp.wait_recv)

        _, xn = _rms(x_ref[...])
        h = (xn * g_ref[...]).astype(BF16)
        dz = (0.5 * do_ref[...]).astype(BF16)
        wgv, wuv, wdv = wg_ref[0], wu_ref[0], wd_ref[0]
        gp, up = gp_ref[0], up_ref[0]
        s = jax.nn.sigmoid(gp)
        sg = gp * s
        a = (sg * up).astype(BF16)
        da = _dot_nt(dz, wdv)
        dup = (da * sg).astype(BF16)
        dgp = (da * up * (s * (1.0 + gp * (1.0 - s)))).astype(BF16)

        @pl.when(i == 0)
        def _():
            dwg_ref[...] = jnp.zeros_like(dwg_ref)
            dwu_ref[...] = jnp.zeros_like(dwu_ref)
            dwd_ref[...] = jnp.zeros_like(dwd_ref)

        dwd_ref[...] += _dot_tn(a, dz)
        dwg_ref[...] += _dot_tn(dgp, h)
        dwu_ref[...] += _dot_tn(dup, h)
        dhp_ref[0] = (_dot(dgp, wgv) + _dot(dup, wuv)).astype(BF16)

        @pl.when(i == ni - 1)
        def _():
            dwg_out[0] = dwg_ref[...].astype(BF16)
            dwu_out[0] = dwu_ref[...].astype(BF16)
            dwd_out[0] = dwd_ref[...].astype(BF16)
            if k:
                @pl.when(j >= 1)
                def _():
                    for cp, leaves in chunk_copies(j - 1):
                        pl.when(leaves)(cp.wait_send)
                for n_, acc in enumerate((dwg_ref, dwu_ref, dwd_ref)):
                    stage_ref[n_] = acc[...].astype(BF16)
                for cp, leaves in chunk_copies(j):
                    pl.when(leaves)(cp.start)

                @pl.when(j == N_SHARD - 1)
                def _():
                    for cp, leaves in chunk_copies(N_SHARD - 1):
                        pl.when(leaves)(cp.wait_send)
                    arrivals()

        if m:
            pl.when((j == N_SHARD - 1) & (i == ni - 1))(lambda: plan.finish(ins, outs, sems))

    chunk = pl.BlockSpec((1, FS, d), lambda j, i, o: (o[j], 0, 0))
    tok = pl.BlockSpec((tm, d), lambda j, i, o: (i, 0))
    pre = pl.BlockSpec((1, tm, FS), lambda j, i, o: (o[j], i, 0))
    grid_spec = pltpu.PrefetchScalarGridSpec(
        num_scalar_prefetch=1, grid=(N_SHARD, ni),
        in_specs=[tok, tok, pl.BlockSpec((1, d), lambda j, i, o: (0, 0)), chunk, chunk, chunk, pre, pre] + [ANY] * m,
        out_specs=[chunk, chunk, chunk, pl.BlockSpec((1, tm, d), lambda j, i, o: (o[j], i, 0))] + [ANY] * (m + k),
        scratch_shapes=[pltpu.VMEM((FS, d), F32), pltpu.VMEM((FS, d), F32), pltpu.VMEM((FS, d), F32)]
        + (plan.scratch if m else [])
        + ([pltpu.VMEM((3, FS, d), BF16), pltpu.SemaphoreType.DMA((6 * N_SHARD,)),
            pltpu.SemaphoreType.DMA((3 * N_DEV,))] if k else []))
    return pl.pallas_call(
        body, name=name, grid_spec=grid_spec,
        out_shape=[jax.ShapeDtypeStruct((N_SHARD, FS, d), BF16),
                   jax.ShapeDtypeStruct((N_SHARD, FS, d), BF16),
                   jax.ShapeDtypeStruct((N_SHARD, FS, d), BF16),
                   jax.ShapeDtypeStruct((N_SHARD, t, d), BF16)] + plan.out_shape
        + [jax.ShapeDtypeStruct((N_DEV, hf, d), BF16)] * k,
        compiler_params=_params(("arbitrary", "arbitrary")),
    )(order, x, dout, g, wg, wu, wd, gate, up_pre, *scatter)


def norm_bwd(x, g, dhp, dres, name):
    t, d = x.shape
    p = dhp.shape[0]
    tm = 512

    def body(x_ref, g_ref, dhp_ref, dres_ref, dx_ref, dg_ref):
        i = pl.program_id(0)
        r, xn = _rms(x_ref[...])
        dh = dhp_ref[0].astype(F32)
        for q in range(1, p):
            dh = dh + dhp_ref[q].astype(F32)
        dhg = dh * g_ref[...]
        dx_ref[...] = dres_ref[...] + r * (dhg - xn * jnp.mean(dhg * xn, axis=-1, keepdims=True))

        @pl.when(i == 0)
        def _():
            dg_ref[...] = jnp.zeros_like(dg_ref)

        dg_ref[...] += jnp.sum(dh * xn, axis=0, keepdims=True)

    return pl.pallas_call(
        body, name=name, grid=(t // tm,),
        in_specs=[pl.BlockSpec((tm, d), lambda i: (i, 0)),
                  pl.BlockSpec((1, d), lambda i: (0, 0)),
                  pl.BlockSpec((p, tm, d), lambda i: (0, i, 0)),
                  pl.BlockSpec((tm, d), lambda i: (i, 0))],
        out_specs=[pl.BlockSpec((tm, d), lambda i: (i, 0)),
                   pl.BlockSpec((1, d), lambda i: (0, 0))],
        out_shape=[jax.ShapeDtypeStruct((t, d), F32), jax.ShapeDtypeStruct((1, d), F32)],
        compiler_params=_params(("arbitrary",)),
    )(x, g, dhp, dres)


def out_proj(x, oa, ob_t, w):
    t, d = x.shape
    half = oa.shape[1]
    tm = 512

    def body(x_ref, oa_ref, obt_ref, w_ref, o_ref):
        o_ref[...] = (x_ref[...] + _dot(oa_ref[...], w_ref[0:half, :])
                      + _dot_tn(obt_ref[...], w_ref[half:2 * half, :]))

    return pl.pallas_call(
        body, name="out_proj", grid=(t // tm,),
        in_specs=[pl.BlockSpec((tm, d), lambda i: (i, 0)),
                  pl.BlockSpec((tm, half), lambda i: (i, 0)),
                  pl.BlockSpec((half, tm), lambda i: (0, i)),
                  pl.BlockSpec((2 * half, d), lambda i: (0, 0))],
        out_specs=pl.BlockSpec((tm, d), lambda i: (i, 0)),
        out_shape=jax.ShapeDtypeStruct((t, d), F32),
        compiler_params=_params(("parallel",)),
    )(x, oa, ob_t, w)


def out_proj_bwd(x, g, dhp, dres, w, oa, ob_t):
    t, d = x.shape
    half = w.shape[0] // 2
    p = dhp.shape[0]
    tm = 512
    ni = t // tm

    def body(x_ref, g_ref, dhp_ref, dres_ref, w_ref, oa_ref, obt_ref,
             dx_ref, dg_ref, da_ref, db_ref, dbt_ref, dw_ref, acc_ref):
        i = pl.program_id(0)

        @pl.when(i == 0)
        def _():
            dg_ref[...] = jnp.zeros_like(dg_ref)
            acc_ref[...] = jnp.zeros_like(acc_ref)

        r, xn = _rms(x_ref[...])
        dh = dhp_ref[0].astype(F32)
        for s in range(1, p):
            dh = dh + dhp_ref[s].astype(F32)
        dhg = dh * g_ref[...]
        dx = dres_ref[...] + r * (dhg - xn * jnp.mean(dhg * xn, axis=-1, keepdims=True))
        dx_ref[...] = dx
        dg_ref[...] += jnp.sum(dh * xn, axis=0, keepdims=True)
        dxb = dx.astype(BF16)
        da_ref[...] = _dot_nt(dxb, w_ref[0:half, :]).astype(BF16)
        db_ref[...] = _dot_nt(dxb, w_ref[half:2 * half, :]).astype(BF16)
        dbt_ref[...] = _dot_nt(w_ref[half:2 * half, :], dxb).astype(BF16)
        acc_ref[0:half, :] += _dot_tn(oa_ref[...], dxb)
        acc_ref[half:2 * half, :] += _dot(obt_ref[...], dxb)

        @pl.when(i == ni - 1)
        def _():
            dw_ref[...] = acc_ref[...].astype(BF16)

    row = lambda w_: pl.BlockSpec((tm, w_), lambda i: (i, 0))
    col = pl.BlockSpec((half, tm), lambda i: (0, i))
    whole = pl.BlockSpec((2 * half, d), lambda i: (0, 0))
    vec = pl.BlockSpec((1, d), lambda i: (0, 0))
    return pl.pallas_call(
        body, name="out_proj_bwd", grid=(ni,),
        in_specs=[row(d), vec, pl.BlockSpec((p, tm, d), lambda i: (0, i, 0)), row(d), whole, row(half), col],
        out_specs=[row(d), vec, row(half), row(half), col, whole],
        out_shape=[jax.ShapeDtypeStruct((t, d), F32), jax.ShapeDtypeStruct((1, d), F32),
                   jax.ShapeDtypeStruct((t, half), BF16), jax.ShapeDtypeStruct((t, half), BF16),
                   jax.ShapeDtypeStruct((half, t), BF16), jax.ShapeDtypeStruct((2 * half, d), BF16)],
        scratch_shapes=[pltpu.VMEM((2 * half, d), F32)],
        compiler_params=_params(("arbitrary",)),
    )(x, g, dhp, dres, w, oa, ob_t)


def _lane(shape):
    return lax.broadcasted_iota(jnp.int32, shape, 1)


PAIR = (0, HD, LANES)
Q_HEAD = (0, HD, HD + ROPE, LANES)
K_ROPE = (0, ROPE, LANES)


def _seg_mean(z, bounds):
    seg = lambda v: sum([(v >= b).astype(jnp.int32) for b in bounds[1:-1]], jnp.zeros_like(v))
    rows = seg(lax.broadcasted_iota(jnp.int32, (LANES, LANES), 0))
    cols = seg(lax.broadcasted_iota(jnp.int32, (LANES, LANES), 1))
    same = (rows == cols).astype(BF16)
    lane = _lane((1, LANES))
    inv = sum([jnp.where((lane >= a) & (lane < b), 1.0 / (b - a), 0.0) for a, b in zip(bounds[:-1], bounds[1:])])
    hi = z.astype(BF16)
    lo = (z - hi.astype(F32)).astype(BF16)
    return (_dot(hi, same) + _dot(lo, same)) * inv


def _seg_norm(x, bounds):
    r = lax.rsqrt(_seg_mean(x * x, bounds) + EPS)
    return r, x * r


def _seg_norm_bwd(r, xn, dyg, bounds):
    return r * (dyg - xn * _seg_mean(dyg * xn, bounds))


A_TM = 256


def mix_fwd(x, g, w, gq, gk, gcq, gckv):
    t, d = x.shape
    tm = A_TM

    def body(x_ref, g_ref, w_ref, gq_ref, gk_ref, gcq_ref, gckv_ref,
             h_ref, p_ref, qa_ref, ka_ref, va_ref, cq_ref, ckv_ref):
        _, xn = _rms(x_ref[...])
        h = (xn * g_ref[...]).astype(BF16)
        h_ref[...] = h
        p_ref[...] = _dot_nt(h, w_ref[...])
        for p in range(4):
            sl = slice(LANES * p, LANES * (p + 1))
            _, xn = _seg_norm(p_ref[:, sl], PAIR)
            qa_ref[:, sl] = (xn * gq_ref[:, sl] * 0.125).astype(BF16)
            _, xn = _seg_norm(p_ref[:, 512 + LANES * p:512 + LANES * (p + 1)], PAIR)
            ka_ref[:, sl] = (xn * gk_ref[:, sl]).astype(BF16)
        va_ref[...] = p_ref[:, 1024:1536].astype(BF16)
        _, xn = _rms(p_ref[:, 1536:1792])
        cq_ref[...] = (xn * gcq_ref[...]).astype(BF16)
        _, xn = _rms(p_ref[:, 1792:1920])
        ckv_ref[...] = (xn * gckv_ref[...]).astype(BF16)

    row = lambda w: pl.BlockSpec((tm, w), lambda i: (i, 0))
    vec = lambda w: pl.BlockSpec((1, w), lambda i: (0, 0))
    return pl.pallas_call(
        body, name="mix_fwd", grid=(t // tm,),
        in_specs=[row(d), vec(d), pl.BlockSpec((PROJ_W, d), lambda i: (0, 0)), vec(512), vec(512), vec(256), vec(128)],
        out_specs=[row(d), row(PROJ_W), row(512), row(512), row(512), row(256), row(128)],
        out_shape=[jax.ShapeDtypeStruct((t, d), BF16), jax.ShapeDtypeStruct((t, PROJ_W), F32)]
        + [jax.ShapeDtypeStruct((t, w_), BF16) for w_ in (512, 512, 512, 256, 128)],
        compiler_params=_params(("parallel",)),
    )(x, g, w, gq, gk, gcq, gckv)


def mix_bwd(proj, x, h, dres, w, g, dqa, dkp, dvp, dcq, dckv, dkr, gq, gk, gcq, gckv):
    t, d = x.shape
    tm = A_TM
    nb = t // tm

    def body(p_ref, x_ref, h_ref, dres_ref, w_ref, g_ref, dqa_ref, dk0_ref, dk1_ref, dk2_ref, dv0_ref, dv1_ref,
             dv2_ref, dcq_ref, dckv_ref, dkr_ref, gq_ref, gk_ref, gcq_ref, gckv_ref,
             dx_ref, dw_ref, dg_ref, dgq_ref, dgk_ref, dgcq_ref, dgckv_ref, dp_ref, acc_ref):
        i = pl.program_id(0)

        @pl.when(i == 0)
        def _():
            for ref in (acc_ref, dg_ref, dgq_ref, dgk_ref, dgcq_ref, dgckv_ref):
                ref[...] = jnp.zeros_like(ref)

        has1 = (i + 1 < nb).astype(F32)
        has2 = (i + 2 < nb).astype(F32)
        for p in range(4):
            sl = slice(LANES * p, LANES * (p + 1))
            r, xn = _seg_norm(p_ref[:, sl], PAIR)
            dy = dqa_ref[:, sl] * 0.125
            dp_ref[:, sl] = _seg_norm_bwd(r, xn, dy * gq_ref[:, sl], PAIR).astype(BF16)
            dgq_ref[:, sl] += jnp.sum(dy * xn, axis=0, keepdims=True)
            ks = slice(512 + LANES * p, 512 + LANES * (p + 1))
            r, xn = _seg_norm(p_ref[:, ks], PAIR)
            dy = dk0_ref[0, :, sl] + has1 * dk1_ref[0, :, sl] + has2 * dk2_ref[0, :, sl]
            dp_ref[:, ks] = _seg_norm_bwd(r, xn, dy * gk_ref[:, sl], PAIR).astype(BF16)
            dgk_ref[:, sl] += jnp.sum(dy * xn, axis=0, keepdims=True)
        dp_ref[:, 1024:1536] = (dv0_ref[0] + has1 * dv1_ref[0] + has2 * dv2_ref[0]).astype(BF16)
        for (a, b, dlat_ref, glat_ref, dglat_ref) in ((1536, 1792, dcq_ref, gcq_ref, dgcq_ref),
                                                      (1792, 1920, dckv_ref, gckv_ref, dgckv_ref)):
            r, xn = _rms(p_ref[:, a:b])
            dy = dlat_ref[...]
            dyg = dy * glat_ref[...]
            dp_ref[:, a:b] = (r * (dyg - xn * jnp.mean(dyg * xn, axis=-1, keepdims=True))).astype(BF16)
            dglat_ref[...] += jnp.sum(dy * xn, axis=0, keepdims=True)
        dp_ref[:, 1920:2048] = dkr_ref[...].astype(BF16)

        dproj = dp_ref[...]
        acc_ref[...] += _dot_tn(dproj, h_ref[...])
        dh = _dot(dproj, w_ref[...])
        r, xn = _rms(x_ref[...])
        dhg = dh * g_ref[...]
        dx_ref[...] = dres_ref[...] + r * (dhg - xn * jnp.mean(dhg * xn, axis=-1, keepdims=True))
        dg_ref[...] += jnp.sum(dh * xn, axis=0, keepdims=True)

        @pl.when(i == nb - 1)
        def _():
            dw_ref[...] = acc_ref[...].astype(BF16)

    row = lambda w_: pl.BlockSpec((tm, w_), lambda i: (i, 0))
    vec = lambda w_: pl.BlockSpec((1, w_), lambda i: (0, 0))
    part = lambda s: pl.BlockSpec((1, tm, 512), lambda i: (s, jnp.minimum(i + s, nb - 1), 0))
    whole = pl.BlockSpec((PROJ_W, d), lambda i: (0, 0))
    return pl.pallas_call(
        body, name="mix_bwd", grid=(nb,),
        in_specs=[row(PROJ_W), row(d), row(d), row(d), whole, vec(d), row(512), part(0), part(1), part(2),
                  part(0), part(1), part(2), row(256), row(128), row(128), vec(512), vec(512), vec(256), vec(128)],
        out_specs=[row(d), whole, vec(d), vec(512), vec(512), vec(256), vec(128)],
        out_shape=[jax.ShapeDtypeStruct((t, d), F32), jax.ShapeDtypeStruct((PROJ_W, d), BF16)]
        + [jax.ShapeDtypeStruct((1, w_), F32) for w_ in (d, 512, 512, 256, 128)],
        scratch_shapes=[pltpu.VMEM((tm, PROJ_W), BF16), pltpu.VMEM((PROJ_W, d), F32)],
        compiler_params=_params(("arbitrary",)),
    )(proj, x, h, dres, w, g, dqa, dkp, dkp, dkp, dvp, dvp, dvp, dcq, dckv, dkr, gq, gk, gcq, gckv)


def _roll(x, shift):
    return pltpu.roll(x, shift % LANES, 1)


def _rope(y, c, s1, s2):
    return y * c + _roll(y, -16) * s1 + _roll(y, 16) * s2


def _rope_bwd(d, c, s1, s2):
    return d * c + _roll(d * s1, 16) + _roll(d * s2, -16)


def _up_proj(cq_ref, ckv_ref, wuq_ref, wukv_ref):
    return _dot_nt(cq_ref[...], wuq_ref[...]), _dot(ckv_ref[...], wukv_ref[...])


def prep2_fwd(cqn, ckvn, proj, wuq, wukv, gq, gk, gkr, tabs):
    t = cqn.shape[0]
    tm = A_TM

    def body(cq_ref, ckv_ref, kr_ref, wuq_ref, wukv_ref, gq_ref, gk_ref, gkr_ref, tab_ref, qf_ref, kf_ref, vp_ref):
        q_all, kv_all = _up_proj(cq_ref, ckv_ref, wuq_ref, wukv_ref)
        _, xn = _seg_norm(kr_ref[...], K_ROPE)
        kpe = _roll(_rope(xn * gkr_ref[...], tab_ref[3], tab_ref[4], tab_ref[5]), 64)
        for h in range(HEADS):
            sl = slice(LANES * h, LANES * (h + 1))
            _, xn = _seg_norm(q_all[:, sl], Q_HEAD)
            qf_ref[:, sl] = (_rope(xn * gq_ref[...], tab_ref[0], tab_ref[1], tab_ref[2]) * B_SCALE2).astype(BF16)
            x = kv_all[:, sl]
            lo = _lane(x.shape) < HD
            _, xkn = _seg_norm(jnp.where(lo, x, 0.0), PAIR)
            kf_ref[:, sl] = (xkn * gk_ref[...] + kpe).astype(BF16)
            if h % 2 == 0:
                v_even = _roll(x, 64)
            else:
                vp_ref[:, LANES * (h // 2):LANES * (h // 2 + 1)] = jnp.where(lo, v_even, x).astype(BF16)

    row = lambda w: pl.BlockSpec((tm, w), lambda i: (i, 0))
    vec = lambda w: pl.BlockSpec((1, w), lambda i: (0, 0))
    full = lambda a: pl.BlockSpec(a.shape, lambda i: (0, 0))
    return pl.pallas_call(
        body, name="prep2_fwd", grid=(t // tm,),
        in_specs=[row(256), row(128), pl.BlockSpec((tm, LANES), lambda i: (i, 15)), full(wuq), full(wukv),
                  vec(128), vec(128), vec(128), pl.BlockSpec((6, tm, LANES), lambda i: (0, i, 0))],
        out_specs=[row(1024), row(1024), row(512)],
        out_shape=[jax.ShapeDtypeStruct((t, 1024), BF16), jax.ShapeDtypeStruct((t, 1024), BF16),
                   jax.ShapeDtypeStruct((t, 512), BF16)],
        compiler_params=_params(("parallel",)),
    )(cqn, ckvn, proj, wuq, wukv, gq, gk, gkr, tabs)


def prep2_bwd(cqn, ckvn, proj, wuq, wukv, dqf, dkf, dvp, gq, gk, gkr, tabs):
    t = cqn.shape[0]
    tm = A_TM

    def body(cq_ref, ckv_ref, kr_ref, wuq_ref, wukv_ref, dqf_ref, dkf_ref, dvp_ref, gq_ref, gk_ref, gkr_ref, tab_ref,
             dcq_ref, dckv_ref, dkr_ref, dwuq_ref, dwukv_ref, dgq_ref, dgk_ref, dgkr_ref, dq_ref, dkv_ref):
        i = pl.program_id(0)

        @pl.when(i == 0)
        def _():
            for ref in (dwuq_ref, dwukv_ref, dgq_ref, dgk_ref, dgkr_ref):
                ref[...] = jnp.zeros_like(ref)

        q_all, kv_all = _up_proj(cq_ref, ckv_ref, wuq_ref, wukv_ref)
        dgq = jnp.zeros((1, LANES), F32)
        dgk = jnp.zeros((1, LANES), F32)
        dkpe = jnp.zeros((tm, LANES), F32)
        for h in range(HEADS):
            sl = slice(LANES * h, LANES * (h + 1))
            lane = _lane((tm, LANES))
            mn, mr = lane < HD, (lane >= HD) & (lane < HD + ROPE)
            r, xn = _seg_norm(q_all[:, sl], Q_HEAD)
            dy = _rope_bwd(dqf_ref[sl, :].T, tab_ref[0], tab_ref[1], tab_ref[2])
            dyg = dy * gq_ref[...]
            dq_ref[:, sl] = _seg_norm_bwd(r, xn, dyg, Q_HEAD).astype(BF16)
            dgq = dgq + jnp.sum(dy * xn, axis=0, keepdims=True)

            x = kv_all[:, sl]
            dk = dkf_ref[:, sl]
            rk, xkn = _seg_norm(jnp.where(mn, x, 0.0), PAIR)
            dyk = jnp.where(mn, dk, 0.0)
            dxk = _seg_norm_bwd(rk, xkn, dyk * gk_ref[...], PAIR)
            dgk = dgk + jnp.sum(dyk * xkn, axis=0, keepdims=True)
            dkpe = dkpe + jnp.where(mr, dk, 0.0)
            dvpair = dvp_ref[:, LANES * (h // 2):LANES * (h // 2 + 1)]
            dv = _roll(dvpair, 64) if h % 2 == 0 else dvpair
            dkv_ref[:, sl] = jnp.where(mn, dxk, dv).astype(BF16)

        r, xn = _seg_norm(kr_ref[...], K_ROPE)
        dy = _rope_bwd(_roll(dkpe, 64), tab_ref[3], tab_ref[4], tab_ref[5])
        dkr_ref[...] = _seg_norm_bwd(r, xn, dy * gkr_ref[...], K_ROPE)
        dgq_ref[...] += dgq
        dgk_ref[...] += dgk
        dgkr_ref[...] += jnp.sum(dy * xn, axis=0, keepdims=True)
        dqb, dkvb = dq_ref[...], dkv_ref[...]
        dcq_ref[...] = _dot(dqb, wuq_ref[...])
        dckv_ref[...] = _dot_nt(dkvb, wukv_ref[...])
        dwuq_ref[...] += _dot_tn(dqb, cq_ref[...])
        dwukv_ref[...] += _dot_tn(ckv_ref[...], dkvb)

    row = lambda w: pl.BlockSpec((tm, w), lambda i: (i, 0))
    vec = lambda w: pl.BlockSpec((1, w), lambda i: (0, 0))
    full = lambda a: pl.BlockSpec(a.shape, lambda i: (0, 0))
    return pl.pallas_call(
        body, name="prep2_bwd", grid=(t // tm,),
        in_specs=[row(256), row(128), pl.BlockSpec((tm, LANES), lambda i: (i, 15)), full(wuq), full(wukv),
                  pl.BlockSpec((1024, tm), lambda i: (0, i)), row(1024), row(512),
                  vec(128), vec(128), vec(128), pl.BlockSpec((6, tm, LANES), lambda i: (0, i, 0))],
        out_specs=[row(256), row(128), row(128), full(wuq), full(wukv), vec(128), vec(128), vec(128)],
        out_shape=[jax.ShapeDtypeStruct((t, 256), F32), jax.ShapeDtypeStruct((t, 128), F32),
                   jax.ShapeDtypeStruct((t, LANES), F32), jax.ShapeDtypeStruct(wuq.shape, F32),
                   jax.ShapeDtypeStruct(wukv.shape, F32)] + [jax.ShapeDtypeStruct((1, LANES), F32)] * 3,
        scratch_shapes=[pltpu.VMEM((tm, 1024), BF16), pltpu.VMEM((tm, 1024), BF16)],
        compiler_params=_params(("arbitrary",)),
    )(cqn, ckvn, proj, wuq, wukv, dqf, dkf, dvp, gq, gk, gkr, tabs)


A_TQ = 256
A_WIN = 3 * A_TQ


def _a_specs(t):
    nb = t // A_TQ
    blk = lambda s: pl.BlockSpec((A_TQ, 512), lambda i: (jnp.maximum(i - s, 0), 0))
    return nb, blk


def _a_exp(q_ref, kc, b_ref, head, sl, lo):
    hm = lo if head % 2 == 0 else ~lo
    qm = jnp.where(hm, q_ref[:, sl], jnp.zeros((), BF16))
    s = _dot_nt(qm, kc) + b_ref[0, head]
    e = jnp.exp(s - jnp.max(s, axis=-1, keepdims=True))
    return hm, qm, e, 1.0 / jnp.sum(e, axis=-1, keepdims=True)


def _a_bias_spec():
    return pl.BlockSpec((1, HEADS, A_TQ, A_WIN), lambda i: (jnp.minimum(i, 2), 0, 0, 0))


def attn_a_fwd(qa, ka, va, bias):
    t = qa.shape[0]
    nb, blk = _a_specs(t)

    def body(q_ref, k2_ref, k1_ref, k0_ref, v2_ref, v1_ref, v0_ref, b_ref, o_ref):
        lo = _lane((A_TQ, LANES)) < HD
        for p in range(4):
            sl = slice(LANES * p, LANES * (p + 1))
            kc = jnp.concatenate([k2_ref[:, sl], k1_ref[:, sl], k0_ref[:, sl]], axis=0)
            vc = jnp.concatenate([v2_ref[:, sl], v1_ref[:, sl], v0_ref[:, sl]], axis=0)
            outs = []
            for h2 in range(2):
                _, _, e, inv = _a_exp(q_ref, kc, b_ref, 2 * p + h2, sl, lo)
                outs.append(_dot(e.astype(BF16), vc) * inv)
            o_ref[:, sl] = jnp.where(lo, outs[0], outs[1]).astype(BF16)

    return pl.pallas_call(
        body, name="attn_a_fwd", grid=(nb,),
        in_specs=[blk(0), blk(2), blk(1), blk(0), blk(2), blk(1), blk(0), _a_bias_spec()],
        out_specs=pl.BlockSpec((A_TQ, 512), lambda i: (i, 0)),
        out_shape=jax.ShapeDtypeStruct((t, 512), BF16),
        compiler_params=_params(("parallel",)),
    )(qa, ka, ka, ka, va, va, va, bias)


def attn_a_bwd(qa, ka, va, bias, do):
    t = qa.shape[0]
    nb, blk = _a_specs(t)

    def body(q_ref, k2_ref, k1_ref, k0_ref, v2_ref, v1_ref, v0_ref, b_ref, do_ref, dq_ref, dk_ref, dv_ref, db_ref):
        qb = pl.program_id(0)

        @pl.when(qb == 0)
        def _():
            db_ref[...] = jnp.zeros_like(db_ref)

        lo = _lane((A_TQ, LANES)) < HD
        for p in range(4):
            sl = slice(LANES * p, LANES * (p + 1))
            kc = jnp.concatenate([k2_ref[:, sl], k1_ref[:, sl], k0_ref[:, sl]], axis=0)
            vc = jnp.concatenate([v2_ref[:, sl], v1_ref[:, sl], v0_ref[:, sl]], axis=0)
            dqs = []
            dkc = jnp.zeros((A_WIN, LANES), F32)
            dvc = jnp.zeros((A_WIN, LANES), F32)
            for h2 in range(2):
                head = 2 * p + h2
                hm, qm, e, inv = _a_exp(q_ref, kc, b_ref, head, sl, lo)
                pr = e * inv
                dom = jnp.where(hm, do_ref[:, sl], jnp.zeros((), BF16))
                dp = _dot_nt(dom, vc)
                ds = pr * (dp - jnp.sum(pr * dp, axis=-1, keepdims=True))
                db_ref[head] += ds
                dsb = ds.astype(BF16)
                dqs.append(_dot(dsb, kc))
                dkc = dkc + _dot_tn(dsb, qm)
                dvc = dvc + _dot_tn(pr.astype(BF16), dom)
            dq_ref[:, sl] = jnp.where(lo, dqs[0], dqs[1])
            for s in range(3):
                rows = slice(A_TQ * (2 - s), A_TQ * (3 - s))
                dk_ref[s, :, sl] = dkc[rows]
                dv_ref[s, :, sl] = dvc[rows]

    share = pl.BlockSpec((3, A_TQ, 512), lambda i: (0, i, 0))
    return pl.pallas_call(
        body, name="attn_a_bwd", grid=(nb,),
        in_specs=[blk(0), blk(2), blk(1), blk(0), blk(2), blk(1), blk(0), _a_bias_spec(), blk(0)],
        out_specs=[pl.BlockSpec((A_TQ, 512), lambda i: (i, 0)), share, share,
                   pl.BlockSpec((HEADS, A_TQ, A_WIN), lambda i: (0, 0, 0))],
        out_shape=[jax.ShapeDtypeStruct((t, 512), F32), jax.ShapeDtypeStruct((3, t, 512), F32),
                   jax.ShapeDtypeStruct((3, t, 512), F32), jax.ShapeDtypeStruct((HEADS, A_TQ, A_WIN), F32)],
        compiler_params=_params(("arbitrary",)),
    )(qa, ka, ka, ka, va, va, va, bias, do)


B_T = 1024


B_SCALE2 = B_SCALE * 1.4426950408889634
_B_ALL = slice(0, B_T)
_B_LO, _B_HI = slice(0, B_T // 2), slice(B_T // 2, B_T)
_B_DIAG = ((_B_LO, _B_LO), (_B_LO, _B_HI), (_B_HI, _B_HI))


def _tri_tables(n, by_query):
    pairs = [(i, j) for i in range(n) for j in range(i + 1)] if by_query else [(i, j) for j in range(n) for i in range(j, n)]
    return (np.asarray([p[0] for p in pairs], np.int32), np.asarray([p[1] for p in pairs], np.int32))


def _b_mask_t(s):
    kc = lax.broadcasted_iota(jnp.int32, s.shape, 0) // CHUNK
    qc = lax.broadcasted_iota(jnp.int32, s.shape, 1) // CHUNK
    return jnp.where(kc <= qc, s, NEG)


def attn_b_fwd(qf, kf, vp):
    t = qf.shape[0]
    n = t // B_T
    qtab, ktab = _tri_tables(n, by_query=True)

    def body(qt_ref, kt_ref, q_ref, k_ref, v_ref, o_ref, lse_ref, m_s, l_s, acc_s):
        qb, kb = qt_ref[pl.program_id(1)], kt_ref[pl.program_id(1)]

        @pl.when(kb == 0)
        def _():
            m_s[...] = jnp.full_like(m_s, NEG)
            l_s[...] = jnp.zeros_like(l_s)
            acc_s[...] = jnp.zeros_like(acc_s)

        def block(kr, qr, masked):
            v = v_ref[kr, :]
            for h2 in range(2):
                sl = slice(LANES * h2, LANES * (h2 + 1))
                s = _dot_nt(k_ref[kr, sl], q_ref[qr, sl])
                if masked:
                    s = _b_mask_t(s)
                m_prev = m_s[h2, :, qr]
                m_new = jnp.maximum(m_prev, jnp.max(s, axis=0, keepdims=True))
                alpha = jnp.exp2(m_prev - m_new)
                pr = jnp.exp2(s - m_new)
                l_s[h2, :, qr] = alpha * l_s[h2, :, qr] + jnp.sum(pr, axis=0, keepdims=True)
                acc_s[h2, :, qr] = alpha * acc_s[h2, :, qr] + _dot_tn(v, pr.astype(BF16))
                m_s[h2, :, qr] = m_new

        @pl.when(kb < qb)
        def _():
            block(_B_ALL, _B_ALL, False)

        @pl.when(kb == qb)
        def _():
            for kr, qr in _B_DIAG:
                block(kr, qr, kr == qr)
            for h2 in range(2):
                l = l_s[h2]
                rows = slice(HD * h2, HD * (h2 + 1))
                o_ref[rows, :] = (acc_s[h2, rows, :] * (1.0 / l)).astype(BF16)
                lse_ref[0, h2:h2 + 1, :] = m_s[h2] + jnp.log2(l)

    grid_spec = pltpu.PrefetchScalarGridSpec(
        num_scalar_prefetch=2, grid=(4, len(qtab)),
        in_specs=[pl.BlockSpec((B_T, 256), lambda p, s, qt, kt: (qt[s], p)),
                  pl.BlockSpec((B_T, 256), lambda p, s, qt, kt: (kt[s], p)),
                  pl.BlockSpec((B_T, LANES), lambda p, s, qt, kt: (kt[s], p))],
        out_specs=[pl.BlockSpec((LANES, B_T), lambda p, s, qt, kt: (p, qt[s])),
                   pl.BlockSpec((1, 2, B_T), lambda p, s, qt, kt: (p, 0, qt[s]))],
        scratch_shapes=[pltpu.VMEM((2, 1, B_T), F32), pltpu.VMEM((2, 1, B_T), F32), pltpu.VMEM((2, LANES, B_T), F32)])
    return pl.pallas_call(
        body, name="attn_b_fwd", grid_spec=grid_spec,
        out_shape=[jax.ShapeDtypeStruct((512, t), BF16), jax.ShapeDtypeStruct((4, 2, t), F32)],
        compiler_params=_params(("parallel", "arbitrary")),
    )(jnp.asarray(qtab), jnp.asarray(ktab), qf, kf, vp)


def attn_b_bwd(qf, kf, vp, do, do_t, o_t, lse, scatter=()):
    t = qf.shape[0]
    n = t // B_T
    qtab, ktab = _tri_tables(n, by_query=False)
    plan = ScatterPlan(scatter)
    m = plan.n
    last = len(qtab) - 1

    def body(*refs):
        qt_ref, kt_ref, q_ref, k_ref, v_ref, do_ref, dot_ref, ot_ref, lse_ref = refs[:9]
        ins, (dq_ref, dk_ref, dv_ref), outs = refs[9:9 + m], refs[9 + m:12 + m], refs[12 + m:12 + 2 * m]
        sems = refs[12 + 2 * m:]
        qb, kb = qt_ref[pl.program_id(1)], kt_ref[pl.program_id(1)]
        if m:
            pl.when((pl.program_id(0) == 0) & (pl.program_id(1) == 0))(lambda: plan.start(ins, outs, sems))

        @pl.when(pl.program_id(1) == 0)
        def _():
            dq_ref[...] = jnp.zeros_like(dq_ref)

        @pl.when(qb == kb)
        def _():
            dk_ref[...] = jnp.zeros_like(dk_ref)
            dv_ref[...] = jnp.zeros_like(dv_ref)

        def block(kr, qr, masked):
            nq = qr.stop - qr.start
            cols = pl.ds(pl.multiple_of(qb * B_T + qr.start, LANES), nq)
            v = v_ref[kr, :]
            dov = do_ref[qr, :]
            prod = dot_ref[:, qr].astype(F32) * ot_ref[:, qr].astype(F32)
            lo = _lane((nq, LANES)) < HD
            for h2 in range(2):
                sl = slice(LANES * h2, LANES * (h2 + 1))
                hm = lo if h2 == 0 else ~lo
                q = q_ref[qr, sl]
                k = k_ref[kr, sl]
                dom = jnp.where(hm, dov, jnp.zeros((), BF16))
                delta = jnp.sum(prod[HD * h2:HD * (h2 + 1), :], axis=0, keepdims=True)
                s = _dot_nt(k, q)
                if masked:
                    s = _b_mask_t(s)
                pr = jnp.exp2(s - lse_ref[0, h2:h2 + 1, qr])
                dp = _dot_nt(v, dom)
                ds = (pr * (dp - delta)).astype(BF16)
                dk_ref[kr, sl] += _dot(ds, q) * (B_SCALE / B_SCALE2)
                dv_ref[kr, :] += _dot(pr.astype(BF16), dom)
                dq_ref[sl, cols] += _dot_tn(k, ds) * B_SCALE

        @pl.when(qb > kb)
        def _():
            block(_B_ALL, _B_ALL, False)

        @pl.when(qb == kb)
        def _():
            for kr, qr in _B_DIAG:
                block(kr, qr, kr == qr)

        if m:
            pl.when((pl.program_id(0) == 3) & (pl.program_id(1) == last))(lambda: plan.finish(ins, outs, sems))

    qrow = lambda w: pl.BlockSpec((B_T, w), lambda p, s, qt, kt: (qt[s], p))
    qcol = pl.BlockSpec((LANES, B_T), lambda p, s, qt, kt: (p, qt[s]))
    krow = lambda w: pl.BlockSpec((B_T, w), lambda p, s, qt, kt: (kt[s], p))
    grid_spec = pltpu.PrefetchScalarGridSpec(
        num_scalar_prefetch=2, grid=(4, len(qtab)),
        in_specs=[qrow(256), krow(256), krow(LANES), qrow(LANES), qcol, qcol,
                  pl.BlockSpec((1, 2, B_T), lambda p, s, qt, kt: (p, 0, qt[s]))] + [ANY] * m,
        out_specs=[pl.BlockSpec((256, t), lambda p, s, qt, kt: (p, 0)), krow(256), krow(LANES)] + [ANY] * m,
        scratch_shapes=plan.scratch if m else [])
    return pl.pallas_call(
        body, name="attn_b_bwd", grid_spec=grid_spec,
        out_shape=[jax.ShapeDtypeStruct((1024, t), F32), jax.ShapeDtypeStruct((t, 1024), F32),
                   jax.ShapeDtypeStruct((t, 512), F32)] + plan.out_shape,
        compiler_params=_params(("arbitrary", "arbitrary")),
    )(jnp.asarray(qtab), jnp.asarray(ktab), qf, kf, vp, do, do_t, o_t, lse, *scatter)


_U_LEN = A_TQ + A_WIN - 1


def _band_mask():
    a = np.arange(A_TQ)[:, None] // CHUNK
    b = np.arange(A_WIN)[None, :] // CHUNK
    return (b >= a) & (b <= a + A_LEFT)


def bias_block(table):
    h = table.shape[0]
    n_lo = A_WIN - 1 - 2 * A_TQ - A_MAX_REL
    ext = jnp.concatenate([jnp.repeat(table[:, :1], n_lo, axis=1), table,
                           jnp.repeat(table[:, -1:], _U_LEN - n_lo - table.shape[1], axis=1)], axis=1)
    row = jnp.pad(ext[:, ::-1], ((0, 0), (0, 1)))[:, None, :]
    band = _band_mask()
    first = [band & (np.arange(A_WIN)[None, :] >= 2 * A_TQ - A_TQ * v) for v in range(3)]
    keep = jnp.asarray(np.stack(first), jnp.int32)

    def body(r_ref, k_ref, o_ref):
        rows = jnp.broadcast_to(r_ref[0], (A_TQ, _U_LEN + 1))
        skew = pltpu.roll(rows, _U_LEN + 1 - (A_TQ - 1), 1, stride=1, stride_axis=0)
        toep = skew[:, :A_WIN]
        for v in range(3):
            o_ref[v, 0] = jnp.where(k_ref[v] != 0, toep, NEG)

    return pl.pallas_call(
        body, name="bias_block", grid=(h,),
        in_specs=[pl.BlockSpec((1, 1, _U_LEN + 1), lambda i: (i, 0, 0)),
                  pl.BlockSpec((3, A_TQ, A_WIN), lambda i: (0, 0, 0))],
        out_specs=pl.BlockSpec((3, 1, A_TQ, A_WIN), lambda i: (0, i, 0, 0)),
        out_shape=jax.ShapeDtypeStruct((3, h, A_TQ, A_WIN), F32),
        compiler_params=_params(("parallel",)),
    )(row, keep)


def bias_block_grad(db):
    h = db.shape[0]
    n_lo = A_WIN - 1 - 2 * A_TQ - A_MAX_REL
    skew = jnp.pad(db, ((0, 0), (0, 0), (A_TQ - 1, 0)))
    flat = jnp.pad(skew.reshape(h, A_TQ * _U_LEN), ((0, 0), (0, A_TQ)))
    ext = jnp.sum(flat.reshape(h, A_TQ, _U_LEN + 1), axis=1)[:, :_U_LEN][:, ::-1]
    n_tab = 2 * A_MAX_REL + 1
    first = jnp.sum(ext[:, :n_lo + 1], axis=1, keepdims=True)
    last = jnp.sum(ext[:, n_lo + n_tab - 1:], axis=1, keepdims=True)
    return jnp.concatenate([first, ext[:, n_lo + 1:n_lo + n_tab - 1], last], axis=1)


def rope_tabs(t):
    inv = 1.0 / (10000.0 ** (jnp.arange(0, ROPE, 2, dtype=F32) / ROPE))
    ang = jnp.arange(t, dtype=F32)[:, None] * inv[None, :]
    cos, sin = jnp.cos(ang), jnp.sin(ang)
    z = lambda w: jnp.zeros((t, w), F32)
    ck = jnp.concatenate([cos, cos, z(96)], axis=1)
    s1k = jnp.concatenate([-sin, z(112)], axis=1)
    s2k = jnp.concatenate([z(16), sin, z(96)], axis=1)
    cq = jnp.concatenate([jnp.ones((t, HD), F32), cos, cos, z(32)], axis=1)
    s1q = jnp.concatenate([z(HD), -sin, z(48)], axis=1)
    s2q = jnp.concatenate([z(HD + 16), sin, z(32)], axis=1)
    return jnp.stack([cq, s1q, s2q, ck, s1k, s2k])


def _pad_lanes(v, width):
    return jnp.pad(v, ((0, 0), (0, width - v.shape[1])))


LATE = ("w_in", "b_w_uq", "b_w_ukv", "w_out", "ffn2_w_gate", "ffn2_w_up", "ffn2_w_down")
FFN2 = ("ffn2_w_gate", "ffn2_w_up", "ffn2_w_down")


def kernel_layout(gathered):
    w = {n: v for n, v in gathered.items() if n.startswith("ffn")}
    if "w_in" in gathered:
        w["w_in"] = jnp.pad(gathered["w_in"].reshape(IN_COLS, D_MODEL), ((0, PROJ_W - IN_COLS), (0, 0)))
        uq = gathered["b_w_uq"].reshape(HEADS, HD + ROPE, 256)
        w["b_w_uq"] = jnp.pad(uq, ((0, 0), (0, LANES - HD - ROPE), (0, 0))).reshape(HEADS * LANES, 256)
        w["b_w_ukv"] = _shards_to_cols(gathered["b_w_ukv"])
        w["w_out"] = gathered["w_out"].reshape(N_SHARD * gathered["w_out"].shape[1], D_MODEL)
    return w


def local_step(x, target, w, late=None):
    t = x.shape[0]
    gq = jnp.tile(w["a_q_norm"], (1, HEADS))
    gk = jnp.tile(w["a_k_norm"], (1, HEADS))
    gq128 = _pad_lanes(jnp.concatenate([w["b_q_nope_norm"], w["b_q_rope_norm"]], axis=1), LANES)
    gk128 = _pad_lanes(w["b_k_nope_norm"], LANES)
    gkr128 = _pad_lanes(w["b_k_rope_norm"], LANES)
    tabs = rope_tabs(t)
    bias = bias_block(w["a_rel_bias"])

    if late is None:
        x1, gate1, up1 = ffn_fwd(x, w["ffn1_norm"], w["ffn1_w_gate"], w["ffn1_w_up"], w["ffn1_w_down"], "ffn_fwd")
    else:
        own, shard = late
        x1, gate1, up1, *got = ffn_fwd(x, w["ffn1_norm"], w["ffn1_w_gate"], w["ffn1_w_up"], w["ffn1_w_down"],
                                       "ffn_fwd_gather", gather=own)
        w = dict(w, **kernel_layout({n: lax.dynamic_update_index_in_dim(g_, o_, shard, 0)
                                     for n, g_, o_ in zip(LATE, got, own)}))
    h, proj, qa, ka, va, cqn, ckvn = mix_fwd(x1, w["mix_norm"], w["w_in"], gq, gk,
                                             w["b_q_lat_norm"], w["b_kv_lat_norm"])
    qf, kf, vp = prep2_fwd(cqn, ckvn, proj, w["b_w_uq"], w["b_w_ukv"], gq128, gk128, gkr128, tabs)
    oa = attn_a_fwd(qa, ka, va, bias)
    ob_t, lse = attn_b_fwd(qf, kf, vp)
    x2 = out_proj(x1, oa, ob_t, w["w_out"])
    g = {}
    dx3, gate2, up2, g["final_norm"], loss = ffn_fwd(
        x2, w["ffn2_norm"], w["ffn2_w_gate"], w["ffn2_w_up"], w["ffn2_w_down"], "ffn_fwd_loss",
        loss_head=(w["final_norm"], target))
    g["ffn2_w_gate"], g["ffn2_w_up"], g["ffn2_w_down"], dhp = ffn_bwd(
        x2, dx3, w["ffn2_norm"], w["ffn2_w_gate"], w["ffn2_w_up"], w["ffn2_w_down"], gate2, up2, "ffn_bwd")
    dx2, g["ffn2_norm"], d_oa, d_ob, d_ob_t, g["w_out"] = out_proj_bwd(
        x2, w["ffn2_norm"], dhp, dx3, w["w_out"], oa, ob_t)
    early = [g[n] for n in FFN2] + [g["w_out"].reshape(N_SHARD, -1, D_MODEL)]
    dqf, dkf, dvp, *landed_early = attn_b_bwd(qf, kf, vp, d_ob, d_ob_t, ob_t, lse,
                                              scatter=() if late is None else early)
    dqa, dkp, dvpa, dbias = attn_a_bwd(qa, ka, va, bias, d_oa)
    dcq, dckv, dkr, dwuq, dwukv, dgq128, dgk128, dgkr128 = prep2_bwd(
        cqn, ckvn, proj, w["b_w_uq"], w["b_w_ukv"], dqf, dkf, dvp, gq128, gk128, gkr128, tabs)
    g["b_w_uq"], g["b_w_ukv"] = dwuq.astype(BF16), dwukv.astype(BF16)
    dx1, g["w_in"], g["mix_norm"], dgq, dgk, g["b_q_lat_norm"], g["b_kv_lat_norm"] = mix_bwd(
        proj, x1, h, dx2, w["w_in"], w["mix_norm"], dqa, dkp, dvpa, dcq, dckv, dkr, gq, gk,
        w["b_q_lat_norm"], w["b_kv_lat_norm"])
    mid = [g["w_in"][:IN_COLS].reshape(N_SHARD, IN_COLS // N_SHARD, D_MODEL),
           g["b_w_uq"].reshape(HEADS, LANES, 256)[:, :HD + ROPE].reshape(N_SHARD, -1, 256),
           _cols_to_shards(g["b_w_ukv"])]
    g["ffn1_w_gate"], g["ffn1_w_up"], g["ffn1_w_down"], dhp, *landed_late = ffn_bwd(
        x, dx1, w["ffn1_norm"], w["ffn1_w_gate"], w["ffn1_w_up"], w["ffn1_w_down"], gate1, up1,
        "ffn_bwd" if late is None else "ffn_bwd_scatter", scatter=() if late is None else mid,
        spread=None if late is None else late[1])
    grad_x, g["ffn1_norm"] = norm_bwd(x, w["ffn1_norm"], dhp, dx1, "ffn_norm_bwd")
    landed = dict(zip(FFN2 + ("w_out", "w_in", "b_w_uq", "b_w_ukv", "ffn1_w_gate", "ffn1_w_up", "ffn1_w_down"),
                      landed_early + landed_late))

    g["a_q_norm"] = jnp.sum(dgq.reshape(HEADS, HD), axis=0, keepdims=True)
    g["a_k_norm"] = jnp.sum(dgk.reshape(HEADS, HD), axis=0, keepdims=True)
    g["a_rel_bias"] = bias_block_grad(dbias)
    g["b_q_nope_norm"] = dgq128[:, :HD]
    g["b_q_rope_norm"] = dgq128[:, HD:HD + ROPE]
    g["b_k_nope_norm"] = dgk128[:, :HD]
    g["b_k_rope_norm"] = dgkr128[:, :ROPE]
    return loss, grad_x, g, landed


ANY = pl.BlockSpec(memory_space=pl.ANY)
N_DEV = 8


def _place():
    return lax.axis_index("x"), lax.axis_index("y"), lax.axis_index("c")


def _flip(v, bit):
    return 1 - v if bit else v


BF16_ROWS = 16


def _split_axis(shape):
    return 0 if (shape[0] // 2) % BF16_ROWS == 0 else 1


def _half_shape(shape):
    axis = _split_axis(shape)
    return tuple(s // 2 if a == axis else s for a, s in enumerate(shape))


def _half(shape, core):
    axis = _split_axis(shape)
    size = shape[axis] // 2
    return tuple(pl.ds(core * size, size) if a == axis else slice(None) for a in range(2))


class GatherPlan:
    def __init__(self, ws):
        self.shapes = [w.shape for w in ws]
        self.n = len(ws)
        self.out_shape = [jax.ShapeDtypeStruct((N_SHARD,) + w.shape, w.dtype) for w in ws]
        self.scratch = [pltpu.SemaphoreType.DMA((6 * self.n,)), pltpu.SemaphoreType.DMA((6 * self.n,))]

    def _copies(self, ins, outs, sems):
        x, y, c = _place()
        s_me = 2 * x + y
        sibling = (x, y, 1 - c)
        send_sems, recv_sems = sems

        def remote(k, src, dst, to):
            return pltpu.make_async_remote_copy(src_ref=src, dst_ref=dst, send_sem=send_sems.at[k],
                                                recv_sem=recv_sems.at[k], device_id=to, device_id_type=MESH)

        ici, fwd = [], []
        for a in range(self.n):
            mine, theirs = _half(self.shapes[a], c), _half(self.shapes[a], 1 - c)
            for j, (cx, cy) in enumerate([(1 - x, y), (x, 1 - y), (1 - x, 1 - y)]):
                got = outs[a].at[(2 * cx + cy,) + mine]
                ici.append((remote(6 * a + j, ins[a].at[mine], outs[a].at[(s_me,) + mine], (cx, cy, c)),
                            remote(6 * a + j, got, got, (cx, cy, c))))
                passed = outs[a].at[(2 * cx + cy,) + theirs]
                fwd.append((remote(6 * a + 3 + j, got, got, sibling), remote(6 * a + 3 + j, passed, passed, sibling)))
        return ici, fwd

    def start(self, ins, outs, sems):
        for send, _ in self._copies(ins, outs, sems)[0]:
            send.start()

    def forward(self, ins, outs, sems):
        ici, fwd = self._copies(ins, outs, sems)
        for (_, arrival), (send, _) in zip(ici, fwd):
            arrival.wait_recv()
            send.start()

    def finish(self, ins, outs, sems):
        ici, fwd = self._copies(ins, outs, sems)
        for _, arrival in fwd:
            arrival.wait_recv()
        for send, _ in ici + fwd:
            send.wait_send()


def allgather_shards(ws):
    plan = GatherPlan(ws)
    n = plan.n

    def body(*refs):
        ins, outs, sems = refs[:n], refs[n:2 * n], refs[2 * n:]
        plan.start(ins, outs, sems)
        plan.forward(ins, outs, sems)
        plan.finish(ins, outs, sems)

    return pl.pallas_call(
        body, name="allgather_shards", in_specs=[ANY] * n, out_specs=[ANY] * n,
        out_shape=plan.out_shape, scratch_shapes=plan.scratch,
    )(*ws)


class ScatterPlan:
    def __init__(self, gs):
        self.shapes = [g.shape[1:] for g in gs]
        self.n = len(gs)
        self.out_shape = [jax.ShapeDtypeStruct((N_DEV,) + _half_shape(g.shape[1:]), g.dtype) for g in gs]
        self.scratch = [pltpu.SemaphoreType.DMA((7 * self.n,)), pltpu.SemaphoreType.DMA((7 * self.n,)),
                        pltpu.SemaphoreType.DMA((self.n,))]

    def _copies(self, ins, outs, sems):
        x, y, c = _place()
        me = 4 * x + 2 * y + c
        send_sems, recv_sems, local_sems = sems
        local, sends, arrivals = [], [], []
        for a in range(self.n):
            piece = lambda px, py, pc, a=a: ins[a].at[(2 * px + py,) + _half(self.shapes[a], pc)]
            local.append(pltpu.make_async_copy(piece(x, y, c), outs[a].at[me], local_sems.at[a]))
            for k in range(1, N_DEV):
                px, py, pc = _flip(x, k & 4), _flip(y, k & 2), _flip(c, k & 1)
                sem = dict(send_sem=send_sems.at[7 * a + k - 1], recv_sem=recv_sems.at[7 * a + k - 1],
                           device_id=(px, py, pc), device_id_type=MESH)
                sends.append(pltpu.make_async_remote_copy(
                    src_ref=piece(px, py, pc), dst_ref=outs[a].at[me], **sem))
                slot = outs[a].at[4 * px + 2 * py + pc]
                arrivals.append(pltpu.make_async_remote_copy(src_ref=slot, dst_ref=slot, **sem))
        return local, sends, arrivals

    def start(self, ins, outs, sems):
        local, sends, _ = self._copies(ins, outs, sems)
        for cp in local + sends:
            cp.start()

    def finish(self, ins, outs, sems):
        local, sends, arrivals = self._copies(ins, outs, sems)
        for cp in arrivals:
            cp.wait_recv()
        for cp in sends:
            cp.wait_send()
        for cp in local:
            cp.wait()


def _row_tile(rows, row_bytes, budget, multiple):
    fits = [r for r in range(multiple, rows + 1, multiple) if rows % r == 0 and r * row_bytes <= budget]
    return max(fits) if fits else rows


SMALL_KERNEL_VMEM = 32 * 1024 * 1024


def sum_slots(lands, name):
    k = len(lands)
    _, rows, cols = lands[0].shape
    tr = _row_tile(rows, k * 2 * (N_DEV * cols * 2 + cols * 4), SMALL_KERNEL_VMEM, BF16_ROWS)

    def body(*refs):
        for l_ref, o_ref in zip(refs[:k], refs[k:]):
            acc = l_ref[0].astype(F32)
            for s in range(1, N_DEV):
                acc = acc + l_ref[s].astype(F32)
            o_ref[...] = acc

    return pl.pallas_call(
        body, name=name, grid=(rows // tr,),
        in_specs=[pl.BlockSpec((N_DEV, tr, cols), lambda i: (0, i, 0))] * k,
        out_specs=[pl.BlockSpec((tr, cols), lambda i: (i, 0))] * k,
        out_shape=[jax.ShapeDtypeStruct((rows, cols), F32)] * k,
        compiler_params=_params(("parallel",)),
    )(*lands)


def join_halves(hs, shapes):
    n = len(hs)

    def body(*refs):
        ins, outs = refs[:n], refs[n:2 * n]
        send_sems, recv_sems = refs[2 * n:]
        x, y, c = _place()
        sends = []
        for a in range(n):
            mine = outs[a].at[_half(shapes[a], c)]
            sends.append(pltpu.make_async_remote_copy(
                src_ref=ins[a], dst_ref=mine, send_sem=send_sems.at[a], recv_sem=recv_sems.at[a],
                device_id=(x, y, 1 - c), device_id_type=MESH))
            sends[-1].start()
        for a in range(n):
            theirs = outs[a].at[_half(shapes[a], 1 - c)]
            pltpu.make_async_remote_copy(
                src_ref=theirs, dst_ref=theirs, send_sem=send_sems.at[a], recv_sem=recv_sems.at[a],
                device_id=(x, y, 1 - c), device_id_type=MESH).wait_recv()
        for cp in sends:
            cp.wait_send()

    return pl.pallas_call(
        body, name="join_halves",
        in_specs=[ANY] * n, out_specs=[ANY] * n,
        out_shape=[jax.ShapeDtypeStruct(tuple(s), h.dtype) for s, h in zip(shapes, hs)],
        scratch_shapes=[pltpu.SemaphoreType.DMA((n,)), pltpu.SemaphoreType.DMA((n,))],
    )(*hs)


def allreduce_small(vec):
    def body(v_ref, o_ref, land_ref, send_sems, recv_sems):
        x, y, c = _place()
        me = 4 * x + 2 * y + c
        land_ref[me] = v_ref[...]
        sends = []
        for k in range(1, N_DEV):
            px, py, pc = _flip(x, k & 4), _flip(y, k & 2), _flip(c, k & 1)
            sends.append(pltpu.make_async_remote_copy(
                src_ref=v_ref, dst_ref=land_ref.at[me], send_sem=send_sems.at[k - 1], recv_sem=recv_sems.at[k - 1],
                device_id=(px, py, pc), device_id_type=MESH))
            sends[-1].start()
        for k in range(1, N_DEV):
            px, py, pc = _flip(x, k & 4), _flip(y, k & 2), _flip(c, k & 1)
            slot = land_ref.at[4 * px + 2 * py + pc]
            pltpu.make_async_remote_copy(
                src_ref=slot, dst_ref=slot, send_sem=send_sems.at[k - 1], recv_sem=recv_sems.at[k - 1],
                device_id=(px, py, pc), device_id_type=MESH).wait_recv()
        for cp in sends:
            cp.wait_send()
        acc = land_ref[0]
        for s in range(1, N_DEV):
            acc = acc + land_ref[s]
        o_ref[...] = acc

    vm = pl.BlockSpec(memory_space=pltpu.VMEM)
    return pl.pallas_call(
        body, name="allreduce_small",
        in_specs=[vm], out_specs=vm,
        out_shape=jax.ShapeDtypeStruct(vec.shape, F32),
        scratch_shapes=[pltpu.VMEM((N_DEV,) + vec.shape, F32), pltpu.SemaphoreType.DMA((N_DEV - 1,)),
                        pltpu.SemaphoreType.DMA((N_DEV - 1,))],
    )(vec)


def adamw(ws, gs, ms, vs, name):
    k = len(ws)
    rows, cols = ws[0].shape
    tr = _row_tile(rows, k * 2 * 7 * cols * 4, SMALL_KERNEL_VMEM, 8)
    c1 = 1.0 - ADAM_B1 ** ADAM_STEP
    c2 = 1.0 - ADAM_B2 ** ADAM_STEP

    def body(*refs):
        for a in range(k):
            w_ref, g_ref, m_ref, v_ref = (refs[s * k + a] for s in range(4))
            d_ref, nm_ref, nv_ref = (refs[(4 + s) * k + a] for s in range(3))
            gv = g_ref[...]
            nm = ADAM_B1 * m_ref[...] + (1.0 - ADAM_B1) * gv
            nv = ADAM_B2 * v_ref[...] + (1.0 - ADAM_B2) * (gv * gv)
            nm_ref[...] = nm
            nv_ref[...] = nv
            d_ref[...] = -ADAM_LR * ((nm / c1) / (jnp.sqrt(nv / c2) + ADAM_EPS) + ADAM_WD * w_ref[...])

    blk = pl.BlockSpec((tr, cols), lambda i: (i, 0))
    out = pl.pallas_call(
        body, name=name, grid=(rows // tr,),
        in_specs=[blk] * (4 * k), out_specs=[blk] * (3 * k),
        out_shape=[jax.ShapeDtypeStruct((rows, cols), F32)] * (3 * k),
        compiler_params=_params(("parallel",)),
    )(*ws, *gs, *ms, *vs)
    return [(out[a], out[k + a], out[2 * k + a]) for a in range(k)]


BIG = ("ffn1_w_gate", "ffn1_w_up", "ffn1_w_down", "w_in", "b_w_uq", "b_w_ukv", "w_out",
       "ffn2_w_gate", "ffn2_w_up", "ffn2_w_down")
SMALL = ("ffn1_norm", "mix_norm", "a_q_norm", "a_k_norm", "a_rel_bias", "b_q_lat_norm", "b_kv_lat_norm",
         "b_q_nope_norm", "b_q_rope_norm", "b_k_nope_norm", "b_k_rope_norm", "ffn2_norm", "final_norm")
WEIGHTS = ("ffn1_norm", "ffn1_w_gate", "ffn1_w_up", "ffn1_w_down", "mix_norm", "w_in", "a_q_norm", "a_k_norm",
           "a_rel_bias", "b_q_lat_norm", "b_w_uq", "b_kv_lat_norm", "b_w_ukv", "b_q_nope_norm", "b_q_rope_norm",
           "b_k_nope_norm", "b_k_rope_norm", "w_out", "ffn2_norm", "ffn2_w_gate", "ffn2_w_up", "ffn2_w_down",
           "final_norm")
TRANSPOSED = ("ffn1_w_gate", "ffn1_w_up", "ffn2_w_gate", "ffn2_w_up", "w_in", "b_w_uq")
PACK_SHAPE = (8, 1024)


def _pack_small(d, last=None):
    flat = [d[n].reshape(-1) for n in SMALL]
    used = sum(f.shape[0] for f in flat)
    total = PACK_SHAPE[0] * PACK_SHAPE[1]
    tail = jnp.zeros((total - used - 1,), F32)
    end = jnp.zeros((1,), F32) if last is None else last.reshape(1)
    return jnp.concatenate(flat + [tail, end]).reshape(PACK_SHAPE)


def _unpack_small(p, like):
    flat = p.reshape(-1)
    out, off = {}, 0
    for n in SMALL:
        size = like[n].size
        out[n] = flat[off:off + size].reshape(like[n].shape)
        off += size
    return out, flat[-1]


def _cols_to_shards(g):
    rows, cols = g.shape
    return g.reshape(rows, N_SHARD, cols // N_SHARD).transpose(1, 0, 2)


def _shards_to_cols(g):
    return g.transpose(1, 0, 2).reshape(g.shape[1], -1)


def kernel(x, ffn1_norm, ffn1_w_gate, ffn1_w_up, ffn1_w_down, mix_norm, w_in, a_q_norm, a_k_norm, a_rel_bias, b_q_lat_norm, b_w_uq, b_kv_lat_norm, b_w_ukv, b_q_nope_norm, b_q_rope_norm, b_k_nope_norm, b_k_rope_norm, w_out, ffn2_norm, ffn2_w_gate, ffn2_w_up, ffn2_w_down, final_norm, loss_target, m_ffn1_norm, m_ffn1_w_gate, m_ffn1_w_up, m_ffn1_w_down, m_mix_norm, m_w_in, m_a_q_norm, m_a_k_norm, m_a_rel_bias, m_b_q_lat_norm, m_b_w_uq, m_b_kv_lat_norm, m_b_w_ukv, m_b_q_nope_norm, m_b_q_rope_norm, m_b_k_nope_norm, m_b_k_rope_norm, m_w_out, m_ffn2_norm, m_ffn2_w_gate, m_ffn2_w_up, m_ffn2_w_down, m_final_norm, v_ffn1_norm, v_ffn1_w_gate, v_ffn1_w_up, v_ffn1_w_down, v_mix_norm, v_w_in, v_a_q_norm, v_a_k_norm, v_a_rel_bias, v_b_q_lat_norm, v_b_w_uq, v_b_kv_lat_norm, v_b_w_ukv, v_b_q_nope_norm, v_b_q_rope_norm, v_b_k_nope_norm, v_b_k_rope_norm, v_w_out, v_ffn2_norm, v_ffn2_w_gate, v_ffn2_w_up, v_ffn2_w_down, v_final_norm):
    args = locals()
    view = lambda a, n: a[0].T if n in TRANSPOSED else a[0]
    wts = {n: view(args[n], n) for n in WEIGHTS}
    mom = {n: view(args["m_" + n], n) for n in WEIGHTS}
    var = {n: view(args["v_" + n], n) for n in WEIGHTS}

    shard = 2 * lax.axis_index("x") + lax.axis_index("y")
    core = lax.axis_index("c")
    first = [n for n in BIG if n not in LATE]
    own = [wts[n].astype(BF16) for n in first]
    w = {n: wts[n] if n == "a_rel_bias" else wts[n][None] for n in SMALL}
    w.update(kernel_layout({n: lax.dynamic_update_index_in_dim(got, mine, shard, 0)
                            for n, got, mine in zip(first, allgather_shards(own), own)}))

    loss, grad_x, g, landed = local_step(x[0], loss_target[0], w,
                                         late=([wts[n].astype(BF16) for n in LATE], shard))

    me = 2 * shard + core
    for n in first:
        piece = lax.dynamic_slice(g[n], (shard, core * (FS // 2), 0), (1, FS // 2, D_MODEL))
        landed[n] = lax.dynamic_update_slice(landed[n], piece, (me, 0, 0))
    groups = {}
    for n in BIG:
        groups.setdefault(wts[n].shape, []).append(n)
    half = {}
    for names in groups.values():
        half.update(zip(names, sum_slots([landed[n] for n in names], "sum_slots")))
    halves = [half[n] for n in BIG]
    shapes = [wts[n].shape for n in BIG]
    axes = [_split_axis(s) for s in shapes]
    grads = dict(zip(BIG, (lax.dynamic_update_slice_in_dim(got, mine, core * mine.shape[ax], ax)
                           for got, mine, ax in zip(join_halves(halves, shapes), halves, axes))))

    small_sum, loss_sum = _unpack_small(allreduce_small(_pack_small(g, loss[0, 0])), wts)
    grads.update(small_sum)

    delta, new_m, new_v = {}, {}, {}
    for names in groups.values():
        stepped = adamw(*([d[n] for n in names] for d in (wts, grads, mom, var)), "adamw")
        for n, (d_new, m_new, v_new) in zip(names, stepped):
            delta[n], new_m[n], new_v[n] = d_new, m_new, v_new
    (packed,) = adamw([_pack_small(wts)], [_pack_small(grads)], [_pack_small(mom)], [_pack_small(var)], "adamw_small")
    for dst, p in zip((delta, new_m, new_v), packed):
        dst.update(_unpack_small(p, wts)[0])

    lead = lambda d: [(d[n].T if n in TRANSPOSED else d[n])[None] for n in WEIGHTS]
    return (loss_sum, grad_x[None], *lead(grads), *lead(delta), *lead(new_m), *lead(new_v))
```

```python
import numpy as np
import jax
import jax.numpy as jnp
from jax import lax
from jax.experimental import pallas as pl
from jax.experimental.pallas import tpu as pltpu

F32 = jnp.float32
BF16 = jnp.bfloat16
EPS = 1e-6
NEG = -1e30

D_MODEL = 1024
D_FF = 2816
N_SHARD = 4
FS = D_FF // N_SHARD
CHUNK = 64
A_LEFT = 8
A_MAX_REL = 128
HEADS = 8
HD = 64
ROPE = 32
PROJ_W = 2048
IN_COLS = 1952
B_SCALE = 96 ** -0.5
LANES = 128

ADAM_LR = 0.001
ADAM_B1 = 0.9
ADAM_B2 = 0.999
ADAM_EPS = 1e-08
ADAM_WD = 0.01
ADAM_STEP = 10

VMEM_LIMIT = 56 * 1024 * 1024

MESH = pl.DeviceIdType.MESH


def _dot(a, b):
    return lax.dot_general(a, b, (((1,), (0,)), ((), ())), preferred_element_type=F32)


def _dot_nt(a, b):
    return lax.dot_general(a, b, (((1,), (1,)), ((), ())), preferred_element_type=F32)


def _dot_tn(a, b):
    return lax.dot_general(a, b, (((0,), (0,)), ((), ())), preferred_element_type=F32)


def _params(sem):
    return pltpu.CompilerParams(dimension_semantics=sem, vmem_limit_bytes=VMEM_LIMIT)


def _rms(xv):
    r = lax.rsqrt(jnp.mean(xv * xv, axis=-1, keepdims=True) + EPS)
    return r, xv * r


def ffn_fwd(x, g, wg, wu, wd, name, gather=(), loss_head=None):
    t, d = x.shape
    tm = 512
    ni = t // tm
    plan = GatherPlan(gather)
    n = plan.n
    head = () if loss_head is None else tuple(loss_head)
    q = len(head)

    def body(*refs):
        x_ref, g_ref, wg_ref, wu_ref, wd_ref = refs[:5]
        head_in, ins = refs[5:5 + q], refs[5 + q:5 + q + n]
        o_ref, gp_ref, up_ref = refs[5 + q + n:8 + q + n]
        head_out, outs = refs[8 + q + n:8 + 2 * q + n], refs[8 + 2 * q + n:8 + 2 * q + 2 * n]
        h_ref, acc_ref = refs[8 + 2 * q + 2 * n:10 + 2 * q + 2 * n]
        sems = refs[10 + 2 * q + 2 * n:]
        i, j = pl.program_id(0), pl.program_id(1)
        if n:
            pl.when((i == 0) & (j == 0))(lambda: plan.start(ins, outs, sems))
            pl.when((i == (3 * ni) // 4) & (j == 0))(lambda: plan.forward(ins, outs, sems))

        @pl.when(j == 0)
        def _():
            _, xn = _rms(x_ref[...])
            h_ref[...] = (xn * g_ref[...]).astype(BF16)
            acc_ref[...] = jnp.zeros_like(acc_ref)

        h = h_ref[...]
        gp = _dot_nt(h, wg_ref[0])
        up = _dot_nt(h, wu_ref[0])
        gp_ref[0] = gp
        up_ref[0] = up
        a = (gp * jax.nn.sigmoid(gp) * up).astype(BF16)
        acc_ref[...] += _dot(a, wd_ref[0])

        @pl.when(j == N_SHARD - 1)
        def _():
            y = x_ref[...] + 0.5 * acc_ref[...]
            if not q:
                o_ref[...] = y
                return
            (gf_ref, t_ref), (dgf_ref, loss_ref) = head_in, head_out
            r, yn = _rms(y)
            gf = gf_ref[...]
            e = yn * gf - t_ref[...]
            dout = e * (1.0 / d)
            dng = dout * gf
            o_ref[...] = r * (dng - yn * jnp.mean(dng * yn, axis=-1, keepdims=True))

            @pl.when(i == 0)
            def _():
                dgf_ref[...] = jnp.zeros_like(dgf_ref)
                loss_ref[...] = jnp.zeros_like(loss_ref)

            dgf_ref[...] += jnp.sum(dout * yn, axis=0, keepdims=True)
            part = jnp.sum(jnp.sum(e * e, axis=-1, keepdims=True), axis=0, keepdims=True) * (0.5 / d)
            loss_ref[...] += jnp.broadcast_to(part, loss_ref.shape)

        if n:
            pl.when((i == ni - 1) & (j == N_SHARD - 1))(lambda: plan.finish(ins, outs, sems))

    tok = pl.BlockSpec((tm, d), lambda i, j: (i, 0))
    vec = pl.BlockSpec((1, d), lambda i, j: (0, 0))
    chunk = pl.BlockSpec((1, FS, d), lambda i, j: (j, 0, 0))
    pre = pl.BlockSpec((1, tm, FS), lambda i, j: (j, i, 0))
    return pl.pallas_call(
        body, name=name, grid=(ni, N_SHARD),
        in_specs=[tok, vec, chunk, chunk, chunk] + [vec, tok][:q] + [ANY] * n,
        out_specs=[tok, pre, pre] + [vec, pl.BlockSpec((1, LANES), lambda i, j: (0, 0))][:q] + [ANY] * n,
        out_shape=[jax.ShapeDtypeStruct((t, d), F32), jax.ShapeDtypeStruct((N_SHARD, t, FS), F32),
                   jax.ShapeDtypeStruct((N_SHARD, t, FS), F32)]
        + [jax.ShapeDtypeStruct((1, d), F32), jax.ShapeDtypeStruct((1, LANES), F32)][:q] + plan.out_shape,
        scratch_shapes=[pltpu.VMEM((tm, d), BF16), pltpu.VMEM((tm, d), F32)] + (plan.scratch if n else []),
        compiler_params=_params(("arbitrary", "arbitrary")),
    )(x, g, wg, wu, wd, *head, *gather)


def ffn_bwd(x, dout, g, wg, wu, wd, gate, up_pre, name, scatter=(), spread=None):
    t, d = x.shape
    tm = 512
    ni = t // tm
    hf = FS // 2
    plan = ScatterPlan(scatter)
    m = plan.n
    k = 0 if spread is None else 3
    steps = jnp.arange(N_SHARD, dtype=jnp.int32)
    order = steps if spread is None else (spread + 1 + steps) % N_SHARD

    def body(*refs):
        ord_ref, x_ref, do_ref, g_ref, wg_ref, wu_ref, wd_ref, gp_ref, up_ref = refs[:9]
        ins, (dwg_out, dwu_out, dwd_out, dhp_ref) = refs[9:9 + m], refs[9 + m:13 + m]
        outs, lands = refs[13 + m:13 + 2 * m], refs[13 + 2 * m:13 + 2 * m + k]
        dwg_ref, dwu_ref, dwd_ref = refs[13 + 2 * m + k:16 + 2 * m + k]
        sems = refs[16 + 2 * m + k:19 + 2 * m + k] if m else ()
        stage_ref = refs[-3] if k else None
        j, i = pl.program_id(0), pl.program_id(1)
        if m:
            pl.when((j == 0) & (i == 0))(lambda: plan.start(ins, outs, sems))

        def chunk_copies(jj):
            send_sems, recv_sems = refs[-2:]
            px, py, pc = _place()
            me = 4 * px + 2 * py + pc
            tx, ty = ord_ref[jj] // 2, ord_ref[jj] % 2
            copies = []
            for n_ in range(3):
                for h_ in range(2):
                    copies.append((pltpu.make_async_remote_copy(
                        src_ref=stage_ref.at[n_, pl.ds(h_ * hf, hf)], dst_ref=lands[n_].at[me],
                        send_sem=send_sems.at[6 * jj + 2 * n_ + h_], recv_sem=recv_sems.at[3 * me + n_],
                        device_id=(tx, ty, h_), device_id_type=MESH), (tx != px) | (ty != py) | (pc != h_)))
            return copies

        def arrivals():
            send_sems, recv_sems = refs[-2:]
            px, py, pc = _place()
            me = 4 * px + 2 * py + pc
            for s_ in range(N_DEV):
                for n_ in range(3):
                    slot = lands[n_].at[s_]
                    cp = pltpu.make_async_remote_copy(
                        src_ref=slot, dst_ref=slot, send_sem=send_sems.at[0], recv_sem=recv_sems.at[3 * s_ + n_],
                        device_id=(px, py, pc), device_id_type=MESH)
                    pl.when(me != s_)(cp.wait_recv)

        _, xn = _rms(x_ref[...])
        h = (xn * g_ref[...]).astype(BF16)
        dz = (0.5 * do_ref[...]).astype(BF16)
        wgv, wuv, wdv = wg_ref[0], wu_ref[0], wd_ref[0]
        gp, up = gp_ref[0], up_ref[0]
        s = jax.nn.sigmoid(gp)
        sg = gp * s
        a = (sg * up).astype(BF16)
        da = _dot_nt(dz, wdv)
        dup = (da * sg).astype(BF16)
        dgp = (da * up * (s * (1.0 + gp * (1.0 - s)))).astype(BF16)

        @pl.when(i == 0)
        def _():
            dwg_ref[...] = jnp.zeros_like(dwg_ref)
            dwu_ref[...] = jnp.zeros_like(dwu_ref)
            dwd_ref[...] = jnp.zeros_like(dwd_ref)

        dwd_ref[...] += _dot_tn(a, dz)
        dwg_ref[...] += _dot_tn(dgp, h)
        dwu_ref[...] += _dot_tn(dup, h)
        dhp_ref[0] = (_dot(dgp, wgv) + _dot(dup, wuv)).astype(BF16)

        @pl.when(i == ni - 1)
        def _():
            dwg_out[0] = dwg_ref[...].astype(BF16)
            dwu_out[0] = dwu_ref[...].astype(BF16)
            dwd_out[0] = dwd_ref[...].astype(BF16)
            if k:
                @pl.when(j >= 1)
                def _():
                    for cp, leaves in chunk_copies(j - 1):
                        pl.when(leaves)(cp.wait_send)
                for n_, acc in enumerate((dwg_ref, dwu_ref, dwd_ref)):
                    stage_ref[n_] = acc[...].astype(BF16)
                for cp, leaves in chunk_copies(j):
                    pl.when(leaves)(cp.start)

                @pl.when(j == N_SHARD - 1)
                def _():
                    for cp, leaves in chunk_copies(N_SHARD - 1):
                        pl.when(leaves)(cp.wait_send)
                    arrivals()

        if m:
            pl.when((j == N_SHARD - 1) & (i == ni - 1))(lambda: plan.finish(ins, outs, sems))

    chunk = pl.BlockSpec((1, FS, d), lambda j, i, o: (o[j], 0, 0))
    tok = pl.BlockSpec((tm, d), lambda j, i, o: (i, 0))
    pre = pl.BlockSpec((1, tm, FS), lambda j, i, o: (o[j], i, 0))
    grid_spec = pltpu.PrefetchScalarGridSpec(
        num_scalar_prefetch=1, grid=(N_SHARD, ni),
        in_specs=[tok, tok, pl.BlockSpec((1, d), lambda j, i, o: (0, 0)), chunk, chunk, chunk, pre, pre] + [ANY] * m,
        out_specs=[chunk, chunk, chunk, pl.BlockSpec((1, tm, d), lambda j, i, o: (o[j], i, 0))] + [ANY] * (m + k),
        scratch_shapes=[pltpu.VMEM((FS, d), F32), pltpu.VMEM((FS, d), F32), pltpu.VMEM((FS, d), F32)]
        + (plan.scratch if m else [])
        + ([pltpu.VMEM((3, FS, d), BF16), pltpu.SemaphoreType.DMA((6 * N_SHARD,)),
            pltpu.SemaphoreType.DMA((3 * N_DEV,))] if k else []))
    return pl.pallas_call(
        body, name=name, grid_spec=grid_spec,
        out_shape=[jax.ShapeDtypeStruct((N_SHARD, FS, d), BF16),
                   jax.ShapeDtypeStruct((N_SHARD, FS, d), BF16),
                   jax.ShapeDtypeStruct((N_SHARD, FS, d), BF16),
                   jax.ShapeDtypeStruct((N_SHARD, t, d), BF16)] + plan.out_shape
        + [jax.ShapeDtypeStruct((N_DEV, hf, d), BF16)] * k,
        compiler_params=_params(("arbitrary", "arbitrary")),
    )(order, x, dout, g, wg, wu, wd, gate, up_pre, *scatter)


def norm_bwd(x, g, dhp, dres, name):
    t, d = x.shape
    p = dhp.shape[0]
    tm = 512

    def body(x_ref, g_ref, dhp_ref, dres_ref, dx_ref, dg_ref):
        i = pl.program_id(0)
        r, xn = _rms(x_ref[...])
        dh = dhp_ref[0].astype(F32)
        for q in range(1, p):
            dh = dh + dhp_ref[q].astype(F32)
        dhg = dh * g_ref[...]
        dx_ref[...] = dres_ref[...] + r * (dhg - xn * jnp.mean(dhg * xn, axis=-1, keepdims=True))

        @pl.when(i == 0)
        def _():
            dg_ref[...] = jnp.zeros_like(dg_ref)

        dg_ref[...] += jnp.sum(dh * xn, axis=0, keepdims=True)

    return pl.pallas_call(
        body, name=name, grid=(t // tm,),
        in_specs=[pl.BlockSpec((tm, d), lambda i: (i, 0)),
                  pl.BlockSpec((1, d), lambda i: (0, 0)),
                  pl.BlockSpec((p, tm, d), lambda i: (0, i, 0)),
                  pl.BlockSpec((tm, d), lambda i: (i, 0))],
        out_specs=[pl.BlockSpec((tm, d), lambda i: (i, 0)),
                   pl.BlockSpec((1, d), lambda i: (0, 0))],
        out_shape=[jax.ShapeDtypeStruct((t, d), F32), jax.ShapeDtypeStruct((1, d), F32)],
        compiler_params=_params(("arbitrary",)),
    )(x, g, dhp, dres)


def out_proj(x, oa, ob_t, w):
    t, d = x.shape
    half = oa.shape[1]
    tm = 512

    def body(x_ref, oa_ref, obt_ref, w_ref, o_ref):
        o_ref[...] = (x_ref[...] + _dot(oa_ref[...], w_ref[0:half, :])
                      + _dot_tn(obt_ref[...], w_ref[half:2 * half, :]))

    return pl.pallas_call(
        body, name="out_proj", grid=(t // tm,),
        in_specs=[pl.BlockSpec((tm, d), lambda i: (i, 0)),
                  pl.BlockSpec((tm, half), lambda i: (i, 0)),
                  pl.BlockSpec((half, tm), lambda i: (0, i)),
                  pl.BlockSpec((2 * half, d), lambda i: (0, 0))],
        out_specs=pl.BlockSpec((tm, d), lambda i: (i, 0)),
        out_shape=jax.ShapeDtypeStruct((t, d), F32),
        compiler_params=_params(("parallel",)),
    )(x, oa, ob_t, w)


def out_proj_bwd(x, g, dhp, dres, w, oa, ob_t):
    t, d = x.shape
    half = w.shape[0] // 2
    p = dhp.shape[0]
    tm = 512
    ni = t // tm

    def body(x_ref, g_ref, dhp_ref, dres_ref, w_ref, oa_ref, obt_ref,
             dx_ref, dg_ref, da_ref, db_ref, dbt_ref, dw_ref, acc_ref):
        i = pl.program_id(0)

        @pl.when(i == 0)
        def _():
            dg_ref[...] = jnp.zeros_like(dg_ref)
            acc_ref[...] = jnp.zeros_like(acc_ref)

        r, xn = _rms(x_ref[...])
        dh = dhp_ref[0].astype(F32)
        for s in range(1, p):
            dh = dh + dhp_ref[s].astype(F32)
        dhg = dh * g_ref[...]
        dx = dres_ref[...] + r * (dhg - xn * jnp.mean(dhg * xn, axis=-1, keepdims=True))
        dx_ref[...] = dx
        dg_ref[...] += jnp.sum(dh * xn, axis=0, keepdims=True)
        dxb = dx.astype(BF16)
        da_ref[...] = _dot_nt(dxb, w_ref[0:half, :]).astype(BF16)
        db_ref[...] = _dot_nt(dxb, w_ref[half:2 * half, :]).astype(BF16)
        dbt_ref[...] = _dot_nt(w_ref[half:2 * half, :], dxb).astype(BF16)
        acc_ref[0:half, :] += _dot_tn(oa_ref[...], dxb)
        acc_ref[half:2 * half, :] += _dot(obt_ref[...], dxb)

        @pl.when(i == ni - 1)
        def _():
            dw_ref[...] = acc_ref[...].astype(BF16)

    row = lambda w_: pl.BlockSpec((tm, w_), lambda i: (i, 0))
    col = pl.BlockSpec((half, tm), lambda i: (0, i))
    whole = pl.BlockSpec((2 * half, d), lambda i: (0, 0))
    vec = pl.BlockSpec((1, d), lambda i: (0, 0))
    return pl.pallas_call(
        body, name="out_proj_bwd", grid=(ni,),
        in_specs=[row(d), vec, pl.BlockSpec((p, tm, d), lambda i: (0, i, 0)), row(d), whole, row(half), col],
        out_specs=[row(d), vec, row(half), row(half), col, whole],
        out_shape=[jax.ShapeDtypeStruct((t, d), F32), jax.ShapeDtypeStruct((1, d), F32),
                   jax.ShapeDtypeStruct((t, half), BF16), jax.ShapeDtypeStruct((t, half), BF16),
                   jax.ShapeDtypeStruct((half, t), BF16), jax.ShapeDtypeStruct((2 * half, d), BF16)],
        scratch_shapes=[pltpu.VMEM((2 * half, d), F32)],
        compiler_params=_params(("arbitrary",)),
    )(x, g, dhp, dres, w, oa, ob_t)


def _lane(shape):
    return lax.broadcasted_iota(jnp.int32, shape, 1)


PAIR = (0, HD, LANES)
Q_HEAD = (0, HD, HD + ROPE, LANES)
K_ROPE = (0, ROPE, LANES)


def _seg_mean(z, bounds):
    seg = lambda v: sum([(v >= b).astype(jnp.int32) for b in bounds[1:-1]], jnp.zeros_like(v))
    rows = seg(lax.broadcasted_iota(jnp.int32, (LANES, LANES), 0))
    cols = seg(lax.broadcasted_iota(jnp.int32, (LANES, LANES), 1))
    same = (rows == cols).astype(BF16)
    lane = _lane((1, LANES))
    inv = sum([jnp.where((lane >= a) & (lane < b), 1.0 / (b - a), 0.0) for a, b in zip(bounds[:-1], bounds[1:])])
    hi = z.astype(BF16)
    lo = (z - hi.astype(F32)).astype(BF16)
    return (_dot(hi, same) + _dot(lo, same)) * inv


def _seg_norm(x, bounds):
    r = lax.rsqrt(_seg_mean(x * x, bounds) + EPS)
    return r, x * r


def _seg_norm_bwd(r, xn, dyg, bounds):
    return r * (dyg - xn * _seg_mean(dyg * xn, bounds))


A_TM = 256


def mix_fwd(x, g, w, gq, gk, gcq, gckv):
    t, d = x.shape
    tm = A_TM

    def body(x_ref, g_ref, w_ref, gq_ref, gk_ref, gcq_ref, gckv_ref,
             h_ref, p_ref, qa_ref, ka_ref, va_ref, cq_ref, ckv_ref):
        _, xn = _rms(x_ref[...])
        h = (xn * g_ref[...]).astype(BF16)
        h_ref[...] = h
        p_ref[...] = _dot_nt(h, w_ref[...])
        for p in range(4):
            sl = slice(LANES * p, LANES * (p + 1))
            _, xn = _seg_norm(p_ref[:, sl], PAIR)
            qa_ref[:, sl] = (xn * gq_ref[:, sl] * 0.125).astype(BF16)
            _, xn = _seg_norm(p_ref[:, 512 + LANES * p:512 + LANES * (p + 1)], PAIR)
            ka_ref[:, sl] = (xn * gk_ref[:, sl]).astype(BF16)
        va_ref[...] = p_ref[:, 1024:1536].astype(BF16)
        _, xn = _rms(p_ref[:, 1536:1792])
        cq_ref[...] = (xn * gcq_ref[...]).astype(BF16)
        _, xn = _rms(p_ref[:, 1792:1920])
        ckv_ref[...] = (xn * gckv_ref[...]).astype(BF16)

    row = lambda w: pl.BlockSpec((tm, w), lambda i: (i, 0))
    vec = lambda w: pl.BlockSpec((1, w), lambda i: (0, 0))
    return pl.pallas_call(
        body, name="mix_fwd", grid=(t // tm,),
        in_specs=[row(d), vec(d), pl.BlockSpec((PROJ_W, d), lambda i: (0, 0)), vec(512), vec(512), vec(256), vec(128)],
        out_specs=[row(d), row(PROJ_W), row(512), row(512), row(512), row(256), row(128)],
        out_shape=[jax.ShapeDtypeStruct((t, d), BF16), jax.ShapeDtypeStruct((t, PROJ_W), F32)]
        + [jax.ShapeDtypeStruct((t, w_), BF16) for w_ in (512, 512, 512, 256, 128)],
        compiler_params=_params(("parallel",)),
    )(x, g, w, gq, gk, gcq, gckv)


def mix_bwd(proj, x, h, dres, w, g, dqa, dkp, dvp, dcq, dckv, dkr, gq, gk, gcq, gckv):
    t, d = x.shape
    tm = A_TM
    nb = t // tm

    def body(p_ref, x_ref, h_ref, dres_ref, w_ref, g_ref, dqa_ref, dk0_ref, dk1_ref, dk2_ref, dv0_ref, dv1_ref,
             dv2_ref, dcq_ref, dckv_ref, dkr_ref, gq_ref, gk_ref, gcq_ref, gckv_ref,
             dx_ref, dw_ref, dg_ref, dgq_ref, dgk_ref, dgcq_ref, dgckv_ref, dp_ref, acc_ref):
        i = pl.program_id(0)

        @pl.when(i == 0)
        def _():
            for ref in (acc_ref, dg_ref, dgq_ref, dgk_ref, dgcq_ref, dgckv_ref):
                ref[...] = jnp.zeros_like(ref)

        has1 = (i + 1 < nb).astype(F32)
        has2 = (i + 2 < nb).astype(F32)
        for p in range(4):
            sl = slice(LANES * p, LANES * (p + 1))
            r, xn = _seg_norm(p_ref[:, sl], PAIR)
            dy = dqa_ref[:, sl] * 0.125
            dp_ref[:, sl] = _seg_norm_bwd(r, xn, dy * gq_ref[:, sl], PAIR).astype(BF16)
            dgq_ref[:, sl] += jnp.sum(dy * xn, axis=0, keepdims=True)
            ks = slice(512 + LANES * p, 512 + LANES * (p + 1))
            r, xn = _seg_norm(p_ref[:, ks], PAIR)
            dy = dk0_ref[0, :, sl] + has1 * dk1_ref[0, :, sl] + has2 * dk2_ref[0, :, sl]
            dp_ref[:, ks] = _seg_norm_bwd(r, xn, dy * gk_ref[:, sl], PAIR).astype(BF16)
            dgk_ref[:, sl] += jnp.sum(dy * xn, axis=0, keepdims=True)
        dp_ref[:, 1024:1536] = (dv0_ref[0] + has1 * dv1_ref[0] + has2 * dv2_ref[0]).astype(BF16)
        for (a, b, dlat_ref, glat_ref, dglat_ref) in ((1536, 1792, dcq_ref, gcq_ref, dgcq_ref),
                                                      (1792, 1920, dckv_ref, gckv_ref, dgckv_ref)):
            r, xn = _rms(p_ref[:, a:b])
            dy = dlat_ref[...]
            dyg = dy * glat_ref[...]
            dp_ref[:, a:b] = (r * (dyg - xn * jnp.mean(dyg * xn, axis=-1, keepdims=True))).astype(BF16)
            dglat_ref[...] += jnp.sum(dy * xn, axis=0, keepdims=True)
        dp_ref[:, 1920:2048] = dkr_ref[...].astype(BF16)

        dproj = dp_ref[...]
        acc_ref[...] += _dot_tn(dproj, h_ref[...])
        dh = _dot(dproj, w_ref[...])
        r, xn = _rms(x_ref[...])
        dhg = dh * g_ref[...]
        dx_ref[...] = dres_ref[...] + r * (dhg - xn * jnp.mean(dhg * xn, axis=-1, keepdims=True))
        dg_ref[...] += jnp.sum(dh * xn, axis=0, keepdims=True)

        @pl.when(i == nb - 1)
        def _():
            dw_ref[...] = acc_ref[...].astype(BF16)

    row = lambda w_: pl.BlockSpec((tm, w_), lambda i: (i, 0))
    vec = lambda w_: pl.BlockSpec((1, w_), lambda i: (0, 0))
    part = lambda s: pl.BlockSpec((1, tm, 512), lambda i: (s, jnp.minimum(i + s, nb - 1), 0))
    whole = pl.BlockSpec((PROJ_W, d), lambda i: (0, 0))
    return pl.pallas_call(
        body, name="mix_bwd", grid=(nb,),
        in_specs=[row(PROJ_W), row(d), row(d), row(d), whole, vec(d), row(512), part(0), part(1), part(2),
                  part(0), part(1), part(2), row(256), row(128), row(128), vec(512), vec(512), vec(256), vec(128)],
        out_specs=[row(d), whole, vec(d), vec(512), vec(512), vec(256), vec(128)],
        out_shape=[jax.ShapeDtypeStruct((t, d), F32), jax.ShapeDtypeStruct((PROJ_W, d), BF16)]
        + [jax.ShapeDtypeStruct((1, w_), F32) for w_ in (d, 512, 512, 256, 128)],
        scratch_shapes=[pltpu.VMEM((tm, PROJ_W), BF16), pltpu.VMEM((PROJ_W, d), F32)],
        compiler_params=_params(("arbitrary",)),
    )(proj, x, h, dres, w, g, dqa, dkp, dkp, dkp, dvp, dvp, dvp, dcq, dckv, dkr, gq, gk, gcq, gckv)


def _roll(x, shift):
    return pltpu.roll(x, shift % LANES, 1)


def _rope(y, c, s1, s2):
    return y * c + _roll(y, -16) * s1 + _roll(y, 16) * s2


def _rope_bwd(d, c, s1, s2):
    return d * c + _roll(d * s1, 16) + _roll(d * s2, -16)


def _up_proj(cq_ref, ckv_ref, wuq_ref, wukv_ref):
    return _dot_nt(cq_ref[...], wuq_ref[...]), _dot(ckv_ref[...], wukv_ref[...])


def prep2_fwd(cqn, ckvn, proj, wuq, wukv, gq, gk, gkr, tabs):
    t = cqn.shape[0]
    tm = A_TM

    def body(cq_ref, ckv_ref, kr_ref, wuq_ref, wukv_ref, gq_ref, gk_ref, gkr_ref, tab_ref, qf_ref, kf_ref, vp_ref):
        q_all, kv_all = _up_proj(cq_ref, ckv_ref, wuq_ref, wukv_ref)
        _, xn = _seg_norm(kr_ref[...], K_ROPE)
        kpe = _roll(_rope(xn * gkr_ref[...], tab_ref[3], tab_ref[4], tab_ref[5]), 64)
        for h in range(HEADS):
            sl = slice(LANES * h, LANES * (h + 1))
            _, xn = _seg_norm(q_all[:, sl], Q_HEAD)
            qf_ref[:, sl] = (_rope(xn * gq_ref[...], tab_ref[0], tab_ref[1], tab_ref[2]) * B_SCALE2).astype(BF16)
            x = kv_all[:, sl]
            lo = _lane(x.shape) < HD
            _, xkn = _seg_norm(jnp.where(lo, x, 0.0), PAIR)
            kf_ref[:, sl] = (xkn * gk_ref[...] + kpe).astype(BF16)
            if h % 2 == 0:
                v_even = _roll(x, 64)
            else:
                vp_ref[:, LANES * (h // 2):LANES * (h // 2 + 1)] = jnp.where(lo, v_even, x).astype(BF16)

    row = lambda w: pl.BlockSpec((tm, w), lambda i: (i, 0))
    vec = lambda w: pl.BlockSpec((1, w), lambda i: (0, 0))
    full = lambda a: pl.BlockSpec(a.shape, lambda i: (0, 0))
    return pl.pallas_call(
        body, name="prep2_fwd", grid=(t // tm,),
        in_specs=[row(256), row(128), pl.BlockSpec((tm, LANES), lambda i: (i, 15)), full(wuq), full(wukv),
                  vec(128), vec(128), vec(128), pl.BlockSpec((6, tm, LANES), lambda i: (0, i, 0))],
        out_specs=[row(1024), row(1024), row(512)],
        out_shape=[jax.ShapeDtypeStruct((t, 1024), BF16), jax.ShapeDtypeStruct((t, 1024), BF16),
                   jax.ShapeDtypeStruct((t, 512), BF16)],
        compiler_params=_params(("parallel",)),
    )(cqn, ckvn, proj, wuq, wukv, gq, gk, gkr, tabs)


def prep2_bwd(cqn, ckvn, proj, wuq, wukv, dqf, dkf, dvp, gq, gk, gkr, tabs):
    t = cqn.shape[0]
    tm = A_TM

    def body(cq_ref, ckv_ref, kr_ref, wuq_ref, wukv_ref, dqf_ref, dkf_ref, dvp_ref, gq_ref, gk_ref, gkr_ref, tab_ref,
             dcq_ref, dckv_ref, dkr_ref, dwuq_ref, dwukv_ref, dgq_ref, dgk_ref, dgkr_ref, dq_ref, dkv_ref):
        i = pl.program_id(0)

        @pl.when(i == 0)
        def _():
            for ref in (dwuq_ref, dwukv_ref, dgq_ref, dgk_ref, dgkr_ref):
                ref[...] = jnp.zeros_like(ref)

        q_all, kv_all = _up_proj(cq_ref, ckv_ref, wuq_ref, wukv_ref)
        dgq = jnp.zeros((1, LANES), F32)
        dgk = jnp.zeros((1, LANES), F32)
        dkpe = jnp.zeros((tm, LANES), F32)
        for h in range(HEADS):
            sl = slice(LANES * h, LANES * (h + 1))
            lane = _lane((tm, LANES))
            mn, mr = lane < HD, (lane >= HD) & (lane < HD + ROPE)
            r, xn = _seg_norm(q_all[:, sl], Q_HEAD)
            dy = _rope_bwd(dqf_ref[sl, :].T, tab_ref[0], tab_ref[1], tab_ref[2])
            dyg = dy * gq_ref[...]
            dq_ref[:, sl] = _seg_norm_bwd(r, xn, dyg, Q_HEAD).astype(BF16)
            dgq = dgq + jnp.sum(dy * xn, axis=0, keepdims=True)

            x = kv_all[:, sl]
            dk = dkf_ref[:, sl]
            rk, xkn = _seg_norm(jnp.where(mn, x, 0.0), PAIR)
            dyk = jnp.where(mn, dk, 0.0)
            dxk = _seg_norm_bwd(rk, xkn, dyk * gk_ref[...], PAIR)
            dgk = dgk + jnp.sum(dyk * xkn, axis=0, keepdims=True)
            dkpe = dkpe + jnp.where(mr, dk, 0.0)
            dvpair = dvp_ref[:, LANES * (h // 2):LANES * (h // 2 + 1)]
            dv = _roll(dvpair, 64) if h % 2 == 0 else dvpair
            dkv_ref[:, sl] = jnp.where(mn, dxk, dv).astype(BF16)

        r, xn = _seg_norm(kr_ref[...], K_ROPE)
        dy = _rope_bwd(_roll(dkpe, 64), tab_ref[3], tab_ref[4], tab_ref[5])
        dkr_ref[...] = _seg_norm_bwd(r, xn, dy * gkr_ref[...], K_ROPE)
        dgq_ref[...] += dgq
        dgk_ref[...] += dgk
        dgkr_ref[...] += jnp.sum(dy * xn, axis=0, keepdims=True)
        dqb, dkvb = dq_ref[...], dkv_ref[...]
        dcq_ref[...] = _dot(dqb, wuq_ref[...])
        dckv_ref[...] = _dot_nt(dkvb, wukv_ref[...])
        dwuq_ref[...] += _dot_tn(dqb, cq_ref[...])
        dwukv_ref[...] += _dot_tn(ckv_ref[...], dkvb)

    row = lambda w: pl.BlockSpec((tm, w), lambda i: (i, 0))
    vec = lambda w: pl.BlockSpec((1, w), lambda i: (0, 0))
    full = lambda a: pl.BlockSpec(a.shape, lambda i: (0, 0))
    return pl.pallas_call(
        body, name="prep2_bwd", grid=(t // tm,),
        in_specs=[row(256), row(128), pl.BlockSpec((tm, LANES), lambda i: (i, 15)), full(wuq), full(wukv),
                  pl.BlockSpec((1024, tm), lambda i: (0, i)), row(1024), row(512),
                  vec(128), vec(128), vec(128), pl.BlockSpec((6, tm, LANES), lambda i: (0, i, 0))],
        out_specs=[row(256), row(128), row(128), full(wuq), full(wukv), vec(128), vec(128), vec(128)],
        out_shape=[jax.ShapeDtypeStruct((t, 256), F32), jax.ShapeDtypeStruct((t, 128), F32),
                   jax.ShapeDtypeStruct((t, LANES), F32), jax.ShapeDtypeStruct(wuq.shape, F32),
                   jax.ShapeDtypeStruct(wukv.shape, F32)] + [jax.ShapeDtypeStruct((1, LANES), F32)] * 3,
        scratch_shapes=[pltpu.VMEM((tm, 1024), BF16), pltpu.VMEM((tm, 1024), BF16)],
        compiler_params=_params(("arbitrary",)),
    )(cqn, ckvn, proj, wuq, wukv, dqf, dkf, dvp, gq, gk, gkr, tabs)


A_TQ = 256
A_WIN = 3 * A_TQ


def _a_specs(t):
    nb = t // A_TQ
    blk = lambda s: pl.BlockSpec((A_TQ, 512), lambda i: (jnp.maximum(i - s, 0), 0))
    return nb, blk


def _a_exp(q_ref, kc, b_ref, head, sl, lo):
    hm = lo if head % 2 == 0 else ~lo
    qm = jnp.where(hm, q_ref[:, sl], jnp.zeros((), BF16))
    s = _dot_nt(qm, kc) + b_ref[0, head]
    e = jnp.exp(s - jnp.max(s, axis=-1, keepdims=True))
    return hm, qm, e, 1.0 / jnp.sum(e, axis=-1, keepdims=True)


def _a_bias_spec():
    return pl.BlockSpec((1, HEADS, A_TQ, A_WIN), lambda i: (jnp.minimum(i, 2), 0, 0, 0))


def attn_a_fwd(qa, ka, va, bias):
    t = qa.shape[0]
    nb, blk = _a_specs(t)

    def body(q_ref, k2_ref, k1_ref, k0_ref, v2_ref, v1_ref, v0_ref, b_ref, o_ref):
        lo = _lane((A_TQ, LANES)) < HD
        for p in range(4):
            sl = slice(LANES * p, LANES * (p + 1))
            kc = jnp.concatenate([k2_ref[:, sl], k1_ref[:, sl], k0_ref[:, sl]], axis=0)
            vc = jnp.concatenate([v2_ref[:, sl], v1_ref[:, sl], v0_ref[:, sl]], axis=0)
            outs = []
            for h2 in range(2):
                _, _, e, inv = _a_exp(q_ref, kc, b_ref, 2 * p + h2, sl, lo)
                outs.append(_dot(e.astype(BF16), vc) * inv)
            o_ref[:, sl] = jnp.where(lo, outs[0], outs[1]).astype(BF16)

    return pl.pallas_call(
        body, name="attn_a_fwd", grid=(nb,),
        in_specs=[blk(0), blk(2), blk(1), blk(0), blk(2), blk(1), blk(0), _a_bias_spec()],
        out_specs=pl.BlockSpec((A_TQ, 512), lambda i: (i, 0)),
        out_shape=jax.ShapeDtypeStruct((t, 512), BF16),
        compiler_params=_params(("parallel",)),
    )(qa, ka, ka, ka, va, va, va, bias)


def attn_a_bwd(qa, ka, va, bias, do):
    t = qa.shape[0]
    nb, blk = _a_specs(t)

    def body(q_ref, k2_ref, k1_ref, k0_ref, v2_ref, v1_ref, v0_ref, b_ref, do_ref, dq_ref, dk_ref, dv_ref, db_ref):
        qb = pl.program_id(0)

        @pl.when(qb == 0)
        def _():
            db_ref[...] = jnp.zeros_like(db_ref)

        lo = _lane((A_TQ, LANES)) < HD
        for p in range(4):
            sl = slice(LANES * p, LANES * (p + 1))
            kc = jnp.concatenate([k2_ref[:, sl], k1_ref[:, sl], k0_ref[:, sl]], axis=0)
            vc = jnp.concatenate([v2_ref[:, sl], v1_ref[:, sl], v0_ref[:, sl]], axis=0)
            dqs = []
            dkt = jnp.zeros((LANES, A_WIN), F32)
            dvt = jnp.zeros((LANES, A_WIN), F32)
            for h2 in range(2):
                head = 2 * p + h2
                hm, qm, e, inv = _a_exp(q_ref, kc, b_ref, head, sl, lo)
                pr = e * inv
                dom = jnp.where(hm, do_ref[:, sl], jnp.zeros((), BF16))
                dp = _dot_nt(dom, vc)
                ds = pr * (dp - jnp.sum(pr * dp, axis=-1, keepdims=True))
                db_ref[head] += ds
                dsb = ds.astype(BF16)
                dqs.append(_dot(dsb, kc))
                dkt = dkt + _dot_tn(qm, dsb)
                dvt = dvt + _dot_tn(dom, pr.astype(BF16))
            dq_ref[:, sl] = jnp.where(lo, dqs[0], dqs[1])
            dkc, dvc = dkt.T, dvt.T
            for s in range(3):
                rows = slice(A_TQ * (2 - s), A_TQ * (3 - s))
                dk_ref[s, :, sl] = dkc[rows]
                dv_ref[s, :, sl] = dvc[rows]

    share = pl.BlockSpec((3, A_TQ, 512), lambda i: (0, i, 0))
    return pl.pallas_call(
        body, name="attn_a_bwd", grid=(nb,),
        in_specs=[blk(0), blk(2), blk(1), blk(0), blk(2), blk(1), blk(0), _a_bias_spec(), blk(0)],
        out_specs=[pl.BlockSpec((A_TQ, 512), lambda i: (i, 0)), share, share,
                   pl.BlockSpec((HEADS, A_TQ, A_WIN), lambda i: (0, 0, 0))],
        out_shape=[jax.ShapeDtypeStruct((t, 512), F32), jax.ShapeDtypeStruct((3, t, 512), F32),
                   jax.ShapeDtypeStruct((3, t, 512), F32), jax.ShapeDtypeStruct((HEADS, A_TQ, A_WIN), F32)],
        compiler_params=_params(("arbitrary",)),
    )(qa, ka, ka, ka, va, va, va, bias, do)


B_T = 1024


B_SCALE2 = B_SCALE * 1.4426950408889634
_B_ALL = slice(0, B_T)
_B_LO, _B_HI = slice(0, B_T // 2), slice(B_T // 2, B_T)
_B_DIAG = ((_B_LO, _B_LO), (_B_LO, _B_HI), (_B_HI, _B_HI))


def _tri_tables(n, by_query):
    pairs = [(i, j) for i in range(n) for j in range(i + 1)] if by_query else [(i, j) for j in range(n) for i in range(j, n)]
    return (np.asarray([p[0] for p in pairs], np.int32), np.asarray([p[1] for p in pairs], np.int32))


def _b_mask_t(s):
    kc = lax.broadcasted_iota(jnp.int32, s.shape, 0) // CHUNK
    qc = lax.broadcasted_iota(jnp.int32, s.shape, 1) // CHUNK
    return jnp.where(kc <= qc, s, NEG)


def attn_b_fwd(qf, kf, vp):
    t = qf.shape[0]
    n = t // B_T
    qtab, ktab = _tri_tables(n, by_query=True)

    def body(qt_ref, kt_ref, q_ref, k_ref, v_ref, o_ref, lse_ref, m_s, l_s, acc_s):
        qb, kb = qt_ref[pl.program_id(1)], kt_ref[pl.program_id(1)]

        @pl.when(kb == 0)
        def _():
            m_s[...] = jnp.full_like(m_s, NEG)
            l_s[...] = jnp.zeros_like(l_s)
            acc_s[...] = jnp.zeros_like(acc_s)

        def block(kr, qr, masked):
            v = v_ref[kr, :]
            for h2 in range(2):
                sl = slice(LANES * h2, LANES * (h2 + 1))
                s = _dot_nt(k_ref[kr, sl], q_ref[qr, sl])
                if masked:
                    s = _b_mask_t(s)
                m_prev = m_s[h2, :, qr]
                m_new = jnp.maximum(m_prev, jnp.max(s, axis=0, keepdims=True))
                alpha = jnp.exp2(m_prev - m_new)
                pr = jnp.exp2(s - m_new)
                l_s[h2, :, qr] = alpha * l_s[h2, :, qr] + jnp.sum(pr, axis=0, keepdims=True)
                acc_s[h2, :, qr] = alpha * acc_s[h2, :, qr] + _dot_tn(v, pr.astype(BF16))
                m_s[h2, :, qr] = m_new

        @pl.when(kb < qb)
        def _():
            block(_B_ALL, _B_ALL, False)

        @pl.when(kb == qb)
        def _():
            for kr, qr in _B_DIAG:
                block(kr, qr, kr == qr)
            for h2 in range(2):
                l = l_s[h2]
                rows = slice(HD * h2, HD * (h2 + 1))
                o_ref[rows, :] = (acc_s[h2, rows, :] * (1.0 / l)).astype(BF16)
                lse_ref[0, h2:h2 + 1, :] = m_s[h2] + jnp.log2(l)

    grid_spec = pltpu.PrefetchScalarGridSpec(
        num_scalar_prefetch=2, grid=(4, len(qtab)),
        in_specs=[pl.BlockSpec((B_T, 256), lambda p, s, qt, kt: (qt[s], p)),
                  pl.BlockSpec((B_T, 256), lambda p, s, qt, kt: (kt[s], p)),
                  pl.BlockSpec((B_T, LANES), lambda p, s, qt, kt: (kt[s], p))],
        out_specs=[pl.BlockSpec((LANES, B_T), lambda p, s, qt, kt: (p, qt[s])),
                   pl.BlockSpec((1, 2, B_T), lambda p, s, qt, kt: (p, 0, qt[s]))],
        scratch_shapes=[pltpu.VMEM((2, 1, B_T), F32), pltpu.VMEM((2, 1, B_T), F32), pltpu.VMEM((2, LANES, B_T), F32)])
    return pl.pallas_call(
        body, name="attn_b_fwd", grid_spec=grid_spec,
        out_shape=[jax.ShapeDtypeStruct((512, t), BF16), jax.ShapeDtypeStruct((4, 2, t), F32)],
        compiler_params=_params(("parallel", "arbitrary")),
    )(jnp.asarray(qtab), jnp.asarray(ktab), qf, kf, vp)


def attn_b_bwd(qf, kf, vp, do, do_t, o_t, lse, scatter=()):
    t = qf.shape[0]
    n = t // B_T
    qtab, ktab = _tri_tables(n, by_query=False)
    plan = ScatterPlan(scatter)
    m = plan.n
    last = len(qtab) - 1

    def body(*refs):
        qt_ref, kt_ref, q_ref, k_ref, v_ref, do_ref, dot_ref, ot_ref, lse_ref = refs[:9]
        ins, (dq_ref, dk_ref, dv_ref), outs = refs[9:9 + m], refs[9 + m:12 + m], refs[12 + m:12 + 2 * m]
        sems = refs[12 + 2 * m:]
        qb, kb = qt_ref[pl.program_id(1)], kt_ref[pl.program_id(1)]
        if m:
            pl.when((pl.program_id(0) == 0) & (pl.program_id(1) == 0))(lambda: plan.start(ins, outs, sems))

        @pl.when(pl.program_id(1) == 0)
        def _():
            dq_ref[...] = jnp.zeros_like(dq_ref)

        @pl.when(qb == kb)
        def _():
            dk_ref[...] = jnp.zeros_like(dk_ref)
            dv_ref[...] = jnp.zeros_like(dv_ref)

        def block(kr, qr, masked):
            nq = qr.stop - qr.start
            cols = pl.ds(pl.multiple_of(qb * B_T + qr.start, LANES), nq)
            v = v_ref[kr, :]
            dov = do_ref[qr, :]
            prod = dot_ref[:, qr].astype(F32) * ot_ref[:, qr].astype(F32)
            lo = _lane((nq, LANES)) < HD
            for h2 in range(2):
                sl = slice(LANES * h2, LANES * (h2 + 1))
                hm = lo if h2 == 0 else ~lo
                q = q_ref[qr, sl]
                k = k_ref[kr, sl]
                dom = jnp.where(hm, dov, jnp.zeros((), BF16))
                delta = jnp.sum(prod[HD * h2:HD * (h2 + 1), :], axis=0, keepdims=True)
                s = _dot_nt(k, q)
                if masked:
                    s = _b_mask_t(s)
                pr = jnp.exp2(s - lse_ref[0, h2:h2 + 1, qr])
                dp = _dot_nt(v, dom)
                ds = (pr * (dp - delta)).astype(BF16)
                dk_ref[kr, sl] += _dot(ds, q) * (B_SCALE / B_SCALE2)
                dv_ref[kr, :] += _dot(pr.astype(BF16), dom)
                dq_ref[sl, cols] += _dot_tn(k, ds) * B_SCALE

        @pl.when(qb > kb)
        def _():
            block(_B_ALL, _B_ALL, False)

        @pl.when(qb == kb)
        def _():
            for kr, qr in _B_DIAG:
                block(kr, qr, kr == qr)

        if m:
            pl.when((pl.program_id(0) == 3) & (pl.program_id(1) == last))(lambda: plan.finish(ins, outs, sems))

    qrow = lambda w: pl.BlockSpec((B_T, w), lambda p, s, qt, kt: (qt[s], p))
    qcol = pl.BlockSpec((LANES, B_T), lambda p, s, qt, kt: (p, qt[s]))
    krow = lambda w: pl.BlockSpec((B_T, w), lambda p, s, qt, kt: (kt[s], p))
    grid_spec = pltpu.PrefetchScalarGridSpec(
        num_scalar_prefetch=2, grid=(4, len(qtab)),
        in_specs=[qrow(256), krow(256), krow(LANES), qrow(LANES), qcol, qcol,
                  pl.BlockSpec((1, 2, B_T), lambda p, s, qt, kt: (p, 0, qt[s]))] + [ANY] * m,
        out_specs=[pl.BlockSpec((256, t), lambda p, s, qt, kt: (p, 0)), krow(256), krow(LANES)] + [ANY] * m,
        scratch_shapes=plan.scratch if m else [])
    return pl.pallas_call(
        body, name="attn_b_bwd", grid_spec=grid_spec,
        out_shape=[jax.ShapeDtypeStruct((1024, t), F32), jax.ShapeDtypeStruct((t, 1024), F32),
                   jax.ShapeDtypeStruct((t, 512), F32)] + plan.out_shape,
        compiler_params=_params(("arbitrary", "arbitrary")),
    )(jnp.asarray(qtab), jnp.asarray(ktab), qf, kf, vp, do, do_t, o_t, lse, *scatter)


_U_LEN = A_TQ + A_WIN - 1


def _band_mask():
    a = np.arange(A_TQ)[:, None] // CHUNK
    b = np.arange(A_WIN)[None, :] // CHUNK
    return (b >= a) & (b <= a + A_LEFT)


def bias_block(table):
    h = table.shape[0]
    n_lo = A_WIN - 1 - 2 * A_TQ - A_MAX_REL
    ext = jnp.concatenate([jnp.repeat(table[:, :1], n_lo, axis=1), table,
                           jnp.repeat(table[:, -1:], _U_LEN - n_lo - table.shape[1], axis=1)], axis=1)
    row = jnp.pad(ext[:, ::-1], ((0, 0), (0, 1)))[:, None, :]
    band = _band_mask()
    first = [band & (np.arange(A_WIN)[None, :] >= 2 * A_TQ - A_TQ * v) for v in range(3)]
    keep = jnp.asarray(np.stack(first), jnp.int32)

    def body(r_ref, k_ref, o_ref):
        rows = jnp.broadcast_to(r_ref[0], (A_TQ, _U_LEN + 1))
        skew = pltpu.roll(rows, _U_LEN + 1 - (A_TQ - 1), 1, stride=1, stride_axis=0)
        toep = skew[:, :A_WIN]
        for v in range(3):
            o_ref[v, 0] = jnp.where(k_ref[v] != 0, toep, NEG)

    return pl.pallas_call(
        body, name="bias_block", grid=(h,),
        in_specs=[pl.BlockSpec((1, 1, _U_LEN + 1), lambda i: (i, 0, 0)),
                  pl.BlockSpec((3, A_TQ, A_WIN), lambda i: (0, 0, 0))],
        out_specs=pl.BlockSpec((3, 1, A_TQ, A_WIN), lambda i: (0, i, 0, 0)),
        out_shape=jax.ShapeDtypeStruct((3, h, A_TQ, A_WIN), F32),
        compiler_params=_params(("parallel",)),
    )(row, keep)


def bias_block_grad(db):
    h = db.shape[0]
    n_lo = A_WIN - 1 - 2 * A_TQ - A_MAX_REL
    skew = jnp.pad(db, ((0, 0), (0, 0), (A_TQ - 1, 0)))
    flat = jnp.pad(skew.reshape(h, A_TQ * _U_LEN), ((0, 0), (0, A_TQ)))
    ext = jnp.sum(flat.reshape(h, A_TQ, _U_LEN + 1), axis=1)[:, :_U_LEN][:, ::-1]
    n_tab = 2 * A_MAX_REL + 1
    first = jnp.sum(ext[:, :n_lo + 1], axis=1, keepdims=True)
    last = jnp.sum(ext[:, n_lo + n_tab - 1:], axis=1, keepdims=True)
    return jnp.concatenate([first, ext[:, n_lo + 1:n_lo + n_tab - 1], last], axis=1)


def rope_tabs(t):
    inv = 1.0 / (10000.0 ** (jnp.arange(0, ROPE, 2, dtype=F32) / ROPE))
    ang = jnp.arange(t, dtype=F32)[:, None] * inv[None, :]
    cos, sin = jnp.cos(ang), jnp.sin(ang)
    z = lambda w: jnp.zeros((t, w), F32)
    ck = jnp.concatenate([cos, cos, z(96)], axis=1)
    s1k = jnp.concatenate([-sin, z(112)], axis=1)
    s2k = jnp.concatenate([z(16), sin, z(96)], axis=1)
    cq = jnp.concatenate([jnp.ones((t, HD), F32), cos, cos, z(32)], axis=1)
    s1q = jnp.concatenate([z(HD), -sin, z(48)], axis=1)
    s2q = jnp.concatenate([z(HD + 16), sin, z(32)], axis=1)
    return jnp.stack([cq, s1q, s2q, ck, s1k, s2k])


def _pad_lanes(v, width):
    return jnp.pad(v, ((0, 0), (0, width - v.shape[1])))


LATE = ("w_in", "b_w_uq", "b_w_ukv", "w_out", "ffn2_w_gate", "ffn2_w_up", "ffn2_w_down")
FFN2 = ("ffn2_w_gate", "ffn2_w_up", "ffn2_w_down")


def kernel_layout(gathered):
    w = {n: v for n, v in gathered.items() if n.startswith("ffn")}
    if "w_in" in gathered:
        w["w_in"] = jnp.pad(gathered["w_in"].reshape(IN_COLS, D_MODEL), ((0, PROJ_W - IN_COLS), (0, 0)))
        uq = gathered["b_w_uq"].reshape(HEADS, HD + ROPE, 256)
        w["b_w_uq"] = jnp.pad(uq, ((0, 0), (0, LANES - HD - ROPE), (0, 0))).reshape(HEADS * LANES, 256)
        w["b_w_ukv"] = _shards_to_cols(gathered["b_w_ukv"])
        w["w_out"] = gathered["w_out"].reshape(N_SHARD * gathered["w_out"].shape[1], D_MODEL)
    return w


def local_step(x, target, w, late=None):
    t = x.shape[0]
    gq = jnp.tile(w["a_q_norm"], (1, HEADS))
    gk = jnp.tile(w["a_k_norm"], (1, HEADS))
    gq128 = _pad_lanes(jnp.concatenate([w["b_q_nope_norm"], w["b_q_rope_norm"]], axis=1), LANES)
    gk128 = _pad_lanes(w["b_k_nope_norm"], LANES)
    gkr128 = _pad_lanes(w["b_k_rope_norm"], LANES)
    tabs = rope_tabs(t)
    bias = bias_block(w["a_rel_bias"])

    if late is None:
        x1, gate1, up1 = ffn_fwd(x, w["ffn1_norm"], w["ffn1_w_gate"], w["ffn1_w_up"], w["ffn1_w_down"], "ffn_fwd")
    else:
        own, shard = late
        x1, gate1, up1, *got = ffn_fwd(x, w["ffn1_norm"], w["ffn1_w_gate"], w["ffn1_w_up"], w["ffn1_w_down"],
                                       "ffn_fwd_gather", gather=own)
        w = dict(w, **kernel_layout({n: lax.dynamic_update_index_in_dim(g_, o_, shard, 0)
                                     for n, g_, o_ in zip(LATE, got, own)}))
    h, proj, qa, ka, va, cqn, ckvn = mix_fwd(x1, w["mix_norm"], w["w_in"], gq, gk,
                                             w["b_q_lat_norm"], w["b_kv_lat_norm"])
    qf, kf, vp = prep2_fwd(cqn, ckvn, proj, w["b_w_uq"], w["b_w_ukv"], gq128, gk128, gkr128, tabs)
    oa = attn_a_fwd(qa, ka, va, bias)
    ob_t, lse = attn_b_fwd(qf, kf, vp)
    x2 = out_proj(x1, oa, ob_t, w["w_out"])
    g = {}
    dx3, gate2, up2, g["final_norm"], loss = ffn_fwd(
        x2, w["ffn2_norm"], w["ffn2_w_gate"], w["ffn2_w_up"], w["ffn2_w_down"], "ffn_fwd_loss",
        loss_head=(w["final_norm"], target))
    g["ffn2_w_gate"], g["ffn2_w_up"], g["ffn2_w_down"], dhp = ffn_bwd(
        x2, dx3, w["ffn2_norm"], w["ffn2_w_gate"], w["ffn2_w_up"], w["ffn2_w_down"], gate2, up2, "ffn_bwd")
    dx2, g["ffn2_norm"], d_oa, d_ob, d_ob_t, g["w_out"] = out_proj_bwd(
        x2, w["ffn2_norm"], dhp, dx3, w["w_out"], oa, ob_t)
    early = [g[n] for n in FFN2] + [g["w_out"].reshape(N_SHARD, -1, D_MODEL)]
    dqf, dkf, dvp, *landed_early = attn_b_bwd(qf, kf, vp, d_ob, d_ob_t, ob_t, lse,
                                              scatter=() if late is None else early)
    dqa, dkp, dvpa, dbias = attn_a_bwd(qa, ka, va, bias, d_oa)
    dcq, dckv, dkr, dwuq, dwukv, dgq128, dgk128, dgkr128 = prep2_bwd(
        cqn, ckvn, proj, w["b_w_uq"], w["b_w_ukv"], dqf, dkf, dvp, gq128, gk128, gkr128, tabs)
    g["b_w_uq"], g["b_w_ukv"] = dwuq.astype(BF16), dwukv.astype(BF16)
    dx1, g["w_in"], g["mix_norm"], dgq, dgk, g["b_q_lat_norm"], g["b_kv_lat_norm"] = mix_bwd(
        proj, x1, h, dx2, w["w_in"], w["mix_norm"], dqa, dkp, dvpa, dcq, dckv, dkr, gq, gk,
        w["b_q_lat_norm"], w["b_kv_lat_norm"])
    mid = [g["w_in"][:IN_COLS].reshape(N_SHARD, IN_COLS // N_SHARD, D_MODEL),
           g["b_w_uq"].reshape(HEADS, LANES, 256)[:, :HD + ROPE].reshape(N_SHARD, -1, 256),
           _cols_to_shards(g["b_w_ukv"])]
    g["ffn1_w_gate"], g["ffn1_w_up"], g["ffn1_w_down"], dhp, *landed_late = ffn_bwd(
        x, dx1, w["ffn1_norm"], w["ffn1_w_gate"], w["ffn1_w_up"], w["ffn1_w_down"], gate1, up1,
        "ffn_bwd" if late is None else "ffn_bwd_scatter", scatter=() if late is None else mid,
        spread=None if late is None else late[1])
    grad_x, g["ffn1_norm"] = norm_bwd(x, w["ffn1_norm"], dhp, dx1, "ffn_norm_bwd")
    landed = dict(zip(FFN2 + ("w_out", "w_in", "b_w_uq", "b_w_ukv", "ffn1_w_gate", "ffn1_w_up", "ffn1_w_down"),
                      landed_early + landed_late))

    g["a_q_norm"] = jnp.sum(dgq.reshape(HEADS, HD), axis=0, keepdims=True)
    g["a_k_norm"] = jnp.sum(dgk.reshape(HEADS, HD), axis=0, keepdims=True)
    g["a_rel_bias"] = bias_block_grad(dbias)
    g["b_q_nope_norm"] = dgq128[:, :HD]
    g["b_q_rope_norm"] = dgq128[:, HD:HD + ROPE]
    g["b_k_nope_norm"] = dgk128[:, :HD]
    g["b_k_rope_norm"] = dgkr128[:, :ROPE]
    return loss, grad_x, g, landed


ANY = pl.BlockSpec(memory_space=pl.ANY)
N_DEV = 8


def _place():
    return lax.axis_index("x"), lax.axis_index("y"), lax.axis_index("c")


def _flip(v, bit):
    return 1 - v if bit else v


BF16_ROWS = 16


def _split_axis(shape):
    return 0 if (shape[0] // 2) % BF16_ROWS == 0 else 1


def _half_shape(shape):
    axis = _split_axis(shape)
    return tuple(s // 2 if a == axis else s for a, s in enumerate(shape))


def _half(shape, core):
    axis = _split_axis(shape)
    size = shape[axis] // 2
    return tuple(pl.ds(core * size, size) if a == axis else slice(None) for a in range(2))


class GatherPlan:
    def __init__(self, ws):
        self.shapes = [w.shape for w in ws]
        self.n = len(ws)
        self.out_shape = [jax.ShapeDtypeStruct((N_SHARD,) + w.shape, w.dtype) for w in ws]
        self.scratch = [pltpu.SemaphoreType.DMA((6 * self.n,)), pltpu.SemaphoreType.DMA((6 * self.n,))]

    def _copies(self, ins, outs, sems):
        x, y, c = _place()
        s_me = 2 * x + y
        sibling = (x, y, 1 - c)
        send_sems, recv_sems = sems

        def remote(k, src, dst, to):
            return pltpu.make_async_remote_copy(src_ref=src, dst_ref=dst, send_sem=send_sems.at[k],
                                                recv_sem=recv_sems.at[k], device_id=to, device_id_type=MESH)

        ici, fwd = [], []
        for a in range(self.n):
            mine, theirs = _half(self.shapes[a], c), _half(self.shapes[a], 1 - c)
            for j, (cx, cy) in enumerate([(1 - x, y), (x, 1 - y), (1 - x, 1 - y)]):
                got = outs[a].at[(2 * cx + cy,) + mine]
                ici.append((remote(6 * a + j, ins[a].at[mine], outs[a].at[(s_me,) + mine], (cx, cy, c)),
                            remote(6 * a + j, got, got, (cx, cy, c))))
                passed = outs[a].at[(2 * cx + cy,) + theirs]
                fwd.append((remote(6 * a + 3 + j, got, got, sibling), remote(6 * a + 3 + j, passed, passed, sibling)))
        return ici, fwd

    def start(self, ins, outs, sems):
        for send, _ in self._copies(ins, outs, sems)[0]:
            send.start()

    def forward(self, ins, outs, sems):
        ici, fwd = self._copies(ins, outs, sems)
        for (_, arrival), (send, _) in zip(ici, fwd):
            arrival.wait_recv()
            send.start()

    def finish(self, ins, outs, sems):
        ici, fwd = self._copies(ins, outs, sems)
        for _, arrival in fwd:
            arrival.wait_recv()
        for send, _ in ici + fwd:
            send.wait_send()


def allgather_shards(ws):
    plan = GatherPlan(ws)
    n = plan.n

    def body(*refs):
        ins, outs, sems = refs[:n], refs[n:2 * n], refs[2 * n:]
        plan.start(ins, outs, sems)
        plan.forward(ins, outs, sems)
        plan.finish(ins, outs, sems)

    return pl.pallas_call(
        body, name="allgather_shards", in_specs=[ANY] * n, out_specs=[ANY] * n,
        out_shape=plan.out_shape, scratch_shapes=plan.scratch,
    )(*ws)


class ScatterPlan:
    def __init__(self, gs):
        self.shapes = [g.shape[1:] for g in gs]
        self.n = len(gs)
        self.out_shape = [jax.ShapeDtypeStruct((N_DEV,) + _half_shape(g.shape[1:]), g.dtype) for g in gs]
        self.scratch = [pltpu.SemaphoreType.DMA((7 * self.n,)), pltpu.SemaphoreType.DMA((7 * self.n,)),
                        pltpu.SemaphoreType.DMA((self.n,))]

    def _copies(self, ins, outs, sems):
        x, y, c = _place()
        me = 4 * x + 2 * y + c
        send_sems, recv_sems, local_sems = sems
        local, sends, arrivals = [], [], []
        for a in range(self.n):
            piece = lambda px, py, pc, a=a: ins[a].at[(2 * px + py,) + _half(self.shapes[a], pc)]
            local.append(pltpu.make_async_copy(piece(x, y, c), outs[a].at[me], local_sems.at[a]))
            for k in range(1, N_DEV):
                px, py, pc = _flip(x, k & 4), _flip(y, k & 2), _flip(c, k & 1)
                sem = dict(send_sem=send_sems.at[7 * a + k - 1], recv_sem=recv_sems.at[7 * a + k - 1],
                           device_id=(px, py, pc), device_id_type=MESH)
                sends.append(pltpu.make_async_remote_copy(
                    src_ref=piece(px, py, pc), dst_ref=outs[a].at[me], **sem))
                slot = outs[a].at[4 * px + 2 * py + pc]
                arrivals.append(pltpu.make_async_remote_copy(src_ref=slot, dst_ref=slot, **sem))
        return local, sends, arrivals

    def start(self, ins, outs, sems):
        local, sends, _ = self._copies(ins, outs, sems)
        for cp in local + sends:
            cp.start()

    def finish(self, ins, outs, sems):
        local, sends, arrivals = self._copies(ins, outs, sems)
        for cp in arrivals:
            cp.wait_recv()
        for cp in sends:
            cp.wait_send()
        for cp in local:
            cp.wait()


def _row_tile(rows, row_bytes, budget, multiple):
    fits = [r for r in range(multiple, rows + 1, multiple) if rows % r == 0 and r * row_bytes <= budget]
    return max(fits) if fits else rows


SMALL_KERNEL_VMEM = 32 * 1024 * 1024


def sum_slots(lands, name):
    k = len(lands)
    _, rows, cols = lands[0].shape
    tr = _row_tile(rows, k * 2 * (N_DEV * cols * 2 + cols * 4), SMALL_KERNEL_VMEM, BF16_ROWS)

    def body(*refs):
        for l_ref, o_ref in zip(refs[:k], refs[k:]):
            acc = l_ref[0].astype(F32)
            for s in range(1, N_DEV):
                acc = acc + l_ref[s].astype(F32)
            o_ref[...] = acc

    return pl.pallas_call(
        body, name=name, grid=(rows // tr,),
        in_specs=[pl.BlockSpec((N_DEV, tr, cols), lambda i: (0, i, 0))] * k,
        out_specs=[pl.BlockSpec((tr, cols), lambda i: (i, 0))] * k,
        out_shape=[jax.ShapeDtypeStruct((rows, cols), F32)] * k,
        compiler_params=_params(("parallel",)),
    )(*lands)


def join_halves(hs, shapes):
    n = len(hs)

    def body(*refs):
        ins, outs = refs[:n], refs[n:2 * n]
        send_sems, recv_sems = refs[2 * n:]
        x, y, c = _place()
        sends = []
        for a in range(n):
            mine = outs[a].at[_half(shapes[a], c)]
            sends.append(pltpu.make_async_remote_copy(
                src_ref=ins[a], dst_ref=mine, send_sem=send_sems.at[a], recv_sem=recv_sems.at[a],
                device_id=(x, y, 1 - c), device_id_type=MESH))
            sends[-1].start()
        for a in range(n):
            theirs = outs[a].at[_half(shapes[a], 1 - c)]
            pltpu.make_async_remote_copy(
                src_ref=theirs, dst_ref=theirs, send_sem=send_sems.at[a], recv_sem=recv_sems.at[a],
                device_id=(x, y, 1 - c), device_id_type=MESH).wait_recv()
        for cp in sends:
            cp.wait_send()

    return pl.pallas_call(
        body, name="join_halves",
        in_specs=[ANY] * n, out_specs=[ANY] * n,
        out_shape=[jax.ShapeDtypeStruct(tuple(s), h.dtype) for s, h in zip(shapes, hs)],
        scratch_shapes=[pltpu.SemaphoreType.DMA((n,)), pltpu.SemaphoreType.DMA((n,))],
    )(*hs)


def allreduce_small(vec):
    def body(v_ref, o_ref, land_ref, send_sems, recv_sems):
        x, y, c = _place()
        me = 4 * x + 2 * y + c
        land_ref[me] = v_ref[...]
        sends = []
        for k in range(1, N_DEV):
            px, py, pc = _flip(x, k & 4), _flip(y, k & 2), _flip(c, k & 1)
            sends.append(pltpu.make_async_remote_copy(
                src_ref=v_ref, dst_ref=land_ref.at[me], send_sem=send_sems.at[k - 1], recv_sem=recv_sems.at[k - 1],
                device_id=(px, py, pc), device_id_type=MESH))
            sends[-1].start()
        for k in range(1, N_DEV):
            px, py, pc = _flip(x, k & 4), _flip(y, k & 2), _flip(c, k & 1)
            slot = land_ref.at[4 * px + 2 * py + pc]
            pltpu.make_async_remote_copy(
                src_ref=slot, dst_ref=slot, send_sem=send_sems.at[k - 1], recv_sem=recv_sems.at[k - 1],
                device_id=(px, py, pc), device_id_type=MESH).wait_recv()
        for cp in sends:
            cp.wait_send()
        acc = land_ref[0]
        for s in range(1, N_DEV):
            acc = acc + land_ref[s]
        o_ref[...] = acc

    vm = pl.BlockSpec(memory_space=pltpu.VMEM)
    return pl.pallas_call(
        body, name="allreduce_small",
        in_specs=[vm], out_specs=vm,
        out_shape=jax.ShapeDtypeStruct(vec.shape, F32),
        scratch_shapes=[pltpu.VMEM((N_DEV,) + vec.shape, F32), pltpu.SemaphoreType.DMA((N_DEV - 1,)),
                        pltpu.SemaphoreType.DMA((N_DEV - 1,))],
    )(vec)


def adamw(ws, gs, ms, vs, name):
    k = len(ws)
    rows, cols = ws[0].shape
    tr = _row_tile(rows, k * 2 * 7 * cols * 4, SMALL_KERNEL_VMEM, 8)
    c1 = 1.0 - ADAM_B1 ** ADAM_STEP
    c2 = 1.0 - ADAM_B2 ** ADAM_STEP

    def body(*refs):
        for a in range(k):
            w_ref, g_ref, m_ref, v_ref = (refs[s * k + a] for s in range(4))
            d_ref, nm_ref, nv_ref = (refs[(4 + s) * k + a] for s in range(3))
            gv = g_ref[...]
            nm = ADAM_B1 * m_ref[...] + (1.0 - ADAM_B1) * gv
            nv = ADAM_B2 * v_ref[...] + (1.0 - ADAM_B2) * (gv * gv)
            nm_ref[...] = nm
            nv_ref[...] = nv
            d_ref[...] = -ADAM_LR * ((nm / c1) / (jnp.sqrt(nv / c2) + ADAM_EPS) + ADAM_WD * w_ref[...])

    blk = pl.BlockSpec((tr, cols), lambda i: (i, 0))
    out = pl.pallas_call(
        body, name=name, grid=(rows // tr,),
        in_specs=[blk] * (4 * k), out_specs=[blk] * (3 * k),
        out_shape=[jax.ShapeDtypeStruct((rows, cols), F32)] * (3 * k),
        compiler_params=_params(("parallel",)),
    )(*ws, *gs, *ms, *vs)
    return [(out[a], out[k + a], out[2 * k + a]) for a in range(k)]


BIG = ("ffn1_w_gate", "ffn1_w_up", "ffn1_w_down", "w_in", "b_w_uq", "b_w_ukv", "w_out",
       "ffn2_w_gate", "ffn2_w_up", "ffn2_w_down")
SMALL = ("ffn1_norm", "mix_norm", "a_q_norm", "a_k_norm", "a_rel_bias", "b_q_lat_norm", "b_kv_lat_norm",
         "b_q_nope_norm", "b_q_rope_norm", "b_k_nope_norm", "b_k_rope_norm", "ffn2_norm", "final_norm")
WEIGHTS = ("ffn1_norm", "ffn1_w_gate", "ffn1_w_up", "ffn1_w_down", "mix_norm", "w_in", "a_q_norm", "a_k_norm",
           "a_rel_bias", "b_q_lat_norm", "b_w_uq", "b_kv_lat_norm", "b_w_ukv", "b_q_nope_norm", "b_q_rope_norm",
           "b_k_nope_norm", "b_k_rope_norm", "w_out", "ffn2_norm", "ffn2_w_gate", "ffn2_w_up", "ffn2_w_down",
           "final_norm")
TRANSPOSED = ("ffn1_w_gate", "ffn1_w_up", "ffn2_w_gate", "ffn2_w_up", "w_in", "b_w_uq")
PACK_SHAPE = (8, 1024)


def _pack_small(d, last=None):
    flat = [d[n].reshape(-1) for n in SMALL]
    used = sum(f.shape[0] for f in flat)
    total = PACK_SHAPE[0] * PACK_SHAPE[1]
    tail = jnp.zeros((total - used - 1,), F32)
    end = jnp.zeros((1,), F32) if last is None else last.reshape(1)
    return jnp.concatenate(flat + [tail, end]).reshape(PACK_SHAPE)


def _unpack_small(p, like):
    flat = p.reshape(-1)
    out, off = {}, 0
    for n in SMALL:
        size = like[n].size
        out[n] = flat[off:off + size].reshape(like[n].shape)
        off += size
    return out, flat[-1]


def _cols_to_shards(g):
    rows, cols = g.shape
    return g.reshape(rows, N_SHARD, cols // N_SHARD).transpose(1, 0, 2)


def _shards_to_cols(g):
    return g.transpose(1, 0, 2).reshape(g.shape[1], -1)


def kernel(x, ffn1_norm, ffn1_w_gate, ffn1_w_up, ffn1_w_down, mix_norm, w_in, a_q_norm, a_k_norm, a_rel_bias, b_q_lat_norm, b_w_uq, b_kv_lat_norm, b_w_ukv, b_q_nope_norm, b_q_rope_norm, b_k_nope_norm, b_k_rope_norm, w_out, ffn2_norm, ffn2_w_gate, ffn2_w_up, ffn2_w_down, final_norm, loss_target, m_ffn1_norm, m_ffn1_w_gate, m_ffn1_w_up, m_ffn1_w_down, m_mix_norm, m_w_in, m_a_q_norm, m_a_k_norm, m_a_rel_bias, m_b_q_lat_norm, m_b_w_uq, m_b_kv_lat_norm, m_b_w_ukv, m_b_q_nope_norm, m_b_q_rope_norm, m_b_k_nope_norm, m_b_k_rope_norm, m_w_out, m_ffn2_norm, m_ffn2_w_gate, m_ffn2_w_up, m_ffn2_w_down, m_final_norm, v_ffn1_norm, v_ffn1_w_gate, v_ffn1_w_up, v_ffn1_w_down, v_mix_norm, v_w_in, v_a_q_norm, v_a_k_norm, v_a_rel_bias, v_b_q_lat_norm, v_b_w_uq, v_b_kv_lat_norm, v_b_w_ukv, v_b_q_nope_norm, v_b_q_rope_norm, v_b_k_nope_norm, v_b_k_rope_norm, v_w_out, v_ffn2_norm, v_ffn2_w_gate, v_ffn2_w_up, v_ffn2_w_down, v_final_norm):
    args = locals()
    view = lambda a, n: a[0].T if n in TRANSPOSED else a[0]
    wts = {n: view(args[n], n) for n in WEIGHTS}
    mom = {n: view(args["m_" + n], n) for n in WEIGHTS}
    var = {n: view(args["v_" + n], n) for n in WEIGHTS}

    shard = 2 * lax.axis_index("x") + lax.axis_index("y")
    core = lax.axis_index("c")
    first = [n for n in BIG if n not in LATE]
    own = [wts[n].astype(BF16) for n in first]
    w = {n: wts[n] if n == "a_rel_bias" else wts[n][None] for n in SMALL}
    w.update(kernel_layout({n: lax.dynamic_update_index_in_dim(got, mine, shard, 0)
                            for n, got, mine in zip(first, allgather_shards(own), own)}))

    loss, grad_x, g, landed = local_step(x[0], loss_target[0], w,
                                         late=([wts[n].astype(BF16) for n in LATE], shard))

    me = 2 * shard + core
    for n in first:
        piece = lax.dynamic_slice(g[n], (shard, core * (FS // 2), 0), (1, FS // 2, D_MODEL))
        landed[n] = lax.dynamic_update_slice(landed[n], piece, (me, 0, 0))
    groups = {}
    for n in BIG:
        groups.setdefault(wts[n].shape, []).append(n)
    half = {}
    for names in groups.values():
        half.update(zip(names, sum_slots([landed[n] for n in names], "sum_slots")))
    halves = [half[n] for n in BIG]
    shapes = [wts[n].shape for n in BIG]
    axes = [_split_axis(s) for s in shapes]
    grads = dict(zip(BIG, (lax.dynamic_update_slice_in_dim(got, mine, core * mine.shape[ax], ax)
                           for got, mine, ax in zip(join_halves(halves, shapes), halves, axes))))

    small_sum, loss_sum = _unpack_small(allreduce_small(_pack_small(g, loss[0, 0])), wts)
    grads.update(small_sum)

    delta, new_m, new_v = {}, {}, {}
    for names in groups.values():
        stepped = adamw(*([d[n] for n in names] for d in (wts, grads, mom, var)), "adamw")
        for n, (d_new, m_new, v_new) in zip(names, stepped):
            delta[n], new_m[n], new_v[n] = d_new, m_new, v_new
    (packed,) = adamw([_pack_small(wts)], [_pack_small(grads)], [_pack_small(mom)], [_pack_small(var)], "adamw_small")
    for dst, p in zip((delta, new_m, new_v), packed):
        dst.update(_unpack_small(p, wts)[0])

    lead = lambda d: [(d[n].T if n in TRANSPOSED else d[n])[None] for n in WEIGHTS]
    return (loss_sum, grad_x[None], *lead(grads), *lead(delta), *lead(new_m), *lead(new_v))
```

```python
import numpy as np
import jax
import jax.numpy as jnp
from jax import lax
from jax.experimental import pallas as pl
from jax.experimental.pallas import tpu as pltpu

F32 = jnp.float32
BF16 = jnp.bfloat16
EPS = 1e-6
NEG = -1e30

D_MODEL = 1024
D_FF = 2816
N_SHARD = 4
FS = D_FF // N_SHARD
CHUNK = 64
A_LEFT = 8
A_MAX_REL = 128
HEADS = 8
HD = 64
ROPE = 32
PROJ_W = 2048
IN_COLS = 1952
B_SCALE = 96 ** -0.5
LANES = 128

ADAM_LR = 0.001
ADAM_B1 = 0.9
ADAM_B2 = 0.999
ADAM_EPS = 1e-08
ADAM_WD = 0.01
ADAM_STEP = 10

VMEM_LIMIT = 56 * 1024 * 1024
TOKEN_TILE = 512

MESH = pl.DeviceIdType.MESH


def _dot(a, b):
    return lax.dot_general(a, b, (((1,), (0,)), ((), ())), preferred_element_type=F32)


def _dot_nt(a, b):
    return lax.dot_general(a, b, (((1,), (1,)), ((), ())), preferred_element_type=F32)


def _dot_tn(a, b):
    return lax.dot_general(a, b, (((0,), (0,)), ((), ())), preferred_element_type=F32)


def _params(sem):
    return pltpu.CompilerParams(dimension_semantics=sem, vmem_limit_bytes=VMEM_LIMIT)


def _rms(xv):
    r = lax.rsqrt(jnp.mean(xv * xv, axis=-1, keepdims=True) + EPS)
    return r, xv * r


def ffn_fwd(x, g, wg, wu, wd, name, gather=(), loss_head=None, pre_proj=None):
    t, d = x.shape
    tm = TOKEN_TILE
    ni = t // tm
    plan = GatherPlan(gather)
    n = plan.n
    head = () if loss_head is None else tuple(loss_head)
    q = len(head)
    proj = () if pre_proj is None else tuple(pre_proj)
    r = len(proj)
    half = proj[0].shape[1] if r else 0

    def body(*refs):
        it = iter(refs)
        take = lambda count: [next(it) for _ in range(count)]
        x_ref, g_ref, wg_ref, wu_ref, wd_ref = take(5)
        head_in, proj_in, ins = take(q), take(r), take(n)
        o_ref, gp_ref, up_ref = take(3)
        head_out, x_out, outs = take(q), take(1 if r else 0), take(n)
        h_ref, acc_ref = take(2)
        sems = list(it)
        i, j = pl.program_id(0), pl.program_id(1)
        if n:
            pl.when((i == 0) & (j == 0))(lambda: plan.start(ins, outs, sems))
            pl.when((i == (3 * ni) // 4) & (j == 0))(lambda: plan.forward(ins, outs, sems))

        @pl.when(j == 0)
        def _():
            xin = x_ref[...]
            if r:
                oa_ref, obt_ref, wo_ref = proj_in
                xin = xin + _dot(oa_ref[...], wo_ref[0:half, :]) + _dot_tn(obt_ref[...], wo_ref[half:2 * half, :])
                x_out[0][...] = xin
            _, xn = _rms(xin)
            h_ref[...] = (xn * g_ref[...]).astype(BF16)
            acc_ref[...] = jnp.zeros_like(acc_ref)

        h = h_ref[...]
        gp = _dot_nt(h, wg_ref[0])
        up = _dot_nt(h, wu_ref[0])
        gp_ref[0] = gp
        up_ref[0] = up
        a = (gp * jax.nn.sigmoid(gp) * up).astype(BF16)
        acc_ref[...] += _dot(a, wd_ref[0])

        @pl.when(j == N_SHARD - 1)
        def _():
            y = (x_out[0][...] if r else x_ref[...]) + 0.5 * acc_ref[...]
            if not q:
                o_ref[...] = y
                return
            (gf_ref, t_ref), (dgf_ref, loss_ref) = head_in, head_out
            rinv, yn = _rms(y)
            gf = gf_ref[...]
            e = yn * gf - t_ref[...]
            dout = e * (1.0 / d)
            dng = dout * gf
            o_ref[...] = rinv * (dng - yn * jnp.mean(dng * yn, axis=-1, keepdims=True))

            @pl.when(i == 0)
            def _():
                dgf_ref[...] = jnp.zeros_like(dgf_ref)
                loss_ref[...] = jnp.zeros_like(loss_ref)

            dgf_ref[...] += jnp.sum(dout * yn, axis=0, keepdims=True)
            part = jnp.sum(jnp.sum(e * e, axis=-1, keepdims=True), axis=0, keepdims=True) * (0.5 / d)
            loss_ref[...] += jnp.broadcast_to(part, loss_ref.shape)

        if n:
            pl.when((i == ni - 1) & (j == N_SHARD - 1))(lambda: plan.finish(ins, outs, sems))

    tok = pl.BlockSpec((tm, d), lambda i, j: (i, 0))
    vec = pl.BlockSpec((1, d), lambda i, j: (0, 0))
    chunk = pl.BlockSpec((1, FS, d), lambda i, j: (j, 0, 0))
    pre = pl.BlockSpec((1, tm, FS), lambda i, j: (j, i, 0))
    return pl.pallas_call(
        body, name=name, grid=(ni, N_SHARD),
        in_specs=[tok, vec, chunk, chunk, chunk] + [vec, tok][:q]
        + [pl.BlockSpec((tm, half), lambda i, j: (i, 0)), pl.BlockSpec((half, tm), lambda i, j: (0, i)),
           pl.BlockSpec((2 * half, d), lambda i, j: (0, 0))][:r] + [ANY] * n,
        out_specs=[tok, pre, pre] + [vec, pl.BlockSpec((1, LANES), lambda i, j: (0, 0))][:q] + [tok][:r] + [ANY] * n,
        out_shape=[jax.ShapeDtypeStruct((t, d), F32), jax.ShapeDtypeStruct((N_SHARD, t, FS), F32),
                   jax.ShapeDtypeStruct((N_SHARD, t, FS), F32)]
        + [jax.ShapeDtypeStruct((1, d), F32), jax.ShapeDtypeStruct((1, LANES), F32)][:q]
        + [jax.ShapeDtypeStruct((t, d), F32)][:r] + plan.out_shape,
        scratch_shapes=[pltpu.VMEM((tm, d), BF16), pltpu.VMEM((tm, d), F32)] + (plan.scratch if n else []),
        compiler_params=_params(("arbitrary", "arbitrary")),
    )(x, g, wg, wu, wd, *head, *proj, *gather)


def ffn_bwd(x, dout, g, wg, wu, wd, gate, up_pre, name, scatter=(), spread=None):
    t, d = x.shape
    tm = TOKEN_TILE
    ni = t // tm
    hf = FS // 2
    plan = ScatterPlan(scatter)
    m = plan.n
    k = 0 if spread is None else 3
    steps = jnp.arange(N_SHARD, dtype=jnp.int32)
    order = steps if spread is None else (spread + 1 + steps) % N_SHARD

    def body(*refs):
        ord_ref, x_ref, do_ref, g_ref, wg_ref, wu_ref, wd_ref, gp_ref, up_ref = refs[:9]
        ins, (dwg_out, dwu_out, dwd_out, dhp_ref) = refs[9:9 + m], refs[9 + m:13 + m]
        outs, lands = refs[13 + m:13 + 2 * m], refs[13 + 2 * m:13 + 2 * m + k]
        dwg_ref, dwu_ref, dwd_ref = refs[13 + 2 * m + k:16 + 2 * m + k]
        sems = refs[16 + 2 * m + k:19 + 2 * m + k] if m else ()
        stage_ref = refs[-3] if k else None
        j, i = pl.program_id(0), pl.program_id(1)
        if m:
            pl.when((j == 0) & (i == 0))(lambda: plan.start(ins, outs, sems))

        def chunk_copies(jj):
            send_sems, recv_sems = refs[-2:]
            px, py, pc = _place()
            me = 4 * px + 2 * py + pc
            tx, ty = ord_ref[jj] // 2, ord_ref[jj] % 2
            copies = []
            for n_ in range(3):
                for h_ in range(2):
                    copies.append((pltpu.make_async_remote_copy(
                        src_ref=stage_ref.at[n_, pl.ds(h_ * hf, hf)], dst_ref=lands[n_].at[me],
                        send_sem=send_sems.at[6 * jj + 2 * n_ + h_], recv_sem=recv_sems.at[3 * me + n_],
                        device_id=(tx, ty, h_), device_id_type=MESH), (tx != px) | (ty != py) | (pc != h_)))
            return copies

        def arrivals():
            send_sems, recv_sems = refs[-2:]
            px, py, pc = _place()
            me = 4 * px + 2 * py + pc
            for s_ in range(N_DEV):
                for n_ in range(3):
                    slot = lands[n_].at[s_]
                    cp = pltpu.make_async_remote_copy(
                        src_ref=slot, dst_ref=slot, send_sem=send_sems.at[0], recv_sem=recv_sems.at[3 * s_ + n_],
                        device_id=(px, py, pc), device_id_type=MESH)
                    pl.when(me != s_)(cp.wait_recv)

        _, xn = _rms(x_ref[...])
        h = (xn * g_ref[...]).astype(BF16)
        dz = (0.5 * do_ref[...]).astype(BF16)
        wgv, wuv, wdv = wg_ref[0], wu_ref[0], wd_ref[0]
        gp, up = gp_ref[0], up_ref[0]
        s = jax.nn.sigmoid(gp)
        sg = gp * s
        a = (sg * up).astype(BF16)
        da = _dot_nt(dz, wdv)
        dup = (da * sg).astype(BF16)
        dgp = (da * up * (s * (1.0 + gp * (1.0 - s)))).astype(BF16)

        @pl.when(i == 0)
        def _():
            dwg_ref[...] = jnp.zeros_like(dwg_ref)
            dwu_ref[...] = jnp.zeros_like(dwu_ref)
            dwd_ref[...] = jnp.zeros_like(dwd_ref)

        dwd_ref[...] += _dot_tn(a, dz)
        dwg_ref[...] += _dot_tn(dgp, h)
        dwu_ref[...] += _dot_tn(dup, h)
        dhp_ref[0] = (_dot(dgp, wgv) + _dot(dup, wuv)).astype(BF16)

        @pl.when(i == ni - 1)
        def _():
            dwg_out[0] = dwg_ref[...].astype(BF16)
            dwu_out[0] = dwu_ref[...].astype(BF16)
            dwd_out[0] = dwd_ref[...].astype(BF16)
            if k:
                @pl.when(j >= 1)
                def _():
                    for cp, leaves in chunk_copies(j - 1):
                        pl.when(leaves)(cp.wait_send)
                for n_, acc in enumerate((dwg_ref, dwu_ref, dwd_ref)):
                    stage_ref[n_] = acc[...].astype(BF16)
                for cp, leaves in chunk_copies(j):
                    pl.when(leaves)(cp.start)

                @pl.when(j == N_SHARD - 1)
                def _():
                    for cp, leaves in chunk_copies(N_SHARD - 1):
                        pl.when(leaves)(cp.wait_send)
                    arrivals()

        if m:
            pl.when((j == N_SHARD - 1) & (i == ni - 1))(lambda: plan.finish(ins, outs, sems))

    chunk = pl.BlockSpec((1, FS, d), lambda j, i, o: (o[j], 0, 0))
    tok = pl.BlockSpec((tm, d), lambda j, i, o: (i, 0))
    pre = pl.BlockSpec((1, tm, FS), lambda j, i, o: (o[j], i, 0))
    grid_spec = pltpu.PrefetchScalarGridSpec(
        num_scalar_prefetch=1, grid=(N_SHARD, ni),
        in_specs=[tok, tok, pl.BlockSpec((1, d), lambda j, i, o: (0, 0)), chunk, chunk, chunk, pre, pre] + [ANY] * m,
        out_specs=[chunk, chunk, chunk, pl.BlockSpec((1, tm, d), lambda j, i, o: (o[j], i, 0))] + [ANY] * (m + k),
        scratch_shapes=[pltpu.VMEM((FS, d), F32), pltpu.VMEM((FS, d), F32), pltpu.VMEM((FS, d), F32)]
        + (plan.scratch if m else [])
        + ([pltpu.VMEM((3, FS, d), BF16), pltpu.SemaphoreType.DMA((6 * N_SHARD,)),
            pltpu.SemaphoreType.DMA((3 * N_DEV,))] if k else []))
    return pl.pallas_call(
        body, name=name, grid_spec=grid_spec,
        out_shape=[jax.ShapeDtypeStruct((N_SHARD, FS, d), BF16),
                   jax.ShapeDtypeStruct((N_SHARD, FS, d), BF16),
                   jax.ShapeDtypeStruct((N_SHARD, FS, d), BF16),
                   jax.ShapeDtypeStruct((N_SHARD, t, d), BF16)] + plan.out_shape
        + [jax.ShapeDtypeStruct((N_DEV, hf, d), BF16)] * k,
        compiler_params=_params(("arbitrary", "arbitrary")),
    )(order, x, dout, g, wg, wu, wd, gate, up_pre, *scatter)


def norm_bwd(x, g, dhp, dres, name):
    t, d = x.shape
    p = dhp.shape[0]
    tm = TOKEN_TILE

    def body(x_ref, g_ref, dhp_ref, dres_ref, dx_ref, dg_ref):
        i = pl.program_id(0)
        r, xn = _rms(x_ref[...])
        dh = dhp_ref[0].astype(F32)
        for q in range(1, p):
            dh = dh + dhp_ref[q].astype(F32)
        dhg = dh * g_ref[...]
        dx_ref[...] = dres_ref[...] + r * (dhg - xn * jnp.mean(dhg * xn, axis=-1, keepdims=True))

        @pl.when(i == 0)
        def _():
            dg_ref[...] = jnp.zeros_like(dg_ref)

        dg_ref[...] += jnp.sum(dh * xn, axis=0, keepdims=True)

    return pl.pallas_call(
        body, name=name, grid=(t // tm,),
        in_specs=[pl.BlockSpec((tm, d), lambda i: (i, 0)),
                  pl.BlockSpec((1, d), lambda i: (0, 0)),
                  pl.BlockSpec((p, tm, d), lambda i: (0, i, 0)),
                  pl.BlockSpec((tm, d), lambda i: (i, 0))],
        out_specs=[pl.BlockSpec((tm, d), lambda i: (i, 0)),
                   pl.BlockSpec((1, d), lambda i: (0, 0))],
        out_shape=[jax.ShapeDtypeStruct((t, d), F32), jax.ShapeDtypeStruct((1, d), F32)],
        compiler_params=_params(("arbitrary",)),
    )(x, g, dhp, dres)


def out_proj_bwd(x, g, dhp, dres, w, oa, ob_t):
    t, d = x.shape
    half = w.shape[0] // 2
    p = dhp.shape[0]
    tm = TOKEN_TILE
    ni = t // tm

    def body(x_ref, g_ref, dhp_ref, dres_ref, w_ref, oa_ref, obt_ref,
             dx_ref, dg_ref, da_ref, db_ref, dbt_ref, dw_ref, acc_ref):
        i = pl.program_id(0)

        @pl.when(i == 0)
        def _():
            dg_ref[...] = jnp.zeros_like(dg_ref)
            acc_ref[...] = jnp.zeros_like(acc_ref)

        r, xn = _rms(x_ref[...])
        dh = dhp_ref[0].astype(F32)
        for s in range(1, p):
            dh = dh + dhp_ref[s].astype(F32)
        dhg = dh * g_ref[...]
        dx = dres_ref[...] + r * (dhg - xn * jnp.mean(dhg * xn, axis=-1, keepdims=True))
        dx_ref[...] = dx
        dg_ref[...] += jnp.sum(dh * xn, axis=0, keepdims=True)
        dxb = dx.astype(BF16)
        da_ref[...] = _dot_nt(dxb, w_ref[0:half, :]).astype(BF16)
        db_ref[...] = _dot_nt(dxb, w_ref[half:2 * half, :]).astype(BF16)
        dbt_ref[...] = _dot_nt(w_ref[half:2 * half, :], dxb).astype(BF16)
        acc_ref[0:half, :] += _dot_tn(oa_ref[...], dxb)
        acc_ref[half:2 * half, :] += _dot(obt_ref[...], dxb)

        @pl.when(i == ni - 1)
        def _():
            dw_ref[...] = acc_ref[...].astype(BF16)

    row = lambda w_: pl.BlockSpec((tm, w_), lambda i: (i, 0))
    col = pl.BlockSpec((half, tm), lambda i: (0, i))
    whole = pl.BlockSpec((2 * half, d), lambda i: (0, 0))
    vec = pl.BlockSpec((1, d), lambda i: (0, 0))
    return pl.pallas_call(
        body, name="out_proj_bwd", grid=(ni,),
        in_specs=[row(d), vec, pl.BlockSpec((p, tm, d), lambda i: (0, i, 0)), row(d), whole, row(half), col],
        out_specs=[row(d), vec, row(half), row(half), col, whole],
        out_shape=[jax.ShapeDtypeStruct((t, d), F32), jax.ShapeDtypeStruct((1, d), F32),
                   jax.ShapeDtypeStruct((t, half), BF16), jax.ShapeDtypeStruct((t, half), BF16),
                   jax.ShapeDtypeStruct((half, t), BF16), jax.ShapeDtypeStruct((2 * half, d), BF16)],
        scratch_shapes=[pltpu.VMEM((2 * half, d), F32)],
        compiler_params=_params(("arbitrary",)),
    )(x, g, dhp, dres, w, oa, ob_t)


def _lane(shape):
    return lax.broadcasted_iota(jnp.int32, shape, 1)


PAIR = (0, HD, LANES)
Q_HEAD = (0, HD, HD + ROPE, LANES)
K_ROPE = (0, ROPE, LANES)


def _seg_mean(z, bounds):
    seg = lambda v: sum([(v >= b).astype(jnp.int32) for b in bounds[1:-1]], jnp.zeros_like(v))
    rows = seg(lax.broadcasted_iota(jnp.int32, (LANES, LANES), 0))
    cols = seg(lax.broadcasted_iota(jnp.int32, (LANES, LANES), 1))
    same = (rows == cols).astype(BF16)
    lane = _lane((1, LANES))
    inv = sum([jnp.where((lane >= a) & (lane < b), 1.0 / (b - a), 0.0) for a, b in zip(bounds[:-1], bounds[1:])])
    hi = z.astype(BF16)
    lo = (z - hi.astype(F32)).astype(BF16)
    return (_dot(hi, same) + _dot(lo, same)) * inv


def _seg_norm(x, bounds):
    r = lax.rsqrt(_seg_mean(x * x, bounds) + EPS)
    return r, x * r


def _seg_norm_bwd(r, xn, dyg, bounds):
    return r * (dyg - xn * _seg_mean(dyg * xn, bounds))


A_TM = 256


def mix_fwd(x, g, w, gq, gk, gcq, gckv):
    t, d = x.shape
    tm = A_TM

    def body(x_ref, g_ref, w_ref, gq_ref, gk_ref, gcq_ref, gckv_ref,
             h_ref, p_ref, qa_ref, ka_ref, va_ref, cq_ref, ckv_ref):
        _, xn = _rms(x_ref[...])
        h = (xn * g_ref[...]).astype(BF16)
        h_ref[...] = h
        p_ref[...] = _dot_nt(h, w_ref[...])
        for p in range(4):
            sl = slice(LANES * p, LANES * (p + 1))
            _, xn = _seg_norm(p_ref[:, sl], PAIR)
            qa_ref[:, sl] = (xn * gq_ref[:, sl] * 0.125).astype(BF16)
            _, xn = _seg_norm(p_ref[:, 512 + LANES * p:512 + LANES * (p + 1)], PAIR)
            ka_ref[:, sl] = (xn * gk_ref[:, sl]).astype(BF16)
        va_ref[...] = p_ref[:, 1024:1536].astype(BF16)
        _, xn = _rms(p_ref[:, 1536:1792])
        cq_ref[...] = (xn * gcq_ref[...]).astype(BF16)
        _, xn = _rms(p_ref[:, 1792:1920])
        ckv_ref[...] = (xn * gckv_ref[...]).astype(BF16)

    row = lambda w: pl.BlockSpec((tm, w), lambda i: (i, 0))
    vec = lambda w: pl.BlockSpec((1, w), lambda i: (0, 0))
    return pl.pallas_call(
        body, name="mix_fwd", grid=(t // tm,),
        in_specs=[row(d), vec(d), pl.BlockSpec((PROJ_W, d), lambda i: (0, 0)), vec(512), vec(512), vec(256), vec(128)],
        out_specs=[row(d), row(PROJ_W), row(512), row(512), row(512), row(256), row(128)],
        out_shape=[jax.ShapeDtypeStruct((t, d), BF16), jax.ShapeDtypeStruct((t, PROJ_W), F32)]
        + [jax.ShapeDtypeStruct((t, w_), BF16) for w_ in (512, 512, 512, 256, 128)],
        compiler_params=_params(("parallel",)),
    )(x, g, w, gq, gk, gcq, gckv)


def mix_bwd(proj, x, h, dres, w, g, dqa, dkp, dvp, dcq, dckv, dkr, gq, gk, gcq, gckv):
    t, d = x.shape
    tm = A_TM
    nb = t // tm

    def body(p_ref, x_ref, h_ref, dres_ref, w_ref, g_ref, dqa_ref, dk0_ref, dk1_ref, dk2_ref, dv0_ref, dv1_ref,
             dv2_ref, dcq_ref, dckv_ref, dkr_ref, gq_ref, gk_ref, gcq_ref, gckv_ref,
             dx_ref, dw_ref, dg_ref, dgq_ref, dgk_ref, dgcq_ref, dgckv_ref, dp_ref, acc_ref):
        i = pl.program_id(0)

        @pl.when(i == 0)
        def _():
            for ref in (acc_ref, dg_ref, dgq_ref, dgk_ref, dgcq_ref, dgckv_ref):
                ref[...] = jnp.zeros_like(ref)

        has1 = (i + 1 < nb).astype(F32)
        has2 = (i + 2 < nb).astype(F32)
        for p in range(4):
            sl = slice(LANES * p, LANES * (p + 1))
            r, xn = _seg_norm(p_ref[:, sl], PAIR)
            dy = dqa_ref[:, sl] * 0.125
            dp_ref[:, sl] = _seg_norm_bwd(r, xn, dy * gq_ref[:, sl], PAIR).astype(BF16)
            dgq_ref[:, sl] += jnp.sum(dy * xn, axis=0, keepdims=True)
            ks = slice(512 + LANES * p, 512 + LANES * (p + 1))
            r, xn = _seg_norm(p_ref[:, ks], PAIR)
            dy = dk0_ref[0, :, sl] + has1 * dk1_ref[0, :, sl] + has2 * dk2_ref[0, :, sl]
            dp_ref[:, ks] = _seg_norm_bwd(r, xn, dy * gk_ref[:, sl], PAIR).astype(BF16)
            dgk_ref[:, sl] += jnp.sum(dy * xn, axis=0, keepdims=True)
        dp_ref[:, 1024:1536] = (dv0_ref[0] + has1 * dv1_ref[0] + has2 * dv2_ref[0]).astype(BF16)
        for (a, b, dlat_ref, glat_ref, dglat_ref) in ((1536, 1792, dcq_ref, gcq_ref, dgcq_ref),
                                                      (1792, 1920, dckv_ref, gckv_ref, dgckv_ref)):
            r, xn = _rms(p_ref[:, a:b])
            dy = dlat_ref[...]
            dyg = dy * glat_ref[...]
            dp_ref[:, a:b] = (r * (dyg - xn * jnp.mean(dyg * xn, axis=-1, keepdims=True))).astype(BF16)
            dglat_ref[...] += jnp.sum(dy * xn, axis=0, keepdims=True)
        dp_ref[:, 1920:2048] = dkr_ref[...].astype(BF16)

        dproj = dp_ref[...]
        acc_ref[...] += _dot_tn(dproj, h_ref[...])
        dh = _dot(dproj, w_ref[...])
        r, xn = _rms(x_ref[...])
        dhg = dh * g_ref[...]
        dx_ref[...] = dres_ref[...] + r * (dhg - xn * jnp.mean(dhg * xn, axis=-1, keepdims=True))
        dg_ref[...] += jnp.sum(dh * xn, axis=0, keepdims=True)

        @pl.when(i == nb - 1)
        def _():
            dw_ref[...] = acc_ref[...].astype(BF16)

    row = lambda w_: pl.BlockSpec((tm, w_), lambda i: (i, 0))
    vec = lambda w_: pl.BlockSpec((1, w_), lambda i: (0, 0))
    part = lambda s: pl.BlockSpec((1, tm, 512), lambda i: (s, jnp.minimum(i + s, nb - 1), 0))
    whole = pl.BlockSpec((PROJ_W, d), lambda i: (0, 0))
    return pl.pallas_call(
        body, name="mix_bwd", grid=(nb,),
        in_specs=[row(PROJ_W), row(d), row(d), row(d), whole, vec(d), row(512), part(0), part(1), part(2),
                  part(0), part(1), part(2), row(256), row(128), row(128), vec(512), vec(512), vec(256), vec(128)],
        out_specs=[row(d), whole, vec(d), vec(512), vec(512), vec(256), vec(128)],
        out_shape=[jax.ShapeDtypeStruct((t, d), F32), jax.ShapeDtypeStruct((PROJ_W, d), BF16)]
        + [jax.ShapeDtypeStruct((1, w_), F32) for w_ in (d, 512, 512, 256, 128)],
        scratch_shapes=[pltpu.VMEM((tm, PROJ_W), BF16), pltpu.VMEM((PROJ_W, d), F32)],
        compiler_params=_params(("arbitrary",)),
    )(proj, x, h, dres, w, g, dqa, dkp, dkp, dkp, dvp, dvp, dvp, dcq, dckv, dkr, gq, gk, gcq, gckv)


def _roll(x, shift):
    return pltpu.roll(x, shift % LANES, 1)


def _rope(y, c, s1, s2):
    return y * c + _roll(y, -16) * s1 + _roll(y, 16) * s2


def _rope_bwd(d, c, s1, s2):
    return d * c + _roll(d * s1, 16) + _roll(d * s2, -16)


def _up_proj(cq_ref, ckv_ref, wuq_ref, wukv_ref):
    return _dot_nt(cq_ref[...], wuq_ref[...]), _dot(ckv_ref[...], wukv_ref[...])


def prep2_fwd(cqn, ckvn, proj, wuq, wukv, gq, gk, gkr, tabs):
    t = cqn.shape[0]
    tm = A_TM

    def body(cq_ref, ckv_ref, kr_ref, wuq_ref, wukv_ref, gq_ref, gk_ref, gkr_ref, tab_ref, qf_ref, kf_ref, vp_ref):
        q_all, kv_all = _up_proj(cq_ref, ckv_ref, wuq_ref, wukv_ref)
        _, xn = _seg_norm(kr_ref[...], K_ROPE)
        kpe = _roll(_rope(xn * gkr_ref[...], tab_ref[3], tab_ref[4], tab_ref[5]), 64)
        for h in range(HEADS):
            sl = slice(LANES * h, LANES * (h + 1))
            _, xn = _seg_norm(q_all[:, sl], Q_HEAD)
            qf_ref[:, sl] = (_rope(xn * gq_ref[...], tab_ref[0], tab_ref[1], tab_ref[2]) * B_SCALE2).astype(BF16)
            x = kv_all[:, sl]
            lo = _lane(x.shape) < HD
            _, xkn = _seg_norm(jnp.where(lo, x, 0.0), PAIR)
            kf_ref[:, sl] = (xkn * gk_ref[...] + kpe).astype(BF16)
            if h % 2 == 0:
                v_even = _roll(x, 64)
            else:
                vp_ref[:, LANES * (h // 2):LANES * (h // 2 + 1)] = jnp.where(lo, v_even, x).astype(BF16)

    row = lambda w: pl.BlockSpec((tm, w), lambda i: (i, 0))
    vec = lambda w: pl.BlockSpec((1, w), lambda i: (0, 0))
    full = lambda a: pl.BlockSpec(a.shape, lambda i: (0, 0))
    return pl.pallas_call(
        body, name="prep2_fwd", grid=(t // tm,),
        in_specs=[row(256), row(128), pl.BlockSpec((tm, LANES), lambda i: (i, 15)), full(wuq), full(wukv),
                  vec(128), vec(128), vec(128), pl.BlockSpec((6, tm, LANES), lambda i: (0, i, 0))],
        out_specs=[row(1024), row(1024), row(512)],
        out_shape=[jax.ShapeDtypeStruct((t, 1024), BF16), jax.ShapeDtypeStruct((t, 1024), BF16),
                   jax.ShapeDtypeStruct((t, 512), BF16)],
        compiler_params=_params(("parallel",)),
    )(cqn, ckvn, proj, wuq, wukv, gq, gk, gkr, tabs)


def prep2_bwd(cqn, ckvn, proj, wuq, wukv, dqf, dkf, dvp, gq, gk, gkr, tabs):
    t = cqn.shape[0]
    tm = A_TM

    def body(cq_ref, ckv_ref, kr_ref, wuq_ref, wukv_ref, dqf_ref, dkf_ref, dvp_ref, gq_ref, gk_ref, gkr_ref, tab_ref,
             dcq_ref, dckv_ref, dkr_ref, dwuq_ref, dwukv_ref, dgq_ref, dgk_ref, dgkr_ref, dq_ref, dkv_ref):
        i = pl.program_id(0)

        @pl.when(i == 0)
        def _():
            for ref in (dwuq_ref, dwukv_ref, dgq_ref, dgk_ref, dgkr_ref):
                ref[...] = jnp.zeros_like(ref)

        q_all, kv_all = _up_proj(cq_ref, ckv_ref, wuq_ref, wukv_ref)
        dgq = jnp.zeros((1, LANES), F32)
        dgk = jnp.zeros((1, LANES), F32)
        dkpe = jnp.zeros((tm, LANES), F32)
        for h in range(HEADS):
            sl = slice(LANES * h, LANES * (h + 1))
            lane = _lane((tm, LANES))
            mn, mr = lane < HD, (lane >= HD) & (lane < HD + ROPE)
            r, xn = _seg_norm(q_all[:, sl], Q_HEAD)
            dy = _rope_bwd(dqf_ref[sl, :].T, tab_ref[0], tab_ref[1], tab_ref[2])
            dyg = dy * gq_ref[...]
            dq_ref[:, sl] = _seg_norm_bwd(r, xn, dyg, Q_HEAD).astype(BF16)
            dgq = dgq + jnp.sum(dy * xn, axis=0, keepdims=True)

            x = kv_all[:, sl]
            dk = dkf_ref[:, sl]
            rk, xkn = _seg_norm(jnp.where(mn, x, 0.0), PAIR)
            dyk = jnp.where(mn, dk, 0.0)
            dxk = _seg_norm_bwd(rk, xkn, dyk * gk_ref[...], PAIR)
            dgk = dgk + jnp.sum(dyk * xkn, axis=0, keepdims=True)
            dkpe = dkpe + jnp.where(mr, dk, 0.0)
            dvpair = dvp_ref[:, LANES * (h // 2):LANES * (h // 2 + 1)]
            dv = _roll(dvpair, 64) if h % 2 == 0 else dvpair
            dkv_ref[:, sl] = jnp.where(mn, dxk, dv).astype(BF16)

        r, xn = _seg_norm(kr_ref[...], K_ROPE)
        dy = _rope_bwd(_roll(dkpe, 64), tab_ref[3], tab_ref[4], tab_ref[5])
        dkr_ref[...] = _seg_norm_bwd(r, xn, dy * gkr_ref[...], K_ROPE)
        dgq_ref[...] += dgq
        dgk_ref[...] += dgk
        dgkr_ref[...] += jnp.sum(dy * xn, axis=0, keepdims=True)
        dqb, dkvb = dq_ref[...], dkv_ref[...]
        dcq_ref[...] = _dot(dqb, wuq_ref[...])
        dckv_ref[...] = _dot_nt(dkvb, wukv_ref[...])
        dwuq_ref[...] += _dot_tn(dqb, cq_ref[...])
        dwukv_ref[...] += _dot_tn(ckv_ref[...], dkvb)

    row = lambda w: pl.BlockSpec((tm, w), lambda i: (i, 0))
    vec = lambda w: pl.BlockSpec((1, w), lambda i: (0, 0))
    full = lambda a: pl.BlockSpec(a.shape, lambda i: (0, 0))
    return pl.pallas_call(
        body, name="prep2_bwd", grid=(t // tm,),
        in_specs=[row(256), row(128), pl.BlockSpec((tm, LANES), lambda i: (i, 15)), full(wuq), full(wukv),
                  pl.BlockSpec((1024, tm), lambda i: (0, i)), row(1024), row(512),
                  vec(128), vec(128), vec(128), pl.BlockSpec((6, tm, LANES), lambda i: (0, i, 0))],
        out_specs=[row(256), row(128), row(128), full(wuq), full(wukv), vec(128), vec(128), vec(128)],
        out_shape=[jax.ShapeDtypeStruct((t, 256), F32), jax.ShapeDtypeStruct((t, 128), F32),
                   jax.ShapeDtypeStruct((t, LANES), F32), jax.ShapeDtypeStruct(wuq.shape, F32),
                   jax.ShapeDtypeStruct(wukv.shape, F32)] + [jax.ShapeDtypeStruct((1, LANES), F32)] * 3,
        scratch_shapes=[pltpu.VMEM((tm, 1024), BF16), pltpu.VMEM((tm, 1024), BF16)],
        compiler_params=_params(("arbitrary",)),
    )(cqn, ckvn, proj, wuq, wukv, dqf, dkf, dvp, gq, gk, gkr, tabs)


A_TQ = 256
A_WIN = 3 * A_TQ


def _a_specs(t):
    nb = t // A_TQ
    blk = lambda s: pl.BlockSpec((A_TQ, 512), lambda i: (jnp.maximum(i - s, 0), 0))
    return nb, blk


def _a_exp(q_ref, kc, b_ref, head, sl, lo):
    hm = lo if head % 2 == 0 else ~lo
    qm = jnp.where(hm, q_ref[:, sl], jnp.zeros((), BF16))
    s = _dot_nt(qm, kc) + b_ref[0, head]
    e = jnp.exp(s - jnp.max(s, axis=-1, keepdims=True))
    return hm, qm, e, 1.0 / jnp.sum(e, axis=-1, keepdims=True)


def _a_bias_spec():
    return pl.BlockSpec((1, HEADS, A_TQ, A_WIN), lambda i: (jnp.minimum(i, 2), 0, 0, 0))


def attn_a_fwd(qa, ka, va, bias):
    t = qa.shape[0]
    nb, blk = _a_specs(t)

    def body(q_ref, k2_ref, k1_ref, k0_ref, v2_ref, v1_ref, v0_ref, b_ref, o_ref):
        lo = _lane((A_TQ, LANES)) < HD
        for p in range(4):
            sl = slice(LANES * p, LANES * (p + 1))
            kc = jnp.concatenate([k2_ref[:, sl], k1_ref[:, sl], k0_ref[:, sl]], axis=0)
            vc = jnp.concatenate([v2_ref[:, sl], v1_ref[:, sl], v0_ref[:, sl]], axis=0)
            outs = []
            for h2 in range(2):
                _, _, e, inv = _a_exp(q_ref, kc, b_ref, 2 * p + h2, sl, lo)
                outs.append(_dot(e.astype(BF16), vc) * inv)
            o_ref[:, sl] = jnp.where(lo, outs[0], outs[1]).astype(BF16)

    return pl.pallas_call(
        body, name="attn_a_fwd", grid=(nb,),
        in_specs=[blk(0), blk(2), blk(1), blk(0), blk(2), blk(1), blk(0), _a_bias_spec()],
        out_specs=pl.BlockSpec((A_TQ, 512), lambda i: (i, 0)),
        out_shape=jax.ShapeDtypeStruct((t, 512), BF16),
        compiler_params=_params(("parallel",)),
    )(qa, ka, ka, ka, va, va, va, bias)


def attn_a_bwd(qa, ka, va, bias, do):
    t = qa.shape[0]
    nb, blk = _a_specs(t)

    def body(q_ref, k2_ref, k1_ref, k0_ref, v2_ref, v1_ref, v0_ref, b_ref, do_ref, dq_ref, dk_ref, dv_ref, db_ref):
        qb = pl.program_id(0)

        @pl.when(qb == 0)
        def _():
            db_ref[...] = jnp.zeros_like(db_ref)

        lo = _lane((A_TQ, LANES)) < HD
        for p in range(4):
            sl = slice(LANES * p, LANES * (p + 1))
            kc = jnp.concatenate([k2_ref[:, sl], k1_ref[:, sl], k0_ref[:, sl]], axis=0)
            vc = jnp.concatenate([v2_ref[:, sl], v1_ref[:, sl], v0_ref[:, sl]], axis=0)
            dqs = []
            dkt = jnp.zeros((LANES, A_WIN), F32)
            dvt = jnp.zeros((LANES, A_WIN), F32)
            for h2 in range(2):
                head = 2 * p + h2
                hm, qm, e, inv = _a_exp(q_ref, kc, b_ref, head, sl, lo)
                pr = e * inv
                dom = jnp.where(hm, do_ref[:, sl], jnp.zeros((), BF16))
                dp = _dot_nt(dom, vc)
                ds = pr * (dp - jnp.sum(pr * dp, axis=-1, keepdims=True))
                db_ref[head] += ds
                dsb = ds.astype(BF16)
                dqs.append(_dot(dsb, kc))
                dkt = dkt + _dot_tn(qm, dsb)
                dvt = dvt + _dot_tn(dom, pr.astype(BF16))
            dq_ref[:, sl] = jnp.where(lo, dqs[0], dqs[1])
            dkc, dvc = dkt.T, dvt.T
            for s in range(3):
                rows = slice(A_TQ * (2 - s), A_TQ * (3 - s))
                dk_ref[s, :, sl] = dkc[rows]
                dv_ref[s, :, sl] = dvc[rows]

    share = pl.BlockSpec((3, A_TQ, 512), lambda i: (0, i, 0))
    return pl.pallas_call(
        body, name="attn_a_bwd", grid=(nb,),
        in_specs=[blk(0), blk(2), blk(1), blk(0), blk(2), blk(1), blk(0), _a_bias_spec(), blk(0)],
        out_specs=[pl.BlockSpec((A_TQ, 512), lambda i: (i, 0)), share, share,
                   pl.BlockSpec((HEADS, A_TQ, A_WIN), lambda i: (0, 0, 0))],
        out_shape=[jax.ShapeDtypeStruct((t, 512), F32), jax.ShapeDtypeStruct((3, t, 512), F32),
                   jax.ShapeDtypeStruct((3, t, 512), F32), jax.ShapeDtypeStruct((HEADS, A_TQ, A_WIN), F32)],
        compiler_params=_params(("arbitrary",)),
    )(qa, ka, ka, ka, va, va, va, bias, do)


B_T = 1024


B_SCALE2 = B_SCALE * 1.4426950408889634
_B_ALL = slice(0, B_T)
_B_LO, _B_HI = slice(0, B_T // 2), slice(B_T // 2, B_T)
_B_DIAG = ((_B_LO, _B_LO), (_B_LO, _B_HI), (_B_HI, _B_HI))


def _tri_tables(n, by_query):
    pairs = [(i, j) for i in range(n) for j in range(i + 1)] if by_query else [(i, j) for j in range(n) for i in range(j, n)]
    return (np.asarray([p[0] for p in pairs], np.int32), np.asarray([p[1] for p in pairs], np.int32))


def _b_mask_t(s):
    kc = lax.broadcasted_iota(jnp.int32, s.shape, 0) // CHUNK
    qc = lax.broadcasted_iota(jnp.int32, s.shape, 1) // CHUNK
    return jnp.where(kc <= qc, s, NEG)


def attn_b_fwd(qf, kf, vp):
    t = qf.shape[0]
    n = t // B_T
    qtab, ktab = _tri_tables(n, by_query=True)

    def body(qt_ref, kt_ref, q_ref, k_ref, v_ref, o_ref, lse_ref, m_s, l_s, acc_s):
        qb, kb = qt_ref[pl.program_id(1)], kt_ref[pl.program_id(1)]

        @pl.when(kb == 0)
        def _():
            m_s[...] = jnp.full_like(m_s, NEG)
            l_s[...] = jnp.zeros_like(l_s)
            acc_s[...] = jnp.zeros_like(acc_s)

        def block(kr, qr, masked):
            v = v_ref[kr, :]
            for h2 in range(2):
                sl = slice(LANES * h2, LANES * (h2 + 1))
                s = _dot_nt(k_ref[kr, sl], q_ref[qr, sl])
                if masked:
                    s = _b_mask_t(s)
                m_prev = m_s[h2, :, qr]
                m_new = jnp.maximum(m_prev, jnp.max(s, axis=0, keepdims=True))
                alpha = jnp.exp2(m_prev - m_new)
                pr = jnp.exp2(s - m_new)
                l_s[h2, :, qr] = alpha * l_s[h2, :, qr] + jnp.sum(pr, axis=0, keepdims=True)
                acc_s[h2, :, qr] = alpha * acc_s[h2, :, qr] + _dot_tn(v, pr.astype(BF16))
                m_s[h2, :, qr] = m_new

        @pl.when(kb < qb)
        def _():
            block(_B_ALL, _B_ALL, False)

        @pl.when(kb == qb)
        def _():
            for kr, qr in _B_DIAG:
                block(kr, qr, kr == qr)
            for h2 in range(2):
                l = l_s[h2]
                rows = slice(HD * h2, HD * (h2 + 1))
                o_ref[rows, :] = (acc_s[h2, rows, :] * (1.0 / l)).astype(BF16)
                lse_ref[0, h2:h2 + 1, :] = m_s[h2] + jnp.log2(l)

    grid_spec = pltpu.PrefetchScalarGridSpec(
        num_scalar_prefetch=2, grid=(4, len(qtab)),
        in_specs=[pl.BlockSpec((B_T, 256), lambda p, s, qt, kt: (qt[s], p)),
                  pl.BlockSpec((B_T, 256), lambda p, s, qt, kt: (kt[s], p)),
                  pl.BlockSpec((B_T, LANES), lambda p, s, qt, kt: (kt[s], p))],
        out_specs=[pl.BlockSpec((LANES, B_T), lambda p, s, qt, kt: (p, qt[s])),
                   pl.BlockSpec((1, 2, B_T), lambda p, s, qt, kt: (p, 0, qt[s]))],
        scratch_shapes=[pltpu.VMEM((2, 1, B_T), F32), pltpu.VMEM((2, 1, B_T), F32), pltpu.VMEM((2, LANES, B_T), F32)])
    return pl.pallas_call(
        body, name="attn_b_fwd", grid_spec=grid_spec,
        out_shape=[jax.ShapeDtypeStruct((512, t), BF16), jax.ShapeDtypeStruct((4, 2, t), F32)],
        compiler_params=_params(("parallel", "arbitrary")),
    )(jnp.asarray(qtab), jnp.asarray(ktab), qf, kf, vp)


def attn_b_bwd(qf, kf, vp, do, do_t, o_t, lse, scatter=()):
    t = qf.shape[0]
    n = t // B_T
    qtab, ktab = _tri_tables(n, by_query=False)
    plan = ScatterPlan(scatter)
    m = plan.n
    last = len(qtab) - 1

    def body(*refs):
        qt_ref, kt_ref, q_ref, k_ref, v_ref, do_ref, dot_ref, ot_ref, lse_ref = refs[:9]
        ins, (dq_ref, dk_ref, dv_ref), outs = refs[9:9 + m], refs[9 + m:12 + m], refs[12 + m:12 + 2 * m]
        sems = refs[12 + 2 * m:]
        qb, kb = qt_ref[pl.program_id(1)], kt_ref[pl.program_id(1)]
        if m:
            pl.when((pl.program_id(0) == 0) & (pl.program_id(1) == 0))(lambda: plan.start(ins, outs, sems))

        @pl.when(pl.program_id(1) == 0)
        def _():
            dq_ref[...] = jnp.zeros_like(dq_ref)

        @pl.when(qb == kb)
        def _():
            dk_ref[...] = jnp.zeros_like(dk_ref)
            dv_ref[...] = jnp.zeros_like(dv_ref)

        def block(kr, qr, masked):
            nq = qr.stop - qr.start
            cols = pl.ds(pl.multiple_of(qb * B_T + qr.start, LANES), nq)
            v = v_ref[kr, :]
            dov = do_ref[qr, :]
            prod = dot_ref[:, qr].astype(F32) * ot_ref[:, qr].astype(F32)
            lo = _lane((nq, LANES)) < HD
            for h2 in range(2):
                sl = slice(LANES * h2, LANES * (h2 + 1))
                hm = lo if h2 == 0 else ~lo
                q = q_ref[qr, sl]
                k = k_ref[kr, sl]
                dom = jnp.where(hm, dov, jnp.zeros((), BF16))
                delta = jnp.sum(prod[HD * h2:HD * (h2 + 1), :], axis=0, keepdims=True)
                s = _dot_nt(k, q)
                if masked:
                    s = _b_mask_t(s)
                pr = jnp.exp2(s - lse_ref[0, h2:h2 + 1, qr])
                dp = _dot_nt(v, dom)
                ds = (pr * (dp - delta)).astype(BF16)
                dk_ref[kr, sl] += _dot(ds, q) * (B_SCALE / B_SCALE2)
                dv_ref[kr, :] += _dot(pr.astype(BF16), dom)
                dq_ref[sl, cols] += _dot_tn(k, ds) * B_SCALE

        @pl.when(qb > kb)
        def _():
            block(_B_ALL, _B_ALL, False)

        @pl.when(qb == kb)
        def _():
            for kr, qr in _B_DIAG:
                block(kr, qr, kr == qr)

        if m:
            pl.when((pl.program_id(0) == 3) & (pl.program_id(1) == last))(lambda: plan.finish(ins, outs, sems))

    qrow = lambda w: pl.BlockSpec((B_T, w), lambda p, s, qt, kt: (qt[s], p))
    qcol = pl.BlockSpec((LANES, B_T), lambda p, s, qt, kt: (p, qt[s]))
    krow = lambda w: pl.BlockSpec((B_T, w), lambda p, s, qt, kt: (kt[s], p))
    grid_spec = pltpu.PrefetchScalarGridSpec(
        num_scalar_prefetch=2, grid=(4, len(qtab)),
        in_specs=[qrow(256), krow(256), krow(LANES), qrow(LANES), qcol, qcol,
                  pl.BlockSpec((1, 2, B_T), lambda p, s, qt, kt: (p, 0, qt[s]))] + [ANY] * m,
        out_specs=[pl.BlockSpec((256, t), lambda p, s, qt, kt: (p, 0)), krow(256), krow(LANES)] + [ANY] * m,
        scratch_shapes=plan.scratch if m else [])
    return pl.pallas_call(
        body, name="attn_b_bwd", grid_spec=grid_spec,
        out_shape=[jax.ShapeDtypeStruct((1024, t), F32), jax.ShapeDtypeStruct((t, 1024), F32),
                   jax.ShapeDtypeStruct((t, 512), F32)] + plan.out_shape,
        compiler_params=_params(("arbitrary", "arbitrary")),
    )(jnp.asarray(qtab), jnp.asarray(ktab), qf, kf, vp, do, do_t, o_t, lse, *scatter)


_U_LEN = A_TQ + A_WIN - 1


def _band_mask():
    a = np.arange(A_TQ)[:, None] // CHUNK
    b = np.arange(A_WIN)[None, :] // CHUNK
    return (b >= a) & (b <= a + A_LEFT)


def bias_block(table):
    h = table.shape[0]
    n_lo = A_WIN - 1 - 2 * A_TQ - A_MAX_REL
    ext = jnp.concatenate([jnp.repeat(table[:, :1], n_lo, axis=1), table,
                           jnp.repeat(table[:, -1:], _U_LEN - n_lo - table.shape[1], axis=1)], axis=1)
    row = jnp.pad(ext[:, ::-1], ((0, 0), (0, 1)))[:, None, :]
    band = _band_mask()
    first = [band & (np.arange(A_WIN)[None, :] >= 2 * A_TQ - A_TQ * v) for v in range(3)]
    keep = jnp.asarray(np.stack(first), jnp.int32)

    def body(r_ref, k_ref, o_ref):
        rows = jnp.broadcast_to(r_ref[0], (A_TQ, _U_LEN + 1))
        skew = pltpu.roll(rows, _U_LEN + 1 - (A_TQ - 1), 1, stride=1, stride_axis=0)
        toep = skew[:, :A_WIN]
        for v in range(3):
            o_ref[v, 0] = jnp.where(k_ref[v] != 0, toep, NEG)

    return pl.pallas_call(
        body, name="bias_block", grid=(h,),
        in_specs=[pl.BlockSpec((1, 1, _U_LEN + 1), lambda i: (i, 0, 0)),
                  pl.BlockSpec((3, A_TQ, A_WIN), lambda i: (0, 0, 0))],
        out_specs=pl.BlockSpec((3, 1, A_TQ, A_WIN), lambda i: (0, i, 0, 0)),
        out_shape=jax.ShapeDtypeStruct((3, h, A_TQ, A_WIN), F32),
        compiler_params=_params(("parallel",)),
    )(row, keep)


def bias_block_grad(db):
    h = db.shape[0]
    n_lo = A_WIN - 1 - 2 * A_TQ - A_MAX_REL
    skew = jnp.pad(db, ((0, 0), (0, 0), (A_TQ - 1, 0)))
    flat = jnp.pad(skew.reshape(h, A_TQ * _U_LEN), ((0, 0), (0, A_TQ)))
    ext = jnp.sum(flat.reshape(h, A_TQ, _U_LEN + 1), axis=1)[:, :_U_LEN][:, ::-1]
    n_tab = 2 * A_MAX_REL + 1
    first = jnp.sum(ext[:, :n_lo + 1], axis=1, keepdims=True)
    last = jnp.sum(ext[:, n_lo + n_tab - 1:], axis=1, keepdims=True)
    return jnp.concatenate([first, ext[:, n_lo + 1:n_lo + n_tab - 1], last], axis=1)


def rope_tabs(t):
    inv = 1.0 / (10000.0 ** (jnp.arange(0, ROPE, 2, dtype=F32) / ROPE))
    ang = jnp.arange(t, dtype=F32)[:, None] * inv[None, :]
    cos, sin = jnp.cos(ang), jnp.sin(ang)
    z = lambda w: jnp.zeros((t, w), F32)
    ck = jnp.concatenate([cos, cos, z(96)], axis=1)
    s1k = jnp.concatenate([-sin, z(112)], axis=1)
    s2k = jnp.concatenate([z(16), sin, z(96)], axis=1)
    cq = jnp.concatenate([jnp.ones((t, HD), F32), cos, cos, z(32)], axis=1)
    s1q = jnp.concatenate([z(HD), -sin, z(48)], axis=1)
    s2q = jnp.concatenate([z(HD + 16), sin, z(32)], axis=1)
    return jnp.stack([cq, s1q, s2q, ck, s1k, s2k])


def _pad_lanes(v, width):
    return jnp.pad(v, ((0, 0), (0, width - v.shape[1])))


LATE = ("w_in", "b_w_uq", "b_w_ukv", "w_out", "ffn2_w_gate", "ffn2_w_up", "ffn2_w_down")
FFN2 = ("ffn2_w_gate", "ffn2_w_up", "ffn2_w_down")


def kernel_layout(gathered):
    w = {n: v for n, v in gathered.items() if n.startswith("ffn")}
    if "w_in" in gathered:
        w["w_in"] = jnp.pad(gathered["w_in"].reshape(IN_COLS, D_MODEL), ((0, PROJ_W - IN_COLS), (0, 0)))
        uq = gathered["b_w_uq"].reshape(HEADS, HD + ROPE, 256)
        w["b_w_uq"] = jnp.pad(uq, ((0, 0), (0, LANES - HD - ROPE), (0, 0))).reshape(HEADS * LANES, 256)
        w["b_w_ukv"] = _shards_to_cols(gathered["b_w_ukv"])
        w["w_out"] = gathered["w_out"].reshape(N_SHARD * gathered["w_out"].shape[1], D_MODEL)
    return w


def local_step(x, target, w, late=None):
    t = x.shape[0]
    gq = jnp.tile(w["a_q_norm"], (1, HEADS))
    gk = jnp.tile(w["a_k_norm"], (1, HEADS))
    gq128 = _pad_lanes(jnp.concatenate([w["b_q_nope_norm"], w["b_q_rope_norm"]], axis=1), LANES)
    gk128 = _pad_lanes(w["b_k_nope_norm"], LANES)
    gkr128 = _pad_lanes(w["b_k_rope_norm"], LANES)
    tabs = rope_tabs(t)
    bias = bias_block(w["a_rel_bias"])

    if late is None:
        x1, gate1, up1 = ffn_fwd(x, w["ffn1_norm"], w["ffn1_w_gate"], w["ffn1_w_up"], w["ffn1_w_down"], "ffn_fwd")
    else:
        own, shard = late
        x1, gate1, up1, *got = ffn_fwd(x, w["ffn1_norm"], w["ffn1_w_gate"], w["ffn1_w_up"], w["ffn1_w_down"],
                                       "ffn_fwd_gather", gather=own)
        w = dict(w, **kernel_layout({n: lax.dynamic_update_index_in_dim(g_, o_, shard, 0)
                                     for n, g_, o_ in zip(LATE, got, own)}))
    h, proj, qa, ka, va, cqn, ckvn = mix_fwd(x1, w["mix_norm"], w["w_in"], gq, gk,
                                             w["b_q_lat_norm"], w["b_kv_lat_norm"])
    qf, kf, vp = prep2_fwd(cqn, ckvn, proj, w["b_w_uq"], w["b_w_ukv"], gq128, gk128, gkr128, tabs)
    oa = attn_a_fwd(qa, ka, va, bias)
    ob_t, lse = attn_b_fwd(qf, kf, vp)
    g = {}
    dx3, gate2, up2, g["final_norm"], loss, x2 = ffn_fwd(
        x1, w["ffn2_norm"], w["ffn2_w_gate"], w["ffn2_w_up"], w["ffn2_w_down"], "ffn_fwd_loss",
        loss_head=(w["final_norm"], target), pre_proj=(oa, ob_t, w["w_out"]))
    g["ffn2_w_gate"], g["ffn2_w_up"], g["ffn2_w_down"], dhp = ffn_bwd(
        x2, dx3, w["ffn2_norm"], w["ffn2_w_gate"], w["ffn2_w_up"], w["ffn2_w_down"], gate2, up2, "ffn_bwd")
    dx2, g["ffn2_norm"], d_oa, d_ob, d_ob_t, g["w_out"] = out_proj_bwd(
        x2, w["ffn2_norm"], dhp, dx3, w["w_out"], oa, ob_t)
    early = [g[n] for n in FFN2] + [g["w_out"].reshape(N_SHARD, -1, D_MODEL)]
    dqf, dkf, dvp, *landed_early = attn_b_bwd(qf, kf, vp, d_ob, d_ob_t, ob_t, lse,
                                              scatter=() if late is None else early)
    dqa, dkp, dvpa, dbias = attn_a_bwd(qa, ka, va, bias, d_oa)
    dcq, dckv, dkr, dwuq, dwukv, dgq128, dgk128, dgkr128 = prep2_bwd(
        cqn, ckvn, proj, w["b_w_uq"], w["b_w_ukv"], dqf, dkf, dvp, gq128, gk128, gkr128, tabs)
    g["b_w_uq"], g["b_w_ukv"] = dwuq.astype(BF16), dwukv.astype(BF16)
    dx1, g["w_in"], g["mix_norm"], dgq, dgk, g["b_q_lat_norm"], g["b_kv_lat_norm"] = mix_bwd(
        proj, x1, h, dx2, w["w_in"], w["mix_norm"], dqa, dkp, dvpa, dcq, dckv, dkr, gq, gk,
        w["b_q_lat_norm"], w["b_kv_lat_norm"])
    mid = [g["w_in"][:IN_COLS].reshape(N_SHARD, IN_COLS // N_SHARD, D_MODEL),
           g["b_w_uq"].reshape(HEADS, LANES, 256)[:, :HD + ROPE].reshape(N_SHARD, -1, 256),
           _cols_to_shards(g["b_w_ukv"])]
    g["ffn1_w_gate"], g["ffn1_w_up"], g["ffn1_w_down"], dhp, *landed_late = ffn_bwd(
        x, dx1, w["ffn1_norm"], w["ffn1_w_gate"], w["ffn1_w_up"], w["ffn1_w_down"], gate1, up1,
        "ffn_bwd" if late is None else "ffn_bwd_scatter", scatter=() if late is None else mid,
        spread=None if late is None else late[1])
    grad_x, g["ffn1_norm"] = norm_bwd(x, w["ffn1_norm"], dhp, dx1, "ffn_norm_bwd")
    landed = dict(zip(FFN2 + ("w_out", "w_in", "b_w_uq", "b_w_ukv", "ffn1_w_gate", "ffn1_w_up", "ffn1_w_down"),
                      landed_early + landed_late))

    g["a_q_norm"] = jnp.sum(dgq.reshape(HEADS, HD), axis=0, keepdims=True)
    g["a_k_norm"] = jnp.sum(dgk.reshape(HEADS, HD), axis=0, keepdims=True)
    g["a_rel_bias"] = bias_block_grad(dbias)
    g["b_q_nope_norm"] = dgq128[:, :HD]
    g["b_q_rope_norm"] = dgq128[:, HD:HD + ROPE]
    g["b_k_nope_norm"] = dgk128[:, :HD]
    g["b_k_rope_norm"] = dgkr128[:, :ROPE]
    return loss, grad_x, g, landed


ANY = pl.BlockSpec(memory_space=pl.ANY)
N_DEV = 8


def _place():
    return lax.axis_index("x"), lax.axis_index("y"), lax.axis_index("c")


def _flip(v, bit):
    return 1 - v if bit else v


BF16_ROWS = 16


def _split_axis(shape):
    return 0 if (shape[0] // 2) % BF16_ROWS == 0 else 1


def _half_shape(shape):
    axis = _split_axis(shape)
    return tuple(s // 2 if a == axis else s for a, s in enumerate(shape))


def _half(shape, core):
    axis = _split_axis(shape)
    size = shape[axis] // 2
    return tuple(pl.ds(core * size, size) if a == axis else slice(None) for a in range(2))


class GatherPlan:
    def __init__(self, ws):
        self.shapes = [w.shape for w in ws]
        self.n = len(ws)
        self.out_shape = [jax.ShapeDtypeStruct((N_SHARD,) + w.shape, w.dtype) for w in ws]
        self.scratch = [pltpu.SemaphoreType.DMA((6 * self.n,)), pltpu.SemaphoreType.DMA((6 * self.n,))]

    def _copies(self, ins, outs, sems):
        x, y, c = _place()
        s_me = 2 * x + y
        sibling = (x, y, 1 - c)
        send_sems, recv_sems = sems

        def remote(k, src, dst, to):
            return pltpu.make_async_remote_copy(src_ref=src, dst_ref=dst, send_sem=send_sems.at[k],
                                                recv_sem=recv_sems.at[k], device_id=to, device_id_type=MESH)

        ici, fwd = [], []
        for a in range(self.n):
            mine, theirs = _half(self.shapes[a], c), _half(self.shapes[a], 1 - c)
            for j, (cx, cy) in enumerate([(1 - x, y), (x, 1 - y), (1 - x, 1 - y)]):
                got = outs[a].at[(2 * cx + cy,) + mine]
                ici.append((remote(6 * a + j, ins[a].at[mine], outs[a].at[(s_me,) + mine], (cx, cy, c)),
                            remote(6 * a + j, got, got, (cx, cy, c))))
                passed = outs[a].at[(2 * cx + cy,) + theirs]
                fwd.append((remote(6 * a + 3 + j, got, got, sibling), remote(6 * a + 3 + j, passed, passed, sibling)))
        return ici, fwd

    def start(self, ins, outs, sems):
        for send, _ in self._copies(ins, outs, sems)[0]:
            send.start()

    def forward(self, ins, outs, sems):
        ici, fwd = self._copies(ins, outs, sems)
        for (_, arrival), (send, _) in zip(ici, fwd):
            arrival.wait_recv()
            send.start()

    def finish(self, ins, outs, sems):
        ici, fwd = self._copies(ins, outs, sems)
        for _, arrival in fwd:
            arrival.wait_recv()
        for send, _ in ici + fwd:
            send.wait_send()


def allgather_shards(ws):
    plan = GatherPlan(ws)
    n = plan.n

    def body(*refs):
        ins, outs, sems = refs[:n], refs[n:2 * n], refs[2 * n:]
        plan.start(ins, outs, sems)
        plan.forward(ins, outs, sems)
        plan.finish(ins, outs, sems)

    return pl.pallas_call(
        body, name="allgather_shards", in_specs=[ANY] * n, out_specs=[ANY] * n,
        out_shape=plan.out_shape, scratch_shapes=plan.scratch,
    )(*ws)


class ScatterPlan:
    def __init__(self, gs):
        self.shapes = [g.shape[1:] for g in gs]
        self.n = len(gs)
        self.out_shape = [jax.ShapeDtypeStruct((N_DEV,) + _half_shape(g.shape[1:]), g.dtype) for g in gs]
        self.scratch = [pltpu.SemaphoreType.DMA((7 * self.n,)), pltpu.SemaphoreType.DMA((7 * self.n,)),
                        pltpu.SemaphoreType.DMA((self.n,))]

    def _copies(self, ins, outs, sems):
        x, y, c = _place()
        me = 4 * x + 2 * y + c
        send_sems, recv_sems, local_sems = sems
        local, sends, arrivals = [], [], []
        for a in range(self.n):
            piece = lambda px, py, pc, a=a: ins[a].at[(2 * px + py,) + _half(self.shapes[a], pc)]
            local.append(pltpu.make_async_copy(piece(x, y, c), outs[a].at[me], local_sems.at[a]))
            for k in range(1, N_DEV):
                px, py, pc = _flip(x, k & 4), _flip(y, k & 2), _flip(c, k & 1)
                sem = dict(send_sem=send_sems.at[7 * a + k - 1], recv_sem=recv_sems.at[7 * a + k - 1],
                           device_id=(px, py, pc), device_id_type=MESH)
                sends.append(pltpu.make_async_remote_copy(
                    src_ref=piece(px, py, pc), dst_ref=outs[a].at[me], **sem))
                slot = outs[a].at[4 * px + 2 * py + pc]
                arrivals.append(pltpu.make_async_remote_copy(src_ref=slot, dst_ref=slot, **sem))
        return local, sends, arrivals

    def start(self, ins, outs, sems):
        local, sends, _ = self._copies(ins, outs, sems)
        for cp in local + sends:
            cp.start()

    def finish(self, ins, outs, sems):
        local, sends, arrivals = self._copies(ins, outs, sems)
        for cp in arrivals:
            cp.wait_recv()
        for cp in sends:
            cp.wait_send()
        for cp in local:
            cp.wait()


def _row_tile(rows, row_bytes, budget, multiple):
    fits = [r for r in range(multiple, rows + 1, multiple) if rows % r == 0 and r * row_bytes <= budget]
    return max(fits) if fits else rows


SMALL_KERNEL_VMEM = 32 * 1024 * 1024


def sum_slots(lands, name):
    k = len(lands)
    _, rows, cols = lands[0].shape
    tr = _row_tile(rows, k * 2 * (N_DEV * cols * 2 + cols * 4), SMALL_KERNEL_VMEM, BF16_ROWS)

    def body(*refs):
        for l_ref, o_ref in zip(refs[:k], refs[k:]):
            acc = l_ref[0].astype(F32)
            for s in range(1, N_DEV):
                acc = acc + l_ref[s].astype(F32)
            o_ref[...] = acc

    return pl.pallas_call(
        body, name=name, grid=(rows // tr,),
        in_specs=[pl.BlockSpec((N_DEV, tr, cols), lambda i: (0, i, 0))] * k,
        out_specs=[pl.BlockSpec((tr, cols), lambda i: (i, 0))] * k,
        out_shape=[jax.ShapeDtypeStruct((rows, cols), F32)] * k,
        compiler_params=_params(("parallel",)),
    )(*lands)


def join_halves(hs, shapes):
    n = len(hs)

    def body(*refs):
        ins, outs = refs[:n], refs[n:2 * n]
        send_sems, recv_sems = refs[2 * n:]
        x, y, c = _place()
        sends = []
        for a in range(n):
            mine = outs[a].at[_half(shapes[a], c)]
            sends.append(pltpu.make_async_remote_copy(
                src_ref=ins[a], dst_ref=mine, send_sem=send_sems.at[a], recv_sem=recv_sems.at[a],
                device_id=(x, y, 1 - c), device_id_type=MESH))
            sends[-1].start()
        for a in range(n):
            theirs = outs[a].at[_half(shapes[a], 1 - c)]
            pltpu.make_async_remote_copy(
                src_ref=theirs, dst_ref=theirs, send_sem=send_sems.at[a], recv_sem=recv_sems.at[a],
                device_id=(x, y, 1 - c), device_id_type=MESH).wait_recv()
        for cp in sends:
            cp.wait_send()

    return pl.pallas_call(
        body, name="join_halves",
        in_specs=[ANY] * n, out_specs=[ANY] * n,
        out_shape=[jax.ShapeDtypeStruct(tuple(s), h.dtype) for s, h in zip(shapes, hs)],
        scratch_shapes=[pltpu.SemaphoreType.DMA((n,)), pltpu.SemaphoreType.DMA((n,))],
    )(*hs)


def allreduce_small(vec):
    def body(v_ref, o_ref, land_ref, send_sems, recv_sems):
        x, y, c = _place()
        me = 4 * x + 2 * y + c
        land_ref[me] = v_ref[...]
        sends = []
        for k in range(1, N_DEV):
            px, py, pc = _flip(x, k & 4), _flip(y, k & 2), _flip(c, k & 1)
            sends.append(pltpu.make_async_remote_copy(
                src_ref=v_ref, dst_ref=land_ref.at[me], send_sem=send_sems.at[k - 1], recv_sem=recv_sems.at[k - 1],
                device_id=(px, py, pc), device_id_type=MESH))
            sends[-1].start()
        for k in range(1, N_DEV):
            px, py, pc = _flip(x, k & 4), _flip(y, k & 2), _flip(c, k & 1)
            slot = land_ref.at[4 * px + 2 * py + pc]
            pltpu.make_async_remote_copy(
                src_ref=slot, dst_ref=slot, send_sem=send_sems.at[k - 1], recv_sem=recv_sems.at[k - 1],
                device_id=(px, py, pc), device_id_type=MESH).wait_recv()
        for cp in sends:
            cp.wait_send()
        acc = land_ref[0]
        for s in range(1, N_DEV):
            acc = acc + land_ref[s]
        o_ref[...] = acc

    vm = pl.BlockSpec(memory_space=pltpu.VMEM)
    return pl.pallas_call(
        body, name="allreduce_small",
        in_specs=[vm], out_specs=vm,
        out_shape=jax.ShapeDtypeStruct(vec.shape, F32),
        scratch_shapes=[pltpu.VMEM((N_DEV,) + vec.shape, F32), pltpu.SemaphoreType.DMA((N_DEV - 1,)),
                        pltpu.SemaphoreType.DMA((N_DEV - 1,))],
    )(vec)


def adamw(ws, gs, ms, vs, name):
    k = len(ws)
    rows, cols = ws[0].shape
    tr = _row_tile(rows, k * 2 * 7 * cols * 4, SMALL_KERNEL_VMEM, 8)
    c1 = 1.0 - ADAM_B1 ** ADAM_STEP
    c2 = 1.0 - ADAM_B2 ** ADAM_STEP

    def body(*refs):
        for a in range(k):
            w_ref, g_ref, m_ref, v_ref = (refs[s * k + a] for s in range(4))
            d_ref, nm_ref, nv_ref = (refs[(4 + s) * k + a] for s in range(3))
            gv = g_ref[...]
            nm = ADAM_B1 * m_ref[...] + (1.0 - ADAM_B1) * gv
            nv = ADAM_B2 * v_ref[...] + (1.0 - ADAM_B2) * (gv * gv)
            nm_ref[...] = nm
            nv_ref[...] = nv
            d_ref[...] = -ADAM_LR * ((nm / c1) / (jnp.sqrt(nv / c2) + ADAM_EPS) + ADAM_WD * w_ref[...])

    blk = pl.BlockSpec((tr, cols), lambda i: (i, 0))
    out = pl.pallas_call(
        body, name=name, grid=(rows // tr,),
        in_specs=[blk] * (4 * k), out_specs=[blk] * (3 * k),
        out_shape=[jax.ShapeDtypeStruct((rows, cols), F32)] * (3 * k),
        compiler_params=_params(("parallel",)),
    )(*ws, *gs, *ms, *vs)
    return [(out[a], out[k + a], out[2 * k + a]) for a in range(k)]


BIG = ("ffn1_w_gate", "ffn1_w_up", "ffn1_w_down", "w_in", "b_w_uq", "b_w_ukv", "w_out",
       "ffn2_w_gate", "ffn2_w_up", "ffn2_w_down")
SMALL = ("ffn1_norm", "mix_norm", "a_q_norm", "a_k_norm", "a_rel_bias", "b_q_lat_norm", "b_kv_lat_norm",
         "b_q_nope_norm", "b_q_rope_norm", "b_k_nope_norm", "b_k_rope_norm", "ffn2_norm", "final_norm")
WEIGHTS = ("ffn1_norm", "ffn1_w_gate", "ffn1_w_up", "ffn1_w_down", "mix_norm", "w_in", "a_q_norm", "a_k_norm",
           "a_rel_bias", "b_q_lat_norm", "b_w_uq", "b_kv_lat_norm", "b_w_ukv", "b_q_nope_norm", "b_q_rope_norm",
           "b_k_nope_norm", "b_k_rope_norm", "w_out", "ffn2_norm", "ffn2_w_gate", "ffn2_w_up", "ffn2_w_down",
           "final_norm")
TRANSPOSED = ("ffn1_w_gate", "ffn1_w_up", "ffn2_w_gate", "ffn2_w_up", "w_in", "b_w_uq")
PACK_SHAPE = (8, 1024)


def _pack_small(d, last=None):
    flat = [d[n].reshape(-1) for n in SMALL]
    used = sum(f.shape[0] for f in flat)
    total = PACK_SHAPE[0] * PACK_SHAPE[1]
    tail = jnp.zeros((total - used - 1,), F32)
    end = jnp.zeros((1,), F32) if last is None else last.reshape(1)
    return jnp.concatenate(flat + [tail, end]).reshape(PACK_SHAPE)


def _unpack_small(p, like):
    flat = p.reshape(-1)
    out, off = {}, 0
    for n in SMALL:
        size = like[n].size
        out[n] = flat[off:off + size].reshape(like[n].shape)
        off += size
    return out, flat[-1]


def _cols_to_shards(g):
    rows, cols = g.shape
    return g.reshape(rows, N_SHARD, cols // N_SHARD).transpose(1, 0, 2)


def _shards_to_cols(g):
    return g.transpose(1, 0, 2).reshape(g.shape[1], -1)


def kernel(x, ffn1_norm, ffn1_w_gate, ffn1_w_up, ffn1_w_down, mix_norm, w_in, a_q_norm, a_k_norm, a_rel_bias, b_q_lat_norm, b_w_uq, b_kv_lat_norm, b_w_ukv, b_q_nope_norm, b_q_rope_norm, b_k_nope_norm, b_k_rope_norm, w_out, ffn2_norm, ffn2_w_gate, ffn2_w_up, ffn2_w_down, final_norm, loss_target, m_ffn1_norm, m_ffn1_w_gate, m_ffn1_w_up, m_ffn1_w_down, m_mix_norm, m_w_in, m_a_q_norm, m_a_k_norm, m_a_rel_bias, m_b_q_lat_norm, m_b_w_uq, m_b_kv_lat_norm, m_b_w_ukv, m_b_q_nope_norm, m_b_q_rope_norm, m_b_k_nope_norm, m_b_k_rope_norm, m_w_out, m_ffn2_norm, m_ffn2_w_gate, m_ffn2_w_up, m_ffn2_w_down, m_final_norm, v_ffn1_norm, v_ffn1_w_gate, v_ffn1_w_up, v_ffn1_w_down, v_mix_norm, v_w_in, v_a_q_norm, v_a_k_norm, v_a_rel_bias, v_b_q_lat_norm, v_b_w_uq, v_b_kv_lat_norm, v_b_w_ukv, v_b_q_nope_norm, v_b_q_rope_norm, v_b_k_nope_norm, v_b_k_rope_norm, v_w_out, v_ffn2_norm, v_ffn2_w_gate, v_ffn2_w_up, v_ffn2_w_down, v_final_norm):
    args = locals()
    view = lambda a, n: a[0].T if n in TRANSPOSED else a[0]
    wts = {n: view(args[n], n) for n in WEIGHTS}
    mom = {n: view(args["m_" + n], n) for n in WEIGHTS}
    var = {n: view(args["v_" + n], n) for n in WEIGHTS}

    shard = 2 * lax.axis_index("x") + lax.axis_index("y")
    core = lax.axis_index("c")
    first = [n for n in BIG if n not in LATE]
    own = [wts[n].astype(BF16) for n in first]
    w = {n: wts[n] if n == "a_rel_bias" else wts[n][None] for n in SMALL}
    w.update(kernel_layout({n: lax.dynamic_update_index_in_dim(got, mine, shard, 0)
                            for n, got, mine in zip(first, allgather_shards(own), own)}))

    loss, grad_x, g, landed = local_step(x[0], loss_target[0], w,
                                         late=([wts[n].astype(BF16) for n in LATE], shard))

    me = 2 * shard + core
    for n in first:
        piece = lax.dynamic_slice(g[n], (shard, core * (FS // 2), 0), (1, FS // 2, D_MODEL))
        landed[n] = lax.dynamic_update_slice(landed[n], piece, (me, 0, 0))
    groups = {}
    for n in BIG:
        groups.setdefault(wts[n].shape, []).append(n)
    half = {}
    for names in groups.values():
        half.update(zip(names, sum_slots([landed[n] for n in names], "sum_slots")))
    halves = [half[n] for n in BIG]
    shapes = [wts[n].shape for n in BIG]
    axes = [_split_axis(s) for s in shapes]
    grads = dict(zip(BIG, (lax.dynamic_update_slice_in_dim(got, mine, core * mine.shape[ax], ax)
                           for got, mine, ax in zip(join_halves(halves, shapes), halves, axes))))

    small_sum, loss_sum = _unpack_small(allreduce_small(_pack_small(g, loss[0, 0])), wts)
    grads.update(small_sum)

    delta, new_m, new_v = {}, {}, {}
    for names in groups.values():
        stepped = adamw(*([d[n] for n in names] for d in (wts, grads, mom, var)), "adamw")
        for n, (d_new, m_new, v_new) in zip(names, stepped):
            delta[n], new_m[n], new_v[n] = d_new, m_new, v_new
    (packed,) = adamw([_pack_small(wts)], [_pack_small(grads)], [_pack_small(mom)], [_pack_small(var)], "adamw_small")
    for dst, p in zip((delta, new_m, new_v), packed):
        dst.update(_unpack_small(p, wts)[0])

    lead = lambda d: [(d[n].T if n in TRANSPOSED else d[n])[None] for n in WEIGHTS]
    return (loss_sum, grad_x[None], *lead(grads), *lead(delta), *lead(new_m), *lead(new_v))
```

```python
import numpy as np
import jax
import jax.numpy as jnp
from jax import lax
from jax.experimental import pallas as pl
from jax.experimental.pallas import tpu as pltpu

F32 = jnp.float32
BF16 = jnp.bfloat16
EPS = 1e-6
NEG = -1e30

D_MODEL = 1024
D_FF = 2816
N_SHARD = 4
FS = D_FF // N_SHARD
CHUNK = 64
A_LEFT = 8
A_MAX_REL = 128
HEADS = 8
HD = 64
ROPE = 32
PROJ_W = 2048
IN_COLS = 1952
B_SCALE = 96 ** -0.5
LANES = 128

ADAM_LR = 0.001
ADAM_B1 = 0.9
ADAM_B2 = 0.999
ADAM_EPS = 1e-08
ADAM_WD = 0.01
ADAM_STEP = 10

VMEM_LIMIT = 56 * 1024 * 1024
TOKEN_TILE = 512

MESH = pl.DeviceIdType.MESH


def _dot(a, b):
    return lax.dot_general(a, b, (((1,), (0,)), ((), ())), preferred_element_type=F32)


def _dot_nt(a, b):
    return lax.dot_general(a, b, (((1,), (1,)), ((), ())), preferred_element_type=F32)


def _dot_tn(a, b):
    return lax.dot_general(a, b, (((0,), (0,)), ((), ())), preferred_element_type=F32)


def _params(sem):
    return pltpu.CompilerParams(dimension_semantics=sem, vmem_limit_bytes=VMEM_LIMIT)


def _rms(xv):
    r = lax.rsqrt(jnp.mean(xv * xv, axis=-1, keepdims=True) + EPS)
    return r, xv * r


def ffn_fwd(x, g, wg, wu, wd, name, gather=(), loss_head=None, pre_proj=None):
    t, d = x.shape
    tm = TOKEN_TILE
    ni = t // tm
    plan = GatherPlan(gather)
    n = plan.n
    head = () if loss_head is None else tuple(loss_head)
    q = len(head)
    proj = () if pre_proj is None else tuple(pre_proj)
    r = len(proj)
    half = proj[0].shape[1] if r else 0

    def body(*refs):
        it = iter(refs)
        take = lambda count: [next(it) for _ in range(count)]
        x_ref, g_ref, wg_ref, wu_ref, wd_ref = take(5)
        head_in, proj_in, ins = take(q), take(r), take(n)
        o_ref, gp_ref, up_ref = take(3)
        head_out, x_out, outs = take(q), take(1 if r else 0), take(n)
        h_ref, acc_ref = take(2)
        sems = list(it)
        i, j = pl.program_id(0), pl.program_id(1)
        if n:
            pl.when((i == 0) & (j == 0))(lambda: plan.start(ins, outs, sems))
            pl.when((i == (3 * ni) // 4) & (j == 0))(lambda: plan.forward(ins, outs, sems))

        @pl.when(j == 0)
        def _():
            xin = x_ref[...]
            if r:
                oa_ref, obt_ref, wo_ref = proj_in
                xin = xin + _dot(oa_ref[...], wo_ref[0:half, :]) + _dot_tn(obt_ref[...], wo_ref[half:2 * half, :])
                x_out[0][...] = xin
            _, xn = _rms(xin)
            h_ref[...] = (xn * g_ref[...]).astype(BF16)
            acc_ref[...] = jnp.zeros_like(acc_ref)

        h = h_ref[...]
        gp = _dot_nt(h, wg_ref[0])
        up = _dot_nt(h, wu_ref[0])
        gp_ref[0] = gp
        up_ref[0] = up
        a = (gp * jax.nn.sigmoid(gp) * up).astype(BF16)
        acc_ref[...] += _dot(a, wd_ref[0])

        @pl.when(j == N_SHARD - 1)
        def _():
            y = (x_out[0][...] if r else x_ref[...]) + 0.5 * acc_ref[...]
            if not q:
                o_ref[...] = y
                return
            (gf_ref, t_ref), (dgf_ref, loss_ref) = head_in, head_out
            rinv, yn = _rms(y)
            gf = gf_ref[...]
            e = yn * gf - t_ref[...]
            dout = e * (1.0 / d)
            dng = dout * gf
            o_ref[...] = rinv * (dng - yn * jnp.mean(dng * yn, axis=-1, keepdims=True))

            @pl.when(i == 0)
            def _():
                dgf_ref[...] = jnp.zeros_like(dgf_ref)
                loss_ref[...] = jnp.zeros_like(loss_ref)

            dgf_ref[...] += jnp.sum(dout * yn, axis=0, keepdims=True)
            part = jnp.sum(jnp.sum(e * e, axis=-1, keepdims=True), axis=0, keepdims=True) * (0.5 / d)
            loss_ref[...] += jnp.broadcast_to(part, loss_ref.shape)

        if n:
            pl.when((i == ni - 1) & (j == N_SHARD - 1))(lambda: plan.finish(ins, outs, sems))

    tok = pl.BlockSpec((tm, d), lambda i, j: (i, 0))
    vec = pl.BlockSpec((1, d), lambda i, j: (0, 0))
    chunk = pl.BlockSpec((1, FS, d), lambda i, j: (j, 0, 0))
    pre = pl.BlockSpec((1, tm, FS), lambda i, j: (j, i, 0))
    return pl.pallas_call(
        body, name=name, grid=(ni, N_SHARD),
        in_specs=[tok, vec, chunk, chunk, chunk] + [vec, tok][:q]
        + [pl.BlockSpec((tm, half), lambda i, j: (i, 0)), pl.BlockSpec((half, tm), lambda i, j: (0, i)),
           pl.BlockSpec((2 * half, d), lambda i, j: (0, 0))][:r] + [ANY] * n,
        out_specs=[tok, pre, pre] + [vec, pl.BlockSpec((1, LANES), lambda i, j: (0, 0))][:q] + [tok][:r] + [ANY] * n,
        out_shape=[jax.ShapeDtypeStruct((t, d), F32), jax.ShapeDtypeStruct((N_SHARD, t, FS), F32),
                   jax.ShapeDtypeStruct((N_SHARD, t, FS), F32)]
        + [jax.ShapeDtypeStruct((1, d), F32), jax.ShapeDtypeStruct((1, LANES), F32)][:q]
        + [jax.ShapeDtypeStruct((t, d), F32)][:r] + plan.out_shape,
        scratch_shapes=[pltpu.VMEM((tm, d), BF16), pltpu.VMEM((tm, d), F32)] + (plan.scratch if n else []),
        compiler_params=_params(("arbitrary", "arbitrary")),
    )(x, g, wg, wu, wd, *head, *proj, *gather)


def ffn_bwd(x, dout, g, wg, wu, wd, gate, up_pre, name, scatter=(), spread=None):
    t, d = x.shape
    tm = TOKEN_TILE
    ni = t // tm
    hf = FS // 2
    plan = ScatterPlan(scatter)
    m = plan.n
    k = 0 if spread is None else 3
    steps = jnp.arange(N_SHARD, dtype=jnp.int32)
    order = steps if spread is None else (spread + 1 + steps) % N_SHARD

    def body(*refs):
        ord_ref, x_ref, do_ref, g_ref, wg_ref, wu_ref, wd_ref, gp_ref, up_ref = refs[:9]
        ins, (dwg_out, dwu_out, dwd_out, dhp_ref) = refs[9:9 + m], refs[9 + m:13 + m]
        outs, lands = refs[13 + m:13 + 2 * m], refs[13 + 2 * m:13 + 2 * m + k]
        dwg_ref, dwu_ref, dwd_ref = refs[13 + 2 * m + k:16 + 2 * m + k]
        sems = refs[16 + 2 * m + k:19 + 2 * m + k] if m else ()
        stage_ref = refs[-3] if k else None
        j, i = pl.program_id(0), pl.program_id(1)
        if m:
            pl.when((j == 0) & (i == 0))(lambda: plan.start(ins, outs, sems))

        def chunk_copies(jj):
            send_sems, recv_sems = refs[-2:]
            px, py, pc = _place()
            me = 4 * px + 2 * py + pc
            tx, ty = ord_ref[jj] // 2, ord_ref[jj] % 2
            copies = []
            for n_ in range(3):
                for h_ in range(2):
                    copies.append((pltpu.make_async_remote_copy(
                        src_ref=stage_ref.at[n_, pl.ds(h_ * hf, hf)], dst_ref=lands[n_].at[me],
                        send_sem=send_sems.at[6 * jj + 2 * n_ + h_], recv_sem=recv_sems.at[3 * me + n_],
                        device_id=(tx, ty, h_), device_id_type=MESH), (tx != px) | (ty != py) | (pc != h_)))
            return copies

        def arrivals():
            send_sems, recv_sems = refs[-2:]
            px, py, pc = _place()
            me = 4 * px + 2 * py + pc
            for s_ in range(N_DEV):
                for n_ in range(3):
                    slot = lands[n_].at[s_]
                    cp = pltpu.make_async_remote_copy(
                        src_ref=slot, dst_ref=slot, send_sem=send_sems.at[0], recv_sem=recv_sems.at[3 * s_ + n_],
                        device_id=(px, py, pc), device_id_type=MESH)
                    pl.when(me != s_)(cp.wait_recv)

        _, xn = _rms(x_ref[...])
        h = (xn * g_ref[...]).astype(BF16)
        dz = (0.5 * do_ref[...]).astype(BF16)
        wgv, wuv, wdv = wg_ref[0], wu_ref[0], wd_ref[0]
        gp, up = gp_ref[0], up_ref[0]
        s = jax.nn.sigmoid(gp)
        sg = gp * s
        a = (sg * up).astype(BF16)
        da = _dot_nt(dz, wdv)
        dup = (da * sg).astype(BF16)
        dgp = (da * up * (s * (1.0 + gp * (1.0 - s)))).astype(BF16)

        @pl.when(i == 0)
        def _():
            dwg_ref[...] = jnp.zeros_like(dwg_ref)
            dwu_ref[...] = jnp.zeros_like(dwu_ref)
            dwd_ref[...] = jnp.zeros_like(dwd_ref)

        dwd_ref[...] += _dot_tn(a, dz)
        dwg_ref[...] += _dot_tn(dgp, h)
        dwu_ref[...] += _dot_tn(dup, h)
        dhp_ref[0] = (_dot(dgp, wgv) + _dot(dup, wuv)).astype(BF16)

        @pl.when(i == ni - 1)
        def _():
            dwg_out[0] = dwg_ref[...].astype(BF16)
            dwu_out[0] = dwu_ref[...].astype(BF16)
            dwd_out[0] = dwd_ref[...].astype(BF16)
            if k:
                @pl.when(j >= 1)
                def _():
                    for cp, leaves in chunk_copies(j - 1):
                        pl.when(leaves)(cp.wait_send)
                for n_, acc in enumerate((dwg_ref, dwu_ref, dwd_ref)):
                    stage_ref[n_] = acc[...].astype(BF16)
                for cp, leaves in chunk_copies(j):
                    pl.when(leaves)(cp.start)

                @pl.when(j == N_SHARD - 1)
                def _():
                    for cp, leaves in chunk_copies(N_SHARD - 1):
                        pl.when(leaves)(cp.wait_send)
                    arrivals()

        if m:
            pl.when((j == N_SHARD - 1) & (i == ni - 1))(lambda: plan.finish(ins, outs, sems))

    chunk = pl.BlockSpec((1, FS, d), lambda j, i, o: (o[j], 0, 0))
    tok = pl.BlockSpec((tm, d), lambda j, i, o: (i, 0))
    pre = pl.BlockSpec((1, tm, FS), lambda j, i, o: (o[j], i, 0))
    grid_spec = pltpu.PrefetchScalarGridSpec(
        num_scalar_prefetch=1, grid=(N_SHARD, ni),
        in_specs=[tok, tok, pl.BlockSpec((1, d), lambda j, i, o: (0, 0)), chunk, chunk, chunk, pre, pre] + [ANY] * m,
        out_specs=[chunk, chunk, chunk, pl.BlockSpec((1, tm, d), lambda j, i, o: (o[j], i, 0))] + [ANY] * (m + k),
        scratch_shapes=[pltpu.VMEM((FS, d), F32), pltpu.VMEM((FS, d), F32), pltpu.VMEM((FS, d), F32)]
        + (plan.scratch if m else [])
        + ([pltpu.VMEM((3, FS, d), BF16), pltpu.SemaphoreType.DMA((6 * N_SHARD,)),
            pltpu.SemaphoreType.DMA((3 * N_DEV,))] if k else []))
    return pl.pallas_call(
        body, name=name, grid_spec=grid_spec,
        out_shape=[jax.ShapeDtypeStruct((N_SHARD, FS, d), BF16),
                   jax.ShapeDtypeStruct((N_SHARD, FS, d), BF16),
                   jax.ShapeDtypeStruct((N_SHARD, FS, d), BF16),
                   jax.ShapeDtypeStruct((N_SHARD, t, d), BF16)] + plan.out_shape
        + [jax.ShapeDtypeStruct((N_DEV, hf, d), BF16)] * k,
        compiler_params=_params(("arbitrary", "arbitrary")),
    )(order, x, dout, g, wg, wu, wd, gate, up_pre, *scatter)


def norm_bwd(x, g, dhp, dres, name):
    t, d = x.shape
    p = dhp.shape[0]
    tm = TOKEN_TILE

    def body(x_ref, g_ref, dhp_ref, dres_ref, dx_ref, dg_ref):
        i = pl.program_id(0)
        r, xn = _rms(x_ref[...])
        dh = dhp_ref[0].astype(F32)
        for q in range(1, p):
            dh = dh + dhp_ref[q].astype(F32)
        dhg = dh * g_ref[...]
        dx_ref[...] = dres_ref[...] + r * (dhg - xn * jnp.mean(dhg * xn, axis=-1, keepdims=True))

        @pl.when(i == 0)
        def _():
            dg_ref[...] = jnp.zeros_like(dg_ref)

        dg_ref[...] += jnp.sum(dh * xn, axis=0, keepdims=True)

    return pl.pallas_call(
        body, name=name, grid=(t // tm,),
        in_specs=[pl.BlockSpec((tm, d), lambda i: (i, 0)),
                  pl.BlockSpec((1, d), lambda i: (0, 0)),
                  pl.BlockSpec((p, tm, d), lambda i: (0, i, 0)),
                  pl.BlockSpec((tm, d), lambda i: (i, 0))],
        out_specs=[pl.BlockSpec((tm, d), lambda i: (i, 0)),
                   pl.BlockSpec((1, d), lambda i: (0, 0))],
        out_shape=[jax.ShapeDtypeStruct((t, d), F32), jax.ShapeDtypeStruct((1, d), F32)],
        compiler_params=_params(("arbitrary",)),
    )(x, g, dhp, dres)


def out_proj_bwd(x, g, dhp, dres, w, oa, ob_t):
    t, d = x.shape
    half = w.shape[0] // 2
    p = dhp.shape[0]
    tm = TOKEN_TILE
    ni = t // tm

    def body(x_ref, g_ref, dhp_ref, dres_ref, w_ref, oa_ref, obt_ref,
             dx_ref, dg_ref, da_ref, db_ref, dbt_ref, dw_ref, acc_ref):
        i = pl.program_id(0)

        @pl.when(i == 0)
        def _():
            dg_ref[...] = jnp.zeros_like(dg_ref)
            acc_ref[...] = jnp.zeros_like(acc_ref)

        r, xn = _rms(x_ref[...])
        dh = dhp_ref[0].astype(F32)
        for s in range(1, p):
            dh = dh + dhp_ref[s].astype(F32)
        dhg = dh * g_ref[...]
        dx = dres_ref[...] + r * (dhg - xn * jnp.mean(dhg * xn, axis=-1, keepdims=True))
        dx_ref[...] = dx
        dg_ref[...] += jnp.sum(dh * xn, axis=0, keepdims=True)
        dxb = dx.astype(BF16)
        da_ref[...] = _dot_nt(dxb, w_ref[0:half, :]).astype(BF16)
        db_ref[...] = _dot_nt(dxb, w_ref[half:2 * half, :]).astype(BF16)
        dbt_ref[...] = _dot_nt(w_ref[half:2 * half, :], dxb).astype(BF16)
        acc_ref[0:half, :] += _dot_tn(oa_ref[...], dxb)
        acc_ref[half:2 * half, :] += _dot(obt_ref[...], dxb)

        @pl.when(i == ni - 1)
        def _():
            dw_ref[...] = acc_ref[...].astype(BF16)

    row = lambda w_: pl.BlockSpec((tm, w_), lambda i: (i, 0))
    col = pl.BlockSpec((half, tm), lambda i: (0, i))
    whole = pl.BlockSpec((2 * half, d), lambda i: (0, 0))
    vec = pl.BlockSpec((1, d), lambda i: (0, 0))
    return pl.pallas_call(
        body, name="out_proj_bwd", grid=(ni,),
        in_specs=[row(d), vec, pl.BlockSpec((p, tm, d), lambda i: (0, i, 0)), row(d), whole, row(half), col],
        out_specs=[row(d), vec, row(half), row(half), col, whole],
        out_shape=[jax.ShapeDtypeStruct((t, d), F32), jax.ShapeDtypeStruct((1, d), F32),
                   jax.ShapeDtypeStruct((t, half), BF16), jax.ShapeDtypeStruct((t, half), BF16),
                   jax.ShapeDtypeStruct((half, t), BF16), jax.ShapeDtypeStruct((2 * half, d), BF16)],
        scratch_shapes=[pltpu.VMEM((2 * half, d), F32)],
        compiler_params=_params(("arbitrary",)),
    )(x, g, dhp, dres, w, oa, ob_t)


def _lane(shape):
    return lax.broadcasted_iota(jnp.int32, shape, 1)


PAIR = (0, HD, LANES)
Q_HEAD = (0, HD, HD + ROPE, LANES)
K_ROPE = (0, ROPE, LANES)


def _seg_mean(z, bounds):
    seg = lambda v: sum([(v >= b).astype(jnp.int32) for b in bounds[1:-1]], jnp.zeros_like(v))
    rows = seg(lax.broadcasted_iota(jnp.int32, (LANES, LANES), 0))
    cols = seg(lax.broadcasted_iota(jnp.int32, (LANES, LANES), 1))
    same = (rows == cols).astype(BF16)
    lane = _lane((1, LANES))
    inv = sum([jnp.where((lane >= a) & (lane < b), 1.0 / (b - a), 0.0) for a, b in zip(bounds[:-1], bounds[1:])])
    hi = z.astype(BF16)
    lo = (z - hi.astype(F32)).astype(BF16)
    return (_dot(hi, same) + _dot(lo, same)) * inv


def _seg_norm(x, bounds):
    r = lax.rsqrt(_seg_mean(x * x, bounds) + EPS)
    return r, x * r


def _seg_norm_bwd(r, xn, dyg, bounds):
    return r * (dyg - xn * _seg_mean(dyg * xn, bounds))


A_TM = 256


def mix_fwd(x, g, w, gq, gk, gcq, gckv):
    t, d = x.shape
    tm = A_TM

    def body(x_ref, g_ref, w_ref, gq_ref, gk_ref, gcq_ref, gckv_ref,
             h_ref, p_ref, qa_ref, ka_ref, va_ref, cq_ref, ckv_ref):
        _, xn = _rms(x_ref[...])
        h = (xn * g_ref[...]).astype(BF16)
        h_ref[...] = h
        p_ref[...] = _dot_nt(h, w_ref[...])
        for p in range(4):
            sl = slice(LANES * p, LANES * (p + 1))
            _, xn = _seg_norm(p_ref[:, sl], PAIR)
            qa_ref[:, sl] = (xn * gq_ref[:, sl] * 0.125).astype(BF16)
            _, xn = _seg_norm(p_ref[:, 512 + LANES * p:512 + LANES * (p + 1)], PAIR)
            ka_ref[:, sl] = (xn * gk_ref[:, sl]).astype(BF16)
        va_ref[...] = p_ref[:, 1024:1536].astype(BF16)
        _, xn = _rms(p_ref[:, 1536:1792])
        cq_ref[...] = (xn * gcq_ref[...]).astype(BF16)
        _, xn = _rms(p_ref[:, 1792:1920])
        ckv_ref[...] = (xn * gckv_ref[...]).astype(BF16)

    row = lambda w: pl.BlockSpec((tm, w), lambda i: (i, 0))
    vec = lambda w: pl.BlockSpec((1, w), lambda i: (0, 0))
    return pl.pallas_call(
        body, name="mix_fwd", grid=(t // tm,),
        in_specs=[row(d), vec(d), pl.BlockSpec((PROJ_W, d), lambda i: (0, 0)), vec(512), vec(512), vec(256), vec(128)],
        out_specs=[row(d), row(PROJ_W), row(512), row(512), row(512), row(256), row(128)],
        out_shape=[jax.ShapeDtypeStruct((t, d), BF16), jax.ShapeDtypeStruct((t, PROJ_W), F32)]
        + [jax.ShapeDtypeStruct((t, w_), BF16) for w_ in (512, 512, 512, 256, 128)],
        compiler_params=_params(("parallel",)),
    )(x, g, w, gq, gk, gcq, gckv)


def mix_bwd(proj, x, h, dres, w, g, dqa, dkp, dvp, dcq, dckv, dkr, gq, gk, gcq, gckv):
    t, d = x.shape
    tm = A_TM
    nb = t // tm

    def body(p_ref, x_ref, h_ref, dres_ref, w_ref, g_ref, dqa_ref, dk0_ref, dk1_ref, dk2_ref, dv0_ref, dv1_ref,
             dv2_ref, dcq_ref, dckv_ref, dkr_ref, gq_ref, gk_ref, gcq_ref, gckv_ref,
             dx_ref, dw_ref, dg_ref, dgq_ref, dgk_ref, dgcq_ref, dgckv_ref, dp_ref, acc_ref):
        i = pl.program_id(0)

        @pl.when(i == 0)
        def _():
            for ref in (acc_ref, dg_ref, dgq_ref, dgk_ref, dgcq_ref, dgckv_ref):
                ref[...] = jnp.zeros_like(ref)

        has1 = (i + 1 < nb).astype(F32)
        has2 = (i + 2 < nb).astype(F32)
        for p in range(4):
            sl = slice(LANES * p, LANES * (p + 1))
            r, xn = _seg_norm(p_ref[:, sl], PAIR)
            dy = dqa_ref[:, sl] * 0.125
            dp_ref[:, sl] = _seg_norm_bwd(r, xn, dy * gq_ref[:, sl], PAIR).astype(BF16)
            dgq_ref[:, sl] += jnp.sum(dy * xn, axis=0, keepdims=True)
            ks = slice(512 + LANES * p, 512 + LANES * (p + 1))
            r, xn = _seg_norm(p_ref[:, ks], PAIR)
            dy = dk0_ref[0, :, sl] + has1 * dk1_ref[0, :, sl] + has2 * dk2_ref[0, :, sl]
            dp_ref[:, ks] = _seg_norm_bwd(r, xn, dy * gk_ref[:, sl], PAIR).astype(BF16)
            dgk_ref[:, sl] += jnp.sum(dy * xn, axis=0, keepdims=True)
        dp_ref[:, 1024:1536] = (dv0_ref[0] + has1 * dv1_ref[0] + has2 * dv2_ref[0]).astype(BF16)
        for (a, b, dlat_ref, glat_ref, dglat_ref) in ((1536, 1792, dcq_ref, gcq_ref, dgcq_ref),
                                                      (1792, 1920, dckv_ref, gckv_ref, dgckv_ref)):
            r, xn = _rms(p_ref[:, a:b])
            dy = dlat_ref[...]
            dyg = dy * glat_ref[...]
            dp_ref[:, a:b] = (r * (dyg - xn * jnp.mean(dyg * xn, axis=-1, keepdims=True))).astype(BF16)
            dglat_ref[...] += jnp.sum(dy * xn, axis=0, keepdims=True)
        dp_ref[:, 1920:2048] = dkr_ref[...].astype(BF16)

        dproj = dp_ref[...]
        acc_ref[...] += _dot_tn(dproj, h_ref[...])
        dh = _dot(dproj, w_ref[...])
        r, xn = _rms(x_ref[...])
        dhg = dh * g_ref[...]
        dx_ref[...] = dres_ref[...] + r * (dhg - xn * jnp.mean(dhg * xn, axis=-1, keepdims=True))
        dg_ref[...] += jnp.sum(dh * xn, axis=0, keepdims=True)

        @pl.when(i == nb - 1)
        def _():
            dw_ref[...] = acc_ref[...].astype(BF16)

    row = lambda w_: pl.BlockSpec((tm, w_), lambda i: (i, 0))
    vec = lambda w_: pl.BlockSpec((1, w_), lambda i: (0, 0))
    part = lambda s: pl.BlockSpec((1, tm, 512), lambda i: (s, jnp.minimum(i + s, nb - 1), 0))
    whole = pl.BlockSpec((PROJ_W, d), lambda i: (0, 0))
    return pl.pallas_call(
        body, name="mix_bwd", grid=(nb,),
        in_specs=[row(PROJ_W), row(d), row(d), row(d), whole, vec(d), row(512), part(0), part(1), part(2),
                  part(0), part(1), part(2), row(256), row(128), row(128), vec(512), vec(512), vec(256), vec(128)],
        out_specs=[row(d), whole, vec(d), vec(512), vec(512), vec(256), vec(128)],
        out_shape=[jax.ShapeDtypeStruct((t, d), F32), jax.ShapeDtypeStruct((PROJ_W, d), BF16)]
        + [jax.ShapeDtypeStruct((1, w_), F32) for w_ in (d, 512, 512, 256, 128)],
        scratch_shapes=[pltpu.VMEM((tm, PROJ_W), BF16), pltpu.VMEM((PROJ_W, d), F32)],
        compiler_params=_params(("arbitrary",)),
    )(proj, x, h, dres, w, g, dqa, dkp, dkp, dkp, dvp, dvp, dvp, dcq, dckv, dkr, gq, gk, gcq, gckv)


def _roll(x, shift):
    return pltpu.roll(x, shift % LANES, 1)


def _rope(y, c, s1, s2):
    return y * c + _roll(y, -16) * s1 + _roll(y, 16) * s2


def _rope_bwd(d, c, s1, s2):
    return d * c + _roll(d * s1, 16) + _roll(d * s2, -16)


def _up_proj(cq_ref, ckv_ref, wuq_ref, wukv_ref):
    return _dot_nt(cq_ref[...], wuq_ref[...]), _dot(ckv_ref[...], wukv_ref[...])


def prep2_fwd(cqn, ckvn, proj, wuq, wukv, gq, gk, gkr, tabs):
    t = cqn.shape[0]
    tm = A_TM

    def body(cq_ref, ckv_ref, kr_ref, wuq_ref, wukv_ref, gq_ref, gk_ref, gkr_ref, tab_ref, qf_ref, kf_ref, vp_ref):
        q_all, kv_all = _up_proj(cq_ref, ckv_ref, wuq_ref, wukv_ref)
        _, xn = _seg_norm(kr_ref[...], K_ROPE)
        kpe = _roll(_rope(xn * gkr_ref[...], tab_ref[3], tab_ref[4], tab_ref[5]), 64)
        for h in range(HEADS):
            sl = slice(LANES * h, LANES * (h + 1))
            _, xn = _seg_norm(q_all[:, sl], Q_HEAD)
            qf_ref[:, sl] = (_rope(xn * gq_ref[...], tab_ref[0], tab_ref[1], tab_ref[2]) * B_SCALE2).astype(BF16)
            x = kv_all[:, sl]
            lo = _lane(x.shape) < HD
            _, xkn = _seg_norm(jnp.where(lo, x, 0.0), PAIR)
            kf_ref[:, sl] = (xkn * gk_ref[...] + kpe).astype(BF16)
            if h % 2 == 0:
                v_even = _roll(x, 64)
            else:
                vp_ref[:, LANES * (h // 2):LANES * (h // 2 + 1)] = jnp.where(lo, v_even, x).astype(BF16)

    row = lambda w: pl.BlockSpec((tm, w), lambda i: (i, 0))
    vec = lambda w: pl.BlockSpec((1, w), lambda i: (0, 0))
    full = lambda a: pl.BlockSpec(a.shape, lambda i: (0, 0))
    return pl.pallas_call(
        body, name="prep2_fwd", grid=(t // tm,),
        in_specs=[row(256), row(128), pl.BlockSpec((tm, LANES), lambda i: (i, 15)), full(wuq), full(wukv),
                  vec(128), vec(128), vec(128), pl.BlockSpec((6, tm, LANES), lambda i: (0, i, 0))],
        out_specs=[row(1024), row(1024), row(512)],
        out_shape=[jax.ShapeDtypeStruct((t, 1024), BF16), jax.ShapeDtypeStruct((t, 1024), BF16),
                   jax.ShapeDtypeStruct((t, 512), BF16)],
        compiler_params=_params(("parallel",)),
    )(cqn, ckvn, proj, wuq, wukv, gq, gk, gkr, tabs)


def prep2_bwd(cqn, ckvn, proj, wuq, wukv, dqf, dkf, dvp, gq, gk, gkr, tabs):
    t = cqn.shape[0]
    tm = A_TM

    def body(cq_ref, ckv_ref, kr_ref, wuq_ref, wukv_ref, dqf_ref, dkf_ref, dvp_ref, gq_ref, gk_ref, gkr_ref, tab_ref,
             dcq_ref, dckv_ref, dkr_ref, dwuq_ref, dwukv_ref, dgq_ref, dgk_ref, dgkr_ref, dq_ref, dkv_ref):
        i = pl.program_id(0)

        @pl.when(i == 0)
        def _():
            for ref in (dwuq_ref, dwukv_ref, dgq_ref, dgk_ref, dgkr_ref):
                ref[...] = jnp.zeros_like(ref)

        q_all, kv_all = _up_proj(cq_ref, ckv_ref, wuq_ref, wukv_ref)
        dgq = jnp.zeros((1, LANES), F32)
        dgk = jnp.zeros((1, LANES), F32)
        dkpe = jnp.zeros((tm, LANES), F32)
        for h in range(HEADS):
            sl = slice(LANES * h, LANES * (h + 1))
            lane = _lane((tm, LANES))
            mn, mr = lane < HD, (lane >= HD) & (lane < HD + ROPE)
            r, xn = _seg_norm(q_all[:, sl], Q_HEAD)
            dy = _rope_bwd(dqf_ref[sl, :].T, tab_ref[0], tab_ref[1], tab_ref[2])
            dyg = dy * gq_ref[...]
            dq_ref[:, sl] = _seg_norm_bwd(r, xn, dyg, Q_HEAD).astype(BF16)
            dgq = dgq + jnp.sum(dy * xn, axis=0, keepdims=True)

            x = kv_all[:, sl]
            dk = dkf_ref[:, sl]
            rk, xkn = _seg_norm(jnp.where(mn, x, 0.0), PAIR)
            dyk = jnp.where(mn, dk, 0.0)
            dxk = _seg_norm_bwd(rk, xkn, dyk * gk_ref[...], PAIR)
            dgk = dgk + jnp.sum(dyk * xkn, axis=0, keepdims=True)
            dkpe = dkpe + jnp.where(mr, dk, 0.0)
            dvpair = dvp_ref[:, LANES * (h // 2):LANES * (h // 2 + 1)]
            dv = _roll(dvpair, 64) if h % 2 == 0 else dvpair
            dkv_ref[:, sl] = jnp.where(mn, dxk, dv).astype(BF16)

        r, xn = _seg_norm(kr_ref[...], K_ROPE)
        dy = _rope_bwd(_roll(dkpe, 64), tab_ref[3], tab_ref[4], tab_ref[5])
        dkr_ref[...] = _seg_norm_bwd(r, xn, dy * gkr_ref[...], K_ROPE)
        dgq_ref[...] += dgq
        dgk_ref[...] += dgk
        dgkr_ref[...] += jnp.sum(dy * xn, axis=0, keepdims=True)
        dqb, dkvb = dq_ref[...], dkv_ref[...]
        dcq_ref[...] = _dot(dqb, wuq_ref[...])
        dckv_ref[...] = _dot_nt(dkvb, wukv_ref[...])
        dwuq_ref[...] += _dot_tn(dqb, cq_ref[...])
        dwukv_ref[...] += _dot_tn(ckv_ref[...], dkvb)

    row = lambda w: pl.BlockSpec((tm, w), lambda i: (i, 0))
    vec = lambda w: pl.BlockSpec((1, w), lambda i: (0, 0))
    full = lambda a: pl.BlockSpec(a.shape, lambda i: (0, 0))
    return pl.pallas_call(
        body, name="prep2_bwd", grid=(t // tm,),
        in_specs=[row(256), row(128), pl.BlockSpec((tm, LANES), lambda i: (i, 15)), full(wuq), full(wukv),
                  pl.BlockSpec((1024, tm), lambda i: (0, i)), row(1024), row(512),
                  vec(128), vec(128), vec(128), pl.BlockSpec((6, tm, LANES), lambda i: (0, i, 0))],
        out_specs=[row(256), row(128), row(128), full(wuq), full(wukv), vec(128), vec(128), vec(128)],
        out_shape=[jax.ShapeDtypeStruct((t, 256), F32), jax.ShapeDtypeStruct((t, 128), F32),
                   jax.ShapeDtypeStruct((t, LANES), F32), jax.ShapeDtypeStruct(wuq.shape, F32),
                   jax.ShapeDtypeStruct(wukv.shape, F32)] + [jax.ShapeDtypeStruct((1, LANES), F32)] * 3,
        scratch_shapes=[pltpu.VMEM((tm, 1024), BF16), pltpu.VMEM((tm, 1024), BF16)],
        compiler_params=_params(("arbitrary",)),
    )(cqn, ckvn, proj, wuq, wukv, dqf, dkf, dvp, gq, gk, gkr, tabs)


A_TQ = 256
A_WIN = 3 * A_TQ


def _a_specs(t):
    nb = t // A_TQ
    blk = lambda s: pl.BlockSpec((A_TQ, 512), lambda i: (jnp.maximum(i - s, 0), 0))
    return nb, blk


def _a_exp(q_ref, kc, b_ref, head, sl, lo):
    hm = lo if head % 2 == 0 else ~lo
    qm = jnp.where(hm, q_ref[:, sl], jnp.zeros((), BF16))
    s = _dot_nt(qm, kc) + b_ref[0, head]
    e = jnp.exp(s - jnp.max(s, axis=-1, keepdims=True))
    return hm, qm, e, 1.0 / jnp.sum(e, axis=-1, keepdims=True)


def _a_bias_spec():
    return pl.BlockSpec((1, HEADS, A_TQ, A_WIN), lambda i: (jnp.minimum(i, 2), 0, 0, 0))


def attn_a_fwd(qa, ka, va, bias):
    t = qa.shape[0]
    nb, blk = _a_specs(t)

    def body(q_ref, k2_ref, k1_ref, k0_ref, v2_ref, v1_ref, v0_ref, b_ref, o_ref):
        lo = _lane((A_TQ, LANES)) < HD
        for p in range(4):
            sl = slice(LANES * p, LANES * (p + 1))
            kc = jnp.concatenate([k2_ref[:, sl], k1_ref[:, sl], k0_ref[:, sl]], axis=0)
            vc = jnp.concatenate([v2_ref[:, sl], v1_ref[:, sl], v0_ref[:, sl]], axis=0)
            outs = []
            for h2 in range(2):
                _, _, e, inv = _a_exp(q_ref, kc, b_ref, 2 * p + h2, sl, lo)
                outs.append(_dot(e.astype(BF16), vc) * inv)
            o_ref[:, sl] = jnp.where(lo, outs[0], outs[1]).astype(BF16)

    return pl.pallas_call(
        body, name="attn_a_fwd", grid=(nb,),
        in_specs=[blk(0), blk(2), blk(1), blk(0), blk(2), blk(1), blk(0), _a_bias_spec()],
        out_specs=pl.BlockSpec((A_TQ, 512), lambda i: (i, 0)),
        out_shape=jax.ShapeDtypeStruct((t, 512), BF16),
        compiler_params=_params(("parallel",)),
    )(qa, ka, ka, ka, va, va, va, bias)


def attn_a_bwd(qa, ka, va, bias, do):
    t = qa.shape[0]
    nb, blk = _a_specs(t)

    def body(q_ref, k2_ref, k1_ref, k0_ref, v2_ref, v1_ref, v0_ref, b_ref, do_ref, dq_ref, dk_ref, dv_ref, db_ref):
        qb = pl.program_id(0)

        @pl.when(qb == 0)
        def _():
            db_ref[...] = jnp.zeros_like(db_ref)

        lo = _lane((A_TQ, LANES)) < HD
        for p in range(4):
            sl = slice(LANES * p, LANES * (p + 1))
            kc = jnp.concatenate([k2_ref[:, sl], k1_ref[:, sl], k0_ref[:, sl]], axis=0)
            vc = jnp.concatenate([v2_ref[:, sl], v1_ref[:, sl], v0_ref[:, sl]], axis=0)
            dqs = []
            dkt = jnp.zeros((LANES, A_WIN), F32)
            dvt = jnp.zeros((LANES, A_WIN), F32)
            for h2 in range(2):
                head = 2 * p + h2
                hm, qm, e, inv = _a_exp(q_ref, kc, b_ref, head, sl, lo)
                pr = e * inv
                dom = jnp.where(hm, do_ref[:, sl], jnp.zeros((), BF16))
                dp = _dot_nt(dom, vc)
                ds = pr * (dp - jnp.sum(pr * dp, axis=-1, keepdims=True))
                db_ref[head] += ds
                dsb = ds.astype(BF16)
                dqs.append(_dot(dsb, kc))
                dkt = dkt + _dot_tn(qm, dsb)
                dvt = dvt + _dot_tn(dom, pr.astype(BF16))
            dq_ref[:, sl] = jnp.where(lo, dqs[0], dqs[1])
            dkc, dvc = dkt.T, dvt.T
            for s in range(3):
                rows = slice(A_TQ * (2 - s), A_TQ * (3 - s))
                dk_ref[s, :, sl] = dkc[rows]
                dv_ref[s, :, sl] = dvc[rows]

    share = pl.BlockSpec((3, A_TQ, 512), lambda i: (0, i, 0))
    return pl.pallas_call(
        body, name="attn_a_bwd", grid=(nb,),
        in_specs=[blk(0), blk(2), blk(1), blk(0), blk(2), blk(1), blk(0), _a_bias_spec(), blk(0)],
        out_specs=[pl.BlockSpec((A_TQ, 512), lambda i: (i, 0)), share, share,
                   pl.BlockSpec((HEADS, A_TQ, A_WIN), lambda i: (0, 0, 0))],
        out_shape=[jax.ShapeDtypeStruct((t, 512), F32), jax.ShapeDtypeStruct((3, t, 512), F32),
                   jax.ShapeDtypeStruct((3, t, 512), F32), jax.ShapeDtypeStruct((HEADS, A_TQ, A_WIN), F32)],
        compiler_params=_params(("arbitrary",)),
    )(qa, ka, ka, ka, va, va, va, bias, do)


B_T = 1024


B_SCALE2 = B_SCALE * 1.4426950408889634
B_FWD_HEADS = 4
_B_ALL = slice(0, B_T)
_B_LO, _B_HI = slice(0, B_T // 2), slice(B_T // 2, B_T)
_B_DIAG = ((_B_LO, _B_LO), (_B_LO, _B_HI), (_B_HI, _B_HI))


def _tri_tables(n, by_query):
    pairs = [(i, j) for i in range(n) for j in range(i + 1)] if by_query else [(i, j) for j in range(n) for i in range(j, n)]
    return (np.asarray([p[0] for p in pairs], np.int32), np.asarray([p[1] for p in pairs], np.int32))


def _b_mask_t(s):
    kc = lax.broadcasted_iota(jnp.int32, s.shape, 0) // CHUNK
    qc = lax.broadcasted_iota(jnp.int32, s.shape, 1) // CHUNK
    return jnp.where(kc <= qc, s, NEG)


def attn_b_fwd(qf, kf, vp):
    t = qf.shape[0]
    n = t // B_T
    qtab, ktab = _tri_tables(n, by_query=True)
    hps = B_FWD_HEADS

    def body(qt_ref, kt_ref, q_ref, k_ref, v_ref, o_ref, lse_ref, m_s, l_s, acc_s):
        qb, kb = qt_ref[pl.program_id(1)], kt_ref[pl.program_id(1)]

        @pl.when(kb == 0)
        def _():
            m_s[...] = jnp.full_like(m_s, NEG)
            l_s[...] = jnp.zeros_like(l_s)
            acc_s[...] = jnp.zeros_like(acc_s)

        def block(kr, qr, masked):
            for h2 in range(hps):
                sl = slice(LANES * h2, LANES * (h2 + 1))
                v = v_ref[kr, LANES * (h2 // 2):LANES * (h2 // 2 + 1)]
                s = _dot_nt(k_ref[kr, sl], q_ref[qr, sl])
                if masked:
                    s = _b_mask_t(s)
                m_prev = m_s[h2, :, qr]
                m_new = jnp.maximum(m_prev, jnp.max(s, axis=0, keepdims=True))
                alpha = jnp.exp2(m_prev - m_new)
                pr = jnp.exp2(s - m_new)
                l_s[h2, :, qr] = alpha * l_s[h2, :, qr] + jnp.sum(pr, axis=0, keepdims=True)
                acc_s[h2, :, qr] = alpha * acc_s[h2, :, qr] + _dot_tn(v, pr.astype(BF16))
                m_s[h2, :, qr] = m_new

        @pl.when(kb < qb)
        def _():
            block(_B_ALL, _B_ALL, False)

        @pl.when(kb == qb)
        def _():
            for kr, qr in _B_DIAG:
                block(kr, qr, kr == qr)
            for h2 in range(hps):
                l = l_s[h2]
                rows = slice(HD * (h2 % 2), HD * (h2 % 2 + 1))
                o_ref[HD * h2:HD * (h2 + 1), :] = (acc_s[h2, rows, :] * (1.0 / l)).astype(BF16)
                lse_ref[h2 // 2, h2 % 2:h2 % 2 + 1, :] = m_s[h2] + jnp.log2(l)

    grid_spec = pltpu.PrefetchScalarGridSpec(
        num_scalar_prefetch=2, grid=(HEADS // hps, len(qtab)),
        in_specs=[pl.BlockSpec((B_T, LANES * hps), lambda p, s, qt, kt: (qt[s], p)),
                  pl.BlockSpec((B_T, LANES * hps), lambda p, s, qt, kt: (kt[s], p)),
                  pl.BlockSpec((B_T, HD * hps), lambda p, s, qt, kt: (kt[s], p))],
        out_specs=[pl.BlockSpec((HD * hps, B_T), lambda p, s, qt, kt: (p, qt[s])),
                   pl.BlockSpec((hps // 2, 2, B_T), lambda p, s, qt, kt: (p, 0, qt[s]))],
        scratch_shapes=[pltpu.VMEM((hps, 1, B_T), F32), pltpu.VMEM((hps, 1, B_T), F32),
                        pltpu.VMEM((hps, LANES, B_T), F32)])
    return pl.pallas_call(
        body, name="attn_b_fwd", grid_spec=grid_spec,
        out_shape=[jax.ShapeDtypeStruct((512, t), BF16), jax.ShapeDtypeStruct((4, 2, t), F32)],
        compiler_params=_params(("parallel", "arbitrary")),
    )(jnp.asarray(qtab), jnp.asarray(ktab), qf, kf, vp)


def attn_b_bwd(qf, kf, vp, do, do_t, o_t, lse, scatter=()):
    t = qf.shape[0]
    n = t // B_T
    qtab, ktab = _tri_tables(n, by_query=False)
    plan = ScatterPlan(scatter)
    m = plan.n
    last = len(qtab) - 1

    def body(*refs):
        qt_ref, kt_ref, q_ref, k_ref, v_ref, do_ref, dot_ref, ot_ref, lse_ref = refs[:9]
        ins, (dq_ref, dk_ref, dv_ref), outs = refs[9:9 + m], refs[9 + m:12 + m], refs[12 + m:12 + 2 * m]
        sems = refs[12 + 2 * m:]
        qb, kb = qt_ref[pl.program_id(1)], kt_ref[pl.program_id(1)]
        if m:
            pl.when((pl.program_id(0) == 0) & (pl.program_id(1) == 0))(lambda: plan.start(ins, outs, sems))

        @pl.when(pl.program_id(1) == 0)
        def _():
            dq_ref[...] = jnp.zeros_like(dq_ref)

        @pl.when(qb == kb)
        def _():
            dk_ref[...] = jnp.zeros_like(dk_ref)
            dv_ref[...] = jnp.zeros_like(dv_ref)

        def block(kr, qr, masked):
            nq = qr.stop - qr.start
            cols = pl.ds(pl.multiple_of(qb * B_T + qr.start, LANES), nq)
            v = v_ref[kr, :]
            dov = do_ref[qr, :]
            prod = dot_ref[:, qr].astype(F32) * ot_ref[:, qr].astype(F32)
            lo = _lane((nq, LANES)) < HD
            for h2 in range(2):
                sl = slice(LANES * h2, LANES * (h2 + 1))
                hm = lo if h2 == 0 else ~lo
                q = q_ref[qr, sl]
                k = k_ref[kr, sl]
                dom = jnp.where(hm, dov, jnp.zeros((), BF16))
                delta = jnp.sum(prod[HD * h2:HD * (h2 + 1), :], axis=0, keepdims=True)
                s = _dot_nt(k, q)
                if masked:
                    s = _b_mask_t(s)
                pr = jnp.exp2(s - lse_ref[0, h2:h2 + 1, qr])
                dp = _dot_nt(v, dom)
                ds = (pr * (dp - delta)).astype(BF16)
                dk_ref[kr, sl] += _dot(ds, q) * (B_SCALE / B_SCALE2)
                dv_ref[kr, :] += _dot(pr.astype(BF16), dom)
                dq_ref[sl, cols] += _dot_tn(k, ds) * B_SCALE

        @pl.when(qb > kb)
        def _():
            block(_B_ALL, _B_ALL, False)

        @pl.when(qb == kb)
        def _():
            for kr, qr in _B_DIAG:
                block(kr, qr, kr == qr)

        if m:
            pl.when((pl.program_id(0) == 3) & (pl.program_id(1) == last))(lambda: plan.finish(ins, outs, sems))

    qrow = lambda w: pl.BlockSpec((B_T, w), lambda p, s, qt, kt: (qt[s], p))
    qcol = pl.BlockSpec((LANES, B_T), lambda p, s, qt, kt: (p, qt[s]))
    krow = lambda w: pl.BlockSpec((B_T, w), lambda p, s, qt, kt: (kt[s], p))
    grid_spec = pltpu.PrefetchScalarGridSpec(
        num_scalar_prefetch=2, grid=(4, len(qtab)),
        in_specs=[qrow(256), krow(256), krow(LANES), qrow(LANES), qcol, qcol,
                  pl.BlockSpec((1, 2, B_T), lambda p, s, qt, kt: (p, 0, qt[s]))] + [ANY] * m,
        out_specs=[pl.BlockSpec((256, t), lambda p, s, qt, kt: (p, 0)), krow(256), krow(LANES)] + [ANY] * m,
        scratch_shapes=plan.scratch if m else [])
    return pl.pallas_call(
        body, name="attn_b_bwd", grid_spec=grid_spec,
        out_shape=[jax.ShapeDtypeStruct((1024, t), F32), jax.ShapeDtypeStruct((t, 1024), F32),
                   jax.ShapeDtypeStruct((t, 512), F32)] + plan.out_shape,
        compiler_params=_params(("arbitrary", "arbitrary")),
    )(jnp.asarray(qtab), jnp.asarray(ktab), qf, kf, vp, do, do_t, o_t, lse, *scatter)


_U_LEN = A_TQ + A_WIN - 1


def _band_mask():
    a = np.arange(A_TQ)[:, None] // CHUNK
    b = np.arange(A_WIN)[None, :] // CHUNK
    return (b >= a) & (b <= a + A_LEFT)


def bias_block(table):
    h = table.shape[0]
    n_lo = A_WIN - 1 - 2 * A_TQ - A_MAX_REL
    ext = jnp.concatenate([jnp.repeat(table[:, :1], n_lo, axis=1), table,
                           jnp.repeat(table[:, -1:], _U_LEN - n_lo - table.shape[1], axis=1)], axis=1)
    row = jnp.pad(ext[:, ::-1], ((0, 0), (0, 1)))[:, None, :]
    band = _band_mask()
    first = [band & (np.arange(A_WIN)[None, :] >= 2 * A_TQ - A_TQ * v) for v in range(3)]
    keep = jnp.asarray(np.stack(first), jnp.int32)

    def body(r_ref, k_ref, o_ref):
        rows = jnp.broadcast_to(r_ref[0], (A_TQ, _U_LEN + 1))
        skew = pltpu.roll(rows, _U_LEN + 1 - (A_TQ - 1), 1, stride=1, stride_axis=0)
        toep = skew[:, :A_WIN]
        for v in range(3):
            o_ref[v, 0] = jnp.where(k_ref[v] != 0, toep, NEG)

    return pl.pallas_call(
        body, name="bias_block", grid=(h,),
        in_specs=[pl.BlockSpec((1, 1, _U_LEN + 1), lambda i: (i, 0, 0)),
                  pl.BlockSpec((3, A_TQ, A_WIN), lambda i: (0, 0, 0))],
        out_specs=pl.BlockSpec((3, 1, A_TQ, A_WIN), lambda i: (0, i, 0, 0)),
        out_shape=jax.ShapeDtypeStruct((3, h, A_TQ, A_WIN), F32),
        compiler_params=_params(("parallel",)),
    )(row, keep)


def bias_block_grad(db):
    h = db.shape[0]
    n_lo = A_WIN - 1 - 2 * A_TQ - A_MAX_REL
    skew = jnp.pad(db, ((0, 0), (0, 0), (A_TQ - 1, 0)))
    flat = jnp.pad(skew.reshape(h, A_TQ * _U_LEN), ((0, 0), (0, A_TQ)))
    ext = jnp.sum(flat.reshape(h, A_TQ, _U_LEN + 1), axis=1)[:, :_U_LEN][:, ::-1]
    n_tab = 2 * A_MAX_REL + 1
    first = jnp.sum(ext[:, :n_lo + 1], axis=1, keepdims=True)
    last = jnp.sum(ext[:, n_lo + n_tab - 1:], axis=1, keepdims=True)
    return jnp.concatenate([first, ext[:, n_lo + 1:n_lo + n_tab - 1], last], axis=1)


def rope_tabs(t):
    inv = 1.0 / (10000.0 ** (jnp.arange(0, ROPE, 2, dtype=F32) / ROPE))
    ang = jnp.arange(t, dtype=F32)[:, None] * inv[None, :]
    cos, sin = jnp.cos(ang), jnp.sin(ang)
    z = lambda w: jnp.zeros((t, w), F32)
    ck = jnp.concatenate([cos, cos, z(96)], axis=1)
    s1k = jnp.concatenate([-sin, z(112)], axis=1)
    s2k = jnp.concatenate([z(16), sin, z(96)], axis=1)
    cq = jnp.concatenate([jnp.ones((t, HD), F32), cos, cos, z(32)], axis=1)
    s1q = jnp.concatenate([z(HD), -sin, z(48)], axis=1)
    s2q = jnp.concatenate([z(HD + 16), sin, z(32)], axis=1)
    return jnp.stack([cq, s1q, s2q, ck, s1k, s2k])


def _pad_lanes(v, width):
    return jnp.pad(v, ((0, 0), (0, width - v.shape[1])))


LATE = ("w_in", "b_w_uq", "b_w_ukv", "w_out", "ffn2_w_gate", "ffn2_w_up", "ffn2_w_down")
FFN2 = ("ffn2_w_gate", "ffn2_w_up", "ffn2_w_down")


def kernel_layout(gathered):
    w = {n: v for n, v in gathered.items() if n.startswith("ffn")}
    if "w_in" in gathered:
        w["w_in"] = jnp.pad(gathered["w_in"].reshape(IN_COLS, D_MODEL), ((0, PROJ_W - IN_COLS), (0, 0)))
        uq = gathered["b_w_uq"].reshape(HEADS, HD + ROPE, 256)
        w["b_w_uq"] = jnp.pad(uq, ((0, 0), (0, LANES - HD - ROPE), (0, 0))).reshape(HEADS * LANES, 256)
        w["b_w_ukv"] = _shards_to_cols(gathered["b_w_ukv"])
        w["w_out"] = gathered["w_out"].reshape(N_SHARD * gathered["w_out"].shape[1], D_MODEL)
    return w


def local_step(x, target, w, late=None):
    t = x.shape[0]
    gq = jnp.tile(w["a_q_norm"], (1, HEADS))
    gk = jnp.tile(w["a_k_norm"], (1, HEADS))
    gq128 = _pad_lanes(jnp.concatenate([w["b_q_nope_norm"], w["b_q_rope_norm"]], axis=1), LANES)
    gk128 = _pad_lanes(w["b_k_nope_norm"], LANES)
    gkr128 = _pad_lanes(w["b_k_rope_norm"], LANES)
    tabs = rope_tabs(t)
    bias = bias_block(w["a_rel_bias"])

    if late is None:
        x1, gate1, up1 = ffn_fwd(x, w["ffn1_norm"], w["ffn1_w_gate"], w["ffn1_w_up"], w["ffn1_w_down"], "ffn_fwd")
    else:
        own, shard = late
        x1, gate1, up1, *got = ffn_fwd(x, w["ffn1_norm"], w["ffn1_w_gate"], w["ffn1_w_up"], w["ffn1_w_down"],
                                       "ffn_fwd_gather", gather=own)
        w = dict(w, **kernel_layout({n: lax.dynamic_update_index_in_dim(g_, o_, shard, 0)
                                     for n, g_, o_ in zip(LATE, got, own)}))
    h, proj, qa, ka, va, cqn, ckvn = mix_fwd(x1, w["mix_norm"], w["w_in"], gq, gk,
                                             w["b_q_lat_norm"], w["b_kv_lat_norm"])
    qf, kf, vp = prep2_fwd(cqn, ckvn, proj, w["b_w_uq"], w["b_w_ukv"], gq128, gk128, gkr128, tabs)
    oa = attn_a_fwd(qa, ka, va, bias)
    ob_t, lse = attn_b_fwd(qf, kf, vp)
    g = {}
    dx3, gate2, up2, g["final_norm"], loss, x2 = ffn_fwd(
        x1, w["ffn2_norm"], w["ffn2_w_gate"], w["ffn2_w_up"], w["ffn2_w_down"], "ffn_fwd_loss",
        loss_head=(w["final_norm"], target), pre_proj=(oa, ob_t, w["w_out"]))
    g["ffn2_w_gate"], g["ffn2_w_up"], g["ffn2_w_down"], dhp = ffn_bwd(
        x2, dx3, w["ffn2_norm"], w["ffn2_w_gate"], w["ffn2_w_up"], w["ffn2_w_down"], gate2, up2, "ffn_bwd")
    dx2, g["ffn2_norm"], d_oa, d_ob, d_ob_t, g["w_out"] = out_proj_bwd(
        x2, w["ffn2_norm"], dhp, dx3, w["w_out"], oa, ob_t)
    early = [g[n] for n in FFN2] + [g["w_out"].reshape(N_SHARD, -1, D_MODEL)]
    dqf, dkf, dvp, *landed_early = attn_b_bwd(qf, kf, vp, d_ob, d_ob_t, ob_t, lse,
                                              scatter=() if late is None else early)
    dqa, dkp, dvpa, dbias = attn_a_bwd(qa, ka, va, bias, d_oa)
    dcq, dckv, dkr, dwuq, dwukv, dgq128, dgk128, dgkr128 = prep2_bwd(
        cqn, ckvn, proj, w["b_w_uq"], w["b_w_ukv"], dqf, dkf, dvp, gq128, gk128, gkr128, tabs)
    g["b_w_uq"], g["b_w_ukv"] = dwuq.astype(BF16), dwukv.astype(BF16)
    dx1, g["w_in"], g["mix_norm"], dgq, dgk, g["b_q_lat_norm"], g["b_kv_lat_norm"] = mix_bwd(
        proj, x1, h, dx2, w["w_in"], w["mix_norm"], dqa, dkp, dvpa, dcq, dckv, dkr, gq, gk,
        w["b_q_lat_norm"], w["b_kv_lat_norm"])
    mid = [g["w_in"][:IN_COLS].reshape(N_SHARD, IN_COLS // N_SHARD, D_MODEL),
           g["b_w_uq"].reshape(HEADS, LANES, 256)[:, :HD + ROPE].reshape(N_SHARD, -1, 256),
           _cols_to_shards(g["b_w_ukv"])]
    g["ffn1_w_gate"], g["ffn1_w_up"], g["ffn1_w_down"], dhp, *landed_late = ffn_bwd(
        x, dx1, w["ffn1_norm"], w["ffn1_w_gate"], w["ffn1_w_up"], w["ffn1_w_down"], gate1, up1,
        "ffn_bwd" if late is None else "ffn_bwd_scatter", scatter=() if late is None else mid,
        spread=None if late is None else late[1])
    grad_x, g["ffn1_norm"] = norm_bwd(x, w["ffn1_norm"], dhp, dx1, "ffn_norm_bwd")
    landed = dict(zip(FFN2 + ("w_out", "w_in", "b_w_uq", "b_w_ukv", "ffn1_w_gate", "ffn1_w_up", "ffn1_w_down"),
                      landed_early + landed_late))

    g["a_q_norm"] = jnp.sum(dgq.reshape(HEADS, HD), axis=0, keepdims=True)
    g["a_k_norm"] = jnp.sum(dgk.reshape(HEADS, HD), axis=0, keepdims=True)
    g["a_rel_bias"] = bias_block_grad(dbias)
    g["b_q_nope_norm"] = dgq128[:, :HD]
    g["b_q_rope_norm"] = dgq128[:, HD:HD + ROPE]
    g["b_k_nope_norm"] = dgk128[:, :HD]
    g["b_k_rope_norm"] = dgkr128[:, :ROPE]
    return loss, grad_x, g, landed


ANY = pl.BlockSpec(memory_space=pl.ANY)
N_DEV = 8


def _place():
    return lax.axis_index("x"), lax.axis_index("y"), lax.axis_index("c")


def _flip(v, bit):
    return 1 - v if bit else v


BF16_ROWS = 16


def _split_axis(shape):
    return 0 if (shape[0] // 2) % BF16_ROWS == 0 else 1


def _half_shape(shape):
    axis = _split_axis(shape)
    return tuple(s // 2 if a == axis else s for a, s in enumerate(shape))


def _half(shape, core):
    axis = _split_axis(shape)
    size = shape[axis] // 2
    return tuple(pl.ds(core * size, size) if a == axis else slice(None) for a in range(2))


class GatherPlan:
    def __init__(self, ws):
        self.shapes = [w.shape for w in ws]
        self.n = len(ws)
        self.out_shape = [jax.ShapeDtypeStruct((N_SHARD,) + w.shape, w.dtype) for w in ws]
        self.scratch = [pltpu.SemaphoreType.DMA((6 * self.n,)), pltpu.SemaphoreType.DMA((6 * self.n,))]

    def _copies(self, ins, outs, sems):
        x, y, c = _place()
        s_me = 2 * x + y
        sibling = (x, y, 1 - c)
        send_sems, recv_sems = sems

        def remote(k, src, dst, to):
            return pltpu.make_async_remote_copy(src_ref=src, dst_ref=dst, send_sem=send_sems.at[k],
                                                recv_sem=recv_sems.at[k], device_id=to, device_id_type=MESH)

        ici, fwd = [], []
        for a in range(self.n):
            mine, theirs = _half(self.shapes[a], c), _half(self.shapes[a], 1 - c)
            for j, (cx, cy) in enumerate([(1 - x, y), (x, 1 - y), (1 - x, 1 - y)]):
                got = outs[a].at[(2 * cx + cy,) + mine]
                ici.append((remote(6 * a + j, ins[a].at[mine], outs[a].at[(s_me,) + mine], (cx, cy, c)),
                            remote(6 * a + j, got, got, (cx, cy, c))))
                passed = outs[a].at[(2 * cx + cy,) + theirs]
                fwd.append((remote(6 * a + 3 + j, got, got, sibling), remote(6 * a + 3 + j, passed, passed, sibling)))
        return ici, fwd

    def start(self, ins, outs, sems):
        for send, _ in self._copies(ins, outs, sems)[0]:
            send.start()

    def forward(self, ins, outs, sems):
        ici, fwd = self._copies(ins, outs, sems)
        for (_, arrival), (send, _) in zip(ici, fwd):
            arrival.wait_recv()
            send.start()

    def finish(self, ins, outs, sems):
        ici, fwd = self._copies(ins, outs, sems)
        for _, arrival in fwd:
            arrival.wait_recv()
        for send, _ in ici + fwd:
            send.wait_send()


def allgather_shards(ws):
    plan = GatherPlan(ws)
    n = plan.n

    def body(*refs):
        ins, outs, sems = refs[:n], refs[n:2 * n], refs[2 * n:]
        plan.start(ins, outs, sems)
        plan.forward(ins, outs, sems)
        plan.finish(ins, outs, sems)

    return pl.pallas_call(
        body, name="allgather_shards", in_specs=[ANY] * n, out_specs=[ANY] * n,
        out_shape=plan.out_shape, scratch_shapes=plan.scratch,
    )(*ws)


class ScatterPlan:
    def __init__(self, gs):
        self.shapes = [g.shape[1:] for g in gs]
        self.n = len(gs)
        self.out_shape = [jax.ShapeDtypeStruct((N_DEV,) + _half_shape(g.shape[1:]), g.dtype) for g in gs]
        self.scratch = [pltpu.SemaphoreType.DMA((7 * self.n,)), pltpu.SemaphoreType.DMA((7 * self.n,)),
                        pltpu.SemaphoreType.DMA((self.n,))]

    def _copies(self, ins, outs, sems):
        x, y, c = _place()
        me = 4 * x + 2 * y + c
        send_sems, recv_sems, local_sems = sems
        local, sends, arrivals = [], [], []
        for a in range(self.n):
            piece = lambda px, py, pc, a=a: ins[a].at[(2 * px + py,) + _half(self.shapes[a], pc)]
            local.append(pltpu.make_async_copy(piece(x, y, c), outs[a].at[me], local_sems.at[a]))
            for k in range(1, N_DEV):
                px, py, pc = _flip(x, k & 4), _flip(y, k & 2), _flip(c, k & 1)
                sem = dict(send_sem=send_sems.at[7 * a + k - 1], recv_sem=recv_sems.at[7 * a + k - 1],
                           device_id=(px, py, pc), device_id_type=MESH)
                sends.append(pltpu.make_async_remote_copy(
                    src_ref=piece(px, py, pc), dst_ref=outs[a].at[me], **sem))
                slot = outs[a].at[4 * px + 2 * py + pc]
                arrivals.append(pltpu.make_async_remote_copy(src_ref=slot, dst_ref=slot, **sem))
        return local, sends, arrivals

    def start(self, ins, outs, sems):
        local, sends, _ = self._copies(ins, outs, sems)
        for cp in local + sends:
            cp.start()

    def finish(self, ins, outs, sems):
        local, sends, arrivals = self._copies(ins, outs, sems)
        for cp in arrivals:
            cp.wait_recv()
        for cp in sends:
            cp.wait_send()
        for cp in local:
            cp.wait()


def _row_tile(rows, row_bytes, budget, multiple):
    fits = [r for r in range(multiple, rows + 1, multiple) if rows % r == 0 and r * row_bytes <= budget]
    return max(fits) if fits else rows


SMALL_KERNEL_VMEM = 32 * 1024 * 1024


def sum_slots(lands, name):
    k = len(lands)
    _, rows, cols = lands[0].shape
    tr = _row_tile(rows, k * 2 * (N_DEV * cols * 2 + cols * 4), SMALL_KERNEL_VMEM, BF16_ROWS)

    def body(*refs):
        for l_ref, o_ref in zip(refs[:k], refs[k:]):
            acc = l_ref[0].astype(F32)
            for s in range(1, N_DEV):
                acc = acc + l_ref[s].astype(F32)
            o_ref[...] = acc

    return pl.pallas_call(
        body, name=name, grid=(rows // tr,),
        in_specs=[pl.BlockSpec((N_DEV, tr, cols), lambda i: (0, i, 0))] * k,
        out_specs=[pl.BlockSpec((tr, cols), lambda i: (i, 0))] * k,
        out_shape=[jax.ShapeDtypeStruct((rows, cols), F32)] * k,
        compiler_params=_params(("parallel",)),
    )(*lands)


def join_halves(hs, shapes):
    n = len(hs)

    def body(*refs):
        ins, outs = refs[:n], refs[n:2 * n]
        send_sems, recv_sems = refs[2 * n:]
        x, y, c = _place()
        sends = []
        for a in range(n):
            mine = outs[a].at[_half(shapes[a], c)]
            sends.append(pltpu.make_async_remote_copy(
                src_ref=ins[a], dst_ref=mine, send_sem=send_sems.at[a], recv_sem=recv_sems.at[a],
                device_id=(x, y, 1 - c), device_id_type=MESH))
            sends[-1].start()
        for a in range(n):
            theirs = outs[a].at[_half(shapes[a], 1 - c)]
            pltpu.make_async_remote_copy(
                src_ref=theirs, dst_ref=theirs, send_sem=send_sems.at[a], recv_sem=recv_sems.at[a],
                device_id=(x, y, 1 - c), device_id_type=MESH).wait_recv()
        for cp in sends:
            cp.wait_send()

    return pl.pallas_call(
        body, name="join_halves",
        in_specs=[ANY] * n, out_specs=[ANY] * n,
        out_shape=[jax.ShapeDtypeStruct(tuple(s), h.dtype) for s, h in zip(shapes, hs)],
        scratch_shapes=[pltpu.SemaphoreType.DMA((n,)), pltpu.SemaphoreType.DMA((n,))],
    )(*hs)


def allreduce_small(vec):
    def body(v_ref, o_ref, land_ref, send_sems, recv_sems):
        x, y, c = _place()
        me = 4 * x + 2 * y + c
        land_ref[me] = v_ref[...]
        sends = []
        for k in range(1, N_DEV):
            px, py, pc = _flip(x, k & 4), _flip(y, k & 2), _flip(c, k & 1)
            sends.append(pltpu.make_async_remote_copy(
                src_ref=v_ref, dst_ref=land_ref.at[me], send_sem=send_sems.at[k - 1], recv_sem=recv_sems.at[k - 1],
                device_id=(px, py, pc), device_id_type=MESH))
            sends[-1].start()
        for k in range(1, N_DEV):
            px, py, pc = _flip(x, k & 4), _flip(y, k & 2), _flip(c, k & 1)
            slot = land_ref.at[4 * px + 2 * py + pc]
            pltpu.make_async_remote_copy(
                src_ref=slot, dst_ref=slot, send_sem=send_sems.at[k - 1], recv_sem=recv_sems.at[k - 1],
                device_id=(px, py, pc), device_id_type=MESH).wait_recv()
        for cp in sends:
            cp.wait_send()
        acc = land_ref[0]
        for s in range(1, N_DEV):
            acc = acc + land_ref[s]
        o_ref[...] = acc

    vm = pl.BlockSpec(memory_space=pltpu.VMEM)
    return pl.pallas_call(
        body, name="allreduce_small",
        in_specs=[vm], out_specs=vm,
        out_shape=jax.ShapeDtypeStruct(vec.shape, F32),
        scratch_shapes=[pltpu.VMEM((N_DEV,) + vec.shape, F32), pltpu.SemaphoreType.DMA((N_DEV - 1,)),
                        pltpu.SemaphoreType.DMA((N_DEV - 1,))],
    )(vec)


def adamw(ws, gs, ms, vs, name):
    k = len(ws)
    rows, cols = ws[0].shape
    tr = _row_tile(rows, k * 2 * 7 * cols * 4, SMALL_KERNEL_VMEM, 8)
    c1 = 1.0 - ADAM_B1 ** ADAM_STEP
    c2 = 1.0 - ADAM_B2 ** ADAM_STEP

    def body(*refs):
        for a in range(k):
            w_ref, g_ref, m_ref, v_ref = (refs[s * k + a] for s in range(4))
            d_ref, nm_ref, nv_ref = (refs[(4 + s) * k + a] for s in range(3))
            gv = g_ref[...]
            nm = ADAM_B1 * m_ref[...] + (1.0 - ADAM_B1) * gv
            nv = ADAM_B2 * v_ref[...] + (1.0 - ADAM_B2) * (gv * gv)
            nm_ref[...] = nm
            nv_ref[...] = nv
            d_ref[...] = -ADAM_LR * ((nm / c1) / (jnp.sqrt(nv / c2) + ADAM_EPS) + ADAM_WD * w_ref[...])

    blk = pl.BlockSpec((tr, cols), lambda i: (i, 0))
    out = pl.pallas_call(
        body, name=name, grid=(rows // tr,),
        in_specs=[blk] * (4 * k), out_specs=[blk] * (3 * k),
        out_shape=[jax.ShapeDtypeStruct((rows, cols), F32)] * (3 * k),
        compiler_params=_params(("parallel",)),
    )(*ws, *gs, *ms, *vs)
    return [(out[a], out[k + a], out[2 * k + a]) for a in range(k)]


BIG = ("ffn1_w_gate", "ffn1_w_up", "ffn1_w_down", "w_in", "b_w_uq", "b_w_ukv", "w_out",
       "ffn2_w_gate", "ffn2_w_up", "ffn2_w_down")
SMALL = ("ffn1_norm", "mix_norm", "a_q_norm", "a_k_norm", "a_rel_bias", "b_q_lat_norm", "b_kv_lat_norm",
         "b_q_nope_norm", "b_q_rope_norm", "b_k_nope_norm", "b_k_rope_norm", "ffn2_norm", "final_norm")
WEIGHTS = ("ffn1_norm", "ffn1_w_gate", "ffn1_w_up", "ffn1_w_down", "mix_norm", "w_in", "a_q_norm", "a_k_norm",
           "a_rel_bias", "b_q_lat_norm", "b_w_uq", "b_kv_lat_norm", "b_w_ukv", "b_q_nope_norm", "b_q_rope_norm",
           "b_k_nope_norm", "b_k_rope_norm", "w_out", "ffn2_norm", "ffn2_w_gate", "ffn2_w_up", "ffn2_w_down",
           "final_norm")
TRANSPOSED = ("ffn1_w_gate", "ffn1_w_up", "ffn2_w_gate", "ffn2_w_up", "w_in", "b_w_uq")
PACK_SHAPE = (8, 1024)


def _pack_small(d, last=None):
    flat = [d[n].reshape(-1) for n in SMALL]
    used = sum(f.shape[0] for f in flat)
    total = PACK_SHAPE[0] * PACK_SHAPE[1]
    tail = jnp.zeros((total - used - 1,), F32)
    end = jnp.zeros((1,), F32) if last is None else last.reshape(1)
    return jnp.concatenate(flat + [tail, end]).reshape(PACK_SHAPE)


def _unpack_small(p, like):
    flat = p.reshape(-1)
    out, off = {}, 0
    for n in SMALL:
        size = like[n].size
        out[n] = flat[off:off + size].reshape(like[n].shape)
        off += size
    return out, flat[-1]


def _cols_to_shards(g):
    rows, cols = g.shape
    return g.reshape(rows, N_SHARD, cols // N_SHARD).transpose(1, 0, 2)


def _shards_to_cols(g):
    return g.transpose(1, 0, 2).reshape(g.shape[1], -1)


def kernel(x, ffn1_norm, ffn1_w_gate, ffn1_w_up, ffn1_w_down, mix_norm, w_in, a_q_norm, a_k_norm, a_rel_bias, b_q_lat_norm, b_w_uq, b_kv_lat_norm, b_w_ukv, b_q_nope_norm, b_q_rope_norm, b_k_nope_norm, b_k_rope_norm, w_out, ffn2_norm, ffn2_w_gate, ffn2_w_up, ffn2_w_down, final_norm, loss_target, m_ffn1_norm, m_ffn1_w_gate, m_ffn1_w_up, m_ffn1_w_down, m_mix_norm, m_w_in, m_a_q_norm, m_a_k_norm, m_a_rel_bias, m_b_q_lat_norm, m_b_w_uq, m_b_kv_lat_norm, m_b_w_ukv, m_b_q_nope_norm, m_b_q_rope_norm, m_b_k_nope_norm, m_b_k_rope_norm, m_w_out, m_ffn2_norm, m_ffn2_w_gate, m_ffn2_w_up, m_ffn2_w_down, m_final_norm, v_ffn1_norm, v_ffn1_w_gate, v_ffn1_w_up, v_ffn1_w_down, v_mix_norm, v_w_in, v_a_q_norm, v_a_k_norm, v_a_rel_bias, v_b_q_lat_norm, v_b_w_uq, v_b_kv_lat_norm, v_b_w_ukv, v_b_q_nope_norm, v_b_q_rope_norm, v_b_k_nope_norm, v_b_k_rope_norm, v_w_out, v_ffn2_norm, v_ffn2_w_gate, v_ffn2_w_up, v_ffn2_w_down, v_final_norm):
    args = locals()
    view = lambda a, n: a[0].T if n in TRANSPOSED else a[0]
    wts = {n: view(args[n], n) for n in WEIGHTS}
    mom = {n: view(args["m_" + n], n) for n in WEIGHTS}
    var = {n: view(args["v_" + n], n) for n in WEIGHTS}

    shard = 2 * lax.axis_index("x") + lax.axis_index("y")
    core = lax.axis_index("c")
    first = [n for n in BIG if n not in LATE]
    own = [wts[n].astype(BF16) for n in first]
    w = {n: wts[n] if n == "a_rel_bias" else wts[n][None] for n in SMALL}
    w.update(kernel_layout({n: lax.dynamic_update_index_in_dim(got, mine, shard, 0)
                            for n, got, mine in zip(first, allgather_shards(own), own)}))

    loss, grad_x, g, landed = local_step(x[0], loss_target[0], w,
                                         late=([wts[n].astype(BF16) for n in LATE], shard))

    me = 2 * shard + core
    for n in first:
        piece = lax.dynamic_slice(g[n], (shard, core * (FS // 2), 0), (1, FS // 2, D_MODEL))
        landed[n] = lax.dynamic_update_slice(landed[n], piece, (me, 0, 0))
    groups = {}
    for n in BIG:
        groups.setdefault(wts[n].shape, []).append(n)
    half = {}
    for names in groups.values():
        half.update(zip(names, sum_slots([landed[n] for n in names], "sum_slots")))
    halves = [half[n] for n in BIG]
    shapes = [wts[n].shape for n in BIG]
    axes = [_split_axis(s) for s in shapes]
    grads = dict(zip(BIG, (lax.dynamic_update_slice_in_dim(got, mine, core * mine.shape[ax], ax)
                           for got, mine, ax in zip(join_halves(halves, shapes), halves, axes))))

    small_sum, loss_sum = _unpack_small(allreduce_small(_pack_small(g, loss[0, 0])), wts)
    grads.update(small_sum)

    delta, new_m, new_v = {}, {}, {}
    for names in groups.values():
        stepped = adamw(*([d[n] for n in names] for d in (wts, grads, mom, var)), "adamw")
        for n, (d_new, m_new, v_new) in zip(names, stepped):
            delta[n], new_m[n], new_v[n] = d_new, m_new, v_new
    (packed,) = adamw([_pack_small(wts)], [_pack_small(grads)], [_pack_small(mom)], [_pack_small(var)], "adamw_small")
    for dst, p in zip((delta, new_m, new_v), packed):
        dst.update(_unpack_small(p, wts)[0])

    lead = lambda d: [(d[n].T if n in TRANSPOSED else d[n])[None] for n in WEIGHTS]
    return (loss_sum, grad_x[None], *lead(grads), *lead(delta), *lead(new_m), *lead(new_v))
```

```python
import numpy as np
import jax
import jax.numpy as jnp
from jax import lax
from jax.experimental import pallas as pl
from jax.experimental.pallas import tpu as pltpu

F32 = jnp.float32
BF16 = jnp.bfloat16
EPS = 1e-6
NEG = -1e30

D_MODEL = 1024
D_FF = 2816
N_SHARD = 4
FS = D_FF // N_SHARD
CHUNK = 64
A_LEFT = 8
A_MAX_REL = 128
HEADS = 8
HD = 64
ROPE = 32
PROJ_W = 2048
IN_COLS = 1952
B_SCALE = 96 ** -0.5
LANES = 128

ADAM_LR = 0.001
ADAM_B1 = 0.9
ADAM_B2 = 0.999
ADAM_EPS = 1e-08
ADAM_WD = 0.01
ADAM_STEP = 10

VMEM_LIMIT = 56 * 1024 * 1024
TOKEN_TILE = 512

MESH = pl.DeviceIdType.MESH


def _dot(a, b):
    return lax.dot_general(a, b, (((1,), (0,)), ((), ())), preferred_element_type=F32)


def _dot_nt(a, b):
    return lax.dot_general(a, b, (((1,), (1,)), ((), ())), preferred_element_type=F32)


def _dot_tn(a, b):
    return lax.dot_general(a, b, (((0,), (0,)), ((), ())), preferred_element_type=F32)


def _params(sem):
    return pltpu.CompilerParams(dimension_semantics=sem, vmem_limit_bytes=VMEM_LIMIT)


def _rms(xv):
    r = lax.rsqrt(jnp.mean(xv * xv, axis=-1, keepdims=True) + EPS)
    return r, xv * r


def ffn_fwd(x, g, wg, wu, wd, name, gather=(), loss_head=None, pre_proj=None):
    t, d = x.shape
    tm = TOKEN_TILE
    ni = t // tm
    plan = GatherPlan(gather)
    n = plan.n
    head = () if loss_head is None else tuple(loss_head)
    q = len(head)
    proj = () if pre_proj is None else tuple(pre_proj)
    r = len(proj)
    half = proj[0].shape[1] if r else 0

    def body(*refs):
        it = iter(refs)
        take = lambda count: [next(it) for _ in range(count)]
        x_ref, g_ref, wg_ref, wu_ref, wd_ref = take(5)
        head_in, proj_in, ins = take(q), take(r), take(n)
        o_ref, gp_ref, up_ref = take(3)
        head_out, x_out, outs = take(q), take(1 if r else 0), take(n)
        h_ref, acc_ref = take(2)
        sems = list(it)
        i, j = pl.program_id(0), pl.program_id(1)
        if n:
            pl.when((i == 0) & (j == 0))(lambda: plan.start(ins, outs, sems))
            pl.when((i == (3 * ni) // 4) & (j == 0))(lambda: plan.forward(ins, outs, sems))

        @pl.when(j == 0)
        def _():
            xin = x_ref[...]
            if r:
                oa_ref, obt_ref, wo_ref = proj_in
                xin = xin + _dot(oa_ref[...], wo_ref[0:half, :]) + _dot_tn(obt_ref[...], wo_ref[half:2 * half, :])
                x_out[0][...] = xin
            _, xn = _rms(xin)
            h_ref[...] = (xn * g_ref[...]).astype(BF16)
            acc_ref[...] = jnp.zeros_like(acc_ref)

        h = h_ref[...]
        gp = _dot_nt(h, wg_ref[0])
        up = _dot_nt(h, wu_ref[0])
        gp_ref[0] = gp
        up_ref[0] = up
        a = (gp * jax.nn.sigmoid(gp) * up).astype(BF16)
        acc_ref[...] += _dot(a, wd_ref[0])

        @pl.when(j == N_SHARD - 1)
        def _():
            y = (x_out[0][...] if r else x_ref[...]) + 0.5 * acc_ref[...]
            if not q:
                o_ref[...] = y
                return
            (gf_ref, t_ref), (dgf_ref, loss_ref) = head_in, head_out
            rinv, yn = _rms(y)
            gf = gf_ref[...]
            e = yn * gf - t_ref[...]
            dout = e * (1.0 / d)
            dng = dout * gf
            o_ref[...] = rinv * (dng - yn * jnp.mean(dng * yn, axis=-1, keepdims=True))

            @pl.when(i == 0)
            def _():
                dgf_ref[...] = jnp.zeros_like(dgf_ref)
                loss_ref[...] = jnp.zeros_like(loss_ref)

            dgf_ref[...] += jnp.sum(dout * yn, axis=0, keepdims=True)
            part = jnp.sum(jnp.sum(e * e, axis=-1, keepdims=True), axis=0, keepdims=True) * (0.5 / d)
            loss_ref[...] += jnp.broadcast_to(part, loss_ref.shape)

        if n:
            pl.when((i == ni - 1) & (j == N_SHARD - 1))(lambda: plan.finish(ins, outs, sems))

    tok = pl.BlockSpec((tm, d), lambda i, j: (i, 0))
    vec = pl.BlockSpec((1, d), lambda i, j: (0, 0))
    chunk = pl.BlockSpec((1, FS, d), lambda i, j: (j, 0, 0))
    pre = pl.BlockSpec((1, tm, FS), lambda i, j: (j, i, 0))
    return pl.pallas_call(
        body, name=name, grid=(ni, N_SHARD),
        in_specs=[tok, vec, chunk, chunk, chunk] + [vec, tok][:q]
        + [pl.BlockSpec((tm, half), lambda i, j: (i, 0)), pl.BlockSpec((half, tm), lambda i, j: (0, i)),
           pl.BlockSpec((2 * half, d), lambda i, j: (0, 0))][:r] + [ANY] * n,
        out_specs=[tok, pre, pre] + [vec, pl.BlockSpec((1, LANES), lambda i, j: (0, 0))][:q] + [tok][:r] + [ANY] * n,
        out_shape=[jax.ShapeDtypeStruct((t, d), F32), jax.ShapeDtypeStruct((N_SHARD, t, FS), F32),
                   jax.ShapeDtypeStruct((N_SHARD, t, FS), F32)]
        + [jax.ShapeDtypeStruct((1, d), F32), jax.ShapeDtypeStruct((1, LANES), F32)][:q]
        + [jax.ShapeDtypeStruct((t, d), F32)][:r] + plan.out_shape,
        scratch_shapes=[pltpu.VMEM((tm, d), BF16), pltpu.VMEM((tm, d), F32)] + (plan.scratch if n else []),
        compiler_params=_params(("arbitrary", "arbitrary")),
    )(x, g, wg, wu, wd, *head, *proj, *gather)


def ffn_bwd(x, dout, g, wg, wu, wd, gate, up_pre, name, scatter=(), spread=None):
    t, d = x.shape
    tm = TOKEN_TILE
    ni = t // tm
    hf = FS // 2
    plan = ScatterPlan(scatter)
    m = plan.n
    k = 0 if spread is None else 3
    steps = jnp.arange(N_SHARD, dtype=jnp.int32)
    order = steps if spread is None else (spread + 1 + steps) % N_SHARD

    def body(*refs):
        ord_ref, x_ref, do_ref, g_ref, wg_ref, wu_ref, wd_ref, gp_ref, up_ref = refs[:9]
        ins, (dwg_out, dwu_out, dwd_out, dhp_ref) = refs[9:9 + m], refs[9 + m:13 + m]
        outs, lands = refs[13 + m:13 + 2 * m], refs[13 + 2 * m:13 + 2 * m + k]
        dwg_ref, dwu_ref, dwd_ref = refs[13 + 2 * m + k:16 + 2 * m + k]
        sems = refs[16 + 2 * m + k:19 + 2 * m + k] if m else ()
        stage_ref = refs[-3] if k else None
        j, i = pl.program_id(0), pl.program_id(1)
        if m:
            pl.when((j == 0) & (i == 0))(lambda: plan.start(ins, outs, sems))

        def chunk_copies(jj):
            send_sems, recv_sems = refs[-2:]
            px, py, pc = _place()
            me = 4 * px + 2 * py + pc
            tx, ty = ord_ref[jj] // 2, ord_ref[jj] % 2
            copies = []
            for n_ in range(3):
                for h_ in range(2):
                    copies.append((pltpu.make_async_remote_copy(
                        src_ref=stage_ref.at[n_, pl.ds(h_ * hf, hf)], dst_ref=lands[n_].at[me],
                        send_sem=send_sems.at[6 * jj + 2 * n_ + h_], recv_sem=recv_sems.at[3 * me + n_],
                        device_id=(tx, ty, h_), device_id_type=MESH), (tx != px) | (ty != py) | (pc != h_)))
            return copies

        def arrivals():
            send_sems, recv_sems = refs[-2:]
            px, py, pc = _place()
            me = 4 * px + 2 * py + pc
            for s_ in range(N_DEV):
                for n_ in range(3):
                    slot = lands[n_].at[s_]
                    cp = pltpu.make_async_remote_copy(
                        src_ref=slot, dst_ref=slot, send_sem=send_sems.at[0], recv_sem=recv_sems.at[3 * s_ + n_],
                        device_id=(px, py, pc), device_id_type=MESH)
                    pl.when(me != s_)(cp.wait_recv)

        _, xn = _rms(x_ref[...])
        h = (xn * g_ref[...]).astype(BF16)
        dz = (0.5 * do_ref[...]).astype(BF16)
        wgv, wuv, wdv = wg_ref[0], wu_ref[0], wd_ref[0]
        gp, up = gp_ref[0], up_ref[0]
        s = jax.nn.sigmoid(gp)
        sg = gp * s
        a = (sg * up).astype(BF16)
        da = _dot_nt(dz, wdv)
        dup = (da * sg).astype(BF16)
        dgp = (da * up * (s * (1.0 + gp * (1.0 - s)))).astype(BF16)

        @pl.when(i == 0)
        def _():
            dwg_ref[...] = jnp.zeros_like(dwg_ref)
            dwu_ref[...] = jnp.zeros_like(dwu_ref)
            dwd_ref[...] = jnp.zeros_like(dwd_ref)

        dwd_ref[...] += _dot_tn(a, dz)
        dwg_ref[...] += _dot_tn(dgp, h)
        dwu_ref[...] += _dot_tn(dup, h)
        dhp_ref[0] = (_dot(dgp, wgv) + _dot(dup, wuv)).astype(BF16)

        @pl.when(i == ni - 1)
        def _():
            dwg_out[0] = dwg_ref[...].astype(BF16)
            dwu_out[0] = dwu_ref[...].astype(BF16)
            dwd_out[0] = dwd_ref[...].astype(BF16)
            if k:
                @pl.when(j >= 1)
                def _():
                    for cp, leaves in chunk_copies(j - 1):
                        pl.when(leaves)(cp.wait_send)
                for n_, acc in enumerate((dwg_ref, dwu_ref, dwd_ref)):
                    stage_ref[n_] = acc[...].astype(BF16)
                for cp, leaves in chunk_copies(j):
                    pl.when(leaves)(cp.start)

                @pl.when(j == N_SHARD - 1)
                def _():
                    for cp, leaves in chunk_copies(N_SHARD - 1):
                        pl.when(leaves)(cp.wait_send)
                    arrivals()

        if m:
            pl.when((j == N_SHARD - 1) & (i == ni - 1))(lambda: plan.finish(ins, outs, sems))

    chunk = pl.BlockSpec((1, FS, d), lambda j, i, o: (o[j], 0, 0))
    tok = pl.BlockSpec((tm, d), lambda j, i, o: (i, 0))
    pre = pl.BlockSpec((1, tm, FS), lambda j, i, o: (o[j], i, 0))
    grid_spec = pltpu.PrefetchScalarGridSpec(
        num_scalar_prefetch=1, grid=(N_SHARD, ni),
        in_specs=[tok, tok, pl.BlockSpec((1, d), lambda j, i, o: (0, 0)), chunk, chunk, chunk, pre, pre] + [ANY] * m,
        out_specs=[chunk, chunk, chunk, pl.BlockSpec((1, tm, d), lambda j, i, o: (o[j], i, 0))] + [ANY] * (m + k),
        scratch_shapes=[pltpu.VMEM((FS, d), F32), pltpu.VMEM((FS, d), F32), pltpu.VMEM((FS, d), F32)]
        + (plan.scratch if m else [])
        + ([pltpu.VMEM((3, FS, d), BF16), pltpu.SemaphoreType.DMA((6 * N_SHARD,)),
            pltpu.SemaphoreType.DMA((3 * N_DEV,))] if k else []))
    return pl.pallas_call(
        body, name=name, grid_spec=grid_spec,
        out_shape=[jax.ShapeDtypeStruct((N_SHARD, FS, d), BF16),
                   jax.ShapeDtypeStruct((N_SHARD, FS, d), BF16),
                   jax.ShapeDtypeStruct((N_SHARD, FS, d), BF16),
                   jax.ShapeDtypeStruct((N_SHARD, t, d), BF16)] + plan.out_shape
        + [jax.ShapeDtypeStruct((N_DEV, hf, d), BF16)] * k,
        compiler_params=_params(("arbitrary", "arbitrary")),
    )(order, x, dout, g, wg, wu, wd, gate, up_pre, *scatter)


def norm_bwd(x, g, dhp, dres, name):
    t, d = x.shape
    p = dhp.shape[0]
    tm = TOKEN_TILE

    def body(x_ref, g_ref, dhp_ref, dres_ref, dx_ref, dg_ref):
        i = pl.program_id(0)
        r, xn = _rms(x_ref[...])
        dh = dhp_ref[0].astype(F32)
        for q in range(1, p):
            dh = dh + dhp_ref[q].astype(F32)
        dhg = dh * g_ref[...]
        dx_ref[...] = dres_ref[...] + r * (dhg - xn * jnp.mean(dhg * xn, axis=-1, keepdims=True))

        @pl.when(i == 0)
        def _():
            dg_ref[...] = jnp.zeros_like(dg_ref)

        dg_ref[...] += jnp.sum(dh * xn, axis=0, keepdims=True)

    return pl.pallas_call(
        body, name=name, grid=(t // tm,),
        in_specs=[pl.BlockSpec((tm, d), lambda i: (i, 0)),
                  pl.BlockSpec((1, d), lambda i: (0, 0)),
                  pl.BlockSpec((p, tm, d), lambda i: (0, i, 0)),
                  pl.BlockSpec((tm, d), lambda i: (i, 0))],
        out_specs=[pl.BlockSpec((tm, d), lambda i: (i, 0)),
                   pl.BlockSpec((1, d), lambda i: (0, 0))],
        out_shape=[jax.ShapeDtypeStruct((t, d), F32), jax.ShapeDtypeStruct((1, d), F32)],
        compiler_params=_params(("arbitrary",)),
    )(x, g, dhp, dres)


def out_proj_bwd(x, g, dhp, dres, w, oa, ob_t):
    t, d = x.shape
    half = w.shape[0] // 2
    p = dhp.shape[0]
    tm = TOKEN_TILE
    ni = t // tm

    def body(x_ref, g_ref, dhp_ref, dres_ref, w_ref, oa_ref, obt_ref,
             dx_ref, dg_ref, da_ref, db_ref, dbt_ref, dw_ref, acc_ref):
        i = pl.program_id(0)

        @pl.when(i == 0)
        def _():
            dg_ref[...] = jnp.zeros_like(dg_ref)
            acc_ref[...] = jnp.zeros_like(acc_ref)

        r, xn = _rms(x_ref[...])
        dh = dhp_ref[0].astype(F32)
        for s in range(1, p):
            dh = dh + dhp_ref[s].astype(F32)
        dhg = dh * g_ref[...]
        dx = dres_ref[...] + r * (dhg - xn * jnp.mean(dhg * xn, axis=-1, keepdims=True))
        dx_ref[...] = dx
        dg_ref[...] += jnp.sum(dh * xn, axis=0, keepdims=True)
        dxb = dx.astype(BF16)
        da_ref[...] = _dot_nt(dxb, w_ref[0:half, :]).astype(BF16)
        db_ref[...] = _dot_nt(dxb, w_ref[half:2 * half, :]).astype(BF16)
        dbt_ref[...] = _dot_nt(w_ref[half:2 * half, :], dxb).astype(BF16)
        acc_ref[0:half, :] += _dot_tn(oa_ref[...], dxb)
        acc_ref[half:2 * half, :] += _dot(obt_ref[...], dxb)

        @pl.when(i == ni - 1)
        def _():
            dw_ref[...] = acc_ref[...].astype(BF16)

    row = lambda w_: pl.BlockSpec((tm, w_), lambda i: (i, 0))
    col = pl.BlockSpec((half, tm), lambda i: (0, i))
    whole = pl.BlockSpec((2 * half, d), lambda i: (0, 0))
    vec = pl.BlockSpec((1, d), lambda i: (0, 0))
    return pl.pallas_call(
        body, name="out_proj_bwd", grid=(ni,),
        in_specs=[row(d), vec, pl.BlockSpec((p, tm, d), lambda i: (0, i, 0)), row(d), whole, row(half), col],
        out_specs=[row(d), vec, row(half), row(half), col, whole],
        out_shape=[jax.ShapeDtypeStruct((t, d), F32), jax.ShapeDtypeStruct((1, d), F32),
                   jax.ShapeDtypeStruct((t, half), BF16), jax.ShapeDtypeStruct((t, half), BF16),
                   jax.ShapeDtypeStruct((half, t), BF16), jax.ShapeDtypeStruct((2 * half, d), BF16)],
        scratch_shapes=[pltpu.VMEM((2 * half, d), F32)],
        compiler_params=_params(("arbitrary",)),
    )(x, g, dhp, dres, w, oa, ob_t)


def _lane(shape):
    return lax.broadcasted_iota(jnp.int32, shape, 1)


PAIR = (0, HD, LANES)
Q_HEAD = (0, HD, HD + ROPE, LANES)
K_ROPE = (0, ROPE, LANES)


def _seg_mean(z, bounds):
    seg = lambda v: sum([(v >= b).astype(jnp.int32) for b in bounds[1:-1]], jnp.zeros_like(v))
    rows = seg(lax.broadcasted_iota(jnp.int32, (LANES, LANES), 0))
    cols = seg(lax.broadcasted_iota(jnp.int32, (LANES, LANES), 1))
    same = (rows == cols).astype(BF16)
    lane = _lane((1, LANES))
    inv = sum([jnp.where((lane >= a) & (lane < b), 1.0 / (b - a), 0.0) for a, b in zip(bounds[:-1], bounds[1:])])
    hi = z.astype(BF16)
    lo = (z - hi.astype(F32)).astype(BF16)
    return (_dot(hi, same) + _dot(lo, same)) * inv


def _seg_norm(x, bounds):
    r = lax.rsqrt(_seg_mean(x * x, bounds) + EPS)
    return r, x * r


def _seg_norm_bwd(r, xn, dyg, bounds):
    return r * (dyg - xn * _seg_mean(dyg * xn, bounds))


A_TM = 256


def mix_fwd(x, g, w, gq, gk, gcq, gckv):
    t, d = x.shape
    tm = A_TM

    def body(x_ref, g_ref, w_ref, gq_ref, gk_ref, gcq_ref, gckv_ref,
             h_ref, p_ref, qa_ref, ka_ref, va_ref, cq_ref, ckv_ref):
        _, xn = _rms(x_ref[...])
        h = (xn * g_ref[...]).astype(BF16)
        h_ref[...] = h
        p_ref[...] = _dot_nt(h, w_ref[...])
        for p in range(4):
            sl = slice(LANES * p, LANES * (p + 1))
            _, xn = _seg_norm(p_ref[:, sl], PAIR)
            qa_ref[:, sl] = (xn * gq_ref[:, sl] * 0.125).astype(BF16)
            _, xn = _seg_norm(p_ref[:, 512 + LANES * p:512 + LANES * (p + 1)], PAIR)
            ka_ref[:, sl] = (xn * gk_ref[:, sl]).astype(BF16)
        va_ref[...] = p_ref[:, 1024:1536].astype(BF16)
        _, xn = _rms(p_ref[:, 1536:1792])
        cq_ref[...] = (xn * gcq_ref[...]).astype(BF16)
        _, xn = _rms(p_ref[:, 1792:1920])
        ckv_ref[...] = (xn * gckv_ref[...]).astype(BF16)

    row = lambda w: pl.BlockSpec((tm, w), lambda i: (i, 0))
    vec = lambda w: pl.BlockSpec((1, w), lambda i: (0, 0))
    return pl.pallas_call(
        body, name="mix_fwd", grid=(t // tm,),
        in_specs=[row(d), vec(d), pl.BlockSpec((PROJ_W, d), lambda i: (0, 0)), vec(512), vec(512), vec(256), vec(128)],
        out_specs=[row(d), row(PROJ_W), row(512), row(512), row(512), row(256), row(128)],
        out_shape=[jax.ShapeDtypeStruct((t, d), BF16), jax.ShapeDtypeStruct((t, PROJ_W), F32)]
        + [jax.ShapeDtypeStruct((t, w_), BF16) for w_ in (512, 512, 512, 256, 128)],
        compiler_params=_params(("parallel",)),
    )(x, g, w, gq, gk, gcq, gckv)


def mix_bwd(proj, x, h, dres, w, g, dqa, dkp, dvp, dcq, dckv, dkr, gq, gk, gcq, gckv):
    t, d = x.shape
    tm = A_TM
    nb = t // tm

    def body(p_ref, x_ref, h_ref, dres_ref, w_ref, g_ref, dqa_ref, dk0_ref, dk1_ref, dk2_ref, dv0_ref, dv1_ref,
             dv2_ref, dcq_ref, dckv_ref, dkr_ref, gq_ref, gk_ref, gcq_ref, gckv_ref,
             dx_ref, dw_ref, dg_ref, dgq_ref, dgk_ref, dgcq_ref, dgckv_ref, dp_ref, acc_ref):
        i = pl.program_id(0)

        @pl.when(i == 0)
        def _():
            for ref in (acc_ref, dg_ref, dgq_ref, dgk_ref, dgcq_ref, dgckv_ref):
                ref[...] = jnp.zeros_like(ref)

        has1 = (i + 1 < nb).astype(F32)
        has2 = (i + 2 < nb).astype(F32)
        for p in range(4):
            sl = slice(LANES * p, LANES * (p + 1))
            r, xn = _seg_norm(p_ref[:, sl], PAIR)
            dy = dqa_ref[:, sl] * 0.125
            dp_ref[:, sl] = _seg_norm_bwd(r, xn, dy * gq_ref[:, sl], PAIR).astype(BF16)
            dgq_ref[:, sl] += jnp.sum(dy * xn, axis=0, keepdims=True)
            ks = slice(512 + LANES * p, 512 + LANES * (p + 1))
            r, xn = _seg_norm(p_ref[:, ks], PAIR)
            dy = dk0_ref[0, :, sl] + has1 * dk1_ref[0, :, sl] + has2 * dk2_ref[0, :, sl]
            dp_ref[:, ks] = _seg_norm_bwd(r, xn, dy * gk_ref[:, sl], PAIR).astype(BF16)
            dgk_ref[:, sl] += jnp.sum(dy * xn, axis=0, keepdims=True)
        dp_ref[:, 1024:1536] = (dv0_ref[0] + has1 * dv1_ref[0] + has2 * dv2_ref[0]).astype(BF16)
        for (a, b, dlat_ref, glat_ref, dglat_ref) in ((1536, 1792, dcq_ref, gcq_ref, dgcq_ref),
                                                      (1792, 1920, dckv_ref, gckv_ref, dgckv_ref)):
            r, xn = _rms(p_ref[:, a:b])
            dy = dlat_ref[...]
            dyg = dy * glat_ref[...]
            dp_ref[:, a:b] = (r * (dyg - xn * jnp.mean(dyg * xn, axis=-1, keepdims=True))).astype(BF16)
            dglat_ref[...] += jnp.sum(dy * xn, axis=0, keepdims=True)
        dp_ref[:, 1920:2048] = dkr_ref[...].astype(BF16)

        dproj = dp_ref[...]
        acc_ref[...] += _dot_tn(dproj, h_ref[...])
        dh = _dot(dproj, w_ref[...])
        r, xn = _rms(x_ref[...])
        dhg = dh * g_ref[...]
        dx_ref[...] = dres_ref[...] + r * (dhg - xn * jnp.mean(dhg * xn, axis=-1, keepdims=True))
        dg_ref[...] += jnp.sum(dh * xn, axis=0, keepdims=True)

        @pl.when(i == nb - 1)
        def _():
            dw_ref[...] = acc_ref[...].astype(BF16)

    row = lambda w_: pl.BlockSpec((tm, w_), lambda i: (i, 0))
    vec = lambda w_: pl.BlockSpec((1, w_), lambda i: (0, 0))
    part = lambda s: pl.BlockSpec((1, tm, 512), lambda i: (s, jnp.minimum(i + s, nb - 1), 0))
    whole = pl.BlockSpec((PROJ_W, d), lambda i: (0, 0))
    return pl.pallas_call(
        body, name="mix_bwd", grid=(nb,),
        in_specs=[row(PROJ_W), row(d), row(d), row(d), whole, vec(d), row(512), part(0), part(1), part(2),
                  part(0), part(1), part(2), row(256), row(128), row(128), vec(512), vec(512), vec(256), vec(128)],
        out_specs=[row(d), whole, vec(d), vec(512), vec(512), vec(256), vec(128)],
        out_shape=[jax.ShapeDtypeStruct((t, d), F32), jax.ShapeDtypeStruct((PROJ_W, d), BF16)]
        + [jax.ShapeDtypeStruct((1, w_), F32) for w_ in (d, 512, 512, 256, 128)],
        scratch_shapes=[pltpu.VMEM((tm, PROJ_W), BF16), pltpu.VMEM((PROJ_W, d), F32)],
        compiler_params=_params(("arbitrary",)),
    )(proj, x, h, dres, w, g, dqa, dkp, dkp, dkp, dvp, dvp, dvp, dcq, dckv, dkr, gq, gk, gcq, gckv)


def _roll(x, shift):
    return pltpu.roll(x, shift % LANES, 1)


def _rope(y, c, s1, s2):
    return y * c + _roll(y, -16) * s1 + _roll(y, 16) * s2


def _rope_bwd(d, c, s1, s2):
    return d * c + _roll(d * s1, 16) + _roll(d * s2, -16)


def _up_proj(cq_ref, ckv_ref, wuq_ref, wukv_ref):
    return _dot_nt(cq_ref[...], wuq_ref[...]), _dot(ckv_ref[...], wukv_ref[...])


def prep2_fwd(cqn, ckvn, proj, wuq, wukv, gq, gk, gkr, tabs):
    t = cqn.shape[0]
    tm = TOKEN_TILE

    def body(cq_ref, ckv_ref, kr_ref, wuq_ref, wukv_ref, gq_ref, gk_ref, gkr_ref, tab_ref, qf_ref, kf_ref, vp_ref):
        q_all, kv_all = _up_proj(cq_ref, ckv_ref, wuq_ref, wukv_ref)
        _, xn = _seg_norm(kr_ref[...], K_ROPE)
        kpe = _roll(_rope(xn * gkr_ref[...], tab_ref[3], tab_ref[4], tab_ref[5]), 64)
        for h in range(HEADS):
            sl = slice(LANES * h, LANES * (h + 1))
            _, xn = _seg_norm(q_all[:, sl], Q_HEAD)
            qf_ref[:, sl] = (_rope(xn * gq_ref[...], tab_ref[0], tab_ref[1], tab_ref[2]) * B_SCALE2).astype(BF16)
            x = kv_all[:, sl]
            lo = _lane(x.shape) < HD
            _, xkn = _seg_norm(jnp.where(lo, x, 0.0), PAIR)
            kf_ref[:, sl] = (xkn * gk_ref[...] + kpe).astype(BF16)
            if h % 2 == 0:
                v_even = _roll(x, 64)
            else:
                vp_ref[:, LANES * (h // 2):LANES * (h // 2 + 1)] = jnp.where(lo, v_even, x).astype(BF16)

    row = lambda w: pl.BlockSpec((tm, w), lambda i: (i, 0))
    vec = lambda w: pl.BlockSpec((1, w), lambda i: (0, 0))
    full = lambda a: pl.BlockSpec(a.shape, lambda i: (0, 0))
    return pl.pallas_call(
        body, name="prep2_fwd", grid=(t // tm,),
        in_specs=[row(256), row(128), pl.BlockSpec((tm, LANES), lambda i: (i, 15)), full(wuq), full(wukv),
                  vec(128), vec(128), vec(128), pl.BlockSpec((6, tm, LANES), lambda i: (0, i, 0))],
        out_specs=[row(1024), row(1024), row(512)],
        out_shape=[jax.ShapeDtypeStruct((t, 1024), BF16), jax.ShapeDtypeStruct((t, 1024), BF16),
                   jax.ShapeDtypeStruct((t, 512), BF16)],
        compiler_params=_params(("parallel",)),
    )(cqn, ckvn, proj, wuq, wukv, gq, gk, gkr, tabs)


def prep2_bwd(cqn, ckvn, proj, wuq, wukv, dqf, dkf, dvp, gq, gk, gkr, tabs):
    t = cqn.shape[0]
    tm = TOKEN_TILE

    def body(cq_ref, ckv_ref, kr_ref, wuq_ref, wukv_ref, dqf_ref, dkf_ref, dvp_ref, gq_ref, gk_ref, gkr_ref, tab_ref,
             dcq_ref, dckv_ref, dkr_ref, dwuq_ref, dwukv_ref, dgq_ref, dgk_ref, dgkr_ref, dq_ref, dkv_ref):
        i = pl.program_id(0)

        @pl.when(i == 0)
        def _():
            for ref in (dwuq_ref, dwukv_ref, dgq_ref, dgk_ref, dgkr_ref):
                ref[...] = jnp.zeros_like(ref)

        q_all, kv_all = _up_proj(cq_ref, ckv_ref, wuq_ref, wukv_ref)
        dgq = jnp.zeros((1, LANES), F32)
        dgk = jnp.zeros((1, LANES), F32)
        dkpe = jnp.zeros((tm, LANES), F32)
        for h in range(HEADS):
            sl = slice(LANES * h, LANES * (h + 1))
            lane = _lane((tm, LANES))
            mn, mr = lane < HD, (lane >= HD) & (lane < HD + ROPE)
            r, xn = _seg_norm(q_all[:, sl], Q_HEAD)
            dy = _rope_bwd(dqf_ref[sl, :].T, tab_ref[0], tab_ref[1], tab_ref[2])
            dyg = dy * gq_ref[...]
            dq_ref[:, sl] = _seg_norm_bwd(r, xn, dyg, Q_HEAD).astype(BF16)
            dgq = dgq + jnp.sum(dy * xn, axis=0, keepdims=True)

            x = kv_all[:, sl]
            dk = dkf_ref[:, sl]
            rk, xkn = _seg_norm(jnp.where(mn, x, 0.0), PAIR)
            dyk = jnp.where(mn, dk, 0.0)
            dxk = _seg_norm_bwd(rk, xkn, dyk * gk_ref[...], PAIR)
            dgk = dgk + jnp.sum(dyk * xkn, axis=0, keepdims=True)
            dkpe = dkpe + jnp.where(mr, dk, 0.0)
            dvpair = dvp_ref[:, LANES * (h // 2):LANES * (h // 2 + 1)]
            dv = _roll(dvpair, 64) if h % 2 == 0 else dvpair
            dkv_ref[:, sl] = jnp.where(mn, dxk, dv).astype(BF16)

        r, xn = _seg_norm(kr_ref[...], K_ROPE)
        dy = _rope_bwd(_roll(dkpe, 64), tab_ref[3], tab_ref[4], tab_ref[5])
        dkr_ref[...] = _seg_norm_bwd(r, xn, dy * gkr_ref[...], K_ROPE)
        dgq_ref[...] += dgq
        dgk_ref[...] += dgk
        dgkr_ref[...] += jnp.sum(dy * xn, axis=0, keepdims=True)
        dqb, dkvb = dq_ref[...], dkv_ref[...]
        dcq_ref[...] = _dot(dqb, wuq_ref[...])
        dckv_ref[...] = _dot_nt(dkvb, wukv_ref[...])
        dwuq_ref[...] += _dot_tn(dqb, cq_ref[...])
        dwukv_ref[...] += _dot_tn(ckv_ref[...], dkvb)

    row = lambda w: pl.BlockSpec((tm, w), lambda i: (i, 0))
    vec = lambda w: pl.BlockSpec((1, w), lambda i: (0, 0))
    full = lambda a: pl.BlockSpec(a.shape, lambda i: (0, 0))
    return pl.pallas_call(
        body, name="prep2_bwd", grid=(t // tm,),
        in_specs=[row(256), row(128), pl.BlockSpec((tm, LANES), lambda i: (i, 15)), full(wuq), full(wukv),
                  pl.BlockSpec((1024, tm), lambda i: (0, i)), row(1024), row(512),
                  vec(128), vec(128), vec(128), pl.BlockSpec((6, tm, LANES), lambda i: (0, i, 0))],
        out_specs=[row(256), row(128), row(128), full(wuq), full(wukv), vec(128), vec(128), vec(128)],
        out_shape=[jax.ShapeDtypeStruct((t, 256), F32), jax.ShapeDtypeStruct((t, 128), F32),
                   jax.ShapeDtypeStruct((t, LANES), F32), jax.ShapeDtypeStruct(wuq.shape, F32),
                   jax.ShapeDtypeStruct(wukv.shape, F32)] + [jax.ShapeDtypeStruct((1, LANES), F32)] * 3,
        scratch_shapes=[pltpu.VMEM((tm, 1024), BF16), pltpu.VMEM((tm, 1024), BF16)],
        compiler_params=_params(("arbitrary",)),
    )(cqn, ckvn, proj, wuq, wukv, dqf, dkf, dvp, gq, gk, gkr, tabs)


A_TQ = 256
A_WIN = 3 * A_TQ


def _a_specs(t):
    nb = t // A_TQ
    blk = lambda s: pl.BlockSpec((A_TQ, 512), lambda i: (jnp.maximum(i - s, 0), 0))
    return nb, blk


def _a_exp(q_ref, kc, b_ref, head, sl, lo):
    hm = lo if head % 2 == 0 else ~lo
    qm = jnp.where(hm, q_ref[:, sl], jnp.zeros((), BF16))
    s = _dot_nt(qm, kc) + b_ref[0, head]
    e = jnp.exp(s - jnp.max(s, axis=-1, keepdims=True))
    return hm, qm, e, 1.0 / jnp.sum(e, axis=-1, keepdims=True)


def _a_bias_spec():
    return pl.BlockSpec((1, HEADS, A_TQ, A_WIN), lambda i: (jnp.minimum(i, 2), 0, 0, 0))


def attn_a_fwd(qa, ka, va, bias):
    t = qa.shape[0]
    nb, blk = _a_specs(t)

    def body(q_ref, k2_ref, k1_ref, k0_ref, v2_ref, v1_ref, v0_ref, b_ref, o_ref):
        lo = _lane((A_TQ, LANES)) < HD
        for p in range(4):
            sl = slice(LANES * p, LANES * (p + 1))
            kc = jnp.concatenate([k2_ref[:, sl], k1_ref[:, sl], k0_ref[:, sl]], axis=0)
            vc = jnp.concatenate([v2_ref[:, sl], v1_ref[:, sl], v0_ref[:, sl]], axis=0)
            outs = []
            for h2 in range(2):
                _, _, e, inv = _a_exp(q_ref, kc, b_ref, 2 * p + h2, sl, lo)
                outs.append(_dot(e.astype(BF16), vc) * inv)
            o_ref[:, sl] = jnp.where(lo, outs[0], outs[1]).astype(BF16)

    return pl.pallas_call(
        body, name="attn_a_fwd", grid=(nb,),
        in_specs=[blk(0), blk(2), blk(1), blk(0), blk(2), blk(1), blk(0), _a_bias_spec()],
        out_specs=pl.BlockSpec((A_TQ, 512), lambda i: (i, 0)),
        out_shape=jax.ShapeDtypeStruct((t, 512), BF16),
        compiler_params=_params(("parallel",)),
    )(qa, ka, ka, ka, va, va, va, bias)


def attn_a_bwd(qa, ka, va, bias, do):
    t = qa.shape[0]
    nb, blk = _a_specs(t)

    def body(q_ref, k2_ref, k1_ref, k0_ref, v2_ref, v1_ref, v0_ref, b_ref, do_ref, dq_ref, dk_ref, dv_ref, db_ref):
        qb = pl.program_id(0)

        @pl.when(qb == 0)
        def _():
            db_ref[...] = jnp.zeros_like(db_ref)

        lo = _lane((A_TQ, LANES)) < HD
        for p in range(4):
            sl = slice(LANES * p, LANES * (p + 1))
            kc = jnp.concatenate([k2_ref[:, sl], k1_ref[:, sl], k0_ref[:, sl]], axis=0)
            vc = jnp.concatenate([v2_ref[:, sl], v1_ref[:, sl], v0_ref[:, sl]], axis=0)
            dqs = []
            dkt = jnp.zeros((LANES, A_WIN), F32)
            dvt = jnp.zeros((LANES, A_WIN), F32)
            for h2 in range(2):
                head = 2 * p + h2
                hm, qm, e, inv = _a_exp(q_ref, kc, b_ref, head, sl, lo)
                pr = e * inv
                dom = jnp.where(hm, do_ref[:, sl], jnp.zeros((), BF16))
                dp = _dot_nt(dom, vc)
                ds = pr * (dp - jnp.sum(pr * dp, axis=-1, keepdims=True))
                db_ref[head] += ds
                dsb = ds.astype(BF16)
                dqs.append(_dot(dsb, kc))
                dkt = dkt + _dot_tn(qm, dsb)
                dvt = dvt + _dot_tn(dom, pr.astype(BF16))
            dq_ref[:, sl] = jnp.where(lo, dqs[0], dqs[1])
            dkc, dvc = dkt.T, dvt.T
            for s in range(3):
                rows = slice(A_TQ * (2 - s), A_TQ * (3 - s))
                dk_ref[s, :, sl] = dkc[rows]
                dv_ref[s, :, sl] = dvc[rows]

    share = pl.BlockSpec((3, A_TQ, 512), lambda i: (0, i, 0))
    return pl.pallas_call(
        body, name="attn_a_bwd", grid=(nb,),
        in_specs=[blk(0), blk(2), blk(1), blk(0), blk(2), blk(1), blk(0), _a_bias_spec(), blk(0)],
        out_specs=[pl.BlockSpec((A_TQ, 512), lambda i: (i, 0)), share, share,
                   pl.BlockSpec((HEADS, A_TQ, A_WIN), lambda i: (0, 0, 0))],
        out_shape=[jax.ShapeDtypeStruct((t, 512), F32), jax.ShapeDtypeStruct((3, t, 512), F32),
                   jax.ShapeDtypeStruct((3, t, 512), F32), jax.ShapeDtypeStruct((HEADS, A_TQ, A_WIN), F32)],
        compiler_params=_params(("arbitrary",)),
    )(qa, ka, ka, ka, va, va, va, bias, do)


B_T = 1024


B_SCALE2 = B_SCALE * 1.4426950408889634
B_FWD_HEADS = 8
_B_ALL = slice(0, B_T)
_B_LO, _B_HI = slice(0, B_T // 2), slice(B_T // 2, B_T)
_B_DIAG = ((_B_LO, _B_LO), (_B_LO, _B_HI), (_B_HI, _B_HI))


def _tri_tables(n, by_query):
    pairs = [(i, j) for i in range(n) for j in range(i + 1)] if by_query else [(i, j) for j in range(n) for i in range(j, n)]
    return (np.asarray([p[0] for p in pairs], np.int32), np.asarray([p[1] for p in pairs], np.int32))


def _b_mask_t(s):
    kc = lax.broadcasted_iota(jnp.int32, s.shape, 0) // CHUNK
    qc = lax.broadcasted_iota(jnp.int32, s.shape, 1) // CHUNK
    return jnp.where(kc <= qc, s, NEG)


def attn_b_fwd(qf, kf, vp):
    t = qf.shape[0]
    n = t // B_T
    qtab, ktab = _tri_tables(n, by_query=True)
    hps = B_FWD_HEADS

    def body(qt_ref, kt_ref, q_ref, k_ref, v_ref, o_ref, lse_ref, m_s, l_s, acc_s):
        qb, kb = qt_ref[pl.program_id(1)], kt_ref[pl.program_id(1)]

        @pl.when(kb == 0)
        def _():
            m_s[...] = jnp.full_like(m_s, NEG)
            l_s[...] = jnp.zeros_like(l_s)
            acc_s[...] = jnp.zeros_like(acc_s)

        def block(kr, qr, masked):
            for h2 in range(hps):
                sl = slice(LANES * h2, LANES * (h2 + 1))
                v = v_ref[kr, LANES * (h2 // 2):LANES * (h2 // 2 + 1)]
                s = _dot_nt(k_ref[kr, sl], q_ref[qr, sl])
                if masked:
                    s = _b_mask_t(s)
                m_prev = m_s[h2, :, qr]
                m_new = jnp.maximum(m_prev, jnp.max(s, axis=0, keepdims=True))
                alpha = jnp.exp2(m_prev - m_new)
                pr = jnp.exp2(s - m_new)
                l_s[h2, :, qr] = alpha * l_s[h2, :, qr] + jnp.sum(pr, axis=0, keepdims=True)
                acc_s[h2, :, qr] = alpha * acc_s[h2, :, qr] + _dot_tn(v, pr.astype(BF16))
                m_s[h2, :, qr] = m_new

        @pl.when(kb < qb)
        def _():
            block(_B_ALL, _B_ALL, False)

        @pl.when(kb == qb)
        def _():
            for kr, qr in _B_DIAG:
                block(kr, qr, kr == qr)
            for h2 in range(hps):
                l = l_s[h2]
                rows = slice(HD * (h2 % 2), HD * (h2 % 2 + 1))
                o_ref[HD * h2:HD * (h2 + 1), :] = (acc_s[h2, rows, :] * (1.0 / l)).astype(BF16)
                lse_ref[h2 // 2, h2 % 2:h2 % 2 + 1, :] = m_s[h2] + jnp.log2(l)

    grid_spec = pltpu.PrefetchScalarGridSpec(
        num_scalar_prefetch=2, grid=(HEADS // hps, len(qtab)),
        in_specs=[pl.BlockSpec((B_T, LANES * hps), lambda p, s, qt, kt: (qt[s], p)),
                  pl.BlockSpec((B_T, LANES * hps), lambda p, s, qt, kt: (kt[s], p)),
                  pl.BlockSpec((B_T, HD * hps), lambda p, s, qt, kt: (kt[s], p))],
        out_specs=[pl.BlockSpec((HD * hps, B_T), lambda p, s, qt, kt: (p, qt[s])),
                   pl.BlockSpec((hps // 2, 2, B_T), lambda p, s, qt, kt: (p, 0, qt[s]))],
        scratch_shapes=[pltpu.VMEM((hps, 1, B_T), F32), pltpu.VMEM((hps, 1, B_T), F32),
                        pltpu.VMEM((hps, LANES, B_T), F32)])
    return pl.pallas_call(
        body, name="attn_b_fwd", grid_spec=grid_spec,
        out_shape=[jax.ShapeDtypeStruct((512, t), BF16), jax.ShapeDtypeStruct((4, 2, t), F32)],
        compiler_params=_params(("parallel", "arbitrary")),
    )(jnp.asarray(qtab), jnp.asarray(ktab), qf, kf, vp)


def attn_b_bwd(qf, kf, vp, do, do_t, o_t, lse, scatter=()):
    t = qf.shape[0]
    n = t // B_T
    qtab, ktab = _tri_tables(n, by_query=False)
    plan = ScatterPlan(scatter)
    m = plan.n
    last = len(qtab) - 1

    def body(*refs):
        qt_ref, kt_ref, q_ref, k_ref, v_ref, do_ref, dot_ref, ot_ref, lse_ref = refs[:9]
        ins, (dq_ref, dk_ref, dv_ref), outs = refs[9:9 + m], refs[9 + m:12 + m], refs[12 + m:12 + 2 * m]
        sems = refs[12 + 2 * m:]
        qb, kb = qt_ref[pl.program_id(1)], kt_ref[pl.program_id(1)]
        if m:
            pl.when((pl.program_id(0) == 0) & (pl.program_id(1) == 0))(lambda: plan.start(ins, outs, sems))

        @pl.when(pl.program_id(1) == 0)
        def _():
            dq_ref[...] = jnp.zeros_like(dq_ref)

        @pl.when(qb == kb)
        def _():
            dk_ref[...] = jnp.zeros_like(dk_ref)
            dv_ref[...] = jnp.zeros_like(dv_ref)

        def block(kr, qr, masked):
            nq = qr.stop - qr.start
            cols = pl.ds(pl.multiple_of(qb * B_T + qr.start, LANES), nq)
            v = v_ref[kr, :]
            dov = do_ref[qr, :]
            prod = dot_ref[:, qr].astype(F32) * ot_ref[:, qr].astype(F32)
            lo = _lane((nq, LANES)) < HD
            for h2 in range(2):
                sl = slice(LANES * h2, LANES * (h2 + 1))
                hm = lo if h2 == 0 else ~lo
                q = q_ref[qr, sl]
                k = k_ref[kr, sl]
                dom = jnp.where(hm, dov, jnp.zeros((), BF16))
                delta = jnp.sum(prod[HD * h2:HD * (h2 + 1), :], axis=0, keepdims=True)
                s = _dot_nt(k, q)
                if masked:
                    s = _b_mask_t(s)
                pr = jnp.exp2(s - lse_ref[0, h2:h2 + 1, qr])
                dp = _dot_nt(v, dom)
                ds = (pr * (dp - delta)).astype(BF16)
                dk_ref[kr, sl] += _dot(ds, q) * (B_SCALE / B_SCALE2)
                dv_ref[kr, :] += _dot(pr.astype(BF16), dom)
                dq_ref[sl, cols] += _dot_tn(k, ds) * B_SCALE

        @pl.when(qb > kb)
        def _():
            block(_B_ALL, _B_ALL, False)

        @pl.when(qb == kb)
        def _():
            for kr, qr in _B_DIAG:
                block(kr, qr, kr == qr)

        if m:
            pl.when((pl.program_id(0) == 3) & (pl.program_id(1) == last))(lambda: plan.finish(ins, outs, sems))

    qrow = lambda w: pl.BlockSpec((B_T, w), lambda p, s, qt, kt: (qt[s], p))
    qcol = pl.BlockSpec((LANES, B_T), lambda p, s, qt, kt: (p, qt[s]))
    krow = lambda w: pl.BlockSpec((B_T, w), lambda p, s, qt, kt: (kt[s], p))
    grid_spec = pltpu.PrefetchScalarGridSpec(
        num_scalar_prefetch=2, grid=(4, len(qtab)),
        in_specs=[qrow(256), krow(256), krow(LANES), qrow(LANES), qcol, qcol,
                  pl.BlockSpec((1, 2, B_T), lambda p, s, qt, kt: (p, 0, qt[s]))] + [ANY] * m,
        out_specs=[pl.BlockSpec((256, t), lambda p, s, qt, kt: (p, 0)), krow(256), krow(LANES)] + [ANY] * m,
        scratch_shapes=plan.scratch if m else [])
    return pl.pallas_call(
        body, name="attn_b_bwd", grid_spec=grid_spec,
        out_shape=[jax.ShapeDtypeStruct((1024, t), F32), jax.ShapeDtypeStruct((t, 1024), F32),
                   jax.ShapeDtypeStruct((t, 512), F32)] + plan.out_shape,
        compiler_params=_params(("arbitrary", "arbitrary")),
    )(jnp.asarray(qtab), jnp.asarray(ktab), qf, kf, vp, do, do_t, o_t, lse, *scatter)


_U_LEN = A_TQ + A_WIN - 1


def _band_mask():
    a = np.arange(A_TQ)[:, None] // CHUNK
    b = np.arange(A_WIN)[None, :] // CHUNK
    return (b >= a) & (b <= a + A_LEFT)


def bias_block(table):
    h = table.shape[0]
    n_lo = A_WIN - 1 - 2 * A_TQ - A_MAX_REL
    ext = jnp.concatenate([jnp.repeat(table[:, :1], n_lo, axis=1), table,
                           jnp.repeat(table[:, -1:], _U_LEN - n_lo - table.shape[1], axis=1)], axis=1)
    row = jnp.pad(ext[:, ::-1], ((0, 0), (0, 1)))[:, None, :]
    band = _band_mask()
    first = [band & (np.arange(A_WIN)[None, :] >= 2 * A_TQ - A_TQ * v) for v in range(3)]
    keep = jnp.asarray(np.stack(first), jnp.int32)

    def body(r_ref, k_ref, o_ref):
        rows = jnp.broadcast_to(r_ref[0], (A_TQ, _U_LEN + 1))
        skew = pltpu.roll(rows, _U_LEN + 1 - (A_TQ - 1), 1, stride=1, stride_axis=0)
        toep = skew[:, :A_WIN]
        for v in range(3):
            o_ref[v, 0] = jnp.where(k_ref[v] != 0, toep, NEG)

    return pl.pallas_call(
        body, name="bias_block", grid=(h,),
        in_specs=[pl.BlockSpec((1, 1, _U_LEN + 1), lambda i: (i, 0, 0)),
                  pl.BlockSpec((3, A_TQ, A_WIN), lambda i: (0, 0, 0))],
        out_specs=pl.BlockSpec((3, 1, A_TQ, A_WIN), lambda i: (0, i, 0, 0)),
        out_shape=jax.ShapeDtypeStruct((3, h, A_TQ, A_WIN), F32),
        compiler_params=_params(("parallel",)),
    )(row, keep)


def bias_block_grad(db):
    h = db.shape[0]
    n_lo = A_WIN - 1 - 2 * A_TQ - A_MAX_REL
    skew = jnp.pad(db, ((0, 0), (0, 0), (A_TQ - 1, 0)))
    flat = jnp.pad(skew.reshape(h, A_TQ * _U_LEN), ((0, 0), (0, A_TQ)))
    ext = jnp.sum(flat.reshape(h, A_TQ, _U_LEN + 1), axis=1)[:, :_U_LEN][:, ::-1]
    n_tab = 2 * A_MAX_REL + 1
    first = jnp.sum(ext[:, :n_lo + 1], axis=1, keepdims=True)
    last = jnp.sum(ext[:, n_lo + n_tab - 1:], axis=1, keepdims=True)
    return jnp.concatenate([first, ext[:, n_lo + 1:n_lo + n_tab - 1], last], axis=1)


def rope_tabs(t):
    inv = 1.0 / (10000.0 ** (jnp.arange(0, ROPE, 2, dtype=F32) / ROPE))
    ang = jnp.arange(t, dtype=F32)[:, None] * inv[None, :]
    cos, sin = jnp.cos(ang), jnp.sin(ang)
    z = lambda w: jnp.zeros((t, w), F32)
    ck = jnp.concatenate([cos, cos, z(96)], axis=1)
    s1k = jnp.concatenate([-sin, z(112)], axis=1)
    s2k = jnp.concatenate([z(16), sin, z(96)], axis=1)
    cq = jnp.concatenate([jnp.ones((t, HD), F32), cos, cos, z(32)], axis=1)
    s1q = jnp.concatenate([z(HD), -sin, z(48)], axis=1)
    s2q = jnp.concatenate([z(HD + 16), sin, z(32)], axis=1)
    return jnp.stack([cq, s1q, s2q, ck, s1k, s2k])


def _pad_lanes(v, width):
    return jnp.pad(v, ((0, 0), (0, width - v.shape[1])))


LATE = ("w_in", "b_w_uq", "b_w_ukv", "w_out", "ffn2_w_gate", "ffn2_w_up", "ffn2_w_down")
FFN2 = ("ffn2_w_gate", "ffn2_w_up", "ffn2_w_down")


def kernel_layout(gathered):
    w = {n: v for n, v in gathered.items() if n.startswith("ffn")}
    if "w_in" in gathered:
        w["w_in"] = jnp.pad(gathered["w_in"].reshape(IN_COLS, D_MODEL), ((0, PROJ_W - IN_COLS), (0, 0)))
        uq = gathered["b_w_uq"].reshape(HEADS, HD + ROPE, 256)
        w["b_w_uq"] = jnp.pad(uq, ((0, 0), (0, LANES - HD - ROPE), (0, 0))).reshape(HEADS * LANES, 256)
        w["b_w_ukv"] = _shards_to_cols(gathered["b_w_ukv"])
        w["w_out"] = gathered["w_out"].reshape(N_SHARD * gathered["w_out"].shape[1], D_MODEL)
    return w


def local_step(x, target, w, late=None):
    t = x.shape[0]
    gq = jnp.tile(w["a_q_norm"], (1, HEADS))
    gk = jnp.tile(w["a_k_norm"], (1, HEADS))
    gq128 = _pad_lanes(jnp.concatenate([w["b_q_nope_norm"], w["b_q_rope_norm"]], axis=1), LANES)
    gk128 = _pad_lanes(w["b_k_nope_norm"], LANES)
    gkr128 = _pad_lanes(w["b_k_rope_norm"], LANES)
    tabs = rope_tabs(t)
    bias = bias_block(w["a_rel_bias"])

    if late is None:
        x1, gate1, up1 = ffn_fwd(x, w["ffn1_norm"], w["ffn1_w_gate"], w["ffn1_w_up"], w["ffn1_w_down"], "ffn_fwd")
    else:
        own, shard = late
        x1, gate1, up1, *got = ffn_fwd(x, w["ffn1_norm"], w["ffn1_w_gate"], w["ffn1_w_up"], w["ffn1_w_down"],
                                       "ffn_fwd_gather", gather=own)
        w = dict(w, **kernel_layout({n: lax.dynamic_update_index_in_dim(g_, o_, shard, 0)
                                     for n, g_, o_ in zip(LATE, got, own)}))
    h, proj, qa, ka, va, cqn, ckvn = mix_fwd(x1, w["mix_norm"], w["w_in"], gq, gk,
                                             w["b_q_lat_norm"], w["b_kv_lat_norm"])
    qf, kf, vp = prep2_fwd(cqn, ckvn, proj, w["b_w_uq"], w["b_w_ukv"], gq128, gk128, gkr128, tabs)
    oa = attn_a_fwd(qa, ka, va, bias)
    ob_t, lse = attn_b_fwd(qf, kf, vp)
    g = {}
    dx3, gate2, up2, g["final_norm"], loss, x2 = ffn_fwd(
        x1, w["ffn2_norm"], w["ffn2_w_gate"], w["ffn2_w_up"], w["ffn2_w_down"], "ffn_fwd_loss",
        loss_head=(w["final_norm"], target), pre_proj=(oa, ob_t, w["w_out"]))
    g["ffn2_w_gate"], g["ffn2_w_up"], g["ffn2_w_down"], dhp = ffn_bwd(
        x2, dx3, w["ffn2_norm"], w["ffn2_w_gate"], w["ffn2_w_up"], w["ffn2_w_down"], gate2, up2, "ffn_bwd")
    dx2, g["ffn2_norm"], d_oa, d_ob, d_ob_t, g["w_out"] = out_proj_bwd(
        x2, w["ffn2_norm"], dhp, dx3, w["w_out"], oa, ob_t)
    early = [g[n] for n in FFN2] + [g["w_out"].reshape(N_SHARD, -1, D_MODEL)]
    dqf, dkf, dvp, *landed_early = attn_b_bwd(qf, kf, vp, d_ob, d_ob_t, ob_t, lse,
                                              scatter=() if late is None else early)
    dqa, dkp, dvpa, dbias = attn_a_bwd(qa, ka, va, bias, d_oa)
    dcq, dckv, dkr, dwuq, dwukv, dgq128, dgk128, dgkr128 = prep2_bwd(
        cqn, ckvn, proj, w["b_w_uq"], w["b_w_ukv"], dqf, dkf, dvp, gq128, gk128, gkr128, tabs)
    g["b_w_uq"], g["b_w_ukv"] = dwuq.astype(BF16), dwukv.astype(BF16)
    dx1, g["w_in"], g["mix_norm"], dgq, dgk, g["b_q_lat_norm"], g["b_kv_lat_norm"] = mix_bwd(
        proj, x1, h, dx2, w["w_in"], w["mix_norm"], dqa, dkp, dvpa, dcq, dckv, dkr, gq, gk,
        w["b_q_lat_norm"], w["b_kv_lat_norm"])
    mid = [g["w_in"][:IN_COLS].reshape(N_SHARD, IN_COLS // N_SHARD, D_MODEL),
           g["b_w_uq"].reshape(HEADS, LANES, 256)[:, :HD + ROPE].reshape(N_SHARD, -1, 256),
           _cols_to_shards(g["b_w_ukv"])]
    g["ffn1_w_gate"], g["ffn1_w_up"], g["ffn1_w_down"], dhp, *landed_late = ffn_bwd(
        x, dx1, w["ffn1_norm"], w["ffn1_w_gate"], w["ffn1_w_up"], w["ffn1_w_down"], gate1, up1,
        "ffn_bwd" if late is None else "ffn_bwd_scatter", scatter=() if late is None else mid,
        spread=None if late is None else late[1])
    grad_x, g["ffn1_norm"] = norm_bwd(x, w["ffn1_norm"], dhp, dx1, "ffn_norm_bwd")
    landed = dict(zip(FFN2 + ("w_out", "w_in", "b_w_uq", "b_w_ukv", "ffn1_w_gate", "ffn1_w_up", "ffn1_w_down"),
                      landed_early + landed_late))

    g["a_q_norm"] = jnp.sum(dgq.reshape(HEADS, HD), axis=0, keepdims=True)
    g["a_k_norm"] = jnp.sum(dgk.reshape(HEADS, HD), axis=0, keepdims=True)
    g["a_rel_bias"] = bias_block_grad(dbias)
    g["b_q_nope_norm"] = dgq128[:, :HD]
    g["b_q_rope_norm"] = dgq128[:, HD:HD + ROPE]
    g["b_k_nope_norm"] = dgk128[:, :HD]
    g["b_k_rope_norm"] = dgkr128[:, :ROPE]
    return loss, grad_x, g, landed


ANY = pl.BlockSpec(memory_space=pl.ANY)
N_DEV = 8


def _place():
    return lax.axis_index("x"), lax.axis_index("y"), lax.axis_index("c")


def _flip(v, bit):
    return 1 - v if bit else v


BF16_ROWS = 16


def _split_axis(shape):
    return 0 if (shape[0] // 2) % BF16_ROWS == 0 else 1


def _half_shape(shape):
    axis = _split_axis(shape)
    return tuple(s // 2 if a == axis else s for a, s in enumerate(shape))


def _half(shape, core):
    axis = _split_axis(shape)
    size = shape[axis] // 2
    return tuple(pl.ds(core * size, size) if a == axis else slice(None) for a in range(2))


class GatherPlan:
    def __init__(self, ws):
        self.shapes = [w.shape for w in ws]
        self.n = len(ws)
        self.out_shape = [jax.ShapeDtypeStruct((N_SHARD,) + w.shape, w.dtype) for w in ws]
        self.scratch = [pltpu.SemaphoreType.DMA((6 * self.n,)), pltpu.SemaphoreType.DMA((6 * self.n,))]

    def _copies(self, ins, outs, sems):
        x, y, c = _place()
        s_me = 2 * x + y
        sibling = (x, y, 1 - c)
        send_sems, recv_sems = sems

        def remote(k, src, dst, to):
            return pltpu.make_async_remote_copy(src_ref=src, dst_ref=dst, send_sem=send_sems.at[k],
                                                recv_sem=recv_sems.at[k], device_id=to, device_id_type=MESH)

        ici, fwd = [], []
        for a in range(self.n):
            mine, theirs = _half(self.shapes[a], c), _half(self.shapes[a], 1 - c)
            for j, (cx, cy) in enumerate([(1 - x, y), (x, 1 - y), (1 - x, 1 - y)]):
                got = outs[a].at[(2 * cx + cy,) + mine]
                ici.append((remote(6 * a + j, ins[a].at[mine], outs[a].at[(s_me,) + mine], (cx, cy, c)),
                            remote(6 * a + j, got, got, (cx, cy, c))))
                passed = outs[a].at[(2 * cx + cy,) + theirs]
                fwd.append((remote(6 * a + 3 + j, got, got, sibling), remote(6 * a + 3 + j, passed, passed, sibling)))
        return ici, fwd

    def start(self, ins, outs, sems):
        for send, _ in self._copies(ins, outs, sems)[0]:
            send.start()

    def forward(self, ins, outs, sems):
        ici, fwd = self._copies(ins, outs, sems)
        for (_, arrival), (send, _) in zip(ici, fwd):
            arrival.wait_recv()
            send.start()

    def finish(self, ins, outs, sems):
        ici, fwd = self._copies(ins, outs, sems)
        for _, arrival in fwd:
            arrival.wait_recv()
        for send, _ in ici + fwd:
            send.wait_send()


def allgather_shards(ws):
    plan = GatherPlan(ws)
    n = plan.n

    def body(*refs):
        ins, outs, sems = refs[:n], refs[n:2 * n], refs[2 * n:]
        plan.start(ins, outs, sems)
        plan.forward(ins, outs, sems)
        plan.finish(ins, outs, sems)

    return pl.pallas_call(
        body, name="allgather_shards", in_specs=[ANY] * n, out_specs=[ANY] * n,
        out_shape=plan.out_shape, scratch_shapes=plan.scratch,
    )(*ws)


class ScatterPlan:
    def __init__(self, gs):
        self.shapes = [g.shape[1:] for g in gs]
        self.n = len(gs)
        self.out_shape = [jax.ShapeDtypeStruct((N_DEV,) + _half_shape(g.shape[1:]), g.dtype) for g in gs]
        self.scratch = [pltpu.SemaphoreType.DMA((7 * self.n,)), pltpu.SemaphoreType.DMA((7 * self.n,)),
                        pltpu.SemaphoreType.DMA((self.n,))]

    def _copies(self, ins, outs, sems):
        x, y, c = _place()
        me = 4 * x + 2 * y + c
        send_sems, recv_sems, local_sems = sems
        local, sends, arrivals = [], [], []
        for a in range(self.n):
            piece = lambda px, py, pc, a=a: ins[a].at[(2 * px + py,) + _half(self.shapes[a], pc)]
            local.append(pltpu.make_async_copy(piece(x, y, c), outs[a].at[me], local_sems.at[a]))
            for k in range(1, N_DEV):
                px, py, pc = _flip(x, k & 4), _flip(y, k & 2), _flip(c, k & 1)
                sem = dict(send_sem=send_sems.at[7 * a + k - 1], recv_sem=recv_sems.at[7 * a + k - 1],
                           device_id=(px, py, pc), device_id_type=MESH)
                sends.append(pltpu.make_async_remote_copy(
                    src_ref=piece(px, py, pc), dst_ref=outs[a].at[me], **sem))
                slot = outs[a].at[4 * px + 2 * py + pc]
                arrivals.append(pltpu.make_async_remote_copy(src_ref=slot, dst_ref=slot, **sem))
        return local, sends, arrivals

    def start(self, ins, outs, sems):
        local, sends, _ = self._copies(ins, outs, sems)
        for cp in local + sends:
            cp.start()

    def finish(self, ins, outs, sems):
        local, sends, arrivals = self._copies(ins, outs, sems)
        for cp in arrivals:
            cp.wait_recv()
        for cp in sends:
            cp.wait_send()
        for cp in local:
            cp.wait()


def _row_tile(rows, row_bytes, budget, multiple):
    fits = [r for r in range(multiple, rows + 1, multiple) if rows % r == 0 and r * row_bytes <= budget]
    return max(fits) if fits else rows


SMALL_KERNEL_VMEM = 32 * 1024 * 1024


def sum_slots(lands, name):
    k = len(lands)
    _, rows, cols = lands[0].shape
    tr = _row_tile(rows, k * 2 * (N_DEV * cols * 2 + cols * 4), SMALL_KERNEL_VMEM, BF16_ROWS)

    def body(*refs):
        for l_ref, o_ref in zip(refs[:k], refs[k:]):
            acc = l_ref[0].astype(F32)
            for s in range(1, N_DEV):
                acc = acc + l_ref[s].astype(F32)
            o_ref[...] = acc

    return pl.pallas_call(
        body, name=name, grid=(rows // tr,),
        in_specs=[pl.BlockSpec((N_DEV, tr, cols), lambda i: (0, i, 0))] * k,
        out_specs=[pl.BlockSpec((tr, cols), lambda i: (i, 0))] * k,
        out_shape=[jax.ShapeDtypeStruct((rows, cols), F32)] * k,
        compiler_params=_params(("parallel",)),
    )(*lands)


def join_halves(hs, shapes):
    n = len(hs)

    def body(*refs):
        ins, outs = refs[:n], refs[n:2 * n]
        send_sems, recv_sems = refs[2 * n:]
        x, y, c = _place()
        sends = []
        for a in range(n):
            mine = outs[a].at[_half(shapes[a], c)]
            sends.append(pltpu.make_async_remote_copy(
                src_ref=ins[a], dst_ref=mine, send_sem=send_sems.at[a], recv_sem=recv_sems.at[a],
                device_id=(x, y, 1 - c), device_id_type=MESH))
            sends[-1].start()
        for a in range(n):
            theirs = outs[a].at[_half(shapes[a], 1 - c)]
            pltpu.make_async_remote_copy(
                src_ref=theirs, dst_ref=theirs, send_sem=send_sems.at[a], recv_sem=recv_sems.at[a],
                device_id=(x, y, 1 - c), device_id_type=MESH).wait_recv()
        for cp in sends:
            cp.wait_send()

    return pl.pallas_call(
        body, name="join_halves",
        in_specs=[ANY] * n, out_specs=[ANY] * n,
        out_shape=[jax.ShapeDtypeStruct(tuple(s), h.dtype) for s, h in zip(shapes, hs)],
        scratch_shapes=[pltpu.SemaphoreType.DMA((n,)), pltpu.SemaphoreType.DMA((n,))],
    )(*hs)


def allreduce_small(vec):
    def body(v_ref, o_ref, land_ref, send_sems, recv_sems):
        x, y, c = _place()
        me = 4 * x + 2 * y + c
        land_ref[me] = v_ref[...]
        sends = []
        for k in range(1, N_DEV):
            px, py, pc = _flip(x, k & 4), _flip(y, k & 2), _flip(c, k & 1)
            sends.append(pltpu.make_async_remote_copy(
                src_ref=v_ref, dst_ref=land_ref.at[me], send_sem=send_sems.at[k - 1], recv_sem=recv_sems.at[k - 1],
                device_id=(px, py, pc), device_id_type=MESH))
            sends[-1].start()
        for k in range(1, N_DEV):
            px, py, pc = _flip(x, k & 4), _flip(y, k & 2), _flip(c, k & 1)
            slot = land_ref.at[4 * px + 2 * py + pc]
            pltpu.make_async_remote_copy(
                src_ref=slot, dst_ref=slot, send_sem=send_sems.at[k - 1], recv_sem=recv_sems.at[k - 1],
                device_id=(px, py, pc), device_id_type=MESH).wait_recv()
        for cp in sends:
            cp.wait_send()
        acc = land_ref[0]
        for s in range(1, N_DEV):
            acc = acc + land_ref[s]
        o_ref[...] = acc

    vm = pl.BlockSpec(memory_space=pltpu.VMEM)
    return pl.pallas_call(
        body, name="allreduce_small",
        in_specs=[vm], out_specs=vm,
        out_shape=jax.ShapeDtypeStruct(vec.shape, F32),
        scratch_shapes=[pltpu.VMEM((N_DEV,) + vec.shape, F32), pltpu.SemaphoreType.DMA((N_DEV - 1,)),
                        pltpu.SemaphoreType.DMA((N_DEV - 1,))],
    )(vec)


def adamw(ws, gs, ms, vs, name):
    k = len(ws)
    rows, cols = ws[0].shape
    tr = _row_tile(rows, k * 2 * 7 * cols * 4, SMALL_KERNEL_VMEM, 8)
    c1 = 1.0 - ADAM_B1 ** ADAM_STEP
    c2 = 1.0 - ADAM_B2 ** ADAM_STEP

    def body(*refs):
        for a in range(k):
            w_ref, g_ref, m_ref, v_ref = (refs[s * k + a] for s in range(4))
            d_ref, nm_ref, nv_ref = (refs[(4 + s) * k + a] for s in range(3))
            gv = g_ref[...]
            nm = ADAM_B1 * m_ref[...] + (1.0 - ADAM_B1) * gv
            nv = ADAM_B2 * v_ref[...] + (1.0 - ADAM_B2) * (gv * gv)
            nm_ref[...] = nm
            nv_ref[...] = nv
            d_ref[...] = -ADAM_LR * ((nm / c1) / (jnp.sqrt(nv / c2) + ADAM_EPS) + ADAM_WD * w_ref[...])

    blk = pl.BlockSpec((tr, cols), lambda i: (i, 0))
    out = pl.pallas_call(
        body, name=name, grid=(rows // tr,),
        in_specs=[blk] * (4 * k), out_specs=[blk] * (3 * k),
        out_shape=[jax.ShapeDtypeStruct((rows, cols), F32)] * (3 * k),
        compiler_params=_params(("parallel",)),
    )(*ws, *gs, *ms, *vs)
    return [(out[a], out[k + a], out[2 * k + a]) for a in range(k)]


BIG = ("ffn1_w_gate", "ffn1_w_up", "ffn1_w_down", "w_in", "b_w_uq", "b_w_ukv", "w_out",
       "ffn2_w_gate", "ffn2_w_up", "ffn2_w_down")
SMALL = ("ffn1_norm", "mix_norm", "a_q_norm", "a_k_norm", "a_rel_bias", "b_q_lat_norm", "b_kv_lat_norm",
         "b_q_nope_norm", "b_q_rope_norm", "b_k_nope_norm", "b_k_rope_norm", "ffn2_norm", "final_norm")
WEIGHTS = ("ffn1_norm", "ffn1_w_gate", "ffn1_w_up", "ffn1_w_down", "mix_norm", "w_in", "a_q_norm", "a_k_norm",
           "a_rel_bias", "b_q_lat_norm", "b_w_uq", "b_kv_lat_norm", "b_w_ukv", "b_q_nope_norm", "b_q_rope_norm",
           "b_k_nope_norm", "b_k_rope_norm", "w_out", "ffn2_norm", "ffn2_w_gate", "ffn2_w_up", "ffn2_w_down",
           "final_norm")
TRANSPOSED = ("ffn1_w_gate", "ffn1_w_up", "ffn2_w_gate", "ffn2_w_up", "w_in", "b_w_uq")
PACK_SHAPE = (8, 1024)


def _pack_small(d, last=None):
    flat = [d[n].reshape(-1) for n in SMALL]
    used = sum(f.shape[0] for f in flat)
    total = PACK_SHAPE[0] * PACK_SHAPE[1]
    tail = jnp.zeros((total - used - 1,), F32)
    end = jnp.zeros((1,), F32) if last is None else last.reshape(1)
    return jnp.concatenate(flat + [tail, end]).reshape(PACK_SHAPE)


def _unpack_small(p, like):
    flat = p.reshape(-1)
    out, off = {}, 0
    for n in SMALL:
        size = like[n].size
        out[n] = flat[off:off + size].reshape(like[n].shape)
        off += size
    return out, flat[-1]


def _cols_to_shards(g):
    rows, cols = g.shape
    return g.reshape(rows, N_SHARD, cols // N_SHARD).transpose(1, 0, 2)


def _shards_to_cols(g):
    return g.transpose(1, 0, 2).reshape(g.shape[1], -1)


def kernel(x, ffn1_norm, ffn1_w_gate, ffn1_w_up, ffn1_w_down, mix_norm, w_in, a_q_norm, a_k_norm, a_rel_bias, b_q_lat_norm, b_w_uq, b_kv_lat_norm, b_w_ukv, b_q_nope_norm, b_q_rope_norm, b_k_nope_norm, b_k_rope_norm, w_out, ffn2_norm, ffn2_w_gate, ffn2_w_up, ffn2_w_down, final_norm, loss_target, m_ffn1_norm, m_ffn1_w_gate, m_ffn1_w_up, m_ffn1_w_down, m_mix_norm, m_w_in, m_a_q_norm, m_a_k_norm, m_a_rel_bias, m_b_q_lat_norm, m_b_w_uq, m_b_kv_lat_norm, m_b_w_ukv, m_b_q_nope_norm, m_b_q_rope_norm, m_b_k_nope_norm, m_b_k_rope_norm, m_w_out, m_ffn2_norm, m_ffn2_w_gate, m_ffn2_w_up, m_ffn2_w_down, m_final_norm, v_ffn1_norm, v_ffn1_w_gate, v_ffn1_w_up, v_ffn1_w_down, v_mix_norm, v_w_in, v_a_q_norm, v_a_k_norm, v_a_rel_bias, v_b_q_lat_norm, v_b_w_uq, v_b_kv_lat_norm, v_b_w_ukv, v_b_q_nope_norm, v_b_q_rope_norm, v_b_k_nope_norm, v_b_k_rope_norm, v_w_out, v_ffn2_norm, v_ffn2_w_gate, v_ffn2_w_up, v_ffn2_w_down, v_final_norm):
    args = locals()
    view = lambda a, n: a[0].T if n in TRANSPOSED else a[0]
    wts = {n: view(args[n], n) for n in WEIGHTS}
    mom = {n: view(args["m_" + n], n) for n in WEIGHTS}
    var = {n: view(args["v_" + n], n) for n in WEIGHTS}

    shard = 2 * lax.axis_index("x") + lax.axis_index("y")
    core = lax.axis_index("c")
    first = [n for n in BIG if n not in LATE]
    own = [wts[n].astype(BF16) for n in first]
    w = {n: wts[n] if n == "a_rel_bias" else wts[n][None] for n in SMALL}
    w.update(kernel_layout({n: lax.dynamic_update_index_in_dim(got, mine, shard, 0)
                            for n, got, mine in zip(first, allgather_shards(own), own)}))

    loss, grad_x, g, landed = local_step(x[0], loss_target[0], w,
                                         late=([wts[n].astype(BF16) for n in LATE], shard))

    me = 2 * shard + core
    for n in first:
        piece = lax.dynamic_slice(g[n], (shard, core * (FS // 2), 0), (1, FS // 2, D_MODEL))
        landed[n] = lax.dynamic_update_slice(landed[n], piece, (me, 0, 0))
    groups = {}
    for n in BIG:
        groups.setdefault(wts[n].shape, []).append(n)
    half = {}
    for names in groups.values():
        half.update(zip(names, sum_slots([landed[n] for n in names], "sum_slots")))
    halves = [half[n] for n in BIG]
    shapes = [wts[n].shape for n in BIG]
    axes = [_split_axis(s) for s in shapes]
    grads = dict(zip(BIG, (lax.dynamic_update_slice_in_dim(got, mine, core * mine.shape[ax], ax)
                           for got, mine, ax in zip(join_halves(halves, shapes), halves, axes))))

    small_sum, loss_sum = _unpack_small(allreduce_small(_pack_small(g, loss[0, 0])), wts)
    grads.update(small_sum)

    delta, new_m, new_v = {}, {}, {}
    for names in groups.values():
        stepped = adamw(*([d[n] for n in names] for d in (wts, grads, mom, var)), "adamw")
        for n, (d_new, m_new, v_new) in zip(names, stepped):
            delta[n], new_m[n], new_v[n] = d_new, m_new, v_new
    (packed,) = adamw([_pack_small(wts)], [_pack_small(grads)], [_pack_small(mom)], [_pack_small(var)], "adamw_small")
    for dst, p in zip((delta, new_m, new_v), packed):
        dst.update(_unpack_small(p, wts)[0])

    lead = lambda d: [(d[n].T if n in TRANSPOSED else d[n])[None] for n in WEIGHTS]
    return (loss_sum, grad_x[None], *lead(grads), *lead(delta), *lead(new_m), *lead(new_v))
```

```python
import numpy as np
import jax
import jax.numpy as jnp
from jax import lax
from jax.experimental import pallas as pl
from jax.experimental.pallas import tpu as pltpu

F32 = jnp.float32
BF16 = jnp.bfloat16
EPS = 1e-6
NEG = -1e30

D_MODEL = 1024
D_FF = 2816
N_SHARD = 4
FS = D_FF // N_SHARD
CHUNK = 64
A_LEFT = 8
A_MAX_REL = 128
HEADS = 8
HD = 64
ROPE = 32
PROJ_W = 2048
IN_COLS = 1952
B_SCALE = 96 ** -0.5
LANES = 128

ADAM_LR = 0.001
ADAM_B1 = 0.9
ADAM_B2 = 0.999
ADAM_EPS = 1e-08
ADAM_WD = 0.01
ADAM_STEP = 10

VMEM_LIMIT = 56 * 1024 * 1024
TOKEN_TILE = 512

MESH = pl.DeviceIdType.MESH


def _dot(a, b):
    return lax.dot_general(a, b, (((1,), (0,)), ((), ())), preferred_element_type=F32)


def _dot_nt(a, b):
    return lax.dot_general(a, b, (((1,), (1,)), ((), ())), preferred_element_type=F32)


def _dot_tn(a, b):
    return lax.dot_general(a, b, (((0,), (0,)), ((), ())), preferred_element_type=F32)


def _params(sem):
    return pltpu.CompilerParams(dimension_semantics=sem, vmem_limit_bytes=VMEM_LIMIT)


def _rms(xv):
    r = lax.rsqrt(jnp.mean(xv * xv, axis=-1, keepdims=True) + EPS)
    return r, xv * r


def ffn_fwd(x, g, wg, wu, wd, name, gather=(), loss_head=None, pre_proj=None):
    t, d = x.shape
    tm = TOKEN_TILE
    ni = t // tm
    plan = GatherPlan(gather)
    n = plan.n
    head = () if loss_head is None else tuple(loss_head)
    q = len(head)
    proj = () if pre_proj is None else tuple(pre_proj)
    r = len(proj)
    half = proj[0].shape[1] if r else 0

    def body(*refs):
        it = iter(refs)
        take = lambda count: [next(it) for _ in range(count)]
        x_ref, g_ref, wg_ref, wu_ref, wd_ref = take(5)
        head_in, proj_in, ins = take(q), take(r), take(n)
        o_ref, gp_ref, up_ref = take(3)
        head_out, x_out, outs = take(q), take(1 if r else 0), take(n)
        h_ref, acc_ref = take(2)
        sems = list(it)
        i, j = pl.program_id(0), pl.program_id(1)
        if n:
            pl.when((i == 0) & (j == 0))(lambda: plan.start(ins, outs, sems))
            pl.when((i == (3 * ni) // 4) & (j == 0))(lambda: plan.forward(ins, outs, sems))

        @pl.when(j == 0)
        def _():
            xin = x_ref[...]
            if r:
                oa_ref, obt_ref, wo_ref = proj_in
                xin = xin + _dot(oa_ref[...], wo_ref[0:half, :]) + _dot_tn(obt_ref[...], wo_ref[half:2 * half, :])
                x_out[0][...] = xin
            _, xn = _rms(xin)
            h_ref[...] = (xn * g_ref[...]).astype(BF16)
            acc_ref[...] = jnp.zeros_like(acc_ref)

        h = h_ref[...]
        gp = _dot_nt(h, wg_ref[0])
        up = _dot_nt(h, wu_ref[0])
        gp_ref[0] = gp
        up_ref[0] = up
        a = (gp * jax.nn.sigmoid(gp) * up).astype(BF16)
        acc_ref[...] += _dot(a, wd_ref[0])

        @pl.when(j == N_SHARD - 1)
        def _():
            y = (x_out[0][...] if r else x_ref[...]) + 0.5 * acc_ref[...]
            if not q:
                o_ref[...] = y
                return
            (gf_ref, t_ref), (dgf_ref, loss_ref) = head_in, head_out
            rinv, yn = _rms(y)
            gf = gf_ref[...]
            e = yn * gf - t_ref[...]
            dout = e * (1.0 / d)
            dng = dout * gf
            o_ref[...] = rinv * (dng - yn * jnp.mean(dng * yn, axis=-1, keepdims=True))

            @pl.when(i == 0)
            def _():
                dgf_ref[...] = jnp.zeros_like(dgf_ref)
                loss_ref[...] = jnp.zeros_like(loss_ref)

            dgf_ref[...] += jnp.sum(dout * yn, axis=0, keepdims=True)
            part = jnp.sum(jnp.sum(e * e, axis=-1, keepdims=True), axis=0, keepdims=True) * (0.5 / d)
            loss_ref[...] += jnp.broadcast_to(part, loss_ref.shape)

        if n:
            pl.when((i == ni - 1) & (j == N_SHARD - 1))(lambda: plan.finish(ins, outs, sems))

    tok = pl.BlockSpec((tm, d), lambda i, j: (i, 0))
    vec = pl.BlockSpec((1, d), lambda i, j: (0, 0))
    chunk = pl.BlockSpec((1, FS, d), lambda i, j: (j, 0, 0))
    pre = pl.BlockSpec((1, tm, FS), lambda i, j: (j, i, 0))
    return pl.pallas_call(
        body, name=name, grid=(ni, N_SHARD),
        in_specs=[tok, vec, chunk, chunk, chunk] + [vec, tok][:q]
        + [pl.BlockSpec((tm, half), lambda i, j: (i, 0)), pl.BlockSpec((half, tm), lambda i, j: (0, i)),
           pl.BlockSpec((2 * half, d), lambda i, j: (0, 0))][:r] + [ANY] * n,
        out_specs=[tok, pre, pre] + [vec, pl.BlockSpec((1, LANES), lambda i, j: (0, 0))][:q] + [tok][:r] + [ANY] * n,
        out_shape=[jax.ShapeDtypeStruct((t, d), F32), jax.ShapeDtypeStruct((N_SHARD, t, FS), F32),
                   jax.ShapeDtypeStruct((N_SHARD, t, FS), F32)]
        + [jax.ShapeDtypeStruct((1, d), F32), jax.ShapeDtypeStruct((1, LANES), F32)][:q]
        + [jax.ShapeDtypeStruct((t, d), F32)][:r] + plan.out_shape,
        scratch_shapes=[pltpu.VMEM((tm, d), BF16), pltpu.VMEM((tm, d), F32)] + (plan.scratch if n else []),
        compiler_params=_params(("arbitrary", "arbitrary")),
    )(x, g, wg, wu, wd, *head, *proj, *gather)


def ffn_bwd(x, dout, g, wg, wu, wd, gate, up_pre, name, scatter=(), spread=None):
    t, d = x.shape
    tm = TOKEN_TILE
    ni = t // tm
    hf = FS // 2
    plan = ScatterPlan(scatter)
    m = plan.n
    k = 0 if spread is None else 3
    steps = jnp.arange(N_SHARD, dtype=jnp.int32)
    order = steps if spread is None else (spread + 1 + steps) % N_SHARD

    def body(*refs):
        ord_ref, x_ref, do_ref, g_ref, wg_ref, wu_ref, wd_ref, gp_ref, up_ref = refs[:9]
        ins, (dwg_out, dwu_out, dwd_out, dhp_ref) = refs[9:9 + m], refs[9 + m:13 + m]
        outs, lands = refs[13 + m:13 + 2 * m], refs[13 + 2 * m:13 + 2 * m + k]
        dwg_ref, dwu_ref, dwd_ref = refs[13 + 2 * m + k:16 + 2 * m + k]
        sems = refs[16 + 2 * m + k:19 + 2 * m + k] if m else ()
        stage_ref = refs[-3] if k else None
        j, i = pl.program_id(0), pl.program_id(1)
        if m:
            pl.when((j == 0) & (i == 0))(lambda: plan.start(ins, outs, sems))

        def chunk_copies(jj):
            send_sems, recv_sems = refs[-2:]
            px, py, pc = _place()
            me = 4 * px + 2 * py + pc
            tx, ty = ord_ref[jj] // 2, ord_ref[jj] % 2
            copies = []
            for n_ in range(3):
                for h_ in range(2):
                    copies.append((pltpu.make_async_remote_copy(
                        src_ref=stage_ref.at[n_, pl.ds(h_ * hf, hf)], dst_ref=lands[n_].at[me],
                        send_sem=send_sems.at[6 * jj + 2 * n_ + h_], recv_sem=recv_sems.at[3 * me + n_],
                        device_id=(tx, ty, h_), device_id_type=MESH), (tx != px) | (ty != py) | (pc != h_)))
            return copies

        def arrivals():
            send_sems, recv_sems = refs[-2:]
            px, py, pc = _place()
            me = 4 * px + 2 * py + pc
            for s_ in range(N_DEV):
                for n_ in range(3):
                    slot = lands[n_].at[s_]
                    cp = pltpu.make_async_remote_copy(
                        src_ref=slot, dst_ref=slot, send_sem=send_sems.at[0], recv_sem=recv_sems.at[3 * s_ + n_],
                        device_id=(px, py, pc), device_id_type=MESH)
                    pl.when(me != s_)(cp.wait_recv)

        _, xn = _rms(x_ref[...])
        h = (xn * g_ref[...]).astype(BF16)
        dz = (0.5 * do_ref[...]).astype(BF16)
        wgv, wuv, wdv = wg_ref[0], wu_ref[0], wd_ref[0]
        gp, up = gp_ref[0], up_ref[0]
        s = jax.nn.sigmoid(gp)
        sg = gp * s
        a = (sg * up).astype(BF16)
        da = _dot_nt(dz, wdv)
        dup = (da * sg).astype(BF16)
        dgp = (da * up * (s * (1.0 + gp * (1.0 - s)))).astype(BF16)

        @pl.when(i == 0)
        def _():
            dwg_ref[...] = jnp.zeros_like(dwg_ref)
            dwu_ref[...] = jnp.zeros_like(dwu_ref)
            dwd_ref[...] = jnp.zeros_like(dwd_ref)

        dwd_ref[...] += _dot_tn(a, dz)
        dwg_ref[...] += _dot_tn(dgp, h)
        dwu_ref[...] += _dot_tn(dup, h)
        dhp_ref[0] = (_dot(dgp, wgv) + _dot(dup, wuv)).astype(BF16)

        @pl.when(i == ni - 1)
        def _():
            dwg_out[0] = dwg_ref[...].astype(BF16)
            dwu_out[0] = dwu_ref[...].astype(BF16)
            dwd_out[0] = dwd_ref[...].astype(BF16)
            if k:
                @pl.when(j >= 1)
                def _():
                    for cp, leaves in chunk_copies(j - 1):
                        pl.when(leaves)(cp.wait_send)
                for n_, acc in enumerate((dwg_ref, dwu_ref, dwd_ref)):
                    stage_ref[n_] = acc[...].astype(BF16)
                for cp, leaves in chunk_copies(j):
                    pl.when(leaves)(cp.start)

                @pl.when(j == N_SHARD - 1)
                def _():
                    for cp, leaves in chunk_copies(N_SHARD - 1):
                        pl.when(leaves)(cp.wait_send)
                    arrivals()

        if m:
            pl.when((j == N_SHARD - 1) & (i == ni - 1))(lambda: plan.finish(ins, outs, sems))

    chunk = pl.BlockSpec((1, FS, d), lambda j, i, o: (o[j], 0, 0))
    tok = pl.BlockSpec((tm, d), lambda j, i, o: (i, 0))
    pre = pl.BlockSpec((1, tm, FS), lambda j, i, o: (o[j], i, 0))
    grid_spec = pltpu.PrefetchScalarGridSpec(
        num_scalar_prefetch=1, grid=(N_SHARD, ni),
        in_specs=[tok, tok, pl.BlockSpec((1, d), lambda j, i, o: (0, 0)), chunk, chunk, chunk, pre, pre] + [ANY] * m,
        out_specs=[chunk, chunk, chunk, pl.BlockSpec((1, tm, d), lambda j, i, o: (o[j], i, 0))] + [ANY] * (m + k),
        scratch_shapes=[pltpu.VMEM((FS, d), F32), pltpu.VMEM((FS, d), F32), pltpu.VMEM((FS, d), F32)]
        + (plan.scratch if m else [])
        + ([pltpu.VMEM((3, FS, d), BF16), pltpu.SemaphoreType.DMA((6 * N_SHARD,)),
            pltpu.SemaphoreType.DMA((3 * N_DEV,))] if k else []))
    return pl.pallas_call(
        body, name=name, grid_spec=grid_spec,
        out_shape=[jax.ShapeDtypeStruct((N_SHARD, FS, d), BF16),
                   jax.ShapeDtypeStruct((N_SHARD, FS, d), BF16),
                   jax.ShapeDtypeStruct((N_SHARD, FS, d), BF16),
                   jax.ShapeDtypeStruct((N_SHARD, t, d), BF16)] + plan.out_shape
        + [jax.ShapeDtypeStruct((N_DEV, hf, d), BF16)] * k,
        compiler_params=_params(("arbitrary", "arbitrary")),
    )(order, x, dout, g, wg, wu, wd, gate, up_pre, *scatter)


def norm_bwd(x, g, dhp, dres, name):
    t, d = x.shape
    p = dhp.shape[0]
    tm = TOKEN_TILE

    def body(x_ref, g_ref, dhp_ref, dres_ref, dx_ref, dg_ref):
        i = pl.program_id(0)
        r, xn = _rms(x_ref[...])
        dh = dhp_ref[0].astype(F32)
        for q in range(1, p):
            dh = dh + dhp_ref[q].astype(F32)
        dhg = dh * g_ref[...]
        dx_ref[...] = dres_ref[...] + r * (dhg - xn * jnp.mean(dhg * xn, axis=-1, keepdims=True))

        @pl.when(i == 0)
        def _():
            dg_ref[...] = jnp.zeros_like(dg_ref)

        dg_ref[...] += jnp.sum(dh * xn, axis=0, keepdims=True)

    return pl.pallas_call(
        body, name=name, grid=(t // tm,),
        in_specs=[pl.BlockSpec((tm, d), lambda i: (i, 0)),
                  pl.BlockSpec((1, d), lambda i: (0, 0)),
                  pl.BlockSpec((p, tm, d), lambda i: (0, i, 0)),
                  pl.BlockSpec((tm, d), lambda i: (i, 0))],
        out_specs=[pl.BlockSpec((tm, d), lambda i: (i, 0)),
                   pl.BlockSpec((1, d), lambda i: (0, 0))],
        out_shape=[jax.ShapeDtypeStruct((t, d), F32), jax.ShapeDtypeStruct((1, d), F32)],
        compiler_params=_params(("arbitrary",)),
    )(x, g, dhp, dres)


def out_proj_bwd(x, g, dhp, dres, w, oa, ob_t):
    t, d = x.shape
    half = w.shape[0] // 2
    p = dhp.shape[0]
    tm = TOKEN_TILE
    ni = t // tm

    def body(x_ref, g_ref, dhp_ref, dres_ref, w_ref, oa_ref, obt_ref,
             dx_ref, dg_ref, da_ref, db_ref, dbt_ref, dw_ref, acc_ref):
        i = pl.program_id(0)

        @pl.when(i == 0)
        def _():
            dg_ref[...] = jnp.zeros_like(dg_ref)
            acc_ref[...] = jnp.zeros_like(acc_ref)

        r, xn = _rms(x_ref[...])
        dh = dhp_ref[0].astype(F32)
        for s in range(1, p):
            dh = dh + dhp_ref[s].astype(F32)
        dhg = dh * g_ref[...]
        dx = dres_ref[...] + r * (dhg - xn * jnp.mean(dhg * xn, axis=-1, keepdims=True))
        dx_ref[...] = dx
        dg_ref[...] += jnp.sum(dh * xn, axis=0, keepdims=True)
        dxb = dx.astype(BF16)
        da_ref[...] = _dot_nt(dxb, w_ref[0:half, :]).astype(BF16)
        db_ref[...] = _dot_nt(dxb, w_ref[half:2 * half, :]).astype(BF16)
        dbt_ref[...] = _dot_nt(w_ref[half:2 * half, :], dxb).astype(BF16)
        acc_ref[0:half, :] += _dot_tn(oa_ref[...], dxb)
        acc_ref[half:2 * half, :] += _dot(obt_ref[...], dxb)

        @pl.when(i == ni - 1)
        def _():
            dw_ref[...] = acc_ref[...].astype(BF16)

    row = lambda w_: pl.BlockSpec((tm, w_), lambda i: (i, 0))
    col = pl.BlockSpec((half, tm), lambda i: (0, i))
    whole = pl.BlockSpec((2 * half, d), lambda i: (0, 0))
    vec = pl.BlockSpec((1, d), lambda i: (0, 0))
    return pl.pallas_call(
        body, name="out_proj_bwd", grid=(ni,),
        in_specs=[row(d), vec, pl.BlockSpec((p, tm, d), lambda i: (0, i, 0)), row(d), whole, row(half), col],
        out_specs=[row(d), vec, row(half), row(half), col, whole],
        out_shape=[jax.ShapeDtypeStruct((t, d), F32), jax.ShapeDtypeStruct((1, d), F32),
                   jax.ShapeDtypeStruct((t, half), BF16), jax.ShapeDtypeStruct((t, half), BF16),
                   jax.ShapeDtypeStruct((half, t), BF16), jax.ShapeDtypeStruct((2 * half, d), BF16)],
        scratch_shapes=[pltpu.VMEM((2 * half, d), F32)],
        compiler_params=_params(("arbitrary",)),
    )(x, g, dhp, dres, w, oa, ob_t)


def _lane(shape):
    return lax.broadcasted_iota(jnp.int32, shape, 1)


PAIR = (0, HD, LANES)
Q_HEAD = (0, HD, HD + ROPE, LANES)
K_ROPE = (0, ROPE, LANES)


def _seg_mean(z, bounds):
    seg = lambda v: sum([(v >= b).astype(jnp.int32) for b in bounds[1:-1]], jnp.zeros_like(v))
    rows = seg(lax.broadcasted_iota(jnp.int32, (LANES, LANES), 0))
    cols = seg(lax.broadcasted_iota(jnp.int32, (LANES, LANES), 1))
    same = (rows == cols).astype(BF16)
    lane = _lane((1, LANES))
    inv = sum([jnp.where((lane >= a) & (lane < b), 1.0 / (b - a), 0.0) for a, b in zip(bounds[:-1], bounds[1:])])
    hi = z.astype(BF16)
    lo = (z - hi.astype(F32)).astype(BF16)
    return (_dot(hi, same) + _dot(lo, same)) * inv


def _seg_norm(x, bounds):
    r = lax.rsqrt(_seg_mean(x * x, bounds) + EPS)
    return r, x * r


def _seg_norm_bwd(r, xn, dyg, bounds):
    return r * (dyg - xn * _seg_mean(dyg * xn, bounds))


A_TM = 256


def mix_fwd(x, g, w, gq, gk, gcq, gckv):
    t, d = x.shape
    tm = TOKEN_TILE

    def body(x_ref, g_ref, w_ref, gq_ref, gk_ref, gcq_ref, gckv_ref,
             h_ref, p_ref, qa_ref, ka_ref, va_ref, cq_ref, ckv_ref):
        _, xn = _rms(x_ref[...])
        h = (xn * g_ref[...]).astype(BF16)
        h_ref[...] = h
        p_ref[...] = _dot_nt(h, w_ref[...])
        for p in range(4):
            sl = slice(LANES * p, LANES * (p + 1))
            _, xn = _seg_norm(p_ref[:, sl], PAIR)
            qa_ref[:, sl] = (xn * gq_ref[:, sl] * 0.125).astype(BF16)
            _, xn = _seg_norm(p_ref[:, 512 + LANES * p:512 + LANES * (p + 1)], PAIR)
            ka_ref[:, sl] = (xn * gk_ref[:, sl]).astype(BF16)
        va_ref[...] = p_ref[:, 1024:1536].astype(BF16)
        _, xn = _rms(p_ref[:, 1536:1792])
        cq_ref[...] = (xn * gcq_ref[...]).astype(BF16)
        _, xn = _rms(p_ref[:, 1792:1920])
        ckv_ref[...] = (xn * gckv_ref[...]).astype(BF16)

    row = lambda w: pl.BlockSpec((tm, w), lambda i: (i, 0))
    vec = lambda w: pl.BlockSpec((1, w), lambda i: (0, 0))
    return pl.pallas_call(
        body, name="mix_fwd", grid=(t // tm,),
        in_specs=[row(d), vec(d), pl.BlockSpec((PROJ_W, d), lambda i: (0, 0)), vec(512), vec(512), vec(256), vec(128)],
        out_specs=[row(d), row(PROJ_W), row(512), row(512), row(512), row(256), row(128)],
        out_shape=[jax.ShapeDtypeStruct((t, d), BF16), jax.ShapeDtypeStruct((t, PROJ_W), F32)]
        + [jax.ShapeDtypeStruct((t, w_), BF16) for w_ in (512, 512, 512, 256, 128)],
        compiler_params=_params(("parallel",)),
    )(x, g, w, gq, gk, gcq, gckv)


def mix_bwd(proj, x, h, dres, w, g, dqa, dkp, dvp, dcq, dckv, dkr, gq, gk, gcq, gckv):
    t, d = x.shape
    tm = A_TM
    nb = t // tm

    def body(p_ref, x_ref, h_ref, dres_ref, w_ref, g_ref, dqa_ref, dk0_ref, dk1_ref, dk2_ref, dv0_ref, dv1_ref,
             dv2_ref, dcq_ref, dckv_ref, dkr_ref, gq_ref, gk_ref, gcq_ref, gckv_ref,
             dx_ref, dw_ref, dg_ref, dgq_ref, dgk_ref, dgcq_ref, dgckv_ref, dp_ref, acc_ref):
        i = pl.program_id(0)

        @pl.when(i == 0)
        def _():
            for ref in (acc_ref, dg_ref, dgq_ref, dgk_ref, dgcq_ref, dgckv_ref):
                ref[...] = jnp.zeros_like(ref)

        has1 = (i + 1 < nb).astype(F32)
        has2 = (i + 2 < nb).astype(F32)
        for p in range(4):
            sl = slice(LANES * p, LANES * (p + 1))
            r, xn = _seg_norm(p_ref[:, sl], PAIR)
            dy = dqa_ref[:, sl] * 0.125
            dp_ref[:, sl] = _seg_norm_bwd(r, xn, dy * gq_ref[:, sl], PAIR).astype(BF16)
            dgq_ref[:, sl] += jnp.sum(dy * xn, axis=0, keepdims=True)
            ks = slice(512 + LANES * p, 512 + LANES * (p + 1))
            r, xn = _seg_norm(p_ref[:, ks], PAIR)
            dy = dk0_ref[0, :, sl] + has1 * dk1_ref[0, :, sl] + has2 * dk2_ref[0, :, sl]
            dp_ref[:, ks] = _seg_norm_bwd(r, xn, dy * gk_ref[:, sl], PAIR).astype(BF16)
            dgk_ref[:, sl] += jnp.sum(dy * xn, axis=0, keepdims=True)
        dp_ref[:, 1024:1536] = (dv0_ref[0] + has1 * dv1_ref[0] + has2 * dv2_ref[0]).astype(BF16)
        for (a, b, dlat_ref, glat_ref, dglat_ref) in ((1536, 1792, dcq_ref, gcq_ref, dgcq_ref),
                                                      (1792, 1920, dckv_ref, gckv_ref, dgckv_ref)):
            r, xn = _rms(p_ref[:, a:b])
            dy = dlat_ref[...]
            dyg = dy * glat_ref[...]
            dp_ref[:, a:b] = (r * (dyg - xn * jnp.mean(dyg * xn, axis=-1, keepdims=True))).astype(BF16)
            dglat_ref[...] += jnp.sum(dy * xn, axis=0, keepdims=True)
        dp_ref[:, 1920:2048] = dkr_ref[...].astype(BF16)

        dproj = dp_ref[...]
        acc_ref[...] += _dot_tn(dproj, h_ref[...])
        dh = _dot(dproj, w_ref[...])
        r, xn = _rms(x_ref[...])
        dhg = dh * g_ref[...]
        dx_ref[...] = dres_ref[...] + r * (dhg - xn * jnp.mean(dhg * xn, axis=-1, keepdims=True))
        dg_ref[...] += jnp.sum(dh * xn, axis=0, keepdims=True)

        @pl.when(i == nb - 1)
        def _():
            dw_ref[...] = acc_ref[...].astype(BF16)

    row = lambda w_: pl.BlockSpec((tm, w_), lambda i: (i, 0))
    vec = lambda w_: pl.BlockSpec((1, w_), lambda i: (0, 0))
    part = lambda s: pl.BlockSpec((1, tm, 512), lambda i: (s, jnp.minimum(i + s, nb - 1), 0))
    whole = pl.BlockSpec((PROJ_W, d), lambda i: (0, 0))
    return pl.pallas_call(
        body, name="mix_bwd", grid=(nb,),
        in_specs=[row(PROJ_W), row(d), row(d), row(d), whole, vec(d), row(512), part(0), part(1), part(2),
                  part(0), part(1), part(2), row(256), row(128), row(128), vec(512), vec(512), vec(256), vec(128)],
        out_specs=[row(d), whole, vec(d), vec(512), vec(512), vec(256), vec(128)],
        out_shape=[jax.ShapeDtypeStruct((t, d), F32), jax.ShapeDtypeStruct((PROJ_W, d), BF16)]
        + [jax.ShapeDtypeStruct((1, w_), F32) for w_ in (d, 512, 512, 256, 128)],
        scratch_shapes=[pltpu.VMEM((tm, PROJ_W), BF16), pltpu.VMEM((PROJ_W, d), F32)],
        compiler_params=_params(("arbitrary",)),
    )(proj, x, h, dres, w, g, dqa, dkp, dkp, dkp, dvp, dvp, dvp, dcq, dckv, dkr, gq, gk, gcq, gckv)


def _roll(x, shift):
    return pltpu.roll(x, shift % LANES, 1)


def _rope(y, c, s1, s2):
    return y * c + _roll(y, -16) * s1 + _roll(y, 16) * s2


def _rope_bwd(d, c, s1, s2):
    return d * c + _roll(d * s1, 16) + _roll(d * s2, -16)


def _up_proj(cq_ref, ckv_ref, wuq_ref, wukv_ref):
    return _dot_nt(cq_ref[...], wuq_ref[...]), _dot(ckv_ref[...], wukv_ref[...])


def prep2_fwd(cqn, ckvn, proj, wuq, wukv, gq, gk, gkr, tabs):
    t = cqn.shape[0]
    tm = TOKEN_TILE

    def body(cq_ref, ckv_ref, kr_ref, wuq_ref, wukv_ref, gq_ref, gk_ref, gkr_ref, tab_ref, qf_ref, kf_ref, vp_ref):
        q_all, kv_all = _up_proj(cq_ref, ckv_ref, wuq_ref, wukv_ref)
        _, xn = _seg_norm(kr_ref[...], K_ROPE)
        kpe = _roll(_rope(xn * gkr_ref[...], tab_ref[3], tab_ref[4], tab_ref[5]), 64)
        for h in range(HEADS):
            sl = slice(LANES * h, LANES * (h + 1))
            _, xn = _seg_norm(q_all[:, sl], Q_HEAD)
            qf_ref[:, sl] = (_rope(xn * gq_ref[...], tab_ref[0], tab_ref[1], tab_ref[2]) * B_SCALE2).astype(BF16)
            x = kv_all[:, sl]
            lo = _lane(x.shape) < HD
            _, xkn = _seg_norm(jnp.where(lo, x, 0.0), PAIR)
            kf_ref[:, sl] = (xkn * gk_ref[...] + kpe).astype(BF16)
            if h % 2 == 0:
                v_even = _roll(x, 64)
            else:
                vp_ref[:, LANES * (h // 2):LANES * (h // 2 + 1)] = jnp.where(lo, v_even, x).astype(BF16)

    row = lambda w: pl.BlockSpec((tm, w), lambda i: (i, 0))
    vec = lambda w: pl.BlockSpec((1, w), lambda i: (0, 0))
    full = lambda a: pl.BlockSpec(a.shape, lambda i: (0, 0))
    return pl.pallas_call(
        body, name="prep2_fwd", grid=(t // tm,),
        in_specs=[row(256), row(128), pl.BlockSpec((tm, LANES), lambda i: (i, 15)), full(wuq), full(wukv),
                  vec(128), vec(128), vec(128), pl.BlockSpec((6, tm, LANES), lambda i: (0, i, 0))],
        out_specs=[row(1024), row(1024), row(512)],
        out_shape=[jax.ShapeDtypeStruct((t, 1024), BF16), jax.ShapeDtypeStruct((t, 1024), BF16),
                   jax.ShapeDtypeStruct((t, 512), BF16)],
        compiler_params=_params(("parallel",)),
    )(cqn, ckvn, proj, wuq, wukv, gq, gk, gkr, tabs)


def prep2_bwd(cqn, ckvn, proj, wuq, wukv, dqf, dkf, dvp, gq, gk, gkr, tabs):
    t = cqn.shape[0]
    tm = TOKEN_TILE

    def body(cq_ref, ckv_ref, kr_ref, wuq_ref, wukv_ref, dqf_ref, dkf_ref, dvp_ref, gq_ref, gk_ref, gkr_ref, tab_ref,
             dcq_ref, dckv_ref, dkr_ref, dwuq_ref, dwukv_ref, dgq_ref, dgk_ref, dgkr_ref, dq_ref, dkv_ref):
        i = pl.program_id(0)

        @pl.when(i == 0)
        def _():
            for ref in (dwuq_ref, dwukv_ref, dgq_ref, dgk_ref, dgkr_ref):
                ref[...] = jnp.zeros_like(ref)

        q_all, kv_all = _up_proj(cq_ref, ckv_ref, wuq_ref, wukv_ref)
        dgq = jnp.zeros((1, LANES), F32)
        dgk = jnp.zeros((1, LANES), F32)
        dkpe = jnp.zeros((tm, LANES), F32)
        for h in range(HEADS):
            sl = slice(LANES * h, LANES * (h + 1))
            lane = _lane((tm, LANES))
            mn, mr = lane < HD, (lane >= HD) & (lane < HD + ROPE)
            r, xn = _seg_norm(q_all[:, sl], Q_HEAD)
            dy = _rope_bwd(dqf_ref[sl, :].T, tab_ref[0], tab_ref[1], tab_ref[2])
            dyg = dy * gq_ref[...]
            dq_ref[:, sl] = _seg_norm_bwd(r, xn, dyg, Q_HEAD).astype(BF16)
            dgq = dgq + jnp.sum(dy * xn, axis=0, keepdims=True)

            x = kv_all[:, sl]
            dk = dkf_ref[:, sl]
            rk, xkn = _seg_norm(jnp.where(mn, x, 0.0), PAIR)
            dyk = jnp.where(mn, dk, 0.0)
            dxk = _seg_norm_bwd(rk, xkn, dyk * gk_ref[...], PAIR)
            dgk = dgk + jnp.sum(dyk * xkn, axis=0, keepdims=True)
            dkpe = dkpe + jnp.where(mr, dk, 0.0)
            dvpair = dvp_ref[:, LANES * (h // 2):LANES * (h // 2 + 1)]
            dv = _roll(dvpair, 64) if h % 2 == 0 else dvpair
            dkv_ref[:, sl] = jnp.where(mn, dxk, dv).astype(BF16)

        r, xn = _seg_norm(kr_ref[...], K_ROPE)
        dy = _rope_bwd(_roll(dkpe, 64), tab_ref[3], tab_ref[4], tab_ref[5])
        dkr_ref[...] = _seg_norm_bwd(r, xn, dy * gkr_ref[...], K_ROPE)
        dgq_ref[...] += dgq
        dgk_ref[...] += dgk
        dgkr_ref[...] += jnp.sum(dy * xn, axis=0, keepdims=True)
        dqb, dkvb = dq_ref[...], dkv_ref[...]
        dcq_ref[...] = _dot(dqb, wuq_ref[...])
        dckv_ref[...] = _dot_nt(dkvb, wukv_ref[...])
        dwuq_ref[...] += _dot_tn(dqb, cq_ref[...])
        dwukv_ref[...] += _dot_tn(ckv_ref[...], dkvb)

    row = lambda w: pl.BlockSpec((tm, w), lambda i: (i, 0))
    vec = lambda w: pl.BlockSpec((1, w), lambda i: (0, 0))
    full = lambda a: pl.BlockSpec(a.shape, lambda i: (0, 0))
    return pl.pallas_call(
        body, name="prep2_bwd", grid=(t // tm,),
        in_specs=[row(256), row(128), pl.BlockSpec((tm, LANES), lambda i: (i, 15)), full(wuq), full(wukv),
                  pl.BlockSpec((1024, tm), lambda i: (0, i)), row(1024), row(512),
                  vec(128), vec(128), vec(128), pl.BlockSpec((6, tm, LANES), lambda i: (0, i, 0))],
        out_specs=[row(256), row(128), row(128), full(wuq), full(wukv), vec(128), vec(128), vec(128)],
        out_shape=[jax.ShapeDtypeStruct((t, 256), F32), jax.ShapeDtypeStruct((t, 128), F32),
                   jax.ShapeDtypeStruct((t, LANES), F32), jax.ShapeDtypeStruct(wuq.shape, F32),
                   jax.ShapeDtypeStruct(wukv.shape, F32)] + [jax.ShapeDtypeStruct((1, LANES), F32)] * 3,
        scratch_shapes=[pltpu.VMEM((tm, 1024), BF16), pltpu.VMEM((tm, 1024), BF16)],
        compiler_params=_params(("arbitrary",)),
    )(cqn, ckvn, proj, wuq, wukv, dqf, dkf, dvp, gq, gk, gkr, tabs)


A_TQ = 256
A_WIN = 3 * A_TQ


def _a_specs(t):
    nb = t // A_TQ
    blk = lambda s: pl.BlockSpec((A_TQ, 512), lambda i: (jnp.maximum(i - s, 0), 0))
    return nb, blk


def _a_exp(q_ref, kc, b_ref, head, sl, lo):
    hm = lo if head % 2 == 0 else ~lo
    qm = jnp.where(hm, q_ref[:, sl], jnp.zeros((), BF16))
    s = _dot_nt(qm, kc) + b_ref[0, head]
    e = jnp.exp(s - jnp.max(s, axis=-1, keepdims=True))
    return hm, qm, e, 1.0 / jnp.sum(e, axis=-1, keepdims=True)


def _a_bias_spec():
    return pl.BlockSpec((1, HEADS, A_TQ, A_WIN), lambda i: (jnp.minimum(i, 2), 0, 0, 0))


def attn_a_fwd(qa, ka, va, bias):
    t = qa.shape[0]
    nb, blk = _a_specs(t)

    def body(q_ref, k2_ref, k1_ref, k0_ref, v2_ref, v1_ref, v0_ref, b_ref, o_ref):
        lo = _lane((A_TQ, LANES)) < HD
        for p in range(4):
            sl = slice(LANES * p, LANES * (p + 1))
            kc = jnp.concatenate([k2_ref[:, sl], k1_ref[:, sl], k0_ref[:, sl]], axis=0)
            vc = jnp.concatenate([v2_ref[:, sl], v1_ref[:, sl], v0_ref[:, sl]], axis=0)
            outs = []
            for h2 in range(2):
                _, _, e, inv = _a_exp(q_ref, kc, b_ref, 2 * p + h2, sl, lo)
                outs.append(_dot(e.astype(BF16), vc) * inv)
            o_ref[:, sl] = jnp.where(lo, outs[0], outs[1]).astype(BF16)

    return pl.pallas_call(
        body, name="attn_a_fwd", grid=(nb,),
        in_specs=[blk(0), blk(2), blk(1), blk(0), blk(2), blk(1), blk(0), _a_bias_spec()],
        out_specs=pl.BlockSpec((A_TQ, 512), lambda i: (i, 0)),
        out_shape=jax.ShapeDtypeStruct((t, 512), BF16),
        compiler_params=_params(("parallel",)),
    )(qa, ka, ka, ka, va, va, va, bias)


def attn_a_bwd(qa, ka, va, bias, do):
    t = qa.shape[0]
    nb, blk = _a_specs(t)

    def body(q_ref, k2_ref, k1_ref, k0_ref, v2_ref, v1_ref, v0_ref, b_ref, do_ref, dq_ref, dk_ref, dv_ref, db_ref):
        qb = pl.program_id(0)

        @pl.when(qb == 0)
        def _():
            db_ref[...] = jnp.zeros_like(db_ref)

        lo = _lane((A_TQ, LANES)) < HD
        for p in range(4):
            sl = slice(LANES * p, LANES * (p + 1))
            kc = jnp.concatenate([k2_ref[:, sl], k1_ref[:, sl], k0_ref[:, sl]], axis=0)
            vc = jnp.concatenate([v2_ref[:, sl], v1_ref[:, sl], v0_ref[:, sl]], axis=0)
            dqs = []
            dkt = jnp.zeros((LANES, A_WIN), F32)
            dvt = jnp.zeros((LANES, A_WIN), F32)
            for h2 in range(2):
                head = 2 * p + h2
                hm, qm, e, inv = _a_exp(q_ref, kc, b_ref, head, sl, lo)
                pr = e * inv
                dom = jnp.where(hm, do_ref[:, sl], jnp.zeros((), BF16))
                dp = _dot_nt(dom, vc)
                ds = pr * (dp - jnp.sum(pr * dp, axis=-1, keepdims=True))
                db_ref[head] += ds
                dsb = ds.astype(BF16)
                dqs.append(_dot(dsb, kc))
                dkt = dkt + _dot_tn(qm, dsb)
                dvt = dvt + _dot_tn(dom, pr.astype(BF16))
            dq_ref[:, sl] = jnp.where(lo, dqs[0], dqs[1])
            dkc, dvc = dkt.T, dvt.T
            for s in range(3):
                rows = slice(A_TQ * (2 - s), A_TQ * (3 - s))
                dk_ref[s, :, sl] = dkc[rows]
                dv_ref[s, :, sl] = dvc[rows]

    share = pl.BlockSpec((3, A_TQ, 512), lambda i: (0, i, 0))
    return pl.pallas_call(
        body, name="attn_a_bwd", grid=(nb,),
        in_specs=[blk(0), blk(2), blk(1), blk(0), blk(2), blk(1), blk(0), _a_bias_spec(), blk(0)],
        out_specs=[pl.BlockSpec((A_TQ, 512), lambda i: (i, 0)), share, share,
                   pl.BlockSpec((HEADS, A_TQ, A_WIN), lambda i: (0, 0, 0))],
        out_shape=[jax.ShapeDtypeStruct((t, 512), F32), jax.ShapeDtypeStruct((3, t, 512), F32),
                   jax.ShapeDtypeStruct((3, t, 512), F32), jax.ShapeDtypeStruct((HEADS, A_TQ, A_WIN), F32)],
        compiler_params=_params(("arbitrary",)),
    )(qa, ka, ka, ka, va, va, va, bias, do)


B_T = 1024


B_SCALE2 = B_SCALE * 1.4426950408889634
B_FWD_HEADS = 8
_B_ALL = slice(0, B_T)
_B_LO, _B_HI = slice(0, B_T // 2), slice(B_T // 2, B_T)
_B_DIAG = ((_B_LO, _B_LO), (_B_LO, _B_HI), (_B_HI, _B_HI))


def _tri_tables(n, by_query):
    pairs = [(i, j) for i in range(n) for j in range(i + 1)] if by_query else [(i, j) for j in range(n) for i in range(j, n)]
    return (np.asarray([p[0] for p in pairs], np.int32), np.asarray([p[1] for p in pairs], np.int32))


def _b_mask_t(s):
    kc = lax.broadcasted_iota(jnp.int32, s.shape, 0) // CHUNK
    qc = lax.broadcasted_iota(jnp.int32, s.shape, 1) // CHUNK
    return jnp.where(kc <= qc, s, NEG)


def attn_b_fwd(qf, kf, vp):
    t = qf.shape[0]
    n = t // B_T
    qtab, ktab = _tri_tables(n, by_query=True)
    hps = B_FWD_HEADS

    def body(qt_ref, kt_ref, q_ref, k_ref, v_ref, o_ref, lse_ref, m_s, l_s, acc_s):
        qb, kb = qt_ref[pl.program_id(1)], kt_ref[pl.program_id(1)]

        @pl.when(kb == 0)
        def _():
            m_s[...] = jnp.full_like(m_s, NEG)
            l_s[...] = jnp.zeros_like(l_s)
            acc_s[...] = jnp.zeros_like(acc_s)

        def block(kr, qr, masked):
            for h2 in range(hps):
                sl = slice(LANES * h2, LANES * (h2 + 1))
                v = v_ref[kr, LANES * (h2 // 2):LANES * (h2 // 2 + 1)]
                s = _dot_nt(k_ref[kr, sl], q_ref[qr, sl])
                if masked:
                    s = _b_mask_t(s)
                m_prev = m_s[h2, :, qr]
                m_new = jnp.maximum(m_prev, jnp.max(s, axis=0, keepdims=True))
                alpha = jnp.exp2(m_prev - m_new)
                pr = jnp.exp2(s - m_new)
                l_s[h2, :, qr] = alpha * l_s[h2, :, qr] + jnp.sum(pr, axis=0, keepdims=True)
                acc_s[h2, :, qr] = alpha * acc_s[h2, :, qr] + _dot_tn(v, pr.astype(BF16))
                m_s[h2, :, qr] = m_new

        @pl.when(kb < qb)
        def _():
            block(_B_ALL, _B_ALL, False)

        @pl.when(kb == qb)
        def _():
            for kr, qr in _B_DIAG:
                block(kr, qr, kr == qr)
            for h2 in range(hps):
                l = l_s[h2]
                rows = slice(HD * (h2 % 2), HD * (h2 % 2 + 1))
                o_ref[HD * h2:HD * (h2 + 1), :] = (acc_s[h2, rows, :] * (1.0 / l)).astype(BF16)
                lse_ref[h2 // 2, h2 % 2:h2 % 2 + 1, :] = m_s[h2] + jnp.log2(l)

    grid_spec = pltpu.PrefetchScalarGridSpec(
        num_scalar_prefetch=2, grid=(HEADS // hps, len(qtab)),
        in_specs=[pl.BlockSpec((B_T, LANES * hps), lambda p, s, qt, kt: (qt[s], p)),
                  pl.BlockSpec((B_T, LANES * hps), lambda p, s, qt, kt: (kt[s], p)),
                  pl.BlockSpec((B_T, HD * hps), lambda p, s, qt, kt: (kt[s], p))],
        out_specs=[pl.BlockSpec((HD * hps, B_T), lambda p, s, qt, kt: (p, qt[s])),
                   pl.BlockSpec((hps // 2, 2, B_T), lambda p, s, qt, kt: (p, 0, qt[s]))],
        scratch_shapes=[pltpu.VMEM((hps, 1, B_T), F32), pltpu.VMEM((hps, 1, B_T), F32),
                        pltpu.VMEM((hps, LANES, B_T), F32)])
    return pl.pallas_call(
        body, name="attn_b_fwd", grid_spec=grid_spec,
        out_shape=[jax.ShapeDtypeStruct((512, t), BF16), jax.ShapeDtypeStruct((4, 2, t), F32)],
        compiler_params=_params(("parallel", "arbitrary")),
    )(jnp.asarray(qtab), jnp.asarray(ktab), qf, kf, vp)


def attn_b_bwd(qf, kf, vp, do, do_t, o_t, lse, scatter=()):
    t = qf.shape[0]
    n = t // B_T
    qtab, ktab = _tri_tables(n, by_query=False)
    plan = ScatterPlan(scatter)
    m = plan.n
    last = len(qtab) - 1

    def body(*refs):
        qt_ref, kt_ref, q_ref, k_ref, v_ref, do_ref, dot_ref, ot_ref, lse_ref = refs[:9]
        ins, (dq_ref, dk_ref, dv_ref), outs = refs[9:9 + m], refs[9 + m:12 + m], refs[12 + m:12 + 2 * m]
        sems = refs[12 + 2 * m:]
        qb, kb = qt_ref[pl.program_id(1)], kt_ref[pl.program_id(1)]
        if m:
            pl.when((pl.program_id(0) == 0) & (pl.program_id(1) == 0))(lambda: plan.start(ins, outs, sems))

        @pl.when(pl.program_id(1) == 0)
        def _():
            dq_ref[...] = jnp.zeros_like(dq_ref)

        @pl.when(qb == kb)
        def _():
            dk_ref[...] = jnp.zeros_like(dk_ref)
            dv_ref[...] = jnp.zeros_like(dv_ref)

        def block(kr, qr, masked):
            nq = qr.stop - qr.start
            cols = pl.ds(pl.multiple_of(qb * B_T + qr.start, LANES), nq)
            v = v_ref[kr, :]
            dov = do_ref[qr, :]
            prod = dot_ref[:, qr].astype(F32) * ot_ref[:, qr].astype(F32)
            lo = _lane((nq, LANES)) < HD
            for h2 in range(2):
                sl = slice(LANES * h2, LANES * (h2 + 1))
                hm = lo if h2 == 0 else ~lo
                q = q_ref[qr, sl]
                k = k_ref[kr, sl]
                dom = jnp.where(hm, dov, jnp.zeros((), BF16))
                delta = jnp.sum(prod[HD * h2:HD * (h2 + 1), :], axis=0, keepdims=True)
                s = _dot_nt(k, q)
                if masked:
                    s = _b_mask_t(s)
                pr = jnp.exp2(s - lse_ref[0, h2:h2 + 1, qr])
                dp = _dot_nt(v, dom)
                ds = (pr * (dp - delta)).astype(BF16)
                dk_ref[kr, sl] += _dot(ds, q) * (B_SCALE / B_SCALE2)
                dv_ref[kr, :] += _dot(pr.astype(BF16), dom)
                dq_ref[sl, cols] += _dot_tn(k, ds) * B_SCALE

        @pl.when(qb > kb)
        def _():
            block(_B_ALL, _B_ALL, False)

        @pl.when(qb == kb)
        def _():
            for kr, qr in _B_DIAG:
                block(kr, qr, kr == qr)

        if m:
            pl.when((pl.program_id(0) == 3) & (pl.program_id(1) == last))(lambda: plan.finish(ins, outs, sems))

    qrow = lambda w: pl.BlockSpec((B_T, w), lambda p, s, qt, kt: (qt[s], p))
    qcol = pl.BlockSpec((LANES, B_T), lambda p, s, qt, kt: (p, qt[s]))
    krow = lambda w: pl.BlockSpec((B_T, w), lambda p, s, qt, kt: (kt[s], p))
    grid_spec = pltpu.PrefetchScalarGridSpec(
        num_scalar_prefetch=2, grid=(4, len(qtab)),
        in_specs=[qrow(256), krow(256), krow(LANES), qrow(LANES), qcol, qcol,
                  pl.BlockSpec((1, 2, B_T), lambda p, s, qt, kt: (p, 0, qt[s]))] + [ANY] * m,
        out_specs=[pl.BlockSpec((256, t), lambda p, s, qt, kt: (p, 0)), krow(256), krow(LANES)] + [ANY] * m,
        scratch_shapes=plan.scratch if m else [])
    return pl.pallas_call(
        body, name="attn_b_bwd", grid_spec=grid_spec,
        out_shape=[jax.ShapeDtypeStruct((1024, t), F32), jax.ShapeDtypeStruct((t, 1024), F32),
                   jax.ShapeDtypeStruct((t, 512), F32)] + plan.out_shape,
        compiler_params=_params(("arbitrary", "arbitrary")),
    )(jnp.asarray(qtab), jnp.asarray(ktab), qf, kf, vp, do, do_t, o_t, lse, *scatter)


_U_LEN = A_TQ + A_WIN - 1


def _band_mask():
    a = np.arange(A_TQ)[:, None] // CHUNK
    b = np.arange(A_WIN)[None, :] // CHUNK
    return (b >= a) & (b <= a + A_LEFT)


def bias_block(table):
    h = table.shape[0]
    n_lo = A_WIN - 1 - 2 * A_TQ - A_MAX_REL
    ext = jnp.concatenate([jnp.repeat(table[:, :1], n_lo, axis=1), table,
                           jnp.repeat(table[:, -1:], _U_LEN - n_lo - table.shape[1], axis=1)], axis=1)
    row = jnp.pad(ext[:, ::-1], ((0, 0), (0, 1)))[:, None, :]
    band = _band_mask()
    first = [band & (np.arange(A_WIN)[None, :] >= 2 * A_TQ - A_TQ * v) for v in range(3)]
    keep = jnp.asarray(np.stack(first), jnp.int32)

    def body(r_ref, k_ref, o_ref):
        rows = jnp.broadcast_to(r_ref[0], (A_TQ, _U_LEN + 1))
        skew = pltpu.roll(rows, _U_LEN + 1 - (A_TQ - 1), 1, stride=1, stride_axis=0)
        toep = skew[:, :A_WIN]
        for v in range(3):
            o_ref[v, 0] = jnp.where(k_ref[v] != 0, toep, NEG)

    return pl.pallas_call(
        body, name="bias_block", grid=(h,),
        in_specs=[pl.BlockSpec((1, 1, _U_LEN + 1), lambda i: (i, 0, 0)),
                  pl.BlockSpec((3, A_TQ, A_WIN), lambda i: (0, 0, 0))],
        out_specs=pl.BlockSpec((3, 1, A_TQ, A_WIN), lambda i: (0, i, 0, 0)),
        out_shape=jax.ShapeDtypeStruct((3, h, A_TQ, A_WIN), F32),
        compiler_params=_params(("parallel",)),
    )(row, keep)


def bias_block_grad(db):
    h = db.shape[0]
    n_lo = A_WIN - 1 - 2 * A_TQ - A_MAX_REL
    skew = jnp.pad(db, ((0, 0), (0, 0), (A_TQ - 1, 0)))
    flat = jnp.pad(skew.reshape(h, A_TQ * _U_LEN), ((0, 0), (0, A_TQ)))
    ext = jnp.sum(flat.reshape(h, A_TQ, _U_LEN + 1), axis=1)[:, :_U_LEN][:, ::-1]
    n_tab = 2 * A_MAX_REL + 1
    first = jnp.sum(ext[:, :n_lo + 1], axis=1, keepdims=True)
    last = jnp.sum(ext[:, n_lo + n_tab - 1:], axis=1, keepdims=True)
    return jnp.concatenate([first, ext[:, n_lo + 1:n_lo + n_tab - 1], last], axis=1)


def rope_tabs(t):
    lane = jnp.arange(LANES)
    in_k = lane < ROPE
    inv = 1.0 / (10000.0 ** ((2 * (lane % (ROPE // 2))).astype(F32) / ROPE))
    ang = jnp.arange(t, dtype=F32)[:, None] * jnp.where(in_k, inv, 0.0)[None, :]
    cos, sin = jnp.cos(ang), jnp.sin(ang)
    first = lane < ROPE // 2
    ck = jnp.where(in_k, cos, 0.0)
    s1k = jnp.where(first, -sin, 0.0)
    s2k = jnp.where(in_k & ~first, sin, 0.0)
    to_q = lambda a: jnp.roll(a, HD, axis=1)
    cq = jnp.where(lane < HD, 1.0, to_q(ck))
    return jnp.stack([cq, to_q(s1k), to_q(s2k), ck, s1k, s2k])


def _pad_lanes(v, width):
    return jnp.pad(v, ((0, 0), (0, width - v.shape[1])))


LATE = ("w_in", "b_w_uq", "b_w_ukv", "w_out", "ffn2_w_gate", "ffn2_w_up", "ffn2_w_down")
FFN2 = ("ffn2_w_gate", "ffn2_w_up", "ffn2_w_down")


def kernel_layout(gathered):
    w = {n: v for n, v in gathered.items() if n.startswith("ffn")}
    if "w_in" in gathered:
        w["w_in"] = jnp.pad(gathered["w_in"].reshape(IN_COLS, D_MODEL), ((0, PROJ_W - IN_COLS), (0, 0)))
        uq = gathered["b_w_uq"].reshape(HEADS, HD + ROPE, 256)
        w["b_w_uq"] = jnp.pad(uq, ((0, 0), (0, LANES - HD - ROPE), (0, 0))).reshape(HEADS * LANES, 256)
        w["b_w_ukv"] = _shards_to_cols(gathered["b_w_ukv"])
        w["w_out"] = gathered["w_out"].reshape(N_SHARD * gathered["w_out"].shape[1], D_MODEL)
    return w


def local_step(x, target, w, late=None):
    t = x.shape[0]
    gq = jnp.tile(w["a_q_norm"], (1, HEADS))
    gk = jnp.tile(w["a_k_norm"], (1, HEADS))
    gq128 = _pad_lanes(jnp.concatenate([w["b_q_nope_norm"], w["b_q_rope_norm"]], axis=1), LANES)
    gk128 = _pad_lanes(w["b_k_nope_norm"], LANES)
    gkr128 = _pad_lanes(w["b_k_rope_norm"], LANES)
    tabs = rope_tabs(t)
    bias = bias_block(w["a_rel_bias"])

    if late is None:
        x1, gate1, up1 = ffn_fwd(x, w["ffn1_norm"], w["ffn1_w_gate"], w["ffn1_w_up"], w["ffn1_w_down"], "ffn_fwd")
    else:
        own, shard = late
        x1, gate1, up1, *got = ffn_fwd(x, w["ffn1_norm"], w["ffn1_w_gate"], w["ffn1_w_up"], w["ffn1_w_down"],
                                       "ffn_fwd_gather", gather=own)
        w = dict(w, **kernel_layout({n: lax.dynamic_update_index_in_dim(g_, o_, shard, 0)
                                     for n, g_, o_ in zip(LATE, got, own)}))
    h, proj, qa, ka, va, cqn, ckvn = mix_fwd(x1, w["mix_norm"], w["w_in"], gq, gk,
                                             w["b_q_lat_norm"], w["b_kv_lat_norm"])
    qf, kf, vp = prep2_fwd(cqn, ckvn, proj, w["b_w_uq"], w["b_w_ukv"], gq128, gk128, gkr128, tabs)
    oa = attn_a_fwd(qa, ka, va, bias)
    ob_t, lse = attn_b_fwd(qf, kf, vp)
    g = {}
    dx3, gate2, up2, g["final_norm"], loss, x2 = ffn_fwd(
        x1, w["ffn2_norm"], w["ffn2_w_gate"], w["ffn2_w_up"], w["ffn2_w_down"], "ffn_fwd_loss",
        loss_head=(w["final_norm"], target), pre_proj=(oa, ob_t, w["w_out"]))
    g["ffn2_w_gate"], g["ffn2_w_up"], g["ffn2_w_down"], dhp = ffn_bwd(
        x2, dx3, w["ffn2_norm"], w["ffn2_w_gate"], w["ffn2_w_up"], w["ffn2_w_down"], gate2, up2, "ffn_bwd")
    dx2, g["ffn2_norm"], d_oa, d_ob, d_ob_t, g["w_out"] = out_proj_bwd(
        x2, w["ffn2_norm"], dhp, dx3, w["w_out"], oa, ob_t)
    early = [g[n] for n in FFN2] + [g["w_out"].reshape(N_SHARD, -1, D_MODEL)]
    dqf, dkf, dvp, *landed_early = attn_b_bwd(qf, kf, vp, d_ob, d_ob_t, ob_t, lse,
                                              scatter=() if late is None else early)
    dqa, dkp, dvpa, dbias = attn_a_bwd(qa, ka, va, bias, d_oa)
    dcq, dckv, dkr, dwuq, dwukv, dgq128, dgk128, dgkr128 = prep2_bwd(
        cqn, ckvn, proj, w["b_w_uq"], w["b_w_ukv"], dqf, dkf, dvp, gq128, gk128, gkr128, tabs)
    g["b_w_uq"], g["b_w_ukv"] = dwuq.astype(BF16), dwukv.astype(BF16)
    dx1, g["w_in"], g["mix_norm"], dgq, dgk, g["b_q_lat_norm"], g["b_kv_lat_norm"] = mix_bwd(
        proj, x1, h, dx2, w["w_in"], w["mix_norm"], dqa, dkp, dvpa, dcq, dckv, dkr, gq, gk,
        w["b_q_lat_norm"], w["b_kv_lat_norm"])
    mid = [g["w_in"][:IN_COLS].reshape(N_SHARD, IN_COLS // N_SHARD, D_MODEL),
           g["b_w_uq"].reshape(HEADS, LANES, 256)[:, :HD + ROPE].reshape(N_SHARD, -1, 256),
           _cols_to_shards(g["b_w_ukv"])]
    g["ffn1_w_gate"], g["ffn1_w_up"], g["ffn1_w_down"], dhp, *landed_late = ffn_bwd(
        x, dx1, w["ffn1_norm"], w["ffn1_w_gate"], w["ffn1_w_up"], w["ffn1_w_down"], gate1, up1,
        "ffn_bwd" if late is None else "ffn_bwd_scatter", scatter=() if late is None else mid,
        spread=None if late is None else late[1])
    grad_x, g["ffn1_norm"] = norm_bwd(x, w["ffn1_norm"], dhp, dx1, "ffn_norm_bwd")
    landed = dict(zip(FFN2 + ("w_out", "w_in", "b_w_uq", "b_w_ukv", "ffn1_w_gate", "ffn1_w_up", "ffn1_w_down"),
                      landed_early + landed_late))

    g["a_q_norm"] = jnp.sum(dgq.reshape(HEADS, HD), axis=0, keepdims=True)
    g["a_k_norm"] = jnp.sum(dgk.reshape(HEADS, HD), axis=0, keepdims=True)
    g["a_rel_bias"] = bias_block_grad(dbias)
    g["b_q_nope_norm"] = dgq128[:, :HD]
    g["b_q_rope_norm"] = dgq128[:, HD:HD + ROPE]
    g["b_k_nope_norm"] = dgk128[:, :HD]
    g["b_k_rope_norm"] = dgkr128[:, :ROPE]
    return loss, grad_x, g, landed


ANY = pl.BlockSpec(memory_space=pl.ANY)
N_DEV = 8


def _place():
    return lax.axis_index("x"), lax.axis_index("y"), lax.axis_index("c")


def _flip(v, bit):
    return 1 - v if bit else v


BF16_ROWS = 16


def _split_axis(shape):
    return 0 if (shape[0] // 2) % BF16_ROWS == 0 else 1


def _half_shape(shape):
    axis = _split_axis(shape)
    return tuple(s // 2 if a == axis else s for a, s in enumerate(shape))


def _half(shape, core):
    axis = _split_axis(shape)
    size = shape[axis] // 2
    return tuple(pl.ds(core * size, size) if a == axis else slice(None) for a in range(2))


class GatherPlan:
    def __init__(self, ws):
        self.shapes = [w.shape for w in ws]
        self.n = len(ws)
        self.out_shape = [jax.ShapeDtypeStruct((N_SHARD,) + w.shape, w.dtype) for w in ws]
        self.scratch = [pltpu.SemaphoreType.DMA((6 * self.n,)), pltpu.SemaphoreType.DMA((6 * self.n,))]

    def _copies(self, ins, outs, sems):
        x, y, c = _place()
        s_me = 2 * x + y
        sibling = (x, y, 1 - c)
        send_sems, recv_sems = sems

        def remote(k, src, dst, to):
            return pltpu.make_async_remote_copy(src_ref=src, dst_ref=dst, send_sem=send_sems.at[k],
                                                recv_sem=recv_sems.at[k], device_id=to, device_id_type=MESH)

        ici, fwd = [], []
        for a in range(self.n):
            mine, theirs = _half(self.shapes[a], c), _half(self.shapes[a], 1 - c)
            for j, (cx, cy) in enumerate([(1 - x, y), (x, 1 - y), (1 - x, 1 - y)]):
                got = outs[a].at[(2 * cx + cy,) + mine]
                ici.append((remote(6 * a + j, ins[a].at[mine], outs[a].at[(s_me,) + mine], (cx, cy, c)),
                            remote(6 * a + j, got, got, (cx, cy, c))))
                passed = outs[a].at[(2 * cx + cy,) + theirs]
                fwd.append((remote(6 * a + 3 + j, got, got, sibling), remote(6 * a + 3 + j, passed, passed, sibling)))
        return ici, fwd

    def start(self, ins, outs, sems):
        for send, _ in self._copies(ins, outs, sems)[0]:
            send.start()

    def forward(self, ins, outs, sems):
        ici, fwd = self._copies(ins, outs, sems)
        for (_, arrival), (send, _) in zip(ici, fwd):
            arrival.wait_recv()
            send.start()

    def finish(self, ins, outs, sems):
        ici, fwd = self._copies(ins, outs, sems)
        for _, arrival in fwd:
            arrival.wait_recv()
        for send, _ in ici + fwd:
            send.wait_send()


def allgather_shards(ws):
    plan = GatherPlan(ws)
    n = plan.n

    def body(*refs):
        ins, outs, sems = refs[:n], refs[n:2 * n], refs[2 * n:]
        plan.start(ins, outs, sems)
        plan.forward(ins, outs, sems)
        plan.finish(ins, outs, sems)

    return pl.pallas_call(
        body, name="allgather_shards", in_specs=[ANY] * n, out_specs=[ANY] * n,
        out_shape=plan.out_shape, scratch_shapes=plan.scratch,
    )(*ws)


class ScatterPlan:
    def __init__(self, gs):
        self.shapes = [g.shape[1:] for g in gs]
        self.n = len(gs)
        self.out_shape = [jax.ShapeDtypeStruct((N_DEV,) + _half_shape(g.shape[1:]), g.dtype) for g in gs]
        self.scratch = [pltpu.SemaphoreType.DMA((7 * self.n,)), pltpu.SemaphoreType.DMA((7 * self.n,)),
                        pltpu.SemaphoreType.DMA((self.n,))]

    def _copies(self, ins, outs, sems):
        x, y, c = _place()
        me = 4 * x + 2 * y + c
        send_sems, recv_sems, local_sems = sems
        local, sends, arrivals = [], [], []
        for a in range(self.n):
            piece = lambda px, py, pc, a=a: ins[a].at[(2 * px + py,) + _half(self.shapes[a], pc)]
            local.append(pltpu.make_async_copy(piece(x, y, c), outs[a].at[me], local_sems.at[a]))
            for k in range(1, N_DEV):
                px, py, pc = _flip(x, k & 4), _flip(y, k & 2), _flip(c, k & 1)
                sem = dict(send_sem=send_sems.at[7 * a + k - 1], recv_sem=recv_sems.at[7 * a + k - 1],
                           device_id=(px, py, pc), device_id_type=MESH)
                sends.append(pltpu.make_async_remote_copy(
                    src_ref=piece(px, py, pc), dst_ref=outs[a].at[me], **sem))
                slot = outs[a].at[4 * px + 2 * py + pc]
                arrivals.append(pltpu.make_async_remote_copy(src_ref=slot, dst_ref=slot, **sem))
        return local, sends, arrivals

    def start(self, ins, outs, sems):
        local, sends, _ = self._copies(ins, outs, sems)
        for cp in local + sends:
            cp.start()

    def finish(self, ins, outs, sems):
        local, sends, arrivals = self._copies(ins, outs, sems)
        for cp in arrivals:
            cp.wait_recv()
        for cp in sends:
            cp.wait_send()
        for cp in local:
            cp.wait()


def _row_tile(rows, row_bytes, budget, multiple):
    fits = [r for r in range(multiple, rows + 1, multiple) if rows % r == 0 and r * row_bytes <= budget]
    return max(fits) if fits else rows


SMALL_KERNEL_VMEM = 32 * 1024 * 1024


def sum_slots(lands, name):
    k = len(lands)
    _, rows, cols = lands[0].shape
    tr = _row_tile(rows, k * 2 * (N_DEV * cols * 2 + cols * 4), SMALL_KERNEL_VMEM, BF16_ROWS)

    def body(*refs):
        for l_ref, o_ref in zip(refs[:k], refs[k:]):
            acc = l_ref[0].astype(F32)
            for s in range(1, N_DEV):
                acc = acc + l_ref[s].astype(F32)
            o_ref[...] = acc

    return pl.pallas_call(
        body, name=name, grid=(rows // tr,),
        in_specs=[pl.BlockSpec((N_DEV, tr, cols), lambda i: (0, i, 0))] * k,
        out_specs=[pl.BlockSpec((tr, cols), lambda i: (i, 0))] * k,
        out_shape=[jax.ShapeDtypeStruct((rows, cols), F32)] * k,
        compiler_params=_params(("parallel",)),
    )(*lands)


def join_halves(hs, shapes):
    n = len(hs)

    def body(*refs):
        ins, outs = refs[:n], refs[n:2 * n]
        send_sems, recv_sems = refs[2 * n:]
        x, y, c = _place()
        sends = []
        for a in range(n):
            mine = outs[a].at[_half(shapes[a], c)]
            sends.append(pltpu.make_async_remote_copy(
                src_ref=ins[a], dst_ref=mine, send_sem=send_sems.at[a], recv_sem=recv_sems.at[a],
                device_id=(x, y, 1 - c), device_id_type=MESH))
            sends[-1].start()
        for a in range(n):
            theirs = outs[a].at[_half(shapes[a], 1 - c)]
            pltpu.make_async_remote_copy(
                src_ref=theirs, dst_ref=theirs, send_sem=send_sems.at[a], recv_sem=recv_sems.at[a],
                device_id=(x, y, 1 - c), device_id_type=MESH).wait_recv()
        for cp in sends:
            cp.wait_send()

    return pl.pallas_call(
        body, name="join_halves",
        in_specs=[ANY] * n, out_specs=[ANY] * n,
        out_shape=[jax.ShapeDtypeStruct(tuple(s), h.dtype) for s, h in zip(shapes, hs)],
        scratch_shapes=[pltpu.SemaphoreType.DMA((n,)), pltpu.SemaphoreType.DMA((n,))],
    )(*hs)


def allreduce_small(vec):
    def body(v_ref, o_ref, land_ref, send_sems, recv_sems):
        x, y, c = _place()
        me = 4 * x + 2 * y + c
        land_ref[me] = v_ref[...]
        sends = []
        for k in range(1, N_DEV):
            px, py, pc = _flip(x, k & 4), _flip(y, k & 2), _flip(c, k & 1)
            sends.append(pltpu.make_async_remote_copy(
                src_ref=v_ref, dst_ref=land_ref.at[me], send_sem=send_sems.at[k - 1], recv_sem=recv_sems.at[k - 1],
                device_id=(px, py, pc), device_id_type=MESH))
            sends[-1].start()
        for k in range(1, N_DEV):
            px, py, pc = _flip(x, k & 4), _flip(y, k & 2), _flip(c, k & 1)
            slot = land_ref.at[4 * px + 2 * py + pc]
            pltpu.make_async_remote_copy(
                src_ref=slot, dst_ref=slot, send_sem=send_sems.at[k - 1], recv_sem=recv_sems.at[k - 1],
                device_id=(px, py, pc), device_id_type=MESH).wait_recv()
        for cp in sends:
            cp.wait_send()
        acc = land_ref[0]
        for s in range(1, N_DEV):
            acc = acc + land_ref[s]
        o_ref[...] = acc

    vm = pl.BlockSpec(memory_space=pltpu.VMEM)
    return pl.pallas_call(
        body, name="allreduce_small",
        in_specs=[vm], out_specs=vm,
        out_shape=jax.ShapeDtypeStruct(vec.shape, F32),
        scratch_shapes=[pltpu.VMEM((N_DEV,) + vec.shape, F32), pltpu.SemaphoreType.DMA((N_DEV - 1,)),
                        pltpu.SemaphoreType.DMA((N_DEV - 1,))],
    )(vec)


def adamw(ws, gs, ms, vs, name):
    k = len(ws)
    rows, cols = ws[0].shape
    tr = _row_tile(rows, k * 2 * 7 * cols * 4, SMALL_KERNEL_VMEM, 8)
    c1 = 1.0 - ADAM_B1 ** ADAM_STEP
    c2 = 1.0 - ADAM_B2 ** ADAM_STEP

    def body(*refs):
        for a in range(k):
            w_ref, g_ref, m_ref, v_ref = (refs[s * k + a] for s in range(4))
            d_ref, nm_ref, nv_ref = (refs[(4 + s) * k + a] for s in range(3))
            gv = g_ref[...]
            nm = ADAM_B1 * m_ref[...] + (1.0 - ADAM_B1) * gv
            nv = ADAM_B2 * v_ref[...] + (1.0 - ADAM_B2) * (gv * gv)
            nm_ref[...] = nm
            nv_ref[...] = nv
            d_ref[...] = -ADAM_LR * ((nm / c1) / (jnp.sqrt(nv / c2) + ADAM_EPS) + ADAM_WD * w_ref[...])

    blk = pl.BlockSpec((tr, cols), lambda i: (i, 0))
    out = pl.pallas_call(
        body, name=name, grid=(rows // tr,),
        in_specs=[blk] * (4 * k), out_specs=[blk] * (3 * k),
        out_shape=[jax.ShapeDtypeStruct((rows, cols), F32)] * (3 * k),
        compiler_params=_params(("parallel",)),
    )(*ws, *gs, *ms, *vs)
    return [(out[a], out[k + a], out[2 * k + a]) for a in range(k)]


BIG = ("ffn1_w_gate", "ffn1_w_up", "ffn1_w_down", "w_in", "b_w_uq", "b_w_ukv", "w_out",
       "ffn2_w_gate", "ffn2_w_up", "ffn2_w_down")
SMALL = ("ffn1_norm", "mix_norm", "a_q_norm", "a_k_norm", "a_rel_bias", "b_q_lat_norm", "b_kv_lat_norm",
         "b_q_nope_norm", "b_q_rope_norm", "b_k_nope_norm", "b_k_rope_norm", "ffn2_norm", "final_norm")
WEIGHTS = ("ffn1_norm", "ffn1_w_gate", "ffn1_w_up", "ffn1_w_down", "mix_norm", "w_in", "a_q_norm", "a_k_norm",
           "a_rel_bias", "b_q_lat_norm", "b_w_uq", "b_kv_lat_norm", "b_w_ukv", "b_q_nope_norm", "b_q_rope_norm",
           "b_k_nope_norm", "b_k_rope_norm", "w_out", "ffn2_norm", "ffn2_w_gate", "ffn2_w_up", "ffn2_w_down",
           "final_norm")
TRANSPOSED = ("ffn1_w_gate", "ffn1_w_up", "ffn2_w_gate", "ffn2_w_up", "w_in", "b_w_uq")
PACK_SHAPE = (8, 1024)


def _pack_small(d, last=None):
    flat = [d[n].reshape(-1) for n in SMALL]
    used = sum(f.shape[0] for f in flat)
    total = PACK_SHAPE[0] * PACK_SHAPE[1]
    tail = jnp.zeros((total - used - 1,), F32)
    end = jnp.zeros((1,), F32) if last is None else last.reshape(1)
    return jnp.concatenate(flat + [tail, end]).reshape(PACK_SHAPE)


def _unpack_small(p, like):
    flat = p.reshape(-1)
    out, off = {}, 0
    for n in SMALL:
        size = like[n].size
        out[n] = flat[off:off + size].reshape(like[n].shape)
        off += size
    return out, flat[-1]


def _cols_to_shards(g):
    rows, cols = g.shape
    return g.reshape(rows, N_SHARD, cols // N_SHARD).transpose(1, 0, 2)


def _shards_to_cols(g):
    return g.transpose(1, 0, 2).reshape(g.shape[1], -1)


def kernel(x, ffn1_norm, ffn1_w_gate, ffn1_w_up, ffn1_w_down, mix_norm, w_in, a_q_norm, a_k_norm, a_rel_bias, b_q_lat_norm, b_w_uq, b_kv_lat_norm, b_w_ukv, b_q_nope_norm, b_q_rope_norm, b_k_nope_norm, b_k_rope_norm, w_out, ffn2_norm, ffn2_w_gate, ffn2_w_up, ffn2_w_down, final_norm, loss_target, m_ffn1_norm, m_ffn1_w_gate, m_ffn1_w_up, m_ffn1_w_down, m_mix_norm, m_w_in, m_a_q_norm, m_a_k_norm, m_a_rel_bias, m_b_q_lat_norm, m_b_w_uq, m_b_kv_lat_norm, m_b_w_ukv, m_b_q_nope_norm, m_b_q_rope_norm, m_b_k_nope_norm, m_b_k_rope_norm, m_w_out, m_ffn2_norm, m_ffn2_w_gate, m_ffn2_w_up, m_ffn2_w_down, m_final_norm, v_ffn1_norm, v_ffn1_w_gate, v_ffn1_w_up, v_ffn1_w_down, v_mix_norm, v_w_in, v_a_q_norm, v_a_k_norm, v_a_rel_bias, v_b_q_lat_norm, v_b_w_uq, v_b_kv_lat_norm, v_b_w_ukv, v_b_q_nope_norm, v_b_q_rope_norm, v_b_k_nope_norm, v_b_k_rope_norm, v_w_out, v_ffn2_norm, v_ffn2_w_gate, v_ffn2_w_up, v_ffn2_w_down, v_final_norm):
    args = locals()
    view = lambda a, n: a[0].T if n in TRANSPOSED else a[0]
    wts = {n: view(args[n], n) for n in WEIGHTS}
    mom = {n: view(args["m_" + n], n) for n in WEIGHTS}
    var = {n: view(args["v_" + n], n) for n in WEIGHTS}

    shard = 2 * lax.axis_index("x") + lax.axis_index("y")
    core = lax.axis_index("c")
    first = [n for n in BIG if n not in LATE]
    own = [wts[n].astype(BF16) for n in first]
    w = {n: wts[n] if n == "a_rel_bias" else wts[n][None] for n in SMALL}
    w.update(kernel_layout({n: lax.dynamic_update_index_in_dim(got, mine, shard, 0)
                            for n, got, mine in zip(first, allgather_shards(own), own)}))

    loss, grad_x, g, landed = local_step(x[0], loss_target[0], w,
                                         late=([wts[n].astype(BF16) for n in LATE], shard))

    me = 2 * shard + core
    for n in first:
        piece = lax.dynamic_slice(g[n], (shard, core * (FS // 2), 0), (1, FS // 2, D_MODEL))
        landed[n] = lax.dynamic_update_slice(landed[n], piece, (me, 0, 0))
    groups = {}
    for n in BIG:
        groups.setdefault(wts[n].shape, []).append(n)
    half = {}
    for names in groups.values():
        half.update(zip(names, sum_slots([landed[n] for n in names], "sum_slots")))
    halves = [half[n] for n in BIG]
    shapes = [wts[n].shape for n in BIG]
    axes = [_split_axis(s) for s in shapes]
    grads = dict(zip(BIG, (lax.dynamic_update_slice_in_dim(got, mine, core * mine.shape[ax], ax)
                           for got, mine, ax in zip(join_halves(halves, shapes), halves, axes))))

    small_sum, loss_sum = _unpack_small(allreduce_small(_pack_small(g, loss[0, 0])), wts)
    grads.update(small_sum)

    delta, new_m, new_v = {}, {}, {}
    for names in groups.values():
        stepped = adamw(*([d[n] for n in names] for d in (wts, grads, mom, var)), "adamw")
        for n, (d_new, m_new, v_new) in zip(names, stepped):
            delta[n], new_m[n], new_v[n] = d_new, m_new, v_new
    (packed,) = adamw([_pack_small(wts)], [_pack_small(grads)], [_pack_small(mom)], [_pack_small(var)], "adamw_small")
    for dst, p in zip((delta, new_m, new_v), packed):
        dst.update(_unpack_small(p, wts)[0])

    lead = lambda d: [(d[n].T if n in TRANSPOSED else d[n])[None] for n in WEIGHTS]
    return (loss_sum, grad_x[None], *lead(grads), *lead(delta), *lead(new_m), *lead(new_v))
```

```python
import numpy as np
import jax
import jax.numpy as jnp
from jax import lax
from jax.experimental import pallas as pl
from jax.experimental.pallas import tpu as pltpu

F32 = jnp.float32
BF16 = jnp.bfloat16
EPS = 1e-6
NEG = -1e30

D_MODEL = 1024
D_FF = 2816
N_SHARD = 4
FS = D_FF // N_SHARD
CHUNK = 64
A_LEFT = 8
A_MAX_REL = 128
HEADS = 8
HD = 64
ROPE = 32
PROJ_W = 2048
IN_COLS = 1952
B_SCALE = 96 ** -0.5
LANES = 128

ADAM_LR = 0.001
ADAM_B1 = 0.9
ADAM_B2 = 0.999
ADAM_EPS = 1e-08
ADAM_WD = 0.01
ADAM_STEP = 10

VMEM_LIMIT = 56 * 1024 * 1024
TOKEN_TILE = 512

MESH = pl.DeviceIdType.MESH


def _dot(a, b):
    return lax.dot_general(a, b, (((1,), (0,)), ((), ())), preferred_element_type=F32)


def _dot_nt(a, b):
    return lax.dot_general(a, b, (((1,), (1,)), ((), ())), preferred_element_type=F32)


def _dot_tn(a, b):
    return lax.dot_general(a, b, (((0,), (0,)), ((), ())), preferred_element_type=F32)


def _params(sem):
    return pltpu.CompilerParams(dimension_semantics=sem, vmem_limit_bytes=VMEM_LIMIT)


def _rms(xv):
    r = lax.rsqrt(jnp.mean(xv * xv, axis=-1, keepdims=True) + EPS)
    return r, xv * r


def ffn_fwd(x, g, wg, wu, wd, name, gather=(), loss_head=None, pre_proj=None):
    t, d = x.shape
    tm = TOKEN_TILE
    ni = t // tm
    plan = GatherPlan(gather)
    n = plan.n
    head = () if loss_head is None else tuple(loss_head)
    q = len(head)
    proj = () if pre_proj is None else tuple(pre_proj)
    r = len(proj)
    half = proj[0].shape[1] if r else 0

    def body(*refs):
        it = iter(refs)
        take = lambda count: [next(it) for _ in range(count)]
        x_ref, g_ref, wg_ref, wu_ref, wd_ref = take(5)
        head_in, proj_in, ins = take(q), take(r), take(n)
        o_ref, gp_ref, up_ref = take(3)
        head_out, x_out, outs = take(q), take(1 if r else 0), take(n)
        h_ref, acc_ref = take(2)
        sems = list(it)
        i, j = pl.program_id(0), pl.program_id(1)
        if n:
            pl.when((i == 0) & (j == 0))(lambda: plan.start(ins, outs, sems))
            pl.when((i == (3 * ni) // 4) & (j == 0))(lambda: plan.forward(ins, outs, sems))

        @pl.when(j == 0)
        def _():
            xin = x_ref[...]
            if r:
                oa_ref, obt_ref, wo_ref = proj_in
                xin = xin + _dot(oa_ref[...], wo_ref[0:half, :]) + _dot_tn(obt_ref[...], wo_ref[half:2 * half, :])
                x_out[0][...] = xin
            _, xn = _rms(xin)
            h_ref[...] = (xn * g_ref[...]).astype(BF16)
            acc_ref[...] = jnp.zeros_like(acc_ref)

        h = h_ref[...]
        gp = _dot_nt(h, wg_ref[0])
        up = _dot_nt(h, wu_ref[0])
        gp_ref[0] = gp
        up_ref[0] = up
        a = (gp * jax.nn.sigmoid(gp) * up).astype(BF16)
        acc_ref[...] += _dot(a, wd_ref[0])

        @pl.when(j == N_SHARD - 1)
        def _():
            y = (x_out[0][...] if r else x_ref[...]) + 0.5 * acc_ref[...]
            if not q:
                o_ref[...] = y
                return
            (gf_ref, t_ref), (dgf_ref, loss_ref) = head_in, head_out
            rinv, yn = _rms(y)
            gf = gf_ref[...]
            e = yn * gf - t_ref[...]
            dout = e * (1.0 / d)
            dng = dout * gf
            o_ref[...] = rinv * (dng - yn * jnp.mean(dng * yn, axis=-1, keepdims=True))

            @pl.when(i == 0)
            def _():
                dgf_ref[...] = jnp.zeros_like(dgf_ref)
                loss_ref[...] = jnp.zeros_like(loss_ref)

            dgf_ref[...] += jnp.sum(dout * yn, axis=0, keepdims=True)
            part = jnp.sum(jnp.sum(e * e, axis=-1, keepdims=True), axis=0, keepdims=True) * (0.5 / d)
            loss_ref[...] += jnp.broadcast_to(part, loss_ref.shape)

        if n:
            pl.when((i == ni - 1) & (j == N_SHARD - 1))(lambda: plan.finish(ins, outs, sems))

    tok = pl.BlockSpec((tm, d), lambda i, j: (i, 0))
    vec = pl.BlockSpec((1, d), lambda i, j: (0, 0))
    chunk = pl.BlockSpec((1, FS, d), lambda i, j: (j, 0, 0))
    pre = pl.BlockSpec((1, tm, FS), lambda i, j: (j, i, 0))
    return pl.pallas_call(
        body, name=name, grid=(ni, N_SHARD),
        in_specs=[tok, vec, chunk, chunk, chunk] + [vec, tok][:q]
        + [pl.BlockSpec((tm, half), lambda i, j: (i, 0)), pl.BlockSpec((half, tm), lambda i, j: (0, i)),
           pl.BlockSpec((2 * half, d), lambda i, j: (0, 0))][:r] + [ANY] * n,
        out_specs=[tok, pre, pre] + [vec, pl.BlockSpec((1, LANES), lambda i, j: (0, 0))][:q] + [tok][:r] + [ANY] * n,
        out_shape=[jax.ShapeDtypeStruct((t, d), F32), jax.ShapeDtypeStruct((N_SHARD, t, FS), F32),
                   jax.ShapeDtypeStruct((N_SHARD, t, FS), F32)]
        + [jax.ShapeDtypeStruct((1, d), F32), jax.ShapeDtypeStruct((1, LANES), F32)][:q]
        + [jax.ShapeDtypeStruct((t, d), F32)][:r] + plan.out_shape,
        scratch_shapes=[pltpu.VMEM((tm, d), BF16), pltpu.VMEM((tm, d), F32)] + (plan.scratch if n else []),
        compiler_params=_params(("arbitrary", "arbitrary")),
    )(x, g, wg, wu, wd, *head, *proj, *gather)


def ffn_bwd(x, dout, g, wg, wu, wd, gate, up_pre, name, scatter=(), spread=None):
    t, d = x.shape
    tm = TOKEN_TILE
    ni = t // tm
    hf = FS // 2
    plan = ScatterPlan(scatter)
    m = plan.n
    k = 0 if spread is None else 3
    steps = jnp.arange(N_SHARD, dtype=jnp.int32)
    order = steps if spread is None else (spread + 1 + steps) % N_SHARD

    def body(*refs):
        ord_ref, x_ref, do_ref, g_ref, wg_ref, wu_ref, wd_ref, gp_ref, up_ref = refs[:9]
        ins, (dwg_out, dwu_out, dwd_out, dhp_ref) = refs[9:9 + m], refs[9 + m:13 + m]
        outs, lands = refs[13 + m:13 + 2 * m], refs[13 + 2 * m:13 + 2 * m + k]
        dwg_ref, dwu_ref, dwd_ref = refs[13 + 2 * m + k:16 + 2 * m + k]
        sems = refs[16 + 2 * m + k:19 + 2 * m + k] if m else ()
        stage_ref = refs[-3] if k else None
        j, i = pl.program_id(0), pl.program_id(1)
        if m:
            pl.when((j == 0) & (i == 0))(lambda: plan.start(ins, outs, sems))

        def chunk_copies(jj):
            send_sems, recv_sems = refs[-2:]
            px, py, pc = _place()
            me = 4 * px + 2 * py + pc
            tx, ty = ord_ref[jj] // 2, ord_ref[jj] % 2
            copies = []
            for n_ in range(3):
                for h_ in range(2):
                    copies.append((pltpu.make_async_remote_copy(
                        src_ref=stage_ref.at[n_, pl.ds(h_ * hf, hf)], dst_ref=lands[n_].at[me],
                        send_sem=send_sems.at[6 * jj + 2 * n_ + h_], recv_sem=recv_sems.at[3 * me + n_],
                        device_id=(tx, ty, h_), device_id_type=MESH), (tx != px) | (ty != py) | (pc != h_)))
            return copies

        def arrivals():
            send_sems, recv_sems = refs[-2:]
            px, py, pc = _place()
            me = 4 * px + 2 * py + pc
            for s_ in range(N_DEV):
                for n_ in range(3):
                    slot = lands[n_].at[s_]
                    cp = pltpu.make_async_remote_copy(
                        src_ref=slot, dst_ref=slot, send_sem=send_sems.at[0], recv_sem=recv_sems.at[3 * s_ + n_],
                        device_id=(px, py, pc), device_id_type=MESH)
                    pl.when(me != s_)(cp.wait_recv)

        _, xn = _rms(x_ref[...])
        h = (xn * g_ref[...]).astype(BF16)
        dz = (0.5 * do_ref[...]).astype(BF16)
        wgv, wuv, wdv = wg_ref[0], wu_ref[0], wd_ref[0]
        gp, up = gp_ref[0], up_ref[0]
        s = jax.nn.sigmoid(gp)
        sg = gp * s
        a = (sg * up).astype(BF16)
        da = _dot_nt(dz, wdv)
        dup = (da * sg).astype(BF16)
        dgp = (da * up * (s * (1.0 + gp * (1.0 - s)))).astype(BF16)

        @pl.when(i == 0)
        def _():
            dwg_ref[...] = jnp.zeros_like(dwg_ref)
            dwu_ref[...] = jnp.zeros_like(dwu_ref)
            dwd_ref[...] = jnp.zeros_like(dwd_ref)

        dwd_ref[...] += _dot_tn(a, dz)
        dwg_ref[...] += _dot_tn(dgp, h)
        dwu_ref[...] += _dot_tn(dup, h)
        dhp_ref[0] = (_dot(dgp, wgv) + _dot(dup, wuv)).astype(BF16)

        @pl.when(i == ni - 1)
        def _():
            dwg_out[0] = dwg_ref[...].astype(BF16)
            dwu_out[0] = dwu_ref[...].astype(BF16)
            dwd_out[0] = dwd_ref[...].astype(BF16)
            if k:
                @pl.when(j >= 1)
                def _():
                    for cp, leaves in chunk_copies(j - 1):
                        pl.when(leaves)(cp.wait_send)
                for n_, acc in enumerate((dwg_ref, dwu_ref, dwd_ref)):
                    stage_ref[n_] = acc[...].astype(BF16)
                for cp, leaves in chunk_copies(j):
                    pl.when(leaves)(cp.start)

                @pl.when(j == N_SHARD - 1)
                def _():
                    for cp, leaves in chunk_copies(N_SHARD - 1):
                        pl.when(leaves)(cp.wait_send)
                    arrivals()

        if m:
            pl.when((j == N_SHARD - 1) & (i == ni - 1))(lambda: plan.finish(ins, outs, sems))

    chunk = pl.BlockSpec((1, FS, d), lambda j, i, o: (o[j], 0, 0))
    tok = pl.BlockSpec((tm, d), lambda j, i, o: (i, 0))
    pre = pl.BlockSpec((1, tm, FS), lambda j, i, o: (o[j], i, 0))
    grid_spec = pltpu.PrefetchScalarGridSpec(
        num_scalar_prefetch=1, grid=(N_SHARD, ni),
        in_specs=[tok, tok, pl.BlockSpec((1, d), lambda j, i, o: (0, 0)), chunk, chunk, chunk, pre, pre] + [ANY] * m,
        out_specs=[chunk, chunk, chunk, pl.BlockSpec((1, tm, d), lambda j, i, o: (o[j], i, 0))] + [ANY] * (m + k),
        scratch_shapes=[pltpu.VMEM((FS, d), F32), pltpu.VMEM((FS, d), F32), pltpu.VMEM((FS, d), F32)]
        + (plan.scratch if m else [])
        + ([pltpu.VMEM((3, FS, d), BF16), pltpu.SemaphoreType.DMA((6 * N_SHARD,)),
            pltpu.SemaphoreType.DMA((3 * N_DEV,))] if k else []))
    return pl.pallas_call(
        body, name=name, grid_spec=grid_spec,
        out_shape=[jax.ShapeDtypeStruct((N_SHARD, FS, d), BF16),
                   jax.ShapeDtypeStruct((N_SHARD, FS, d), BF16),
                   jax.ShapeDtypeStruct((N_SHARD, FS, d), BF16),
                   jax.ShapeDtypeStruct((N_SHARD, t, d), BF16)] + plan.out_shape
        + [jax.ShapeDtypeStruct((N_DEV, hf, d), BF16)] * k,
        compiler_params=_params(("arbitrary", "arbitrary")),
    )(order, x, dout, g, wg, wu, wd, gate, up_pre, *scatter)


def norm_bwd(x, g, dhp, dres, name):
    t, d = x.shape
    p = dhp.shape[0]
    tm = TOKEN_TILE

    def body(x_ref, g_ref, dhp_ref, dres_ref, dx_ref, dg_ref):
        i = pl.program_id(0)
        r, xn = _rms(x_ref[...])
        dh = dhp_ref[0].astype(F32)
        for q in range(1, p):
            dh = dh + dhp_ref[q].astype(F32)
        dhg = dh * g_ref[...]
        dx_ref[...] = dres_ref[...] + r * (dhg - xn * jnp.mean(dhg * xn, axis=-1, keepdims=True))

        @pl.when(i == 0)
        def _():
            dg_ref[...] = jnp.zeros_like(dg_ref)

        dg_ref[...] += jnp.sum(dh * xn, axis=0, keepdims=True)

    return pl.pallas_call(
        body, name=name, grid=(t // tm,),
        in_specs=[pl.BlockSpec((tm, d), lambda i: (i, 0)),
                  pl.BlockSpec((1, d), lambda i: (0, 0)),
                  pl.BlockSpec((p, tm, d), lambda i: (0, i, 0)),
                  pl.BlockSpec((tm, d), lambda i: (i, 0))],
        out_specs=[pl.BlockSpec((tm, d), lambda i: (i, 0)),
                   pl.BlockSpec((1, d), lambda i: (0, 0))],
        out_shape=[jax.ShapeDtypeStruct((t, d), F32), jax.ShapeDtypeStruct((1, d), F32)],
        compiler_params=_params(("arbitrary",)),
    )(x, g, dhp, dres)


def out_proj_bwd(x, g, dhp, dres, w, oa, ob_t):
    t, d = x.shape
    half = w.shape[0] // 2
    p = dhp.shape[0]
    tm = TOKEN_TILE
    ni = t // tm

    def body(x_ref, g_ref, dhp_ref, dres_ref, w_ref, oa_ref, obt_ref,
             dx_ref, dg_ref, da_ref, db_ref, dbt_ref, dw_ref, acc_ref):
        i = pl.program_id(0)

        @pl.when(i == 0)
        def _():
            dg_ref[...] = jnp.zeros_like(dg_ref)
            acc_ref[...] = jnp.zeros_like(acc_ref)

        r, xn = _rms(x_ref[...])
        dh = dhp_ref[0].astype(F32)
        for s in range(1, p):
            dh = dh + dhp_ref[s].astype(F32)
        dhg = dh * g_ref[...]
        dx = dres_ref[...] + r * (dhg - xn * jnp.mean(dhg * xn, axis=-1, keepdims=True))
        dx_ref[...] = dx
        dg_ref[...] += jnp.sum(dh * xn, axis=0, keepdims=True)
        dxb = dx.astype(BF16)
        da_ref[...] = _dot_nt(dxb, w_ref[0:half, :]).astype(BF16)
        db_ref[...] = _dot_nt(dxb, w_ref[half:2 * half, :]).astype(BF16)
        dbt_ref[...] = _dot_nt(w_ref[half:2 * half, :], dxb).astype(BF16)
        acc_ref[0:half, :] += _dot_tn(oa_ref[...], dxb)
        acc_ref[half:2 * half, :] += _dot(obt_ref[...], dxb)

        @pl.when(i == ni - 1)
        def _():
            dw_ref[...] = acc_ref[...].astype(BF16)

    row = lambda w_: pl.BlockSpec((tm, w_), lambda i: (i, 0))
    col = pl.BlockSpec((half, tm), lambda i: (0, i))
    whole = pl.BlockSpec((2 * half, d), lambda i: (0, 0))
    vec = pl.BlockSpec((1, d), lambda i: (0, 0))
    return pl.pallas_call(
        body, name="out_proj_bwd", grid=(ni,),
        in_specs=[row(d), vec, pl.BlockSpec((p, tm, d), lambda i: (0, i, 0)), row(d), whole, row(half), col],
        out_specs=[row(d), vec, row(half), row(half), col, whole],
        out_shape=[jax.ShapeDtypeStruct((t, d), F32), jax.ShapeDtypeStruct((1, d), F32),
                   jax.ShapeDtypeStruct((t, half), BF16), jax.ShapeDtypeStruct((t, half), BF16),
                   jax.ShapeDtypeStruct((half, t), BF16), jax.ShapeDtypeStruct((2 * half, d), BF16)],
        scratch_shapes=[pltpu.VMEM((2 * half, d), F32)],
        compiler_params=_params(("arbitrary",)),
    )(x, g, dhp, dres, w, oa, ob_t)


def _lane(shape):
    return lax.broadcasted_iota(jnp.int32, shape, 1)


PAIR = (0, HD, LANES)
Q_HEAD = (0, HD, HD + ROPE, LANES)
K_ROPE = (0, ROPE, LANES)


def _seg_mean(z, bounds):
    seg = lambda v: sum([(v >= b).astype(jnp.int32) for b in bounds[1:-1]], jnp.zeros_like(v))
    rows = seg(lax.broadcasted_iota(jnp.int32, (LANES, LANES), 0))
    cols = seg(lax.broadcasted_iota(jnp.int32, (LANES, LANES), 1))
    same = (rows == cols).astype(BF16)
    lane = _lane((1, LANES))
    inv = sum([jnp.where((lane >= a) & (lane < b), 1.0 / (b - a), 0.0) for a, b in zip(bounds[:-1], bounds[1:])])
    hi = z.astype(BF16)
    lo = (z - hi.astype(F32)).astype(BF16)
    return (_dot(hi, same) + _dot(lo, same)) * inv


def _seg_norm(x, bounds):
    r = lax.rsqrt(_seg_mean(x * x, bounds) + EPS)
    return r, x * r


def _seg_norm_bwd(r, xn, dyg, bounds):
    return r * (dyg - xn * _seg_mean(dyg * xn, bounds))


A_TM = 256


def mix_fwd(x, g, w, gq, gk, gcq, gckv):
    t, d = x.shape
    tm = TOKEN_TILE

    def body(x_ref, g_ref, w_ref, gq_ref, gk_ref, gcq_ref, gckv_ref,
             h_ref, p_ref, qa_ref, ka_ref, va_ref, cq_ref, ckv_ref):
        _, xn = _rms(x_ref[...])
        h = (xn * g_ref[...]).astype(BF16)
        h_ref[...] = h
        p_ref[...] = _dot_nt(h, w_ref[...])
        for p in range(4):
            sl = slice(LANES * p, LANES * (p + 1))
            _, xn = _seg_norm(p_ref[:, sl], PAIR)
            qa_ref[:, sl] = (xn * gq_ref[:, sl] * 0.125).astype(BF16)
            _, xn = _seg_norm(p_ref[:, 512 + LANES * p:512 + LANES * (p + 1)], PAIR)
            ka_ref[:, sl] = (xn * gk_ref[:, sl]).astype(BF16)
        va_ref[...] = p_ref[:, 1024:1536].astype(BF16)
        _, xn = _rms(p_ref[:, 1536:1792])
        cq_ref[...] = (xn * gcq_ref[...]).astype(BF16)
        _, xn = _rms(p_ref[:, 1792:1920])
        ckv_ref[...] = (xn * gckv_ref[...]).astype(BF16)

    row = lambda w: pl.BlockSpec((tm, w), lambda i: (i, 0))
    vec = lambda w: pl.BlockSpec((1, w), lambda i: (0, 0))
    return pl.pallas_call(
        body, name="mix_fwd", grid=(t // tm,),
        in_specs=[row(d), vec(d), pl.BlockSpec((PROJ_W, d), lambda i: (0, 0)), vec(512), vec(512), vec(256), vec(128)],
        out_specs=[row(d), row(PROJ_W), row(512), row(512), row(512), row(256), row(128)],
        out_shape=[jax.ShapeDtypeStruct((t, d), BF16), jax.ShapeDtypeStruct((t, PROJ_W), F32)]
        + [jax.ShapeDtypeStruct((t, w_), BF16) for w_ in (512, 512, 512, 256, 128)],
        compiler_params=_params(("parallel",)),
    )(x, g, w, gq, gk, gcq, gckv)


def mix_bwd(proj, x, h, dres, w, g, dqa, dkp, dvp, dcq, dckv, dkr, gq, gk, gcq, gckv):
    t, d = x.shape
    tm = A_TM
    nb = t // tm

    def body(p_ref, x_ref, h_ref, dres_ref, w_ref, g_ref, dqa_ref, dk0_ref, dk1_ref, dk2_ref, dv0_ref, dv1_ref,
             dv2_ref, dcq_ref, dckv_ref, dkr_ref, gq_ref, gk_ref, gcq_ref, gckv_ref,
             dx_ref, dw_ref, dg_ref, dgq_ref, dgk_ref, dgcq_ref, dgckv_ref, dp_ref, acc_ref):
        i = pl.program_id(0)

        @pl.when(i == 0)
        def _():
            for ref in (acc_ref, dg_ref, dgq_ref, dgk_ref, dgcq_ref, dgckv_ref):
                ref[...] = jnp.zeros_like(ref)

        has1 = (i + 1 < nb).astype(F32)
        has2 = (i + 2 < nb).astype(F32)
        for p in range(4):
            sl = slice(LANES * p, LANES * (p + 1))
            r, xn = _seg_norm(p_ref[:, sl], PAIR)
            dy = dqa_ref[:, sl] * 0.125
            dp_ref[:, sl] = _seg_norm_bwd(r, xn, dy * gq_ref[:, sl], PAIR).astype(BF16)
            dgq_ref[:, sl] += jnp.sum(dy * xn, axis=0, keepdims=True)
            ks = slice(512 + LANES * p, 512 + LANES * (p + 1))
            r, xn = _seg_norm(p_ref[:, ks], PAIR)
            dy = dk0_ref[0, :, sl] + has1 * dk1_ref[0, :, sl] + has2 * dk2_ref[0, :, sl]
            dp_ref[:, ks] = _seg_norm_bwd(r, xn, dy * gk_ref[:, sl], PAIR).astype(BF16)
            dgk_ref[:, sl] += jnp.sum(dy * xn, axis=0, keepdims=True)
        dp_ref[:, 1024:1536] = (dv0_ref[0] + has1 * dv1_ref[0] + has2 * dv2_ref[0]).astype(BF16)
        for (a, b, dlat_ref, glat_ref, dglat_ref) in ((1536, 1792, dcq_ref, gcq_ref, dgcq_ref),
                                                      (1792, 1920, dckv_ref, gckv_ref, dgckv_ref)):
            r, xn = _rms(p_ref[:, a:b])
            dy = dlat_ref[...]
            dyg = dy * glat_ref[...]
            dp_ref[:, a:b] = (r * (dyg - xn * jnp.mean(dyg * xn, axis=-1, keepdims=True))).astype(BF16)
            dglat_ref[...] += jnp.sum(dy * xn, axis=0, keepdims=True)
        dp_ref[:, 1920:2048] = dkr_ref[...].astype(BF16)

        dproj = dp_ref[...]
        acc_ref[...] += _dot_tn(dproj, h_ref[...])
        dh = _dot(dproj, w_ref[...])
        r, xn = _rms(x_ref[...])
        dhg = dh * g_ref[...]
        dx_ref[...] = dres_ref[...] + r * (dhg - xn * jnp.mean(dhg * xn, axis=-1, keepdims=True))
        dg_ref[...] += jnp.sum(dh * xn, axis=0, keepdims=True)

        @pl.when(i == nb - 1)
        def _():
            dw_ref[...] = acc_ref[...].astype(BF16)

    row = lambda w_: pl.BlockSpec((tm, w_), lambda i: (i, 0))
    vec = lambda w_: pl.BlockSpec((1, w_), lambda i: (0, 0))
    part = lambda s: pl.BlockSpec((1, tm, 512), lambda i: (s, jnp.minimum(i + s, nb - 1), 0))
    whole = pl.BlockSpec((PROJ_W, d), lambda i: (0, 0))
    return pl.pallas_call(
        body, name="mix_bwd", grid=(nb,),
        in_specs=[row(PROJ_W), row(d), row(d), row(d), whole, vec(d), row(512), part(0), part(1), part(2),
                  part(0), part(1), part(2), row(256), row(128), row(128), vec(512), vec(512), vec(256), vec(128)],
        out_specs=[row(d), whole, vec(d), vec(512), vec(512), vec(256), vec(128)],
        out_shape=[jax.ShapeDtypeStruct((t, d), F32), jax.ShapeDtypeStruct((PROJ_W, d), BF16)]
        + [jax.ShapeDtypeStruct((1, w_), F32) for w_ in (d, 512, 512, 256, 128)],
        scratch_shapes=[pltpu.VMEM((tm, PROJ_W), BF16), pltpu.VMEM((PROJ_W, d), F32)],
        compiler_params=_params(("arbitrary",)),
    )(proj, x, h, dres, w, g, dqa, dkp, dkp, dkp, dvp, dvp, dvp, dcq, dckv, dkr, gq, gk, gcq, gckv)


def _roll(x, shift):
    return pltpu.roll(x, shift % LANES, 1)


def _rope(y, c, s1, s2):
    return y * c + _roll(y, -16) * s1 + _roll(y, 16) * s2


def _rope_bwd(d, c, s1, s2):
    return d * c + _roll(d * s1, 16) + _roll(d * s2, -16)


def _up_proj(cq_ref, ckv_ref, wuq_ref, wukv_ref):
    return _dot_nt(cq_ref[...], wuq_ref[...]), _dot(ckv_ref[...], wukv_ref[...])


def prep2_fwd(cqn, ckvn, proj, wuq, wukv, gq, gk, gkr, tabs):
    t = cqn.shape[0]
    tm = TOKEN_TILE

    def body(cq_ref, ckv_ref, kr_ref, wuq_ref, wukv_ref, gq_ref, gk_ref, gkr_ref, tab_ref, qf_ref, kf_ref, vp_ref):
        q_all, kv_all = _up_proj(cq_ref, ckv_ref, wuq_ref, wukv_ref)
        _, xn = _seg_norm(kr_ref[...], K_ROPE)
        kpe = _roll(_rope(xn * gkr_ref[...], tab_ref[3], tab_ref[4], tab_ref[5]), 64)
        for h in range(HEADS):
            sl = slice(LANES * h, LANES * (h + 1))
            _, xn = _seg_norm(q_all[:, sl], Q_HEAD)
            qf_ref[:, sl] = (_rope(xn * gq_ref[...], tab_ref[0], tab_ref[1], tab_ref[2]) * B_SCALE2).astype(BF16)
            x = kv_all[:, sl]
            lo = _lane(x.shape) < HD
            _, xkn = _seg_norm(jnp.where(lo, x, 0.0), PAIR)
            kf_ref[:, sl] = (xkn * gk_ref[...] + kpe).astype(BF16)
            if h % 2 == 0:
                v_even = _roll(x, 64)
            else:
                vp_ref[:, LANES * (h // 2):LANES * (h // 2 + 1)] = jnp.where(lo, v_even, x).astype(BF16)

    row = lambda w: pl.BlockSpec((tm, w), lambda i: (i, 0))
    vec = lambda w: pl.BlockSpec((1, w), lambda i: (0, 0))
    full = lambda a: pl.BlockSpec(a.shape, lambda i: (0, 0))
    return pl.pallas_call(
        body, name="prep2_fwd", grid=(t // tm,),
        in_specs=[row(256), row(128), pl.BlockSpec((tm, LANES), lambda i: (i, 15)), full(wuq), full(wukv),
                  vec(128), vec(128), vec(128), pl.BlockSpec((6, tm, LANES), lambda i: (0, i, 0))],
        out_specs=[row(1024), row(1024), row(512)],
        out_shape=[jax.ShapeDtypeStruct((t, 1024), BF16), jax.ShapeDtypeStruct((t, 1024), BF16),
                   jax.ShapeDtypeStruct((t, 512), BF16)],
        compiler_params=_params(("parallel",)),
    )(cqn, ckvn, proj, wuq, wukv, gq, gk, gkr, tabs)


def prep2_bwd(cqn, ckvn, proj, wuq, wukv, dqf, dkf, dvp, gq, gk, gkr, tabs):
    t = cqn.shape[0]
    tm = TOKEN_TILE

    def body(cq_ref, ckv_ref, kr_ref, wuq_ref, wukv_ref, dqf_ref, dkf_ref, dvp_ref, gq_ref, gk_ref, gkr_ref, tab_ref,
             dcq_ref, dckv_ref, dkr_ref, dwuq_ref, dwukv_ref, dgq_ref, dgk_ref, dgkr_ref, dq_ref, dkv_ref):
        i = pl.program_id(0)

        @pl.when(i == 0)
        def _():
            for ref in (dwuq_ref, dwukv_ref, dgq_ref, dgk_ref, dgkr_ref):
                ref[...] = jnp.zeros_like(ref)

        q_all, kv_all = _up_proj(cq_ref, ckv_ref, wuq_ref, wukv_ref)
        dgq = jnp.zeros((1, LANES), F32)
        dgk = jnp.zeros((1, LANES), F32)
        dkpe = jnp.zeros((tm, LANES), F32)
        for h in range(HEADS):
            sl = slice(LANES * h, LANES * (h + 1))
            lane = _lane((tm, LANES))
            mn, mr = lane < HD, (lane >= HD) & (lane < HD + ROPE)
            r, xn = _seg_norm(q_all[:, sl], Q_HEAD)
            dy = _rope_bwd(dqf_ref[sl, :].T, tab_ref[0], tab_ref[1], tab_ref[2])
            dyg = dy * gq_ref[...]
            dq_ref[:, sl] = _seg_norm_bwd(r, xn, dyg, Q_HEAD).astype(BF16)
            dgq = dgq + jnp.sum(dy * xn, axis=0, keepdims=True)

            x = kv_all[:, sl]
            dk = dkf_ref[:, sl]
            rk, xkn = _seg_norm(jnp.where(mn, x, 0.0), PAIR)
            dyk = jnp.where(mn, dk, 0.0)
            dxk = _seg_norm_bwd(rk, xkn, dyk * gk_ref[...], PAIR)
            dgk = dgk + jnp.sum(dyk * xkn, axis=0, keepdims=True)
            dkpe = dkpe + jnp.where(mr, dk, 0.0)
            dvpair = dvp_ref[:, LANES * (h // 2):LANES * (h // 2 + 1)]
            dv = _roll(dvpair, 64) if h % 2 == 0 else dvpair
            dkv_ref[:, sl] = jnp.where(mn, dxk, dv).astype(BF16)

        r, xn = _seg_norm(kr_ref[...], K_ROPE)
        dy = _rope_bwd(_roll(dkpe, 64), tab_ref[3], tab_ref[4], tab_ref[5])
        dkr_ref[...] = _seg_norm_bwd(r, xn, dy * gkr_ref[...], K_ROPE)
        dgq_ref[...] += dgq
        dgk_ref[...] += dgk
        dgkr_ref[...] += jnp.sum(dy * xn, axis=0, keepdims=True)
        dqb, dkvb = dq_ref[...], dkv_ref[...]
        dcq_ref[...] = _dot(dqb, wuq_ref[...])
        dckv_ref[...] = _dot_nt(dkvb, wukv_ref[...])
        dwuq_ref[...] += _dot_tn(dqb, cq_ref[...])
        dwukv_ref[...] += _dot_tn(ckv_ref[...], dkvb)

    row = lambda w: pl.BlockSpec((tm, w), lambda i: (i, 0))
    vec = lambda w: pl.BlockSpec((1, w), lambda i: (0, 0))
    full = lambda a: pl.BlockSpec(a.shape, lambda i: (0, 0))
    return pl.pallas_call(
        body, name="prep2_bwd", grid=(t // tm,),
        in_specs=[row(256), row(128), pl.BlockSpec((tm, LANES), lambda i: (i, 15)), full(wuq), full(wukv),
                  pl.BlockSpec((1024, tm), lambda i: (0, i)), row(1024), row(512),
                  vec(128), vec(128), vec(128), pl.BlockSpec((6, tm, LANES), lambda i: (0, i, 0))],
        out_specs=[row(256), row(128), row(128), full(wuq), full(wukv), vec(128), vec(128), vec(128)],
        out_shape=[jax.ShapeDtypeStruct((t, 256), F32), jax.ShapeDtypeStruct((t, 128), F32),
                   jax.ShapeDtypeStruct((t, LANES), F32), jax.ShapeDtypeStruct(wuq.shape, F32),
                   jax.ShapeDtypeStruct(wukv.shape, F32)] + [jax.ShapeDtypeStruct((1, LANES), F32)] * 3,
        scratch_shapes=[pltpu.VMEM((tm, 1024), BF16), pltpu.VMEM((tm, 1024), BF16)],
        compiler_params=_params(("arbitrary",)),
    )(cqn, ckvn, proj, wuq, wukv, dqf, dkf, dvp, gq, gk, gkr, tabs)


A_TQ = 256
A_WIN = 3 * A_TQ


def _a_specs(t):
    nb = t // A_TQ
    blk = lambda s: pl.BlockSpec((A_TQ, 512), lambda i: (jnp.maximum(i - s, 0), 0))
    return nb, blk


def _a_exp(q_ref, kc, b_ref, head, sl, lo):
    hm = lo if head % 2 == 0 else ~lo
    qm = jnp.where(hm, q_ref[:, sl], jnp.zeros((), BF16))
    s = _dot_nt(qm, kc) + b_ref[0, head]
    e = jnp.exp(s - jnp.max(s, axis=-1, keepdims=True))
    return hm, qm, e, 1.0 / jnp.sum(e, axis=-1, keepdims=True)


def _a_bias_spec():
    return pl.BlockSpec((1, HEADS, A_TQ, A_WIN), lambda i: (jnp.minimum(i, 2), 0, 0, 0))


def attn_a_fwd(qa, ka, va, bias):
    t = qa.shape[0]
    nb, blk = _a_specs(t)

    def body(q_ref, k2_ref, k1_ref, k0_ref, v2_ref, v1_ref, v0_ref, b_ref, o_ref):
        lo = _lane((A_TQ, LANES)) < HD
        for p in range(4):
            sl = slice(LANES * p, LANES * (p + 1))
            kc = jnp.concatenate([k2_ref[:, sl], k1_ref[:, sl], k0_ref[:, sl]], axis=0)
            vc = jnp.concatenate([v2_ref[:, sl], v1_ref[:, sl], v0_ref[:, sl]], axis=0)
            outs = []
            for h2 in range(2):
                _, _, e, inv = _a_exp(q_ref, kc, b_ref, 2 * p + h2, sl, lo)
                outs.append(_dot(e.astype(BF16), vc) * inv)
            o_ref[:, sl] = jnp.where(lo, outs[0], outs[1]).astype(BF16)

    return pl.pallas_call(
        body, name="attn_a_fwd", grid=(nb,),
        in_specs=[blk(0), blk(2), blk(1), blk(0), blk(2), blk(1), blk(0), _a_bias_spec()],
        out_specs=pl.BlockSpec((A_TQ, 512), lambda i: (i, 0)),
        out_shape=jax.ShapeDtypeStruct((t, 512), BF16),
        compiler_params=_params(("parallel",)),
    )(qa, ka, ka, ka, va, va, va, bias)


def attn_a_bwd(qa, ka, va, bias, do):
    t = qa.shape[0]
    nb, blk = _a_specs(t)

    def body(q_ref, k2_ref, k1_ref, k0_ref, v2_ref, v1_ref, v0_ref, b_ref, do_ref, dq_ref, dk_ref, dv_ref, db_ref):
        qb = pl.program_id(0)

        @pl.when(qb == 0)
        def _():
            db_ref[...] = jnp.zeros_like(db_ref)

        lo = _lane((A_TQ, LANES)) < HD
        for p in range(4):
            sl = slice(LANES * p, LANES * (p + 1))
            kc = jnp.concatenate([k2_ref[:, sl], k1_ref[:, sl], k0_ref[:, sl]], axis=0)
            vc = jnp.concatenate([v2_ref[:, sl], v1_ref[:, sl], v0_ref[:, sl]], axis=0)
            dqs = []
            dkt = jnp.zeros((LANES, A_WIN), F32)
            dvt = jnp.zeros((LANES, A_WIN), F32)
            for h2 in range(2):
                head = 2 * p + h2
                hm, qm, e, inv = _a_exp(q_ref, kc, b_ref, head, sl, lo)
                pr = e * inv
                dom = jnp.where(hm, do_ref[:, sl], jnp.zeros((), BF16))
                dp = _dot_nt(dom, vc)
                ds = pr * (dp - jnp.sum(pr * dp, axis=-1, keepdims=True))
                db_ref[head] += ds
                dsb = ds.astype(BF16)
                dqs.append(_dot(dsb, kc))
                dkt = dkt + _dot_tn(qm, dsb)
                dvt = dvt + _dot_tn(dom, pr.astype(BF16))
            dq_ref[:, sl] = jnp.where(lo, dqs[0], dqs[1])
            dkc, dvc = dkt.T, dvt.T
            for s in range(3):
                rows = slice(A_TQ * (2 - s), A_TQ * (3 - s))
                dk_ref[s, :, sl] = dkc[rows]
                dv_ref[s, :, sl] = dvc[rows]

    share = pl.BlockSpec((3, A_TQ, 512), lambda i: (0, i, 0))
    return pl.pallas_call(
        body, name="attn_a_bwd", grid=(nb,),
        in_specs=[blk(0), blk(2), blk(1), blk(0), blk(2), blk(1), blk(0), _a_bias_spec(), blk(0)],
        out_specs=[pl.BlockSpec((A_TQ, 512), lambda i: (i, 0)), share, share,
                   pl.BlockSpec((HEADS, A_TQ, A_WIN), lambda i: (0, 0, 0))],
        out_shape=[jax.ShapeDtypeStruct((t, 512), F32), jax.ShapeDtypeStruct((3, t, 512), F32),
                   jax.ShapeDtypeStruct((3, t, 512), F32), jax.ShapeDtypeStruct((HEADS, A_TQ, A_WIN), F32)],
        compiler_params=_params(("arbitrary",)),
    )(qa, ka, ka, ka, va, va, va, bias, do)


B_T = 1024


B_SCALE2 = B_SCALE * 1.4426950408889634
B_FWD_HEADS = 8
_B_ALL = slice(0, B_T)
_B_LO, _B_HI = slice(0, B_T // 2), slice(B_T // 2, B_T)
_B_DIAG = ((_B_LO, _B_LO), (_B_LO, _B_HI), (_B_HI, _B_HI))


def _tri_tables(n, by_query):
    pairs = [(i, j) for i in range(n) for j in range(i + 1)] if by_query else [(i, j) for j in range(n) for i in range(j, n)]
    return (np.asarray([p[0] for p in pairs], np.int32), np.asarray([p[1] for p in pairs], np.int32))


def _b_mask_t(s):
    kc = lax.broadcasted_iota(jnp.int32, s.shape, 0) // CHUNK
    qc = lax.broadcasted_iota(jnp.int32, s.shape, 1) // CHUNK
    return jnp.where(kc <= qc, s, NEG)


def attn_b_fwd(qf, kf, vp):
    t = qf.shape[0]
    n = t // B_T
    qtab, ktab = _tri_tables(n, by_query=True)
    hps = B_FWD_HEADS

    def body(qt_ref, kt_ref, q_ref, k_ref, v_ref, o_ref, lse_ref, m_s, l_s, acc_s):
        qb, kb = qt_ref[pl.program_id(1)], kt_ref[pl.program_id(1)]

        @pl.when(kb == 0)
        def _():
            m_s[...] = jnp.full_like(m_s, NEG)
            l_s[...] = jnp.zeros_like(l_s)
            acc_s[...] = jnp.zeros_like(acc_s)

        def block(kr, qr, masked):
            for h2 in range(hps):
                sl = slice(LANES * h2, LANES * (h2 + 1))
                v = v_ref[kr, LANES * (h2 // 2):LANES * (h2 // 2 + 1)]
                s = _dot_nt(k_ref[kr, sl], q_ref[qr, sl])
                if masked:
                    s = _b_mask_t(s)
                m_prev = m_s[h2, :, qr]
                m_new = jnp.maximum(m_prev, jnp.max(s, axis=0, keepdims=True))
                alpha = jnp.exp2(m_prev - m_new)
                pr = jnp.exp2(s - m_new)
                l_s[h2, :, qr] = alpha * l_s[h2, :, qr] + jnp.sum(pr, axis=0, keepdims=True)
                acc_s[h2, :, qr] = alpha * acc_s[h2, :, qr] + _dot_tn(v, pr.astype(BF16))
                m_s[h2, :, qr] = m_new

        @pl.when(kb < qb)
        def _():
            block(_B_ALL, _B_ALL, False)

        @pl.when(kb == qb)
        def _():
            for kr, qr in _B_DIAG:
                block(kr, qr, kr == qr)
            for h2 in range(hps):
                l = l_s[h2]
                rows = slice(HD * (h2 % 2), HD * (h2 % 2 + 1))
                o_ref[HD * h2:HD * (h2 + 1), :] = (acc_s[h2, rows, :] * (1.0 / l)).astype(BF16)
                lse_ref[h2 // 2, h2 % 2:h2 % 2 + 1, :] = m_s[h2] + jnp.log2(l)

    grid_spec = pltpu.PrefetchScalarGridSpec(
        num_scalar_prefetch=2, grid=(HEADS // hps, len(qtab)),
        in_specs=[pl.BlockSpec((B_T, LANES * hps), lambda p, s, qt, kt: (qt[s], p)),
                  pl.BlockSpec((B_T, LANES * hps), lambda p, s, qt, kt: (kt[s], p)),
                  pl.BlockSpec((B_T, HD * hps), lambda p, s, qt, kt: (kt[s], p))],
        out_specs=[pl.BlockSpec((HD * hps, B_T), lambda p, s, qt, kt: (p, qt[s])),
                   pl.BlockSpec((hps // 2, 2, B_T), lambda p, s, qt, kt: (p, 0, qt[s]))],
        scratch_shapes=[pltpu.VMEM((hps, 1, B_T), F32), pltpu.VMEM((hps, 1, B_T), F32),
                        pltpu.VMEM((hps, LANES, B_T), F32)])
    return pl.pallas_call(
        body, name="attn_b_fwd", grid_spec=grid_spec,
        out_shape=[jax.ShapeDtypeStruct((512, t), BF16), jax.ShapeDtypeStruct((4, 2, t), F32)],
        compiler_params=_params(("parallel", "arbitrary")),
    )(jnp.asarray(qtab), jnp.asarray(ktab), qf, kf, vp)


def attn_b_bwd(qf, kf, vp, do, do_t, o_t, lse, scatter=()):
    t = qf.shape[0]
    n = t // B_T
    qtab, ktab = _tri_tables(n, by_query=False)
    plan = ScatterPlan(scatter)
    m = plan.n
    last = len(qtab) - 1

    def body(*refs):
        qt_ref, kt_ref, q_ref, k_ref, v_ref, do_ref, dot_ref, ot_ref, lse_ref = refs[:9]
        ins, (dq_ref, dk_ref, dv_ref), outs = refs[9:9 + m], refs[9 + m:12 + m], refs[12 + m:12 + 2 * m]
        sems = refs[12 + 2 * m:]
        qb, kb = qt_ref[pl.program_id(1)], kt_ref[pl.program_id(1)]
        if m:
            pl.when((pl.program_id(0) == 0) & (pl.program_id(1) == 0))(lambda: plan.start(ins, outs, sems))

        @pl.when(pl.program_id(1) == 0)
        def _():
            dq_ref[...] = jnp.zeros_like(dq_ref)

        @pl.when(qb == kb)
        def _():
            dk_ref[...] = jnp.zeros_like(dk_ref)
            dv_ref[...] = jnp.zeros_like(dv_ref)

        def block(kr, qr, masked):
            nq = qr.stop - qr.start
            cols = pl.ds(pl.multiple_of(qb * B_T + qr.start, LANES), nq)
            v = v_ref[kr, :]
            dov = do_ref[qr, :]
            prod = dot_ref[:, qr].astype(F32) * ot_ref[:, qr].astype(F32)
            lo = _lane((nq, LANES)) < HD
            for h2 in range(2):
                sl = slice(LANES * h2, LANES * (h2 + 1))
                hm = lo if h2 == 0 else ~lo
                q = q_ref[qr, sl]
                k = k_ref[kr, sl]
                dom = jnp.where(hm, dov, jnp.zeros((), BF16))
                delta = jnp.sum(prod[HD * h2:HD * (h2 + 1), :], axis=0, keepdims=True)
                s = _dot_nt(k, q)
                if masked:
                    s = _b_mask_t(s)
                pr = jnp.exp2(s - lse_ref[0, h2:h2 + 1, qr])
                dp = _dot_nt(v, dom)
                ds = (pr * (dp - delta)).astype(BF16)
                dk_ref[kr, sl] += _dot(ds, q) * (B_SCALE / B_SCALE2)
                dv_ref[kr, :] += _dot(pr.astype(BF16), dom)
                dq_ref[sl, cols] += _dot_tn(k, ds) * B_SCALE

        @pl.when(qb > kb)
        def _():
            block(_B_ALL, _B_ALL, False)

        @pl.when(qb == kb)
        def _():
            for kr, qr in _B_DIAG:
                block(kr, qr, kr == qr)

        if m:
            pl.when((pl.program_id(0) == 3) & (pl.program_id(1) == last))(lambda: plan.finish(ins, outs, sems))

    qrow = lambda w: pl.BlockSpec((B_T, w), lambda p, s, qt, kt: (qt[s], p))
    qcol = pl.BlockSpec((LANES, B_T), lambda p, s, qt, kt: (p, qt[s]))
    krow = lambda w: pl.BlockSpec((B_T, w), lambda p, s, qt, kt: (kt[s], p))
    grid_spec = pltpu.PrefetchScalarGridSpec(
        num_scalar_prefetch=2, grid=(4, len(qtab)),
        in_specs=[qrow(256), krow(256), krow(LANES), qrow(LANES), qcol, qcol,
                  pl.BlockSpec((1, 2, B_T), lambda p, s, qt, kt: (p, 0, qt[s]))] + [ANY] * m,
        out_specs=[pl.BlockSpec((256, t), lambda p, s, qt, kt: (p, 0)), krow(256), krow(LANES)] + [ANY] * m,
        scratch_shapes=plan.scratch if m else [])
    return pl.pallas_call(
        body, name="attn_b_bwd", grid_spec=grid_spec,
        out_shape=[jax.ShapeDtypeStruct((1024, t), F32), jax.ShapeDtypeStruct((t, 1024), F32),
                   jax.ShapeDtypeStruct((t, 512), F32)] + plan.out_shape,
        compiler_params=_params(("arbitrary", "arbitrary")),
    )(jnp.asarray(qtab), jnp.asarray(ktab), qf, kf, vp, do, do_t, o_t, lse, *scatter)


_U_LEN = A_TQ + A_WIN - 1


def _band_mask():
    a = np.arange(A_TQ)[:, None] // CHUNK
    b = np.arange(A_WIN)[None, :] // CHUNK
    return (b >= a) & (b <= a + A_LEFT)


def bias_block(table):
    h = table.shape[0]
    n_lo = A_WIN - 1 - 2 * A_TQ - A_MAX_REL
    ext = jnp.concatenate([jnp.repeat(table[:, :1], n_lo, axis=1), table,
                           jnp.repeat(table[:, -1:], _U_LEN - n_lo - table.shape[1], axis=1)], axis=1)
    row = jnp.pad(ext[:, ::-1], ((0, 0), (0, 1)))[:, None, :]
    band = _band_mask()
    first = [band & (np.arange(A_WIN)[None, :] >= 2 * A_TQ - A_TQ * v) for v in range(3)]
    keep = jnp.asarray(np.stack(first), jnp.int32)

    def body(r_ref, k_ref, o_ref):
        rows = jnp.broadcast_to(r_ref[0], (A_TQ, _U_LEN + 1))
        skew = pltpu.roll(rows, _U_LEN + 1 - (A_TQ - 1), 1, stride=1, stride_axis=0)
        toep = skew[:, :A_WIN]
        for v in range(3):
            o_ref[v, 0] = jnp.where(k_ref[v] != 0, toep, NEG)

    return pl.pallas_call(
        body, name="bias_block", grid=(h,),
        in_specs=[pl.BlockSpec((1, 1, _U_LEN + 1), lambda i: (i, 0, 0)),
                  pl.BlockSpec((3, A_TQ, A_WIN), lambda i: (0, 0, 0))],
        out_specs=pl.BlockSpec((3, 1, A_TQ, A_WIN), lambda i: (0, i, 0, 0)),
        out_shape=jax.ShapeDtypeStruct((3, h, A_TQ, A_WIN), F32),
        compiler_params=_params(("parallel",)),
    )(row, keep)


def bias_block_grad(db):
    h = db.shape[0]
    n_lo = A_WIN - 1 - 2 * A_TQ - A_MAX_REL
    skew = jnp.pad(db, ((0, 0), (0, 0), (A_TQ - 1, 0)))
    flat = jnp.pad(skew.reshape(h, A_TQ * _U_LEN), ((0, 0), (0, A_TQ)))
    ext = jnp.sum(flat.reshape(h, A_TQ, _U_LEN + 1), axis=1)[:, :_U_LEN][:, ::-1]
    n_tab = 2 * A_MAX_REL + 1
    first = jnp.sum(ext[:, :n_lo + 1], axis=1, keepdims=True)
    last = jnp.sum(ext[:, n_lo + n_tab - 1:], axis=1, keepdims=True)
    return jnp.concatenate([first, ext[:, n_lo + 1:n_lo + n_tab - 1], last], axis=1)


def rope_tabs(t):
    inv = 1.0 / (10000.0 ** (jnp.arange(0, ROPE, 2, dtype=F32) / ROPE))
    ang = jnp.arange(t, dtype=F32)[:, None] * inv[None, :]
    cos, sin = jnp.cos(ang), jnp.sin(ang)
    z = lambda w: jnp.zeros((t, w), F32)
    ck = jnp.concatenate([cos, cos, z(96)], axis=1)
    s1k = jnp.concatenate([-sin, z(112)], axis=1)
    s2k = jnp.concatenate([z(16), sin, z(96)], axis=1)
    cq = jnp.concatenate([jnp.ones((t, HD), F32), cos, cos, z(32)], axis=1)
    s1q = jnp.concatenate([z(HD), -sin, z(48)], axis=1)
    s2q = jnp.concatenate([z(HD + 16), sin, z(32)], axis=1)
    return jnp.stack([cq, s1q, s2q, ck, s1k, s2k])


def _pad_lanes(v, width):
    return jnp.pad(v, ((0, 0), (0, width - v.shape[1])))


LATE = ("w_in", "b_w_uq", "b_w_ukv", "w_out", "ffn2_w_gate", "ffn2_w_up", "ffn2_w_down")
FFN2 = ("ffn2_w_gate", "ffn2_w_up", "ffn2_w_down")


def kernel_layout(gathered):
    w = {n: v for n, v in gathered.items() if n.startswith("ffn")}
    if "w_in" in gathered:
        w["w_in"] = jnp.pad(gathered["w_in"].reshape(IN_COLS, D_MODEL), ((0, PROJ_W - IN_COLS), (0, 0)))
        uq = gathered["b_w_uq"].reshape(HEADS, HD + ROPE, 256)
        w["b_w_uq"] = jnp.pad(uq, ((0, 0), (0, LANES - HD - ROPE), (0, 0))).reshape(HEADS * LANES, 256)
        w["b_w_ukv"] = _shards_to_cols(gathered["b_w_ukv"])
        w["w_out"] = gathered["w_out"].reshape(N_SHARD * gathered["w_out"].shape[1], D_MODEL)
    return w


def local_step(x, target, w, late=None):
    t = x.shape[0]
    gq = jnp.tile(w["a_q_norm"], (1, HEADS))
    gk = jnp.tile(w["a_k_norm"], (1, HEADS))
    gq128 = _pad_lanes(jnp.concatenate([w["b_q_nope_norm"], w["b_q_rope_norm"]], axis=1), LANES)
    gk128 = _pad_lanes(w["b_k_nope_norm"], LANES)
    gkr128 = _pad_lanes(w["b_k_rope_norm"], LANES)
    tabs = rope_tabs(t)
    bias = bias_block(w["a_rel_bias"])

    if late is None:
        x1, gate1, up1 = ffn_fwd(x, w["ffn1_norm"], w["ffn1_w_gate"], w["ffn1_w_up"], w["ffn1_w_down"], "ffn_fwd")
    else:
        own, shard = late
        x1, gate1, up1, *got = ffn_fwd(x, w["ffn1_norm"], w["ffn1_w_gate"], w["ffn1_w_up"], w["ffn1_w_down"],
                                       "ffn_fwd_gather", gather=own)
        w = dict(w, **kernel_layout({n: lax.dynamic_update_index_in_dim(g_, o_, shard, 0)
                                     for n, g_, o_ in zip(LATE, got, own)}))
    h, proj, qa, ka, va, cqn, ckvn = mix_fwd(x1, w["mix_norm"], w["w_in"], gq, gk,
                                             w["b_q_lat_norm"], w["b_kv_lat_norm"])
    qf, kf, vp = prep2_fwd(cqn, ckvn, proj, w["b_w_uq"], w["b_w_ukv"], gq128, gk128, gkr128, tabs)
    oa = attn_a_fwd(qa, ka, va, bias)
    ob_t, lse = attn_b_fwd(qf, kf, vp)
    g = {}
    dx3, gate2, up2, g["final_norm"], loss, x2 = ffn_fwd(
        x1, w["ffn2_norm"], w["ffn2_w_gate"], w["ffn2_w_up"], w["ffn2_w_down"], "ffn_fwd_loss",
        loss_head=(w["final_norm"], target), pre_proj=(oa, ob_t, w["w_out"]))
    g["ffn2_w_gate"], g["ffn2_w_up"], g["ffn2_w_down"], dhp = ffn_bwd(
        x2, dx3, w["ffn2_norm"], w["ffn2_w_gate"], w["ffn2_w_up"], w["ffn2_w_down"], gate2, up2, "ffn_bwd")
    dx2, g["ffn2_norm"], d_oa, d_ob, d_ob_t, g["w_out"] = out_proj_bwd(
        x2, w["ffn2_norm"], dhp, dx3, w["w_out"], oa, ob_t)
    early = [g[n] for n in FFN2] + [g["w_out"].reshape(N_SHARD, -1, D_MODEL)]
    dqf, dkf, dvp, *landed_early = attn_b_bwd(qf, kf, vp, d_ob, d_ob_t, ob_t, lse,
                                              scatter=() if late is None else early)
    dqa, dkp, dvpa, dbias = attn_a_bwd(qa, ka, va, bias, d_oa)
    dcq, dckv, dkr, dwuq, dwukv, dgq128, dgk128, dgkr128 = prep2_bwd(
        cqn, ckvn, proj, w["b_w_uq"], w["b_w_ukv"], dqf, dkf, dvp, gq128, gk128, gkr128, tabs)
    g["b_w_uq"], g["b_w_ukv"] = dwuq.astype(BF16), dwukv.astype(BF16)
    dx1, g["w_in"], g["mix_norm"], dgq, dgk, g["b_q_lat_norm"], g["b_kv_lat_norm"] = mix_bwd(
        proj, x1, h, dx2, w["w_in"], w["mix_norm"], dqa, dkp, dvpa, dcq, dckv, dkr, gq, gk,
        w["b_q_lat_norm"], w["b_kv_lat_norm"])
    mid = [g["w_in"][:IN_COLS].reshape(N_SHARD, IN_COLS // N_SHARD, D_MODEL),
           g["b_w_uq"].reshape(HEADS, LANES, 256)[:, :HD + ROPE].reshape(N_SHARD, -1, 256),
           _cols_to_shards(g["b_w_ukv"])]
    g["ffn1_w_gate"], g["ffn1_w_up"], g["ffn1_w_down"], dhp, *landed_late = ffn_bwd(
        x, dx1, w["ffn1_norm"], w["ffn1_w_gate"], w["ffn1_w_up"], w["ffn1_w_down"], gate1, up1,
        "ffn_bwd" if late is None else "ffn_bwd_scatter", scatter=() if late is None else mid,
        spread=None if late is None else late[1])
    grad_x, g["ffn1_norm"] = norm_bwd(x, w["ffn1_norm"], dhp, dx1, "ffn_norm_bwd")
    landed = dict(zip(FFN2 + ("w_out", "w_in", "b_w_uq", "b_w_ukv", "ffn1_w_gate", "ffn1_w_up", "ffn1_w_down"),
                      landed_early + landed_late))

    g["a_q_norm"] = jnp.sum(dgq.reshape(HEADS, HD), axis=0, keepdims=True)
    g["a_k_norm"] = jnp.sum(dgk.reshape(HEADS, HD), axis=0, keepdims=True)
    g["a_rel_bias"] = bias_block_grad(dbias)
    g["b_q_nope_norm"] = dgq128[:, :HD]
    g["b_q_rope_norm"] = dgq128[:, HD:HD + ROPE]
    g["b_k_nope_norm"] = dgk128[:, :HD]
    g["b_k_rope_norm"] = dgkr128[:, :ROPE]
    return loss, grad_x, g, landed


ANY = pl.BlockSpec(memory_space=pl.ANY)
N_DEV = 8


def _place():
    return lax.axis_index("x"), lax.axis_index("y"), lax.axis_index("c")


def _flip(v, bit):
    return 1 - v if bit else v


BF16_ROWS = 16


def _split_axis(shape):
    return 0 if (shape[0] // 2) % BF16_ROWS == 0 else 1


def _half_shape(shape):
    axis = _split_axis(shape)
    return tuple(s // 2 if a == axis else s for a, s in enumerate(shape))


def _half(shape, core):
    axis = _split_axis(shape)
    size = shape[axis] // 2
    return tuple(pl.ds(core * size, size) if a == axis else slice(None) for a in range(2))


class GatherPlan:
    def __init__(self, ws):
        self.shapes = [w.shape for w in ws]
        self.n = len(ws)
        self.out_shape = [jax.ShapeDtypeStruct((N_SHARD,) + w.shape, w.dtype) for w in ws]
        self.scratch = [pltpu.SemaphoreType.DMA((6 * self.n,)), pltpu.SemaphoreType.DMA((6 * self.n,))]

    def _copies(self, ins, outs, sems):
        x, y, c = _place()
        s_me = 2 * x + y
        sibling = (x, y, 1 - c)
        send_sems, recv_sems = sems

        def remote(k, src, dst, to):
            return pltpu.make_async_remote_copy(src_ref=src, dst_ref=dst, send_sem=send_sems.at[k],
                                                recv_sem=recv_sems.at[k], device_id=to, device_id_type=MESH)

        ici, fwd = [], []
        for a in range(self.n):
            mine, theirs = _half(self.shapes[a], c), _half(self.shapes[a], 1 - c)
            for j, (cx, cy) in enumerate([(1 - x, y), (x, 1 - y), (1 - x, 1 - y)]):
                got = outs[a].at[(2 * cx + cy,) + mine]
                ici.append((remote(6 * a + j, ins[a].at[mine], outs[a].at[(s_me,) + mine], (cx, cy, c)),
                            remote(6 * a + j, got, got, (cx, cy, c))))
                passed = outs[a].at[(2 * cx + cy,) + theirs]
                fwd.append((remote(6 * a + 3 + j, got, got, sibling), remote(6 * a + 3 + j, passed, passed, sibling)))
        return ici, fwd

    def start(self, ins, outs, sems):
        for send, _ in self._copies(ins, outs, sems)[0]:
            send.start()

    def forward(self, ins, outs, sems):
        ici, fwd = self._copies(ins, outs, sems)
        for (_, arrival), (send, _) in zip(ici, fwd):
            arrival.wait_recv()
            send.start()

    def finish(self, ins, outs, sems):
        ici, fwd = self._copies(ins, outs, sems)
        for _, arrival in fwd:
            arrival.wait_recv()
        for send, _ in ici + fwd:
            send.wait_send()


def allgather_shards(ws):
    plan = GatherPlan(ws)
    n = plan.n

    def body(*refs):
        ins, outs, sems = refs[:n], refs[n:2 * n], refs[2 * n:]
        plan.start(ins, outs, sems)
        plan.forward(ins, outs, sems)
        plan.finish(ins, outs, sems)

    return pl.pallas_call(
        body, name="allgather_shards", in_specs=[ANY] * n, out_specs=[ANY] * n,
        out_shape=plan.out_shape, scratch_shapes=plan.scratch,
    )(*ws)


class ScatterPlan:
    def __init__(self, gs):
        self.shapes = [g.shape[1:] for g in gs]
        self.n = len(gs)
        self.out_shape = [jax.ShapeDtypeStruct((N_DEV,) + _half_shape(g.shape[1:]), g.dtype) for g in gs]
        self.scratch = [pltpu.SemaphoreType.DMA((7 * self.n,)), pltpu.SemaphoreType.DMA((7 * self.n,)),
                        pltpu.SemaphoreType.DMA((self.n,))]

    def _copies(self, ins, outs, sems):
        x, y, c = _place()
        me = 4 * x + 2 * y + c
        send_sems, recv_sems, local_sems = sems
        local, sends, arrivals = [], [], []
        for a in range(self.n):
            piece = lambda px, py, pc, a=a: ins[a].at[(2 * px + py,) + _half(self.shapes[a], pc)]
            local.append(pltpu.make_async_copy(piece(x, y, c), outs[a].at[me], local_sems.at[a]))
            for k in range(1, N_DEV):
                px, py, pc = _flip(x, k & 4), _flip(y, k & 2), _flip(c, k & 1)
                sem = dict(send_sem=send_sems.at[7 * a + k - 1], recv_sem=recv_sems.at[7 * a + k - 1],
                           device_id=(px, py, pc), device_id_type=MESH)
                sends.append(pltpu.make_async_remote_copy(
                    src_ref=piece(px, py, pc), dst_ref=outs[a].at[me], **sem))
                slot = outs[a].at[4 * px + 2 * py + pc]
                arrivals.append(pltpu.make_async_remote_copy(src_ref=slot, dst_ref=slot, **sem))
        return local, sends, arrivals

    def start(self, ins, outs, sems):
        local, sends, _ = self._copies(ins, outs, sems)
        for cp in local + sends:
            cp.start()

    def finish(self, ins, outs, sems):
        local, sends, arrivals = self._copies(ins, outs, sems)
        for cp in arrivals:
            cp.wait_recv()
        for cp in sends:
            cp.wait_send()
        for cp in local:
            cp.wait()


def _row_tile(rows, row_bytes, budget, multiple):
    fits = [r for r in range(multiple, rows + 1, multiple) if rows % r == 0 and r * row_bytes <= budget]
    return max(fits) if fits else rows


SMALL_KERNEL_VMEM = 32 * 1024 * 1024


def sum_slots(lands, name):
    k = len(lands)
    _, rows, cols = lands[0].shape
    tr = _row_tile(rows, k * 2 * (N_DEV * cols * 2 + cols * 4), SMALL_KERNEL_VMEM, BF16_ROWS)

    def body(*refs):
        for l_ref, o_ref in zip(refs[:k], refs[k:]):
            acc = l_ref[0].astype(F32)
            for s in range(1, N_DEV):
                acc = acc + l_ref[s].astype(F32)
            o_ref[...] = acc

    return pl.pallas_call(
        body, name=name, grid=(rows // tr,),
        in_specs=[pl.BlockSpec((N_DEV, tr, cols), lambda i: (0, i, 0))] * k,
        out_specs=[pl.BlockSpec((tr, cols), lambda i: (i, 0))] * k,
        out_shape=[jax.ShapeDtypeStruct((rows, cols), F32)] * k,
        compiler_params=_params(("parallel",)),
    )(*lands)


def join_halves(hs, shapes):
    n = len(hs)

    def body(*refs):
        ins, outs = refs[:n], refs[n:2 * n]
        send_sems, recv_sems = refs[2 * n:]
        x, y, c = _place()
        sends = []
        for a in range(n):
            mine = outs[a].at[_half(shapes[a], c)]
            sends.append(pltpu.make_async_remote_copy(
                src_ref=ins[a], dst_ref=mine, send_sem=send_sems.at[a], recv_sem=recv_sems.at[a],
                device_id=(x, y, 1 - c), device_id_type=MESH))
            sends[-1].start()
        for a in range(n):
            theirs = outs[a].at[_half(shapes[a], 1 - c)]
            pltpu.make_async_remote_copy(
                src_ref=theirs, dst_ref=theirs, send_sem=send_sems.at[a], recv_sem=recv_sems.at[a],
                device_id=(x, y, 1 - c), device_id_type=MESH).wait_recv()
        for cp in sends:
            cp.wait_send()

    return pl.pallas_call(
        body, name="join_halves",
        in_specs=[ANY] * n, out_specs=[ANY] * n,
        out_shape=[jax.ShapeDtypeStruct(tuple(s), h.dtype) for s, h in zip(shapes, hs)],
        scratch_shapes=[pltpu.SemaphoreType.DMA((n,)), pltpu.SemaphoreType.DMA((n,))],
    )(*hs)


def allreduce_small(vec):
    def body(v_ref, o_ref, land_ref, send_sems, recv_sems):
        x, y, c = _place()
        me = 4 * x + 2 * y + c
        land_ref[me] = v_ref[...]
        sends = []
        for k in range(1, N_DEV):
            px, py, pc = _flip(x, k & 4), _flip(y, k & 2), _flip(c, k & 1)
            sends.append(pltpu.make_async_remote_copy(
                src_ref=v_ref, dst_ref=land_ref.at[me], send_sem=send_sems.at[k - 1], recv_sem=recv_sems.at[k - 1],
                device_id=(px, py, pc), device_id_type=MESH))
            sends[-1].start()
        for k in range(1, N_DEV):
            px, py, pc = _flip(x, k & 4), _flip(y, k & 2), _flip(c, k & 1)
            slot = land_ref.at[4 * px + 2 * py + pc]
            pltpu.make_async_remote_copy(
                src_ref=slot, dst_ref=slot, send_sem=send_sems.at[k - 1], recv_sem=recv_sems.at[k - 1],
                device_id=(px, py, pc), device_id_type=MESH).wait_recv()
        for cp in sends:
            cp.wait_send()
        acc = land_ref[0]
        for s in range(1, N_DEV):
            acc = acc + land_ref[s]
        o_ref[...] = acc

    vm = pl.BlockSpec(memory_space=pltpu.VMEM)
    return pl.pallas_call(
        body, name="allreduce_small",
        in_specs=[vm], out_specs=vm,
        out_shape=jax.ShapeDtypeStruct(vec.shape, F32),
        scratch_shapes=[pltpu.VMEM((N_DEV,) + vec.shape, F32), pltpu.SemaphoreType.DMA((N_DEV - 1,)),
                        pltpu.SemaphoreType.DMA((N_DEV - 1,))],
    )(vec)


def adamw(ws, gs, ms, vs, name):
    k = len(ws)
    rows, cols = ws[0].shape
    tr = _row_tile(rows, k * 2 * 7 * cols * 4, SMALL_KERNEL_VMEM, 8)
    c1 = 1.0 - ADAM_B1 ** ADAM_STEP
    c2 = 1.0 - ADAM_B2 ** ADAM_STEP

    def body(*refs):
        for a in range(k):
            w_ref, g_ref, m_ref, v_ref = (refs[s * k + a] for s in range(4))
            d_ref, nm_ref, nv_ref = (refs[(4 + s) * k + a] for s in range(3))
            gv = g_ref[...]
            nm = ADAM_B1 * m_ref[...] + (1.0 - ADAM_B1) * gv
            nv = ADAM_B2 * v_ref[...] + (1.0 - ADAM_B2) * (gv * gv)
            nm_ref[...] = nm
            nv_ref[...] = nv
            d_ref[...] = -ADAM_LR * ((nm / c1) / (jnp.sqrt(nv / c2) + ADAM_EPS) + ADAM_WD * w_ref[...])

    blk = pl.BlockSpec((tr, cols), lambda i: (i, 0))
    out = pl.pallas_call(
        body, name=name, grid=(rows // tr,),
        in_specs=[blk] * (4 * k), out_specs=[blk] * (3 * k),
        out_shape=[jax.ShapeDtypeStruct((rows, cols), F32)] * (3 * k),
        compiler_params=_params(("parallel",)),
    )(*ws, *gs, *ms, *vs)
    return [(out[a], out[k + a], out[2 * k + a]) for a in range(k)]


BIG = ("ffn1_w_gate", "ffn1_w_up", "ffn1_w_down", "w_in", "b_w_uq", "b_w_ukv", "w_out",
       "ffn2_w_gate", "ffn2_w_up", "ffn2_w_down")
SMALL = ("ffn1_norm", "mix_norm", "a_q_norm", "a_k_norm", "a_rel_bias", "b_q_lat_norm", "b_kv_lat_norm",
         "b_q_nope_norm", "b_q_rope_norm", "b_k_nope_norm", "b_k_rope_norm", "ffn2_norm", "final_norm")
WEIGHTS = ("ffn1_norm", "ffn1_w_gate", "ffn1_w_up", "ffn1_w_down", "mix_norm", "w_in", "a_q_norm", "a_k_norm",
           "a_rel_bias", "b_q_lat_norm", "b_w_uq", "b_kv_lat_norm", "b_w_ukv", "b_q_nope_norm", "b_q_rope_norm",
           "b_k_nope_norm", "b_k_rope_norm", "w_out", "ffn2_norm", "ffn2_w_gate", "ffn2_w_up", "ffn2_w_down",
           "final_norm")
TRANSPOSED = ("ffn1_w_gate", "ffn1_w_up", "ffn2_w_gate", "ffn2_w_up", "w_in", "b_w_uq")
PACK_SHAPE = (8, 1024)


def _pack_small(d, last=None):
    flat = [d[n].reshape(-1) for n in SMALL]
    used = sum(f.shape[0] for f in flat)
    total = PACK_SHAPE[0] * PACK_SHAPE[1]
    tail = jnp.zeros((total - used - 1,), F32)
    end = jnp.zeros((1,), F32) if last is None else last.reshape(1)
    return jnp.concatenate(flat + [tail, end]).reshape(PACK_SHAPE)


def _unpack_small(p, like):
    flat = p.reshape(-1)
    out, off = {}, 0
    for n in SMALL:
        size = like[n].size
        out[n] = flat[off:off + size].reshape(like[n].shape)
        off += size
    return out, flat[-1]


def _cols_to_shards(g):
    rows, cols = g.shape
    return g.reshape(rows, N_SHARD, cols // N_SHARD).transpose(1, 0, 2)


def _shards_to_cols(g):
    return g.transpose(1, 0, 2).reshape(g.shape[1], -1)


def kernel(x, ffn1_norm, ffn1_w_gate, ffn1_w_up, ffn1_w_down, mix_norm, w_in, a_q_norm, a_k_norm, a_rel_bias, b_q_lat_norm, b_w_uq, b_kv_lat_norm, b_w_ukv, b_q_nope_norm, b_q_rope_norm, b_k_nope_norm, b_k_rope_norm, w_out, ffn2_norm, ffn2_w_gate, ffn2_w_up, ffn2_w_down, final_norm, loss_target, m_ffn1_norm, m_ffn1_w_gate, m_ffn1_w_up, m_ffn1_w_down, m_mix_norm, m_w_in, m_a_q_norm, m_a_k_norm, m_a_rel_bias, m_b_q_lat_norm, m_b_w_uq, m_b_kv_lat_norm, m_b_w_ukv, m_b_q_nope_norm, m_b_q_rope_norm, m_b_k_nope_norm, m_b_k_rope_norm, m_w_out, m_ffn2_norm, m_ffn2_w_gate, m_ffn2_w_up, m_ffn2_w_down, m_final_norm, v_ffn1_norm, v_ffn1_w_gate, v_ffn1_w_up, v_ffn1_w_down, v_mix_norm, v_w_in, v_a_q_norm, v_a_k_norm, v_a_rel_bias, v_b_q_lat_norm, v_b_w_uq, v_b_kv_lat_norm, v_b_w_ukv, v_b_q_nope_norm, v_b_q_rope_norm, v_b_k_nope_norm, v_b_k_rope_norm, v_w_out, v_ffn2_norm, v_ffn2_w_gate, v_ffn2_w_up, v_ffn2_w_down, v_final_norm):
    args = locals()
    view = lambda a, n: a[0].T if n in TRANSPOSED else a[0]
    wts = {n: view(args[n], n) for n in WEIGHTS}
    mom = {n: view(args["m_" + n], n) for n in WEIGHTS}
    var = {n: view(args["v_" + n], n) for n in WEIGHTS}

    shard = 2 * lax.axis_index("x") + lax.axis_index("y")
    core = lax.axis_index("c")
    first = [n for n in BIG if n not in LATE]
    own = [wts[n].astype(BF16) for n in first]
    w = {n: wts[n] if n == "a_rel_bias" else wts[n][None] for n in SMALL}
    w.update(kernel_layout({n: lax.dynamic_update_index_in_dim(got, mine, shard, 0)
                            for n, got, mine in zip(first, allgather_shards(own), own)}))

    loss, grad_x, g, landed = local_step(x[0], loss_target[0], w,
                                         late=([wts[n].astype(BF16) for n in LATE], shard))

    me = 2 * shard + core
    for n in first:
        piece = lax.dynamic_slice(g[n], (shard, core * (FS // 2), 0), (1, FS // 2, D_MODEL))
        landed[n] = lax.dynamic_update_slice(landed[n], piece, (me, 0, 0))
    groups = {}
    for n in BIG:
        groups.setdefault(wts[n].shape, []).append(n)
    half = {}
    for names in groups.values():
        half.update(zip(names, sum_slots([landed[n] for n in names], "sum_slots")))
    halves = [half[n] for n in BIG]
    shapes = [wts[n].shape for n in BIG]
    axes = [_split_axis(s) for s in shapes]
    grads = dict(zip(BIG, (lax.dynamic_update_slice_in_dim(got, mine, core * mine.shape[ax], ax)
                           for got, mine, ax in zip(join_halves(halves, shapes), halves, axes))))

    small_sum, loss_sum = _unpack_small(allreduce_small(_pack_small(g, loss[0, 0])), wts)
    grads.update(small_sum)

    delta, new_m, new_v = {}, {}, {}
    for names in groups.values():
        stepped = adamw(*([d[n] for n in names] for d in (wts, grads, mom, var)), "adamw")
        for n, (d_new, m_new, v_new) in zip(names, stepped):
            delta[n], new_m[n], new_v[n] = d_new, m_new, v_new
    (packed,) = adamw([_pack_small(wts)], [_pack_small(grads)], [_pack_small(mom)], [_pack_small(var)], "adamw_small")
    for dst, p in zip((delta, new_m, new_v), packed):
        dst.update(_unpack_small(p, wts)[0])

    lead = lambda d: [(d[n].T if n in TRANSPOSED else d[n])[None] for n in WEIGHTS]
    return (loss_sum, grad_x[None], *lead(grads), *lead(delta), *lead(new_m), *lead(new_v))
```

```python
import numpy as np
import jax
import jax.numpy as jnp
from jax import lax
from jax.experimental import pallas as pl
from jax.experimental.pallas import tpu as pltpu

F32 = jnp.float32
BF16 = jnp.bfloat16
EPS = 1e-6
NEG = -1e30

D_MODEL = 1024
D_FF = 2816
N_SHARD = 4
FS = D_FF // N_SHARD
CHUNK = 64
A_LEFT = 8
A_MAX_REL = 128
HEADS = 8
HD = 64
ROPE = 32
PROJ_W = 2048
IN_COLS = 1952
B_SCALE = 96 ** -0.5
LANES = 128

ADAM_LR = 0.001
ADAM_B1 = 0.9
ADAM_B2 = 0.999
ADAM_EPS = 1e-08
ADAM_WD = 0.01
ADAM_STEP = 10

VMEM_LIMIT = 56 * 1024 * 1024
TOKEN_TILE = 512

MESH = pl.DeviceIdType.MESH


def _dot(a, b):
    return lax.dot_general(a, b, (((1,), (0,)), ((), ())), preferred_element_type=F32)


def _dot_nt(a, b):
    return lax.dot_general(a, b, (((1,), (1,)), ((), ())), preferred_element_type=F32)


def _dot_tn(a, b):
    return lax.dot_general(a, b, (((0,), (0,)), ((), ())), preferred_element_type=F32)


def _params(sem):
    return pltpu.CompilerParams(dimension_semantics=sem, vmem_limit_bytes=VMEM_LIMIT)


def _rms(xv):
    r = lax.rsqrt(jnp.mean(xv * xv, axis=-1, keepdims=True) + EPS)
    return r, xv * r


def ffn_fwd(x, g, wg, wu, wd, name, gather=(), loss_head=None, pre_proj=None):
    t, d = x.shape
    tm = TOKEN_TILE
    ni = t // tm
    plan = GatherPlan(gather)
    n = plan.n
    head = () if loss_head is None else tuple(loss_head)
    q = len(head)
    proj = () if pre_proj is None else tuple(pre_proj)
    r = len(proj)
    half = proj[0].shape[1] if r else 0

    def body(*refs):
        it = iter(refs)
        take = lambda count: [next(it) for _ in range(count)]
        x_ref, g_ref, wg_ref, wu_ref, wd_ref = take(5)
        head_in, proj_in, ins = take(q), take(r), take(n)
        o_ref, gp_ref, up_ref = take(3)
        head_out, x_out, outs = take(q), take(1 if r else 0), take(n)
        h_ref, acc_ref = take(2)
        sems = list(it)
        i, j = pl.program_id(0), pl.program_id(1)
        if n:
            pl.when((i == 0) & (j == 0))(lambda: plan.start(ins, outs, sems))
            pl.when((i == (3 * ni) // 4) & (j == 0))(lambda: plan.forward(ins, outs, sems))

        @pl.when(j == 0)
        def _():
            xin = x_ref[...]
            if r:
                oa_ref, obt_ref, wo_ref = proj_in
                xin = xin + _dot(oa_ref[...], wo_ref[0:half, :]) + _dot_tn(obt_ref[...], wo_ref[half:2 * half, :])
                x_out[0][...] = xin
            _, xn = _rms(xin)
            h_ref[...] = (xn * g_ref[...]).astype(BF16)
            acc_ref[...] = jnp.zeros_like(acc_ref)

        h = h_ref[...]
        gp = _dot_nt(h, wg_ref[0])
        up = _dot_nt(h, wu_ref[0])
        gp_ref[0] = gp.astype(BF16)
        up_ref[0] = up.astype(BF16)
        a = (gp * jax.nn.sigmoid(gp) * up).astype(BF16)
        acc_ref[...] += _dot(a, wd_ref[0])

        @pl.when(j == N_SHARD - 1)
        def _():
            y = (x_out[0][...] if r else x_ref[...]) + 0.5 * acc_ref[...]
            if not q:
                o_ref[...] = y
                return
            (gf_ref, t_ref), (dgf_ref, loss_ref) = head_in, head_out
            rinv, yn = _rms(y)
            gf = gf_ref[...]
            e = yn * gf - t_ref[...]
            dout = e * (1.0 / d)
            dng = dout * gf
            o_ref[...] = rinv * (dng - yn * jnp.mean(dng * yn, axis=-1, keepdims=True))

            @pl.when(i == 0)
            def _():
                dgf_ref[...] = jnp.zeros_like(dgf_ref)
                loss_ref[...] = jnp.zeros_like(loss_ref)

            dgf_ref[...] += jnp.sum(dout * yn, axis=0, keepdims=True)
            part = jnp.sum(jnp.sum(e * e, axis=-1, keepdims=True), axis=0, keepdims=True) * (0.5 / d)
            loss_ref[...] += jnp.broadcast_to(part, loss_ref.shape)

        if n:
            pl.when((i == ni - 1) & (j == N_SHARD - 1))(lambda: plan.finish(ins, outs, sems))

    tok = pl.BlockSpec((tm, d), lambda i, j: (i, 0))
    vec = pl.BlockSpec((1, d), lambda i, j: (0, 0))
    chunk = pl.BlockSpec((1, FS, d), lambda i, j: (j, 0, 0))
    pre = pl.BlockSpec((1, tm, FS), lambda i, j: (j, i, 0))
    return pl.pallas_call(
        body, name=name, grid=(ni, N_SHARD),
        in_specs=[tok, vec, chunk, chunk, chunk] + [vec, tok][:q]
        + [pl.BlockSpec((tm, half), lambda i, j: (i, 0)), pl.BlockSpec((half, tm), lambda i, j: (0, i)),
           pl.BlockSpec((2 * half, d), lambda i, j: (0, 0))][:r] + [ANY] * n,
        out_specs=[tok, pre, pre] + [vec, pl.BlockSpec((1, LANES), lambda i, j: (0, 0))][:q] + [tok][:r] + [ANY] * n,
        out_shape=[jax.ShapeDtypeStruct((t, d), F32), jax.ShapeDtypeStruct((N_SHARD, t, FS), BF16),
                   jax.ShapeDtypeStruct((N_SHARD, t, FS), BF16)]
        + [jax.ShapeDtypeStruct((1, d), F32), jax.ShapeDtypeStruct((1, LANES), F32)][:q]
        + [jax.ShapeDtypeStruct((t, d), F32)][:r] + plan.out_shape,
        scratch_shapes=[pltpu.VMEM((tm, d), BF16), pltpu.VMEM((tm, d), F32)] + (plan.scratch if n else []),
        compiler_params=_params(("arbitrary", "arbitrary")),
    )(x, g, wg, wu, wd, *head, *proj, *gather)


def ffn_bwd(x, dout, g, wg, wu, wd, gate, up_pre, name, scatter=(), spread=None):
    t, d = x.shape
    tm = TOKEN_TILE
    ni = t // tm
    hf = FS // 2
    plan = ScatterPlan(scatter)
    m = plan.n
    k = 0 if spread is None else 3
    steps = jnp.arange(N_SHARD, dtype=jnp.int32)
    order = steps if spread is None else (spread + 1 + steps) % N_SHARD

    def body(*refs):
        ord_ref, x_ref, do_ref, g_ref, wg_ref, wu_ref, wd_ref, gp_ref, up_ref = refs[:9]
        ins, (dwg_out, dwu_out, dwd_out, dhp_ref) = refs[9:9 + m], refs[9 + m:13 + m]
        outs, lands = refs[13 + m:13 + 2 * m], refs[13 + 2 * m:13 + 2 * m + k]
        dwg_ref, dwu_ref, dwd_ref = refs[13 + 2 * m + k:16 + 2 * m + k]
        sems = refs[16 + 2 * m + k:19 + 2 * m + k] if m else ()
        stage_ref = refs[-3] if k else None
        j, i = pl.program_id(0), pl.program_id(1)
        if m:
            pl.when((j == 0) & (i == 0))(lambda: plan.start(ins, outs, sems))

        def chunk_copies(jj):
            send_sems, recv_sems = refs[-2:]
            px, py, pc = _place()
            me = 4 * px + 2 * py + pc
            tx, ty = ord_ref[jj] // 2, ord_ref[jj] % 2
            copies = []
            for n_ in range(3):
                for h_ in range(2):
                    copies.append((pltpu.make_async_remote_copy(
                        src_ref=stage_ref.at[n_, pl.ds(h_ * hf, hf)], dst_ref=lands[n_].at[me],
                        send_sem=send_sems.at[6 * jj + 2 * n_ + h_], recv_sem=recv_sems.at[3 * me + n_],
                        device_id=(tx, ty, h_), device_id_type=MESH), (tx != px) | (ty != py) | (pc != h_)))
            return copies

        def arrivals():
            send_sems, recv_sems = refs[-2:]
            px, py, pc = _place()
            me = 4 * px + 2 * py + pc
            for s_ in range(N_DEV):
                for n_ in range(3):
                    slot = lands[n_].at[s_]
                    cp = pltpu.make_async_remote_copy(
                        src_ref=slot, dst_ref=slot, send_sem=send_sems.at[0], recv_sem=recv_sems.at[3 * s_ + n_],
                        device_id=(px, py, pc), device_id_type=MESH)
                    pl.when(me != s_)(cp.wait_recv)

        _, xn = _rms(x_ref[...])
        h = (xn * g_ref[...]).astype(BF16)
        dz = (0.5 * do_ref[...]).astype(BF16)
        wgv, wuv, wdv = wg_ref[0], wu_ref[0], wd_ref[0]
        gp, up = gp_ref[0].astype(F32), up_ref[0].astype(F32)
        s = jax.nn.sigmoid(gp)
        sg = gp * s
        a = (sg * up).astype(BF16)
        da = _dot_nt(dz, wdv)
        dup = (da * sg).astype(BF16)
        dgp = (da * up * (s * (1.0 + gp * (1.0 - s)))).astype(BF16)

        @pl.when(i == 0)
        def _():
            dwg_ref[...] = jnp.zeros_like(dwg_ref)
            dwu_ref[...] = jnp.zeros_like(dwu_ref)
            dwd_ref[...] = jnp.zeros_like(dwd_ref)

        dwd_ref[...] += _dot_tn(a, dz)
        dwg_ref[...] += _dot_tn(dgp, h)
        dwu_ref[...] += _dot_tn(dup, h)
        dhp_ref[0] = (_dot(dgp, wgv) + _dot(dup, wuv)).astype(BF16)

        @pl.when(i == ni - 1)
        def _():
            dwg_out[0] = dwg_ref[...].astype(BF16)
            dwu_out[0] = dwu_ref[...].astype(BF16)
            dwd_out[0] = dwd_ref[...].astype(BF16)
            if k:
                @pl.when(j >= 1)
                def _():
                    for cp, leaves in chunk_copies(j - 1):
                        pl.when(leaves)(cp.wait_send)
                for n_, acc in enumerate((dwg_ref, dwu_ref, dwd_ref)):
                    stage_ref[n_] = acc[...].astype(BF16)
                for cp, leaves in chunk_copies(j):
                    pl.when(leaves)(cp.start)

                @pl.when(j == N_SHARD - 1)
                def _():
                    for cp, leaves in chunk_copies(N_SHARD - 1):
                        pl.when(leaves)(cp.wait_send)
                    arrivals()

        if m:
            pl.when((j == N_SHARD - 1) & (i == ni - 1))(lambda: plan.finish(ins, outs, sems))

    chunk = pl.BlockSpec((1, FS, d), lambda j, i, o: (o[j], 0, 0))
    tok = pl.BlockSpec((tm, d), lambda j, i, o: (i, 0))
    pre = pl.BlockSpec((1, tm, FS), lambda j, i, o: (o[j], i, 0))
    grid_spec = pltpu.PrefetchScalarGridSpec(
        num_scalar_prefetch=1, grid=(N_SHARD, ni),
        in_specs=[tok, tok, pl.BlockSpec((1, d), lambda j, i, o: (0, 0)), chunk, chunk, chunk, pre, pre] + [ANY] * m,
        out_specs=[chunk, chunk, chunk, pl.BlockSpec((1, tm, d), lambda j, i, o: (o[j], i, 0))] + [ANY] * (m + k),
        scratch_shapes=[pltpu.VMEM((FS, d), F32), pltpu.VMEM((FS, d), F32), pltpu.VMEM((FS, d), F32)]
        + (plan.scratch if m else [])
        + ([pltpu.VMEM((3, FS, d), BF16), pltpu.SemaphoreType.DMA((6 * N_SHARD,)),
            pltpu.SemaphoreType.DMA((3 * N_DEV,))] if k else []))
    return pl.pallas_call(
        body, name=name, grid_spec=grid_spec,
        out_shape=[jax.ShapeDtypeStruct((N_SHARD, FS, d), BF16),
                   jax.ShapeDtypeStruct((N_SHARD, FS, d), BF16),
                   jax.ShapeDtypeStruct((N_SHARD, FS, d), BF16),
                   jax.ShapeDtypeStruct((N_SHARD, t, d), BF16)] + plan.out_shape
        + [jax.ShapeDtypeStruct((N_DEV, hf, d), BF16)] * k,
        compiler_params=_params(("arbitrary", "arbitrary")),
    )(order, x, dout, g, wg, wu, wd, gate, up_pre, *scatter)


def norm_bwd(x, g, dhp, dres, name):
    t, d = x.shape
    p = dhp.shape[0]
    tm = TOKEN_TILE

    def body(x_ref, g_ref, dhp_ref, dres_ref, dx_ref, dg_ref):
        i = pl.program_id(0)
        r, xn = _rms(x_ref[...])
        dh = dhp_ref[0].astype(F32)
        for q in range(1, p):
            dh = dh + dhp_ref[q].astype(F32)
        dhg = dh * g_ref[...]
        dx_ref[...] = dres_ref[...] + r * (dhg - xn * jnp.mean(dhg * xn, axis=-1, keepdims=True))

        @pl.when(i == 0)
        def _():
            dg_ref[...] = jnp.zeros_like(dg_ref)

        dg_ref[...] += jnp.sum(dh * xn, axis=0, keepdims=True)

    return pl.pallas_call(
        body, name=name, grid=(t // tm,),
        in_specs=[pl.BlockSpec((tm, d), lambda i: (i, 0)),
                  pl.BlockSpec((1, d), lambda i: (0, 0)),
                  pl.BlockSpec((p, tm, d), lambda i: (0, i, 0)),
                  pl.BlockSpec((tm, d), lambda i: (i, 0))],
        out_specs=[pl.BlockSpec((tm, d), lambda i: (i, 0)),
                   pl.BlockSpec((1, d), lambda i: (0, 0))],
        out_shape=[jax.ShapeDtypeStruct((t, d), F32), jax.ShapeDtypeStruct((1, d), F32)],
        compiler_params=_params(("arbitrary",)),
    )(x, g, dhp, dres)


def out_proj_bwd(x, g, dhp, dres, w, oa, ob_t):
    t, d = x.shape
    half = w.shape[0] // 2
    p = dhp.shape[0]
    tm = TOKEN_TILE
    ni = t // tm

    def body(x_ref, g_ref, dhp_ref, dres_ref, w_ref, oa_ref, obt_ref,
             dx_ref, dg_ref, da_ref, db_ref, dbt_ref, dw_ref, acc_ref):
        i = pl.program_id(0)

        @pl.when(i == 0)
        def _():
            dg_ref[...] = jnp.zeros_like(dg_ref)
            acc_ref[...] = jnp.zeros_like(acc_ref)

        r, xn = _rms(x_ref[...])
        dh = dhp_ref[0].astype(F32)
        for s in range(1, p):
            dh = dh + dhp_ref[s].astype(F32)
        dhg = dh * g_ref[...]
        dx = dres_ref[...] + r * (dhg - xn * jnp.mean(dhg * xn, axis=-1, keepdims=True))
        dx_ref[...] = dx
        dg_ref[...] += jnp.sum(dh * xn, axis=0, keepdims=True)
        dxb = dx.astype(BF16)
        da_ref[...] = _dot_nt(dxb, w_ref[0:half, :]).astype(BF16)
        db_ref[...] = _dot_nt(dxb, w_ref[half:2 * half, :]).astype(BF16)
        dbt_ref[...] = _dot_nt(w_ref[half:2 * half, :], dxb).astype(BF16)
        acc_ref[0:half, :] += _dot_tn(oa_ref[...], dxb)
        acc_ref[half:2 * half, :] += _dot(obt_ref[...], dxb)

        @pl.when(i == ni - 1)
        def _():
            dw_ref[...] = acc_ref[...].astype(BF16)

    row = lambda w_: pl.BlockSpec((tm, w_), lambda i: (i, 0))
    col = pl.BlockSpec((half, tm), lambda i: (0, i))
    whole = pl.BlockSpec((2 * half, d), lambda i: (0, 0))
    vec = pl.BlockSpec((1, d), lambda i: (0, 0))
    return pl.pallas_call(
        body, name="out_proj_bwd", grid=(ni,),
        in_specs=[row(d), vec, pl.BlockSpec((p, tm, d), lambda i: (0, i, 0)), row(d), whole, row(half), col],
        out_specs=[row(d), vec, row(half), row(half), col, whole],
        out_shape=[jax.ShapeDtypeStruct((t, d), F32), jax.ShapeDtypeStruct((1, d), F32),
                   jax.ShapeDtypeStruct((t, half), BF16), jax.ShapeDtypeStruct((t, half), BF16),
                   jax.ShapeDtypeStruct((half, t), BF16), jax.ShapeDtypeStruct((2 * half, d), BF16)],
        scratch_shapes=[pltpu.VMEM((2 * half, d), F32)],
        compiler_params=_params(("arbitrary",)),
    )(x, g, dhp, dres, w, oa, ob_t)


def _lane(shape):
    return lax.broadcasted_iota(jnp.int32, shape, 1)


PAIR = (0, HD, LANES)
Q_HEAD = (0, HD, HD + ROPE, LANES)
K_ROPE = (0, ROPE, LANES)


def _seg_mean(z, bounds):
    seg = lambda v: sum([(v >= b).astype(jnp.int32) for b in bounds[1:-1]], jnp.zeros_like(v))
    rows = seg(lax.broadcasted_iota(jnp.int32, (LANES, LANES), 0))
    cols = seg(lax.broadcasted_iota(jnp.int32, (LANES, LANES), 1))
    same = (rows == cols).astype(BF16)
    lane = _lane((1, LANES))
    inv = sum([jnp.where((lane >= a) & (lane < b), 1.0 / (b - a), 0.0) for a, b in zip(bounds[:-1], bounds[1:])])
    hi = z.astype(BF16)
    lo = (z - hi.astype(F32)).astype(BF16)
    return (_dot(hi, same) + _dot(lo, same)) * inv


def _seg_norm(x, bounds):
    r = lax.rsqrt(_seg_mean(x * x, bounds) + EPS)
    return r, x * r


def _seg_norm_bwd(r, xn, dyg, bounds):
    return r * (dyg - xn * _seg_mean(dyg * xn, bounds))


A_TM = 256


def mix_fwd(x, g, w, gq, gk, gcq, gckv):
    t, d = x.shape
    tm = TOKEN_TILE

    def body(x_ref, g_ref, w_ref, gq_ref, gk_ref, gcq_ref, gckv_ref,
             h_ref, p_ref, qa_ref, ka_ref, va_ref, cq_ref, ckv_ref):
        _, xn = _rms(x_ref[...])
        h = (xn * g_ref[...]).astype(BF16)
        h_ref[...] = h
        p_ref[...] = _dot_nt(h, w_ref[...])
        for p in range(4):
            sl = slice(LANES * p, LANES * (p + 1))
            _, xn = _seg_norm(p_ref[:, sl], PAIR)
            qa_ref[:, sl] = (xn * gq_ref[:, sl] * 0.125).astype(BF16)
            _, xn = _seg_norm(p_ref[:, 512 + LANES * p:512 + LANES * (p + 1)], PAIR)
            ka_ref[:, sl] = (xn * gk_ref[:, sl]).astype(BF16)
        va_ref[...] = p_ref[:, 1024:1536].astype(BF16)
        _, xn = _rms(p_ref[:, 1536:1792])
        cq_ref[...] = (xn * gcq_ref[...]).astype(BF16)
        _, xn = _rms(p_ref[:, 1792:1920])
        ckv_ref[...] = (xn * gckv_ref[...]).astype(BF16)

    row = lambda w: pl.BlockSpec((tm, w), lambda i: (i, 0))
    vec = lambda w: pl.BlockSpec((1, w), lambda i: (0, 0))
    return pl.pallas_call(
        body, name="mix_fwd", grid=(t // tm,),
        in_specs=[row(d), vec(d), pl.BlockSpec((PROJ_W, d), lambda i: (0, 0)), vec(512), vec(512), vec(256), vec(128)],
        out_specs=[row(d), row(PROJ_W), row(512), row(512), row(512), row(256), row(128)],
        out_shape=[jax.ShapeDtypeStruct((t, d), BF16), jax.ShapeDtypeStruct((t, PROJ_W), F32)]
        + [jax.ShapeDtypeStruct((t, w_), BF16) for w_ in (512, 512, 512, 256, 128)],
        compiler_params=_params(("parallel",)),
    )(x, g, w, gq, gk, gcq, gckv)


def mix_bwd(proj, x, h, dres, w, g, dqa, dkp, dvp, dcq, dckv, dkr, gq, gk, gcq, gckv):
    t, d = x.shape
    tm = A_TM
    nb = t // tm

    def body(p_ref, x_ref, h_ref, dres_ref, w_ref, g_ref, dqa_ref, dk0_ref, dk1_ref, dk2_ref, dv0_ref, dv1_ref,
             dv2_ref, dcq_ref, dckv_ref, dkr_ref, gq_ref, gk_ref, gcq_ref, gckv_ref,
             dx_ref, dw_ref, dg_ref, dgq_ref, dgk_ref, dgcq_ref, dgckv_ref, dp_ref, acc_ref):
        i = pl.program_id(0)

        @pl.when(i == 0)
        def _():
            for ref in (acc_ref, dg_ref, dgq_ref, dgk_ref, dgcq_ref, dgckv_ref):
                ref[...] = jnp.zeros_like(ref)

        has1 = (i + 1 < nb).astype(F32)
        has2 = (i + 2 < nb).astype(F32)
        for p in range(4):
            sl = slice(LANES * p, LANES * (p + 1))
            r, xn = _seg_norm(p_ref[:, sl], PAIR)
            dy = dqa_ref[:, sl] * 0.125
            dp_ref[:, sl] = _seg_norm_bwd(r, xn, dy * gq_ref[:, sl], PAIR).astype(BF16)
            dgq_ref[:, sl] += jnp.sum(dy * xn, axis=0, keepdims=True)
            ks = slice(512 + LANES * p, 512 + LANES * (p + 1))
            r, xn = _seg_norm(p_ref[:, ks], PAIR)
            dy = dk0_ref[0, :, sl] + has1 * dk1_ref[0, :, sl] + has2 * dk2_ref[0, :, sl]
            dp_ref[:, ks] = _seg_norm_bwd(r, xn, dy * gk_ref[:, sl], PAIR).astype(BF16)
            dgk_ref[:, sl] += jnp.sum(dy * xn, axis=0, keepdims=True)
        dp_ref[:, 1024:1536] = (dv0_ref[0] + has1 * dv1_ref[0] + has2 * dv2_ref[0]).astype(BF16)
        for (a, b, dlat_ref, glat_ref, dglat_ref) in ((1536, 1792, dcq_ref, gcq_ref, dgcq_ref),
                                                      (1792, 1920, dckv_ref, gckv_ref, dgckv_ref)):
            r, xn = _rms(p_ref[:, a:b])
            dy = dlat_ref[...]
            dyg = dy * glat_ref[...]
            dp_ref[:, a:b] = (r * (dyg - xn * jnp.mean(dyg * xn, axis=-1, keepdims=True))).astype(BF16)
            dglat_ref[...] += jnp.sum(dy * xn, axis=0, keepdims=True)
        dp_ref[:, 1920:2048] = dkr_ref[...].astype(BF16)

        dproj = dp_ref[...]
        acc_ref[...] += _dot_tn(dproj, h_ref[...])
        dh = _dot(dproj, w_ref[...])
        r, xn = _rms(x_ref[...])
        dhg = dh * g_ref[...]
        dx_ref[...] = dres_ref[...] + r * (dhg - xn * jnp.mean(dhg * xn, axis=-1, keepdims=True))
        dg_ref[...] += jnp.sum(dh * xn, axis=0, keepdims=True)

        @pl.when(i == nb - 1)
        def _():
            dw_ref[...] = acc_ref[...].astype(BF16)

    row = lambda w_: pl.BlockSpec((tm, w_), lambda i: (i, 0))
    vec = lambda w_: pl.BlockSpec((1, w_), lambda i: (0, 0))
    part = lambda s: pl.BlockSpec((1, tm, 512), lambda i: (s, jnp.minimum(i + s, nb - 1), 0))
    whole = pl.BlockSpec((PROJ_W, d), lambda i: (0, 0))
    return pl.pallas_call(
        body, name="mix_bwd", grid=(nb,),
        in_specs=[row(PROJ_W), row(d), row(d), row(d), whole, vec(d), row(512), part(0), part(1), part(2),
                  part(0), part(1), part(2), row(256), row(128), row(128), vec(512), vec(512), vec(256), vec(128)],
        out_specs=[row(d), whole, vec(d), vec(512), vec(512), vec(256), vec(128)],
        out_shape=[jax.ShapeDtypeStruct((t, d), F32), jax.ShapeDtypeStruct((PROJ_W, d), BF16)]
        + [jax.ShapeDtypeStruct((1, w_), F32) for w_ in (d, 512, 512, 256, 128)],
        scratch_shapes=[pltpu.VMEM((tm, PROJ_W), BF16), pltpu.VMEM((PROJ_W, d), F32)],
        compiler_params=_params(("arbitrary",)),
    )(proj, x, h, dres, w, g, dqa, dkp, dkp, dkp, dvp, dvp, dvp, dcq, dckv, dkr, gq, gk, gcq, gckv)


def _roll(x, shift):
    return pltpu.roll(x, shift % LANES, 1)


def _rope(y, c, s1, s2):
    return y * c + _roll(y, -16) * s1 + _roll(y, 16) * s2


def _rope_bwd(d, c, s1, s2):
    return d * c + _roll(d * s1, 16) + _roll(d * s2, -16)


def _up_proj(cq_ref, ckv_ref, wuq_ref, wukv_ref):
    return _dot_nt(cq_ref[...], wuq_ref[...]), _dot(ckv_ref[...], wukv_ref[...])


def prep2_fwd(cqn, ckvn, proj, wuq, wukv, gq, gk, gkr, tabs):
    t = cqn.shape[0]
    tm = TOKEN_TILE

    def body(cq_ref, ckv_ref, kr_ref, wuq_ref, wukv_ref, gq_ref, gk_ref, gkr_ref, tab_ref, qf_ref, kf_ref, vp_ref):
        q_all, kv_all = _up_proj(cq_ref, ckv_ref, wuq_ref, wukv_ref)
        _, xn = _seg_norm(kr_ref[...], K_ROPE)
        kpe = _roll(_rope(xn * gkr_ref[...], tab_ref[3], tab_ref[4], tab_ref[5]), 64)
        for h in range(HEADS):
            sl = slice(LANES * h, LANES * (h + 1))
            _, xn = _seg_norm(q_all[:, sl], Q_HEAD)
            qf_ref[:, sl] = (_rope(xn * gq_ref[...], tab_ref[0], tab_ref[1], tab_ref[2]) * B_SCALE2).astype(BF16)
            x = kv_all[:, sl]
            lo = _lane(x.shape) < HD
            _, xkn = _seg_norm(jnp.where(lo, x, 0.0), PAIR)
            kf_ref[:, sl] = (xkn * gk_ref[...] + kpe).astype(BF16)
            if h % 2 == 0:
                v_even = _roll(x, 64)
            else:
                vp_ref[:, LANES * (h // 2):LANES * (h // 2 + 1)] = jnp.where(lo, v_even, x).astype(BF16)

    row = lambda w: pl.BlockSpec((tm, w), lambda i: (i, 0))
    vec = lambda w: pl.BlockSpec((1, w), lambda i: (0, 0))
    full = lambda a: pl.BlockSpec(a.shape, lambda i: (0, 0))
    return pl.pallas_call(
        body, name="prep2_fwd", grid=(t // tm,),
        in_specs=[row(256), row(128), pl.BlockSpec((tm, LANES), lambda i: (i, 15)), full(wuq), full(wukv),
                  vec(128), vec(128), vec(128), pl.BlockSpec((6, tm, LANES), lambda i: (0, i, 0))],
        out_specs=[row(1024), row(1024), row(512)],
        out_shape=[jax.ShapeDtypeStruct((t, 1024), BF16), jax.ShapeDtypeStruct((t, 1024), BF16),
                   jax.ShapeDtypeStruct((t, 512), BF16)],
        compiler_params=_params(("parallel",)),
    )(cqn, ckvn, proj, wuq, wukv, gq, gk, gkr, tabs)


def prep2_bwd(cqn, ckvn, proj, wuq, wukv, dqf, dkf, dvp, gq, gk, gkr, tabs):
    t = cqn.shape[0]
    tm = TOKEN_TILE

    def body(cq_ref, ckv_ref, kr_ref, wuq_ref, wukv_ref, dqf_ref, dkf_ref, dvp_ref, gq_ref, gk_ref, gkr_ref, tab_ref,
             dcq_ref, dckv_ref, dkr_ref, dwuq_ref, dwukv_ref, dgq_ref, dgk_ref, dgkr_ref, dq_ref, dkv_ref):
        i = pl.program_id(0)

        @pl.when(i == 0)
        def _():
            for ref in (dwuq_ref, dwukv_ref, dgq_ref, dgk_ref, dgkr_ref):
                ref[...] = jnp.zeros_like(ref)

        q_all, kv_all = _up_proj(cq_ref, ckv_ref, wuq_ref, wukv_ref)
        dgq = jnp.zeros((1, LANES), F32)
        dgk = jnp.zeros((1, LANES), F32)
        dkpe = jnp.zeros((tm, LANES), F32)
        for h in range(HEADS):
            sl = slice(LANES * h, LANES * (h + 1))
            lane = _lane((tm, LANES))
            mn, mr = lane < HD, (lane >= HD) & (lane < HD + ROPE)
            r, xn = _seg_norm(q_all[:, sl], Q_HEAD)
            dy = _rope_bwd(dqf_ref[sl, :].T, tab_ref[0], tab_ref[1], tab_ref[2])
            dyg = dy * gq_ref[...]
            dq_ref[:, sl] = _seg_norm_bwd(r, xn, dyg, Q_HEAD).astype(BF16)
            dgq = dgq + jnp.sum(dy * xn, axis=0, keepdims=True)

            x = kv_all[:, sl]
            dk = dkf_ref[:, sl]
            rk, xkn = _seg_norm(jnp.where(mn, x, 0.0), PAIR)
            dyk = jnp.where(mn, dk, 0.0)
            dxk = _seg_norm_bwd(rk, xkn, dyk * gk_ref[...], PAIR)
            dgk = dgk + jnp.sum(dyk * xkn, axis=0, keepdims=True)
            dkpe = dkpe + jnp.where(mr, dk, 0.0)
            dvpair = dvp_ref[:, LANES * (h // 2):LANES * (h // 2 + 1)]
            dv = _roll(dvpair, 64) if h % 2 == 0 else dvpair
            dkv_ref[:, sl] = jnp.where(mn, dxk, dv).astype(BF16)

        r, xn = _seg_norm(kr_ref[...], K_ROPE)
        dy = _rope_bwd(_roll(dkpe, 64), tab_ref[3], tab_ref[4], tab_ref[5])
        dkr_ref[...] = _seg_norm_bwd(r, xn, dy * gkr_ref[...], K_ROPE)
        dgq_ref[...] += dgq
        dgk_ref[...] += dgk
        dgkr_ref[...] += jnp.sum(dy * xn, axis=0, keepdims=True)
        dqb, dkvb = dq_ref[...], dkv_ref[...]
        dcq_ref[...] = _dot(dqb, wuq_ref[...])
        dckv_ref[...] = _dot_nt(dkvb, wukv_ref[...])
        dwuq_ref[...] += _dot_tn(dqb, cq_ref[...])
        dwukv_ref[...] += _dot_tn(ckv_ref[...], dkvb)

    row = lambda w: pl.BlockSpec((tm, w), lambda i: (i, 0))
    vec = lambda w: pl.BlockSpec((1, w), lambda i: (0, 0))
    full = lambda a: pl.BlockSpec(a.shape, lambda i: (0, 0))
    return pl.pallas_call(
        body, name="prep2_bwd", grid=(t // tm,),
        in_specs=[row(256), row(128), pl.BlockSpec((tm, LANES), lambda i: (i, 15)), full(wuq), full(wukv),
                  pl.BlockSpec((1024, tm), lambda i: (0, i)), row(1024), row(512),
                  vec(128), vec(128), vec(128), pl.BlockSpec((6, tm, LANES), lambda i: (0, i, 0))],
        out_specs=[row(256), row(128), row(128), full(wuq), full(wukv), vec(128), vec(128), vec(128)],
        out_shape=[jax.ShapeDtypeStruct((t, 256), F32), jax.ShapeDtypeStruct((t, 128), F32),
                   jax.ShapeDtypeStruct((t, LANES), F32), jax.ShapeDtypeStruct(wuq.shape, F32),
                   jax.ShapeDtypeStruct(wukv.shape, F32)] + [jax.ShapeDtypeStruct((1, LANES), F32)] * 3,
        scratch_shapes=[pltpu.VMEM((tm, 1024), BF16), pltpu.VMEM((tm, 1024), BF16)],
        compiler_params=_params(("arbitrary",)),
    )(cqn, ckvn, proj, wuq, wukv, dqf, dkf, dvp, gq, gk, gkr, tabs)


A_TQ = 256
A_WIN = 3 * A_TQ


def _a_specs(t):
    nb = t // A_TQ
    blk = lambda s: pl.BlockSpec((A_TQ, 512), lambda i: (jnp.maximum(i - s, 0), 0))
    return nb, blk


def _a_exp(q_ref, kc, b_ref, head, sl, lo):
    hm = lo if head % 2 == 0 else ~lo
    qm = jnp.where(hm, q_ref[:, sl], jnp.zeros((), BF16))
    s = _dot_nt(qm, kc) + b_ref[0, head]
    e = jnp.exp(s - jnp.max(s, axis=-1, keepdims=True))
    return hm, qm, e, 1.0 / jnp.sum(e, axis=-1, keepdims=True)


def _a_bias_spec():
    return pl.BlockSpec((1, HEADS, A_TQ, A_WIN), lambda i: (jnp.minimum(i, 2), 0, 0, 0))


def attn_a_fwd(qa, ka, va, bias):
    t = qa.shape[0]
    nb, blk = _a_specs(t)

    def body(q_ref, k2_ref, k1_ref, k0_ref, v2_ref, v1_ref, v0_ref, b_ref, o_ref):
        lo = _lane((A_TQ, LANES)) < HD
        for p in range(4):
            sl = slice(LANES * p, LANES * (p + 1))
            kc = jnp.concatenate([k2_ref[:, sl], k1_ref[:, sl], k0_ref[:, sl]], axis=0)
            vc = jnp.concatenate([v2_ref[:, sl], v1_ref[:, sl], v0_ref[:, sl]], axis=0)
            outs = []
            for h2 in range(2):
                _, _, e, inv = _a_exp(q_ref, kc, b_ref, 2 * p + h2, sl, lo)
                outs.append(_dot(e.astype(BF16), vc) * inv)
            o_ref[:, sl] = jnp.where(lo, outs[0], outs[1]).astype(BF16)

    return pl.pallas_call(
        body, name="attn_a_fwd", grid=(nb,),
        in_specs=[blk(0), blk(2), blk(1), blk(0), blk(2), blk(1), blk(0), _a_bias_spec()],
        out_specs=pl.BlockSpec((A_TQ, 512), lambda i: (i, 0)),
        out_shape=jax.ShapeDtypeStruct((t, 512), BF16),
        compiler_params=_params(("parallel",)),
    )(qa, ka, ka, ka, va, va, va, bias)


def attn_a_bwd(qa, ka, va, bias, do):
    t = qa.shape[0]
    nb, blk = _a_specs(t)

    def body(q_ref, k2_ref, k1_ref, k0_ref, v2_ref, v1_ref, v0_ref, b_ref, do_ref, dq_ref, dk_ref, dv_ref, db_ref):
        qb = pl.program_id(0)

        @pl.when(qb == 0)
        def _():
            db_ref[...] = jnp.zeros_like(db_ref)

        lo = _lane((A_TQ, LANES)) < HD
        for p in range(4):
            sl = slice(LANES * p, LANES * (p + 1))
            kc = jnp.concatenate([k2_ref[:, sl], k1_ref[:, sl], k0_ref[:, sl]], axis=0)
            vc = jnp.concatenate([v2_ref[:, sl], v1_ref[:, sl], v0_ref[:, sl]], axis=0)
            dqs = []
            dkt = jnp.zeros((LANES, A_WIN), F32)
            dvt = jnp.zeros((LANES, A_WIN), F32)
            for h2 in range(2):
                head = 2 * p + h2
                hm, qm, e, inv = _a_exp(q_ref, kc, b_ref, head, sl, lo)
                pr = e * inv
                dom = jnp.where(hm, do_ref[:, sl], jnp.zeros((), BF16))
                dp = _dot_nt(dom, vc)
                ds = pr * (dp - jnp.sum(pr * dp, axis=-1, keepdims=True))
                db_ref[head] += ds
                dsb = ds.astype(BF16)
                dqs.append(_dot(dsb, kc))
                dkt = dkt + _dot_tn(qm, dsb)
                dvt = dvt + _dot_tn(dom, pr.astype(BF16))
            dq_ref[:, sl] = jnp.where(lo, dqs[0], dqs[1])
            dkc, dvc = dkt.T, dvt.T
            for s in range(3):
                rows = slice(A_TQ * (2 - s), A_TQ * (3 - s))
                dk_ref[s, :, sl] = dkc[rows]
                dv_ref[s, :, sl] = dvc[rows]

    share = pl.BlockSpec((3, A_TQ, 512), lambda i: (0, i, 0))
    return pl.pallas_call(
        body, name="attn_a_bwd", grid=(nb,),
        in_specs=[blk(0), blk(2), blk(1), blk(0), blk(2), blk(1), blk(0), _a_bias_spec(), blk(0)],
        out_specs=[pl.BlockSpec((A_TQ, 512), lambda i: (i, 0)), share, share,
                   pl.BlockSpec((HEADS, A_TQ, A_WIN), lambda i: (0, 0, 0))],
        out_shape=[jax.ShapeDtypeStruct((t, 512), F32), jax.ShapeDtypeStruct((3, t, 512), F32),
                   jax.ShapeDtypeStruct((3, t, 512), F32), jax.ShapeDtypeStruct((HEADS, A_TQ, A_WIN), F32)],
        compiler_params=_params(("arbitrary",)),
    )(qa, ka, ka, ka, va, va, va, bias, do)


B_T = 1024


B_SCALE2 = B_SCALE * 1.4426950408889634
B_FWD_HEADS = 8
_B_ALL = slice(0, B_T)
_B_LO, _B_HI = slice(0, B_T // 2), slice(B_T // 2, B_T)
_B_DIAG = ((_B_LO, _B_LO), (_B_LO, _B_HI), (_B_HI, _B_HI))


def _tri_tables(n, by_query):
    pairs = [(i, j) for i in range(n) for j in range(i + 1)] if by_query else [(i, j) for j in range(n) for i in range(j, n)]
    return (np.asarray([p[0] for p in pairs], np.int32), np.asarray([p[1] for p in pairs], np.int32))


def _b_mask_t(s):
    kc = lax.broadcasted_iota(jnp.int32, s.shape, 0) // CHUNK
    qc = lax.broadcasted_iota(jnp.int32, s.shape, 1) // CHUNK
    return jnp.where(kc <= qc, s, NEG)


def attn_b_fwd(qf, kf, vp):
    t = qf.shape[0]
    n = t // B_T
    qtab, ktab = _tri_tables(n, by_query=True)
    hps = B_FWD_HEADS

    def body(qt_ref, kt_ref, q_ref, k_ref, v_ref, o_ref, lse_ref, m_s, l_s, acc_s):
        qb, kb = qt_ref[pl.program_id(1)], kt_ref[pl.program_id(1)]

        @pl.when(kb == 0)
        def _():
            m_s[...] = jnp.full_like(m_s, NEG)
            l_s[...] = jnp.zeros_like(l_s)
            acc_s[...] = jnp.zeros_like(acc_s)

        def block(kr, qr, masked):
            for h2 in range(hps):
                sl = slice(LANES * h2, LANES * (h2 + 1))
                v = v_ref[kr, LANES * (h2 // 2):LANES * (h2 // 2 + 1)]
                s = _dot_nt(k_ref[kr, sl], q_ref[qr, sl])
                if masked:
                    s = _b_mask_t(s)
                m_prev = m_s[h2, :, qr]
                m_new = jnp.maximum(m_prev, jnp.max(s, axis=0, keepdims=True))
                alpha = jnp.exp2(m_prev - m_new)
                pr = jnp.exp2(s - m_new)
                l_s[h2, :, qr] = alpha * l_s[h2, :, qr] + jnp.sum(pr, axis=0, keepdims=True)
                acc_s[h2, :, qr] = alpha * acc_s[h2, :, qr] + _dot_tn(v, pr.astype(BF16))
                m_s[h2, :, qr] = m_new

        @pl.when(kb < qb)
        def _():
            block(_B_ALL, _B_ALL, False)

        @pl.when(kb == qb)
        def _():
            for kr, qr in _B_DIAG:
                block(kr, qr, kr == qr)
            for h2 in range(hps):
                l = l_s[h2]
                rows = slice(HD * (h2 % 2), HD * (h2 % 2 + 1))
                o_ref[HD * h2:HD * (h2 + 1), :] = (acc_s[h2, rows, :] * (1.0 / l)).astype(BF16)
                lse_ref[h2 // 2, h2 % 2:h2 % 2 + 1, :] = m_s[h2] + jnp.log2(l)

    grid_spec = pltpu.PrefetchScalarGridSpec(
        num_scalar_prefetch=2, grid=(HEADS // hps, len(qtab)),
        in_specs=[pl.BlockSpec((B_T, LANES * hps), lambda p, s, qt, kt: (qt[s], p)),
                  pl.BlockSpec((B_T, LANES * hps), lambda p, s, qt, kt: (kt[s], p)),
                  pl.BlockSpec((B_T, HD * hps), lambda p, s, qt, kt: (kt[s], p))],
        out_specs=[pl.BlockSpec((HD * hps, B_T), lambda p, s, qt, kt: (p, qt[s])),
                   pl.BlockSpec((hps // 2, 2, B_T), lambda p, s, qt, kt: (p, 0, qt[s]))],
        scratch_shapes=[pltpu.VMEM((hps, 1, B_T), F32), pltpu.VMEM((hps, 1, B_T), F32),
                        pltpu.VMEM((hps, LANES, B_T), F32)])
    return pl.pallas_call(
        body, name="attn_b_fwd", grid_spec=grid_spec,
        out_shape=[jax.ShapeDtypeStruct((512, t), BF16), jax.ShapeDtypeStruct((4, 2, t), F32)],
        compiler_params=_params(("parallel", "arbitrary")),
    )(jnp.asarray(qtab), jnp.asarray(ktab), qf, kf, vp)


def attn_b_bwd(qf, kf, vp, do, do_t, o_t, lse, scatter=()):
    t = qf.shape[0]
    n = t // B_T
    qtab, ktab = _tri_tables(n, by_query=False)
    plan = ScatterPlan(scatter)
    m = plan.n
    last = len(qtab) - 1

    def body(*refs):
        qt_ref, kt_ref, q_ref, k_ref, v_ref, do_ref, dot_ref, ot_ref, lse_ref = refs[:9]
        ins, (dq_ref, dk_ref, dv_ref), outs = refs[9:9 + m], refs[9 + m:12 + m], refs[12 + m:12 + 2 * m]
        sems = refs[12 + 2 * m:]
        qb, kb = qt_ref[pl.program_id(1)], kt_ref[pl.program_id(1)]
        if m:
            pl.when((pl.program_id(0) == 0) & (pl.program_id(1) == 0))(lambda: plan.start(ins, outs, sems))

        @pl.when(pl.program_id(1) == 0)
        def _():
            dq_ref[...] = jnp.zeros_like(dq_ref)

        @pl.when(qb == kb)
        def _():
            dk_ref[...] = jnp.zeros_like(dk_ref)
            dv_ref[...] = jnp.zeros_like(dv_ref)

        def block(kr, qr, masked):
            nq = qr.stop - qr.start
            cols = pl.ds(pl.multiple_of(qb * B_T + qr.start, LANES), nq)
            v = v_ref[kr, :]
            dov = do_ref[qr, :]
            prod = dot_ref[:, qr].astype(F32) * ot_ref[:, qr].astype(F32)
            lo = _lane((nq, LANES)) < HD
            for h2 in range(2):
                sl = slice(LANES * h2, LANES * (h2 + 1))
                hm = lo if h2 == 0 else ~lo
                q = q_ref[qr, sl]
                k = k_ref[kr, sl]
                dom = jnp.where(hm, dov, jnp.zeros((), BF16))
                delta = jnp.sum(prod[HD * h2:HD * (h2 + 1), :], axis=0, keepdims=True)
                s = _dot_nt(k, q)
                if masked:
                    s = _b_mask_t(s)
                pr = jnp.exp2(s - lse_ref[0, h2:h2 + 1, qr])
                dp = _dot_nt(v, dom)
                ds = (pr * (dp - delta)).astype(BF16)
                dk_ref[kr, sl] += _dot(ds, q) * (B_SCALE / B_SCALE2)
                dv_ref[kr, :] += _dot(pr.astype(BF16), dom)
                dq_ref[sl, cols] += _dot_tn(k, ds) * B_SCALE

        @pl.when(qb > kb)
        def _():
            block(_B_ALL, _B_ALL, False)

        @pl.when(qb == kb)
        def _():
            for kr, qr in _B_DIAG:
                block(kr, qr, kr == qr)

        if m:
            pl.when((pl.program_id(0) == 3) & (pl.program_id(1) == last))(lambda: plan.finish(ins, outs, sems))

    qrow = lambda w: pl.BlockSpec((B_T, w), lambda p, s, qt, kt: (qt[s], p))
    qcol = pl.BlockSpec((LANES, B_T), lambda p, s, qt, kt: (p, qt[s]))
    krow = lambda w: pl.BlockSpec((B_T, w), lambda p, s, qt, kt: (kt[s], p))
    grid_spec = pltpu.PrefetchScalarGridSpec(
        num_scalar_prefetch=2, grid=(4, len(qtab)),
        in_specs=[qrow(256), krow(256), krow(LANES), qrow(LANES), qcol, qcol,
                  pl.BlockSpec((1, 2, B_T), lambda p, s, qt, kt: (p, 0, qt[s]))] + [ANY] * m,
        out_specs=[pl.BlockSpec((256, t), lambda p, s, qt, kt: (p, 0)), krow(256), krow(LANES)] + [ANY] * m,
        scratch_shapes=plan.scratch if m else [])
    return pl.pallas_call(
        body, name="attn_b_bwd", grid_spec=grid_spec,
        out_shape=[jax.ShapeDtypeStruct((1024, t), F32), jax.ShapeDtypeStruct((t, 1024), F32),
                   jax.ShapeDtypeStruct((t, 512), F32)] + plan.out_shape,
        compiler_params=_params(("arbitrary", "arbitrary")),
    )(jnp.asarray(qtab), jnp.asarray(ktab), qf, kf, vp, do, do_t, o_t, lse, *scatter)


_U_LEN = A_TQ + A_WIN - 1


def _band_mask():
    a = np.arange(A_TQ)[:, None] // CHUNK
    b = np.arange(A_WIN)[None, :] // CHUNK
    return (b >= a) & (b <= a + A_LEFT)


def bias_block(table):
    h = table.shape[0]
    n_lo = A_WIN - 1 - 2 * A_TQ - A_MAX_REL
    ext = jnp.concatenate([jnp.repeat(table[:, :1], n_lo, axis=1), table,
                           jnp.repeat(table[:, -1:], _U_LEN - n_lo - table.shape[1], axis=1)], axis=1)
    row = jnp.pad(ext[:, ::-1], ((0, 0), (0, 1)))[:, None, :]
    band = _band_mask()
    first = [band & (np.arange(A_WIN)[None, :] >= 2 * A_TQ - A_TQ * v) for v in range(3)]
    keep = jnp.asarray(np.stack(first), jnp.int32)

    def body(r_ref, k_ref, o_ref):
        rows = jnp.broadcast_to(r_ref[0], (A_TQ, _U_LEN + 1))
        skew = pltpu.roll(rows, _U_LEN + 1 - (A_TQ - 1), 1, stride=1, stride_axis=0)
        toep = skew[:, :A_WIN]
        for v in range(3):
            o_ref[v, 0] = jnp.where(k_ref[v] != 0, toep, NEG)

    return pl.pallas_call(
        body, name="bias_block", grid=(h,),
        in_specs=[pl.BlockSpec((1, 1, _U_LEN + 1), lambda i: (i, 0, 0)),
                  pl.BlockSpec((3, A_TQ, A_WIN), lambda i: (0, 0, 0))],
        out_specs=pl.BlockSpec((3, 1, A_TQ, A_WIN), lambda i: (0, i, 0, 0)),
        out_shape=jax.ShapeDtypeStruct((3, h, A_TQ, A_WIN), F32),
        compiler_params=_params(("parallel",)),
    )(row, keep)


def bias_block_grad(db):
    h = db.shape[0]
    n_lo = A_WIN - 1 - 2 * A_TQ - A_MAX_REL
    skew = jnp.pad(db, ((0, 0), (0, 0), (A_TQ - 1, 0)))
    flat = jnp.pad(skew.reshape(h, A_TQ * _U_LEN), ((0, 0), (0, A_TQ)))
    ext = jnp.sum(flat.reshape(h, A_TQ, _U_LEN + 1), axis=1)[:, :_U_LEN][:, ::-1]
    n_tab = 2 * A_MAX_REL + 1
    first = jnp.sum(ext[:, :n_lo + 1], axis=1, keepdims=True)
    last = jnp.sum(ext[:, n_lo + n_tab - 1:], axis=1, keepdims=True)
    return jnp.concatenate([first, ext[:, n_lo + 1:n_lo + n_tab - 1], last], axis=1)


def rope_tabs(t):
    inv = 1.0 / (10000.0 ** (jnp.arange(0, ROPE, 2, dtype=F32) / ROPE))
    ang = jnp.arange(t, dtype=F32)[:, None] * inv[None, :]
    cos, sin = jnp.cos(ang), jnp.sin(ang)
    z = lambda w: jnp.zeros((t, w), F32)
    ck = jnp.concatenate([cos, cos, z(96)], axis=1)
    s1k = jnp.concatenate([-sin, z(112)], axis=1)
    s2k = jnp.concatenate([z(16), sin, z(96)], axis=1)
    cq = jnp.concatenate([jnp.ones((t, HD), F32), cos, cos, z(32)], axis=1)
    s1q = jnp.concatenate([z(HD), -sin, z(48)], axis=1)
    s2q = jnp.concatenate([z(HD + 16), sin, z(32)], axis=1)
    return jnp.stack([cq, s1q, s2q, ck, s1k, s2k])


def _pad_lanes(v, width):
    return jnp.pad(v, ((0, 0), (0, width - v.shape[1])))


LATE = ("w_in", "b_w_uq", "b_w_ukv", "w_out", "ffn2_w_gate", "ffn2_w_up", "ffn2_w_down")
FFN2 = ("ffn2_w_gate", "ffn2_w_up", "ffn2_w_down")


def kernel_layout(gathered):
    w = {n: v for n, v in gathered.items() if n.startswith("ffn")}
    if "w_in" in gathered:
        w["w_in"] = jnp.pad(gathered["w_in"].reshape(IN_COLS, D_MODEL), ((0, PROJ_W - IN_COLS), (0, 0)))
        uq = gathered["b_w_uq"].reshape(HEADS, HD + ROPE, 256)
        w["b_w_uq"] = jnp.pad(uq, ((0, 0), (0, LANES - HD - ROPE), (0, 0))).reshape(HEADS * LANES, 256)
        w["b_w_ukv"] = _shards_to_cols(gathered["b_w_ukv"])
        w["w_out"] = gathered["w_out"].reshape(N_SHARD * gathered["w_out"].shape[1], D_MODEL)
    return w


def local_step(x, target, w, late=None):
    t = x.shape[0]
    gq = jnp.tile(w["a_q_norm"], (1, HEADS))
    gk = jnp.tile(w["a_k_norm"], (1, HEADS))
    gq128 = _pad_lanes(jnp.concatenate([w["b_q_nope_norm"], w["b_q_rope_norm"]], axis=1), LANES)
    gk128 = _pad_lanes(w["b_k_nope_norm"], LANES)
    gkr128 = _pad_lanes(w["b_k_rope_norm"], LANES)
    tabs = rope_tabs(t)
    bias = bias_block(w["a_rel_bias"])

    if late is None:
        x1, gate1, up1 = ffn_fwd(x, w["ffn1_norm"], w["ffn1_w_gate"], w["ffn1_w_up"], w["ffn1_w_down"], "ffn_fwd")
    else:
        own, shard = late
        x1, gate1, up1, *got = ffn_fwd(x, w["ffn1_norm"], w["ffn1_w_gate"], w["ffn1_w_up"], w["ffn1_w_down"],
                                       "ffn_fwd_gather", gather=own)
        w = dict(w, **kernel_layout({n: lax.dynamic_update_index_in_dim(g_, o_, shard, 0)
                                     for n, g_, o_ in zip(LATE, got, own)}))
    h, proj, qa, ka, va, cqn, ckvn = mix_fwd(x1, w["mix_norm"], w["w_in"], gq, gk,
                                             w["b_q_lat_norm"], w["b_kv_lat_norm"])
    qf, kf, vp = prep2_fwd(cqn, ckvn, proj, w["b_w_uq"], w["b_w_ukv"], gq128, gk128, gkr128, tabs)
    oa = attn_a_fwd(qa, ka, va, bias)
    ob_t, lse = attn_b_fwd(qf, kf, vp)
    g = {}
    dx3, gate2, up2, g["final_norm"], loss, x2 = ffn_fwd(
        x1, w["ffn2_norm"], w["ffn2_w_gate"], w["ffn2_w_up"], w["ffn2_w_down"], "ffn_fwd_loss",
        loss_head=(w["final_norm"], target), pre_proj=(oa, ob_t, w["w_out"]))
    g["ffn2_w_gate"], g["ffn2_w_up"], g["ffn2_w_down"], dhp = ffn_bwd(
        x2, dx3, w["ffn2_norm"], w["ffn2_w_gate"], w["ffn2_w_up"], w["ffn2_w_down"], gate2, up2, "ffn_bwd")
    dx2, g["ffn2_norm"], d_oa, d_ob, d_ob_t, g["w_out"] = out_proj_bwd(
        x2, w["ffn2_norm"], dhp, dx3, w["w_out"], oa, ob_t)
    early = [g[n] for n in FFN2] + [g["w_out"].reshape(N_SHARD, -1, D_MODEL)]
    dqf, dkf, dvp, *landed_early = attn_b_bwd(qf, kf, vp, d_ob, d_ob_t, ob_t, lse,
                                              scatter=() if late is None else early)
    dqa, dkp, dvpa, dbias = attn_a_bwd(qa, ka, va, bias, d_oa)
    dcq, dckv, dkr, dwuq, dwukv, dgq128, dgk128, dgkr128 = prep2_bwd(
        cqn, ckvn, proj, w["b_w_uq"], w["b_w_ukv"], dqf, dkf, dvp, gq128, gk128, gkr128, tabs)
    g["b_w_uq"], g["b_w_ukv"] = dwuq.astype(BF16), dwukv.astype(BF16)
    dx1, g["w_in"], g["mix_norm"], dgq, dgk, g["b_q_lat_norm"], g["b_kv_lat_norm"] = mix_bwd(
        proj, x1, h, dx2, w["w_in"], w["mix_norm"], dqa, dkp, dvpa, dcq, dckv, dkr, gq, gk,
        w["b_q_lat_norm"], w["b_kv_lat_norm"])
    mid = [g["w_in"][:IN_COLS].reshape(N_SHARD, IN_COLS // N_SHARD, D_MODEL),
           g["b_w_uq"].reshape(HEADS, LANES, 256)[:, :HD + ROPE].reshape(N_SHARD, -1, 256),
           _cols_to_shards(g["b_w_ukv"])]
    g["ffn1_w_gate"], g["ffn1_w_up"], g["ffn1_w_down"], dhp, *landed_late = ffn_bwd(
        x, dx1, w["ffn1_norm"], w["ffn1_w_gate"], w["ffn1_w_up"], w["ffn1_w_down"], gate1, up1,
        "ffn_bwd" if late is None else "ffn_bwd_scatter", scatter=() if late is None else mid,
        spread=None if late is None else late[1])
    grad_x, g["ffn1_norm"] = norm_bwd(x, w["ffn1_norm"], dhp, dx1, "ffn_norm_bwd")
    landed = dict(zip(FFN2 + ("w_out", "w_in", "b_w_uq", "b_w_ukv", "ffn1_w_gate", "ffn1_w_up", "ffn1_w_down"),
                      landed_early + landed_late))

    g["a_q_norm"] = jnp.sum(dgq.reshape(HEADS, HD), axis=0, keepdims=True)
    g["a_k_norm"] = jnp.sum(dgk.reshape(HEADS, HD), axis=0, keepdims=True)
    g["a_rel_bias"] = bias_block_grad(dbias)
    g["b_q_nope_norm"] = dgq128[:, :HD]
    g["b_q_rope_norm"] = dgq128[:, HD:HD + ROPE]
    g["b_k_nope_norm"] = dgk128[:, :HD]
    g["b_k_rope_norm"] = dgkr128[:, :ROPE]
    return loss, grad_x, g, landed


ANY = pl.BlockSpec(memory_space=pl.ANY)
N_DEV = 8


def _place():
    return lax.axis_index("x"), lax.axis_index("y"), lax.axis_index("c")


def _flip(v, bit):
    return 1 - v if bit else v


BF16_ROWS = 16


def _split_axis(shape):
    return 0 if (shape[0] // 2) % BF16_ROWS == 0 else 1


def _half_shape(shape):
    axis = _split_axis(shape)
    return tuple(s // 2 if a == axis else s for a, s in enumerate(shape))


def _half(shape, core):
    axis = _split_axis(shape)
    size = shape[axis] // 2
    return tuple(pl.ds(core * size, size) if a == axis else slice(None) for a in range(2))


class GatherPlan:
    def __init__(self, ws):
        self.shapes = [w.shape for w in ws]
        self.n = len(ws)
        self.out_shape = [jax.ShapeDtypeStruct((N_SHARD,) + w.shape, w.dtype) for w in ws]
        self.scratch = [pltpu.SemaphoreType.DMA((6 * self.n,)), pltpu.SemaphoreType.DMA((6 * self.n,))]

    def _copies(self, ins, outs, sems):
        x, y, c = _place()
        s_me = 2 * x + y
        sibling = (x, y, 1 - c)
        send_sems, recv_sems = sems

        def remote(k, src, dst, to):
            return pltpu.make_async_remote_copy(src_ref=src, dst_ref=dst, send_sem=send_sems.at[k],
                                                recv_sem=recv_sems.at[k], device_id=to, device_id_type=MESH)

        ici, fwd = [], []
        for a in range(self.n):
            mine, theirs = _half(self.shapes[a], c), _half(self.shapes[a], 1 - c)
            for j, (cx, cy) in enumerate([(1 - x, y), (x, 1 - y), (1 - x, 1 - y)]):
                got = outs[a].at[(2 * cx + cy,) + mine]
                ici.append((remote(6 * a + j, ins[a].at[mine], outs[a].at[(s_me,) + mine], (cx, cy, c)),
                            remote(6 * a + j, got, got, (cx, cy, c))))
                passed = outs[a].at[(2 * cx + cy,) + theirs]
                fwd.append((remote(6 * a + 3 + j, got, got, sibling), remote(6 * a + 3 + j, passed, passed, sibling)))
        return ici, fwd

    def start(self, ins, outs, sems):
        for send, _ in self._copies(ins, outs, sems)[0]:
            send.start()

    def forward(self, ins, outs, sems):
        ici, fwd = self._copies(ins, outs, sems)
        for (_, arrival), (send, _) in zip(ici, fwd):
            arrival.wait_recv()
            send.start()

    def finish(self, ins, outs, sems):
        ici, fwd = self._copies(ins, outs, sems)
        for _, arrival in fwd:
            arrival.wait_recv()
        for send, _ in ici + fwd:
            send.wait_send()


def allgather_shards(ws):
    plan = GatherPlan(ws)
    n = plan.n

    def body(*refs):
        ins, outs, sems = refs[:n], refs[n:2 * n], refs[2 * n:]
        plan.start(ins, outs, sems)
        plan.forward(ins, outs, sems)
        plan.finish(ins, outs, sems)

    return pl.pallas_call(
        body, name="allgather_shards", in_specs=[ANY] * n, out_specs=[ANY] * n,
        out_shape=plan.out_shape, scratch_shapes=plan.scratch,
    )(*ws)


class ScatterPlan:
    def __init__(self, gs):
        self.shapes = [g.shape[1:] for g in gs]
        self.n = len(gs)
        self.out_shape = [jax.ShapeDtypeStruct((N_DEV,) + _half_shape(g.shape[1:]), g.dtype) for g in gs]
        self.scratch = [pltpu.SemaphoreType.DMA((7 * self.n,)), pltpu.SemaphoreType.DMA((7 * self.n,)),
                        pltpu.SemaphoreType.DMA((self.n,))]

    def _copies(self, ins, outs, sems):
        x, y, c = _place()
        me = 4 * x + 2 * y + c
        send_sems, recv_sems, local_sems = sems
        local, sends, arrivals = [], [], []
        for a in range(self.n):
            piece = lambda px, py, pc, a=a: ins[a].at[(2 * px + py,) + _half(self.shapes[a], pc)]
            local.append(pltpu.make_async_copy(piece(x, y, c), outs[a].at[me], local_sems.at[a]))
            for k in range(1, N_DEV):
                px, py, pc = _flip(x, k & 4), _flip(y, k & 2), _flip(c, k & 1)
                sem = dict(send_sem=send_sems.at[7 * a + k - 1], recv_sem=recv_sems.at[7 * a + k - 1],
                           device_id=(px, py, pc), device_id_type=MESH)
                sends.append(pltpu.make_async_remote_copy(
                    src_ref=piece(px, py, pc), dst_ref=outs[a].at[me], **sem))
                slot = outs[a].at[4 * px + 2 * py + pc]
                arrivals.append(pltpu.make_async_remote_copy(src_ref=slot, dst_ref=slot, **sem))
        return local, sends, arrivals

    def start(self, ins, outs, sems):
        local, sends, _ = self._copies(ins, outs, sems)
        for cp in local + sends:
            cp.start()

    def finish(self, ins, outs, sems):
        local, sends, arrivals = self._copies(ins, outs, sems)
        for cp in arrivals:
            cp.wait_recv()
        for cp in sends:
            cp.wait_send()
        for cp in local:
            cp.wait()


def _row_tile(rows, row_bytes, budget, multiple):
    fits = [r for r in range(multiple, rows + 1, multiple) if rows % r == 0 and r * row_bytes <= budget]
    return max(fits) if fits else rows


SMALL_KERNEL_VMEM = 32 * 1024 * 1024


def sum_slots(lands, name):
    k = len(lands)
    _, rows, cols = lands[0].shape
    tr = _row_tile(rows, k * 2 * (N_DEV * cols * 2 + cols * 4), SMALL_KERNEL_VMEM, BF16_ROWS)

    def body(*refs):
        for l_ref, o_ref in zip(refs[:k], refs[k:]):
            acc = l_ref[0].astype(F32)
            for s in range(1, N_DEV):
                acc = acc + l_ref[s].astype(F32)
            o_ref[...] = acc

    return pl.pallas_call(
        body, name=name, grid=(rows // tr,),
        in_specs=[pl.BlockSpec((N_DEV, tr, cols), lambda i: (0, i, 0))] * k,
        out_specs=[pl.BlockSpec((tr, cols), lambda i: (i, 0))] * k,
        out_shape=[jax.ShapeDtypeStruct((rows, cols), F32)] * k,
        compiler_params=_params(("parallel",)),
    )(*lands)


def join_halves(hs, shapes):
    n = len(hs)

    def body(*refs):
        ins, outs = refs[:n], refs[n:2 * n]
        send_sems, recv_sems = refs[2 * n:]
        x, y, c = _place()
        sends = []
        for a in range(n):
            mine = outs[a].at[_half(shapes[a], c)]
            sends.append(pltpu.make_async_remote_copy(
                src_ref=ins[a], dst_ref=mine, send_sem=send_sems.at[a], recv_sem=recv_sems.at[a],
                device_id=(x, y, 1 - c), device_id_type=MESH))
            sends[-1].start()
        for a in range(n):
            theirs = outs[a].at[_half(shapes[a], 1 - c)]
            pltpu.make_async_remote_copy(
                src_ref=theirs, dst_ref=theirs, send_sem=send_sems.at[a], recv_sem=recv_sems.at[a],
                device_id=(x, y, 1 - c), device_id_type=MESH).wait_recv()
        for cp in sends:
            cp.wait_send()

    return pl.pallas_call(
        body, name="join_halves",
        in_specs=[ANY] * n, out_specs=[ANY] * n,
        out_shape=[jax.ShapeDtypeStruct(tuple(s), h.dtype) for s, h in zip(shapes, hs)],
        scratch_shapes=[pltpu.SemaphoreType.DMA((n,)), pltpu.SemaphoreType.DMA((n,))],
    )(*hs)


def allreduce_small(vec):
    def body(v_ref, o_ref, land_ref, send_sems, recv_sems):
        x, y, c = _place()
        me = 4 * x + 2 * y + c
        land_ref[me] = v_ref[...]
        sends = []
        for k in range(1, N_DEV):
            px, py, pc = _flip(x, k & 4), _flip(y, k & 2), _flip(c, k & 1)
            sends.append(pltpu.make_async_remote_copy(
                src_ref=v_ref, dst_ref=land_ref.at[me], send_sem=send_sems.at[k - 1], recv_sem=recv_sems.at[k - 1],
                device_id=(px, py, pc), device_id_type=MESH))
            sends[-1].start()
        for k in range(1, N_DEV):
            px, py, pc = _flip(x, k & 4), _flip(y, k & 2), _flip(c, k & 1)
            slot = land_ref.at[4 * px + 2 * py + pc]
            pltpu.make_async_remote_copy(
                src_ref=slot, dst_ref=slot, send_sem=send_sems.at[k - 1], recv_sem=recv_sems.at[k - 1],
                device_id=(px, py, pc), device_id_type=MESH).wait_recv()
        for cp in sends:
            cp.wait_send()
        acc = land_ref[0]
        for s in range(1, N_DEV):
            acc = acc + land_ref[s]
        o_ref[...] = acc

    vm = pl.BlockSpec(memory_space=pltpu.VMEM)
    return pl.pallas_call(
        body, name="allreduce_small",
        in_specs=[vm], out_specs=vm,
        out_shape=jax.ShapeDtypeStruct(vec.shape, F32),
        scratch_shapes=[pltpu.VMEM((N_DEV,) + vec.shape, F32), pltpu.SemaphoreType.DMA((N_DEV - 1,)),
                        pltpu.SemaphoreType.DMA((N_DEV - 1,))],
    )(vec)


def adamw(ws, gs, ms, vs, name):
    k = len(ws)
    rows, cols = ws[0].shape
    tr = _row_tile(rows, k * 2 * 7 * cols * 4, SMALL_KERNEL_VMEM, 8)
    c1 = 1.0 - ADAM_B1 ** ADAM_STEP
    c2 = 1.0 - ADAM_B2 ** ADAM_STEP

    def body(*refs):
        for a in range(k):
            w_ref, g_ref, m_ref, v_ref = (refs[s * k + a] for s in range(4))
            d_ref, nm_ref, nv_ref = (refs[(4 + s) * k + a] for s in range(3))
            gv = g_ref[...]
            nm = ADAM_B1 * m_ref[...] + (1.0 - ADAM_B1) * gv
            nv = ADAM_B2 * v_ref[...] + (1.0 - ADAM_B2) * (gv * gv)
            nm_ref[...] = nm
            nv_ref[...] = nv
            d_ref[...] = -ADAM_LR * ((nm / c1) / (jnp.sqrt(nv / c2) + ADAM_EPS) + ADAM_WD * w_ref[...])

    blk = pl.BlockSpec((tr, cols), lambda i: (i, 0))
    out = pl.pallas_call(
        body, name=name, grid=(rows // tr,),
        in_specs=[blk] * (4 * k), out_specs=[blk] * (3 * k),
        out_shape=[jax.ShapeDtypeStruct((rows, cols), F32)] * (3 * k),
        compiler_params=_params(("parallel",)),
    )(*ws, *gs, *ms, *vs)
    return [(out[a], out[k + a], out[2 * k + a]) for a in range(k)]


BIG = ("ffn1_w_gate", "ffn1_w_up", "ffn1_w_down", "w_in", "b_w_uq", "b_w_ukv", "w_out",
       "ffn2_w_gate", "ffn2_w_up", "ffn2_w_down")
SMALL = ("ffn1_norm", "mix_norm", "a_q_norm", "a_k_norm", "a_rel_bias", "b_q_lat_norm", "b_kv_lat_norm",
         "b_q_nope_norm", "b_q_rope_norm", "b_k_nope_norm", "b_k_rope_norm", "ffn2_norm", "final_norm")
WEIGHTS = ("ffn1_norm", "ffn1_w_gate", "ffn1_w_up", "ffn1_w_down", "mix_norm", "w_in", "a_q_norm", "a_k_norm",
           "a_rel_bias", "b_q_lat_norm", "b_w_uq", "b_kv_lat_norm", "b_w_ukv", "b_q_nope_norm", "b_q_rope_norm",
           "b_k_nope_norm", "b_k_rope_norm", "w_out", "ffn2_norm", "ffn2_w_gate", "ffn2_w_up", "ffn2_w_down",
           "final_norm")
TRANSPOSED = ("ffn1_w_gate", "ffn1_w_up", "ffn2_w_gate", "ffn2_w_up", "w_in", "b_w_uq")
PACK_SHAPE = (8, 1024)


def _pack_small(d, last=None):
    flat = [d[n].reshape(-1) for n in SMALL]
    used = sum(f.shape[0] for f in flat)
    total = PACK_SHAPE[0] * PACK_SHAPE[1]
    tail = jnp.zeros((total - used - 1,), F32)
    end = jnp.zeros((1,), F32) if last is None else last.reshape(1)
    return jnp.concatenate(flat + [tail, end]).reshape(PACK_SHAPE)


def _unpack_small(p, like):
    flat = p.reshape(-1)
    out, off = {}, 0
    for n in SMALL:
        size = like[n].size
        out[n] = flat[off:off + size].reshape(like[n].shape)
        off += size
    return out, flat[-1]


def _cols_to_shards(g):
    rows, cols = g.shape
    return g.reshape(rows, N_SHARD, cols // N_SHARD).transpose(1, 0, 2)


def _shards_to_cols(g):
    return g.transpose(1, 0, 2).reshape(g.shape[1], -1)


def kernel(x, ffn1_norm, ffn1_w_gate, ffn1_w_up, ffn1_w_down, mix_norm, w_in, a_q_norm, a_k_norm, a_rel_bias, b_q_lat_norm, b_w_uq, b_kv_lat_norm, b_w_ukv, b_q_nope_norm, b_q_rope_norm, b_k_nope_norm, b_k_rope_norm, w_out, ffn2_norm, ffn2_w_gate, ffn2_w_up, ffn2_w_down, final_norm, loss_target, m_ffn1_norm, m_ffn1_w_gate, m_ffn1_w_up, m_ffn1_w_down, m_mix_norm, m_w_in, m_a_q_norm, m_a_k_norm, m_a_rel_bias, m_b_q_lat_norm, m_b_w_uq, m_b_kv_lat_norm, m_b_w_ukv, m_b_q_nope_norm, m_b_q_rope_norm, m_b_k_nope_norm, m_b_k_rope_norm, m_w_out, m_ffn2_norm, m_ffn2_w_gate, m_ffn2_w_up, m_ffn2_w_down, m_final_norm, v_ffn1_norm, v_ffn1_w_gate, v_ffn1_w_up, v_ffn1_w_down, v_mix_norm, v_w_in, v_a_q_norm, v_a_k_norm, v_a_rel_bias, v_b_q_lat_norm, v_b_w_uq, v_b_kv_lat_norm, v_b_w_ukv, v_b_q_nope_norm, v_b_q_rope_norm, v_b_k_nope_norm, v_b_k_rope_norm, v_w_out, v_ffn2_norm, v_ffn2_w_gate, v_ffn2_w_up, v_ffn2_w_down, v_final_norm):
    args = locals()
    view = lambda a, n: a[0].T if n in TRANSPOSED else a[0]
    wts = {n: view(args[n], n) for n in WEIGHTS}
    mom = {n: view(args["m_" + n], n) for n in WEIGHTS}
    var = {n: view(args["v_" + n], n) for n in WEIGHTS}

    shard = 2 * lax.axis_index("x") + lax.axis_index("y")
    core = lax.axis_index("c")
    first = [n for n in BIG if n not in LATE]
    own = [wts[n].astype(BF16) for n in first]
    w = {n: wts[n] if n == "a_rel_bias" else wts[n][None] for n in SMALL}
    w.update(kernel_layout({n: lax.dynamic_update_index_in_dim(got, mine, shard, 0)
                            for n, got, mine in zip(first, allgather_shards(own), own)}))

    loss, grad_x, g, landed = local_step(x[0], loss_target[0], w,
                                         late=([wts[n].astype(BF16) for n in LATE], shard))

    me = 2 * shard + core
    for n in first:
        piece = lax.dynamic_slice(g[n], (shard, core * (FS // 2), 0), (1, FS // 2, D_MODEL))
        landed[n] = lax.dynamic_update_slice(landed[n], piece, (me, 0, 0))
    groups = {}
    for n in BIG:
        groups.setdefault(wts[n].shape, []).append(n)
    half = {}
    for names in groups.values():
        half.update(zip(names, sum_slots([landed[n] for n in names], "sum_slots")))
    halves = [half[n] for n in BIG]
    shapes = [wts[n].shape for n in BIG]
    axes = [_split_axis(s) for s in shapes]
    grads = dict(zip(BIG, (lax.dynamic_update_slice_in_dim(got, mine, core * mine.shape[ax], ax)
                           for got, mine, ax in zip(join_halves(halves, shapes), halves, axes))))

    small_sum, loss_sum = _unpack_small(allreduce_small(_pack_small(g, loss[0, 0])), wts)
    grads.update(small_sum)

    delta, new_m, new_v = {}, {}, {}
    for names in groups.values():
        stepped = adamw(*([d[n] for n in names] for d in (wts, grads, mom, var)), "adamw")
        for n, (d_new, m_new, v_new) in zip(names, stepped):
            delta[n], new_m[n], new_v[n] = d_new, m_new, v_new
    (packed,) = adamw([_pack_small(wts)], [_pack_small(grads)], [_pack_small(mom)], [_pack_small(var)], "adamw_small")
    for dst, p in zip((delta, new_m, new_v), packed):
        dst.update(_unpack_small(p, wts)[0])

    lead = lambda d: [(d[n].T if n in TRANSPOSED else d[n])[None] for n in WEIGHTS]
    return (loss_sum, grad_x[None], *lead(grads), *lead(delta), *lead(new_m), *lead(new_v))
```

```python
import numpy as np
import jax
import jax.numpy as jnp
from jax import lax
from jax.experimental import pallas as pl
from jax.experimental.pallas import tpu as pltpu

F32 = jnp.float32
BF16 = jnp.bfloat16
EPS = 1e-6
NEG = -1e30

D_MODEL = 1024
D_FF = 2816
N_SHARD = 4
FS = D_FF // N_SHARD
CHUNK = 64
A_LEFT = 8
A_MAX_REL = 128
HEADS = 8
HD = 64
ROPE = 32
PROJ_W = 2048
IN_COLS = 1952
B_SCALE = 96 ** -0.5
LANES = 128

ADAM_LR = 0.001
ADAM_B1 = 0.9
ADAM_B2 = 0.999
ADAM_EPS = 1e-08
ADAM_WD = 0.01
ADAM_STEP = 10

VMEM_LIMIT = 56 * 1024 * 1024
TOKEN_TILE = 512

MESH = pl.DeviceIdType.MESH


def _dot(a, b):
    return lax.dot_general(a, b, (((1,), (0,)), ((), ())), preferred_element_type=F32)


def _dot_nt(a, b):
    return lax.dot_general(a, b, (((1,), (1,)), ((), ())), preferred_element_type=F32)


def _dot_tn(a, b):
    return lax.dot_general(a, b, (((0,), (0,)), ((), ())), preferred_element_type=F32)


def _params(sem):
    return pltpu.CompilerParams(dimension_semantics=sem, vmem_limit_bytes=VMEM_LIMIT)


def _rms(xv):
    r = lax.rsqrt(jnp.mean(xv * xv, axis=-1, keepdims=True) + EPS)
    return r, xv * r


def ffn_fwd(x, g, wg, wu, wd, name, gather=(), loss_head=None, pre_proj=None):
    t, d = x.shape
    tm = TOKEN_TILE
    ni = t // tm
    plan = GatherPlan(gather)
    n = plan.n
    head = () if loss_head is None else tuple(loss_head)
    q = len(head)
    proj = () if pre_proj is None else tuple(pre_proj)
    r = len(proj)
    half = proj[0].shape[1] if r else 0

    def body(*refs):
        it = iter(refs)
        take = lambda count: [next(it) for _ in range(count)]
        x_ref, g_ref, wg_ref, wu_ref, wd_ref = take(5)
        head_in, proj_in, ins = take(q), take(r), take(n)
        o_ref, gp_ref, up_ref = take(3)
        head_out, x_out, outs = take(q), take(1 if r else 0), take(n)
        h_ref, acc_ref = take(2)
        sems = list(it)
        i, j = pl.program_id(0), pl.program_id(1)
        if n:
            pl.when((i == 0) & (j == 0))(lambda: plan.start(ins, outs, sems))
            pl.when((i == (3 * ni) // 4) & (j == 0))(lambda: plan.forward(ins, outs, sems))

        @pl.when(j == 0)
        def _():
            xin = x_ref[...]
            if r:
                oa_ref, obt_ref, wo_ref = proj_in
                xin = xin + _dot(oa_ref[...], wo_ref[0:half, :]) + _dot_tn(obt_ref[...], wo_ref[half:2 * half, :])
                x_out[0][...] = xin
            _, xn = _rms(xin)
            h_ref[...] = (xn * g_ref[...]).astype(BF16)
            acc_ref[...] = jnp.zeros_like(acc_ref)

        h = h_ref[...]
        gp = _dot_nt(h, wg_ref[0])
        up = _dot_nt(h, wu_ref[0])
        gp_ref[0] = gp.astype(BF16)
        up_ref[0] = up.astype(BF16)
        a = (gp * jax.nn.sigmoid(gp) * up).astype(BF16)
        acc_ref[...] += _dot(a, wd_ref[0])

        @pl.when(j == N_SHARD - 1)
        def _():
            y = (x_out[0][...] if r else x_ref[...]) + 0.5 * acc_ref[...]
            if not q:
                o_ref[...] = y
                return
            (gf_ref, t_ref), (dgf_ref, loss_ref) = head_in, head_out
            rinv, yn = _rms(y)
            gf = gf_ref[...]
            e = yn * gf - t_ref[...]
            dout = e * (1.0 / d)
            dng = dout * gf
            o_ref[...] = rinv * (dng - yn * jnp.mean(dng * yn, axis=-1, keepdims=True))

            @pl.when(i == 0)
            def _():
                dgf_ref[...] = jnp.zeros_like(dgf_ref)
                loss_ref[...] = jnp.zeros_like(loss_ref)

            dgf_ref[...] += jnp.sum(dout * yn, axis=0, keepdims=True)
            part = jnp.sum(jnp.sum(e * e, axis=-1, keepdims=True), axis=0, keepdims=True) * (0.5 / d)
            loss_ref[...] += jnp.broadcast_to(part, loss_ref.shape)

        if n:
            pl.when((i == ni - 1) & (j == N_SHARD - 1))(lambda: plan.finish(ins, outs, sems))

    tok = pl.BlockSpec((tm, d), lambda i, j: (i, 0))
    vec = pl.BlockSpec((1, d), lambda i, j: (0, 0))
    chunk = pl.BlockSpec((1, FS, d), lambda i, j: (j, 0, 0))
    pre = pl.BlockSpec((1, tm, FS), lambda i, j: (j, i, 0))
    return pl.pallas_call(
        body, name=name, grid=(ni, N_SHARD),
        in_specs=[tok, vec, chunk, chunk, chunk] + [vec, tok][:q]
        + [pl.BlockSpec((tm, half), lambda i, j: (i, 0)), pl.BlockSpec((half, tm), lambda i, j: (0, i)),
           pl.BlockSpec((2 * half, d), lambda i, j: (0, 0))][:r] + [ANY] * n,
        out_specs=[tok, pre, pre] + [vec, pl.BlockSpec((1, LANES), lambda i, j: (0, 0))][:q] + [tok][:r] + [ANY] * n,
        out_shape=[jax.ShapeDtypeStruct((t, d), F32), jax.ShapeDtypeStruct((N_SHARD, t, FS), BF16),
                   jax.ShapeDtypeStruct((N_SHARD, t, FS), BF16)]
        + [jax.ShapeDtypeStruct((1, d), F32), jax.ShapeDtypeStruct((1, LANES), F32)][:q]
        + [jax.ShapeDtypeStruct((t, d), F32)][:r] + plan.out_shape,
        scratch_shapes=[pltpu.VMEM((tm, d), BF16), pltpu.VMEM((tm, d), F32)] + (plan.scratch if n else []),
        compiler_params=_params(("arbitrary", "arbitrary")),
    )(x, g, wg, wu, wd, *head, *proj, *gather)


def ffn_bwd(x, dout, g, wg, wu, wd, gate, up_pre, name, scatter=(), spread=None):
    t, d = x.shape
    tm = TOKEN_TILE
    ni = t // tm
    hf = FS // 2
    plan = ScatterPlan(scatter)
    m = plan.n
    k = 0 if spread is None else 3
    steps = jnp.arange(N_SHARD, dtype=jnp.int32)
    order = steps if spread is None else (spread + 1 + steps) % N_SHARD

    def body(*refs):
        ord_ref, x_ref, do_ref, g_ref, wg_ref, wu_ref, wd_ref, gp_ref, up_ref = refs[:9]
        ins, (dwg_out, dwu_out, dwd_out, dhp_ref) = refs[9:9 + m], refs[9 + m:13 + m]
        outs, lands = refs[13 + m:13 + 2 * m], refs[13 + 2 * m:13 + 2 * m + k]
        dwg_ref, dwu_ref, dwd_ref = refs[13 + 2 * m + k:16 + 2 * m + k]
        sems = refs[16 + 2 * m + k:19 + 2 * m + k] if m else ()
        stage_ref = refs[-3] if k else None
        j, i = pl.program_id(0), pl.program_id(1)
        if m:
            pl.when((j == 0) & (i == 0))(lambda: plan.start(ins, outs, sems))

        def chunk_copies(jj):
            send_sems, recv_sems = refs[-2:]
            px, py, pc = _place()
            me = 4 * px + 2 * py + pc
            tx, ty = ord_ref[jj] // 2, ord_ref[jj] % 2
            copies = []
            for n_ in range(3):
                for h_ in range(2):
                    copies.append((pltpu.make_async_remote_copy(
                        src_ref=stage_ref.at[n_, pl.ds(h_ * hf, hf)], dst_ref=lands[n_].at[me],
                        send_sem=send_sems.at[6 * jj + 2 * n_ + h_], recv_sem=recv_sems.at[3 * me + n_],
                        device_id=(tx, ty, h_), device_id_type=MESH), (tx != px) | (ty != py) | (pc != h_)))
            return copies

        def arrivals():
            send_sems, recv_sems = refs[-2:]
            px, py, pc = _place()
            me = 4 * px + 2 * py + pc
            for s_ in range(N_DEV):
                for n_ in range(3):
                    slot = lands[n_].at[s_]
                    cp = pltpu.make_async_remote_copy(
                        src_ref=slot, dst_ref=slot, send_sem=send_sems.at[0], recv_sem=recv_sems.at[3 * s_ + n_],
                        device_id=(px, py, pc), device_id_type=MESH)
                    pl.when(me != s_)(cp.wait_recv)

        _, xn = _rms(x_ref[...])
        h = (xn * g_ref[...]).astype(BF16)
        dz = (0.5 * do_ref[...]).astype(BF16)
        wgv, wuv, wdv = wg_ref[0], wu_ref[0], wd_ref[0]
        gp, up = gp_ref[0].astype(F32), up_ref[0].astype(F32)
        s = jax.nn.sigmoid(gp)
        sg = gp * s
        a = (sg * up).astype(BF16)
        da = _dot_nt(dz, wdv)
        dup = (da * sg).astype(BF16)
        dgp = (da * up * (s * (1.0 + gp * (1.0 - s)))).astype(BF16)

        @pl.when(i == 0)
        def _():
            dwg_ref[...] = jnp.zeros_like(dwg_ref)
            dwu_ref[...] = jnp.zeros_like(dwu_ref)
            dwd_ref[...] = jnp.zeros_like(dwd_ref)

        dwd_ref[...] += _dot_tn(a, dz)
        dwg_ref[...] += _dot_tn(dgp, h)
        dwu_ref[...] += _dot_tn(dup, h)
        dhp_ref[0] = (_dot(dgp, wgv) + _dot(dup, wuv)).astype(BF16)

        @pl.when(i == ni - 1)
        def _():
            dwg_out[0] = dwg_ref[...].astype(BF16)
            dwu_out[0] = dwu_ref[...].astype(BF16)
            dwd_out[0] = dwd_ref[...].astype(BF16)
            if k:
                @pl.when(j >= 1)
                def _():
                    for cp, leaves in chunk_copies(j - 1):
                        pl.when(leaves)(cp.wait_send)
                for n_, acc in enumerate((dwg_ref, dwu_ref, dwd_ref)):
                    stage_ref[n_] = acc[...].astype(BF16)
                for cp, leaves in chunk_copies(j):
                    pl.when(leaves)(cp.start)

                @pl.when(j == N_SHARD - 1)
                def _():
                    for cp, leaves in chunk_copies(N_SHARD - 1):
                        pl.when(leaves)(cp.wait_send)
                    arrivals()

        if m:
            pl.when((j == N_SHARD - 1) & (i == ni - 1))(lambda: plan.finish(ins, outs, sems))

    chunk = pl.BlockSpec((1, FS, d), lambda j, i, o: (o[j], 0, 0))
    tok = pl.BlockSpec((tm, d), lambda j, i, o: (i, 0))
    pre = pl.BlockSpec((1, tm, FS), lambda j, i, o: (o[j], i, 0))
    grid_spec = pltpu.PrefetchScalarGridSpec(
        num_scalar_prefetch=1, grid=(N_SHARD, ni),
        in_specs=[tok, tok, pl.BlockSpec((1, d), lambda j, i, o: (0, 0)), chunk, chunk, chunk, pre, pre] + [ANY] * m,
        out_specs=[chunk, chunk, chunk, pl.BlockSpec((1, tm, d), lambda j, i, o: (o[j], i, 0))] + [ANY] * (m + k),
        scratch_shapes=[pltpu.VMEM((FS, d), F32), pltpu.VMEM((FS, d), F32), pltpu.VMEM((FS, d), F32)]
        + (plan.scratch if m else [])
        + ([pltpu.VMEM((3, FS, d), BF16), pltpu.SemaphoreType.DMA((6 * N_SHARD,)),
            pltpu.SemaphoreType.DMA((3 * N_DEV,))] if k else []))
    return pl.pallas_call(
        body, name=name, grid_spec=grid_spec,
        out_shape=[jax.ShapeDtypeStruct((N_SHARD, FS, d), BF16),
                   jax.ShapeDtypeStruct((N_SHARD, FS, d), BF16),
                   jax.ShapeDtypeStruct((N_SHARD, FS, d), BF16),
                   jax.ShapeDtypeStruct((N_SHARD, t, d), BF16)] + plan.out_shape
        + [jax.ShapeDtypeStruct((N_DEV, hf, d), BF16)] * k,
        compiler_params=_params(("arbitrary", "arbitrary")),
    )(order, x, dout, g, wg, wu, wd, gate, up_pre, *scatter)


def norm_bwd(x, g, dhp, dres, name):
    t, d = x.shape
    p = dhp.shape[0]
    tm = TOKEN_TILE

    def body(x_ref, g_ref, dhp_ref, dres_ref, dx_ref, dg_ref):
        i = pl.program_id(0)
        r, xn = _rms(x_ref[...])
        dh = dhp_ref[0].astype(F32)
        for q in range(1, p):
            dh = dh + dhp_ref[q].astype(F32)
        dhg = dh * g_ref[...]
        dx_ref[...] = dres_ref[...] + r * (dhg - xn * jnp.mean(dhg * xn, axis=-1, keepdims=True))

        @pl.when(i == 0)
        def _():
            dg_ref[...] = jnp.zeros_like(dg_ref)

        dg_ref[...] += jnp.sum(dh * xn, axis=0, keepdims=True)

    return pl.pallas_call(
        body, name=name, grid=(t // tm,),
        in_specs=[pl.BlockSpec((tm, d), lambda i: (i, 0)),
                  pl.BlockSpec((1, d), lambda i: (0, 0)),
                  pl.BlockSpec((p, tm, d), lambda i: (0, i, 0)),
                  pl.BlockSpec((tm, d), lambda i: (i, 0))],
        out_specs=[pl.BlockSpec((tm, d), lambda i: (i, 0)),
                   pl.BlockSpec((1, d), lambda i: (0, 0))],
        out_shape=[jax.ShapeDtypeStruct((t, d), F32), jax.ShapeDtypeStruct((1, d), F32)],
        compiler_params=_params(("arbitrary",)),
    )(x, g, dhp, dres)


def out_proj_bwd(x, g, dhp, dres, w, oa, ob_t):
    t, d = x.shape
    half = w.shape[0] // 2
    p = dhp.shape[0]
    tm = TOKEN_TILE
    ni = t // tm

    def body(x_ref, g_ref, dhp_ref, dres_ref, w_ref, oa_ref, obt_ref,
             dx_ref, dg_ref, da_ref, db_ref, dbt_ref, dw_ref, acc_ref):
        i = pl.program_id(0)

        @pl.when(i == 0)
        def _():
            dg_ref[...] = jnp.zeros_like(dg_ref)
            acc_ref[...] = jnp.zeros_like(acc_ref)

        r, xn = _rms(x_ref[...])
        dh = dhp_ref[0].astype(F32)
        for s in range(1, p):
            dh = dh + dhp_ref[s].astype(F32)
        dhg = dh * g_ref[...]
        dx = dres_ref[...] + r * (dhg - xn * jnp.mean(dhg * xn, axis=-1, keepdims=True))
        dx_ref[...] = dx
        dg_ref[...] += jnp.sum(dh * xn, axis=0, keepdims=True)
        dxb = dx.astype(BF16)
        da_ref[...] = _dot_nt(dxb, w_ref[0:half, :]).astype(BF16)
        db_ref[...] = _dot_nt(dxb, w_ref[half:2 * half, :]).astype(BF16)
        dbt_ref[...] = _dot_nt(w_ref[half:2 * half, :], dxb).astype(BF16)
        acc_ref[0:half, :] += _dot_tn(oa_ref[...], dxb)
        acc_ref[half:2 * half, :] += _dot(obt_ref[...], dxb)

        @pl.when(i == ni - 1)
        def _():
            dw_ref[...] = acc_ref[...].astype(BF16)

    row = lambda w_: pl.BlockSpec((tm, w_), lambda i: (i, 0))
    col = pl.BlockSpec((half, tm), lambda i: (0, i))
    whole = pl.BlockSpec((2 * half, d), lambda i: (0, 0))
    vec = pl.BlockSpec((1, d), lambda i: (0, 0))
    return pl.pallas_call(
        body, name="out_proj_bwd", grid=(ni,),
        in_specs=[row(d), vec, pl.BlockSpec((p, tm, d), lambda i: (0, i, 0)), row(d), whole, row(half), col],
        out_specs=[row(d), vec, row(half), row(half), col, whole],
        out_shape=[jax.ShapeDtypeStruct((t, d), F32), jax.ShapeDtypeStruct((1, d), F32),
                   jax.ShapeDtypeStruct((t, half), BF16), jax.ShapeDtypeStruct((t, half), BF16),
                   jax.ShapeDtypeStruct((half, t), BF16), jax.ShapeDtypeStruct((2 * half, d), BF16)],
        scratch_shapes=[pltpu.VMEM((2 * half, d), F32)],
        compiler_params=_params(("arbitrary",)),
    )(x, g, dhp, dres, w, oa, ob_t)


def _lane(shape):
    return lax.broadcasted_iota(jnp.int32, shape, 1)


PAIR = (0, HD, LANES)
Q_HEAD = (0, HD, HD + ROPE, LANES)
K_ROPE = (0, ROPE, LANES)


def _seg_mean(z, bounds):
    seg = lambda v: sum([(v >= b).astype(jnp.int32) for b in bounds[1:-1]], jnp.zeros_like(v))
    rows = seg(lax.broadcasted_iota(jnp.int32, (LANES, LANES), 0))
    cols = seg(lax.broadcasted_iota(jnp.int32, (LANES, LANES), 1))
    same = (rows == cols).astype(BF16)
    lane = _lane((1, LANES))
    inv = sum([jnp.where((lane >= a) & (lane < b), 1.0 / (b - a), 0.0) for a, b in zip(bounds[:-1], bounds[1:])])
    hi = z.astype(BF16)
    lo = (z - hi.astype(F32)).astype(BF16)
    return (_dot(hi, same) + _dot(lo, same)) * inv


def _seg_norm(x, bounds):
    r = lax.rsqrt(_seg_mean(x * x, bounds) + EPS)
    return r, x * r


def _seg_norm_bwd(r, xn, dyg, bounds):
    return r * (dyg - xn * _seg_mean(dyg * xn, bounds))


A_TM = 256


def mix_fwd(x, g, w, gq, gk, gcq, gckv):
    t, d = x.shape
    tm = TOKEN_TILE

    def body(x_ref, g_ref, w_ref, gq_ref, gk_ref, gcq_ref, gckv_ref,
             h_ref, p_ref, qa_ref, ka_ref, va_ref, cq_ref, ckv_ref):
        _, xn = _rms(x_ref[...])
        h = (xn * g_ref[...]).astype(BF16)
        h_ref[...] = h
        p_ref[...] = _dot_nt(h, w_ref[...])
        for p in range(4):
            sl = slice(LANES * p, LANES * (p + 1))
            _, xn = _seg_norm(p_ref[:, sl], PAIR)
            qa_ref[:, sl] = (xn * gq_ref[:, sl] * 0.125).astype(BF16)
            _, xn = _seg_norm(p_ref[:, 512 + LANES * p:512 + LANES * (p + 1)], PAIR)
            ka_ref[:, sl] = (xn * gk_ref[:, sl]).astype(BF16)
        va_ref[...] = p_ref[:, 1024:1536].astype(BF16)
        _, xn = _rms(p_ref[:, 1536:1792])
        cq_ref[...] = (xn * gcq_ref[...]).astype(BF16)
        _, xn = _rms(p_ref[:, 1792:1920])
        ckv_ref[...] = (xn * gckv_ref[...]).astype(BF16)

    row = lambda w: pl.BlockSpec((tm, w), lambda i: (i, 0))
    vec = lambda w: pl.BlockSpec((1, w), lambda i: (0, 0))
    return pl.pallas_call(
        body, name="mix_fwd", grid=(t // tm,),
        in_specs=[row(d), vec(d), pl.BlockSpec((PROJ_W, d), lambda i: (0, 0)), vec(512), vec(512), vec(256), vec(128)],
        out_specs=[row(d), row(PROJ_W), row(512), row(512), row(512), row(256), row(128)],
        out_shape=[jax.ShapeDtypeStruct((t, d), BF16), jax.ShapeDtypeStruct((t, PROJ_W), F32)]
        + [jax.ShapeDtypeStruct((t, w_), BF16) for w_ in (512, 512, 512, 256, 128)],
        compiler_params=_params(("parallel",)),
    )(x, g, w, gq, gk, gcq, gckv)


def mix_bwd(proj, x, h, dres, w, g, dqa, dkp, dvp, dcq, dckv, dkr, gq, gk, gcq, gckv):
    t, d = x.shape
    tm = A_TM
    nb = t // tm

    def body(p_ref, x_ref, h_ref, dres_ref, w_ref, g_ref, dqa_ref, dk0_ref, dk1_ref, dk2_ref, dv0_ref, dv1_ref,
             dv2_ref, dcq_ref, dckv_ref, dkr_ref, gq_ref, gk_ref, gcq_ref, gckv_ref,
             dx_ref, dw_ref, dg_ref, dgq_ref, dgk_ref, dgcq_ref, dgckv_ref, dp_ref, acc_ref):
        i = pl.program_id(0)

        @pl.when(i == 0)
        def _():
            for ref in (acc_ref, dg_ref, dgq_ref, dgk_ref, dgcq_ref, dgckv_ref):
                ref[...] = jnp.zeros_like(ref)

        has1 = (i + 1 < nb).astype(F32)
        has2 = (i + 2 < nb).astype(F32)
        for p in range(4):
            sl = slice(LANES * p, LANES * (p + 1))
            r, xn = _seg_norm(p_ref[:, sl], PAIR)
            dy = dqa_ref[:, sl] * 0.125
            dp_ref[:, sl] = _seg_norm_bwd(r, xn, dy * gq_ref[:, sl], PAIR).astype(BF16)
            dgq_ref[:, sl] += jnp.sum(dy * xn, axis=0, keepdims=True)
            ks = slice(512 + LANES * p, 512 + LANES * (p + 1))
            r, xn = _seg_norm(p_ref[:, ks], PAIR)
            dy = dk0_ref[0, :, sl] + has1 * dk1_ref[0, :, sl] + has2 * dk2_ref[0, :, sl]
            dp_ref[:, ks] = _seg_norm_bwd(r, xn, dy * gk_ref[:, sl], PAIR).astype(BF16)
            dgk_ref[:, sl] += jnp.sum(dy * xn, axis=0, keepdims=True)
        dp_ref[:, 1024:1536] = (dv0_ref[0] + has1 * dv1_ref[0] + has2 * dv2_ref[0]).astype(BF16)
        for (a, b, dlat_ref, glat_ref, dglat_ref) in ((1536, 1792, dcq_ref, gcq_ref, dgcq_ref),
                                                      (1792, 1920, dckv_ref, gckv_ref, dgckv_ref)):
            r, xn = _rms(p_ref[:, a:b])
            dy = dlat_ref[...]
            dyg = dy * glat_ref[...]
            dp_ref[:, a:b] = (r * (dyg - xn * jnp.mean(dyg * xn, axis=-1, keepdims=True))).astype(BF16)
            dglat_ref[...] += jnp.sum(dy * xn, axis=0, keepdims=True)
        dp_ref[:, 1920:2048] = dkr_ref[...].astype(BF16)

        dproj = dp_ref[...]
        acc_ref[...] += _dot_tn(dproj, h_ref[...])
        dh = _dot(dproj, w_ref[...])
        r, xn = _rms(x_ref[...])
        dhg = dh * g_ref[...]
        dx_ref[...] = dres_ref[...] + r * (dhg - xn * jnp.mean(dhg * xn, axis=-1, keepdims=True))
        dg_ref[...] += jnp.sum(dh * xn, axis=0, keepdims=True)

        @pl.when(i == nb - 1)
        def _():
            dw_ref[...] = acc_ref[...].astype(BF16)

    row = lambda w_: pl.BlockSpec((tm, w_), lambda i: (i, 0))
    vec = lambda w_: pl.BlockSpec((1, w_), lambda i: (0, 0))
    part = lambda s: pl.BlockSpec((1, tm, 512), lambda i: (s, jnp.minimum(i + s, nb - 1), 0))
    whole = pl.BlockSpec((PROJ_W, d), lambda i: (0, 0))
    return pl.pallas_call(
        body, name="mix_bwd", grid=(nb,),
        in_specs=[row(PROJ_W), row(d), row(d), row(d), whole, vec(d), row(512), part(0), part(1), part(2),
                  part(0), part(1), part(2), row(256), row(128), row(128), vec(512), vec(512), vec(256), vec(128)],
        out_specs=[row(d), whole, vec(d), vec(512), vec(512), vec(256), vec(128)],
        out_shape=[jax.ShapeDtypeStruct((t, d), F32), jax.ShapeDtypeStruct((PROJ_W, d), BF16)]
        + [jax.ShapeDtypeStruct((1, w_), F32) for w_ in (d, 512, 512, 256, 128)],
        scratch_shapes=[pltpu.VMEM((tm, PROJ_W), BF16), pltpu.VMEM((PROJ_W, d), F32)],
        compiler_params=_params(("arbitrary",)),
    )(proj, x, h, dres, w, g, dqa, dkp, dkp, dkp, dvp, dvp, dvp, dcq, dckv, dkr, gq, gk, gcq, gckv)


def _roll(x, shift):
    return pltpu.roll(x, shift % LANES, 1)


def _rope(y, c, s1, s2):
    return y * c + _roll(y, -16) * s1 + _roll(y, 16) * s2


def _rope_bwd(d, c, s1, s2):
    return d * c + _roll(d * s1, 16) + _roll(d * s2, -16)


def _up_proj(cq_ref, ckv_ref, wuq_ref, wukv_ref):
    return _dot_nt(cq_ref[...], wuq_ref[...]), _dot(ckv_ref[...], wukv_ref[...])


def prep2_fwd(cqn, ckvn, proj, wuq, wukv, gq, gk, gkr, tabs):
    t = cqn.shape[0]
    tm = TOKEN_TILE

    def body(cq_ref, ckv_ref, kr_ref, wuq_ref, wukv_ref, gq_ref, gk_ref, gkr_ref, tab_ref, qf_ref, kf_ref, vp_ref):
        q_all, kv_all = _up_proj(cq_ref, ckv_ref, wuq_ref, wukv_ref)
        _, xn = _seg_norm(kr_ref[...], K_ROPE)
        kpe = _roll(_rope(xn * gkr_ref[...], tab_ref[3], tab_ref[4], tab_ref[5]), 64)
        for h in range(HEADS):
            sl = slice(LANES * h, LANES * (h + 1))
            _, xn = _seg_norm(q_all[:, sl], Q_HEAD)
            qf_ref[:, sl] = (_rope(xn * gq_ref[...], tab_ref[0], tab_ref[1], tab_ref[2]) * B_SCALE2).astype(BF16)
            x = kv_all[:, sl]
            lo = _lane(x.shape) < HD
            _, xkn = _seg_norm(jnp.where(lo, x, 0.0), PAIR)
            kf_ref[:, sl] = (xkn * gk_ref[...] + kpe).astype(BF16)
            if h % 2 == 0:
                v_even = _roll(x, 64)
            else:
                vp_ref[:, LANES * (h // 2):LANES * (h // 2 + 1)] = jnp.where(lo, v_even, x).astype(BF16)

    row = lambda w: pl.BlockSpec((tm, w), lambda i: (i, 0))
    vec = lambda w: pl.BlockSpec((1, w), lambda i: (0, 0))
    full = lambda a: pl.BlockSpec(a.shape, lambda i: (0, 0))
    return pl.pallas_call(
        body, name="prep2_fwd", grid=(t // tm,),
        in_specs=[row(256), row(128), pl.BlockSpec((tm, LANES), lambda i: (i, 15)), full(wuq), full(wukv),
                  vec(128), vec(128), vec(128), pl.BlockSpec((6, tm, LANES), lambda i: (0, i, 0))],
        out_specs=[row(1024), row(1024), row(512)],
        out_shape=[jax.ShapeDtypeStruct((t, 1024), BF16), jax.ShapeDtypeStruct((t, 1024), BF16),
                   jax.ShapeDtypeStruct((t, 512), BF16)],
        compiler_params=_params(("parallel",)),
    )(cqn, ckvn, proj, wuq, wukv, gq, gk, gkr, tabs)


def prep2_bwd(cqn, ckvn, proj, wuq, wukv, dqf, dkf, dvp, gq, gk, gkr, tabs):
    t = cqn.shape[0]
    tm = TOKEN_TILE

    def body(cq_ref, ckv_ref, kr_ref, wuq_ref, wukv_ref, dqf_ref, dkf_ref, dvp_ref, gq_ref, gk_ref, gkr_ref, tab_ref,
             dcq_ref, dckv_ref, dkr_ref, dwuq_ref, dwukv_ref, dgq_ref, dgk_ref, dgkr_ref, dq_ref, dkv_ref):
        i = pl.program_id(0)

        @pl.when(i == 0)
        def _():
            for ref in (dwuq_ref, dwukv_ref, dgq_ref, dgk_ref, dgkr_ref):
                ref[...] = jnp.zeros_like(ref)

        q_all, kv_all = _up_proj(cq_ref, ckv_ref, wuq_ref, wukv_ref)
        dgq = jnp.zeros((1, LANES), F32)
        dgk = jnp.zeros((1, LANES), F32)
        dkpe = jnp.zeros((tm, LANES), F32)
        for h in range(HEADS):
            sl = slice(LANES * h, LANES * (h + 1))
            lane = _lane((tm, LANES))
            mn, mr = lane < HD, (lane >= HD) & (lane < HD + ROPE)
            r, xn = _seg_norm(q_all[:, sl], Q_HEAD)
            dy = _rope_bwd(dqf_ref[sl, :].T, tab_ref[0], tab_ref[1], tab_ref[2])
            dyg = dy * gq_ref[...]
            dq_ref[:, sl] = _seg_norm_bwd(r, xn, dyg, Q_HEAD).astype(BF16)
            dgq = dgq + jnp.sum(dy * xn, axis=0, keepdims=True)

            x = kv_all[:, sl]
            dk = dkf_ref[:, sl]
            rk, xkn = _seg_norm(jnp.where(mn, x, 0.0), PAIR)
            dyk = jnp.where(mn, dk, 0.0)
            dxk = _seg_norm_bwd(rk, xkn, dyk * gk_ref[...], PAIR)
            dgk = dgk + jnp.sum(dyk * xkn, axis=0, keepdims=True)
            dkpe = dkpe + jnp.where(mr, dk, 0.0)
            dvpair = dvp_ref[:, LANES * (h // 2):LANES * (h // 2 + 1)]
            dv = _roll(dvpair, 64) if h % 2 == 0 else dvpair
            dkv_ref[:, sl] = jnp.where(mn, dxk, dv).astype(BF16)

        r, xn = _seg_norm(kr_ref[...], K_ROPE)
        dy = _rope_bwd(_roll(dkpe, 64), tab_ref[3], tab_ref[4], tab_ref[5])
        dkr_ref[...] = _seg_norm_bwd(r, xn, dy * gkr_ref[...], K_ROPE)
        dgq_ref[...] += dgq
        dgk_ref[...] += dgk
        dgkr_ref[...] += jnp.sum(dy * xn, axis=0, keepdims=True)
        dqb, dkvb = dq_ref[...], dkv_ref[...]
        dcq_ref[...] = _dot(dqb, wuq_ref[...])
        dckv_ref[...] = _dot_nt(dkvb, wukv_ref[...])
        dwuq_ref[...] += _dot_tn(dqb, cq_ref[...])
        dwukv_ref[...] += _dot_tn(ckv_ref[...], dkvb)

    row = lambda w: pl.BlockSpec((tm, w), lambda i: (i, 0))
    vec = lambda w: pl.BlockSpec((1, w), lambda i: (0, 0))
    full = lambda a: pl.BlockSpec(a.shape, lambda i: (0, 0))
    return pl.pallas_call(
        body, name="prep2_bwd", grid=(t // tm,),
        in_specs=[row(256), row(128), pl.BlockSpec((tm, LANES), lambda i: (i, 15)), full(wuq), full(wukv),
                  pl.BlockSpec((1024, tm), lambda i: (0, i)), row(1024), row(512),
                  vec(128), vec(128), vec(128), pl.BlockSpec((6, tm, LANES), lambda i: (0, i, 0))],
        out_specs=[row(256), row(128), row(128), full(wuq), full(wukv), vec(128), vec(128), vec(128)],
        out_shape=[jax.ShapeDtypeStruct((t, 256), F32), jax.ShapeDtypeStruct((t, 128), F32),
                   jax.ShapeDtypeStruct((t, LANES), F32), jax.ShapeDtypeStruct(wuq.shape, F32),
                   jax.ShapeDtypeStruct(wukv.shape, F32)] + [jax.ShapeDtypeStruct((1, LANES), F32)] * 3,
        scratch_shapes=[pltpu.VMEM((tm, 1024), BF16), pltpu.VMEM((tm, 1024), BF16)],
        compiler_params=_params(("arbitrary",)),
    )(cqn, ckvn, proj, wuq, wukv, dqf, dkf, dvp, gq, gk, gkr, tabs)


A_TQ = 256
A_WIN = 3 * A_TQ


def _a_specs(t):
    nb = t // A_TQ
    blk = lambda s: pl.BlockSpec((A_TQ, 512), lambda i: (jnp.maximum(i - s, 0), 0))
    return nb, blk


def _a_exp(q_ref, kc, b_ref, head, sl, lo):
    hm = lo if head % 2 == 0 else ~lo
    qm = jnp.where(hm, q_ref[:, sl], jnp.zeros((), BF16))
    s = _dot_nt(qm, kc) + b_ref[0, head]
    e = jnp.exp(s - jnp.max(s, axis=-1, keepdims=True))
    return hm, qm, e, 1.0 / jnp.sum(e, axis=-1, keepdims=True)


def _a_bias_spec():
    return pl.BlockSpec((1, HEADS, A_TQ, A_WIN), lambda i: (jnp.minimum(i, 2), 0, 0, 0))


def attn_a_fwd(qa, ka, va, bias):
    t = qa.shape[0]
    nb, blk = _a_specs(t)

    def body(q_ref, k2_ref, k1_ref, k0_ref, v2_ref, v1_ref, v0_ref, b_ref, o_ref):
        lo = _lane((A_TQ, LANES)) < HD
        for p in range(4):
            sl = slice(LANES * p, LANES * (p + 1))
            kc = jnp.concatenate([k2_ref[:, sl], k1_ref[:, sl], k0_ref[:, sl]], axis=0)
            vc = jnp.concatenate([v2_ref[:, sl], v1_ref[:, sl], v0_ref[:, sl]], axis=0)
            outs = []
            for h2 in range(2):
                _, _, e, inv = _a_exp(q_ref, kc, b_ref, 2 * p + h2, sl, lo)
                outs.append(_dot(e.astype(BF16), vc) * inv)
            o_ref[:, sl] = jnp.where(lo, outs[0], outs[1]).astype(BF16)

    return pl.pallas_call(
        body, name="attn_a_fwd", grid=(nb,),
        in_specs=[blk(0), blk(2), blk(1), blk(0), blk(2), blk(1), blk(0), _a_bias_spec()],
        out_specs=pl.BlockSpec((A_TQ, 512), lambda i: (i, 0)),
        out_shape=jax.ShapeDtypeStruct((t, 512), BF16),
        compiler_params=_params(("parallel",)),
    )(qa, ka, ka, ka, va, va, va, bias)


def attn_a_bwd(qa, ka, va, bias, do):
    t = qa.shape[0]
    nb, blk = _a_specs(t)

    def body(q_ref, k2_ref, k1_ref, k0_ref, v2_ref, v1_ref, v0_ref, b_ref, do_ref, dq_ref, dk_ref, dv_ref, db_ref):
        qb = pl.program_id(0)

        @pl.when(qb == 0)
        def _():
            db_ref[...] = jnp.zeros_like(db_ref)

        lo = _lane((A_TQ, LANES)) < HD
        for p in range(4):
            sl = slice(LANES * p, LANES * (p + 1))
            kc = jnp.concatenate([k2_ref[:, sl], k1_ref[:, sl], k0_ref[:, sl]], axis=0)
            vc = jnp.concatenate([v2_ref[:, sl], v1_ref[:, sl], v0_ref[:, sl]], axis=0)
            dqs = []
            dkt = jnp.zeros((LANES, A_WIN), F32)
            dvt = jnp.zeros((LANES, A_WIN), F32)
            for h2 in range(2):
                head = 2 * p + h2
                hm, qm, e, inv = _a_exp(q_ref, kc, b_ref, head, sl, lo)
                pr = e * inv
                dom = jnp.where(hm, do_ref[:, sl], jnp.zeros((), BF16))
                dp = _dot_nt(dom, vc)
                ds = pr * (dp - jnp.sum(pr * dp, axis=-1, keepdims=True))
                db_ref[head] += ds
                dsb = ds.astype(BF16)
                dqs.append(_dot(dsb, kc))
                dkt = dkt + _dot_tn(qm, dsb)
                dvt = dvt + _dot_tn(dom, pr.astype(BF16))
            dq_ref[:, sl] = jnp.where(lo, dqs[0], dqs[1])
            dkc, dvc = dkt.T, dvt.T
            for s in range(3):
                rows = slice(A_TQ * (2 - s), A_TQ * (3 - s))
                dk_ref[s, :, sl] = dkc[rows]
                dv_ref[s, :, sl] = dvc[rows]

    share = pl.BlockSpec((3, A_TQ, 512), lambda i: (0, i, 0))
    return pl.pallas_call(
        body, name="attn_a_bwd", grid=(nb,),
        in_specs=[blk(0), blk(2), blk(1), blk(0), blk(2), blk(1), blk(0), _a_bias_spec(), blk(0)],
        out_specs=[pl.BlockSpec((A_TQ, 512), lambda i: (i, 0)), share, share,
                   pl.BlockSpec((HEADS, A_TQ, A_WIN), lambda i: (0, 0, 0))],
        out_shape=[jax.ShapeDtypeStruct((t, 512), F32), jax.ShapeDtypeStruct((3, t, 512), F32),
                   jax.ShapeDtypeStruct((3, t, 512), F32), jax.ShapeDtypeStruct((HEADS, A_TQ, A_WIN), F32)],
        compiler_params=_params(("arbitrary",)),
    )(qa, ka, ka, ka, va, va, va, bias, do)


B_T = 1024


B_SCALE2 = B_SCALE * 1.4426950408889634
B_FWD_HEADS = 8
_B_ALL = slice(0, B_T)
_B_LO, _B_HI = slice(0, B_T // 2), slice(B_T // 2, B_T)
_B_DIAG = ((_B_LO, _B_LO), (_B_LO, _B_HI), (_B_HI, _B_HI))


def _tri_tables(n, by_query):
    pairs = [(i, j) for i in range(n) for j in range(i + 1)] if by_query else [(i, j) for j in range(n) for i in range(j, n)]
    return (np.asarray([p[0] for p in pairs], np.int32), np.asarray([p[1] for p in pairs], np.int32))


def _b_mask_t(s):
    kc = lax.broadcasted_iota(jnp.int32, s.shape, 0) // CHUNK
    qc = lax.broadcasted_iota(jnp.int32, s.shape, 1) // CHUNK
    return jnp.where(kc <= qc, s, NEG)


def attn_b_fwd(qf, kf, vp):
    t = qf.shape[0]
    n = t // B_T
    qtab, ktab = _tri_tables(n, by_query=True)
    hps = B_FWD_HEADS

    def body(qt_ref, kt_ref, q_ref, k_ref, v_ref, o_ref, lse_ref, m_s, l_s, acc_s):
        qb, kb = qt_ref[pl.program_id(1)], kt_ref[pl.program_id(1)]

        @pl.when(kb == 0)
        def _():
            m_s[...] = jnp.full_like(m_s, NEG)
            l_s[...] = jnp.zeros_like(l_s)
            acc_s[...] = jnp.zeros_like(acc_s)

        def block(kr, qr, masked):
            for h2 in range(hps):
                sl = slice(LANES * h2, LANES * (h2 + 1))
                v = v_ref[kr, LANES * (h2 // 2):LANES * (h2 // 2 + 1)]
                s = _dot_nt(k_ref[kr, sl], q_ref[qr, sl])
                if masked:
                    s = _b_mask_t(s)
                m_prev = m_s[h2, :, qr]
                m_new = jnp.maximum(m_prev, jnp.max(s, axis=0, keepdims=True))
                alpha = jnp.exp2(m_prev - m_new)
                pr = jnp.exp2(s - m_new)
                l_s[h2, :, qr] = alpha * l_s[h2, :, qr] + jnp.sum(pr, axis=0, keepdims=True)
                acc_s[h2, :, qr] = alpha * acc_s[h2, :, qr] + _dot_tn(v, pr.astype(BF16))
                m_s[h2, :, qr] = m_new

        @pl.when(kb < qb)
        def _():
            block(_B_ALL, _B_ALL, False)

        @pl.when(kb == qb)
        def _():
            for kr, qr in _B_DIAG:
                block(kr, qr, kr == qr)
            for h2 in range(hps):
                l = l_s[h2]
                rows = slice(HD * (h2 % 2), HD * (h2 % 2 + 1))
                o_ref[HD * h2:HD * (h2 + 1), :] = (acc_s[h2, rows, :] * (1.0 / l)).astype(BF16)
                lse_ref[h2 // 2, h2 % 2:h2 % 2 + 1, :] = m_s[h2] + jnp.log2(l)

    grid_spec = pltpu.PrefetchScalarGridSpec(
        num_scalar_prefetch=2, grid=(HEADS // hps, len(qtab)),
        in_specs=[pl.BlockSpec((B_T, LANES * hps), lambda p, s, qt, kt: (qt[s], p)),
                  pl.BlockSpec((B_T, LANES * hps), lambda p, s, qt, kt: (kt[s], p)),
                  pl.BlockSpec((B_T, HD * hps), lambda p, s, qt, kt: (kt[s], p))],
        out_specs=[pl.BlockSpec((HD * hps, B_T), lambda p, s, qt, kt: (p, qt[s])),
                   pl.BlockSpec((hps // 2, 2, B_T), lambda p, s, qt, kt: (p, 0, qt[s]))],
        scratch_shapes=[pltpu.VMEM((hps, 1, B_T), F32), pltpu.VMEM((hps, 1, B_T), F32),
                        pltpu.VMEM((hps, LANES, B_T), F32)])
    return pl.pallas_call(
        body, name="attn_b_fwd", grid_spec=grid_spec,
        out_shape=[jax.ShapeDtypeStruct((512, t), BF16), jax.ShapeDtypeStruct((4, 2, t), F32)],
        compiler_params=_params(("parallel", "arbitrary")),
    )(jnp.asarray(qtab), jnp.asarray(ktab), qf, kf, vp)


def attn_b_bwd(qf, kf, vp, do, do_t, o_t, lse, scatter=()):
    t = qf.shape[0]
    n = t // B_T
    qtab, ktab = _tri_tables(n, by_query=False)
    plan = ScatterPlan(scatter)
    m = plan.n
    last = len(qtab) - 1

    def body(*refs):
        qt_ref, kt_ref, q_ref, k_ref, v_ref, do_ref, dot_ref, ot_ref, lse_ref = refs[:9]
        ins, (dq_ref, dk_ref, dv_ref), outs = refs[9:9 + m], refs[9 + m:12 + m], refs[12 + m:12 + 2 * m]
        sems = refs[12 + 2 * m:]
        qb, kb = qt_ref[pl.program_id(1)], kt_ref[pl.program_id(1)]
        if m:
            pl.when((pl.program_id(0) == 0) & (pl.program_id(1) == 0))(lambda: plan.start(ins, outs, sems))

        @pl.when(pl.program_id(1) == 0)
        def _():
            dq_ref[...] = jnp.zeros_like(dq_ref)

        @pl.when(qb == kb)
        def _():
            dk_ref[...] = jnp.zeros_like(dk_ref)
            dv_ref[...] = jnp.zeros_like(dv_ref)

        def block(kr, qr, masked):
            nq = qr.stop - qr.start
            cols = pl.ds(pl.multiple_of(qb * B_T + qr.start, LANES), nq)
            v = v_ref[kr, :]
            dov = do_ref[qr, :]
            prod = dot_ref[:, qr].astype(F32) * ot_ref[:, qr].astype(F32)
            lo = _lane((nq, LANES)) < HD
            for h2 in range(2):
                sl = slice(LANES * h2, LANES * (h2 + 1))
                hm = lo if h2 == 0 else ~lo
                q = q_ref[qr, sl]
                k = k_ref[kr, sl]
                dom = jnp.where(hm, dov, jnp.zeros((), BF16))
                delta = jnp.sum(prod[HD * h2:HD * (h2 + 1), :], axis=0, keepdims=True)
                s = _dot_nt(k, q)
                if masked:
                    s = _b_mask_t(s)
                pr = jnp.exp2(s - lse_ref[0, h2:h2 + 1, qr])
                dp = _dot_nt(v, dom)
                ds = (pr * (dp - delta)).astype(BF16)
                dk_ref[kr, sl] += _dot(ds, q) * (B_SCALE / B_SCALE2)
                dv_ref[kr, :] += _dot(pr.astype(BF16), dom)
                dq_ref[sl, cols] += _dot_tn(k, ds) * B_SCALE

        @pl.when(qb > kb)
        def _():
            block(_B_ALL, _B_ALL, False)

        @pl.when(qb == kb)
        def _():
            for kr, qr in _B_DIAG:
                block(kr, qr, kr == qr)

        if m:
            pl.when((pl.program_id(0) == 3) & (pl.program_id(1) == last))(lambda: plan.finish(ins, outs, sems))

    qrow = lambda w: pl.BlockSpec((B_T, w), lambda p, s, qt, kt: (qt[s], p))
    qcol = pl.BlockSpec((LANES, B_T), lambda p, s, qt, kt: (p, qt[s]))
    krow = lambda w: pl.BlockSpec((B_T, w), lambda p, s, qt, kt: (kt[s], p))
    grid_spec = pltpu.PrefetchScalarGridSpec(
        num_scalar_prefetch=2, grid=(4, len(qtab)),
        in_specs=[qrow(256), krow(256), krow(LANES), qrow(LANES), qcol, qcol,
                  pl.BlockSpec((1, 2, B_T), lambda p, s, qt, kt: (p, 0, qt[s]))] + [ANY] * m,
        out_specs=[pl.BlockSpec((256, t), lambda p, s, qt, kt: (p, 0)), krow(256), krow(LANES)] + [ANY] * m,
        scratch_shapes=plan.scratch if m else [])
    return pl.pallas_call(
        body, name="attn_b_bwd", grid_spec=grid_spec,
        out_shape=[jax.ShapeDtypeStruct((1024, t), F32), jax.ShapeDtypeStruct((t, 1024), F32),
                   jax.ShapeDtypeStruct((t, 512), F32)] + plan.out_shape,
        compiler_params=_params(("arbitrary", "arbitrary")),
    )(jnp.asarray(qtab), jnp.asarray(ktab), qf, kf, vp, do, do_t, o_t, lse, *scatter)


_U_LEN = A_TQ + A_WIN - 1


def _band_mask():
    a = np.arange(A_TQ)[:, None] // CHUNK
    b = np.arange(A_WIN)[None, :] // CHUNK
    return (b >= a) & (b <= a + A_LEFT)


def bias_block(table):
    h = table.shape[0]
    n_lo = A_WIN - 1 - 2 * A_TQ - A_MAX_REL
    ext = jnp.concatenate([jnp.repeat(table[:, :1], n_lo, axis=1), table,
                           jnp.repeat(table[:, -1:], _U_LEN - n_lo - table.shape[1], axis=1)], axis=1)
    row = jnp.pad(ext[:, ::-1], ((0, 0), (0, 1)))[:, None, :]
    band = _band_mask()
    first = [band & (np.arange(A_WIN)[None, :] >= 2 * A_TQ - A_TQ * v) for v in range(3)]
    keep = jnp.asarray(np.stack(first), jnp.int32)

    def body(r_ref, k_ref, o_ref):
        rows = jnp.broadcast_to(r_ref[0], (A_TQ, _U_LEN + 1))
        skew = pltpu.roll(rows, _U_LEN + 1 - (A_TQ - 1), 1, stride=1, stride_axis=0)
        toep = skew[:, :A_WIN]
        for v in range(3):
            o_ref[v, 0] = jnp.where(k_ref[v] != 0, toep, NEG)

    return pl.pallas_call(
        body, name="bias_block", grid=(h,),
        in_specs=[pl.BlockSpec((1, 1, _U_LEN + 1), lambda i: (i, 0, 0)),
                  pl.BlockSpec((3, A_TQ, A_WIN), lambda i: (0, 0, 0))],
        out_specs=pl.BlockSpec((3, 1, A_TQ, A_WIN), lambda i: (0, i, 0, 0)),
        out_shape=jax.ShapeDtypeStruct((3, h, A_TQ, A_WIN), F32),
        compiler_params=_params(("parallel",)),
    )(row, keep)


def bias_block_grad(db):
    h = db.shape[0]
    n_lo = A_WIN - 1 - 2 * A_TQ - A_MAX_REL
    skew = jnp.pad(db, ((0, 0), (0, 0), (A_TQ - 1, 0)))
    flat = jnp.pad(skew.reshape(h, A_TQ * _U_LEN), ((0, 0), (0, A_TQ)))
    ext = jnp.sum(flat.reshape(h, A_TQ, _U_LEN + 1), axis=1)[:, :_U_LEN][:, ::-1]
    n_tab = 2 * A_MAX_REL + 1
    first = jnp.sum(ext[:, :n_lo + 1], axis=1, keepdims=True)
    last = jnp.sum(ext[:, n_lo + n_tab - 1:], axis=1, keepdims=True)
    return jnp.concatenate([first, ext[:, n_lo + 1:n_lo + n_tab - 1], last], axis=1)


def rope_tabs(t):
    inv = 1.0 / (10000.0 ** (jnp.arange(0, ROPE, 2, dtype=F32) / ROPE))
    ang = jnp.arange(t, dtype=F32)[:, None] * inv[None, :]
    cos, sin = jnp.cos(ang), jnp.sin(ang)
    z = lambda w: jnp.zeros((t, w), F32)
    ck = jnp.concatenate([cos, cos, z(96)], axis=1)
    s1k = jnp.concatenate([-sin, z(112)], axis=1)
    s2k = jnp.concatenate([z(16), sin, z(96)], axis=1)
    cq = jnp.concatenate([jnp.ones((t, HD), F32), cos, cos, z(32)], axis=1)
    s1q = jnp.concatenate([z(HD), -sin, z(48)], axis=1)
    s2q = jnp.concatenate([z(HD + 16), sin, z(32)], axis=1)
    return jnp.stack([cq, s1q, s2q, ck, s1k, s2k])


def _pad_lanes(v, width):
    return jnp.pad(v, ((0, 0), (0, width - v.shape[1])))


LATE = ("w_in", "b_w_uq", "b_w_ukv", "w_out", "ffn2_w_gate", "ffn2_w_up", "ffn2_w_down")
FFN2 = ("ffn2_w_gate", "ffn2_w_up", "ffn2_w_down")


def kernel_layout(gathered):
    w = {n: v for n, v in gathered.items() if n.startswith("ffn")}
    if "w_in" in gathered:
        w["w_in"] = jnp.pad(gathered["w_in"].reshape(IN_COLS, D_MODEL), ((0, PROJ_W - IN_COLS), (0, 0)))
        uq = gathered["b_w_uq"].reshape(HEADS, HD + ROPE, 256)
        w["b_w_uq"] = jnp.pad(uq, ((0, 0), (0, LANES - HD - ROPE), (0, 0))).reshape(HEADS * LANES, 256)
        w["b_w_ukv"] = _shards_to_cols(gathered["b_w_ukv"])
        w["w_out"] = gathered["w_out"].reshape(N_SHARD * gathered["w_out"].shape[1], D_MODEL)
    return w


def local_step(x, target, w, late=None):
    t = x.shape[0]
    gq = jnp.tile(w["a_q_norm"], (1, HEADS))
    gk = jnp.tile(w["a_k_norm"], (1, HEADS))
    gq128 = _pad_lanes(jnp.concatenate([w["b_q_nope_norm"], w["b_q_rope_norm"]], axis=1), LANES)
    gk128 = _pad_lanes(w["b_k_nope_norm"], LANES)
    gkr128 = _pad_lanes(w["b_k_rope_norm"], LANES)
    tabs = rope_tabs(t)
    bias = bias_block(w["a_rel_bias"])

    if late is None:
        x1, gate1, up1 = ffn_fwd(x, w["ffn1_norm"], w["ffn1_w_gate"], w["ffn1_w_up"], w["ffn1_w_down"], "ffn_fwd")
    else:
        own, shard = late
        x1, gate1, up1, *got = ffn_fwd(x, w["ffn1_norm"], w["ffn1_w_gate"], w["ffn1_w_up"], w["ffn1_w_down"],
                                       "ffn_fwd_gather", gather=own)
        w = dict(w, **kernel_layout({n: lax.dynamic_update_index_in_dim(g_, o_, shard, 0)
                                     for n, g_, o_ in zip(LATE, got, own)}))
    h, proj, qa, ka, va, cqn, ckvn = mix_fwd(x1, w["mix_norm"], w["w_in"], gq, gk,
                                             w["b_q_lat_norm"], w["b_kv_lat_norm"])
    qf, kf, vp = prep2_fwd(cqn, ckvn, proj, w["b_w_uq"], w["b_w_ukv"], gq128, gk128, gkr128, tabs)
    oa = attn_a_fwd(qa, ka, va, bias)
    ob_t, lse = attn_b_fwd(qf, kf, vp)
    g = {}
    dx3, gate2, up2, g["final_norm"], loss, x2 = ffn_fwd(
        x1, w["ffn2_norm"], w["ffn2_w_gate"], w["ffn2_w_up"], w["ffn2_w_down"], "ffn_fwd_loss",
        loss_head=(w["final_norm"], target), pre_proj=(oa, ob_t, w["w_out"]))
    g["ffn2_w_gate"], g["ffn2_w_up"], g["ffn2_w_down"], dhp = ffn_bwd(
        x2, dx3, w["ffn2_norm"], w["ffn2_w_gate"], w["ffn2_w_up"], w["ffn2_w_down"], gate2, up2, "ffn_bwd")
    dx2, g["ffn2_norm"], d_oa, d_ob, d_ob_t, g["w_out"] = out_proj_bwd(
        x2, w["ffn2_norm"], dhp, dx3, w["w_out"], oa, ob_t)
    early = [g[n] for n in FFN2] + [g["w_out"].reshape(N_SHARD, -1, D_MODEL)]
    dqf, dkf, dvp, *landed_early = attn_b_bwd(qf, kf, vp, d_ob, d_ob_t, ob_t, lse,
                                              scatter=() if late is None else early)
    dqa, dkp, dvpa, dbias = attn_a_bwd(qa, ka, va, bias, d_oa)
    dcq, dckv, dkr, dwuq, dwukv, dgq128, dgk128, dgkr128 = prep2_bwd(
        cqn, ckvn, proj, w["b_w_uq"], w["b_w_ukv"], dqf, dkf, dvp, gq128, gk128, gkr128, tabs)
    g["b_w_uq"], g["b_w_ukv"] = dwuq.astype(BF16), dwukv.astype(BF16)
    dx1, g["w_in"], g["mix_norm"], dgq, dgk, g["b_q_lat_norm"], g["b_kv_lat_norm"] = mix_bwd(
        proj, x1, h, dx2, w["w_in"], w["mix_norm"], dqa, dkp, dvpa, dcq, dckv, dkr, gq, gk,
        w["b_q_lat_norm"], w["b_kv_lat_norm"])
    mid = [g["w_in"][:IN_COLS].reshape(N_SHARD, IN_COLS // N_SHARD, D_MODEL),
           g["b_w_uq"].reshape(HEADS, LANES, 256)[:, :HD + ROPE].reshape(N_SHARD, -1, 256),
           _cols_to_shards(g["b_w_ukv"])]
    g["ffn1_w_gate"], g["ffn1_w_up"], g["ffn1_w_down"], dhp, *landed_late = ffn_bwd(
        x, dx1, w["ffn1_norm"], w["ffn1_w_gate"], w["ffn1_w_up"], w["ffn1_w_down"], gate1, up1,
        "ffn_bwd" if late is None else "ffn_bwd_scatter", scatter=() if late is None else mid,
        spread=None if late is None else late[1])
    grad_x, g["ffn1_norm"] = norm_bwd(x, w["ffn1_norm"], dhp, dx1, "ffn_norm_bwd")
    landed = dict(zip(FFN2 + ("w_out", "w_in", "b_w_uq", "b_w_ukv", "ffn1_w_gate", "ffn1_w_up", "ffn1_w_down"),
                      landed_early + landed_late))

    g["a_q_norm"] = jnp.sum(dgq.reshape(HEADS, HD), axis=0, keepdims=True)
    g["a_k_norm"] = jnp.sum(dgk.reshape(HEADS, HD), axis=0, keepdims=True)
    g["a_rel_bias"] = bias_block_grad(dbias)
    g["b_q_nope_norm"] = dgq128[:, :HD]
    g["b_q_rope_norm"] = dgq128[:, HD:HD + ROPE]
    g["b_k_nope_norm"] = dgk128[:, :HD]
    g["b_k_rope_norm"] = dgkr128[:, :ROPE]
    return loss, grad_x, g, landed


ANY = pl.BlockSpec(memory_space=pl.ANY)
N_DEV = 8


def _place():
    return lax.axis_index("x"), lax.axis_index("y"), lax.axis_index("c")


def _flip(v, bit):
    return 1 - v if bit else v


BF16_ROWS = 16


def _split_axis(shape):
    return 0 if (shape[0] // 2) % BF16_ROWS == 0 else 1


def _half_shape(shape):
    axis = _split_axis(shape)
    return tuple(s // 2 if a == axis else s for a, s in enumerate(shape))


def _half(shape, core):
    axis = _split_axis(shape)
    size = shape[axis] // 2
    return tuple(pl.ds(core * size, size) if a == axis else slice(None) for a in range(2))


class GatherPlan:
    def __init__(self, ws):
        self.shapes = [w.shape for w in ws]
        self.n = len(ws)
        self.out_shape = [jax.ShapeDtypeStruct((N_SHARD,) + w.shape, w.dtype) for w in ws]
        self.scratch = [pltpu.SemaphoreType.DMA((6 * self.n,)), pltpu.SemaphoreType.DMA((6 * self.n,))]

    def _copies(self, ins, outs, sems):
        x, y, c = _place()
        s_me = 2 * x + y
        sibling = (x, y, 1 - c)
        send_sems, recv_sems = sems

        def remote(k, src, dst, to):
            return pltpu.make_async_remote_copy(src_ref=src, dst_ref=dst, send_sem=send_sems.at[k],
                                                recv_sem=recv_sems.at[k], device_id=to, device_id_type=MESH)

        ici, fwd = [], []
        for a in range(self.n):
            mine, theirs = _half(self.shapes[a], c), _half(self.shapes[a], 1 - c)
            for j, (cx, cy) in enumerate([(1 - x, y), (x, 1 - y), (1 - x, 1 - y)]):
                got = outs[a].at[(2 * cx + cy,) + mine]
                ici.append((remote(6 * a + j, ins[a].at[mine], outs[a].at[(s_me,) + mine], (cx, cy, c)),
                            remote(6 * a + j, got, got, (cx, cy, c))))
                passed = outs[a].at[(2 * cx + cy,) + theirs]
                fwd.append((remote(6 * a + 3 + j, got, got, sibling), remote(6 * a + 3 + j, passed, passed, sibling)))
        return ici, fwd

    def start(self, ins, outs, sems):
        for send, _ in self._copies(ins, outs, sems)[0]:
            send.start()

    def forward(self, ins, outs, sems):
        ici, fwd = self._copies(ins, outs, sems)
        for (_, arrival), (send, _) in zip(ici, fwd):
            arrival.wait_recv()
            send.start()

    def finish(self, ins, outs, sems):
        ici, fwd = self._copies(ins, outs, sems)
        for _, arrival in fwd:
            arrival.wait_recv()
        for send, _ in ici + fwd:
            send.wait_send()


def allgather_shards(ws):
    plan = GatherPlan(ws)
    n = plan.n

    def body(*refs):
        ins, outs, sems = refs[:n], refs[n:2 * n], refs[2 * n:]
        plan.start(ins, outs, sems)
        plan.forward(ins, outs, sems)
        plan.finish(ins, outs, sems)

    return pl.pallas_call(
        body, name="allgather_shards", in_specs=[ANY] * n, out_specs=[ANY] * n,
        out_shape=plan.out_shape, scratch_shapes=plan.scratch,
    )(*ws)


class ScatterPlan:
    def __init__(self, gs):
        self.shapes = [g.shape[1:] for g in gs]
        self.n = len(gs)
        self.out_shape = [jax.ShapeDtypeStruct((N_DEV,) + _half_shape(g.shape[1:]), g.dtype) for g in gs]
        self.scratch = [pltpu.SemaphoreType.DMA((7 * self.n,)), pltpu.SemaphoreType.DMA((7 * self.n,)),
                        pltpu.SemaphoreType.DMA((self.n,))]

    def _copies(self, ins, outs, sems):
        x, y, c = _place()
        me = 4 * x + 2 * y + c
        send_sems, recv_sems, local_sems = sems
        local, sends, arrivals = [], [], []
        for a in range(self.n):
            piece = lambda px, py, pc, a=a: ins[a].at[(2 * px + py,) + _half(self.shapes[a], pc)]
            local.append(pltpu.make_async_copy(piece(x, y, c), outs[a].at[me], local_sems.at[a]))
            for k in range(1, N_DEV):
                px, py, pc = _flip(x, k & 4), _flip(y, k & 2), _flip(c, k & 1)
                sem = dict(send_sem=send_sems.at[7 * a + k - 1], recv_sem=recv_sems.at[7 * a + k - 1],
                           device_id=(px, py, pc), device_id_type=MESH)
                sends.append(pltpu.make_async_remote_copy(
                    src_ref=piece(px, py, pc), dst_ref=outs[a].at[me], **sem))
                slot = outs[a].at[4 * px + 2 * py + pc]
                arrivals.append(pltpu.make_async_remote_copy(src_ref=slot, dst_ref=slot, **sem))
        return local, sends, arrivals

    def start(self, ins, outs, sems):
        local, sends, _ = self._copies(ins, outs, sems)
        for cp in local + sends:
            cp.start()

    def finish(self, ins, outs, sems):
        local, sends, arrivals = self._copies(ins, outs, sems)
        for cp in arrivals:
            cp.wait_recv()
        for cp in sends:
            cp.wait_send()
        for cp in local:
            cp.wait()


def _row_tile(rows, row_bytes, budget, multiple):
    fits = [r for r in range(multiple, rows + 1, multiple) if rows % r == 0 and r * row_bytes <= budget]
    return max(fits) if fits else rows


SMALL_KERNEL_VMEM = 32 * 1024 * 1024


def sum_slots(lands, name):
    k = len(lands)
    _, rows, cols = lands[0].shape
    tr = _row_tile(rows, k * 2 * (N_DEV * cols * 2 + cols * 4), SMALL_KERNEL_VMEM, BF16_ROWS)

    def body(*refs):
        for l_ref, o_ref in zip(refs[:k], refs[k:]):
            acc = l_ref[0].astype(F32)
            for s in range(1, N_DEV):
                acc = acc + l_ref[s].astype(F32)
            o_ref[...] = acc

    return pl.pallas_call(
        body, name=name, grid=(rows // tr,),
        in_specs=[pl.BlockSpec((N_DEV, tr, cols), lambda i: (0, i, 0))] * k,
        out_specs=[pl.BlockSpec((tr, cols), lambda i: (i, 0))] * k,
        out_shape=[jax.ShapeDtypeStruct((rows, cols), F32)] * k,
        compiler_params=_params(("parallel",)),
    )(*lands)


def join_halves(hs, shapes, vec):
    n = len(hs)

    def body(*refs):
        ins, v_ref, outs, o_ref = refs[:n], refs[n], refs[n + 1:2 * n + 1], refs[2 * n + 1]
        land_ref, send_sems, recv_sems = refs[2 * n + 2:]
        x, y, c = _place()
        me = 4 * x + 2 * y + c
        land_ref[me] = v_ref[...]
        peers = [(_flip(x, k & 4), _flip(y, k & 2), _flip(c, k & 1)) for k in range(1, N_DEV)]
        sends = []
        for k, peer in enumerate(peers):
            sends.append(pltpu.make_async_remote_copy(
                src_ref=v_ref, dst_ref=land_ref.at[me], send_sem=send_sems.at[n + k], recv_sem=recv_sems.at[n + k],
                device_id=peer, device_id_type=MESH))
            sends[-1].start()
        for a in range(n):
            mine = outs[a].at[_half(shapes[a], c)]
            sends.append(pltpu.make_async_remote_copy(
                src_ref=ins[a], dst_ref=mine, send_sem=send_sems.at[a], recv_sem=recv_sems.at[a],
                device_id=(x, y, 1 - c), device_id_type=MESH))
            sends[-1].start()
        for a in range(n):
            theirs = outs[a].at[_half(shapes[a], 1 - c)]
            pltpu.make_async_remote_copy(
                src_ref=theirs, dst_ref=theirs, send_sem=send_sems.at[a], recv_sem=recv_sems.at[a],
                device_id=(x, y, 1 - c), device_id_type=MESH).wait_recv()
        for k, (px, py, pc) in enumerate(peers):
            slot = land_ref.at[4 * px + 2 * py + pc]
            pltpu.make_async_remote_copy(
                src_ref=slot, dst_ref=slot, send_sem=send_sems.at[n + k], recv_sem=recv_sems.at[n + k],
                device_id=(px, py, pc), device_id_type=MESH).wait_recv()
        for cp in sends:
            cp.wait_send()
        acc = land_ref[0]
        for s in range(1, N_DEV):
            acc = acc + land_ref[s]
        o_ref[...] = acc

    vm = pl.BlockSpec(memory_space=pltpu.VMEM)
    m = n + N_DEV - 1
    out = pl.pallas_call(
        body, name="join_halves",
        in_specs=[ANY] * n + [vm], out_specs=[ANY] * n + [vm],
        out_shape=[jax.ShapeDtypeStruct(tuple(s), h.dtype) for s, h in zip(shapes, hs)]
        + [jax.ShapeDtypeStruct(vec.shape, F32)],
        scratch_shapes=[pltpu.VMEM((N_DEV,) + vec.shape, F32), pltpu.SemaphoreType.DMA((m,)),
                        pltpu.SemaphoreType.DMA((m,))],
    )(*hs, vec)
    return out[:n], out[n]


def adamw(ws, gs, ms, vs, name):
    k = len(ws)
    rows, cols = ws[0].shape
    tr = _row_tile(rows, k * 2 * 7 * cols * 4, SMALL_KERNEL_VMEM, 8)
    c1 = 1.0 - ADAM_B1 ** ADAM_STEP
    c2 = 1.0 - ADAM_B2 ** ADAM_STEP

    def body(*refs):
        for a in range(k):
            w_ref, g_ref, m_ref, v_ref = (refs[s * k + a] for s in range(4))
            d_ref, nm_ref, nv_ref = (refs[(4 + s) * k + a] for s in range(3))
            gv = g_ref[...]
            nm = ADAM_B1 * m_ref[...] + (1.0 - ADAM_B1) * gv
            nv = ADAM_B2 * v_ref[...] + (1.0 - ADAM_B2) * (gv * gv)
            nm_ref[...] = nm
            nv_ref[...] = nv
            d_ref[...] = -ADAM_LR * ((nm / c1) / (jnp.sqrt(nv / c2) + ADAM_EPS) + ADAM_WD * w_ref[...])

    blk = pl.BlockSpec((tr, cols), lambda i: (i, 0))
    out = pl.pallas_call(
        body, name=name, grid=(rows // tr,),
        in_specs=[blk] * (4 * k), out_specs=[blk] * (3 * k),
        out_shape=[jax.ShapeDtypeStruct((rows, cols), F32)] * (3 * k),
        compiler_params=_params(("parallel",)),
    )(*ws, *gs, *ms, *vs)
    return [(out[a], out[k + a], out[2 * k + a]) for a in range(k)]


BIG = ("ffn1_w_gate", "ffn1_w_up", "ffn1_w_down", "w_in", "b_w_uq", "b_w_ukv", "w_out",
       "ffn2_w_gate", "ffn2_w_up", "ffn2_w_down")
SMALL = ("ffn1_norm", "mix_norm", "a_q_norm", "a_k_norm", "a_rel_bias", "b_q_lat_norm", "b_kv_lat_norm",
         "b_q_nope_norm", "b_q_rope_norm", "b_k_nope_norm", "b_k_rope_norm", "ffn2_norm", "final_norm")
WEIGHTS = ("ffn1_norm", "ffn1_w_gate", "ffn1_w_up", "ffn1_w_down", "mix_norm", "w_in", "a_q_norm", "a_k_norm",
           "a_rel_bias", "b_q_lat_norm", "b_w_uq", "b_kv_lat_norm", "b_w_ukv", "b_q_nope_norm", "b_q_rope_norm",
           "b_k_nope_norm", "b_k_rope_norm", "w_out", "ffn2_norm", "ffn2_w_gate", "ffn2_w_up", "ffn2_w_down",
           "final_norm")
TRANSPOSED = ("ffn1_w_gate", "ffn1_w_up", "ffn2_w_gate", "ffn2_w_up", "w_in", "b_w_uq")
PACK_SHAPE = (8, 1024)


def _pack_small(d, last=None):
    flat = [d[n].reshape(-1) for n in SMALL]
    used = sum(f.shape[0] for f in flat)
    total = PACK_SHAPE[0] * PACK_SHAPE[1]
    tail = jnp.zeros((total - used - 1,), F32)
    end = jnp.zeros((1,), F32) if last is None else last.reshape(1)
    return jnp.concatenate(flat + [tail, end]).reshape(PACK_SHAPE)


def _unpack_small(p, like):
    flat = p.reshape(-1)
    out, off = {}, 0
    for n in SMALL:
        size = like[n].size
        out[n] = flat[off:off + size].reshape(like[n].shape)
        off += size
    return out, flat[-1]


def _cols_to_shards(g):
    rows, cols = g.shape
    return g.reshape(rows, N_SHARD, cols // N_SHARD).transpose(1, 0, 2)


def _shards_to_cols(g):
    return g.transpose(1, 0, 2).reshape(g.shape[1], -1)


def kernel(x, ffn1_norm, ffn1_w_gate, ffn1_w_up, ffn1_w_down, mix_norm, w_in, a_q_norm, a_k_norm, a_rel_bias, b_q_lat_norm, b_w_uq, b_kv_lat_norm, b_w_ukv, b_q_nope_norm, b_q_rope_norm, b_k_nope_norm, b_k_rope_norm, w_out, ffn2_norm, ffn2_w_gate, ffn2_w_up, ffn2_w_down, final_norm, loss_target, m_ffn1_norm, m_ffn1_w_gate, m_ffn1_w_up, m_ffn1_w_down, m_mix_norm, m_w_in, m_a_q_norm, m_a_k_norm, m_a_rel_bias, m_b_q_lat_norm, m_b_w_uq, m_b_kv_lat_norm, m_b_w_ukv, m_b_q_nope_norm, m_b_q_rope_norm, m_b_k_nope_norm, m_b_k_rope_norm, m_w_out, m_ffn2_norm, m_ffn2_w_gate, m_ffn2_w_up, m_ffn2_w_down, m_final_norm, v_ffn1_norm, v_ffn1_w_gate, v_ffn1_w_up, v_ffn1_w_down, v_mix_norm, v_w_in, v_a_q_norm, v_a_k_norm, v_a_rel_bias, v_b_q_lat_norm, v_b_w_uq, v_b_kv_lat_norm, v_b_w_ukv, v_b_q_nope_norm, v_b_q_rope_norm, v_b_k_nope_norm, v_b_k_rope_norm, v_w_out, v_ffn2_norm, v_ffn2_w_gate, v_ffn2_w_up, v_ffn2_w_down, v_final_norm):
    args = locals()
    view = lambda a, n: a[0].T if n in TRANSPOSED else a[0]
    wts = {n: view(args[n], n) for n in WEIGHTS}
    mom = {n: view(args["m_" + n], n) for n in WEIGHTS}
    var = {n: view(args["v_" + n], n) for n in WEIGHTS}

    shard = 2 * lax.axis_index("x") + lax.axis_index("y")
    core = lax.axis_index("c")
    first = [n for n in BIG if n not in LATE]
    own = [wts[n].astype(BF16) for n in first]
    w = {n: wts[n] if n == "a_rel_bias" else wts[n][None] for n in SMALL}
    w.update(kernel_layout({n: lax.dynamic_update_index_in_dim(got, mine, shard, 0)
                            for n, got, mine in zip(first, allgather_shards(own), own)}))

    loss, grad_x, g, landed = local_step(x[0], loss_target[0], w,
                                         late=([wts[n].astype(BF16) for n in LATE], shard))

    me = 2 * shard + core
    for n in first:
        piece = lax.dynamic_slice(g[n], (shard, core * (FS // 2), 0), (1, FS // 2, D_MODEL))
        landed[n] = lax.dynamic_update_slice(landed[n], piece, (me, 0, 0))
    groups = {}
    for n in BIG:
        groups.setdefault(wts[n].shape, []).append(n)
    half = {}
    for names in groups.values():
        half.update(zip(names, sum_slots([landed[n] for n in names], "sum_slots")))
    halves = [half[n] for n in BIG]
    shapes = [wts[n].shape for n in BIG]
    axes = [_split_axis(s) for s in shapes]
    joined, small = join_halves(halves, shapes, _pack_small(g, loss[0, 0]))
    grads = dict(zip(BIG, (lax.dynamic_update_slice_in_dim(got, mine, core * mine.shape[ax], ax)
                           for got, mine, ax in zip(joined, halves, axes))))
    small_sum, loss_sum = _unpack_small(small, wts)
    grads.update(small_sum)

    delta, new_m, new_v = {}, {}, {}
    for names in groups.values():
        stepped = adamw(*([d[n] for n in names] for d in (wts, grads, mom, var)), "adamw")
        for n, (d_new, m_new, v_new) in zip(names, stepped):
            delta[n], new_m[n], new_v[n] = d_new, m_new, v_new
    (packed,) = adamw([_pack_small(wts)], [_pack_small(grads)], [_pack_small(mom)], [_pack_small(var)], "adamw_small")
    for dst, p in zip((delta, new_m, new_v), packed):
        dst.update(_unpack_small(p, wts)[0])

    lead = lambda d: [(d[n].T if n in TRANSPOSED else d[n])[None] for n in WEIGHTS]
    return (loss_sum, grad_x[None], *lead(grads), *lead(delta), *lead(new_m), *lead(new_v))
```
